```python
import math
import jax, jax.numpy as jnp
from jax import lax
import numpy as np

D_MODEL = 1024
BATCH = 8
SEQ = 2048
DEPTH = 4

MEM_LEN = 256
EPS = 1e-6
SSM_HEADS = 16
SSM_HEAD_DIM = 64
D_SSM = SSM_HEADS * SSM_HEAD_DIM
SSM_GROUPS = 4
SSM_STATE = 128
SSM_CONV = 4
SSM_CHUNK = 128
CONV_CH = D_SSM + 2 * SSM_GROUPS * SSM_STATE
MLA_HEADS = 16
QK_NOPE = 64
QK_ROPE = 32
V_DIM = 64
Q_LORA = 384
KV_LORA = 256
D_ATTN = MLA_HEADS * V_DIM
ROPE_THETA = 10000.0
Q_BLOCK = 128
D_MIX = D_SSM + D_ATTN
_O1 = D_SSM
_O2 = _O1 + CONV_CH
_O3 = _O2 + SSM_HEADS
_O4 = _O3 + Q_LORA
_O5 = _O4 + KV_LORA
D_IN = _O5 + QK_ROPE
IN_SPLITS = (_O1, _O2, _O3, _O4, _O5)
MEM_HEADS = 4
MEM_HEAD_DIM = D_MODEL // MEM_HEADS
D_FF = 2816
FFN_CONV = 3

kernel_name = "hymba_ssd_mla_memxattn_convffn"


def rmsnorm(x, g):
    xf = x.astype(jnp.float32)
    var = jnp.mean(xf * xf, axis=-1, keepdims=True)
    return (xf * lax.rsqrt(var + EPS) * g.astype(jnp.float32)).astype(x.dtype)


def causal_dwconv(x, w, b):
    k = w.shape[0]
    s = x.shape[1]
    xp = jnp.pad(x, ((0, 0), (k - 1, 0), (0, 0)))
    y = xp[:, 0:s] * w[0]
    for j in range(1, k):
        y = y + xp[:, j:j + s] * w[j]
    return y + b


def rope_tables(positions):
    inv_freq = 1.0 / (ROPE_THETA ** (jnp.arange(0, QK_ROPE, 2, dtype=jnp.float32) / QK_ROPE))
    ang = positions.astype(jnp.float32)[..., None] * inv_freq
    return jnp.cos(ang), jnp.sin(ang)


def apply_rope(t, cos, sin):
    half = t.shape[-1] // 2
    t1, t2 = t[..., :half], t[..., half:]
    out = jnp.concatenate([t1 * cos - t2 * sin, t2 * cos + t1 * sin], axis=-1)
    return out.astype(t.dtype)


def segsum_exp(a):
    t = a.shape[-1]
    cs = jnp.cumsum(a, axis=-1)
    diff = cs[..., :, None] - cs[..., None, :]
    mask = jnp.tril(jnp.ones((t, t), dtype=bool))
    return jnp.exp(jnp.where(mask, diff, -jnp.inf))


def ssd_scan(x, dt, a, bm, cm):
    out_dtype = x.dtype
    x = x.astype(jnp.float32)
    bm = bm.astype(jnp.float32)
    cm = cm.astype(jnp.float32)
    b, l, h, p = x.shape
    g, n = bm.shape[-2:]
    r = h // g
    c = l // SSM_CHUNK
    xd = (x * dt[..., None]).reshape(b, c, SSM_CHUNK, g, r, p)
    ad = jnp.moveaxis((dt * a).reshape(b, c, SSM_CHUNK, g, r), 2, -1)
    a_cs = jnp.cumsum(ad, axis=-1)
    bc = bm.reshape(b, c, SSM_CHUNK, g, n)
    cc = cm.reshape(b, c, SSM_CHUNK, g, n)
    lmat = segsum_exp(ad)
    cb = jnp.einsum('bclgn,bcsgn->bcgls', cc, bc)
    y_diag = jnp.einsum('bcgls,bcgrls,bcsgrp->bclgrp', cb, lmat, xd)
    decay_states = jnp.exp(a_cs[..., -1:] - a_cs)
    states = jnp.einsum('bclgn,bcgrl,bclgrp->bcgrpn', bc, decay_states, xd)
    chunk_decay = jnp.exp(a_cs[..., -1])

    def step(prev, inp):
        st, dec = inp
        return prev * dec[..., None, None] + st, prev

    init = jnp.zeros((b, g, r, p, n), jnp.float32)
    _, prev_states = lax.scan(step, init, (jnp.moveaxis(states, 1, 0), jnp.moveaxis(chunk_decay, 1, 0)))
    prev_states = jnp.moveaxis(prev_states, 0, 1)
    y_off = jnp.einsum('bclgn,bcgrpn,bcgrl->bclgrp', cc, prev_states, jnp.exp(a_cs))
    return (y_diag + y_off).reshape(b, l, h, p).astype(out_dtype)


def mla_attention(c_q, c_kv, k_rope, q_norm, w_uq, kv_norm, w_ukv, cos, sin):
    b, s, _ = c_q.shape
    q = (rmsnorm(c_q, q_norm) @ w_uq).reshape(b, s, MLA_HEADS, QK_NOPE + QK_ROPE)
    q_nope = q[..., :QK_NOPE]
    q_pe = apply_rope(q[..., QK_NOPE:], cos[:, :, None, :], sin[:, :, None, :])
    kv = (rmsnorm(c_kv, kv_norm) @ w_ukv).reshape(b, s, MLA_HEADS, QK_NOPE + V_DIM)
    k_nope, v = kv[..., :QK_NOPE], kv[..., QK_NOPE:]
    k_pe = apply_rope(k_rope, cos, sin)
    scale = (QK_NOPE + QK_ROPE) ** -0.5
    outs = []
    for i in range(s // Q_BLOCK):
        q0 = i * Q_BLOCK
        kend = q0 + Q_BLOCK
        sc = (jnp.einsum('bqhd,bkhd->bhqk', q_nope[:, q0:kend], k_nope[:, :kend])
              + jnp.einsum('bqhr,bkr->bhqk', q_pe[:, q0:kend], k_pe[:, :kend]))
        sc = sc.astype(jnp.float32) * scale
        mask = (q0 + jnp.arange(Q_BLOCK))[:, None] >= jnp.arange(kend)[None, :]
        pr = jax.nn.softmax(jnp.where(mask, sc, -jnp.inf), axis=-1).astype(v.dtype)
        outs.append(jnp.einsum('bhqk,bkhd->bqhd', pr, v[:, :kend]))
    return jnp.concatenate(outs, axis=1).reshape(b, s, D_ATTN)


def memory_attention(h, m, w_q, w_k, w_v, w_o):
    b, s, _ = h.shape
    ml = m.shape[1]
    q = (h @ w_q).reshape(b, s, MEM_HEADS, MEM_HEAD_DIM)
    k = (m @ w_k).reshape(b, ml, MEM_HEADS, MEM_HEAD_DIM)
    v = (m @ w_v).reshape(b, ml, MEM_HEADS, MEM_HEAD_DIM)
    sc = jnp.einsum('bqhd,bkhd->bhqk', q, k).astype(jnp.float32) * (MEM_HEAD_DIM ** -0.5)
    pr = jax.nn.softmax(sc, axis=-1).astype(v.dtype)
    o = jnp.einsum('bhqk,bkhd->bqhd', pr, v).reshape(b, s, D_MODEL)
    return o @ w_o


def conv_glu_ffn(h, w_up, conv_w, conv_b, w_down):
    u = causal_dwconv(h @ w_up, conv_w, conv_b)
    gate, val = u[..., :D_FF], u[..., D_FF:]
    return (jax.nn.silu(gate) * val) @ w_down


def _fwd_setup_inputs(seed: int = 0) -> dict:
    key = jax.random.key(seed)
    ks = iter(jax.random.split(key, 64))

    def nrm(shape, scale):
        return jax.random.normal(next(ks), shape, jnp.float32) * scale

    def gain(shape):
        return 1.0 + nrm(shape, 0.02)

    L = DEPTH
    dt0 = jnp.exp(jax.random.uniform(next(ks), (L, SSM_HEADS), jnp.float32,
                                     math.log(1e-3), math.log(1e-1)))
    dt_bias = dt0 + jnp.log(-jnp.expm1(-dt0))
    a_log = jnp.log(jax.random.uniform(next(ks), (L, SSM_HEADS), jnp.float32, 1.0, 16.0))
    offsets = jax.random.randint(next(ks), (BATCH, 1), 0, 1024, dtype=jnp.int32)
    positions = offsets + jnp.arange(SEQ, dtype=jnp.int32)[None, :]
    return {
        "x": nrm((BATCH, SEQ, D_MODEL), 1.0),
        "mem": nrm((BATCH, MEM_LEN, D_MODEL), 1.0),
        "positions": positions,
        "norm_mix": gain((L, D_MODEL)),
        "w_in": nrm((L, D_MODEL, D_IN), D_MODEL ** -0.5),
        "ssm_conv_w": nrm((L, SSM_CONV, CONV_CH), SSM_CONV ** -0.5),
        "ssm_conv_b": nrm((L, CONV_CH), 0.02),
        "dt_bias": dt_bias,
        "a_log": a_log,
        "d_skip": 1.0 + nrm((L, SSM_HEADS), 0.1),
        "ssm_norm": gain((L, D_SSM)),
        "q_norm": gain((L, Q_LORA)),
        "w_uq": nrm((L, Q_LORA, MLA_HEADS * (QK_NOPE + QK_ROPE)), Q_LORA ** -0.5),
        "kv_norm": gain((L, KV_LORA)),
        "w_ukv": nrm((L, KV_LORA, MLA_HEADS * (QK_NOPE + V_DIM)), KV_LORA ** -0.5),
        "attn_out_norm": gain((L, D_ATTN)),
        "w_out": nrm((L, D_MIX, D_MODEL), D_MIX ** -0.5),
        "norm_mem_q": gain((L, D_MODEL)),
        "norm_mem_kv": gain((L, D_MODEL)),
        "w_mq": nrm((L, D_MODEL, D_MODEL), D_MODEL ** -0.5),
        "w_mk": nrm((L, D_MODEL, D_MODEL), D_MODEL ** -0.5),
        "w_mv": nrm((L, D_MODEL, D_MODEL), D_MODEL ** -0.5),
        "w_mo": nrm((L, D_MODEL, D_MODEL), D_MODEL ** -0.5),
        "norm_ffn": gain((L, D_MODEL)),
        "w_up": nrm((L, D_MODEL, 2 * D_FF), D_MODEL ** -0.5),
        "ffn_conv_w": nrm((L, FFN_CONV, 2 * D_FF), FFN_CONV ** -0.5),
        "ffn_conv_b": nrm((L, 2 * D_FF), 0.02),
        "w_down": nrm((L, D_FF, D_MODEL), D_FF ** -0.5),
        "final_norm": gain((D_MODEL,)),
    }


def _fwd_reference(x, mem, positions, norm_mix, w_in, ssm_conv_w, ssm_conv_b, dt_bias, a_log,
              d_skip, ssm_norm, q_norm, w_uq, kv_norm, w_ukv, attn_out_norm, w_out,
              norm_mem_q, norm_mem_kv, w_mq, w_mk, w_mv, w_mo, norm_ffn, w_up,
              ffn_conv_w, ffn_conv_b, w_down, final_norm):
    b, s, _ = x.shape
    cos, sin = rope_tables(positions)
    for i in range(DEPTH):
        h = rmsnorm(x, norm_mix[i])
        proj = h @ w_in[i]
        z, xbc, dt_raw, c_q, c_kv, k_rope = jnp.split(proj, IN_SPLITS, axis=-1)
        xbc = jax.nn.silu(causal_dwconv(xbc, ssm_conv_w[i], ssm_conv_b[i]))
        xs = xbc[..., :D_SSM].reshape(b, s, SSM_HEADS, SSM_HEAD_DIM)
        bm = xbc[..., D_SSM:D_SSM + SSM_GROUPS * SSM_STATE].reshape(b, s, SSM_GROUPS, SSM_STATE)
        cm = xbc[..., D_SSM + SSM_GROUPS * SSM_STATE:].reshape(b, s, SSM_GROUPS, SSM_STATE)
        dt = jax.nn.softplus(dt_raw.astype(jnp.float32) + dt_bias[i].astype(jnp.float32))
        a = -jnp.exp(a_log[i].astype(jnp.float32))
        y = ssd_scan(xs, dt, a, bm, cm) + xs * d_skip[i][:, None]
        y_ssm = rmsnorm(y.reshape(b, s, D_SSM) * jax.nn.silu(z), ssm_norm[i])
        y_att = mla_attention(c_q, c_kv, k_rope, q_norm[i], w_uq[i], kv_norm[i], w_ukv[i], cos, sin)
        y_att = rmsnorm(y_att, attn_out_norm[i])
        x = x + jnp.concatenate([y_ssm, y_att], axis=-1) @ w_out[i]
        x = x + memory_attention(rmsnorm(x, norm_mem_q[i]), rmsnorm(mem, norm_mem_kv[i]),
                                 w_mq[i], w_mk[i], w_mv[i], w_mo[i])
        x = x + conv_glu_ffn(rmsnorm(x, norm_ffn[i]), w_up[i], ffn_conv_w[i], ffn_conv_b[i], w_down[i])
    return rmsnorm(x, final_norm)


import jax as _jax
import jax.numpy as _jnp

TWIN_FORMAT = 'train_step'
FWD_PARAMS = ['x', 'mem', 'positions', 'norm_mix', 'w_in', 'ssm_conv_w', 'ssm_conv_b', 'dt_bias', 'a_log', 'd_skip', 'ssm_norm', 'q_norm', 'w_uq', 'kv_norm', 'w_ukv', 'attn_out_norm', 'w_out', 'norm_mem_q', 'norm_mem_kv', 'w_mq', 'w_mk', 'w_mv', 'w_mo', 'norm_ffn', 'w_up', 'ffn_conv_w', 'ffn_conv_b', 'w_down', 'final_norm']
TWIN_WEIGHTS = ['norm_mix', 'w_in', 'ssm_conv_w', 'ssm_conv_b', 'dt_bias', 'a_log', 'd_skip', 'ssm_norm', 'q_norm', 'w_uq', 'kv_norm', 'w_ukv', 'attn_out_norm', 'w_out', 'norm_mem_q', 'norm_mem_kv', 'w_mq', 'w_mk', 'w_mv', 'w_mo', 'norm_ffn', 'w_up', 'ffn_conv_w', 'ffn_conv_b', 'w_down', 'final_norm']
TWIN_DIFF_INPUT = 'x'
TWIN_INPUTS = ['x', 'mem', 'positions', 'norm_mix', 'w_in', 'ssm_conv_w', 'ssm_conv_b', 'dt_bias', 'a_log', 'd_skip', 'ssm_norm', 'q_norm', 'w_uq', 'kv_norm', 'w_ukv', 'attn_out_norm', 'w_out', 'norm_mem_q', 'norm_mem_kv', 'w_mq', 'w_mk', 'w_mv', 'w_mo', 'norm_ffn', 'w_up', 'ffn_conv_w', 'ffn_conv_b', 'w_down', 'final_norm', 'loss_target', 'm_norm_mix', 'm_w_in', 'm_ssm_conv_w', 'm_ssm_conv_b', 'm_dt_bias', 'm_a_log', 'm_d_skip', 'm_ssm_norm', 'm_q_norm', 'm_w_uq', 'm_kv_norm', 'm_w_ukv', 'm_attn_out_norm', 'm_w_out', 'm_norm_mem_q', 'm_norm_mem_kv', 'm_w_mq', 'm_w_mk', 'm_w_mv', 'm_w_mo', 'm_norm_ffn', 'm_w_up', 'm_ffn_conv_w', 'm_ffn_conv_b', 'm_w_down', 'm_final_norm', 'v_norm_mix', 'v_w_in', 'v_ssm_conv_w', 'v_ssm_conv_b', 'v_dt_bias', 'v_a_log', 'v_d_skip', 'v_ssm_norm', 'v_q_norm', 'v_w_uq', 'v_kv_norm', 'v_w_ukv', 'v_attn_out_norm', 'v_w_out', 'v_norm_mem_q', 'v_norm_mem_kv', 'v_w_mq', 'v_w_mk', 'v_w_mv', 'v_w_mo', 'v_norm_ffn', 'v_w_up', 'v_ffn_conv_w', 'v_ffn_conv_b', 'v_w_down', 'v_final_norm']
TWIN_OUTPUTS = ['loss', 'grad_x', 'grad_norm_mix', 'grad_w_in', 'grad_ssm_conv_w', 'grad_ssm_conv_b', 'grad_dt_bias', 'grad_a_log', 'grad_d_skip', 'grad_ssm_norm', 'grad_q_norm', 'grad_w_uq', 'grad_kv_norm', 'grad_w_ukv', 'grad_attn_out_norm', 'grad_w_out', 'grad_norm_mem_q', 'grad_norm_mem_kv', 'grad_w_mq', 'grad_w_mk', 'grad_w_mv', 'grad_w_mo', 'grad_norm_ffn', 'grad_w_up', 'grad_ffn_conv_w', 'grad_ffn_conv_b', 'grad_w_down', 'grad_final_norm', 'delta_norm_mix', 'delta_w_in', 'delta_ssm_conv_w', 'delta_ssm_conv_b', 'delta_dt_bias', 'delta_a_log', 'delta_d_skip', 'delta_ssm_norm', 'delta_q_norm', 'delta_w_uq', 'delta_kv_norm', 'delta_w_ukv', 'delta_attn_out_norm', 'delta_w_out', 'delta_norm_mem_q', 'delta_norm_mem_kv', 'delta_w_mq', 'delta_w_mk', 'delta_w_mv', 'delta_w_mo', 'delta_norm_ffn', 'delta_w_up', 'delta_ffn_conv_w', 'delta_ffn_conv_b', 'delta_w_down', 'delta_final_norm', 'new_m_norm_mix', 'new_m_w_in', 'new_m_ssm_conv_w', 'new_m_ssm_conv_b', 'new_m_dt_bias', 'new_m_a_log', 'new_m_d_skip', 'new_m_ssm_norm', 'new_m_q_norm', 'new_m_w_uq', 'new_m_kv_norm', 'new_m_w_ukv', 'new_m_attn_out_norm', 'new_m_w_out', 'new_m_norm_mem_q', 'new_m_norm_mem_kv', 'new_m_w_mq', 'new_m_w_mk', 'new_m_w_mv', 'new_m_w_mo', 'new_m_norm_ffn', 'new_m_w_up', 'new_m_ffn_conv_w', 'new_m_ffn_conv_b', 'new_m_w_down', 'new_m_final_norm', 'new_v_norm_mix', 'new_v_w_in', 'new_v_ssm_conv_w', 'new_v_ssm_conv_b', 'new_v_dt_bias', 'new_v_a_log', 'new_v_d_skip', 'new_v_ssm_norm', 'new_v_q_norm', 'new_v_w_uq', 'new_v_kv_norm', 'new_v_w_ukv', 'new_v_attn_out_norm', 'new_v_w_out', 'new_v_norm_mem_q', 'new_v_norm_mem_kv', 'new_v_w_mq', 'new_v_w_mk', 'new_v_w_mv', 'new_v_w_mo', 'new_v_norm_ffn', 'new_v_w_up', 'new_v_ffn_conv_w', 'new_v_ffn_conv_b', 'new_v_w_down', 'new_v_final_norm']
TWIN_LEAF_KINDS = {'loss': 'loss', 'grad_x': 'grad_x', 'grad_norm_mix': 'grad_w', 'grad_w_in': 'grad_w', 'grad_ssm_conv_w': 'grad_w', 'grad_ssm_conv_b': 'grad_w', 'grad_dt_bias': 'grad_w', 'grad_a_log': 'grad_w', 'grad_d_skip': 'grad_w', 'grad_ssm_norm': 'grad_w', 'grad_q_norm': 'grad_w', 'grad_w_uq': 'grad_w', 'grad_kv_norm': 'grad_w', 'grad_w_ukv': 'grad_w', 'grad_attn_out_norm': 'grad_w', 'grad_w_out': 'grad_w', 'grad_norm_mem_q': 'grad_w', 'grad_norm_mem_kv': 'grad_w', 'grad_w_mq': 'grad_w', 'grad_w_mk': 'grad_w', 'grad_w_mv': 'grad_w', 'grad_w_mo': 'grad_w', 'grad_norm_ffn': 'grad_w', 'grad_w_up': 'grad_w', 'grad_ffn_conv_w': 'grad_w', 'grad_ffn_conv_b': 'grad_w', 'grad_w_down': 'grad_w', 'grad_final_norm': 'grad_w', 'delta_norm_mix': 'delta_w', 'delta_w_in': 'delta_w', 'delta_ssm_conv_w': 'delta_w', 'delta_ssm_conv_b': 'delta_w', 'delta_dt_bias': 'delta_w', 'delta_a_log': 'delta_w', 'delta_d_skip': 'delta_w', 'delta_ssm_norm': 'delta_w', 'delta_q_norm': 'delta_w', 'delta_w_uq': 'delta_w', 'delta_kv_norm': 'delta_w', 'delta_w_ukv': 'delta_w', 'delta_attn_out_norm': 'delta_w', 'delta_w_out': 'delta_w', 'delta_norm_mem_q': 'delta_w', 'delta_norm_mem_kv': 'delta_w', 'delta_w_mq': 'delta_w', 'delta_w_mk': 'delta_w', 'delta_w_mv': 'delta_w', 'delta_w_mo': 'delta_w', 'delta_norm_ffn': 'delta_w', 'delta_w_up': 'delta_w', 'delta_ffn_conv_w': 'delta_w', 'delta_ffn_conv_b': 'delta_w', 'delta_w_down': 'delta_w', 'delta_final_norm': 'delta_w', 'new_m_norm_mix': 'new_m', 'new_m_w_in': 'new_m', 'new_m_ssm_conv_w': 'new_m', 'new_m_ssm_conv_b': 'new_m', 'new_m_dt_bias': 'new_m', 'new_m_a_log': 'new_m', 'new_m_d_skip': 'new_m', 'new_m_ssm_norm': 'new_m', 'new_m_q_norm': 'new_m', 'new_m_w_uq': 'new_m', 'new_m_kv_norm': 'new_m', 'new_m_w_ukv': 'new_m', 'new_m_attn_out_norm': 'new_m', 'new_m_w_out': 'new_m', 'new_m_norm_mem_q': 'new_m', 'new_m_norm_mem_kv': 'new_m', 'new_m_w_mq': 'new_m', 'new_m_w_mk': 'new_m', 'new_m_w_mv': 'new_m', 'new_m_w_mo': 'new_m', 'new_m_norm_ffn': 'new_m', 'new_m_w_up': 'new_m', 'new_m_ffn_conv_w': 'new_m', 'new_m_ffn_conv_b': 'new_m', 'new_m_w_down': 'new_m', 'new_m_final_norm': 'new_m', 'new_v_norm_mix': 'new_v', 'new_v_w_in': 'new_v', 'new_v_ssm_conv_w': 'new_v', 'new_v_ssm_conv_b': 'new_v', 'new_v_dt_bias': 'new_v', 'new_v_a_log': 'new_v', 'new_v_d_skip': 'new_v', 'new_v_ssm_norm': 'new_v', 'new_v_q_norm': 'new_v', 'new_v_w_uq': 'new_v', 'new_v_kv_norm': 'new_v', 'new_v_w_ukv': 'new_v', 'new_v_attn_out_norm': 'new_v', 'new_v_w_out': 'new_v', 'new_v_norm_mem_q': 'new_v', 'new_v_norm_mem_kv': 'new_v', 'new_v_w_mq': 'new_v', 'new_v_w_mk': 'new_v', 'new_v_w_mv': 'new_v', 'new_v_w_mo': 'new_v', 'new_v_norm_ffn': 'new_v', 'new_v_w_up': 'new_v', 'new_v_ffn_conv_w': 'new_v', 'new_v_ffn_conv_b': 'new_v', 'new_v_w_down': 'new_v', 'new_v_final_norm': 'new_v'}


def _forward(args):
    return _fwd_reference(*[args[k] for k in FWD_PARAMS])


def _output_shape():
    out = _jax.eval_shape(lambda: _forward(_fwd_setup_inputs(0)))
    return out.shape, out.dtype

N_MICROBATCH = 1
ADAM_LR = 0.001
ADAM_B1 = 0.9
ADAM_B2 = 0.999
ADAM_EPS = 1e-08
ADAM_WD = 0.01
ADAM_STEP = 10
PER_EXAMPLE_BATCH_AXIS = {'x': 0, 'mem': 0, 'positions': 0, 'loss_target': 0}
SHARED_INPUTS = []
_WEIGHT_DTYPES = {'norm_mix': _jnp.float32, 'w_in': _jnp.float32, 'ssm_conv_w': _jnp.float32, 'ssm_conv_b': _jnp.float32, 'dt_bias': _jnp.float32, 'a_log': _jnp.float32, 'd_skip': _jnp.float32, 'ssm_norm': _jnp.float32, 'q_norm': _jnp.float32, 'w_uq': _jnp.float32, 'kv_norm': _jnp.float32, 'w_ukv': _jnp.float32, 'attn_out_norm': _jnp.float32, 'w_out': _jnp.float32, 'norm_mem_q': _jnp.float32, 'norm_mem_kv': _jnp.float32, 'w_mq': _jnp.float32, 'w_mk': _jnp.float32, 'w_mv': _jnp.float32, 'w_mo': _jnp.float32, 'norm_ffn': _jnp.float32, 'w_up': _jnp.float32, 'ffn_conv_w': _jnp.float32, 'ffn_conv_b': _jnp.float32, 'w_down': _jnp.float32, 'final_norm': _jnp.float32}
MOMENT_SCALE = {'norm_mix': 1.330712e-01, 'w_in': 6.914014e-02, 'ssm_conv_w': 4.659167e-02, 'ssm_conv_b': 7.518202e-02, 'dt_bias': 1.773518e-01, 'a_log': 3.177557e-01, 'd_skip': 2.425020e-01, 'ssm_norm': 6.375649e-02, 'q_norm': 7.143357e-02, 'w_uq': 3.635017e-02, 'kv_norm': 2.031936e-01, 'w_ukv': 6.620084e-02, 'attn_out_norm': 8.267042e-02, 'w_out': 1.068867e-01, 'norm_mem_q': 9.221829e-03, 'norm_mem_kv': 1.761679e-02, 'w_mq': 8.991719e-03, 'w_mk': 9.005074e-03, 'w_mv': 1.478240e-02, 'w_mo': 1.447432e-02, 'norm_ffn': 6.535609e-02, 'w_up': 2.799377e-02, 'ffn_conv_w': 2.828507e-02, 'ffn_conv_b': 3.599710e-02, 'w_down': 4.611487e-02, 'final_norm': 1.622203e+01}


def _to_microbatches(a, axis):
    t = _jnp.moveaxis(a, axis, 0)
    t = t.reshape((N_MICROBATCH, t.shape[0] // N_MICROBATCH) + t.shape[1:])
    return _jnp.moveaxis(t, 1, axis + 1)


def setup_inputs(seed: int = 0) -> dict:
    inp = _fwd_setup_inputs(seed)
    key = _jax.random.fold_in(_jax.random.key(seed), 7919)
    shape, _ = _output_shape()
    out = dict(inp)
    out["loss_target"] = _jax.random.normal(_jax.random.fold_in(key, 0), shape, _jnp.float32)
    for i, name in enumerate(TWIN_WEIGHTS):
        w = inp[name].astype(_jnp.float32)
        if MOMENT_SCALE is None:
            s = _jnp.sqrt(_jnp.mean(_jnp.square(w)) + 1e-30)
        else:
            s = MOMENT_SCALE[name]
        km, kv = _jax.random.split(_jax.random.fold_in(key, i + 1))
        out[name] = w
        out["m_" + name] = s * _jax.random.normal(km, w.shape, _jnp.float32)
        out["v_" + name] = (s * s) * _jax.random.uniform(kv, w.shape, _jnp.float32, 0.5, 1.5)
    if N_MICROBATCH > 1:
        for name, axis in PER_EXAMPLE_BATCH_AXIS.items():
            out[name] = _to_microbatches(out[name], axis)
    return {'x': out['x'], 'mem': out['mem'], 'positions': out['positions'], 'norm_mix': out['norm_mix'], 'w_in': out['w_in'], 'ssm_conv_w': out['ssm_conv_w'], 'ssm_conv_b': out['ssm_conv_b'], 'dt_bias': out['dt_bias'], 'a_log': out['a_log'], 'd_skip': out['d_skip'], 'ssm_norm': out['ssm_norm'], 'q_norm': out['q_norm'], 'w_uq': out['w_uq'], 'kv_norm': out['kv_norm'], 'w_ukv': out['w_ukv'], 'attn_out_norm': out['attn_out_norm'], 'w_out': out['w_out'], 'norm_mem_q': out['norm_mem_q'], 'norm_mem_kv': out['norm_mem_kv'], 'w_mq': out['w_mq'], 'w_mk': out['w_mk'], 'w_mv': out['w_mv'], 'w_mo': out['w_mo'], 'norm_ffn': out['norm_ffn'], 'w_up': out['w_up'], 'ffn_conv_w': out['ffn_conv_w'], 'ffn_conv_b': out['ffn_conv_b'], 'w_down': out['w_down'], 'final_norm': out['final_norm'], 'loss_target': out['loss_target'], 'm_norm_mix': out['m_norm_mix'], 'm_w_in': out['m_w_in'], 'm_ssm_conv_w': out['m_ssm_conv_w'], 'm_ssm_conv_b': out['m_ssm_conv_b'], 'm_dt_bias': out['m_dt_bias'], 'm_a_log': out['m_a_log'], 'm_d_skip': out['m_d_skip'], 'm_ssm_norm': out['m_ssm_norm'], 'm_q_norm': out['m_q_norm'], 'm_w_uq': out['m_w_uq'], 'm_kv_norm': out['m_kv_norm'], 'm_w_ukv': out['m_w_ukv'], 'm_attn_out_norm': out['m_attn_out_norm'], 'm_w_out': out['m_w_out'], 'm_norm_mem_q': out['m_norm_mem_q'], 'm_norm_mem_kv': out['m_norm_mem_kv'], 'm_w_mq': out['m_w_mq'], 'm_w_mk': out['m_w_mk'], 'm_w_mv': out['m_w_mv'], 'm_w_mo': out['m_w_mo'], 'm_norm_ffn': out['m_norm_ffn'], 'm_w_up': out['m_w_up'], 'm_ffn_conv_w': out['m_ffn_conv_w'], 'm_ffn_conv_b': out['m_ffn_conv_b'], 'm_w_down': out['m_w_down'], 'm_final_norm': out['m_final_norm'], 'v_norm_mix': out['v_norm_mix'], 'v_w_in': out['v_w_in'], 'v_ssm_conv_w': out['v_ssm_conv_w'], 'v_ssm_conv_b': out['v_ssm_conv_b'], 'v_dt_bias': out['v_dt_bias'], 'v_a_log': out['v_a_log'], 'v_d_skip': out['v_d_skip'], 'v_ssm_norm': out['v_ssm_norm'], 'v_q_norm': out['v_q_norm'], 'v_w_uq': out['v_w_uq'], 'v_kv_norm': out['v_kv_norm'], 'v_w_ukv': out['v_w_ukv'], 'v_attn_out_norm': out['v_attn_out_norm'], 'v_w_out': out['v_w_out'], 'v_norm_mem_q': out['v_norm_mem_q'], 'v_norm_mem_kv': out['v_norm_mem_kv'], 'v_w_mq': out['v_w_mq'], 'v_w_mk': out['v_w_mk'], 'v_w_mv': out['v_w_mv'], 'v_w_mo': out['v_w_mo'], 'v_norm_ffn': out['v_norm_ffn'], 'v_w_up': out['v_w_up'], 'v_ffn_conv_w': out['v_ffn_conv_w'], 'v_ffn_conv_b': out['v_ffn_conv_b'], 'v_w_down': out['v_w_down'], 'v_final_norm': out['v_final_norm']}


def _loss(weights, diff, rest, loss_target):
    with _jax.named_scope("forward"):
        args = {**rest, TWIN_DIFF_INPUT: diff, **{k: w.astype(_WEIGHT_DTYPES[k]) for k, w in weights.items()}}
        y = _forward(args)
    with _jax.named_scope("loss_head"):
        err = _jnp.square(y.astype(_jnp.float32) - loss_target)
        return 0.5 * _jnp.sum(_jnp.mean(err, axis=-1)) if err.ndim else 0.5 * err


def _adamw(w, g, m, v):
    m = ADAM_B1 * m + (1.0 - ADAM_B1) * g
    v = ADAM_B2 * v + (1.0 - ADAM_B2) * _jnp.square(g)
    m_hat = m / (1.0 - ADAM_B1 ** ADAM_STEP)
    v_hat = v / (1.0 - ADAM_B2 ** ADAM_STEP)
    delta = -ADAM_LR * (m_hat / (_jnp.sqrt(v_hat) + ADAM_EPS) + ADAM_WD * w)
    return delta, m, v


def reference(x, mem, positions, norm_mix, w_in, ssm_conv_w, ssm_conv_b, dt_bias, a_log, d_skip, ssm_norm, q_norm, w_uq, kv_norm, w_ukv, attn_out_norm, w_out, norm_mem_q, norm_mem_kv, w_mq, w_mk, w_mv, w_mo, norm_ffn, w_up, ffn_conv_w, ffn_conv_b, w_down, final_norm, loss_target, m_norm_mix, m_w_in, m_ssm_conv_w, m_ssm_conv_b, m_dt_bias, m_a_log, m_d_skip, m_ssm_norm, m_q_norm, m_w_uq, m_kv_norm, m_w_ukv, m_attn_out_norm, m_w_out, m_norm_mem_q, m_norm_mem_kv, m_w_mq, m_w_mk, m_w_mv, m_w_mo, m_norm_ffn, m_w_up, m_ffn_conv_w, m_ffn_conv_b, m_w_down, m_final_norm, v_norm_mix, v_w_in, v_ssm_conv_w, v_ssm_conv_b, v_dt_bias, v_a_log, v_d_skip, v_ssm_norm, v_q_norm, v_w_uq, v_kv_norm, v_w_ukv, v_attn_out_norm, v_w_out, v_norm_mem_q, v_norm_mem_kv, v_w_mq, v_w_mk, v_w_mv, v_w_mo, v_norm_ffn, v_w_up, v_ffn_conv_w, v_ffn_conv_b, v_w_down, v_final_norm):
    given = dict(x=x, mem=mem, positions=positions, norm_mix=norm_mix, w_in=w_in, ssm_conv_w=ssm_conv_w, ssm_conv_b=ssm_conv_b, dt_bias=dt_bias, a_log=a_log, d_skip=d_skip, ssm_norm=ssm_norm, q_norm=q_norm, w_uq=w_uq, kv_norm=kv_norm, w_ukv=w_ukv, attn_out_norm=attn_out_norm, w_out=w_out, norm_mem_q=norm_mem_q, norm_mem_kv=norm_mem_kv, w_mq=w_mq, w_mk=w_mk, w_mv=w_mv, w_mo=w_mo, norm_ffn=norm_ffn, w_up=w_up, ffn_conv_w=ffn_conv_w, ffn_conv_b=ffn_conv_b, w_down=w_down, final_norm=final_norm, loss_target=loss_target, m_norm_mix=m_norm_mix, m_w_in=m_w_in, m_ssm_conv_w=m_ssm_conv_w, m_ssm_conv_b=m_ssm_conv_b, m_dt_bias=m_dt_bias, m_a_log=m_a_log, m_d_skip=m_d_skip, m_ssm_norm=m_ssm_norm, m_q_norm=m_q_norm, m_w_uq=m_w_uq, m_kv_norm=m_kv_norm, m_w_ukv=m_w_ukv, m_attn_out_norm=m_attn_out_norm, m_w_out=m_w_out, m_norm_mem_q=m_norm_mem_q, m_norm_mem_kv=m_norm_mem_kv, m_w_mq=m_w_mq, m_w_mk=m_w_mk, m_w_mv=m_w_mv, m_w_mo=m_w_mo, m_norm_ffn=m_norm_ffn, m_w_up=m_w_up, m_ffn_conv_w=m_ffn_conv_w, m_ffn_conv_b=m_ffn_conv_b, m_w_down=m_w_down, m_final_norm=m_final_norm, v_norm_mix=v_norm_mix, v_w_in=v_w_in, v_ssm_conv_w=v_ssm_conv_w, v_ssm_conv_b=v_ssm_conv_b, v_dt_bias=v_dt_bias, v_a_log=v_a_log, v_d_skip=v_d_skip, v_ssm_norm=v_ssm_norm, v_q_norm=v_q_norm, v_w_uq=v_w_uq, v_kv_norm=v_kv_norm, v_w_ukv=v_w_ukv, v_attn_out_norm=v_attn_out_norm, v_w_out=v_w_out, v_norm_mem_q=v_norm_mem_q, v_norm_mem_kv=v_norm_mem_kv, v_w_mq=v_w_mq, v_w_mk=v_w_mk, v_w_mv=v_w_mv, v_w_mo=v_w_mo, v_norm_ffn=v_norm_ffn, v_w_up=v_w_up, v_ffn_conv_w=v_ffn_conv_w, v_ffn_conv_b=v_ffn_conv_b, v_w_down=v_w_down, v_final_norm=v_final_norm)
    weights = {n: given[n] for n in TWIN_WEIGHTS}
    shared = {n: given[n] for n in SHARED_INPUTS}
    per_example = {n: given[n] for n in ['x', 'mem', 'positions']}
    grad_fn = _jax.value_and_grad(_loss, argnums=(0, 1))

    def one_microbatch(ex, loss_target):
        ex = dict(ex)
        diff = ex.pop(TWIN_DIFF_INPUT)
        return grad_fn(weights, diff, {**shared, **ex}, loss_target)

    if N_MICROBATCH == 1:
        loss, (grad_w, grad_x) = one_microbatch(per_example, given["loss_target"])
    else:
        def body(carry, xs):
            loss_sum, grad_sum = carry
            l_k, (gw_k, gx_k) = one_microbatch(xs[0], xs[1])
            with _jax.named_scope("update"):
                return (loss_sum + l_k, _jax.tree.map(_jnp.add, grad_sum, gw_k)), gx_k

        init = (_jnp.zeros((), _jnp.float32), _jax.tree.map(_jnp.zeros_like, weights))
        (loss, grad_w), grad_x = _jax.lax.scan(body, init, (per_example, given["loss_target"]))
    with _jax.named_scope("update"):
        delta_w, new_m, new_v = {}, {}, {}
        for n in TWIN_WEIGHTS:
            delta_w[n], new_m[n], new_v[n] = _adamw(weights[n], grad_w[n], given["m_" + n], given["v_" + n])
    return (loss, grad_x, *[grad_w[n] for n in TWIN_WEIGHTS], *[delta_w[n] for n in TWIN_WEIGHTS],
            *[new_m[n] for n in TWIN_WEIGHTS], *[new_v[n] for n in TWIN_WEIGHTS])
```

```python
import functools
import math

import jax
import jax.numpy as jnp
from jax import lax
from jax.experimental import pallas as pl
from jax.experimental.pallas import tpu as pltpu

F32 = jnp.float32
BF16 = jnp.bfloat16
HIGHEST = lax.Precision.HIGHEST
SDS = jax.ShapeDtypeStruct
MESH = pl.DeviceIdType.MESH

D_MODEL = 1024
DEPTH = 4
EPS = 1e-6
SSM_HEADS = 16
SSM_HEAD_DIM = 64
D_SSM = 1024
SSM_GROUPS = 4
SSM_STATE = 128
SSM_CONV = 4
SSM_CHUNK = 128
CONV_CH = 2048
MLA_HEADS = 16
QK_NOPE = 64
QK_ROPE = 32
V_DIM = 64
Q_LORA = 384
KV_LORA = 256
ROPE_THETA = 10000.0
MEM_HEADS = 4
MEM_HEAD_DIM = 256
D_FF = 2816
FFN_CONV = 3
D_IN = 3760
ADAM_LR = 0.001
ADAM_B1 = 0.9
ADAM_B2 = 0.999
ADAM_EPS = 1e-08
ADAM_WD = 0.01
ADAM_STEP = 10

LANES = 128
HEAD_PAD = 128
N_CHIPS = 4
N_DEV = 8
VMEM_CAP_MB = 56

P_XBC, P_Z, P_CQ, P_DT, P_CKV, P_KR, P_IN = 0, 2048, 3072, 3456, 3584, 3840, 4096
NEG = -1e30


def _tile(n, pref):
    t = (min(n, pref) // LANES) * LANES
    while t >= LANES:
        if n % t == 0:
            return t
        t -= LANES
    return n


def _params(sem=None, vmem_bytes=None):
    kw = {}
    if sem is not None:
        kw["dimension_semantics"] = sem
    if vmem_bytes is not None:
        kw["vmem_limit_bytes"] = int(min(max(vmem_bytes, 16 << 20), VMEM_CAP_MB << 20))
    return pltpu.CompilerParams(**kw)


def _nbytes(shape, dtype):
    return math.prod(shape) * jnp.dtype(dtype).itemsize


def _mm(a, b, *, ta=False, tb=False, res=None, out_dtype=F32, name):
    (k, m) = a.shape if ta else a.shape[::-1]
    (n, kb) = b.shape if tb else b.shape[::-1]
    assert k == kb, (a.shape, b.shape, ta, tb)
    tm, tn = _tile(m, 512), _tile(n, 512)
    a_blk = (k, tm) if ta else (tm, k)
    b_blk = (tn, k) if tb else (k, tn)
    a_spec = pl.BlockSpec(a_blk, (lambda i, j: (0, i)) if ta else (lambda i, j: (i, 0)))
    b_spec = pl.BlockSpec(b_blk, (lambda i, j: (j, 0)) if tb else (lambda i, j: (0, j)))
    o_spec = pl.BlockSpec((tm, tn), lambda i, j: (i, j))
    dims = (((0 if ta else 1,), (1 if tb else 0,)), ((), ()))
    has_res = res is not None

    def body(*refs):
        a_ref, b_ref = refs[0], refs[1]
        o_ref = refs[-1]
        acc = lax.dot_general(a_ref[...].astype(BF16), b_ref[...].astype(BF16), dims, preferred_element_type=F32)
        if has_res:
            acc = acc + refs[2][...]
        o_ref[...] = acc.astype(o_ref.dtype)

    vmem = 2 * (_nbytes(a_blk, a.dtype) + _nbytes(b_blk, b.dtype) + (2 if has_res else 1) * _nbytes((tm, tn), F32))
    vmem += _nbytes(a_blk, BF16) + _nbytes(b_blk, BF16) + 2 * _nbytes((tm, tn), F32) + (4 << 20)
    args = (a, b) + ((res,) if has_res else ())
    specs = [a_spec, b_spec] + ([o_spec] if has_res else [])
    return pl.pallas_call(body, grid=(m // tm, n // tn), in_specs=specs, out_specs=o_spec,
                          out_shape=SDS((m, n), out_dtype), name=name,
                          compiler_params=_params(("parallel", "parallel"), vmem))(*args)


def _sigmoid(x):
    return 1.0 / (1.0 + jnp.exp(-x))


def _rms_fwd(x, g, *, col=None, name):
    s = x.shape[0]
    w, ci = (x.shape[1], 0) if col is None else col
    tm = min(s, 512)

    def body(x_ref, g_ref, o_ref):
        xv = x_ref[...].astype(F32)
        r = lax.rsqrt(jnp.mean(xv * xv, axis=-1, keepdims=True) + EPS)
        o_ref[...] = (xv * r * g_ref[...]).astype(o_ref.dtype)

    return pl.pallas_call(
        body, grid=(s // tm,),
        in_specs=[pl.BlockSpec((tm, w), lambda i: (i, ci)), pl.BlockSpec((1, w), lambda i: (0, 0))],
        out_specs=pl.BlockSpec((tm, w), lambda i: (i, 0)), out_shape=SDS((s, w), BF16), name=name,
        compiler_params=_params(("parallel",), 10 * tm * w * 4))(x, g.reshape(1, w))


def _rms_bwd(x, g, dy, dres=None, *, col=None, name):
    s = x.shape[0]
    w, ci = (x.shape[1], 0) if col is None else col
    tm = min(s, 512)
    has_res = dres is not None

    def body(*refs):
        x_ref, g_ref, dy_ref = refs[:3]
        dx_ref, dg_ref = refs[-2:]
        xv = x_ref[...].astype(F32)
        dyv = dy_ref[...].astype(F32)
        r = lax.rsqrt(jnp.mean(xv * xv, axis=-1, keepdims=True) + EPS)
        u = dyv * g_ref[...]
        dx = r * u - xv * (r * r * r) * jnp.mean(xv * u, axis=-1, keepdims=True)
        if has_res:
            dx = dx + refs[3][...]
        dx_ref[...] = dx

        @pl.when(pl.program_id(0) == 0)
        def _():
            dg_ref[...] = jnp.zeros_like(dg_ref)

        dg_ref[...] += jnp.sum(dyv * xv * r, axis=0, keepdims=True)

    blk = pl.BlockSpec((tm, w), lambda i: (i, 0))
    specs = [pl.BlockSpec((tm, w), lambda i: (i, ci)), pl.BlockSpec((1, w), lambda i: (0, 0)), blk]
    args = [x, g.reshape(1, w), dy]
    if has_res:
        specs.append(blk)
        args.append(dres)
    dx, dg = pl.pallas_call(
        body, grid=(s // tm,), in_specs=specs,
        out_specs=(blk, pl.BlockSpec((1, w), lambda i: (0, 0))),
        out_shape=(SDS((s, w), F32), SDS((1, w), F32)), name=name,
        compiler_params=_params(("arbitrary",), 16 * tm * w * 4))(*args)
    return dx, dg.reshape(w)


def _gated_rms_fwd(y, proj, g, *, name):
    s, w = y.shape
    tm = min(s, 512)

    def body(y_ref, z_ref, g_ref, o_ref):
        z = z_ref[...]
        t = y_ref[...] * (z * _sigmoid(z))
        r = lax.rsqrt(jnp.mean(t * t, axis=-1, keepdims=True) + EPS)
        o_ref[...] = (t * r * g_ref[...]).astype(o_ref.dtype)

    blk = pl.BlockSpec((tm, w), lambda i: (i, 0))
    return pl.pallas_call(
        body, grid=(s // tm,),
        in_specs=[blk, pl.BlockSpec((tm, w), lambda i: (i, P_Z // w)), pl.BlockSpec((1, w), lambda i: (0, 0))],
        out_specs=blk, out_shape=SDS((s, w), BF16), name=name,
        compiler_params=_params(("parallel",), 14 * tm * w * 4))(y, proj, g.reshape(1, w))


def _gated_rms_bwd(y, proj, g, dout, *, name):
    s, w = y.shape
    tm = min(s, 512)

    def body(y_ref, z_ref, g_ref, do_ref, dy_ref, dz_ref, dg_ref):
        z = z_ref[...]
        yv = y_ref[...]
        dov = do_ref[...]
        sg = _sigmoid(z)
        sz = z * sg
        t = yv * sz
        r = lax.rsqrt(jnp.mean(t * t, axis=-1, keepdims=True) + EPS)
        u = dov * g_ref[...]
        dt = r * u - t * (r * r * r) * jnp.mean(t * u, axis=-1, keepdims=True)
        dy_ref[...] = dt * sz
        dz_ref[...] = dt * yv * (sg * (1.0 + z * (1.0 - sg)))

        @pl.when(pl.program_id(0) == 0)
        def _():
            dg_ref[...] = jnp.zeros_like(dg_ref)

        dg_ref[...] += jnp.sum(dov * t * r, axis=0, keepdims=True)

    blk = pl.BlockSpec((tm, w), lambda i: (i, 0))
    vec = pl.BlockSpec((1, w), lambda i: (0, 0))
    dy, dz, dg = pl.pallas_call(
        body, grid=(s // tm,),
        in_specs=[blk, pl.BlockSpec((tm, w), lambda i: (i, P_Z // w)), vec, blk],
        out_specs=(blk, blk, vec), out_shape=(SDS((s, w), F32), SDS((s, w), F32), SDS((1, w), F32)), name=name,
        compiler_params=_params(("arbitrary",), 24 * tm * w * 4))(y, proj, g.reshape(1, w), dout)
    return dy, dz, dg.reshape(w)


def _final_loss(x, g, target, *, name):
    s, w = x.shape
    tm = min(s, 512)

    def body(x_ref, g_ref, t_ref, loss_ref, dx_ref, dg_ref):
        xv = x_ref[...]
        gv = g_ref[...]
        r = lax.rsqrt(jnp.mean(xv * xv, axis=-1, keepdims=True) + EPS)
        xn = xv * r
        diff = xn * gv - t_ref[...]
        dy = diff * (1.0 / w)
        u = dy * gv
        dx_ref[...] = r * u - xv * (r * r * r) * jnp.mean(xv * u, axis=-1, keepdims=True)

        @pl.when(pl.program_id(0) == 0)
        def _():
            dg_ref[...] = jnp.zeros_like(dg_ref)
            loss_ref[...] = jnp.zeros_like(loss_ref)

        dg_ref[...] += jnp.sum(dy * xn, axis=0, keepdims=True)
        part = jnp.sum(jnp.sum(diff * diff, axis=1, keepdims=True), axis=0, keepdims=True) * (0.5 / w)
        loss_ref[...] += jnp.broadcast_to(part, loss_ref.shape)

    blk = pl.BlockSpec((tm, w), lambda i: (i, 0))
    vec = pl.BlockSpec((1, w), lambda i: (0, 0))
    loss, dx, dg = pl.pallas_call(
        body, grid=(s // tm,), in_specs=[blk, vec, blk],
        out_specs=(pl.BlockSpec((1, LANES), lambda i: (0, 0)), blk, vec),
        out_shape=(SDS((1, LANES), F32), SDS((s, w), F32), SDS((1, w), F32)), name=name,
        compiler_params=_params(("arbitrary",), 16 * tm * w * 4))(x, g.reshape(1, w), target)
    return loss[0, 0], dx, dg.reshape(w)


def _shift_down(x, k):
    if k == 0:
        return x
    row = lax.broadcasted_iota(jnp.int32, x.shape, 0)
    return jnp.where(row < k, 0.0, pltpu.roll(x, k, axis=0))


def _shift_up(x, k):
    if k == 0:
        return x
    s = x.shape[0]
    row = lax.broadcasted_iota(jnp.int32, x.shape, 0)
    return jnp.where(row >= s - k, 0.0, pltpu.roll(x, s - k, axis=0))


def _conv_pre(x, w, b, kw):
    pre = b
    for j in range(kw):
        pre = pre + w[j:j + 1, :] * _shift_down(x, kw - 1 - j)
    return pre


def _conv_bwd_terms(x, w, dpre, kw):
    dx = jnp.zeros_like(x)
    dws = []
    for j in range(kw):
        dx = dx + w[j:j + 1, :] * _shift_up(dpre, kw - 1 - j)
        dws.append(jnp.sum(dpre * _shift_down(x, kw - 1 - j), axis=0, keepdims=True))
    return dx, jnp.concatenate(dws, axis=0), jnp.sum(dpre, axis=0, keepdims=True)


def _ssm_conv_fwd(proj, w, b, *, name):
    s = proj.shape[0]
    cw = 256

    def body(x_ref, w_ref, b_ref, o_ref):
        pre = _conv_pre(x_ref[...], w_ref[...], b_ref[...], SSM_CONV)
        o_ref[...] = pre * _sigmoid(pre)

    return pl.pallas_call(
        body, grid=(CONV_CH // cw,),
        in_specs=[pl.BlockSpec((s, cw), lambda j: (0, j)), pl.BlockSpec((SSM_CONV, cw), lambda j: (0, j)),
                  pl.BlockSpec((1, cw), lambda j: (0, j))],
        out_specs=pl.BlockSpec((s, cw), lambda j: (0, j)), out_shape=SDS((s, CONV_CH), F32), name=name,
        compiler_params=_params(("parallel",), 12 * s * cw * 4))(proj, w, b.reshape(1, CONV_CH))


def _ssm_conv_bwd(proj, w, b, dxbc, *, name):
    s = proj.shape[0]
    cw = 256

    def body(x_ref, w_ref, b_ref, dy_ref, dx_ref, dw_ref, db_ref):
        x = x_ref[...]
        wv = w_ref[...]
        pre = _conv_pre(x, wv, b_ref[...], SSM_CONV)
        sg = _sigmoid(pre)
        dpre = dy_ref[...] * (sg * (1.0 + pre * (1.0 - sg)))
        dx, dw, db = _conv_bwd_terms(x, wv, dpre, SSM_CONV)
        dx_ref[...] = dx.astype(dx_ref.dtype)
        dw_ref[...] = dw
        db_ref[...] = db

    col = pl.BlockSpec((s, cw), lambda j: (0, j))
    wsp = pl.BlockSpec((SSM_CONV, cw), lambda j: (0, j))
    bsp = pl.BlockSpec((1, cw), lambda j: (0, j))
    dx, dw, db = pl.pallas_call(
        body, grid=(CONV_CH // cw,), in_specs=[col, wsp, bsp, col], out_specs=(col, wsp, bsp),
        out_shape=(SDS((s, CONV_CH), BF16), SDS((SSM_CONV, CONV_CH), F32), SDS((1, CONV_CH), F32)), name=name,
        compiler_params=_params(("parallel",), 20 * s * cw * 4))(proj, w, b.reshape(1, CONV_CH), dxbc)
    return dx, dw, db.reshape(CONV_CH)


def _ffn_conv_fwd(up_g, up_v, w, b, *, name):
    s = up_g.shape[0]
    cw = 256
    nb = D_FF // cw

    def body(g_ref, v_ref, wg_ref, wv_ref, bg_ref, bv_ref, o_ref):
        gate = _conv_pre(g_ref[...], wg_ref[...], bg_ref[...], FFN_CONV)
        val = _conv_pre(v_ref[...], wv_ref[...], bv_ref[...], FFN_CONV)
        o_ref[...] = (gate * _sigmoid(gate) * val).astype(o_ref.dtype)

    col = pl.BlockSpec((s, cw), lambda j: (0, j))
    b2 = b.reshape(1, 2 * D_FF)
    return pl.pallas_call(
        body, grid=(nb,),
        in_specs=[col, col, pl.BlockSpec((FFN_CONV, cw), lambda j: (0, j)), pl.BlockSpec((FFN_CONV, cw), lambda j: (0, j + nb)),
                  pl.BlockSpec((1, cw), lambda j: (0, j)), pl.BlockSpec((1, cw), lambda j: (0, j + nb))],
        out_specs=col, out_shape=SDS((s, D_FF), BF16), name=name,
        compiler_params=_params(("parallel",), 16 * s * cw * 4))(up_g, up_v, w, w, b2, b2)


def _ffn_conv_bwd(up_g, up_v, w, b, dact, *, name):
    s = up_g.shape[0]
    cw = 256
    nb = D_FF // cw

    def body(g_ref, v_ref, wg_ref, wv_ref, bg_ref, bv_ref, da_ref, dg_ref, dv_ref, dwg_ref, dwv_ref, dbg_ref, dbv_ref):
        xg, xv = g_ref[...], v_ref[...]
        wg, wv = wg_ref[...], wv_ref[...]
        gate = _conv_pre(xg, wg, bg_ref[...], FFN_CONV)
        val = _conv_pre(xv, wv, bv_ref[...], FFN_CONV)
        da = da_ref[...].astype(F32)
        sg = _sigmoid(gate)
        dgate = da * val * (sg * (1.0 + gate * (1.0 - sg)))
        dval = da * gate * sg
        dxg, dwg, dbg = _conv_bwd_terms(xg, wg, dgate, FFN_CONV)
        dxv, dwv, dbv = _conv_bwd_terms(xv, wv, dval, FFN_CONV)
        dg_ref[...] = dxg.astype(dg_ref.dtype)
        dv_ref[...] = dxv.astype(dv_ref.dtype)
        dwg_ref[...] = dwg
        dwv_ref[...] = dwv
        dbg_ref[...] = dbg
        dbv_ref[...] = dbv

    col = pl.BlockSpec((s, cw), lambda j: (0, j))
    wsp = pl.BlockSpec((FFN_CONV, cw), lambda j: (0, j))
    bsp = pl.BlockSpec((1, cw), lambda j: (0, j))
    b2 = b.reshape(1, 2 * D_FF)
    dg, dv, dwg, dwv, dbg, dbv = pl.pallas_call(
        body, grid=(nb,),
        in_specs=[col, col, wsp, pl.BlockSpec((FFN_CONV, cw), lambda j: (0, j + nb)), bsp,
                  pl.BlockSpec((1, cw), lambda j: (0, j + nb)), col],
        out_specs=(col, col, wsp, wsp, bsp, bsp),
        out_shape=(SDS((s, D_FF), BF16), SDS((s, D_FF), BF16), SDS((FFN_CONV, D_FF), F32), SDS((FFN_CONV, D_FF), F32),
                   SDS((1, D_FF), F32), SDS((1, D_FF), F32)), name=name,
        compiler_params=_params(("parallel",), 32 * s * cw * 4))(up_g, up_v, w, w, b2, b2, dact)
    return dg, dv, jnp.concatenate([dwg, dwv], axis=1), jnp.concatenate([dbg, dbv], axis=1).reshape(2 * D_FF)


def _dot(a, b):
    return jnp.dot(a.astype(BF16), b.astype(BF16), preferred_element_type=F32)


def _dot_nt(a, b):
    return lax.dot_general(a.astype(BF16), b.astype(BF16), (((1,), (1,)), ((), ())), preferred_element_type=F32)


def _dot_tn(a, b):
    return lax.dot_general(a.astype(BF16), b.astype(BF16), (((0,), (0,)), ((), ())), preferred_element_type=F32)


def _ssd_chunk_terms(dtraw, bias, a_log):
    ell = dtraw.shape[0]
    lane = lax.broadcasted_iota(jnp.int32, dtraw.shape, 1)
    valid = lane < SSM_HEADS
    pre = dtraw + bias
    dt = jnp.where(valid, jnp.where(pre > 20.0, pre, jnp.log(1.0 + jnp.exp(jnp.minimum(pre, 20.0)))), 0.0)
    a = -jnp.exp(a_log)
    ad = dt * a
    row = lax.broadcasted_iota(jnp.int32, (ell, ell), 0)
    colm = lax.broadcasted_iota(jnp.int32, (ell, ell), 1)
    tril = row >= colm
    cs = jnp.dot(tril.astype(F32), ad, precision=HIGHEST, preferred_element_type=F32)
    cs_last = cs[ell - 1:ell, :]
    return pre, dt, a, cs, cs_last, tril


def _lane_put(col, h, shape):
    lane = lax.broadcasted_iota(jnp.int32, shape, 1)
    return jnp.where(lane == h, col, 0.0)


def _ssd_fwd(xbc, proj, dt_bias, a_log, d_skip, *, name):
    s = xbc.shape[0]
    nc = s // SSM_CHUNK
    ell, n, p = SSM_CHUNK, SSM_STATE, SSM_HEAD_DIM
    rpg = SSM_HEADS // SSM_GROUPS

    def body(x_ref, dt_ref, bias_ref, alog_ref, dskip_ref, y_ref, ps_ref, state):
        @pl.when(pl.program_id(0) == 0)
        def _():
            state[...] = jnp.zeros_like(state)

        _, dt, _, cs, cs_last, tril = _ssd_chunk_terms(dt_ref[...], bias_ref[...], alog_ref[...])
        e = jnp.exp(cs)
        ds = jnp.exp(cs_last - cs)
        cd = jnp.exp(cs_last)
        cst = cs.T
        dskip = dskip_ref[...]
        ps_ref[0] = state[...]
        for g in range(SSM_GROUPS):
            bg = x_ref[:, D_SSM + n * g:D_SSM + n * (g + 1)]
            cg = x_ref[:, D_SSM + n * (SSM_GROUPS + g):D_SSM + n * (SSM_GROUPS + g + 1)]
            cb = _dot_nt(cg, bg)
            for r in range(rpg):
                h = g * rpg + r
                hs = slice(p * h, p * (h + 1))
                xs = x_ref[:, hs]
                xd = xs * dt[:, h:h + 1]
                lmat = jnp.exp(jnp.where(tril, cs[:, h:h + 1] - cst[h:h + 1, :], -jnp.inf))
                prev = state[:, hs]
                y = _dot(cb * lmat, xd) + _dot(cg, prev) * e[:, h:h + 1] + xs * dskip[:, h:h + 1]
                y_ref[:, hs] = y
                state[:, hs] = prev * cd[:, h:h + 1] + _dot_tn(bg, xd * ds[:, h:h + 1])

    vec = pl.BlockSpec((1, LANES), lambda c: (0, 0))
    return pl.pallas_call(
        body, grid=(nc,),
        in_specs=[pl.BlockSpec((ell, CONV_CH), lambda c: (c, 0)), pl.BlockSpec((ell, LANES), lambda c: (c, P_DT // LANES)),
                  vec, vec, vec],
        out_specs=(pl.BlockSpec((ell, D_SSM), lambda c: (c, 0)), pl.BlockSpec((1, n, D_SSM), lambda c: (c, 0, 0))),
        out_shape=(SDS((s, D_SSM), F32), SDS((nc, n, D_SSM), F32)),
        scratch_shapes=[pltpu.VMEM((n, D_SSM), F32)], name=name,
        compiler_params=_params(("arbitrary",), 24 << 20))(xbc, proj, dt_bias, a_log, d_skip)


def _ssd_bwd(xbc, proj, dt_bias, a_log, d_skip, prev_states, dy, *, name):
    s = xbc.shape[0]
    nc = s // SSM_CHUNK
    ell, n, p = SSM_CHUNK, SSM_STATE, SSM_HEAD_DIM
    rpg = SSM_HEADS // SSM_GROUPS

    def body(x_ref, dt_ref, bias_ref, alog_ref, dskip_ref, ps_ref, dy_ref,
             dx_ref, ddt_ref, dalog_ref, ddskip_ref, dbias_ref, dstate):
        @pl.when(pl.program_id(0) == 0)
        def _():
            dstate[...] = jnp.zeros_like(dstate)
            dalog_ref[...] = jnp.zeros_like(dalog_ref)
            ddskip_ref[...] = jnp.zeros_like(ddskip_ref)
            dbias_ref[...] = jnp.zeros_like(dbias_ref)

        pre, dt, a, cs, cs_last, tril = _ssd_chunk_terms(dt_ref[...], bias_ref[...], alog_ref[...])
        e = jnp.exp(cs)
        ds = jnp.exp(cs_last - cs)
        cd = jnp.exp(cs_last)
        cst = cs.T
        dskip = dskip_ref[...]
        shape = (ell, LANES)
        ddt_acc = jnp.zeros(shape, F32)
        dcs_acc = jnp.zeros(shape, F32)
        dcs_rows = jnp.zeros(shape, F32)
        dlast_acc = jnp.zeros((1, LANES), F32)
        dskip_acc = jnp.zeros((1, LANES), F32)
        for g in range(SSM_GROUPS):
            bsl = slice(D_SSM + n * g, D_SSM + n * (g + 1))
            csl = slice(D_SSM + n * (SSM_GROUPS + g), D_SSM + n * (SSM_GROUPS + g + 1))
            bg = x_ref[:, bsl]
            cg = x_ref[:, csl]
            cb = _dot_nt(cg, bg)
            dcb = jnp.zeros((ell, ell), F32)
            dbg = jnp.zeros((ell, n), F32)
            dcg = jnp.zeros((ell, n), F32)
            for r in range(rpg):
                h = g * rpg + r
                hs = slice(p * h, p * (h + 1))
                xs = x_ref[:, hs]
                dyh = dy_ref[:, hs]
                dt_h, e_h, ds_h, cd_h = dt[:, h:h + 1], e[:, h:h + 1], ds[:, h:h + 1], cd[:, h:h + 1]
                prev = ps_ref[0, :, hs]
                dsn = dstate[:, hs]
                xd = xs * dt_h
                dye = dyh * e_h
                cprev = _dot(cg, prev)
                dprev = dsn * cd_h + _dot_tn(cg, dye)
                dcg = dcg + _dot_nt(dye, prev)
                dcs_h = jnp.sum(dyh * cprev, axis=1, keepdims=True) * e_h
                dcd = jnp.sum(jnp.sum(dsn * prev, axis=1, keepdims=True), axis=0, keepdims=True)
                dlast_h = dcd * cd_h
                dxdd = _dot(bg, dsn)
                dbg = dbg + _dot_nt(xd * ds_h, dsn)
                dxd = dxdd * ds_h
                tmp = jnp.sum(dxdd * xd, axis=1, keepdims=True) * ds_h
                dlast_h = dlast_h + jnp.sum(tmp, axis=0, keepdims=True)
                dcs_h = dcs_h - tmp
                lmat = jnp.exp(jnp.where(tril, cs[:, h:h + 1] - cst[h:h + 1, :], -jnp.inf))
                gm = cb * lmat
                dgm = _dot_nt(dyh, xd)
                dxd = dxd + _dot_tn(gm, dyh)
                mm = dgm * gm
                dcs_h = dcs_h + jnp.sum(mm, axis=1, keepdims=True)
                sub = lax.broadcasted_iota(jnp.int32, shape, 0)
                dcs_rows = dcs_rows + jnp.where(sub == h, jnp.sum(mm, axis=0, keepdims=True), 0.0)
                dcb = dcb + dgm * lmat
                dx_ref[:, hs] = dxd * dt_h + dyh * dskip[:, h:h + 1]
                ddt_acc = ddt_acc + _lane_put(jnp.sum(dxd * xs, axis=1, keepdims=True), h, shape)
                dcs_acc = dcs_acc + _lane_put(dcs_h, h, shape)
                dlast_acc = dlast_acc + _lane_put(dlast_h, h, (1, LANES))
                dskip_acc = dskip_acc + _lane_put(
                    jnp.sum(jnp.sum(dyh * xs, axis=1, keepdims=True), axis=0, keepdims=True), h, (1, LANES))
                dstate[:, hs] = dprev
            dx_ref[:, bsl] = dbg + _dot_tn(dcb, cg)
            dx_ref[:, csl] = dcg + _dot(dcb, bg)
        rowi = lax.broadcasted_iota(jnp.int32, shape, 0)
        dcs = dcs_acc - dcs_rows.T + jnp.where(rowi == ell - 1, dlast_acc, 0.0)
        triu = lax.broadcasted_iota(jnp.int32, (ell, ell), 0) <= lax.broadcasted_iota(jnp.int32, (ell, ell), 1)
        dad = jnp.dot(triu.astype(F32), dcs, precision=HIGHEST, preferred_element_type=F32)
        ddt = ddt_acc + dad * a
        dalog_ref[...] += jnp.sum(dad * dt, axis=0, keepdims=True) * a
        ddskip_ref[...] += dskip_acc
        lane = lax.broadcasted_iota(jnp.int32, shape, 1)
        ddraw = jnp.where(lane < SSM_HEADS, ddt * _sigmoid(pre), 0.0)
        ddt_ref[...] = ddraw
        dbias_ref[...] += jnp.sum(ddraw, axis=0, keepdims=True)

    vec = pl.BlockSpec((1, LANES), lambda c: (0, 0))
    rev = lambda c: nc - 1 - c
    outs = pl.pallas_call(
        body, grid=(nc,),
        in_specs=[pl.BlockSpec((ell, CONV_CH), lambda c: (rev(c), 0)),
                  pl.BlockSpec((ell, LANES), lambda c: (rev(c), P_DT // LANES)), vec, vec, vec,
                  pl.BlockSpec((1, n, D_SSM), lambda c: (rev(c), 0, 0)),
                  pl.BlockSpec((ell, D_SSM), lambda c: (rev(c), 0))],
        out_specs=(pl.BlockSpec((ell, CONV_CH), lambda c: (rev(c), 0)), pl.BlockSpec((ell, LANES), lambda c: (rev(c), 0)),
                   vec, vec, vec),
        out_shape=(SDS((s, CONV_CH), F32), SDS((s, LANES), F32), SDS((1, LANES), F32), SDS((1, LANES), F32),
                   SDS((1, LANES), F32)),
        scratch_shapes=[pltpu.VMEM((n, D_SSM), F32)], name=name,
        compiler_params=_params(("arbitrary",), 32 << 20))(xbc, proj, dt_bias, a_log, d_skip, prev_states, dy)
    return outs


def _rope_swap(t):
    lane = lax.broadcasted_iota(jnp.int32, t.shape, 1)
    half = QK_ROPE // 2
    lo = (lane >= QK_NOPE) & (lane < QK_NOPE + half)
    hi = (lane >= QK_NOPE + half) & (lane < QK_NOPE + QK_ROPE)
    return jnp.where(lo, pltpu.roll(t, HEAD_PAD - half, axis=1), jnp.where(hi, pltpu.roll(t, half, axis=1), 0.0))


def _mla_prep(q, kn, proj, cos, sins, *, name):
    s = q.shape[0]
    tm = min(s, 256)
    scale = (QK_NOPE + QK_ROPE) ** -0.5

    def body(q_ref, kn_ref, kr_ref, cos_ref, sin_ref, qo_ref, ko_ref):
        cosv, sinv = cos_ref[...], sin_ref[...]
        kr = pltpu.roll(kr_ref[...], QK_NOPE, axis=1)
        lane = lax.broadcasted_iota(jnp.int32, kr.shape, 1)
        kr = jnp.where(lane >= QK_NOPE, kr, 0.0)
        kpe = kr * cosv + _rope_swap(kr) * sinv
        for h in range(MLA_HEADS):
            hs = slice(HEAD_PAD * h, HEAD_PAD * (h + 1))
            qh = q_ref[:, hs]
            qo_ref[:, hs] = ((qh * cosv + _rope_swap(qh) * sinv) * scale).astype(qo_ref.dtype)
            ko_ref[:, hs] = (kn_ref[:, hs] + kpe).astype(ko_ref.dtype)

    wide = pl.BlockSpec((tm, MLA_HEADS * HEAD_PAD), lambda i: (i, 0))
    tab = pl.BlockSpec((tm, LANES), lambda i: (i, 0))
    return pl.pallas_call(
        body, grid=(s // tm,),
        in_specs=[wide, wide, pl.BlockSpec((tm, LANES), lambda i: (i, P_KR // LANES)), tab, tab],
        out_specs=(wide, wide),
        out_shape=(SDS((s, MLA_HEADS * HEAD_PAD), BF16), SDS((s, MLA_HEADS * HEAD_PAD), BF16)), name=name,
        compiler_params=_params(("parallel",), 32 << 20))(q, kn, proj, cos, sins)


def _mla_prep_bwd(dqr, dkr, cos, sins, *, name):
    s = dqr.shape[0]
    tm = min(s, 256)
    scale = (QK_NOPE + QK_ROPE) ** -0.5

    def body(dq_ref, dk_ref, cos_ref, sin_ref, dqo_ref, dkr_ref):
        cosv, sinv = cos_ref[...], sin_ref[...]
        lane = lax.broadcasted_iota(jnp.int32, cosv.shape, 1)
        ksum = jnp.zeros(cosv.shape, F32)
        for h in range(MLA_HEADS):
            hs = slice(HEAD_PAD * h, HEAD_PAD * (h + 1))
            d = dq_ref[:, hs]
            dqo_ref[:, hs] = ((d * cosv + _rope_swap(d * sinv)) * scale).astype(dqo_ref.dtype)
            ksum = ksum + dk_ref[:, hs]
        ksum = jnp.where((lane >= QK_NOPE) & (lane < QK_NOPE + QK_ROPE), ksum, 0.0)
        un = ksum * cosv + _rope_swap(ksum * sinv)
        dkr_ref[...] = pltpu.roll(un, HEAD_PAD - QK_NOPE, axis=1)

    wide = pl.BlockSpec((tm, MLA_HEADS * HEAD_PAD), lambda i: (i, 0))
    tab = pl.BlockSpec((tm, LANES), lambda i: (i, 0))
    return pl.pallas_call(
        body, grid=(s // tm,), in_specs=[wide, wide, tab, tab], out_specs=(wide, tab),
        out_shape=(SDS((s, MLA_HEADS * HEAD_PAD), BF16), SDS((s, LANES), F32)), name=name,
        compiler_params=_params(("parallel",), 32 << 20))(dqr, dkr, cos, sins)


def _causal_mask(i, j, t):
    row = i * t + lax.broadcasted_iota(jnp.int32, (t, t), 0)
    colm = j * t + lax.broadcasted_iota(jnp.int32, (t, t), 1)
    return row >= colm


def _flash_fwd(q, k, v, *, name):
    s = q.shape[0]
    t = min(s, 256)
    nq = s // t
    npair = MLA_HEADS // 2

    def body(q_ref, k_ref, v_ref, o_ref, lse_ref):
        i = pl.program_id(1)
        outs, lses = [], []
        for e in range(2):
            qe = q_ref[:, HEAD_PAD * e:HEAD_PAD * (e + 1)]

            def kv_step(j, carry, e=e, qe=qe):
                m, l, acc = carry
                rows = pl.ds(pl.multiple_of(j * t, t), t)
                sc = _dot_nt(qe, k_ref[rows, HEAD_PAD * e:HEAD_PAD * (e + 1)])
                sc = jnp.where(_causal_mask(i, j, t), sc, NEG)
                m_new = jnp.maximum(m, jnp.max(sc, axis=1, keepdims=True))
                pr = jnp.exp(sc - m_new)
                alpha = jnp.exp(m - m_new)
                l = alpha * l + jnp.sum(pr, axis=1, keepdims=True)
                acc = alpha * acc + _dot(pr, v_ref[rows, V_DIM * e:V_DIM * (e + 1)])
                return m_new, l, acc

            init = (jnp.full((t, 1), NEG, F32), jnp.zeros((t, 1), F32), jnp.zeros((t, V_DIM), F32))
            m, l, acc = lax.fori_loop(0, i + 1, kv_step, init)
            outs.append(acc / l)
            lses.append(jnp.broadcast_to(m + jnp.log(l), (t, V_DIM)))
        o_ref[...] = jnp.concatenate(outs, axis=1)
        lse_ref[0] = jnp.concatenate(lses, axis=1)

    return pl.pallas_call(
        body, grid=(npair, nq),
        in_specs=[pl.BlockSpec((t, 2 * HEAD_PAD), lambda hp, i: (i, hp)), pl.BlockSpec((s, 2 * HEAD_PAD), lambda hp, i: (0, hp)),
                  pl.BlockSpec((s, 2 * V_DIM), lambda hp, i: (0, hp))],
        out_specs=(pl.BlockSpec((t, 2 * V_DIM), lambda hp, i: (i, hp)), pl.BlockSpec((1, t, LANES), lambda hp, i: (hp, i, 0))),
        out_shape=(SDS((s, MLA_HEADS * V_DIM), F32), SDS((npair, s, LANES), F32)), name=name,
        compiler_params=_params(("parallel", "parallel"), 24 << 20))(q, k, v)


def _flash_bwd(q, k, v, o, lse, do, *, name):
    s = q.shape[0]
    t = min(s, 256)
    nq = s // t
    npair = MLA_HEADS // 2

    def body(q_ref, k_ref, v_ref, o_ref, lse_ref, do_ref, dq_ref, dk_ref, dv_ref):
        j = pl.program_id(1)

        @pl.when(j == 0)
        def _():
            dq_ref[...] = jnp.zeros_like(dq_ref)

        dks, dvs = [], []
        for e in range(2):
            qs = slice(HEAD_PAD * e, HEAD_PAD * (e + 1))
            vs = slice(V_DIM * e, V_DIM * (e + 1))
            ke = k_ref[:, qs]
            ve = v_ref[:, vs]

            def q_step(i, carry, qs=qs, vs=vs, ke=ke, ve=ve):
                dk, dv = carry
                rows = pl.ds(pl.multiple_of(i * t, t), t)
                qi = q_ref[rows, qs]
                doi = do_ref[rows, vs]
                delta = jnp.sum(doi * o_ref[rows, vs], axis=1, keepdims=True)
                lse_i = lse_ref[0, rows, vs][:, 0:1]
                sc = _dot_nt(qi, ke)
                pr = jnp.where(_causal_mask(i, j, t), jnp.exp(sc - lse_i), 0.0)
                dv = dv + _dot_tn(pr, doi)
                dsc = (pr * (_dot_nt(doi, ve) - delta)).astype(BF16)
                dk = dk + _dot_tn(dsc, qi)
                dq_ref[rows, qs] += _dot(dsc, ke)
                return dk, dv

            dk, dv = lax.fori_loop(j, nq, q_step, (jnp.zeros((t, HEAD_PAD), F32), jnp.zeros((t, V_DIM), F32)))
            dks.append(dk)
            dvs.append(dv)
        dk_ref[...] = jnp.concatenate(dks, axis=1)
        dv_ref[...] = jnp.concatenate(dvs, axis=1)

    full_q = pl.BlockSpec((s, 2 * HEAD_PAD), lambda hp, j: (0, hp))
    full_v = pl.BlockSpec((s, 2 * V_DIM), lambda hp, j: (0, hp))
    blk_k = pl.BlockSpec((t, 2 * HEAD_PAD), lambda hp, j: (j, hp))
    blk_v = pl.BlockSpec((t, 2 * V_DIM), lambda hp, j: (j, hp))
    return pl.pallas_call(
        body, grid=(npair, nq),
        in_specs=[full_q, blk_k, blk_v, full_v, pl.BlockSpec((1, s, LANES), lambda hp, j: (hp, 0, 0)), full_v],
        out_specs=(full_q, blk_k, blk_v),
        out_shape=(SDS((s, MLA_HEADS * HEAD_PAD), F32), SDS((s, MLA_HEADS * HEAD_PAD), F32), SDS((s, MLA_HEADS * V_DIM), F32)),
        name=name, compiler_params=_params(("parallel", "arbitrary"), 32 << 20))(q, k, v, o, lse, do)


def _mem_attn_fwd(q, k, v, *, name):
    s = q.shape[0]
    tm = min(s, 512)
    ml = k.shape[0]
    scale = MEM_HEAD_DIM ** -0.5

    def body(q_ref, k_ref, v_ref, o_ref):
        for h in range(MEM_HEADS):
            hs = slice(MEM_HEAD_DIM * h, MEM_HEAD_DIM * (h + 1))
            sc = _dot_nt(q_ref[:, hs], k_ref[:, hs]) * scale
            pr = jnp.exp(sc - jnp.max(sc, axis=1, keepdims=True))
            pr = pr / jnp.sum(pr, axis=1, keepdims=True)
            o_ref[:, hs] = _dot(pr, v_ref[:, hs]).astype(o_ref.dtype)

    blk = pl.BlockSpec((tm, D_MODEL), lambda i: (i, 0))
    kv = pl.BlockSpec((ml, D_MODEL), lambda i: (0, 0))
    return pl.pallas_call(body, grid=(s // tm,), in_specs=[blk, kv, kv], out_specs=blk,
                          out_shape=SDS((s, D_MODEL), BF16), name=name,
                          compiler_params=_params(("parallel",), 24 << 20))(q, k, v)


def _mem_attn_bwd(q, k, v, do, *, name):
    s = q.shape[0]
    tm = min(s, 512)
    ml = k.shape[0]
    scale = MEM_HEAD_DIM ** -0.5

    def body(q_ref, k_ref, v_ref, do_ref, dq_ref, dk_ref, dv_ref):
        @pl.when(pl.program_id(0) == 0)
        def _():
            dk_ref[...] = jnp.zeros_like(dk_ref)
            dv_ref[...] = jnp.zeros_like(dv_ref)

        for h in range(MEM_HEADS):
            hs = slice(MEM_HEAD_DIM * h, MEM_HEAD_DIM * (h + 1))
            qh, kh, vh, doh = q_ref[:, hs], k_ref[:, hs], v_ref[:, hs], do_ref[:, hs]
            sc = _dot_nt(qh, kh) * scale
            pr = jnp.exp(sc - jnp.max(sc, axis=1, keepdims=True))
            pr = pr / jnp.sum(pr, axis=1, keepdims=True)
            dp = _dot_nt(doh, vh)
            dsc = pr * (dp - jnp.sum(pr * dp, axis=1, keepdims=True)) * scale
            dq_ref[:, hs] = _dot(dsc, kh).astype(dq_ref.dtype)
            dk_ref[:, hs] += _dot_tn(dsc, qh)
            dv_ref[:, hs] += _dot_tn(pr, doh)

    blk = pl.BlockSpec((tm, D_MODEL), lambda i: (i, 0))
    kv = pl.BlockSpec((ml, D_MODEL), lambda i: (0, 0))
    return pl.pallas_call(body, grid=(s // tm,), in_specs=[blk, kv, kv, blk], out_specs=(blk, kv, kv),
                          out_shape=(SDS((s, D_MODEL), BF16), SDS((ml, D_MODEL), F32), SDS((ml, D_MODEL), F32)), name=name,
                          compiler_params=_params(("arbitrary",), 32 << 20))(q, k, v, do)


def _layer_fwd(x0, mem, cos, sins, w, sp, li):
    n = lambda t: f"l{li}_{t}"
    sv = {"x0": x0}
    h = _rms_fwd(x0, sp["norm_mix"], name=n("mix_norm"))
    proj = _mm(h, w["w_in"], name=n("mix_proj"))
    xbc = _ssm_conv_fwd(proj, w["ssm_conv_w"], sp["ssm_conv_b"], name=n("ssm_conv"))
    y, pstates = _ssd_fwd(xbc, proj, sp["dt_bias"], sp["a_log"], sp["d_skip"], name=n("ssd"))
    y_ssm = _gated_rms_fwd(y, proj, sp["ssm_norm"], name=n("ssm_gate"))
    cqn = _rms_fwd(proj, sp["q_norm"], col=(Q_LORA, P_CQ // Q_LORA), name=n("q_norm"))
    ckvn = _rms_fwd(proj, sp["kv_norm"], col=(KV_LORA, P_CKV // KV_LORA), name=n("kv_norm"))
    q = _mm(cqn, w["w_uq"], name=n("uq"))
    kn = _mm(ckvn, w["w_uk"], name=n("uk"))
    v = _mm(ckvn, w["w_uv"], out_dtype=BF16, name=n("uv"))
    qr, kr = _mla_prep(q, kn, proj, cos, sins, name=n("rope"))
    att, lse = _flash_fwd(qr, kr, v, name=n("flash"))
    y_att = _rms_fwd(att, sp["attn_out_norm"], name=n("att_norm"))
    x1 = _mm(y_ssm, w["w_out_a"], res=x0, name=n("out_a"))
    x1 = _mm(y_att, w["w_out_b"], res=x1, name=n("out_b"))
    sv.update(h=h, proj=proj, xbc=xbc, y=y, pstates=pstates, y_ssm=y_ssm, cqn=cqn, ckvn=ckvn, qr=qr, kr=kr, v=v,
              att=att, lse=lse, y_att=y_att, x1=x1)
    hq = _rms_fwd(x1, sp["norm_mem_q"], name=n("memq_norm"))
    hm = _rms_fwd(mem, sp["norm_mem_kv"], name=n("memkv_norm"))
    mq = _mm(hq, w["w_mq"], out_dtype=BF16, name=n("mq"))
    mk = _mm(hm, w["w_mk"], out_dtype=BF16, name=n("mk"))
    mv = _mm(hm, w["w_mv"], out_dtype=BF16, name=n("mv"))
    mo = _mem_attn_fwd(mq, mk, mv, name=n("mem_attn"))
    x2 = _mm(mo, w["w_mo"], res=x1, name=n("mo"))
    sv.update(hq=hq, hm=hm, mq=mq, mk=mk, mv=mv, mo=mo, x2=x2)
    hf = _rms_fwd(x2, sp["norm_ffn"], name=n("ffn_norm"))
    up_g = _mm(hf, w["w_up_g"], name=n("up_g"))
    up_v = _mm(hf, w["w_up_v"], name=n("up_v"))
    act = _ffn_conv_fwd(up_g, up_v, w["ffn_conv_w"], sp["ffn_conv_b"], name=n("ffn_conv"))
    x3 = _mm(act, w["w_down"], res=x2, name=n("down"))
    sv.update(hf=hf, up_g=up_g, up_v=up_v, act=act)
    return x3, sv


def _layer_bwd(dx3, mem, cos, sins, w, sp, sv, li):
    n = lambda t: f"l{li}_b_{t}"
    g = {}
    dact = _mm(dx3, w["w_down"], tb=True, out_dtype=BF16, name=n("down_dx"))
    g["w_down"] = _mm(sv["act"], dx3, ta=True, name=n("down_dw"))
    dup_g, dup_v, g["ffn_conv_w"], g["ffn_conv_b"] = _ffn_conv_bwd(
        sv["up_g"], sv["up_v"], w["ffn_conv_w"], sp["ffn_conv_b"], dact, name=n("ffn_conv"))
    dhf = _mm(dup_g, w["w_up_g"], tb=True, name=n("upg_dx"))
    dhf = _mm(dup_v, w["w_up_v"], tb=True, res=dhf, name=n("upv_dx"))
    g["w_up_g"] = _mm(sv["hf"], dup_g, ta=True, name=n("upg_dw"))
    g["w_up_v"] = _mm(sv["hf"], dup_v, ta=True, name=n("upv_dw"))
    dx2, g["norm_ffn"] = _rms_bwd(sv["x2"], sp["norm_ffn"], dhf, dx3, name=n("ffn_norm"))
    dmo = _mm(dx2, w["w_mo"], tb=True, out_dtype=BF16, name=n("mo_dx"))
    g["w_mo"] = _mm(sv["mo"], dx2, ta=True, name=n("mo_dw"))
    dmq, dmk, dmv = _mem_attn_bwd(sv["mq"], sv["mk"], sv["mv"], dmo, name=n("mem_attn"))
    dhq = _mm(dmq, w["w_mq"], tb=True, name=n("mq_dx"))
    g["w_mq"] = _mm(sv["hq"], dmq, ta=True, name=n("mq_dw"))
    dhm = _mm(dmk, w["w_mk"], tb=True, name=n("mk_dx"))
    dhm = _mm(dmv, w["w_mv"], tb=True, res=dhm, name=n("mv_dx"))
    g["w_mk"] = _mm(sv["hm"], dmk, ta=True, name=n("mk_dw"))
    g["w_mv"] = _mm(sv["hm"], dmv, ta=True, name=n("mv_dw"))
    dx1, g["norm_mem_q"] = _rms_bwd(sv["x1"], sp["norm_mem_q"], dhq, dx2, name=n("memq_norm"))
    _, g["norm_mem_kv"] = _rms_bwd(mem, sp["norm_mem_kv"], dhm, name=n("memkv_norm"))
    dy_ssm = _mm(dx1, w["w_out_a"], tb=True, name=n("outa_dx"))
    dy_att = _mm(dx1, w["w_out_b"], tb=True, name=n("outb_dx"))
    g["w_out_a"] = _mm(sv["y_ssm"], dx1, ta=True, name=n("outa_dw"))
    g["w_out_b"] = _mm(sv["y_att"], dx1, ta=True, name=n("outb_dw"))
    datt, g["attn_out_norm"] = _rms_bwd(sv["att"], sp["attn_out_norm"], dy_att, name=n("att_norm"))
    dqr, dkr, dv = _flash_bwd(sv["qr"], sv["kr"], sv["v"], sv["att"], sv["lse"], datt, name=n("flash"))
    dq, dkrope = _mla_prep_bwd(dqr, dkr, cos, sins, name=n("rope"))
    g["w_uq"] = _mm(sv["cqn"], dq, ta=True, name=n("uq_dw"))
    dcqn = _mm(dq, w["w_uq"], tb=True, name=n("uq_dx"))
    g["w_uk"] = _mm(sv["ckvn"], dkr, ta=True, name=n("uk_dw"))
    g["w_uv"] = _mm(sv["ckvn"], dv, ta=True, name=n("uv_dw"))
    dckvn = _mm(dkr, w["w_uk"], tb=True, name=n("uk_dx"))
    dckvn = _mm(dv, w["w_uv"], tb=True, res=dckvn, name=n("uv_dx"))
    proj = sv["proj"]
    dcq, g["q_norm"] = _rms_bwd(proj, sp["q_norm"], dcqn, col=(Q_LORA, P_CQ // Q_LORA), name=n("q_norm"))
    dckv, g["kv_norm"] = _rms_bwd(proj, sp["kv_norm"], dckvn, col=(KV_LORA, P_CKV // KV_LORA), name=n("kv_norm"))
    dy, dz, g["ssm_norm"] = _gated_rms_bwd(sv["y"], proj, sp["ssm_norm"], dy_ssm, name=n("ssm_gate"))
    dxbc, ddt, g["a_log"], g["d_skip"], g["dt_bias"] = _ssd_bwd(
        sv["xbc"], proj, sp["dt_bias"], sp["a_log"], sp["d_skip"], sv["pstates"], dy, name=n("ssd"))
    dxbc_pre, g["ssm_conv_w"], g["ssm_conv_b"] = _ssm_conv_bwd(proj, w["ssm_conv_w"], sp["ssm_conv_b"], dxbc, name=n("ssm_conv"))
    s = proj.shape[0]
    dproj = jnp.concatenate([dxbc_pre, dz.astype(BF16), dcq.astype(BF16), ddt.astype(BF16), dckv.astype(BF16),
                             dkrope.astype(BF16), jnp.zeros((s, P_IN - P_KR - LANES), BF16)], axis=1)
    dh = _mm(dproj, w["w_in"], tb=True, name=n("proj_dx"))
    g["w_in"] = _mm(sv["h"], dproj, ta=True, name=n("proj_dw"))
    dx0, g["norm_mix"] = _rms_bwd(sv["x0"], sp["norm_mix"], dh, dx1, name=n("mix_norm"))
    return dx0, g


BIG = (("w_in", (1024, 940), 1), ("w_uq", (384, 384), 1), ("w_ukv", (256, 512), 1), ("w_out", (512, 1024), 0),
       ("w_mq", (256, 1024), 0), ("w_mk", (256, 1024), 0), ("w_mv", (256, 1024), 0), ("w_mo", (256, 1024), 0),
       ("w_up", (1024, 1408), 1), ("w_down", (704, 1024), 0))
CONVS = (("ssm_conv_w", (4, 512), 1), ("ffn_conv_w", (3, 1408), 1))
SMALL = (("norm_mix", 1024), ("ssm_conv_b", 2048), ("dt_bias", 16), ("a_log", 16), ("d_skip", 16), ("ssm_norm", 1024),
         ("q_norm", 384), ("kv_norm", 256), ("attn_out_norm", 1024), ("norm_mem_q", 1024), ("norm_mem_kv", 1024),
         ("norm_ffn", 1024), ("ffn_conv_b", 5632))
PACK_COLS = 1024
PACK_ROW_BLOCK = 256


def _pack_len():
    nb = sum(DEPTH * math.prod(shp) for _, shp, _ in BIG) + sum(2 * DEPTH * math.prod(shp) for _, shp, _ in CONVS)
    unit = PACK_COLS * PACK_ROW_BLOCK
    return -(-nb // unit) * unit


def _pack(parts, conv_parts, dtype):
    flat = [p.astype(dtype).reshape(-1) for p in parts]
    for a, b in conv_parts:
        flat += [a.astype(dtype).reshape(-1), b.astype(dtype).reshape(-1)]
    n = sum(f.shape[0] for f in flat)
    flat.append(jnp.zeros((_pack_len() - n,), dtype))
    return jnp.concatenate(flat).reshape(-1, PACK_COLS)


def _unpack(buf):
    lead = buf.shape[:-2]
    flat = buf.reshape(lead + (-1,))
    out, off = {}, 0
    for name, shp, _ in BIG:
        n = DEPTH * math.prod(shp)
        out[name] = flat[..., off:off + n].reshape(lead + (DEPTH,) + shp)
        off += n
    for name, shp, _ in CONVS:
        n = DEPTH * math.prod(shp)
        out[name] = (flat[..., off:off + n].reshape(lead + (DEPTH,) + shp),
                     flat[..., off + n:off + 2 * n].reshape(lead + (DEPTH,) + shp))
        off += 2 * n
    return out


def _join_shards(t, axis):
    return jnp.concatenate([t[j] for j in range(N_CHIPS)], axis=1 + axis)


def _split_shards(t, axis):
    return jnp.stack(jnp.split(t, N_CHIPS, axis=1 + axis), axis=0)


def _pad_cols(t, n):
    return jnp.pad(t, ((0, 0),) * (t.ndim - 1) + ((0, n - t.shape[-1]),))


def _w_in_to_padded(t):
    z, xbc, dt, cq, ckv, kr = jnp.split(t, (1024, 3072, 3088, 3472, 3728), axis=-1)
    return jnp.concatenate([xbc, z, cq, _pad_cols(dt, LANES), ckv, _pad_cols(kr, LANES + P_IN - P_KR - LANES)], axis=-1)


def _w_in_from_padded(t):
    return jnp.concatenate([t[..., P_Z:P_Z + 1024], t[..., P_XBC:P_XBC + 2048], t[..., P_DT:P_DT + SSM_HEADS],
                            t[..., P_CQ:P_CQ + Q_LORA], t[..., P_CKV:P_CKV + KV_LORA], t[..., P_KR:P_KR + QK_ROPE]], axis=-1)


def _layer_weights(full, li):
    w = {}
    w["w_in"] = _w_in_to_padded(full["w_in"][li])
    uq = full["w_uq"][li].reshape(Q_LORA, MLA_HEADS, QK_NOPE + QK_ROPE)
    w["w_uq"] = _pad_cols(uq, HEAD_PAD).reshape(Q_LORA, MLA_HEADS * HEAD_PAD)
    ukv = full["w_ukv"][li].reshape(KV_LORA, MLA_HEADS, QK_NOPE + V_DIM)
    w["w_uk"] = _pad_cols(ukv[..., :QK_NOPE], HEAD_PAD).reshape(KV_LORA, MLA_HEADS * HEAD_PAD)
    w["w_uv"] = ukv[..., QK_NOPE:].reshape(KV_LORA, MLA_HEADS * V_DIM)
    w["w_out_a"] = full["w_out"][li][:D_SSM]
    w["w_out_b"] = full["w_out"][li][D_SSM:]
    for k in ("w_mq", "w_mk", "w_mv", "w_mo", "w_down"):
        w[k] = full[k][li]
    w["w_up_g"] = full["w_up"][li][:, :D_FF]
    w["w_up_v"] = full["w_up"][li][:, D_FF:]
    w["ssm_conv_w"] = full["ssm_conv_w"][li]
    w["ffn_conv_w"] = full["ffn_conv_w"][li]
    return w


def _layer_grads_to_full(g):
    out = {}
    out["w_in"] = _w_in_from_padded(g["w_in"])
    out["w_uq"] = g["w_uq"].reshape(Q_LORA, MLA_HEADS, HEAD_PAD)[..., :QK_NOPE + QK_ROPE].reshape(Q_LORA, -1)
    uk = g["w_uk"].reshape(KV_LORA, MLA_HEADS, HEAD_PAD)[..., :QK_NOPE]
    uv = g["w_uv"].reshape(KV_LORA, MLA_HEADS, V_DIM)
    out["w_ukv"] = jnp.concatenate([uk, uv], axis=-1).reshape(KV_LORA, -1)
    out["w_out"] = jnp.concatenate([g["w_out_a"], g["w_out_b"]], axis=0)
    for k in ("w_mq", "w_mk", "w_mv", "w_mo", "w_down", "ssm_conv_w", "ffn_conv_w"):
        out[k] = g[k]
    out["w_up"] = jnp.concatenate([g["w_up_g"], g["w_up_v"]], axis=1)
    for k, nel in SMALL:
        v = g[k]
        out[k] = v[0, :nel] if v.ndim == 2 else v
    return out


def _chip_peers(x, y):
    return [(1 - x, y), (x, 1 - y), (1 - x, 1 - y)]


def _plane_exchange(src, *, per_dest, name):
    shape = src.shape[1:] if per_dest else src.shape

    def body(src_ref, out_ref, send_sems, recv_sems, local_sem):
        x, y, c = lax.axis_index("x"), lax.axis_index("y"), lax.axis_index("c")
        me = 2 * x + y
        mine = pltpu.make_async_copy(src_ref.at[me] if per_dest else src_ref, out_ref.at[me], local_sem)
        mine.start()
        sends = []
        for k, (px, py) in enumerate(_chip_peers(x, y)):
            chip = 2 * px + py
            cp = pltpu.make_async_remote_copy(
                src_ref=src_ref.at[chip] if per_dest else src_ref, dst_ref=out_ref.at[me],
                send_sem=send_sems.at[k], recv_sem=recv_sems.at[k], device_id=(px, py, c), device_id_type=MESH)
            cp.start()
            sends.append(cp)
        for k, (px, py) in enumerate(_chip_peers(x, y)):
            chip = 2 * px + py
            pltpu.make_async_remote_copy(
                src_ref=src_ref.at[chip] if per_dest else src_ref, dst_ref=out_ref.at[chip],
                send_sem=send_sems.at[k], recv_sem=recv_sems.at[k], device_id=(px, py, c), device_id_type=MESH).wait_recv()
        for cp in sends:
            cp.wait_send()
        mine.wait()

    any_spec = pl.BlockSpec(memory_space=pl.ANY)
    return pl.pallas_call(
        body, in_specs=[any_spec], out_specs=any_spec, out_shape=SDS((N_CHIPS,) + tuple(shape), src.dtype),
        scratch_shapes=[pltpu.SemaphoreType.DMA((3,)), pltpu.SemaphoreType.DMA((3,)), pltpu.SemaphoreType.DMA],
        name=name)(src)


def _sibling_exchange(src, *, name):
    def body(src_ref, out_ref, send_sem, recv_sem):
        x, y, c = lax.axis_index("x"), lax.axis_index("y"), lax.axis_index("c")
        cp = pltpu.make_async_remote_copy(src_ref=src_ref, dst_ref=out_ref, send_sem=send_sem, recv_sem=recv_sem,
                                          device_id=(x, y, 1 - c), device_id_type=MESH)
        cp.start()
        cp.wait()

    any_spec = pl.BlockSpec(memory_space=pl.ANY)
    return pl.pallas_call(body, in_specs=[any_spec], out_specs=any_spec, out_shape=SDS(src.shape, src.dtype),
                          scratch_shapes=[pltpu.SemaphoreType.DMA, pltpu.SemaphoreType.DMA], name=name)(src)


def _all_gather8(src, *, name):
    def body(src_ref, out_ref, send_sems, recv_sems, local_sem):
        x, y, c = lax.axis_index("x"), lax.axis_index("y"), lax.axis_index("c")
        me = 4 * x + 2 * y + c
        mine = pltpu.make_async_copy(src_ref, out_ref.at[me], local_sem)
        mine.start()

        def peer(k):
            return (x ^ (k >> 2 & 1), y ^ (k >> 1 & 1), c ^ (k & 1))

        sends = []
        for k in range(1, N_DEV):
            cp = pltpu.make_async_remote_copy(src_ref=src_ref, dst_ref=out_ref.at[me], send_sem=send_sems.at[k - 1],
                                              recv_sem=recv_sems.at[k - 1], device_id=peer(k), device_id_type=MESH)
            cp.start()
            sends.append(cp)
        for k in range(1, N_DEV):
            px, py, pc = peer(k)
            pltpu.make_async_remote_copy(src_ref=src_ref, dst_ref=out_ref.at[4 * px + 2 * py + pc],
                                         send_sem=send_sems.at[k - 1], recv_sem=recv_sems.at[k - 1],
                                         device_id=peer(k), device_id_type=MESH).wait_recv()
        for cp in sends:
            cp.wait_send()
        mine.wait()

    any_spec = pl.BlockSpec(memory_space=pl.ANY)
    return pl.pallas_call(
        body, in_specs=[any_spec], out_specs=any_spec, out_shape=SDS((N_DEV,) + src.shape, src.dtype),
        scratch_shapes=[pltpu.SemaphoreType.DMA((N_DEV - 1,)), pltpu.SemaphoreType.DMA((N_DEV - 1,)), pltpu.SemaphoreType.DMA],
        name=name)(src)


def _sum4(buf, *, name):
    _, r, c = buf.shape

    def body(b_ref, o_ref):
        o_ref[...] = ((b_ref[0].astype(F32) + b_ref[1].astype(F32)) + b_ref[2].astype(F32)) + b_ref[3].astype(F32)

    return pl.pallas_call(
        body, grid=(r // PACK_ROW_BLOCK,), in_specs=[pl.BlockSpec((N_CHIPS, PACK_ROW_BLOCK, c), lambda i: (0, i, 0))],
        out_specs=pl.BlockSpec((PACK_ROW_BLOCK, c), lambda i: (i, 0)), out_shape=SDS((r, c), F32), name=name,
        compiler_params=_params(("parallel",)))(buf)


def _adam_terms(w, g, m, v):
    m = ADAM_B1 * m + (1.0 - ADAM_B1) * g
    v = ADAM_B2 * v + (1.0 - ADAM_B2) * (g * g)
    m_hat = m / (1.0 - ADAM_B1 ** ADAM_STEP)
    v_hat = v / (1.0 - ADAM_B2 ** ADAM_STEP)
    delta = -ADAM_LR * (m_hat / (jnp.sqrt(v_hat) + ADAM_EPS) + ADAM_WD * w)
    return delta, m, v


def _adamw_packed(g_lo, g_hi, w, m, v, *, name):
    r, c = w.shape

    def body(ga_ref, gb_ref, w_ref, m_ref, v_ref, g_ref, d_ref, nm_ref, nv_ref):
        g = ga_ref[...] + gb_ref[...]
        delta, mn, vn = _adam_terms(w_ref[...], g, m_ref[...], v_ref[...])
        g_ref[...] = g
        d_ref[...] = delta
        nm_ref[...] = mn
        nv_ref[...] = vn

    blk = pl.BlockSpec((PACK_ROW_BLOCK, c), lambda i: (i, 0))
    shp = SDS((r, c), F32)
    return pl.pallas_call(body, grid=(r // PACK_ROW_BLOCK,), in_specs=[blk] * 5, out_specs=(blk,) * 4,
                          out_shape=(shp,) * 4, name=name, compiler_params=_params(("parallel",), 32 << 20))(g_lo, g_hi, w, m, v)


def _adamw_small(g8, w, m, v, *, name):
    n = w.shape[1]

    def body(g8_ref, w_ref, m_ref, v_ref, g_ref, d_ref, nm_ref, nv_ref):
        g = g8_ref[0]
        for k in range(1, N_DEV):
            g = g + g8_ref[k]
        delta, mn, vn = _adam_terms(w_ref[...], g, m_ref[...], v_ref[...])
        g_ref[...] = g
        d_ref[...] = delta
        nm_ref[...] = mn
        nv_ref[...] = vn

    shp = SDS((1, n), F32)
    return pl.pallas_call(body, out_shape=(shp,) * 4, name=name, compiler_params=_params(None, 24 << 20))(g8, w, m, v)


def _rope_tables(positions):
    half = QK_ROPE // 2
    inv_freq = 1.0 / (ROPE_THETA ** (jnp.arange(0, QK_ROPE, 2, dtype=F32) / QK_ROPE))
    ang = positions.astype(F32)[:, None] * inv_freq
    c, s = jnp.cos(ang), jnp.sin(ang)
    n = positions.shape[0]
    pad = jnp.zeros((n, HEAD_PAD - QK_NOPE - QK_ROPE), F32)
    cos = jnp.concatenate([jnp.ones((n, QK_NOPE), F32), c, c, pad], axis=1)
    sins = jnp.concatenate([jnp.zeros((n, QK_NOPE), F32), -s, s, pad], axis=1)
    assert half * 2 == QK_ROPE
    return cos, sins


def _pad_lanes(v):
    return _pad_cols(v.reshape(1, -1), LANES)


def _local_step(x, mem, positions, full, small, final_norm, loss_target):
    cos, sins = _rope_tables(positions)
    saved, ws, sps = [], [], []
    h = x
    for li in range(DEPTH):
        w = _layer_weights(full, li)
        sp = {k: small[k][li] for k, _ in SMALL}
        for k in ("dt_bias", "a_log", "d_skip"):
            sp[k] = _pad_lanes(sp[k])
        h, sv = _layer_fwd(h, mem, cos, sins, w, sp, li)
        saved.append(sv)
        ws.append(w)
        sps.append(sp)
    loss, dh, g_final = _final_loss(h, final_norm, loss_target, name="final_loss")
    grads = [None] * DEPTH
    for li in reversed(range(DEPTH)):
        dh, g = _layer_bwd(dh, mem, cos, sins, ws[li], sps[li], saved[li], li)
        grads[li] = _layer_grads_to_full(g)
    return loss, dh, grads, g_final


def kernel(x, mem, positions, norm_mix, w_in, ssm_conv_w, ssm_conv_b, dt_bias, a_log, d_skip, ssm_norm, q_norm, w_uq, kv_norm, w_ukv, attn_out_norm, w_out, norm_mem_q, norm_mem_kv, w_mq, w_mk, w_mv, w_mo, norm_ffn, w_up, ffn_conv_w, ffn_conv_b, w_down, final_norm, loss_target, m_norm_mix, m_w_in, m_ssm_conv_w, m_ssm_conv_b, m_dt_bias, m_a_log, m_d_skip, m_ssm_norm, m_q_norm, m_w_uq, m_kv_norm, m_w_ukv, m_attn_out_norm, m_w_out, m_norm_mem_q, m_norm_mem_kv, m_w_mq, m_w_mk, m_w_mv, m_w_mo, m_norm_ffn, m_w_up, m_ffn_conv_w, m_ffn_conv_b, m_w_down, m_final_norm, v_norm_mix, v_w_in, v_ssm_conv_w, v_ssm_conv_b, v_dt_bias, v_a_log, v_d_skip, v_ssm_norm, v_q_norm, v_w_uq, v_kv_norm, v_w_ukv, v_attn_out_norm, v_w_out, v_norm_mem_q, v_norm_mem_kv, v_w_mq, v_w_mk, v_w_mv, v_w_mo, v_norm_ffn, v_w_up, v_ffn_conv_w, v_ffn_conv_b, v_w_down, v_final_norm):
    args = dict(locals())
    names = ["norm_mix", "w_in", "ssm_conv_w", "ssm_conv_b", "dt_bias", "a_log", "d_skip", "ssm_norm", "q_norm", "w_uq",
             "kv_norm", "w_ukv", "attn_out_norm", "w_out", "norm_mem_q", "norm_mem_kv", "w_mq", "w_mk", "w_mv", "w_mo",
             "norm_ffn", "w_up", "ffn_conv_w", "ffn_conv_b", "w_down", "final_norm"]
    wts = {k: args[k] for k in names}
    mom = {k: args["m_" + k] for k in names}
    var = {k: args["v_" + k] for k in names}

    def hi_lo(t):
        hi = t.astype(BF16)
        return hi, (t - hi.astype(F32)).astype(BF16)

    wb = _pack([wts[k] for k, _, _ in BIG], [hi_lo(wts[k]) for k, _, _ in CONVS], BF16)
    gathered = _unpack(_plane_exchange(wb, per_dest=False, name="gather_weights"))
    full = {k: _join_shards(gathered[k], ax) for k, _, ax in BIG}
    for k, _, ax in CONVS:
        hi, lo = gathered[k]
        full[k] = _join_shards(hi.astype(F32) + lo.astype(F32), ax)
    small = {k: wts[k] for k, _ in SMALL}

    loss, grad_x, grads, g_final = _local_step(x[0], mem[0], positions[0], full, small, wts["final_norm"], loss_target[0])
    loss = lax.psum(loss, ("x", "y", "c"))

    stacked = {k: jnp.stack([grads[li][k] for li in range(DEPTH)]) for k in grads[0]}
    zeros_like_conv = lambda t: jnp.zeros_like(t)
    gb = jnp.stack([
        _pack([_split_shards(stacked[k], ax)[j] for k, _, ax in BIG],
              [(_split_shards(stacked[k], ax)[j], zeros_like_conv(_split_shards(stacked[k], ax)[j])) for k, _, ax in CONVS], BF16)
        for j in range(N_CHIPS)])
    plane = _sum4(_plane_exchange(gb, per_dest=True, name="scatter_grads"), name="sum_plane")
    other = _sibling_exchange(plane, name="swap_cores")
    conv_pad = lambda d: [(d[k], jnp.zeros_like(d[k])) for k, _, _ in CONVS]
    packed = _adamw_packed(plane, other, _pack([wts[k] for k, _, _ in BIG], conv_pad(wts), F32),
                           _pack([mom[k] for k, _, _ in BIG], conv_pad(mom), F32),
                           _pack([var[k] for k, _, _ in BIG], conv_pad(var), F32), name="adamw_matrices")
    big_out = [_unpack(p) for p in packed]
    for d in big_out:
        for k, _, _ in CONVS:
            d[k] = d[k][0]

    def pack_small(d, fin):
        flat = [d[k].reshape(-1) for k, _ in SMALL] + [fin.reshape(-1)]
        n = sum(f.shape[0] for f in flat)
        return jnp.concatenate(flat + [jnp.zeros((-n % PACK_COLS,), F32)]).reshape(1, -1)

    gs = pack_small(stacked, g_final)
    g8 = _all_gather8(gs, name="gather_small_grads")
    small_out = _adamw_small(g8, pack_small(wts, wts["final_norm"]), pack_small(mom, mom["final_norm"]),
                             pack_small(var, var["final_norm"]), name="adamw_small")

    def unpack_small(buf):
        out, off = {}, 0
        for k, nel in SMALL:
            out[k] = buf[0, off:off + DEPTH * nel].reshape(DEPTH, nel)
            off += DEPTH * nel
        out["final_norm"] = buf[0, off:off + D_MODEL]
        return out

    small_res = [unpack_small(b) for b in small_out]
    res = []
    for kind in range(4):
        for k in names:
            res.append(small_res[kind][k] if k in small_res[kind] else big_out[kind][k])
    return (loss, grad_x[None], *res)
```

```python
import functools
import math

import jax
import jax.numpy as jnp
from jax import lax
from jax.experimental import pallas as pl
from jax.experimental.pallas import tpu as pltpu

F32 = jnp.float32
BF16 = jnp.bfloat16
HIGHEST = lax.Precision.HIGHEST
SDS = jax.ShapeDtypeStruct
MESH = pl.DeviceIdType.MESH

D_MODEL = 1024
DEPTH = 4
EPS = 1e-6
SSM_HEADS = 16
SSM_HEAD_DIM = 64
D_SSM = 1024
SSM_GROUPS = 4
SSM_STATE = 128
SSM_CONV = 4
SSM_CHUNK = 128
CONV_CH = 2048
MLA_HEADS = 16
QK_NOPE = 64
QK_ROPE = 32
V_DIM = 64
Q_LORA = 384
KV_LORA = 256
ROPE_THETA = 10000.0
MEM_HEADS = 4
MEM_HEAD_DIM = 256
D_FF = 2816
FFN_CONV = 3
D_IN = 3760
ADAM_LR = 0.001
ADAM_B1 = 0.9
ADAM_B2 = 0.999
ADAM_EPS = 1e-08
ADAM_WD = 0.01
ADAM_STEP = 10

LANES = 128
HEAD_PAD = 128
N_CHIPS = 4
N_DEV = 8
VMEM_CAP_MB = 56

P_XBC, P_Z, P_CQ, P_DT, P_CKV, P_KR, P_IN = 0, 2048, 3072, 3456, 3584, 3840, 4096
NEG = -1e30


def _tile(n, pref):
    t = (min(n, pref) // LANES) * LANES
    while t >= LANES:
        if n % t == 0:
            return t
        t -= LANES
    return n


def _params(sem=None, vmem_bytes=None):
    kw = {}
    if sem is not None:
        kw["dimension_semantics"] = sem
    if vmem_bytes is not None:
        kw["vmem_limit_bytes"] = int(min(max(vmem_bytes, 16 << 20), VMEM_CAP_MB << 20))
    return pltpu.CompilerParams(**kw)


def _nbytes(shape, dtype):
    return math.prod(shape) * jnp.dtype(dtype).itemsize


def _mm(a, b, *, ta=False, tb=False, res=None, out_dtype=F32, name, a_col=None, b_lead=(), b_rows=None,
        b_chips=None, o_chips=None):
    if ta:
        k, m = a.shape
    else:
        m, k = (a.shape[0], a.shape[1] if a_col is None else a_col[0])
    rows_b, cols_b = b.shape[-2:]
    row0 = 0
    if b_rows is not None:
        row0, rows_b = b_rows
    nlead = len(b_lead)
    if b_chips is not None:
        assert not tb
        kb, tn, n = rows_b, cols_b, b_chips[1] * cols_b
        b_blk = (None,) * (1 + nlead) + (kb, tn)
        b_map = lambda i, j: (b_chips[0] + j,) + tuple(b_lead) + (0, 0)
    elif tb:
        n, kb = rows_b, cols_b
        tn = _tile(n, 512)
        assert row0 % tn == 0
        b_blk = (None,) * nlead + (tn, kb)
        b_map = lambda i, j: tuple(b_lead) + (j + row0 // tn, 0)
    else:
        kb, n = rows_b, cols_b
        tn = o_chips if o_chips else _tile(n, 512)
        assert row0 % kb == 0
        b_blk = (None,) * nlead + (kb, tn)
        b_map = lambda i, j: tuple(b_lead) + (row0 // kb, j)
    assert k == kb, (a.shape, b.shape, ta, tb, k, kb)
    tm = _tile(m, 512)
    if ta:
        a_blk, a_map = (k, tm), (lambda i, j: (0, i))
    else:
        a_blk, a_map = (tm, k), ((lambda i, j: (i, 0)) if a_col is None else (lambda i, j: (i, a_col[1])))
    if o_chips:
        o_spec = pl.BlockSpec((None, tm, tn), lambda i, j: (j, i, 0))
        o_shape = SDS((n // tn, m, tn), out_dtype)
    else:
        o_spec = pl.BlockSpec((tm, tn), lambda i, j: (i, j))
        o_shape = SDS((m, n), out_dtype)
    dims = (((0 if ta else 1,), (1 if tb else 0,)), ((), ()))
    has_res = res is not None

    def body(*refs):
        a_ref, b_ref = refs[0], refs[1]
        o_ref = refs[-1]
        acc = lax.dot_general(a_ref[...].astype(BF16), b_ref[...].astype(BF16), dims, preferred_element_type=F32)
        if has_res:
            acc = acc + refs[2][...]
        o_ref[...] = acc.astype(o_ref.dtype)

    bb = tuple(d for d in b_blk if d is not None)
    vmem = 2 * (_nbytes(a_blk, a.dtype) + _nbytes(bb, b.dtype) + (2 if has_res else 1) * _nbytes((tm, tn), F32))
    vmem += _nbytes(a_blk, BF16) + _nbytes(bb, BF16) + 2 * _nbytes((tm, tn), F32) + (4 << 20)
    args = (a, b) + ((res,) if has_res else ())
    specs = [pl.BlockSpec(a_blk, a_map), pl.BlockSpec(b_blk, b_map)] + ([o_spec] if has_res else [])
    return pl.pallas_call(body, grid=(m // tm, n // tn), in_specs=specs, out_specs=o_spec, out_shape=o_shape, name=name,
                          compiler_params=_params(("parallel", "parallel"), vmem))(*args)


def _sigmoid(x):
    return 1.0 / (1.0 + jnp.exp(-x))


def _rms_fwd(x, g, *, col=None, name):
    s = x.shape[0]
    w, ci = (x.shape[1], 0) if col is None else col
    tm = min(s, 512)

    def body(x_ref, g_ref, o_ref):
        xv = x_ref[...].astype(F32)
        r = lax.rsqrt(jnp.mean(xv * xv, axis=-1, keepdims=True) + EPS)
        o_ref[...] = (xv * r * g_ref[...]).astype(o_ref.dtype)

    return pl.pallas_call(
        body, grid=(s // tm,),
        in_specs=[pl.BlockSpec((tm, w), lambda i: (i, ci)), pl.BlockSpec((1, w), lambda i: (0, 0))],
        out_specs=pl.BlockSpec((tm, w), lambda i: (i, 0)), out_shape=SDS((s, w), BF16), name=name,
        compiler_params=_params(("parallel",), 10 * tm * w * 4))(x, g.reshape(1, w))


def _rms_bwd(x, g, dy, dres=None, *, col=None, name):
    s = x.shape[0]
    w, ci = (x.shape[1], 0) if col is None else col
    tm = min(s, 512)
    has_res = dres is not None

    def body(*refs):
        x_ref, g_ref, dy_ref = refs[:3]
        dx_ref, dg_ref = refs[-2:]
        xv = x_ref[...].astype(F32)
        dyv = dy_ref[...].astype(F32)
        r = lax.rsqrt(jnp.mean(xv * xv, axis=-1, keepdims=True) + EPS)
        u = dyv * g_ref[...]
        dx = r * u - xv * (r * r * r) * jnp.mean(xv * u, axis=-1, keepdims=True)
        if has_res:
            dx = dx + refs[3][...]
        dx_ref[...] = dx

        @pl.when(pl.program_id(0) == 0)
        def _():
            dg_ref[...] = jnp.zeros_like(dg_ref)

        dg_ref[...] += jnp.sum(dyv * xv * r, axis=0, keepdims=True)

    blk = pl.BlockSpec((tm, w), lambda i: (i, 0))
    specs = [pl.BlockSpec((tm, w), lambda i: (i, ci)), pl.BlockSpec((1, w), lambda i: (0, 0)), blk]
    args = [x, g.reshape(1, w), dy]
    if has_res:
        specs.append(blk)
        args.append(dres)
    dx, dg = pl.pallas_call(
        body, grid=(s // tm,), in_specs=specs,
        out_specs=(blk, pl.BlockSpec((1, w), lambda i: (0, 0))),
        out_shape=(SDS((s, w), F32), SDS((1, w), F32)), name=name,
        compiler_params=_params(("arbitrary",), 16 * tm * w * 4))(*args)
    return dx, dg.reshape(w)


def _gated_rms_fwd(y, proj, g, *, name):
    s, w = y.shape
    tm = min(s, 512)

    def body(y_ref, z_ref, g_ref, o_ref):
        z = z_ref[...]
        t = y_ref[...] * (z * _sigmoid(z))
        r = lax.rsqrt(jnp.mean(t * t, axis=-1, keepdims=True) + EPS)
        o_ref[...] = (t * r * g_ref[...]).astype(o_ref.dtype)

    blk = pl.BlockSpec((tm, w), lambda i: (i, 0))
    return pl.pallas_call(
        body, grid=(s // tm,),
        in_specs=[blk, pl.BlockSpec((tm, w), lambda i: (i, P_Z // w)), pl.BlockSpec((1, w), lambda i: (0, 0))],
        out_specs=blk, out_shape=SDS((s, w), BF16), name=name,
        compiler_params=_params(("parallel",), 14 * tm * w * 4))(y, proj, g.reshape(1, w))


def _gated_rms_bwd(y, proj, g, dout, *, name):
    s, w = y.shape
    tm = min(s, 512)

    def body(y_ref, z_ref, g_ref, do_ref, dy_ref, dz_ref, dg_ref):
        z = z_ref[...]
        yv = y_ref[...]
        dov = do_ref[...]
        sg = _sigmoid(z)
        sz = z * sg
        t = yv * sz
        r = lax.rsqrt(jnp.mean(t * t, axis=-1, keepdims=True) + EPS)
        u = dov * g_ref[...]
        dt = r * u - t * (r * r * r) * jnp.mean(t * u, axis=-1, keepdims=True)
        dy_ref[...] = dt * sz
        dz_ref[...] = (dt * yv * (sg * (1.0 + z * (1.0 - sg)))).astype(dz_ref.dtype)

        @pl.when(pl.program_id(0) == 0)
        def _():
            dg_ref[...] = jnp.zeros_like(dg_ref)

        dg_ref[...] += jnp.sum(dov * t * r, axis=0, keepdims=True)

    blk = pl.BlockSpec((tm, w), lambda i: (i, 0))
    vec = pl.BlockSpec((1, w), lambda i: (0, 0))
    dy, dz, dg = pl.pallas_call(
        body, grid=(s // tm,),
        in_specs=[blk, pl.BlockSpec((tm, w), lambda i: (i, P_Z // w)), vec, blk],
        out_specs=(blk, blk, vec), out_shape=(SDS((s, w), F32), SDS((s, w), BF16), SDS((1, w), F32)), name=name,
        compiler_params=_params(("arbitrary",), 24 * tm * w * 4))(y, proj, g.reshape(1, w), dout)
    return dy, dz, dg.reshape(w)


def _final_loss(x, g, target, *, name):
    s, w = x.shape
    tm = min(s, 512)

    def body(x_ref, g_ref, t_ref, loss_ref, dx_ref, dg_ref):
        xv = x_ref[...]
        gv = g_ref[...]
        r = lax.rsqrt(jnp.mean(xv * xv, axis=-1, keepdims=True) + EPS)
        xn = xv * r
        diff = xn * gv - t_ref[...]
        dy = diff * (1.0 / w)
        u = dy * gv
        dx_ref[...] = r * u - xv * (r * r * r) * jnp.mean(xv * u, axis=-1, keepdims=True)

        @pl.when(pl.program_id(0) == 0)
        def _():
            dg_ref[...] = jnp.zeros_like(dg_ref)
            loss_ref[...] = jnp.zeros_like(loss_ref)

        dg_ref[...] += jnp.sum(dy * xn, axis=0, keepdims=True)
        part = jnp.sum(jnp.sum(diff * diff, axis=1, keepdims=True), axis=0, keepdims=True) * (0.5 / w)
        loss_ref[...] += jnp.broadcast_to(part, loss_ref.shape)

    blk = pl.BlockSpec((tm, w), lambda i: (i, 0))
    vec = pl.BlockSpec((1, w), lambda i: (0, 0))
    loss, dx, dg = pl.pallas_call(
        body, grid=(s // tm,), in_specs=[blk, vec, blk],
        out_specs=(pl.BlockSpec((1, LANES), lambda i: (0, 0)), blk, vec),
        out_shape=(SDS((1, LANES), F32), SDS((s, w), F32), SDS((1, w), F32)), name=name,
        compiler_params=_params(("arbitrary",), 16 * tm * w * 4))(x, g.reshape(1, w), target)
    return loss[0, 0], dx, dg.reshape(w)


def _shift_down(x, k):
    if k == 0:
        return x
    row = lax.broadcasted_iota(jnp.int32, x.shape, 0)
    return jnp.where(row < k, 0.0, pltpu.roll(x, k, axis=0))


def _shift_up(x, k):
    if k == 0:
        return x
    s = x.shape[0]
    row = lax.broadcasted_iota(jnp.int32, x.shape, 0)
    return jnp.where(row >= s - k, 0.0, pltpu.roll(x, s - k, axis=0))


def _conv_pre(x, w, b, kw):
    pre = b
    for j in range(kw):
        pre = pre + w[j:j + 1, :] * _shift_down(x, kw - 1 - j)
    return pre


def _conv_bwd_terms(x, w, dpre, kw):
    dx = jnp.zeros_like(x)
    dws = []
    for j in range(kw):
        dx = dx + w[j:j + 1, :] * _shift_up(dpre, kw - 1 - j)
        dws.append(jnp.sum(dpre * _shift_down(x, kw - 1 - j), axis=0, keepdims=True))
    return dx, jnp.concatenate(dws, axis=0), jnp.sum(dpre, axis=0, keepdims=True)


def _ssm_conv_fwd(proj, w, b, *, name):
    s = proj.shape[0]
    cw = 256

    def body(x_ref, w_ref, b_ref, o_ref):
        pre = _conv_pre(x_ref[...], w_ref[...], b_ref[...], SSM_CONV)
        o_ref[...] = pre * _sigmoid(pre)

    return pl.pallas_call(
        body, grid=(CONV_CH // cw,),
        in_specs=[pl.BlockSpec((s, cw), lambda j: (0, j)), pl.BlockSpec((SSM_CONV, cw), lambda j: (0, j)),
                  pl.BlockSpec((1, cw), lambda j: (0, j))],
        out_specs=pl.BlockSpec((s, cw), lambda j: (0, j)), out_shape=SDS((s, CONV_CH), F32), name=name,
        compiler_params=_params(("parallel",), 12 * s * cw * 4))(proj, w, b.reshape(1, CONV_CH))


def _ssm_conv_bwd(proj, w, b, dxbc, *, name):
    s = proj.shape[0]
    cw = 256

    def body(x_ref, w_ref, b_ref, dy_ref, dx_ref, dw_ref, db_ref):
        x = x_ref[...]
        wv = w_ref[...]
        pre = _conv_pre(x, wv, b_ref[...], SSM_CONV)
        sg = _sigmoid(pre)
        dpre = dy_ref[...] * (sg * (1.0 + pre * (1.0 - sg)))
        dx, dw, db = _conv_bwd_terms(x, wv, dpre, SSM_CONV)
        dx_ref[...] = dx.astype(dx_ref.dtype)
        dw_ref[...] = dw
        db_ref[...] = db

    col = pl.BlockSpec((s, cw), lambda j: (0, j))
    wsp = pl.BlockSpec((SSM_CONV, cw), lambda j: (0, j))
    bsp = pl.BlockSpec((1, cw), lambda j: (0, j))
    dx, dw, db = pl.pallas_call(
        body, grid=(CONV_CH // cw,), in_specs=[col, wsp, bsp, col], out_specs=(col, wsp, bsp),
        out_shape=(SDS((s, CONV_CH), BF16), SDS((SSM_CONV, CONV_CH), F32), SDS((1, CONV_CH), F32)), name=name,
        compiler_params=_params(("parallel",), 20 * s * cw * 4))(proj, w, b.reshape(1, CONV_CH), dxbc)
    return dx, dw, db.reshape(CONV_CH)


def _ffn_conv_fwd(up_g, up_v, w, b, *, name):
    s = up_g.shape[0]
    cw = 256
    nb = D_FF // cw

    def body(g_ref, v_ref, wg_ref, wv_ref, bg_ref, bv_ref, o_ref):
        gate = _conv_pre(g_ref[...], wg_ref[...], bg_ref[...], FFN_CONV)
        val = _conv_pre(v_ref[...], wv_ref[...], bv_ref[...], FFN_CONV)
        o_ref[...] = (gate * _sigmoid(gate) * val).astype(o_ref.dtype)

    col = pl.BlockSpec((s, cw), lambda j: (0, j))
    b2 = b.reshape(1, 2 * D_FF)
    return pl.pallas_call(
        body, grid=(nb,),
        in_specs=[col, col, pl.BlockSpec((FFN_CONV, cw), lambda j: (0, j)), pl.BlockSpec((FFN_CONV, cw), lambda j: (0, j + nb)),
                  pl.BlockSpec((1, cw), lambda j: (0, j)), pl.BlockSpec((1, cw), lambda j: (0, j + nb))],
        out_specs=col, out_shape=SDS((s, D_FF), BF16), name=name,
        compiler_params=_params(("parallel",), 16 * s * cw * 4))(up_g, up_v, w, w, b2, b2)


def _ffn_conv_bwd(up_g, up_v, w, b, dact, *, name):
    s = up_g.shape[0]
    cw = 256
    nb = D_FF // cw

    def body(g_ref, v_ref, wg_ref, wv_ref, bg_ref, bv_ref, da_ref, dg_ref, dv_ref, dwg_ref, dwv_ref, dbg_ref, dbv_ref):
        xg, xv = g_ref[...], v_ref[...]
        wg, wv = wg_ref[...], wv_ref[...]
        gate = _conv_pre(xg, wg, bg_ref[...], FFN_CONV)
        val = _conv_pre(xv, wv, bv_ref[...], FFN_CONV)
        da = da_ref[...].astype(F32)
        sg = _sigmoid(gate)
        dgate = da * val * (sg * (1.0 + gate * (1.0 - sg)))
        dval = da * gate * sg
        dxg, dwg, dbg = _conv_bwd_terms(xg, wg, dgate, FFN_CONV)
        dxv, dwv, dbv = _conv_bwd_terms(xv, wv, dval, FFN_CONV)
        dg_ref[...] = dxg.astype(dg_ref.dtype)
        dv_ref[...] = dxv.astype(dv_ref.dtype)
        dwg_ref[...] = dwg
        dwv_ref[...] = dwv
        dbg_ref[...] = dbg
        dbv_ref[...] = dbv

    col = pl.BlockSpec((s, cw), lambda j: (0, j))
    wsp = pl.BlockSpec((FFN_CONV, cw), lambda j: (0, j))
    bsp = pl.BlockSpec((1, cw), lambda j: (0, j))
    b2 = b.reshape(1, 2 * D_FF)
    dg, dv, dwg, dwv, dbg, dbv = pl.pallas_call(
        body, grid=(nb,),
        in_specs=[col, col, wsp, pl.BlockSpec((FFN_CONV, cw), lambda j: (0, j + nb)), bsp,
                  pl.BlockSpec((1, cw), lambda j: (0, j + nb)), col],
        out_specs=(col, col, wsp, wsp, bsp, bsp),
        out_shape=(SDS((s, D_FF), BF16), SDS((s, D_FF), BF16), SDS((FFN_CONV, D_FF), F32), SDS((FFN_CONV, D_FF), F32),
                   SDS((1, D_FF), F32), SDS((1, D_FF), F32)), name=name,
        compiler_params=_params(("parallel",), 32 * s * cw * 4))(up_g, up_v, w, w, b2, b2, dact)
    return dg, dv, jnp.concatenate([dwg, dwv], axis=1), jnp.concatenate([dbg, dbv], axis=1).reshape(2 * D_FF)


def _dot(a, b):
    return jnp.dot(a.astype(BF16), b.astype(BF16), preferred_element_type=F32)


def _dot_nt(a, b):
    return lax.dot_general(a.astype(BF16), b.astype(BF16), (((1,), (1,)), ((), ())), preferred_element_type=F32)


def _dot_tn(a, b):
    return lax.dot_general(a.astype(BF16), b.astype(BF16), (((0,), (0,)), ((), ())), preferred_element_type=F32)


def _ssd_chunk_terms(dtraw, bias, a_log):
    ell = dtraw.shape[0]
    lane = lax.broadcasted_iota(jnp.int32, dtraw.shape, 1)
    valid = lane < SSM_HEADS
    pre = dtraw + bias
    dt = jnp.where(valid, jnp.where(pre > 20.0, pre, jnp.log(1.0 + jnp.exp(jnp.minimum(pre, 20.0)))), 0.0)
    a = -jnp.exp(a_log)
    ad = dt * a
    row = lax.broadcasted_iota(jnp.int32, (ell, ell), 0)
    colm = lax.broadcasted_iota(jnp.int32, (ell, ell), 1)
    tril = row >= colm
    cs = jnp.dot(tril.astype(F32), ad, precision=HIGHEST, preferred_element_type=F32)
    cs_last = cs[ell - 1:ell, :]
    return pre, dt, a, cs, cs_last, tril


def _lane_put(col, h, shape):
    lane = lax.broadcasted_iota(jnp.int32, shape, 1)
    return jnp.where(lane == h, col, 0.0)


def _ssd_fwd(xbc, proj, dt_bias, a_log, d_skip, *, name):
    s = xbc.shape[0]
    nc = s // SSM_CHUNK
    ell, n, p = SSM_CHUNK, SSM_STATE, SSM_HEAD_DIM
    rpg = SSM_HEADS // SSM_GROUPS

    def body(x_ref, dt_ref, bias_ref, alog_ref, dskip_ref, y_ref, ps_ref, state):
        @pl.when(pl.program_id(0) == 0)
        def _():
            state[...] = jnp.zeros_like(state)

        _, dt, _, cs, cs_last, tril = _ssd_chunk_terms(dt_ref[...], bias_ref[...], alog_ref[...])
        e = jnp.exp(cs)
        ds = jnp.exp(cs_last - cs)
        cd = jnp.exp(cs_last)
        cst = cs.T
        dskip = dskip_ref[...]
        ps_ref[0] = state[...]
        for g in range(SSM_GROUPS):
            bg = x_ref[:, D_SSM + n * g:D_SSM + n * (g + 1)]
            cg = x_ref[:, D_SSM + n * (SSM_GROUPS + g):D_SSM + n * (SSM_GROUPS + g + 1)]
            cb = _dot_nt(cg, bg)
            for r in range(rpg):
                h = g * rpg + r
                hs = slice(p * h, p * (h + 1))
                xs = x_ref[:, hs]
                xd = xs * dt[:, h:h + 1]
                lmat = jnp.exp(jnp.where(tril, cs[:, h:h + 1] - cst[h:h + 1, :], -jnp.inf))
                prev = state[:, hs]
                y = _dot(cb * lmat, xd) + _dot(cg, prev) * e[:, h:h + 1] + xs * dskip[:, h:h + 1]
                y_ref[:, hs] = y
                state[:, hs] = prev * cd[:, h:h + 1] + _dot_tn(bg, xd * ds[:, h:h + 1])

    vec = pl.BlockSpec((1, LANES), lambda c: (0, 0))
    return pl.pallas_call(
        body, grid=(nc,),
        in_specs=[pl.BlockSpec((ell, CONV_CH), lambda c: (c, 0)), pl.BlockSpec((ell, LANES), lambda c: (c, P_DT // LANES)),
                  vec, vec, vec],
        out_specs=(pl.BlockSpec((ell, D_SSM), lambda c: (c, 0)), pl.BlockSpec((1, n, D_SSM), lambda c: (c, 0, 0))),
        out_shape=(SDS((s, D_SSM), F32), SDS((nc, n, D_SSM), F32)),
        scratch_shapes=[pltpu.VMEM((n, D_SSM), F32)], name=name,
        compiler_params=_params(("arbitrary",), 24 << 20))(xbc, proj, dt_bias, a_log, d_skip)


def _ssd_bwd(xbc, proj, dt_bias, a_log, d_skip, prev_states, dy, *, name):
    s = xbc.shape[0]
    nc = s // SSM_CHUNK
    ell, n, p = SSM_CHUNK, SSM_STATE, SSM_HEAD_DIM
    rpg = SSM_HEADS // SSM_GROUPS

    def body(x_ref, dt_ref, bias_ref, alog_ref, dskip_ref, ps_ref, dy_ref,
             dx_ref, ddt_ref, dalog_ref, ddskip_ref, dbias_ref, dstate):
        @pl.when(pl.program_id(0) == 0)
        def _():
            dstate[...] = jnp.zeros_like(dstate)
            dalog_ref[...] = jnp.zeros_like(dalog_ref)
            ddskip_ref[...] = jnp.zeros_like(ddskip_ref)
            dbias_ref[...] = jnp.zeros_like(dbias_ref)

        pre, dt, a, cs, cs_last, tril = _ssd_chunk_terms(dt_ref[...], bias_ref[...], alog_ref[...])
        e = jnp.exp(cs)
        ds = jnp.exp(cs_last - cs)
        cd = jnp.exp(cs_last)
        cst = cs.T
        dskip = dskip_ref[...]
        shape = (ell, LANES)
        ddt_acc = jnp.zeros(shape, F32)
        dcs_acc = jnp.zeros(shape, F32)
        dcs_rows = jnp.zeros(shape, F32)
        dlast_acc = jnp.zeros((1, LANES), F32)
        dskip_acc = jnp.zeros((1, LANES), F32)
        for g in range(SSM_GROUPS):
            bsl = slice(D_SSM + n * g, D_SSM + n * (g + 1))
            csl = slice(D_SSM + n * (SSM_GROUPS + g), D_SSM + n * (SSM_GROUPS + g + 1))
            bg = x_ref[:, bsl]
            cg = x_ref[:, csl]
            cb = _dot_nt(cg, bg)
            dcb = jnp.zeros((ell, ell), F32)
            dbg = jnp.zeros((ell, n), F32)
            dcg = jnp.zeros((ell, n), F32)
            for r in range(rpg):
                h = g * rpg + r
                hs = slice(p * h, p * (h + 1))
                xs = x_ref[:, hs]
                dyh = dy_ref[:, hs]
                dt_h, e_h, ds_h, cd_h = dt[:, h:h + 1], e[:, h:h + 1], ds[:, h:h + 1], cd[:, h:h + 1]
                prev = ps_ref[0, :, hs]
                dsn = dstate[:, hs]
                xd = xs * dt_h
                dye = dyh * e_h
                cprev = _dot(cg, prev)
                dprev = dsn * cd_h + _dot_tn(cg, dye)
                dcg = dcg + _dot_nt(dye, prev)
                dcs_h = jnp.sum(dyh * cprev, axis=1, keepdims=True) * e_h
                dcd = jnp.sum(jnp.sum(dsn * prev, axis=1, keepdims=True), axis=0, keepdims=True)
                dlast_h = dcd * cd_h
                dxdd = _dot(bg, dsn)
                dbg = dbg + _dot_nt(xd * ds_h, dsn)
                dxd = dxdd * ds_h
                tmp = jnp.sum(dxdd * xd, axis=1, keepdims=True) * ds_h
                dlast_h = dlast_h + jnp.sum(tmp, axis=0, keepdims=True)
                dcs_h = dcs_h - tmp
                lmat = jnp.exp(jnp.where(tril, cs[:, h:h + 1] - cst[h:h + 1, :], -jnp.inf))
                gm = cb * lmat
                dgm = _dot_nt(dyh, xd)
                dxd = dxd + _dot_tn(gm, dyh)
                mm = dgm * gm
                dcs_h = dcs_h + jnp.sum(mm, axis=1, keepdims=True)
                sub = lax.broadcasted_iota(jnp.int32, shape, 0)
                dcs_rows = dcs_rows + jnp.where(sub == h, jnp.sum(mm, axis=0, keepdims=True), 0.0)
                dcb = dcb + dgm * lmat
                dx_ref[:, hs] = dxd * dt_h + dyh * dskip[:, h:h + 1]
                ddt_acc = ddt_acc + _lane_put(jnp.sum(dxd * xs, axis=1, keepdims=True), h, shape)
                dcs_acc = dcs_acc + _lane_put(dcs_h, h, shape)
                dlast_acc = dlast_acc + _lane_put(dlast_h, h, (1, LANES))
                dskip_acc = dskip_acc + _lane_put(
                    jnp.sum(jnp.sum(dyh * xs, axis=1, keepdims=True), axis=0, keepdims=True), h, (1, LANES))
                dstate[:, hs] = dprev
            dx_ref[:, bsl] = dbg + _dot_tn(dcb, cg)
            dx_ref[:, csl] = dcg + _dot(dcb, bg)
        rowi = lax.broadcasted_iota(jnp.int32, shape, 0)
        dcs = dcs_acc - dcs_rows.T + jnp.where(rowi == ell - 1, dlast_acc, 0.0)
        triu = lax.broadcasted_iota(jnp.int32, (ell, ell), 0) <= lax.broadcasted_iota(jnp.int32, (ell, ell), 1)
        dad = jnp.dot(triu.astype(F32), dcs, precision=HIGHEST, preferred_element_type=F32)
        ddt = ddt_acc + dad * a
        dalog_ref[...] += jnp.sum(dad * dt, axis=0, keepdims=True) * a
        ddskip_ref[...] += dskip_acc
        lane = lax.broadcasted_iota(jnp.int32, shape, 1)
        ddraw = jnp.where(lane < SSM_HEADS, ddt * _sigmoid(pre), 0.0)
        ddt_ref[...] = ddraw.astype(ddt_ref.dtype)
        dbias_ref[...] += jnp.sum(ddraw, axis=0, keepdims=True)

    vec = pl.BlockSpec((1, LANES), lambda c: (0, 0))
    rev = lambda c: nc - 1 - c
    outs = pl.pallas_call(
        body, grid=(nc,),
        in_specs=[pl.BlockSpec((ell, CONV_CH), lambda c: (rev(c), 0)),
                  pl.BlockSpec((ell, LANES), lambda c: (rev(c), P_DT // LANES)), vec, vec, vec,
                  pl.BlockSpec((1, n, D_SSM), lambda c: (rev(c), 0, 0)),
                  pl.BlockSpec((ell, D_SSM), lambda c: (rev(c), 0))],
        out_specs=(pl.BlockSpec((ell, CONV_CH), lambda c: (rev(c), 0)), pl.BlockSpec((ell, LANES), lambda c: (rev(c), 0)),
                   vec, vec, vec),
        out_shape=(SDS((s, CONV_CH), F32), SDS((s, LANES), BF16), SDS((1, LANES), F32), SDS((1, LANES), F32),
                   SDS((1, LANES), F32)),
        scratch_shapes=[pltpu.VMEM((n, D_SSM), F32)], name=name,
        compiler_params=_params(("arbitrary",), 32 << 20))(xbc, proj, dt_bias, a_log, d_skip, prev_states, dy)
    return outs


def _rope_swap(t):
    lane = lax.broadcasted_iota(jnp.int32, t.shape, 1)
    half = QK_ROPE // 2
    lo = (lane >= QK_NOPE) & (lane < QK_NOPE + half)
    hi = (lane >= QK_NOPE + half) & (lane < QK_NOPE + QK_ROPE)
    return jnp.where(lo, pltpu.roll(t, HEAD_PAD - half, axis=1), jnp.where(hi, pltpu.roll(t, half, axis=1), 0.0))


def _mla_prep(q, kv, proj, cos, sins, *, name):
    s = q.shape[0]
    tm = min(s, 256)
    scale = (QK_NOPE + QK_ROPE) ** -0.5

    def body(q_ref, kv_ref, kr_ref, cos_ref, sin_ref, qo_ref, ko_ref, vo_ref):
        cosv, sinv = cos_ref[...], sin_ref[...]
        kr = pltpu.roll(kr_ref[...], QK_NOPE, axis=1)
        lane = lax.broadcasted_iota(jnp.int32, kr.shape, 1)
        nope = lane < QK_NOPE
        kr = jnp.where(nope, 0.0, kr)
        kpe = kr * cosv + _rope_swap(kr) * sinv
        for hp in range(MLA_HEADS // 2):
            vs = []
            for h in (2 * hp, 2 * hp + 1):
                hs = slice(HEAD_PAD * h, HEAD_PAD * (h + 1))
                qh = q_ref[:, hs]
                kvh = kv_ref[:, hs]
                qo_ref[:, hs] = ((qh * cosv + _rope_swap(qh) * sinv) * scale).astype(qo_ref.dtype)
                ko_ref[:, hs] = (jnp.where(nope, kvh, 0.0) + kpe).astype(ko_ref.dtype)
                vs.append(kvh[:, QK_NOPE:])
            vo_ref[:, 2 * V_DIM * hp:2 * V_DIM * (hp + 1)] = jnp.concatenate(vs, axis=1).astype(vo_ref.dtype)

    wide = pl.BlockSpec((tm, MLA_HEADS * HEAD_PAD), lambda i: (i, 0))
    half = pl.BlockSpec((tm, MLA_HEADS * V_DIM), lambda i: (i, 0))
    tab = pl.BlockSpec((tm, LANES), lambda i: (i, 0))
    return pl.pallas_call(
        body, grid=(s // tm,),
        in_specs=[wide, wide, pl.BlockSpec((tm, LANES), lambda i: (i, P_KR // LANES)), tab, tab],
        out_specs=(wide, wide, half),
        out_shape=(SDS((s, MLA_HEADS * HEAD_PAD), BF16), SDS((s, MLA_HEADS * HEAD_PAD), BF16),
                   SDS((s, MLA_HEADS * V_DIM), BF16)), name=name,
        compiler_params=_params(("parallel",), 32 << 20))(q, kv, proj, cos, sins)


def _mla_prep_bwd(dqr, dkr, dv, cos, sins, *, name):
    s = dqr.shape[0]
    tm = min(s, 256)
    scale = (QK_NOPE + QK_ROPE) ** -0.5

    def body(dq_ref, dk_ref, dv_ref, cos_ref, sin_ref, dqo_ref, dkv_ref, dkr_ref):
        cosv, sinv = cos_ref[...], sin_ref[...]
        lane = lax.broadcasted_iota(jnp.int32, cosv.shape, 1)
        ksum = jnp.zeros(cosv.shape, F32)
        for h in range(MLA_HEADS):
            hs = slice(HEAD_PAD * h, HEAD_PAD * (h + 1))
            d = dq_ref[:, hs]
            dk = dk_ref[:, hs]
            dqo_ref[:, hs] = ((d * cosv + _rope_swap(d * sinv)) * scale).astype(dqo_ref.dtype)
            dkv_ref[:, hs] = jnp.concatenate([dk[:, :QK_NOPE], dv_ref[:, V_DIM * h:V_DIM * (h + 1)]], axis=1).astype(dkv_ref.dtype)
            ksum = ksum + dk
        ksum = jnp.where((lane >= QK_NOPE) & (lane < QK_NOPE + QK_ROPE), ksum, 0.0)
        un = ksum * cosv + _rope_swap(ksum * sinv)
        dkr_ref[...] = pltpu.roll(un, HEAD_PAD - QK_NOPE, axis=1).astype(dkr_ref.dtype)

    wide = pl.BlockSpec((tm, MLA_HEADS * HEAD_PAD), lambda i: (i, 0))
    half = pl.BlockSpec((tm, MLA_HEADS * V_DIM), lambda i: (i, 0))
    tab = pl.BlockSpec((tm, LANES), lambda i: (i, 0))
    return pl.pallas_call(
        body, grid=(s // tm,), in_specs=[wide, wide, half, tab, tab], out_specs=(wide, wide, tab),
        out_shape=(SDS((s, MLA_HEADS * HEAD_PAD), BF16), SDS((s, MLA_HEADS * HEAD_PAD), BF16), SDS((s, LANES), BF16)),
        name=name, compiler_params=_params(("parallel",), 40 << 20))(dqr, dkr, dv, cos, sins)


FLASH_TILE = 512


def _flash_fwd(q, k, v, *, name):
    s = q.shape[0]
    t = min(s, FLASH_TILE)
    nq = s // t
    npair = MLA_HEADS // 2

    def body(q_ref, k_ref, v_ref, o_ref, lse_ref):
        i = pl.program_id(1)
        qs = [q_ref[:, HEAD_PAD * e:HEAD_PAD * (e + 1)] for e in range(2)]
        diag = lax.broadcasted_iota(jnp.int32, (t, t), 0) >= lax.broadcasted_iota(jnp.int32, (t, t), 1)

        def step(j, carry, masked):
            rows = pl.ds(pl.multiple_of(j * t, t), t)
            new = []
            for e in range(2):
                m, l, acc = carry[e]
                sc = _dot_nt(qs[e], k_ref[rows, HEAD_PAD * e:HEAD_PAD * (e + 1)])
                if masked:
                    sc = jnp.where(diag, sc, NEG)
                m_new = jnp.maximum(m, jnp.max(sc, axis=1, keepdims=True))
                pr = jnp.exp(sc - m_new)
                alpha = jnp.exp(m - m_new)
                l = alpha * l + jnp.sum(pr, axis=1, keepdims=True)
                acc = alpha * acc + _dot(pr, v_ref[rows, V_DIM * e:V_DIM * (e + 1)])
                new.append((m_new, l, acc))
            return tuple(new)

        init = tuple((jnp.full((t, 1), NEG, F32), jnp.zeros((t, 1), F32), jnp.zeros((t, V_DIM), F32)) for _ in range(2))
        carry = lax.fori_loop(0, i, functools.partial(step, masked=False), init)
        carry = step(i, carry, True)
        o_ref[...] = jnp.concatenate([acc / l for _, l, acc in carry], axis=1)
        lse_ref[0] = jnp.concatenate([jnp.broadcast_to(m + jnp.log(l), (t, V_DIM)) for m, l, _ in carry], axis=1)

    return pl.pallas_call(
        body, grid=(npair, nq),
        in_specs=[pl.BlockSpec((t, 2 * HEAD_PAD), lambda hp, i: (i, hp)), pl.BlockSpec((s, 2 * HEAD_PAD), lambda hp, i: (0, hp)),
                  pl.BlockSpec((s, 2 * V_DIM), lambda hp, i: (0, hp))],
        out_specs=(pl.BlockSpec((t, 2 * V_DIM), lambda hp, i: (i, hp)), pl.BlockSpec((1, t, LANES), lambda hp, i: (hp, i, 0))),
        out_shape=(SDS((s, MLA_HEADS * V_DIM), F32), SDS((npair, s, LANES), F32)), name=name,
        compiler_params=_params(("parallel", "parallel"), 40 << 20))(q, k, v)


def _flash_bwd(q, k, v, o, lse, do, *, name):
    s = q.shape[0]
    t = min(s, FLASH_TILE)
    nq = s // t
    npair = MLA_HEADS // 2

    def body(q_ref, k_ref, v_ref, o_ref, lse_ref, do_ref, dq_ref, dk_ref, dv_ref):
        j = pl.program_id(1)

        @pl.when(j == 0)
        def _():
            dq_ref[...] = jnp.zeros_like(dq_ref)

        qsl = [slice(HEAD_PAD * e, HEAD_PAD * (e + 1)) for e in range(2)]
        vsl = [slice(V_DIM * e, V_DIM * (e + 1)) for e in range(2)]
        ks = [k_ref[:, qsl[e]] for e in range(2)]
        vs = [v_ref[:, vsl[e]] for e in range(2)]
        diag = lax.broadcasted_iota(jnp.int32, (t, t), 0) >= lax.broadcasted_iota(jnp.int32, (t, t), 1)

        def step(i, carry, masked):
            rows = pl.ds(pl.multiple_of(i * t, t), t)
            new = []
            for e in range(2):
                dk, dv = carry[e]
                qi = q_ref[rows, qsl[e]]
                doi = do_ref[rows, vsl[e]]
                delta = jnp.sum(doi * o_ref[rows, vsl[e]], axis=1, keepdims=True)
                lse_i = lse_ref[0, rows, vsl[e]][:, 0:1]
                sc = _dot_nt(qi, ks[e])
                if masked:
                    sc = jnp.where(diag, sc, NEG)
                pr = jnp.exp(sc - lse_i)
                dv = dv + _dot_tn(pr, doi)
                dsc = (pr * (_dot_nt(doi, vs[e]) - delta)).astype(BF16)
                dk = dk + _dot_tn(dsc, qi)
                dq_ref[rows, qsl[e]] += _dot(dsc, ks[e])
                new.append((dk, dv))
            return tuple(new)

        init = tuple((jnp.zeros((t, HEAD_PAD), F32), jnp.zeros((t, V_DIM), F32)) for _ in range(2))
        carry = step(j, init, True)
        carry = lax.fori_loop(j + 1, nq, functools.partial(step, masked=False), carry)
        dk_ref[...] = jnp.concatenate([dk for dk, _ in carry], axis=1)
        dv_ref[...] = jnp.concatenate([dv for _, dv in carry], axis=1)

    full_q = pl.BlockSpec((s, 2 * HEAD_PAD), lambda hp, j: (0, hp))
    full_v = pl.BlockSpec((s, 2 * V_DIM), lambda hp, j: (0, hp))
    blk_k = pl.BlockSpec((t, 2 * HEAD_PAD), lambda hp, j: (j, hp))
    blk_v = pl.BlockSpec((t, 2 * V_DIM), lambda hp, j: (j, hp))
    return pl.pallas_call(
        body, grid=(npair, nq),
        in_specs=[full_q, blk_k, blk_v, full_v, pl.BlockSpec((1, s, LANES), lambda hp, j: (hp, 0, 0)), full_v],
        out_specs=(full_q, blk_k, blk_v),
        out_shape=(SDS((s, MLA_HEADS * HEAD_PAD), F32), SDS((s, MLA_HEADS * HEAD_PAD), F32), SDS((s, MLA_HEADS * V_DIM), F32)),
        name=name, compiler_params=_params(("parallel", "arbitrary"), 48 << 20))(q, k, v, o, lse, do)


def _mem_attn_fwd(q, k, v, *, name):
    s = q.shape[0]
    tm = min(s, 512)
    ml = k.shape[0]
    scale = MEM_HEAD_DIM ** -0.5

    def body(q_ref, k_ref, v_ref, o_ref):
        for h in range(MEM_HEADS):
            hs = slice(MEM_HEAD_DIM * h, MEM_HEAD_DIM * (h + 1))
            sc = _dot_nt(q_ref[:, hs], k_ref[:, hs]) * scale
            pr = jnp.exp(sc - jnp.max(sc, axis=1, keepdims=True))
            pr = pr / jnp.sum(pr, axis=1, keepdims=True)
            o_ref[:, hs] = _dot(pr, v_ref[:, hs]).astype(o_ref.dtype)

    blk = pl.BlockSpec((tm, D_MODEL), lambda i: (i, 0))
    kv = pl.BlockSpec((ml, D_MODEL), lambda i: (0, 0))
    return pl.pallas_call(body, grid=(s // tm,), in_specs=[blk, kv, kv], out_specs=blk,
                          out_shape=SDS((s, D_MODEL), BF16), name=name,
                          compiler_params=_params(("parallel",), 24 << 20))(q, k, v)


def _mem_attn_bwd(q, k, v, do, *, name):
    s = q.shape[0]
    tm = min(s, 512)
    ml = k.shape[0]
    scale = MEM_HEAD_DIM ** -0.5

    def body(q_ref, k_ref, v_ref, do_ref, dq_ref, dk_ref, dv_ref):
        @pl.when(pl.program_id(0) == 0)
        def _():
            dk_ref[...] = jnp.zeros_like(dk_ref)
            dv_ref[...] = jnp.zeros_like(dv_ref)

        for h in range(MEM_HEADS):
            hs = slice(MEM_HEAD_DIM * h, MEM_HEAD_DIM * (h + 1))
            qh, kh, vh, doh = q_ref[:, hs], k_ref[:, hs], v_ref[:, hs], do_ref[:, hs]
            sc = _dot_nt(qh, kh) * scale
            pr = jnp.exp(sc - jnp.max(sc, axis=1, keepdims=True))
            pr = pr / jnp.sum(pr, axis=1, keepdims=True)
            dp = _dot_nt(doh, vh)
            dsc = pr * (dp - jnp.sum(pr * dp, axis=1, keepdims=True)) * scale
            dq_ref[:, hs] = _dot(dsc, kh).astype(dq_ref.dtype)
            dk_ref[:, hs] += _dot_tn(dsc, qh)
            dv_ref[:, hs] += _dot_tn(pr, doh)

    blk = pl.BlockSpec((tm, D_MODEL), lambda i: (i, 0))
    kv = pl.BlockSpec((ml, D_MODEL), lambda i: (0, 0))
    return pl.pallas_call(body, grid=(s // tm,), in_specs=[blk, kv, kv, blk], out_specs=(blk, kv, kv),
                          out_shape=(SDS((s, D_MODEL), BF16), SDS((ml, D_MODEL), F32), SDS((ml, D_MODEL), F32)), name=name,
                          compiler_params=_params(("arbitrary",), 32 << 20))(q, k, v, do)


MATS = (("w_in", (1024, 940), 1), ("w_uq", (384, 384), 1), ("w_ukv", (256, 512), 1), ("w_out", (512, 1024), 0),
        ("w_mq", (256, 1024), 0), ("w_mk", (256, 1024), 0), ("w_mv", (256, 1024), 0), ("w_mo", (256, 1024), 0),
        ("w_up", (1024, 1408), 1), ("w_down", (704, 1024), 0), ("ssm_conv_w", (4, 512), 1), ("ffn_conv_w", (3, 1408), 1))
F32_ON_WIRE = ("ssm_conv_w", "ffn_conv_w")
SMALL = (("norm_mix", 1024), ("ssm_conv_b", 2048), ("dt_bias", 16), ("a_log", 16), ("d_skip", 16), ("ssm_norm", 1024),
         ("q_norm", 384), ("kv_norm", 256), ("attn_out_norm", 1024), ("norm_mem_q", 1024), ("norm_mem_kv", 1024),
         ("norm_ffn", 1024), ("ffn_conv_b", 5632))
PACK_COLS = 1024


def _pad_cols(t, n):
    return jnp.pad(t, ((0, 0),) * (t.ndim - 1) + ((0, n - t.shape[-1]),))


def _w_in_to_padded(t):
    z, xbc, dt, cq, ckv, kr = jnp.split(t, (1024, 3072, 3088, 3472, 3728), axis=-1)
    return jnp.concatenate([xbc, z, cq, _pad_cols(dt, LANES), ckv, _pad_cols(kr, P_IN - P_KR)], axis=-1)


def _w_in_from_padded(t):
    return jnp.concatenate([t[..., P_Z:P_Z + 1024], t[..., P_XBC:P_XBC + 2048], t[..., P_DT:P_DT + SSM_HEADS],
                            t[..., P_CQ:P_CQ + Q_LORA], t[..., P_CKV:P_CKV + KV_LORA], t[..., P_KR:P_KR + QK_ROPE]], axis=-1)


def _cols_joined(g, li):
    return jnp.concatenate([g[j, li] for j in range(N_CHIPS)], axis=-1)


def _cols_by_chip(t, dtype):
    k = t.shape[0]
    return t.reshape(k, N_CHIPS, -1).transpose(1, 0, 2).astype(dtype)


def _rows_by_chip(t):
    return t.reshape(N_CHIPS, -1, t.shape[-1])


def _layer_weights(gw, li):
    wl = {}
    wl["w_in"] = _w_in_to_padded(_cols_joined(gw["w_in"], li))
    uq = _cols_joined(gw["w_uq"], li).reshape(Q_LORA, MLA_HEADS, QK_NOPE + QK_ROPE)
    wl["w_uq"] = _pad_cols(uq, HEAD_PAD).reshape(Q_LORA, MLA_HEADS * HEAD_PAD)
    wl["w_ukv"] = _cols_joined(gw["w_ukv"], li)
    wl["ssm_conv_w"] = _cols_joined(gw["ssm_conv_w"], li)
    wl["ffn_conv_w"] = _cols_joined(gw["ffn_conv_w"], li)
    return wl


def _layer_fwd(x0, mem, cos, sins, gw, wl, sp, li):
    n = lambda t: f"l{li}_{t}"
    lead = (li,)
    sv = {"x0": x0}
    h = _rms_fwd(x0, sp["norm_mix"], name=n("mix_norm"))
    proj = _mm(h, wl["w_in"], name=n("mix_proj"))
    xbc = _ssm_conv_fwd(proj, wl["ssm_conv_w"], sp["ssm_conv_b"], name=n("ssm_conv"))
    y, pstates = _ssd_fwd(xbc, proj, sp["dt_bias"], sp["a_log"], sp["d_skip"], name=n("ssd"))
    y_ssm = _gated_rms_fwd(y, proj, sp["ssm_norm"], name=n("ssm_gate"))
    cqn = _rms_fwd(proj, sp["q_norm"], col=(Q_LORA, P_CQ // Q_LORA), name=n("q_norm"))
    ckvn = _rms_fwd(proj, sp["kv_norm"], col=(KV_LORA, P_CKV // KV_LORA), name=n("kv_norm"))
    q = _mm(cqn, wl["w_uq"], name=n("uq"))
    kv = _mm(ckvn, wl["w_ukv"], name=n("ukv"))
    qr, kr, v = _mla_prep(q, kv, proj, cos, sins, name=n("rope"))
    att, lse = _flash_fwd(qr, kr, v, name=n("flash"))
    y_att = _rms_fwd(att, sp["attn_out_norm"], name=n("att_norm"))
    x1 = _mm(y_ssm, gw["w_out"], b_lead=lead, b_rows=(0, D_SSM), res=x0, name=n("out_a"))
    x1 = _mm(y_att, gw["w_out"], b_lead=lead, b_rows=(D_SSM, D_SSM), res=x1, name=n("out_b"))
    sv.update(h=h, proj=proj, xbc=xbc, y=y, pstates=pstates, y_ssm=y_ssm, cqn=cqn, ckvn=ckvn, qr=qr, kr=kr, v=v,
              att=att, lse=lse, y_att=y_att, x1=x1)
    hq = _rms_fwd(x1, sp["norm_mem_q"], name=n("memq_norm"))
    hm = _rms_fwd(mem, sp["norm_mem_kv"], name=n("memkv_norm"))
    mq = _mm(hq, gw["w_mq"], b_lead=lead, out_dtype=BF16, name=n("mq"))
    mk = _mm(hm, gw["w_mk"], b_lead=lead, out_dtype=BF16, name=n("mk"))
    mv = _mm(hm, gw["w_mv"], b_lead=lead, out_dtype=BF16, name=n("mv"))
    mo = _mem_attn_fwd(mq, mk, mv, name=n("mem_attn"))
    x2 = _mm(mo, gw["w_mo"], b_lead=lead, res=x1, name=n("mo"))
    sv.update(hq=hq, hm=hm, mq=mq, mk=mk, mv=mv, mo=mo, x2=x2)
    hf = _rms_fwd(x2, sp["norm_ffn"], name=n("ffn_norm"))
    up_g = _mm(hf, gw["w_up"], b_lead=lead, b_chips=(0, 2), name=n("up_g"))
    up_v = _mm(hf, gw["w_up"], b_lead=lead, b_chips=(2, 2), name=n("up_v"))
    act = _ffn_conv_fwd(up_g, up_v, wl["ffn_conv_w"], sp["ffn_conv_b"], name=n("ffn_conv"))
    x3 = _mm(act, gw["w_down"], b_lead=lead, res=x2, name=n("down"))
    sv.update(hf=hf, up_g=up_g, up_v=up_v, act=act)
    return x3, sv


def _layer_bwd(dx3, mem, cos, sins, gw, wl, sp, sv, li):
    n = lambda t: f"l{li}_b_{t}"
    lead = (li,)
    g = {}
    dact = _mm(dx3, gw["w_down"], tb=True, b_lead=lead, out_dtype=BF16, name=n("down_dx"))
    g["w_down"] = _rows_by_chip(_mm(sv["act"], dx3, ta=True, out_dtype=BF16, name=n("down_dw")))
    dup_g, dup_v, dcw, g["ffn_conv_b"] = _ffn_conv_bwd(
        sv["up_g"], sv["up_v"], wl["ffn_conv_w"], sp["ffn_conv_b"], dact, name=n("ffn_conv"))
    g["ffn_conv_w"] = _cols_by_chip(dcw, F32)
    nsh = MATS[8][1][1]
    dhf = None
    for c4 in range(N_CHIPS):
        dhf = _mm(dup_g if c4 < 2 else dup_v, gw["w_up"], tb=True, a_col=(nsh, c4 % 2), b_lead=(c4, li), res=dhf,
                  name=n(f"up{c4}_dx"))
    g["w_up"] = jnp.concatenate([_mm(sv["hf"], dup_g, ta=True, o_chips=nsh, out_dtype=BF16, name=n("upg_dw")),
                                 _mm(sv["hf"], dup_v, ta=True, o_chips=nsh, out_dtype=BF16, name=n("upv_dw"))], axis=0)
    dx2, g["norm_ffn"] = _rms_bwd(sv["x2"], sp["norm_ffn"], dhf, dx3, name=n("ffn_norm"))
    dmo = _mm(dx2, gw["w_mo"], tb=True, b_lead=lead, out_dtype=BF16, name=n("mo_dx"))
    g["w_mo"] = _rows_by_chip(_mm(sv["mo"], dx2, ta=True, out_dtype=BF16, name=n("mo_dw")))
    dmq, dmk, dmv = _mem_attn_bwd(sv["mq"], sv["mk"], sv["mv"], dmo, name=n("mem_attn"))
    dhq = _mm(dmq, gw["w_mq"], tb=True, b_lead=lead, name=n("mq_dx"))
    g["w_mq"] = _rows_by_chip(_mm(sv["hq"], dmq, ta=True, out_dtype=BF16, name=n("mq_dw")))
    dhm = _mm(dmk, gw["w_mk"], tb=True, b_lead=lead, name=n("mk_dx"))
    dhm = _mm(dmv, gw["w_mv"], tb=True, b_lead=lead, res=dhm, name=n("mv_dx"))
    g["w_mk"] = _rows_by_chip(_mm(sv["hm"], dmk, ta=True, out_dtype=BF16, name=n("mk_dw")))
    g["w_mv"] = _rows_by_chip(_mm(sv["hm"], dmv, ta=True, out_dtype=BF16, name=n("mv_dw")))
    dx1, g["norm_mem_q"] = _rms_bwd(sv["x1"], sp["norm_mem_q"], dhq, dx2, name=n("memq_norm"))
    _, g["norm_mem_kv"] = _rms_bwd(mem, sp["norm_mem_kv"], dhm, name=n("memkv_norm"))
    dy_ssm = _mm(dx1, gw["w_out"], tb=True, b_lead=lead, b_rows=(0, D_SSM), name=n("outa_dx"))
    dy_att = _mm(dx1, gw["w_out"], tb=True, b_lead=lead, b_rows=(D_SSM, D_SSM), name=n("outb_dx"))
    g["w_out"] = _rows_by_chip(jnp.concatenate([_mm(sv["y_ssm"], dx1, ta=True, out_dtype=BF16, name=n("outa_dw")),
                                                _mm(sv["y_att"], dx1, ta=True, out_dtype=BF16, name=n("outb_dw"))], axis=0))
    datt, g["attn_out_norm"] = _rms_bwd(sv["att"], sp["attn_out_norm"], dy_att, name=n("att_norm"))
    dqr, dkr, dv = _flash_bwd(sv["qr"], sv["kr"], sv["v"], sv["att"], sv["lse"], datt, name=n("flash"))
    dq, dkv, dkrope = _mla_prep_bwd(dqr, dkr, dv, cos, sins, name=n("rope"))
    duq = _mm(sv["cqn"], dq, ta=True, name=n("uq_dw")).reshape(Q_LORA, MLA_HEADS, HEAD_PAD)[..., :QK_NOPE + QK_ROPE]
    g["w_uq"] = _cols_by_chip(duq.reshape(Q_LORA, -1), BF16)
    dcqn = _mm(dq, wl["w_uq"], tb=True, name=n("uq_dx"))
    g["w_ukv"] = _cols_by_chip(_mm(sv["ckvn"], dkv, ta=True, name=n("ukv_dw")), BF16)
    dckvn = _mm(dkv, wl["w_ukv"], tb=True, name=n("ukv_dx"))
    proj = sv["proj"]
    dcq, g["q_norm"] = _rms_bwd(proj, sp["q_norm"], dcqn, col=(Q_LORA, P_CQ // Q_LORA), name=n("q_norm"))
    dckv, g["kv_norm"] = _rms_bwd(proj, sp["kv_norm"], dckvn, col=(KV_LORA, P_CKV // KV_LORA), name=n("kv_norm"))
    dy, dz, g["ssm_norm"] = _gated_rms_bwd(sv["y"], proj, sp["ssm_norm"], dy_ssm, name=n("ssm_gate"))
    dxbc, ddt, da_log, dd_skip, ddt_bias = _ssd_bwd(
        sv["xbc"], proj, sp["dt_bias"], sp["a_log"], sp["d_skip"], sv["pstates"], dy, name=n("ssd"))
    g["a_log"], g["d_skip"], g["dt_bias"] = da_log[0, :SSM_HEADS], dd_skip[0, :SSM_HEADS], ddt_bias[0, :SSM_HEADS]
    dxbc_pre, dsw, g["ssm_conv_b"] = _ssm_conv_bwd(proj, wl["ssm_conv_w"], sp["ssm_conv_b"], dxbc, name=n("ssm_conv"))
    g["ssm_conv_w"] = _cols_by_chip(dsw, F32)
    s = proj.shape[0]
    dproj = jnp.concatenate([dxbc_pre, dz, dcq.astype(BF16), ddt, dckv.astype(BF16), dkrope,
                             jnp.zeros((s, P_IN - P_KR - LANES), BF16)], axis=1)
    dh = _mm(dproj, wl["w_in"], tb=True, name=n("proj_dx"))
    g["w_in"] = _cols_by_chip(_w_in_from_padded(_mm(sv["h"], dproj, ta=True, name=n("proj_dw"))), BF16)
    dx0, g["norm_mix"] = _rms_bwd(sv["x0"], sp["norm_mix"], dh, dx1, name=n("mix_norm"))
    return dx0, g


def _chip_peers(x, y):
    return [(1 - x, y), (x, 1 - y), (1 - x, 1 - y)]


def _plane_exchange(bufs, out_shapes, src_view, dst_view, *, name):
    n = len(bufs)

    def body(*refs):
        srcs, outs = refs[:n], refs[n:2 * n]
        send_sems, recv_sems, local_sems = refs[2 * n:]
        x, y, c = lax.axis_index("x"), lax.axis_index("y"), lax.axis_index("c")
        me = 2 * x + y
        local = [pltpu.make_async_copy(src_view(t, srcs[t], me), dst_view(t, outs[t], me), local_sems.at[t]) for t in range(n)]
        for cp in local:
            cp.start()
        sends = []
        for k, (px, py) in enumerate(_chip_peers(x, y)):
            chip = 2 * px + py
            for t in range(n):
                cp = pltpu.make_async_remote_copy(
                    src_ref=src_view(t, srcs[t], chip), dst_ref=dst_view(t, outs[t], me),
                    send_sem=send_sems.at[t, k], recv_sem=recv_sems.at[t, k], device_id=(px, py, c), device_id_type=MESH)
                cp.start()
                sends.append(cp)
        for k, (px, py) in enumerate(_chip_peers(x, y)):
            chip = 2 * px + py
            for t in range(n):
                pltpu.make_async_remote_copy(
                    src_ref=src_view(t, srcs[t], chip), dst_ref=dst_view(t, outs[t], chip),
                    send_sem=send_sems.at[t, k], recv_sem=recv_sems.at[t, k], device_id=(px, py, c),
                    device_id_type=MESH).wait_recv()
        for cp in sends:
            cp.wait_send()
        for cp in local:
            cp.wait()

    any_spec = pl.BlockSpec(memory_space=pl.ANY)
    return pl.pallas_call(
        body, in_specs=[any_spec] * n, out_specs=[any_spec] * n, out_shape=out_shapes,
        scratch_shapes=[pltpu.SemaphoreType.DMA((n, 3)), pltpu.SemaphoreType.DMA((n, 3)), pltpu.SemaphoreType.DMA((n,))],
        name=name)(*bufs)


def _gather_weights(shards, row_sharded, *, name):
    def shape(t, s):
        d, a, b = s.shape
        return SDS((d, N_CHIPS, a, b) if row_sharded[t] else (N_CHIPS, d, a, b), s.dtype)

    return _plane_exchange(
        shards, [shape(t, s) for t, s in enumerate(shards)], lambda t, ref, chip: ref,
        lambda t, ref, chip: ref.at[:, chip] if row_sharded[t] else ref.at[chip], name=name)


def _scatter_grads(grads, *, name):
    def shape(s):
        d, _, a, b = s.shape
        return SDS((N_CHIPS, d, a, b), s.dtype)

    return _plane_exchange(grads, [shape(s) for s in grads], lambda t, ref, chip: ref.at[:, chip],
                           lambda t, ref, chip: ref.at[chip], name=name)


def _swap_cores(bufs, *, name):
    n = len(bufs)

    def body(*refs):
        srcs, outs = refs[:n], refs[n:2 * n]
        send_sems, recv_sems = refs[2 * n:]
        x, y, c = lax.axis_index("x"), lax.axis_index("y"), lax.axis_index("c")
        cps = [pltpu.make_async_remote_copy(src_ref=srcs[t], dst_ref=outs[t], send_sem=send_sems.at[t], recv_sem=recv_sems.at[t],
                                            device_id=(x, y, 1 - c), device_id_type=MESH) for t in range(n)]
        for cp in cps:
            cp.start()
        for cp in cps:
            cp.wait()

    any_spec = pl.BlockSpec(memory_space=pl.ANY)
    return pl.pallas_call(body, in_specs=[any_spec] * n, out_specs=[any_spec] * n,
                          out_shape=[SDS(b.shape, b.dtype) for b in bufs],
                          scratch_shapes=[pltpu.SemaphoreType.DMA((n,)), pltpu.SemaphoreType.DMA((n,))], name=name)(*bufs)


def _all_gather8(src, *, name):
    def body(src_ref, out_ref, send_sems, recv_sems, local_sem):
        x, y, c = lax.axis_index("x"), lax.axis_index("y"), lax.axis_index("c")
        me = 4 * x + 2 * y + c
        mine = pltpu.make_async_copy(src_ref, out_ref.at[me], local_sem)
        mine.start()

        def peer(k):
            return (x ^ (k >> 2 & 1), y ^ (k >> 1 & 1), c ^ (k & 1))

        sends = []
        for k in range(1, N_DEV):
            cp = pltpu.make_async_remote_copy(src_ref=src_ref, dst_ref=out_ref.at[me], send_sem=send_sems.at[k - 1],
                                              recv_sem=recv_sems.at[k - 1], device_id=peer(k), device_id_type=MESH)
            cp.start()
            sends.append(cp)
        for k in range(1, N_DEV):
            px, py, pc = peer(k)
            pltpu.make_async_remote_copy(src_ref=src_ref, dst_ref=out_ref.at[4 * px + 2 * py + pc],
                                         send_sem=send_sems.at[k - 1], recv_sem=recv_sems.at[k - 1],
                                         device_id=peer(k), device_id_type=MESH).wait_recv()
        for cp in sends:
            cp.wait_send()
        mine.wait()

    any_spec = pl.BlockSpec(memory_space=pl.ANY)
    return pl.pallas_call(
        body, in_specs=[any_spec], out_specs=any_spec, out_shape=SDS((N_DEV,) + src.shape, src.dtype),
        scratch_shapes=[pltpu.SemaphoreType.DMA((N_DEV - 1,)), pltpu.SemaphoreType.DMA((N_DEV - 1,)), pltpu.SemaphoreType.DMA],
        name=name)(src)


def _adam_terms(w, g, m, v):
    m = ADAM_B1 * m + (1.0 - ADAM_B1) * g
    v = ADAM_B2 * v + (1.0 - ADAM_B2) * (g * g)
    m_hat = m / (1.0 - ADAM_B1 ** ADAM_STEP)
    v_hat = v / (1.0 - ADAM_B2 ** ADAM_STEP)
    delta = -ADAM_LR * (m_hat / (jnp.sqrt(v_hat) + ADAM_EPS) + ADAM_WD * w)
    return delta, m, v


def _adamw_shard(mine, other, w, m, v, *, name):
    d, a, b = w.shape
    rows = d * a
    tr = 128 if rows % 128 == 0 else rows

    def body(ga_ref, gb_ref, w_ref, m_ref, v_ref, g_ref, d_ref, nm_ref, nv_ref):
        def plane(ref):
            return ((ref[0].astype(F32) + ref[1].astype(F32)) + ref[2].astype(F32)) + ref[3].astype(F32)

        g = plane(ga_ref) + plane(gb_ref)
        delta, mn, vn = _adam_terms(w_ref[...], g, m_ref[...], v_ref[...])
        g_ref[...] = g
        d_ref[...] = delta
        nm_ref[...] = mn
        nv_ref[...] = vn

    blk = pl.BlockSpec((tr, b), lambda i: (i, 0))
    gblk = pl.BlockSpec((N_CHIPS, tr, b), lambda i: (0, i, 0))
    shp = SDS((rows, b), F32)
    flat = lambda t: t.reshape(rows, b)
    outs = pl.pallas_call(
        body, grid=(rows // tr,), in_specs=[gblk, gblk, blk, blk, blk], out_specs=(blk,) * 4, out_shape=(shp,) * 4, name=name,
        compiler_params=_params(("parallel",), 40 << 20))(
            mine.reshape(N_CHIPS, rows, b), other.reshape(N_CHIPS, rows, b), flat(w), flat(m), flat(v))
    return [o.reshape(d, a, b) for o in outs]


def _adamw_small(g8, w, m, v, *, name):
    n = w.shape[1]

    def body(g8_ref, w_ref, m_ref, v_ref, g_ref, d_ref, nm_ref, nv_ref):
        g = g8_ref[0]
        for k in range(1, N_DEV):
            g = g + g8_ref[k]
        delta, mn, vn = _adam_terms(w_ref[...], g, m_ref[...], v_ref[...])
        g_ref[...] = g
        d_ref[...] = delta
        nm_ref[...] = mn
        nv_ref[...] = vn

    shp = SDS((1, n), F32)
    return pl.pallas_call(body, out_shape=(shp,) * 4, name=name, compiler_params=_params(None, 24 << 20))(g8, w, m, v)


def _rope_tables(positions):
    inv_freq = 1.0 / (ROPE_THETA ** (jnp.arange(0, QK_ROPE, 2, dtype=F32) / QK_ROPE))
    ang = positions.astype(F32)[:, None] * inv_freq
    c, s = jnp.cos(ang), jnp.sin(ang)
    n = positions.shape[0]
    pad = jnp.zeros((n, HEAD_PAD - QK_NOPE - QK_ROPE), F32)
    cos = jnp.concatenate([jnp.ones((n, QK_NOPE), F32), c, c, pad], axis=1)
    sins = jnp.concatenate([jnp.zeros((n, QK_NOPE), F32), -s, s, pad], axis=1)
    return cos, sins


def _pad_lanes(v):
    return _pad_cols(v.reshape(1, -1), LANES)


def _local_step(x, mem, positions, gw, small, final_norm, loss_target):
    cos, sins = _rope_tables(positions)
    saved, wls, sps = [], [], []
    h = x
    for li in range(DEPTH):
        wl = _layer_weights(gw, li)
        sp = {k: small[k][li] for k, _ in SMALL}
        for k in ("dt_bias", "a_log", "d_skip"):
            sp[k] = _pad_lanes(sp[k])
        h, sv = _layer_fwd(h, mem, cos, sins, gw, wl, sp, li)
        saved.append(sv)
        wls.append(wl)
        sps.append(sp)
    loss, dh, g_final = _final_loss(h, final_norm, loss_target, name="final_loss")
    grads = [None] * DEPTH
    for li in reversed(range(DEPTH)):
        dh, grads[li] = _layer_bwd(dh, mem, cos, sins, gw, wls[li], sps[li], saved[li], li)
    return loss, dh, grads, g_final


def _gathered_views(gathered):
    gw = {}
    for (k, _, axis), t in zip(MATS, gathered):
        gw[k] = t.reshape(t.shape[0], -1, t.shape[-1]) if axis == 0 else t
    return gw


def kernel(x, mem, positions, norm_mix, w_in, ssm_conv_w, ssm_conv_b, dt_bias, a_log, d_skip, ssm_norm, q_norm, w_uq, kv_norm, w_ukv, attn_out_norm, w_out, norm_mem_q, norm_mem_kv, w_mq, w_mk, w_mv, w_mo, norm_ffn, w_up, ffn_conv_w, ffn_conv_b, w_down, final_norm, loss_target, m_norm_mix, m_w_in, m_ssm_conv_w, m_ssm_conv_b, m_dt_bias, m_a_log, m_d_skip, m_ssm_norm, m_q_norm, m_w_uq, m_kv_norm, m_w_ukv, m_attn_out_norm, m_w_out, m_norm_mem_q, m_norm_mem_kv, m_w_mq, m_w_mk, m_w_mv, m_w_mo, m_norm_ffn, m_w_up, m_ffn_conv_w, m_ffn_conv_b, m_w_down, m_final_norm, v_norm_mix, v_w_in, v_ssm_conv_w, v_ssm_conv_b, v_dt_bias, v_a_log, v_d_skip, v_ssm_norm, v_q_norm, v_w_uq, v_kv_norm, v_w_ukv, v_attn_out_norm, v_w_out, v_norm_mem_q, v_norm_mem_kv, v_w_mq, v_w_mk, v_w_mv, v_w_mo, v_norm_ffn, v_w_up, v_ffn_conv_w, v_ffn_conv_b, v_w_down, v_final_norm):
    args = dict(locals())
    names = ["norm_mix", "w_in", "ssm_conv_w", "ssm_conv_b", "dt_bias", "a_log", "d_skip", "ssm_norm", "q_norm", "w_uq",
             "kv_norm", "w_ukv", "attn_out_norm", "w_out", "norm_mem_q", "norm_mem_kv", "w_mq", "w_mk", "w_mv", "w_mo",
             "norm_ffn", "w_up", "ffn_conv_w", "ffn_conv_b", "w_down", "final_norm"]
    wts = {k: args[k] for k in names}
    mom = {k: args["m_" + k] for k in names}
    var = {k: args["v_" + k] for k in names}
    mat_names = [k for k, _, _ in MATS]

    shards = [wts[k] if k in F32_ON_WIRE else wts[k].astype(BF16) for k in mat_names]
    gw = _gathered_views(_gather_weights(shards, [axis == 0 for _, _, axis in MATS], name="gather_weights"))
    small = {k: wts[k] for k, _ in SMALL}

    loss, grad_x, grads, g_final = _local_step(x[0], mem[0], positions[0], gw, small, wts["final_norm"], loss_target[0])
    loss = lax.psum(loss, ("x", "y", "c"))

    stacked = [jnp.stack([grads[li][k] for li in range(DEPTH)]) for k in mat_names]
    mine = _scatter_grads(stacked, name="scatter_grads")
    other = _swap_cores(mine, name="swap_cores")
    mat_out = {k: _adamw_shard(mine[t], other[t], wts[k], mom[k], var[k], name=f"adamw_{k}") for t, k in enumerate(mat_names)}

    def pack_small(get, fin):
        flat = [get(k).reshape(-1) for k, _ in SMALL] + [fin.reshape(-1)]
        n = sum(f.shape[0] for f in flat)
        return jnp.concatenate(flat + [jnp.zeros((-n % PACK_COLS,), F32)]).reshape(1, -1)

    gs = pack_small(lambda k: jnp.stack([grads[li][k] for li in range(DEPTH)]), g_final)
    g8 = _all_gather8(gs, name="gather_small_grads")
    small_out = _adamw_small(g8, pack_small(wts.get, wts["final_norm"]), pack_small(mom.get, mom["final_norm"]),
                             pack_small(var.get, var["final_norm"]), name="adamw_small")

    def unpack_small(buf):
        out, off = {}, 0
        for k, nel in SMALL:
            out[k] = buf[0, off:off + DEPTH * nel].reshape(DEPTH, nel)
            off += DEPTH * nel
        out["final_norm"] = buf[0, off:off + D_MODEL]
        return out

    small_res = [unpack_small(b) for b in small_out]
    res = []
    for kind in range(4):
        for k in names:
            res.append(small_res[kind][k] if k in small_res[kind] else mat_out[k][kind])
    return (loss, grad_x[None], *res)
```

```python
import functools
import math

import jax
import jax.numpy as jnp
from jax import lax
from jax.experimental import pallas as pl
from jax.experimental.pallas import tpu as pltpu

F32 = jnp.float32
BF16 = jnp.bfloat16
HIGHEST = lax.Precision.HIGHEST
SDS = jax.ShapeDtypeStruct
MESH = pl.DeviceIdType.MESH

D_MODEL = 1024
DEPTH = 4
EPS = 1e-6
SSM_HEADS = 16
SSM_HEAD_DIM = 64
D_SSM = 1024
SSM_GROUPS = 4
SSM_STATE = 128
SSM_CONV = 4
SSM_CHUNK = 128
CONV_CH = 2048
MLA_HEADS = 16
QK_NOPE = 64
QK_ROPE = 32
V_DIM = 64
Q_LORA = 384
KV_LORA = 256
ROPE_THETA = 10000.0
MEM_HEADS = 4
MEM_HEAD_DIM = 256
D_FF = 2816
FFN_CONV = 3
D_IN = 3760
ADAM_LR = 0.001
ADAM_B1 = 0.9
ADAM_B2 = 0.999
ADAM_EPS = 1e-08
ADAM_WD = 0.01
ADAM_STEP = 10

LANES = 128
HEAD_PAD = 128
N_CHIPS = 4
N_DEV = 8
VMEM_CAP_MB = 56

P_XBC, P_Z, P_CQ, P_DT, P_CKV, P_KR, P_IN = 0, 2048, 3072, 3456, 3584, 3840, 4096
NEG = -1e30


def _tile(n, pref):
    t = (min(n, pref) // LANES) * LANES
    while t >= LANES:
        if n % t == 0:
            return t
        t -= LANES
    return n


def _params(sem=None, vmem_bytes=None):
    kw = {}
    if sem is not None:
        kw["dimension_semantics"] = sem
    if vmem_bytes is not None:
        kw["vmem_limit_bytes"] = int(min(max(vmem_bytes, 16 << 20), VMEM_CAP_MB << 20))
    return pltpu.CompilerParams(**kw)


def _nbytes(shape, dtype):
    return math.prod(shape) * jnp.dtype(dtype).itemsize


def _mm(a, b, *, ta=False, tb=False, res=None, out_dtype=F32, name, a_col=None, b_lead=(), b_rows=None,
        b_chips=None, o_chips=None):
    if ta:
        k, m = a.shape
    else:
        m, k = (a.shape[0], a.shape[1] if a_col is None else a_col[0])
    rows_b, cols_b = b.shape[-2:]
    row0 = 0
    if b_rows is not None:
        row0, rows_b = b_rows
    nlead = len(b_lead)
    if b_chips is not None:
        assert not tb
        kb, tn, n = rows_b, cols_b, b_chips[1] * cols_b
        b_blk = (None,) * (1 + nlead) + (kb, tn)
        b_map = lambda i, j: (b_chips[0] + j,) + tuple(b_lead) + (0, 0)
    elif tb:
        n, kb = rows_b, cols_b
        tn = _tile(n, 512)
        assert row0 % tn == 0
        b_blk = (None,) * nlead + (tn, kb)
        b_map = lambda i, j: tuple(b_lead) + (j + row0 // tn, 0)
    else:
        kb, n = rows_b, cols_b
        tn = o_chips if o_chips else _tile(n, 512)
        assert row0 % kb == 0
        b_blk = (None,) * nlead + (kb, tn)
        b_map = lambda i, j: tuple(b_lead) + (row0 // kb, j)
    assert k == kb, (a.shape, b.shape, ta, tb, k, kb)
    tm = _tile(m, 512)
    if ta:
        a_blk, a_map = (k, tm), (lambda i, j: (0, i))
    else:
        a_blk, a_map = (tm, k), ((lambda i, j: (i, 0)) if a_col is None else (lambda i, j: (i, a_col[1])))
    if o_chips:
        o_spec = pl.BlockSpec((None, tm, tn), lambda i, j: (j, i, 0))
        o_shape = SDS((n // tn, m, tn), out_dtype)
    else:
        o_spec = pl.BlockSpec((tm, tn), lambda i, j: (i, j))
        o_shape = SDS((m, n), out_dtype)
    dims = (((0 if ta else 1,), (1 if tb else 0,)), ((), ()))
    has_res = res is not None

    def body(*refs):
        a_ref, b_ref = refs[0], refs[1]
        o_ref = refs[-1]
        acc = lax.dot_general(a_ref[...].astype(BF16), b_ref[...].astype(BF16), dims, preferred_element_type=F32)
        if has_res:
            acc = acc + refs[2][...]
        o_ref[...] = acc.astype(o_ref.dtype)

    bb = tuple(d for d in b_blk if d is not None)
    vmem = 2 * (_nbytes(a_blk, a.dtype) + _nbytes(bb, b.dtype) + (2 if has_res else 1) * _nbytes((tm, tn), F32))
    vmem += _nbytes(a_blk, BF16) + _nbytes(bb, BF16) + 2 * _nbytes((tm, tn), F32) + (4 << 20)
    args = (a, b) + ((res,) if has_res else ())
    specs = [pl.BlockSpec(a_blk, a_map), pl.BlockSpec(b_blk, b_map)] + ([o_spec] if has_res else [])
    return pl.pallas_call(body, grid=(m // tm, n // tn), in_specs=specs, out_specs=o_spec, out_shape=o_shape, name=name,
                          compiler_params=_params(("parallel", "parallel"), vmem))(*args)


def _sigmoid(x):
    return 1.0 / (1.0 + jnp.exp(-x))


def _rms_fwd(x, g, *, col=None, name):
    s = x.shape[0]
    w, ci = (x.shape[1], 0) if col is None else col
    tm = min(s, 512)

    def body(x_ref, g_ref, o_ref):
        xv = x_ref[...].astype(F32)
        r = lax.rsqrt(jnp.mean(xv * xv, axis=-1, keepdims=True) + EPS)
        o_ref[...] = (xv * r * g_ref[...]).astype(o_ref.dtype)

    return pl.pallas_call(
        body, grid=(s // tm,),
        in_specs=[pl.BlockSpec((tm, w), lambda i: (i, ci)), pl.BlockSpec((1, w), lambda i: (0, 0))],
        out_specs=pl.BlockSpec((tm, w), lambda i: (i, 0)), out_shape=SDS((s, w), BF16), name=name,
        compiler_params=_params(("parallel",), 10 * tm * w * 4))(x, g.reshape(1, w))


def _rms_bwd(x, g, dy, dres=None, *, col=None, name):
    s = x.shape[0]
    w, ci = (x.shape[1], 0) if col is None else col
    tm = min(s, 512)
    has_res = dres is not None

    def body(*refs):
        x_ref, g_ref, dy_ref = refs[:3]
        dx_ref, dg_ref = refs[-2:]
        xv = x_ref[...].astype(F32)
        dyv = dy_ref[...].astype(F32)
        r = lax.rsqrt(jnp.mean(xv * xv, axis=-1, keepdims=True) + EPS)
        u = dyv * g_ref[...]
        dx = r * u - xv * (r * r * r) * jnp.mean(xv * u, axis=-1, keepdims=True)
        if has_res:
            dx = dx + refs[3][...]
        dx_ref[...] = dx

        @pl.when(pl.program_id(0) == 0)
        def _():
            dg_ref[...] = jnp.zeros_like(dg_ref)

        dg_ref[...] += jnp.sum(dyv * xv * r, axis=0, keepdims=True)

    blk = pl.BlockSpec((tm, w), lambda i: (i, 0))
    specs = [pl.BlockSpec((tm, w), lambda i: (i, ci)), pl.BlockSpec((1, w), lambda i: (0, 0)), blk]
    args = [x, g.reshape(1, w), dy]
    if has_res:
        specs.append(blk)
        args.append(dres)
    dx, dg = pl.pallas_call(
        body, grid=(s // tm,), in_specs=specs,
        out_specs=(blk, pl.BlockSpec((1, w), lambda i: (0, 0))),
        out_shape=(SDS((s, w), F32), SDS((1, w), F32)), name=name,
        compiler_params=_params(("arbitrary",), 16 * tm * w * 4))(*args)
    return dx, dg.reshape(w)


def _gated_rms_fwd(y, proj, g, *, name):
    s, w = y.shape
    tm = min(s, 512)

    def body(y_ref, z_ref, g_ref, o_ref):
        z = z_ref[...]
        t = y_ref[...] * (z * _sigmoid(z))
        r = lax.rsqrt(jnp.mean(t * t, axis=-1, keepdims=True) + EPS)
        o_ref[...] = (t * r * g_ref[...]).astype(o_ref.dtype)

    blk = pl.BlockSpec((tm, w), lambda i: (i, 0))
    return pl.pallas_call(
        body, grid=(s // tm,),
        in_specs=[blk, pl.BlockSpec((tm, w), lambda i: (i, P_Z // w)), pl.BlockSpec((1, w), lambda i: (0, 0))],
        out_specs=blk, out_shape=SDS((s, w), BF16), name=name,
        compiler_params=_params(("parallel",), 14 * tm * w * 4))(y, proj, g.reshape(1, w))


def _gated_rms_bwd(y, proj, g, dout, *, name):
    s, w = y.shape
    tm = min(s, 512)

    def body(y_ref, z_ref, g_ref, do_ref, dy_ref, dz_ref, dg_ref):
        z = z_ref[...]
        yv = y_ref[...]
        dov = do_ref[...]
        sg = _sigmoid(z)
        sz = z * sg
        t = yv * sz
        r = lax.rsqrt(jnp.mean(t * t, axis=-1, keepdims=True) + EPS)
        u = dov * g_ref[...]
        dt = r * u - t * (r * r * r) * jnp.mean(t * u, axis=-1, keepdims=True)
        dy_ref[...] = dt * sz
        dz_ref[...] = (dt * yv * (sg * (1.0 + z * (1.0 - sg)))).astype(dz_ref.dtype)

        @pl.when(pl.program_id(0) == 0)
        def _():
            dg_ref[...] = jnp.zeros_like(dg_ref)

        dg_ref[...] += jnp.sum(dov * t * r, axis=0, keepdims=True)

    blk = pl.BlockSpec((tm, w), lambda i: (i, 0))
    vec = pl.BlockSpec((1, w), lambda i: (0, 0))
    dy, dz, dg = pl.pallas_call(
        body, grid=(s // tm,),
        in_specs=[blk, pl.BlockSpec((tm, w), lambda i: (i, P_Z // w)), vec, blk],
        out_specs=(blk, blk, vec), out_shape=(SDS((s, w), F32), SDS((s, w), BF16), SDS((1, w), F32)), name=name,
        compiler_params=_params(("arbitrary",), 24 * tm * w * 4))(y, proj, g.reshape(1, w), dout)
    return dy, dz, dg.reshape(w)


def _final_loss(x, g, target, *, name):
    s, w = x.shape
    tm = min(s, 512)

    def body(x_ref, g_ref, t_ref, loss_ref, dx_ref, dg_ref):
        xv = x_ref[...]
        gv = g_ref[...]
        r = lax.rsqrt(jnp.mean(xv * xv, axis=-1, keepdims=True) + EPS)
        xn = xv * r
        diff = xn * gv - t_ref[...]
        dy = diff * (1.0 / w)
        u = dy * gv
        dx_ref[...] = r * u - xv * (r * r * r) * jnp.mean(xv * u, axis=-1, keepdims=True)

        @pl.when(pl.program_id(0) == 0)
        def _():
            dg_ref[...] = jnp.zeros_like(dg_ref)
            loss_ref[...] = jnp.zeros_like(loss_ref)

        dg_ref[...] += jnp.sum(dy * xn, axis=0, keepdims=True)
        part = jnp.sum(jnp.sum(diff * diff, axis=1, keepdims=True), axis=0, keepdims=True) * (0.5 / w)
        loss_ref[...] += jnp.broadcast_to(part, loss_ref.shape)

    blk = pl.BlockSpec((tm, w), lambda i: (i, 0))
    vec = pl.BlockSpec((1, w), lambda i: (0, 0))
    loss, dx, dg = pl.pallas_call(
        body, grid=(s // tm,), in_specs=[blk, vec, blk],
        out_specs=(pl.BlockSpec((1, LANES), lambda i: (0, 0)), blk, vec),
        out_shape=(SDS((1, LANES), F32), SDS((s, w), F32), SDS((1, w), F32)), name=name,
        compiler_params=_params(("arbitrary",), 16 * tm * w * 4))(x, g.reshape(1, w), target)
    return loss[0, 0], dx, dg.reshape(w)


def _shift_down(x, k):
    if k == 0:
        return x
    row = lax.broadcasted_iota(jnp.int32, x.shape, 0)
    return jnp.where(row < k, 0.0, pltpu.roll(x, k, axis=0))


def _shift_up(x, k):
    if k == 0:
        return x
    s = x.shape[0]
    row = lax.broadcasted_iota(jnp.int32, x.shape, 0)
    return jnp.where(row >= s - k, 0.0, pltpu.roll(x, s - k, axis=0))


def _conv_pre(x, w, b, kw):
    pre = b
    for j in range(kw):
        pre = pre + w[j:j + 1, :] * _shift_down(x, kw - 1 - j)
    return pre


def _conv_bwd_terms(x, w, dpre, kw):
    dx = jnp.zeros_like(x)
    dws = []
    for j in range(kw):
        dx = dx + w[j:j + 1, :] * _shift_up(dpre, kw - 1 - j)
        dws.append(jnp.sum(dpre * _shift_down(x, kw - 1 - j), axis=0, keepdims=True))
    return dx, jnp.concatenate(dws, axis=0), jnp.sum(dpre, axis=0, keepdims=True)


def _ssm_conv_fwd(proj, w, b, *, name):
    s = proj.shape[0]
    cw = 256

    def body(x_ref, w_ref, b_ref, o_ref):
        pre = _conv_pre(x_ref[...], w_ref[...], b_ref[...], SSM_CONV)
        o_ref[...] = pre * _sigmoid(pre)

    return pl.pallas_call(
        body, grid=(CONV_CH // cw,),
        in_specs=[pl.BlockSpec((s, cw), lambda j: (0, j)), pl.BlockSpec((SSM_CONV, cw), lambda j: (0, j)),
                  pl.BlockSpec((1, cw), lambda j: (0, j))],
        out_specs=pl.BlockSpec((s, cw), lambda j: (0, j)), out_shape=SDS((s, CONV_CH), F32), name=name,
        compiler_params=_params(("parallel",), 12 * s * cw * 4))(proj, w, b.reshape(1, CONV_CH))


def _ssm_conv_bwd(proj, w, b, dxbc, *, name):
    s = proj.shape[0]
    cw = 256

    def body(x_ref, w_ref, b_ref, dy_ref, dx_ref, dw_ref, db_ref):
        x = x_ref[...]
        wv = w_ref[...]
        pre = _conv_pre(x, wv, b_ref[...], SSM_CONV)
        sg = _sigmoid(pre)
        dpre = dy_ref[...] * (sg * (1.0 + pre * (1.0 - sg)))
        dx, dw, db = _conv_bwd_terms(x, wv, dpre, SSM_CONV)
        dx_ref[...] = dx.astype(dx_ref.dtype)
        dw_ref[...] = dw
        db_ref[...] = db

    col = pl.BlockSpec((s, cw), lambda j: (0, j))
    wsp = pl.BlockSpec((SSM_CONV, cw), lambda j: (0, j))
    bsp = pl.BlockSpec((1, cw), lambda j: (0, j))
    dx, dw, db = pl.pallas_call(
        body, grid=(CONV_CH // cw,), in_specs=[col, wsp, bsp, col], out_specs=(col, wsp, bsp),
        out_shape=(SDS((s, CONV_CH), BF16), SDS((SSM_CONV, CONV_CH), F32), SDS((1, CONV_CH), F32)), name=name,
        compiler_params=_params(("parallel",), 20 * s * cw * 4))(proj, w, b.reshape(1, CONV_CH), dxbc)
    return dx, dw, db.reshape(CONV_CH)


def _ffn_conv_fwd(up_g, up_v, w, b, *, name):
    s = up_g.shape[0]
    cw = 256
    nb = D_FF // cw

    def body(g_ref, v_ref, wg_ref, wv_ref, bg_ref, bv_ref, o_ref):
        gate = _conv_pre(g_ref[...], wg_ref[...], bg_ref[...], FFN_CONV)
        val = _conv_pre(v_ref[...], wv_ref[...], bv_ref[...], FFN_CONV)
        o_ref[...] = (gate * _sigmoid(gate) * val).astype(o_ref.dtype)

    col = pl.BlockSpec((s, cw), lambda j: (0, j))
    b2 = b.reshape(1, 2 * D_FF)
    return pl.pallas_call(
        body, grid=(nb,),
        in_specs=[col, col, pl.BlockSpec((FFN_CONV, cw), lambda j: (0, j)), pl.BlockSpec((FFN_CONV, cw), lambda j: (0, j + nb)),
                  pl.BlockSpec((1, cw), lambda j: (0, j)), pl.BlockSpec((1, cw), lambda j: (0, j + nb))],
        out_specs=col, out_shape=SDS((s, D_FF), BF16), name=name,
        compiler_params=_params(("parallel",), 16 * s * cw * 4))(up_g, up_v, w, w, b2, b2)


def _ffn_conv_bwd(up_g, up_v, w, b, dact, *, name):
    s = up_g.shape[0]
    cw = 256
    nb = D_FF // cw

    def body(g_ref, v_ref, wg_ref, wv_ref, bg_ref, bv_ref, da_ref, dg_ref, dv_ref, dwg_ref, dwv_ref, dbg_ref, dbv_ref):
        xg, xv = g_ref[...], v_ref[...]
        wg, wv = wg_ref[...], wv_ref[...]
        gate = _conv_pre(xg, wg, bg_ref[...], FFN_CONV)
        val = _conv_pre(xv, wv, bv_ref[...], FFN_CONV)
        da = da_ref[...].astype(F32)
        sg = _sigmoid(gate)
        dgate = da * val * (sg * (1.0 + gate * (1.0 - sg)))
        dval = da * gate * sg
        dxg, dwg, dbg = _conv_bwd_terms(xg, wg, dgate, FFN_CONV)
        dxv, dwv, dbv = _conv_bwd_terms(xv, wv, dval, FFN_CONV)
        dg_ref[...] = dxg.astype(dg_ref.dtype)
        dv_ref[...] = dxv.astype(dv_ref.dtype)
        dwg_ref[...] = dwg
        dwv_ref[...] = dwv
        dbg_ref[...] = dbg
        dbv_ref[...] = dbv

    col = pl.BlockSpec((s, cw), lambda j: (0, j))
    wsp = pl.BlockSpec((FFN_CONV, cw), lambda j: (0, j))
    bsp = pl.BlockSpec((1, cw), lambda j: (0, j))
    b2 = b.reshape(1, 2 * D_FF)
    dg, dv, dwg, dwv, dbg, dbv = pl.pallas_call(
        body, grid=(nb,),
        in_specs=[col, col, wsp, pl.BlockSpec((FFN_CONV, cw), lambda j: (0, j + nb)), bsp,
                  pl.BlockSpec((1, cw), lambda j: (0, j + nb)), col],
        out_specs=(col, col, wsp, wsp, bsp, bsp),
        out_shape=(SDS((s, D_FF), BF16), SDS((s, D_FF), BF16), SDS((FFN_CONV, D_FF), F32), SDS((FFN_CONV, D_FF), F32),
                   SDS((1, D_FF), F32), SDS((1, D_FF), F32)), name=name,
        compiler_params=_params(("parallel",), 32 * s * cw * 4))(up_g, up_v, w, w, b2, b2, dact)
    return dg, dv, jnp.concatenate([dwg, dwv], axis=1), jnp.concatenate([dbg, dbv], axis=1).reshape(2 * D_FF)


def _dot(a, b):
    return jnp.dot(a.astype(BF16), b.astype(BF16), preferred_element_type=F32)


def _dot_nt(a, b):
    return lax.dot_general(a.astype(BF16), b.astype(BF16), (((1,), (1,)), ((), ())), preferred_element_type=F32)


def _dot_tn(a, b):
    return lax.dot_general(a.astype(BF16), b.astype(BF16), (((0,), (0,)), ((), ())), preferred_element_type=F32)


def _ssd_chunk_terms(dtraw, bias, a_log):
    ell = dtraw.shape[0]
    lane = lax.broadcasted_iota(jnp.int32, dtraw.shape, 1)
    valid = lane < SSM_HEADS
    pre = dtraw + bias
    dt = jnp.where(valid, jnp.where(pre > 20.0, pre, jnp.log(1.0 + jnp.exp(jnp.minimum(pre, 20.0)))), 0.0)
    a = -jnp.exp(a_log)
    ad = dt * a
    row = lax.broadcasted_iota(jnp.int32, (ell, ell), 0)
    colm = lax.broadcasted_iota(jnp.int32, (ell, ell), 1)
    tril = row >= colm
    cs = jnp.dot(tril.astype(F32), ad, precision=HIGHEST, preferred_element_type=F32)
    cs_last = cs[ell - 1:ell, :]
    return pre, dt, a, cs, cs_last, tril


def _lane_put(col, h, shape):
    lane = lax.broadcasted_iota(jnp.int32, shape, 1)
    return jnp.where(lane == h, col, 0.0)


def _ssd_fwd(xbc, proj, dt_bias, a_log, d_skip, *, name):
    s = xbc.shape[0]
    nc = s // SSM_CHUNK
    ell, n, p = SSM_CHUNK, SSM_STATE, SSM_HEAD_DIM
    rpg = SSM_HEADS // SSM_GROUPS

    def body(x_ref, dt_ref, bias_ref, alog_ref, dskip_ref, y_ref, ps_ref, state):
        @pl.when(pl.program_id(0) == 0)
        def _():
            state[...] = jnp.zeros_like(state)

        _, dt, _, cs, cs_last, tril = _ssd_chunk_terms(dt_ref[...], bias_ref[...], alog_ref[...])
        e = jnp.exp(cs)
        ds = jnp.exp(cs_last - cs)
        cd = jnp.exp(cs_last)
        cst = cs.T
        dskip = dskip_ref[...]
        ps_ref[0] = state[...]
        for g in range(SSM_GROUPS):
            bg = x_ref[:, D_SSM + n * g:D_SSM + n * (g + 1)]
            cg = x_ref[:, D_SSM + n * (SSM_GROUPS + g):D_SSM + n * (SSM_GROUPS + g + 1)]
            cb = _dot_nt(cg, bg)
            for r in range(rpg):
                h = g * rpg + r
                hs = slice(p * h, p * (h + 1))
                xs = x_ref[:, hs]
                xd = xs * dt[:, h:h + 1]
                lmat = jnp.exp(jnp.where(tril, cs[:, h:h + 1] - cst[h:h + 1, :], -jnp.inf))
                prev = state[:, hs]
                y = _dot(cb * lmat, xd) + _dot(cg, prev) * e[:, h:h + 1] + xs * dskip[:, h:h + 1]
                y_ref[:, hs] = y
                state[:, hs] = prev * cd[:, h:h + 1] + _dot_tn(bg, xd * ds[:, h:h + 1])

    vec = pl.BlockSpec((1, LANES), lambda c: (0, 0))
    return pl.pallas_call(
        body, grid=(nc,),
        in_specs=[pl.BlockSpec((ell, CONV_CH), lambda c: (c, 0)), pl.BlockSpec((ell, LANES), lambda c: (c, P_DT // LANES)),
                  vec, vec, vec],
        out_specs=(pl.BlockSpec((ell, D_SSM), lambda c: (c, 0)), pl.BlockSpec((1, n, D_SSM), lambda c: (c, 0, 0))),
        out_shape=(SDS((s, D_SSM), F32), SDS((nc, n, D_SSM), F32)),
        scratch_shapes=[pltpu.VMEM((n, D_SSM), F32)], name=name,
        compiler_params=_params(("arbitrary",), 24 << 20))(xbc, proj, dt_bias, a_log, d_skip)


def _ssd_bwd(xbc, proj, dt_bias, a_log, d_skip, prev_states, dy, *, name):
    s = xbc.shape[0]
    nc = s // SSM_CHUNK
    ell, n, p = SSM_CHUNK, SSM_STATE, SSM_HEAD_DIM
    rpg = SSM_HEADS // SSM_GROUPS

    def body(x_ref, dt_ref, bias_ref, alog_ref, dskip_ref, ps_ref, dy_ref,
             dx_ref, ddt_ref, dalog_ref, ddskip_ref, dbias_ref, dstate):
        @pl.when(pl.program_id(0) == 0)
        def _():
            dstate[...] = jnp.zeros_like(dstate)
            dalog_ref[...] = jnp.zeros_like(dalog_ref)
            ddskip_ref[...] = jnp.zeros_like(ddskip_ref)
            dbias_ref[...] = jnp.zeros_like(dbias_ref)

        pre, dt, a, cs, cs_last, tril = _ssd_chunk_terms(dt_ref[...], bias_ref[...], alog_ref[...])
        e = jnp.exp(cs)
        ds = jnp.exp(cs_last - cs)
        cd = jnp.exp(cs_last)
        cst = cs.T
        dskip = dskip_ref[...]
        shape = (ell, LANES)
        ddt_acc = jnp.zeros(shape, F32)
        dcs_acc = jnp.zeros(shape, F32)
        dcs_rows = jnp.zeros(shape, F32)
        dlast_acc = jnp.zeros((1, LANES), F32)
        dskip_acc = jnp.zeros((1, LANES), F32)
        for g in range(SSM_GROUPS):
            bsl = slice(D_SSM + n * g, D_SSM + n * (g + 1))
            csl = slice(D_SSM + n * (SSM_GROUPS + g), D_SSM + n * (SSM_GROUPS + g + 1))
            bg = x_ref[:, bsl]
            cg = x_ref[:, csl]
            cb = _dot_nt(cg, bg)
            dcb = jnp.zeros((ell, ell), F32)
            dbg = jnp.zeros((ell, n), F32)
            dcg = jnp.zeros((ell, n), F32)
            for r in range(rpg):
                h = g * rpg + r
                hs = slice(p * h, p * (h + 1))
                xs = x_ref[:, hs]
                dyh = dy_ref[:, hs]
                dt_h, e_h, ds_h, cd_h = dt[:, h:h + 1], e[:, h:h + 1], ds[:, h:h + 1], cd[:, h:h + 1]
                prev = ps_ref[0, :, hs]
                dsn = dstate[:, hs]
                xd = xs * dt_h
                dye = dyh * e_h
                cprev = _dot(cg, prev)
                dprev = dsn * cd_h + _dot_tn(cg, dye)
                dcg = dcg + _dot_nt(dye, prev)
                dcs_h = jnp.sum(dyh * cprev, axis=1, keepdims=True) * e_h
                dcd = jnp.sum(jnp.sum(dsn * prev, axis=1, keepdims=True), axis=0, keepdims=True)
                dlast_h = dcd * cd_h
                dxdd = _dot(bg, dsn)
                dbg = dbg + _dot_nt(xd * ds_h, dsn)
                dxd = dxdd * ds_h
                tmp = jnp.sum(dxdd * xd, axis=1, keepdims=True) * ds_h
                dlast_h = dlast_h + jnp.sum(tmp, axis=0, keepdims=True)
                dcs_h = dcs_h - tmp
                lmat = jnp.exp(jnp.where(tril, cs[:, h:h + 1] - cst[h:h + 1, :], -jnp.inf))
                gm = cb * lmat
                dgm = _dot_nt(dyh, xd)
                dxd = dxd + _dot_tn(gm, dyh)
                mm = dgm * gm
                dcs_h = dcs_h + jnp.sum(mm, axis=1, keepdims=True)
                sub = lax.broadcasted_iota(jnp.int32, shape, 0)
                dcs_rows = dcs_rows + jnp.where(sub == h, jnp.sum(mm, axis=0, keepdims=True), 0.0)
                dcb = dcb + dgm * lmat
                dx_ref[:, hs] = dxd * dt_h + dyh * dskip[:, h:h + 1]
                ddt_acc = ddt_acc + _lane_put(jnp.sum(dxd * xs, axis=1, keepdims=True), h, shape)
                dcs_acc = dcs_acc + _lane_put(dcs_h, h, shape)
                dlast_acc = dlast_acc + _lane_put(dlast_h, h, (1, LANES))
                dskip_acc = dskip_acc + _lane_put(
                    jnp.sum(jnp.sum(dyh * xs, axis=1, keepdims=True), axis=0, keepdims=True), h, (1, LANES))
                dstate[:, hs] = dprev
            dx_ref[:, bsl] = dbg + _dot_tn(dcb, cg)
            dx_ref[:, csl] = dcg + _dot(dcb, bg)
        rowi = lax.broadcasted_iota(jnp.int32, shape, 0)
        dcs = dcs_acc - dcs_rows.T + jnp.where(rowi == ell - 1, dlast_acc, 0.0)
        triu = lax.broadcasted_iota(jnp.int32, (ell, ell), 0) <= lax.broadcasted_iota(jnp.int32, (ell, ell), 1)
        dad = jnp.dot(triu.astype(F32), dcs, precision=HIGHEST, preferred_element_type=F32)
        ddt = ddt_acc + dad * a
        dalog_ref[...] += jnp.sum(dad * dt, axis=0, keepdims=True) * a
        ddskip_ref[...] += dskip_acc
        lane = lax.broadcasted_iota(jnp.int32, shape, 1)
        ddraw = jnp.where(lane < SSM_HEADS, ddt * _sigmoid(pre), 0.0)
        ddt_ref[...] = ddraw.astype(ddt_ref.dtype)
        dbias_ref[...] += jnp.sum(ddraw, axis=0, keepdims=True)

    vec = pl.BlockSpec((1, LANES), lambda c: (0, 0))
    rev = lambda c: nc - 1 - c
    outs = pl.pallas_call(
        body, grid=(nc,),
        in_specs=[pl.BlockSpec((ell, CONV_CH), lambda c: (rev(c), 0)),
                  pl.BlockSpec((ell, LANES), lambda c: (rev(c), P_DT // LANES)), vec, vec, vec,
                  pl.BlockSpec((1, n, D_SSM), lambda c: (rev(c), 0, 0)),
                  pl.BlockSpec((ell, D_SSM), lambda c: (rev(c), 0))],
        out_specs=(pl.BlockSpec((ell, CONV_CH), lambda c: (rev(c), 0)), pl.BlockSpec((ell, LANES), lambda c: (rev(c), 0)),
                   vec, vec, vec),
        out_shape=(SDS((s, CONV_CH), F32), SDS((s, LANES), BF16), SDS((1, LANES), F32), SDS((1, LANES), F32),
                   SDS((1, LANES), F32)),
        scratch_shapes=[pltpu.VMEM((n, D_SSM), F32)], name=name,
        compiler_params=_params(("arbitrary",), 32 << 20))(xbc, proj, dt_bias, a_log, d_skip, prev_states, dy)
    return outs


def _rope_swap(t):
    lane = lax.broadcasted_iota(jnp.int32, t.shape, 1)
    half = QK_ROPE // 2
    lo = (lane >= QK_NOPE) & (lane < QK_NOPE + half)
    hi = (lane >= QK_NOPE + half) & (lane < QK_NOPE + QK_ROPE)
    return jnp.where(lo, pltpu.roll(t, HEAD_PAD - half, axis=1), jnp.where(hi, pltpu.roll(t, half, axis=1), 0.0))


def _mla_prep(q, kv, proj, cos, sins, *, name):
    s = q.shape[0]
    tm = min(s, 256)
    scale = (QK_NOPE + QK_ROPE) ** -0.5

    def body(q_ref, kv_ref, kr_ref, cos_ref, sin_ref, qo_ref, ko_ref, vo_ref):
        cosv, sinv = cos_ref[...], sin_ref[...]
        kr = pltpu.roll(kr_ref[...], QK_NOPE, axis=1)
        lane = lax.broadcasted_iota(jnp.int32, kr.shape, 1)
        nope = lane < QK_NOPE
        kr = jnp.where(nope, 0.0, kr)
        kpe = kr * cosv + _rope_swap(kr) * sinv
        for hp in range(MLA_HEADS // 2):
            vs = []
            for h in (2 * hp, 2 * hp + 1):
                hs = slice(HEAD_PAD * h, HEAD_PAD * (h + 1))
                qh = q_ref[:, hs]
                kvh = kv_ref[:, hs]
                qo_ref[:, hs] = ((qh * cosv + _rope_swap(qh) * sinv) * scale).astype(qo_ref.dtype)
                ko_ref[:, hs] = (jnp.where(nope, kvh, 0.0) + kpe).astype(ko_ref.dtype)
                vs.append(kvh[:, QK_NOPE:])
            vo_ref[:, 2 * V_DIM * hp:2 * V_DIM * (hp + 1)] = jnp.concatenate(vs, axis=1).astype(vo_ref.dtype)

    wide = pl.BlockSpec((tm, MLA_HEADS * HEAD_PAD), lambda i: (i, 0))
    half = pl.BlockSpec((tm, MLA_HEADS * V_DIM), lambda i: (i, 0))
    tab = pl.BlockSpec((tm, LANES), lambda i: (i, 0))
    return pl.pallas_call(
        body, grid=(s // tm,),
        in_specs=[wide, wide, pl.BlockSpec((tm, LANES), lambda i: (i, P_KR // LANES)), tab, tab],
        out_specs=(wide, wide, half),
        out_shape=(SDS((s, MLA_HEADS * HEAD_PAD), BF16), SDS((s, MLA_HEADS * HEAD_PAD), BF16),
                   SDS((s, MLA_HEADS * V_DIM), BF16)), name=name,
        compiler_params=_params(("parallel",), 32 << 20))(q, kv, proj, cos, sins)


def _mla_prep_bwd(dqr, dkr, dv, cos, sins, *, name):
    s = dqr.shape[0]
    tm = min(s, 256)
    scale = (QK_NOPE + QK_ROPE) ** -0.5

    def body(dq_ref, dk_ref, dv_ref, cos_ref, sin_ref, dqo_ref, dkv_ref, dkr_ref):
        cosv, sinv = cos_ref[...], sin_ref[...]
        lane = lax.broadcasted_iota(jnp.int32, cosv.shape, 1)
        ksum = jnp.zeros(cosv.shape, F32)
        for h in range(MLA_HEADS):
            hs = slice(HEAD_PAD * h, HEAD_PAD * (h + 1))
            d = dq_ref[:, hs]
            dk = dk_ref[:, hs]
            dqo_ref[:, hs] = ((d * cosv + _rope_swap(d * sinv)) * scale).astype(dqo_ref.dtype)
            dkv_ref[:, hs] = jnp.concatenate([dk[:, :QK_NOPE], dv_ref[:, V_DIM * h:V_DIM * (h + 1)]], axis=1).astype(dkv_ref.dtype)
            ksum = ksum + dk
        ksum = jnp.where((lane >= QK_NOPE) & (lane < QK_NOPE + QK_ROPE), ksum, 0.0)
        un = ksum * cosv + _rope_swap(ksum * sinv)
        dkr_ref[...] = pltpu.roll(un, HEAD_PAD - QK_NOPE, axis=1).astype(dkr_ref.dtype)

    wide = pl.BlockSpec((tm, MLA_HEADS * HEAD_PAD), lambda i: (i, 0))
    half = pl.BlockSpec((tm, MLA_HEADS * V_DIM), lambda i: (i, 0))
    tab = pl.BlockSpec((tm, LANES), lambda i: (i, 0))
    return pl.pallas_call(
        body, grid=(s // tm,), in_specs=[wide, wide, half, tab, tab], out_specs=(wide, wide, tab),
        out_shape=(SDS((s, MLA_HEADS * HEAD_PAD), BF16), SDS((s, MLA_HEADS * HEAD_PAD), BF16), SDS((s, LANES), BF16)),
        name=name, compiler_params=_params(("parallel",), 40 << 20))(dqr, dkr, dv, cos, sins)


FLASH_TILE = 512


def _flash_fwd(q, k, v, *, name):
    s = q.shape[0]
    t = min(s, FLASH_TILE)
    nq = s // t
    npair = MLA_HEADS // 2

    def body(q_ref, k_ref, v_ref, o_ref, lse_ref):
        i = pl.program_id(1)
        qs = [q_ref[:, HEAD_PAD * e:HEAD_PAD * (e + 1)] for e in range(2)]
        diag = lax.broadcasted_iota(jnp.int32, (t, t), 0) >= lax.broadcasted_iota(jnp.int32, (t, t), 1)

        def step(j, carry, masked):
            rows = pl.ds(pl.multiple_of(j * t, t), t)
            new = []
            for e in range(2):
                m, l, acc = carry[e]
                sc = _dot_nt(qs[e], k_ref[rows, HEAD_PAD * e:HEAD_PAD * (e + 1)])
                if masked:
                    sc = jnp.where(diag, sc, NEG)
                m_new = jnp.maximum(m, jnp.max(sc, axis=1, keepdims=True))
                pr = jnp.exp(sc - m_new)
                alpha = jnp.exp(m - m_new)
                l = alpha * l + jnp.sum(pr, axis=1, keepdims=True)
                acc = alpha * acc + _dot(pr, v_ref[rows, V_DIM * e:V_DIM * (e + 1)])
                new.append((m_new, l, acc))
            return tuple(new)

        init = tuple((jnp.full((t, 1), NEG, F32), jnp.zeros((t, 1), F32), jnp.zeros((t, V_DIM), F32)) for _ in range(2))
        carry = lax.fori_loop(0, i, functools.partial(step, masked=False), init)
        carry = step(i, carry, True)
        o_ref[...] = jnp.concatenate([acc / l for _, l, acc in carry], axis=1)
        lse_ref[0] = jnp.concatenate([jnp.broadcast_to(m + jnp.log(l), (t, V_DIM)) for m, l, _ in carry], axis=1)

    return pl.pallas_call(
        body, grid=(npair, nq),
        in_specs=[pl.BlockSpec((t, 2 * HEAD_PAD), lambda hp, i: (i, hp)), pl.BlockSpec((s, 2 * HEAD_PAD), lambda hp, i: (0, hp)),
                  pl.BlockSpec((s, 2 * V_DIM), lambda hp, i: (0, hp))],
        out_specs=(pl.BlockSpec((t, 2 * V_DIM), lambda hp, i: (i, hp)), pl.BlockSpec((1, t, LANES), lambda hp, i: (hp, i, 0))),
        out_shape=(SDS((s, MLA_HEADS * V_DIM), F32), SDS((npair, s, LANES), F32)), name=name,
        compiler_params=_params(("parallel", "parallel"), 40 << 20))(q, k, v)


def _flash_bwd(q, k, v, o, lse, do, *, name):
    s = q.shape[0]
    t = min(s, FLASH_TILE)
    nq = s // t
    npair = MLA_HEADS // 2

    def body(q_ref, k_ref, v_ref, o_ref, lse_ref, do_ref, dq_ref, dk_ref, dv_ref):
        j = pl.program_id(1)

        @pl.when(j == 0)
        def _():
            dq_ref[...] = jnp.zeros_like(dq_ref)

        qsl = [slice(HEAD_PAD * e, HEAD_PAD * (e + 1)) for e in range(2)]
        vsl = [slice(V_DIM * e, V_DIM * (e + 1)) for e in range(2)]
        ks = [k_ref[:, qsl[e]] for e in range(2)]
        vs = [v_ref[:, vsl[e]] for e in range(2)]
        diag = lax.broadcasted_iota(jnp.int32, (t, t), 0) >= lax.broadcasted_iota(jnp.int32, (t, t), 1)

        def step(i, carry, masked):
            rows = pl.ds(pl.multiple_of(i * t, t), t)
            new = []
            for e in range(2):
                dk, dv = carry[e]
                qi = q_ref[rows, qsl[e]]
                doi = do_ref[rows, vsl[e]]
                delta = jnp.sum(doi * o_ref[rows, vsl[e]], axis=1, keepdims=True)
                lse_i = lse_ref[0, rows, vsl[e]][:, 0:1]
                sc = _dot_nt(qi, ks[e])
                if masked:
                    sc = jnp.where(diag, sc, NEG)
                pr = jnp.exp(sc - lse_i)
                dv = dv + _dot_tn(pr, doi)
                dsc = (pr * (_dot_nt(doi, vs[e]) - delta)).astype(BF16)
                dk = dk + _dot_tn(dsc, qi)
                dq_ref[rows, qsl[e]] += _dot(dsc, ks[e])
                new.append((dk, dv))
            return tuple(new)

        init = tuple((jnp.zeros((t, HEAD_PAD), F32), jnp.zeros((t, V_DIM), F32)) for _ in range(2))
        carry = step(j, init, True)
        carry = lax.fori_loop(j + 1, nq, functools.partial(step, masked=False), carry)
        dk_ref[...] = jnp.concatenate([dk for dk, _ in carry], axis=1)
        dv_ref[...] = jnp.concatenate([dv for _, dv in carry], axis=1)

    full_q = pl.BlockSpec((s, 2 * HEAD_PAD), lambda hp, j: (0, hp))
    full_v = pl.BlockSpec((s, 2 * V_DIM), lambda hp, j: (0, hp))
    blk_k = pl.BlockSpec((t, 2 * HEAD_PAD), lambda hp, j: (j, hp))
    blk_v = pl.BlockSpec((t, 2 * V_DIM), lambda hp, j: (j, hp))
    return pl.pallas_call(
        body, grid=(npair, nq),
        in_specs=[full_q, blk_k, blk_v, full_v, pl.BlockSpec((1, s, LANES), lambda hp, j: (hp, 0, 0)), full_v],
        out_specs=(full_q, blk_k, blk_v),
        out_shape=(SDS((s, MLA_HEADS * HEAD_PAD), F32), SDS((s, MLA_HEADS * HEAD_PAD), F32), SDS((s, MLA_HEADS * V_DIM), F32)),
        name=name, compiler_params=_params(("parallel", "arbitrary"), 48 << 20))(q, k, v, o, lse, do)


def _mem_attn_fwd(q, k, v, *, name):
    s = q.shape[0]
    tm = min(s, 512)
    ml = k.shape[0]
    scale = MEM_HEAD_DIM ** -0.5

    def body(q_ref, k_ref, v_ref, o_ref):
        for h in range(MEM_HEADS):
            hs = slice(MEM_HEAD_DIM * h, MEM_HEAD_DIM * (h + 1))
            sc = _dot_nt(q_ref[:, hs], k_ref[:, hs]) * scale
            pr = jnp.exp(sc - jnp.max(sc, axis=1, keepdims=True))
            pr = pr / jnp.sum(pr, axis=1, keepdims=True)
            o_ref[:, hs] = _dot(pr, v_ref[:, hs]).astype(o_ref.dtype)

    blk = pl.BlockSpec((tm, D_MODEL), lambda i: (i, 0))
    kv = pl.BlockSpec((ml, D_MODEL), lambda i: (0, 0))
    return pl.pallas_call(body, grid=(s // tm,), in_specs=[blk, kv, kv], out_specs=blk,
                          out_shape=SDS((s, D_MODEL), BF16), name=name,
                          compiler_params=_params(("parallel",), 24 << 20))(q, k, v)


def _mem_attn_bwd(q, k, v, do, *, name):
    s = q.shape[0]
    tm = min(s, 512)
    ml = k.shape[0]
    scale = MEM_HEAD_DIM ** -0.5

    def body(q_ref, k_ref, v_ref, do_ref, dq_ref, dk_ref, dv_ref):
        @pl.when(pl.program_id(0) == 0)
        def _():
            dk_ref[...] = jnp.zeros_like(dk_ref)
            dv_ref[...] = jnp.zeros_like(dv_ref)

        for h in range(MEM_HEADS):
            hs = slice(MEM_HEAD_DIM * h, MEM_HEAD_DIM * (h + 1))
            qh, kh, vh, doh = q_ref[:, hs], k_ref[:, hs], v_ref[:, hs], do_ref[:, hs]
            sc = _dot_nt(qh, kh) * scale
            pr = jnp.exp(sc - jnp.max(sc, axis=1, keepdims=True))
            pr = pr / jnp.sum(pr, axis=1, keepdims=True)
            dp = _dot_nt(doh, vh)
            dsc = pr * (dp - jnp.sum(pr * dp, axis=1, keepdims=True)) * scale
            dq_ref[:, hs] = _dot(dsc, kh).astype(dq_ref.dtype)
            dk_ref[:, hs] += _dot_tn(dsc, qh)
            dv_ref[:, hs] += _dot_tn(pr, doh)

    blk = pl.BlockSpec((tm, D_MODEL), lambda i: (i, 0))
    kv = pl.BlockSpec((ml, D_MODEL), lambda i: (0, 0))
    return pl.pallas_call(body, grid=(s // tm,), in_specs=[blk, kv, kv, blk], out_specs=(blk, kv, kv),
                          out_shape=(SDS((s, D_MODEL), BF16), SDS((ml, D_MODEL), F32), SDS((ml, D_MODEL), F32)), name=name,
                          compiler_params=_params(("arbitrary",), 32 << 20))(q, k, v, do)


MATS = (("w_in", (1024, 940), 1), ("w_uq", (384, 384), 1), ("w_ukv", (256, 512), 1), ("w_out", (512, 1024), 0),
        ("w_mq", (256, 1024), 0), ("w_mk", (256, 1024), 0), ("w_mv", (256, 1024), 0), ("w_mo", (256, 1024), 0),
        ("w_up", (1024, 1408), 1), ("w_down", (704, 1024), 0), ("ssm_conv_w", (4, 512), 1), ("ffn_conv_w", (3, 1408), 1))
F32_ON_WIRE = ("ssm_conv_w", "ffn_conv_w")
SMALL = (("norm_mix", 1024), ("ssm_conv_b", 2048), ("dt_bias", 16), ("a_log", 16), ("d_skip", 16), ("ssm_norm", 1024),
         ("q_norm", 384), ("kv_norm", 256), ("attn_out_norm", 1024), ("norm_mem_q", 1024), ("norm_mem_kv", 1024),
         ("norm_ffn", 1024), ("ffn_conv_b", 5632))
PACK_COLS = 1024


def _pad_cols(t, n):
    return jnp.pad(t, ((0, 0),) * (t.ndim - 1) + ((0, n - t.shape[-1]),))


def _w_in_to_padded(t):
    z, xbc, dt, cq, ckv, kr = jnp.split(t, (1024, 3072, 3088, 3472, 3728), axis=-1)
    return jnp.concatenate([xbc, z, cq, _pad_cols(dt, LANES), ckv, _pad_cols(kr, P_IN - P_KR)], axis=-1)


def _w_in_from_padded(t):
    return jnp.concatenate([t[..., P_Z:P_Z + 1024], t[..., P_XBC:P_XBC + 2048], t[..., P_DT:P_DT + SSM_HEADS],
                            t[..., P_CQ:P_CQ + Q_LORA], t[..., P_CKV:P_CKV + KV_LORA], t[..., P_KR:P_KR + QK_ROPE]], axis=-1)


def _cols_joined(g):
    return jnp.concatenate([g[j] for j in range(N_CHIPS)], axis=-1)


def _cols_by_chip(t, dtype):
    k = t.shape[0]
    return t.reshape(k, N_CHIPS, -1).transpose(1, 0, 2).astype(dtype)


def _rows_by_chip(t):
    return t.reshape(N_CHIPS, -1, t.shape[-1])


def _layer_weights(gw):
    wl = {}
    wl["w_in"] = _w_in_to_padded(_cols_joined(gw["w_in"]))
    uq = _cols_joined(gw["w_uq"]).reshape(Q_LORA, MLA_HEADS, QK_NOPE + QK_ROPE)
    wl["w_uq"] = _pad_cols(uq, HEAD_PAD).reshape(Q_LORA, MLA_HEADS * HEAD_PAD)
    wl["w_ukv"] = _cols_joined(gw["w_ukv"])
    wl["ssm_conv_w"] = _cols_joined(gw["ssm_conv_w"])
    wl["ffn_conv_w"] = _cols_joined(gw["ffn_conv_w"])
    return wl


def _layer_fwd(x0, mem, cos, sins, gw, wl, sp, li):
    n = lambda t: f"l{li}_{t}"
    lead = ()
    sv = {"x0": x0}
    h = _rms_fwd(x0, sp["norm_mix"], name=n("mix_norm"))
    proj = _mm(h, wl["w_in"], name=n("mix_proj"))
    xbc = _ssm_conv_fwd(proj, wl["ssm_conv_w"], sp["ssm_conv_b"], name=n("ssm_conv"))
    y, pstates = _ssd_fwd(xbc, proj, sp["dt_bias"], sp["a_log"], sp["d_skip"], name=n("ssd"))
    y_ssm = _gated_rms_fwd(y, proj, sp["ssm_norm"], name=n("ssm_gate"))
    cqn = _rms_fwd(proj, sp["q_norm"], col=(Q_LORA, P_CQ // Q_LORA), name=n("q_norm"))
    ckvn = _rms_fwd(proj, sp["kv_norm"], col=(KV_LORA, P_CKV // KV_LORA), name=n("kv_norm"))
    q = _mm(cqn, wl["w_uq"], name=n("uq"))
    kv = _mm(ckvn, wl["w_ukv"], name=n("ukv"))
    qr, kr, v = _mla_prep(q, kv, proj, cos, sins, name=n("rope"))
    att, lse = _flash_fwd(qr, kr, v, name=n("flash"))
    y_att = _rms_fwd(att, sp["attn_out_norm"], name=n("att_norm"))
    x1 = _mm(y_ssm, gw["w_out"], b_lead=lead, b_rows=(0, D_SSM), res=x0, name=n("out_a"))
    x1 = _mm(y_att, gw["w_out"], b_lead=lead, b_rows=(D_SSM, D_SSM), res=x1, name=n("out_b"))
    sv.update(h=h, proj=proj, xbc=xbc, y=y, pstates=pstates, y_ssm=y_ssm, cqn=cqn, ckvn=ckvn, qr=qr, kr=kr, v=v,
              att=att, lse=lse, y_att=y_att, x1=x1)
    hq = _rms_fwd(x1, sp["norm_mem_q"], name=n("memq_norm"))
    hm = _rms_fwd(mem, sp["norm_mem_kv"], name=n("memkv_norm"))
    mq = _mm(hq, gw["w_mq"], b_lead=lead, out_dtype=BF16, name=n("mq"))
    mk = _mm(hm, gw["w_mk"], b_lead=lead, out_dtype=BF16, name=n("mk"))
    mv = _mm(hm, gw["w_mv"], b_lead=lead, out_dtype=BF16, name=n("mv"))
    mo = _mem_attn_fwd(mq, mk, mv, name=n("mem_attn"))
    x2 = _mm(mo, gw["w_mo"], b_lead=lead, res=x1, name=n("mo"))
    sv.update(hq=hq, hm=hm, mq=mq, mk=mk, mv=mv, mo=mo, x2=x2)
    hf = _rms_fwd(x2, sp["norm_ffn"], name=n("ffn_norm"))
    up_g = _mm(hf, gw["w_up"], b_lead=lead, b_chips=(0, 2), name=n("up_g"))
    up_v = _mm(hf, gw["w_up"], b_lead=lead, b_chips=(2, 2), name=n("up_v"))
    act = _ffn_conv_fwd(up_g, up_v, wl["ffn_conv_w"], sp["ffn_conv_b"], name=n("ffn_conv"))
    x3 = _mm(act, gw["w_down"], b_lead=lead, res=x2, name=n("down"))
    sv.update(hf=hf, up_g=up_g, up_v=up_v, act=act)
    return x3, sv


def _layer_bwd(dx3, mem, cos, sins, gw, wl, sp, sv, li):
    n = lambda t: f"l{li}_b_{t}"
    lead = ()
    g = {}
    dact = _mm(dx3, gw["w_down"], tb=True, b_lead=lead, out_dtype=BF16, name=n("down_dx"))
    g["w_down"] = _rows_by_chip(_mm(sv["act"], dx3, ta=True, out_dtype=BF16, name=n("down_dw")))
    dup_g, dup_v, dcw, g["ffn_conv_b"] = _ffn_conv_bwd(
        sv["up_g"], sv["up_v"], wl["ffn_conv_w"], sp["ffn_conv_b"], dact, name=n("ffn_conv"))
    g["ffn_conv_w"] = _cols_by_chip(dcw, F32)
    nsh = MATS[8][1][1]
    dhf = None
    for c4 in range(N_CHIPS):
        dhf = _mm(dup_g if c4 < 2 else dup_v, gw["w_up"], tb=True, a_col=(nsh, c4 % 2), b_lead=(c4,), res=dhf,
                  name=n(f"up{c4}_dx"))
    g["w_up"] = jnp.concatenate([_mm(sv["hf"], dup_g, ta=True, o_chips=nsh, out_dtype=BF16, name=n("upg_dw")),
                                 _mm(sv["hf"], dup_v, ta=True, o_chips=nsh, out_dtype=BF16, name=n("upv_dw"))], axis=0)
    dx2, g["norm_ffn"] = _rms_bwd(sv["x2"], sp["norm_ffn"], dhf, dx3, name=n("ffn_norm"))
    dmo = _mm(dx2, gw["w_mo"], tb=True, b_lead=lead, out_dtype=BF16, name=n("mo_dx"))
    g["w_mo"] = _rows_by_chip(_mm(sv["mo"], dx2, ta=True, out_dtype=BF16, name=n("mo_dw")))
    dmq, dmk, dmv = _mem_attn_bwd(sv["mq"], sv["mk"], sv["mv"], dmo, name=n("mem_attn"))
    dhq = _mm(dmq, gw["w_mq"], tb=True, b_lead=lead, name=n("mq_dx"))
    g["w_mq"] = _rows_by_chip(_mm(sv["hq"], dmq, ta=True, out_dtype=BF16, name=n("mq_dw")))
    dhm = _mm(dmk, gw["w_mk"], tb=True, b_lead=lead, name=n("mk_dx"))
    dhm = _mm(dmv, gw["w_mv"], tb=True, b_lead=lead, res=dhm, name=n("mv_dx"))
    g["w_mk"] = _rows_by_chip(_mm(sv["hm"], dmk, ta=True, out_dtype=BF16, name=n("mk_dw")))
    g["w_mv"] = _rows_by_chip(_mm(sv["hm"], dmv, ta=True, out_dtype=BF16, name=n("mv_dw")))
    dx1, g["norm_mem_q"] = _rms_bwd(sv["x1"], sp["norm_mem_q"], dhq, dx2, name=n("memq_norm"))
    _, g["norm_mem_kv"] = _rms_bwd(mem, sp["norm_mem_kv"], dhm, name=n("memkv_norm"))
    dy_ssm = _mm(dx1, gw["w_out"], tb=True, b_lead=lead, b_rows=(0, D_SSM), name=n("outa_dx"))
    dy_att = _mm(dx1, gw["w_out"], tb=True, b_lead=lead, b_rows=(D_SSM, D_SSM), name=n("outb_dx"))
    g["w_out"] = _rows_by_chip(jnp.concatenate([_mm(sv["y_ssm"], dx1, ta=True, out_dtype=BF16, name=n("outa_dw")),
                                                _mm(sv["y_att"], dx1, ta=True, out_dtype=BF16, name=n("outb_dw"))], axis=0))
    datt, g["attn_out_norm"] = _rms_bwd(sv["att"], sp["attn_out_norm"], dy_att, name=n("att_norm"))
    dqr, dkr, dv = _flash_bwd(sv["qr"], sv["kr"], sv["v"], sv["att"], sv["lse"], datt, name=n("flash"))
    dq, dkv, dkrope = _mla_prep_bwd(dqr, dkr, dv, cos, sins, name=n("rope"))
    duq = _mm(sv["cqn"], dq, ta=True, name=n("uq_dw")).reshape(Q_LORA, MLA_HEADS, HEAD_PAD)[..., :QK_NOPE + QK_ROPE]
    g["w_uq"] = _cols_by_chip(duq.reshape(Q_LORA, -1), BF16)
    dcqn = _mm(dq, wl["w_uq"], tb=True, name=n("uq_dx"))
    g["w_ukv"] = _cols_by_chip(_mm(sv["ckvn"], dkv, ta=True, name=n("ukv_dw")), BF16)
    dckvn = _mm(dkv, wl["w_ukv"], tb=True, name=n("ukv_dx"))
    proj = sv["proj"]
    dcq, g["q_norm"] = _rms_bwd(proj, sp["q_norm"], dcqn, col=(Q_LORA, P_CQ // Q_LORA), name=n("q_norm"))
    dckv, g["kv_norm"] = _rms_bwd(proj, sp["kv_norm"], dckvn, col=(KV_LORA, P_CKV // KV_LORA), name=n("kv_norm"))
    dy, dz, g["ssm_norm"] = _gated_rms_bwd(sv["y"], proj, sp["ssm_norm"], dy_ssm, name=n("ssm_gate"))
    dxbc, ddt, da_log, dd_skip, ddt_bias = _ssd_bwd(
        sv["xbc"], proj, sp["dt_bias"], sp["a_log"], sp["d_skip"], sv["pstates"], dy, name=n("ssd"))
    g["a_log"], g["d_skip"], g["dt_bias"] = da_log[0, :SSM_HEADS], dd_skip[0, :SSM_HEADS], ddt_bias[0, :SSM_HEADS]
    dxbc_pre, dsw, g["ssm_conv_b"] = _ssm_conv_bwd(proj, wl["ssm_conv_w"], sp["ssm_conv_b"], dxbc, name=n("ssm_conv"))
    g["ssm_conv_w"] = _cols_by_chip(dsw, F32)
    s = proj.shape[0]
    dproj = jnp.concatenate([dxbc_pre, dz, dcq.astype(BF16), ddt, dckv.astype(BF16), dkrope,
                             jnp.zeros((s, P_IN - P_KR - LANES), BF16)], axis=1)
    dh = _mm(dproj, wl["w_in"], tb=True, name=n("proj_dx"))
    g["w_in"] = _cols_by_chip(_w_in_from_padded(_mm(sv["h"], dproj, ta=True, name=n("proj_dw"))), BF16)
    dx0, g["norm_mix"] = _rms_bwd(sv["x0"], sp["norm_mix"], dh, dx1, name=n("mix_norm"))
    return dx0, g


def _chip_peers(x, y):
    return [(1 - x, y), (x, 1 - y), (1 - x, 1 - y)]


HBM_SPEC = pl.BlockSpec(memory_space=pltpu.HBM)
SEM_SPEC = pl.BlockSpec(memory_space=pltpu.SEMAPHORE)
ANY_SPEC = pl.BlockSpec(memory_space=pl.ANY)
VMEM_SPEC = pl.BlockSpec(memory_space=pltpu.VMEM)
DATAFLOW = pltpu.SideEffectType.DATAFLOW_SIDE_EFFECTING
TOKEN_SHAPE = (8, LANES)


def _place_own(srcs, land_shapes, src_view, dst_view, *, name):
    n = len(srcs)

    def body(*refs):
        s, o, sems = refs[:n], refs[n:2 * n], refs[2 * n]
        me = 2 * lax.axis_index("x") + lax.axis_index("y")
        cps = [pltpu.make_async_copy(src_view(t, s[t], me), dst_view(t, o[t], me), sems.at[t]) for t in range(n)]
        for cp in cps:
            cp.start()
        for cp in cps:
            cp.wait()

    return pl.pallas_call(body, in_specs=[ANY_SPEC] * n, out_specs=[ANY_SPEC] * n, out_shape=land_shapes,
                          scratch_shapes=[pltpu.SemaphoreType.DMA((n,))], name=name)(*srcs)


def _exchange_start(srcs, lands, src_view, dst_view, token, *, name):
    n = len(srcs)

    def body(*refs):
        s, l, tok_in = refs[:n], refs[n:2 * n], refs[2 * n]
        send_sems, recv_sems = refs[2 * n + 1], refs[2 * n + 2]
        tok_out = refs[-1]
        x, y, c = lax.axis_index("x"), lax.axis_index("y"), lax.axis_index("c")
        me = 2 * x + y
        for k, (px, py) in enumerate(_chip_peers(x, y)):
            chip = 2 * px + py
            for t in range(n):
                pltpu.make_async_remote_copy(
                    src_ref=src_view(t, s[t], chip), dst_ref=dst_view(t, l[t], me), send_sem=send_sems.at[3 * t + k],
                    recv_sem=recv_sems.at[3 * t + k], device_id=(px, py, c), device_id_type=MESH).start()
        tok_out[...] = tok_in[...]

    hbm = lambda t: pltpu.with_memory_space_constraint(t, pltpu.HBM)
    outs = pl.pallas_call(
        body, name=name,
        out_shape=(pltpu.SemaphoreType.DMA((3 * n,)), pltpu.SemaphoreType.DMA((3 * n,)),
                   *[pltpu.HBM(l.shape, l.dtype) for l in lands], SDS(TOKEN_SHAPE, F32)),
        in_specs=[HBM_SPEC] * (2 * n) + [VMEM_SPEC], out_specs=(SEM_SPEC, SEM_SPEC, *[HBM_SPEC] * n, VMEM_SPEC),
        input_output_aliases={n + t: 2 + t for t in range(n)},
        compiler_params=pltpu.CompilerParams(has_side_effects=DATAFLOW))(*[hbm(t) for t in srcs], *[hbm(t) for t in lands], token)
    return outs[0], outs[1], list(outs[2:2 + n]), outs[-1]


def _exchange_wait(srcs, lands, send_sems, recv_sems, after, src_view, dst_view, *, name):
    n = len(srcs)

    def body(*refs):
        s, l = refs[:n], refs[n:2 * n]
        send_ref, recv_ref = refs[2 * n], refs[2 * n + 1]
        x, y, c = lax.axis_index("x"), lax.axis_index("y"), lax.axis_index("c")
        for k, (px, py) in enumerate(_chip_peers(x, y)):
            chip = 2 * px + py
            for t in range(n):
                cp = pltpu.make_async_remote_copy(
                    src_ref=src_view(t, s[t], chip), dst_ref=dst_view(t, l[t], chip), send_sem=send_ref.at[3 * t + k],
                    recv_sem=recv_ref.at[3 * t + k], device_id=(px, py, c), device_id_type=MESH)
                cp.wait_send()
                cp.wait_recv()

    outs = pl.pallas_call(
        body, name=name, out_shape=[pltpu.HBM(l.shape, l.dtype) for l in lands],
        in_specs=[HBM_SPEC] * (2 * n) + [SEM_SPEC, SEM_SPEC, ANY_SPEC], out_specs=[HBM_SPEC] * n,
        input_output_aliases={n + t: t for t in range(n)},
        compiler_params=pltpu.CompilerParams(has_side_effects=DATAFLOW))(*srcs, *lands, send_sems, recv_sems, after)
    return list(outs)


def _gather_layer_start(shards, li, token):
    src_view = lambda t, ref, chip: ref.at[li]
    dst_view = lambda t, ref, chip: ref.at[chip]
    lands = _place_own(shards, [SDS((N_CHIPS,) + s.shape[1:], s.dtype) for s in shards], src_view, dst_view,
                       name=f"gather{li}_own")
    send_sems, recv_sems, lands, token = _exchange_start(shards, lands, src_view, dst_view, token, name=f"gather{li}_start")
    return (shards, lands, send_sems, recv_sems, src_view, dst_view, f"gather{li}_wait"), token


def _scatter_layer_start(grads, li, token):
    view = lambda t, ref, chip: ref.at[chip]
    lands = _place_own(grads, [SDS(g.shape, g.dtype) for g in grads], view, view, name=f"scatter{li}_own")
    send_sems, recv_sems, lands, token = _exchange_start(grads, lands, view, view, token, name=f"scatter{li}_start")
    return (grads, lands, send_sems, recv_sems, view, view, f"scatter{li}_wait"), token


def _exchange_finish(handle, after):
    srcs, lands, send_sems, recv_sems, src_view, dst_view, name = handle
    return _exchange_wait(srcs, lands, send_sems, recv_sems, after, src_view, dst_view, name=name)


def _swap_cores(bufs, *, name):
    n = len(bufs)

    def body(*refs):
        srcs, outs = refs[:n], refs[n:2 * n]
        send_sems, recv_sems = refs[2 * n:]
        x, y, c = lax.axis_index("x"), lax.axis_index("y"), lax.axis_index("c")
        cps = [pltpu.make_async_remote_copy(src_ref=srcs[t], dst_ref=outs[t], send_sem=send_sems.at[t], recv_sem=recv_sems.at[t],
                                            device_id=(x, y, 1 - c), device_id_type=MESH) for t in range(n)]
        for cp in cps:
            cp.start()
        for cp in cps:
            cp.wait()

    any_spec = pl.BlockSpec(memory_space=pl.ANY)
    return pl.pallas_call(body, in_specs=[any_spec] * n, out_specs=[any_spec] * n,
                          out_shape=[SDS(b.shape, b.dtype) for b in bufs],
                          scratch_shapes=[pltpu.SemaphoreType.DMA((n,)), pltpu.SemaphoreType.DMA((n,))], name=name)(*bufs)


def _all_gather8(src, *, name):
    def body(src_ref, out_ref, send_sems, recv_sems, local_sem):
        x, y, c = lax.axis_index("x"), lax.axis_index("y"), lax.axis_index("c")
        me = 4 * x + 2 * y + c
        mine = pltpu.make_async_copy(src_ref, out_ref.at[me], local_sem)
        mine.start()

        def peer(k):
            return (x ^ (k >> 2 & 1), y ^ (k >> 1 & 1), c ^ (k & 1))

        sends = []
        for k in range(1, N_DEV):
            cp = pltpu.make_async_remote_copy(src_ref=src_ref, dst_ref=out_ref.at[me], send_sem=send_sems.at[k - 1],
                                              recv_sem=recv_sems.at[k - 1], device_id=peer(k), device_id_type=MESH)
            cp.start()
            sends.append(cp)
        for k in range(1, N_DEV):
            px, py, pc = peer(k)
            pltpu.make_async_remote_copy(src_ref=src_ref, dst_ref=out_ref.at[4 * px + 2 * py + pc],
                                         send_sem=send_sems.at[k - 1], recv_sem=recv_sems.at[k - 1],
                                         device_id=peer(k), device_id_type=MESH).wait_recv()
        for cp in sends:
            cp.wait_send()
        mine.wait()

    any_spec = pl.BlockSpec(memory_space=pl.ANY)
    return pl.pallas_call(
        body, in_specs=[any_spec], out_specs=any_spec, out_shape=SDS((N_DEV,) + src.shape, src.dtype),
        scratch_shapes=[pltpu.SemaphoreType.DMA((N_DEV - 1,)), pltpu.SemaphoreType.DMA((N_DEV - 1,)), pltpu.SemaphoreType.DMA],
        name=name)(src)


def _adam_terms(w, g, m, v):
    m = ADAM_B1 * m + (1.0 - ADAM_B1) * g
    v = ADAM_B2 * v + (1.0 - ADAM_B2) * (g * g)
    m_hat = m / (1.0 - ADAM_B1 ** ADAM_STEP)
    v_hat = v / (1.0 - ADAM_B2 ** ADAM_STEP)
    delta = -ADAM_LR * (m_hat / (jnp.sqrt(v_hat) + ADAM_EPS) + ADAM_WD * w)
    return delta, m, v


def _adamw_shard(mine, other, w, m, v, *, name):
    d, a, b = w.shape
    tr = next((t for t in (128, 64, 32, 16) if a % t == 0), a)

    def body(*refs):
        ga, gb = refs[:d], refs[d:2 * d]
        w_ref, m_ref, v_ref, g_ref, d_ref, nm_ref, nv_ref = refs[2 * d:]

        def plane(ref):
            return ((ref[0].astype(F32) + ref[1].astype(F32)) + ref[2].astype(F32)) + ref[3].astype(F32)

        for lp in range(d):
            @pl.when(pl.program_id(0) == lp)
            def _(lp=lp):
                g = plane(ga[lp]) + plane(gb[lp])
                delta, mn, vn = _adam_terms(w_ref[...], g, m_ref[...], v_ref[...])
                g_ref[...] = g
                d_ref[...] = delta
                nm_ref[...] = mn
                nv_ref[...] = vn

    gspecs = [pl.BlockSpec((N_CHIPS, tr, b), lambda l, i, lp=lp: (0, jnp.where(l == lp, i, 0), 0)) for lp in range(d)]
    blk = pl.BlockSpec((None, tr, b), lambda l, i: (l, i, 0))
    shp = SDS((d, a, b), F32)
    return pl.pallas_call(
        body, grid=(d, a // tr), in_specs=gspecs + gspecs + [blk, blk, blk], out_specs=(blk,) * 4, out_shape=(shp,) * 4,
        name=name, compiler_params=_params(("arbitrary", "arbitrary"), 48 << 20))(*mine, *other, w, m, v)


def _adamw_small(g8, w, m, v, *, name):
    n = w.shape[1]

    def body(g8_ref, w_ref, m_ref, v_ref, g_ref, d_ref, nm_ref, nv_ref):
        g = g8_ref[0]
        for k in range(1, N_DEV):
            g = g + g8_ref[k]
        delta, mn, vn = _adam_terms(w_ref[...], g, m_ref[...], v_ref[...])
        g_ref[...] = g
        d_ref[...] = delta
        nm_ref[...] = mn
        nv_ref[...] = vn

    shp = SDS((1, n), F32)
    return pl.pallas_call(body, out_shape=(shp,) * 4, name=name, compiler_params=_params(None, 24 << 20))(g8, w, m, v)


def _rope_tables(positions):
    inv_freq = 1.0 / (ROPE_THETA ** (jnp.arange(0, QK_ROPE, 2, dtype=F32) / QK_ROPE))
    ang = positions.astype(F32)[:, None] * inv_freq
    c, s = jnp.cos(ang), jnp.sin(ang)
    n = positions.shape[0]
    pad = jnp.zeros((n, HEAD_PAD - QK_NOPE - QK_ROPE), F32)
    cos = jnp.concatenate([jnp.ones((n, QK_NOPE), F32), c, c, pad], axis=1)
    sins = jnp.concatenate([jnp.zeros((n, QK_NOPE), F32), -s, s, pad], axis=1)
    return cos, sins


def _pad_lanes(v):
    return _pad_cols(v.reshape(1, -1), LANES)


def _local_step(x, mem, positions, layer_weights, small, final_norm, loss_target, on_layer_grads):
    cos, sins = _rope_tables(positions)
    saved, gws, wls, sps = [], [], [], []
    h = x
    for li in range(DEPTH):
        gw = layer_weights(li, h)
        wl = _layer_weights(gw)
        sp = {k: small[k][li] for k, _ in SMALL}
        for k in ("dt_bias", "a_log", "d_skip"):
            sp[k] = _pad_lanes(sp[k])
        h, sv = _layer_fwd(h, mem, cos, sins, gw, wl, sp, li)
        saved.append(sv)
        gws.append(gw)
        wls.append(wl)
        sps.append(sp)
    loss, dh, g_final = _final_loss(h, final_norm, loss_target, name="final_loss")
    grads = [None] * DEPTH
    for li in reversed(range(DEPTH)):
        dh, grads[li] = _layer_bwd(dh, mem, cos, sins, gws[li], wls[li], sps[li], saved[li], li)
        on_layer_grads(li, grads[li])
    return loss, dh, grads, g_final


def _gathered_views(lands):
    return {k: (t.reshape(-1, t.shape[-1]) if axis == 0 else t) for (k, _, axis), t in zip(MATS, lands)}


def kernel(x, mem, positions, norm_mix, w_in, ssm_conv_w, ssm_conv_b, dt_bias, a_log, d_skip, ssm_norm, q_norm, w_uq, kv_norm, w_ukv, attn_out_norm, w_out, norm_mem_q, norm_mem_kv, w_mq, w_mk, w_mv, w_mo, norm_ffn, w_up, ffn_conv_w, ffn_conv_b, w_down, final_norm, loss_target, m_norm_mix, m_w_in, m_ssm_conv_w, m_ssm_conv_b, m_dt_bias, m_a_log, m_d_skip, m_ssm_norm, m_q_norm, m_w_uq, m_kv_norm, m_w_ukv, m_attn_out_norm, m_w_out, m_norm_mem_q, m_norm_mem_kv, m_w_mq, m_w_mk, m_w_mv, m_w_mo, m_norm_ffn, m_w_up, m_ffn_conv_w, m_ffn_conv_b, m_w_down, m_final_norm, v_norm_mix, v_w_in, v_ssm_conv_w, v_ssm_conv_b, v_dt_bias, v_a_log, v_d_skip, v_ssm_norm, v_q_norm, v_w_uq, v_kv_norm, v_w_ukv, v_attn_out_norm, v_w_out, v_norm_mem_q, v_norm_mem_kv, v_w_mq, v_w_mk, v_w_mv, v_w_mo, v_norm_ffn, v_w_up, v_ffn_conv_w, v_ffn_conv_b, v_w_down, v_final_norm):
    args = dict(locals())
    names = ["norm_mix", "w_in", "ssm_conv_w", "ssm_conv_b", "dt_bias", "a_log", "d_skip", "ssm_norm", "q_norm", "w_uq",
             "kv_norm", "w_ukv", "attn_out_norm", "w_out", "norm_mem_q", "norm_mem_kv", "w_mq", "w_mk", "w_mv", "w_mo",
             "norm_ffn", "w_up", "ffn_conv_w", "ffn_conv_b", "w_down", "final_norm"]
    wts = {k: args[k] for k in names}
    mom = {k: args["m_" + k] for k in names}
    var = {k: args["v_" + k] for k in names}
    mat_names = [k for k, _, _ in MATS]

    shards = [wts[k] if k in F32_ON_WIRE else wts[k].astype(BF16) for k in mat_names]
    token = jnp.zeros(TOKEN_SHAPE, F32)
    gathers = []
    for li in range(DEPTH):
        handle, token = _gather_layer_start(shards, li, token)
        gathers.append(handle)
    small = {k: wts[k] for k, _ in SMALL}

    scatters = [None] * DEPTH

    def on_layer_grads(li, g):
        scatters[li], _ = _scatter_layer_start([g[k] for k in mat_names], li, jnp.zeros(TOKEN_SHAPE, F32))

    loss, grad_x, grads, g_final = _local_step(
        x[0], mem[0], positions[0], lambda li, after: _gathered_views(_exchange_finish(gathers[li], after)), small,
        wts["final_norm"], loss_target[0], on_layer_grads)
    loss = lax.psum(loss, ("x", "y", "c"))

    mine = [_exchange_finish(scatters[li], grad_x) for li in range(DEPTH)]
    nm = len(mat_names)
    swapped = _swap_cores([b for layer in mine for b in layer], name="swap_cores")
    other = [swapped[li * nm:(li + 1) * nm] for li in range(DEPTH)]
    mat_out = {k: _adamw_shard([mine[li][t] for li in range(DEPTH)], [other[li][t] for li in range(DEPTH)],
                               wts[k], mom[k], var[k], name=f"adamw_{k}") for t, k in enumerate(mat_names)}

    def pack_small(get, fin):
        flat = [get(k).reshape(-1) for k, _ in SMALL] + [fin.reshape(-1)]
        n = sum(f.shape[0] for f in flat)
        return jnp.concatenate(flat + [jnp.zeros((-n % PACK_COLS,), F32)]).reshape(1, -1)

    gs = pack_small(lambda k: jnp.stack([grads[li][k] for li in range(DEPTH)]), g_final)
    g8 = _all_gather8(gs, name="gather_small_grads")
    small_out = _adamw_small(g8, pack_small(wts.get, wts["final_norm"]), pack_small(mom.get, mom["final_norm"]),
                             pack_small(var.get, var["final_norm"]), name="adamw_small")

    def unpack_small(buf):
        out, off = {}, 0
        for k, nel in SMALL:
            out[k] = buf[0, off:off + DEPTH * nel].reshape(DEPTH, nel)
            off += DEPTH * nel
        out["final_norm"] = buf[0, off:off + D_MODEL]
        return out

    small_res = [unpack_small(b) for b in small_out]
    res = []
    for kind in range(4):
        for k in names:
            res.append(small_res[kind][k] if k in small_res[kind] else mat_out[k][kind])
    return (loss, grad_x[None], *res)
```

```python
import functools
import math

import jax
import jax.numpy as jnp
from jax import lax
from jax.experimental import pallas as pl
from jax.experimental.pallas import tpu as pltpu

F32 = jnp.float32
BF16 = jnp.bfloat16
HIGHEST = lax.Precision.HIGHEST
SDS = jax.ShapeDtypeStruct
MESH = pl.DeviceIdType.MESH

D_MODEL = 1024
DEPTH = 4
EPS = 1e-6
SSM_HEADS = 16
SSM_HEAD_DIM = 64
D_SSM = 1024
SSM_GROUPS = 4
SSM_STATE = 128
SSM_CONV = 4
SSM_CHUNK = 128
CONV_CH = 2048
MLA_HEADS = 16
QK_NOPE = 64
QK_ROPE = 32
V_DIM = 64
Q_LORA = 384
KV_LORA = 256
ROPE_THETA = 10000.0
MEM_HEADS = 4
MEM_HEAD_DIM = 256
D_FF = 2816
FFN_CONV = 3
D_IN = 3760
ADAM_LR = 0.001
ADAM_B1 = 0.9
ADAM_B2 = 0.999
ADAM_EPS = 1e-08
ADAM_WD = 0.01
ADAM_STEP = 10

LANES = 128
HEAD_PAD = 128
N_CHIPS = 4
N_DEV = 8
VMEM_CAP_MB = 56

P_XBC, P_Z, P_CQ, P_DT, P_CKV, P_KR, P_IN = 0, 2048, 3072, 3456, 3584, 3840, 4096
NEG = -1e30


def _tile(n, pref):
    t = (min(n, pref) // LANES) * LANES
    while t >= LANES:
        if n % t == 0:
            return t
        t -= LANES
    return n


def _params(sem=None, vmem_bytes=None):
    kw = {}
    if sem is not None:
        kw["dimension_semantics"] = sem
    if vmem_bytes is not None:
        kw["vmem_limit_bytes"] = int(min(max(vmem_bytes, 16 << 20), VMEM_CAP_MB << 20))
    return pltpu.CompilerParams(**kw)


def _nbytes(shape, dtype):
    return math.prod(shape) * jnp.dtype(dtype).itemsize


def _mm(a, b, *, ta=False, tb=False, res=None, out_dtype=F32, name, a_col=None, b_lead=(), b_rows=None,
        b_chips=None, o_chips=None):
    if ta:
        k, m = a.shape
    else:
        m, k = (a.shape[0], a.shape[1] if a_col is None else a_col[0])
    rows_b, cols_b = b.shape[-2:]
    row0 = 0
    if b_rows is not None:
        row0, rows_b = b_rows
    nlead = len(b_lead)
    if b_chips is not None:
        assert not tb
        kb, tn, n = rows_b, cols_b, b_chips[1] * cols_b
        b_blk = (None,) * (1 + nlead) + (kb, tn)
        b_map = lambda i, j: (b_chips[0] + j,) + tuple(b_lead) + (0, 0)
    elif tb:
        n, kb = rows_b, cols_b
        tn = _tile(n, 512)
        assert row0 % tn == 0
        b_blk = (None,) * nlead + (tn, kb)
        b_map = lambda i, j: tuple(b_lead) + (j + row0 // tn, 0)
    else:
        kb, n = rows_b, cols_b
        tn = o_chips if o_chips else _tile(n, 512)
        assert row0 % kb == 0
        b_blk = (None,) * nlead + (kb, tn)
        b_map = lambda i, j: tuple(b_lead) + (row0 // kb, j)
    assert k == kb, (a.shape, b.shape, ta, tb, k, kb)
    tm = _tile(m, 512)
    if ta:
        a_blk, a_map = (k, tm), (lambda i, j: (0, i))
    else:
        a_blk, a_map = (tm, k), ((lambda i, j: (i, 0)) if a_col is None else (lambda i, j: (i, a_col[1])))
    if o_chips:
        o_spec = pl.BlockSpec((None, tm, tn), lambda i, j: (j, i, 0))
        o_shape = SDS((n // tn, m, tn), out_dtype)
    else:
        o_spec = pl.BlockSpec((tm, tn), lambda i, j: (i, j))
        o_shape = SDS((m, n), out_dtype)
    dims = (((0 if ta else 1,), (1 if tb else 0,)), ((), ()))
    has_res = res is not None

    def body(*refs):
        a_ref, b_ref = refs[0], refs[1]
        o_ref = refs[-1]
        acc = lax.dot_general(a_ref[...].astype(BF16), b_ref[...].astype(BF16), dims, preferred_element_type=F32)
        if has_res:
            acc = acc + refs[2][...]
        o_ref[...] = acc.astype(o_ref.dtype)

    bb = tuple(d for d in b_blk if d is not None)
    vmem = 2 * (_nbytes(a_blk, a.dtype) + _nbytes(bb, b.dtype) + (2 if has_res else 1) * _nbytes((tm, tn), F32))
    vmem += _nbytes(a_blk, BF16) + _nbytes(bb, BF16) + 2 * _nbytes((tm, tn), F32) + (4 << 20)
    args = (a, b) + ((res,) if has_res else ())
    specs = [pl.BlockSpec(a_blk, a_map), pl.BlockSpec(b_blk, b_map)] + ([o_spec] if has_res else [])
    return pl.pallas_call(body, grid=(m // tm, n // tn), in_specs=specs, out_specs=o_spec, out_shape=o_shape, name=name,
                          compiler_params=_params(("parallel", "parallel"), vmem))(*args)


def _sigmoid(x):
    return 1.0 / (1.0 + jnp.exp(-x))


def _rms_fwd(x, g, *, col=None, name):
    s = x.shape[0]
    w, ci = (x.shape[1], 0) if col is None else col
    tm = min(s, 512)

    def body(x_ref, g_ref, o_ref):
        xv = x_ref[...].astype(F32)
        r = lax.rsqrt(jnp.mean(xv * xv, axis=-1, keepdims=True) + EPS)
        o_ref[...] = (xv * r * g_ref[...]).astype(o_ref.dtype)

    return pl.pallas_call(
        body, grid=(s // tm,),
        in_specs=[pl.BlockSpec((tm, w), lambda i: (i, ci)), pl.BlockSpec((1, w), lambda i: (0, 0))],
        out_specs=pl.BlockSpec((tm, w), lambda i: (i, 0)), out_shape=SDS((s, w), BF16), name=name,
        compiler_params=_params(("parallel",), 10 * tm * w * 4))(x, g.reshape(1, w))


def _rms_bwd(x, g, dy, dres=None, *, col=None, name):
    s = x.shape[0]
    w, ci = (x.shape[1], 0) if col is None else col
    tm = min(s, 512)
    has_res = dres is not None

    def body(*refs):
        x_ref, g_ref, dy_ref = refs[:3]
        dx_ref, dg_ref = refs[-2:]
        xv = x_ref[...].astype(F32)
        dyv = dy_ref[...].astype(F32)
        r = lax.rsqrt(jnp.mean(xv * xv, axis=-1, keepdims=True) + EPS)
        u = dyv * g_ref[...]
        dx = r * u - xv * (r * r * r) * jnp.mean(xv * u, axis=-1, keepdims=True)
        if has_res:
            dx = dx + refs[3][...]
        dx_ref[...] = dx

        @pl.when(pl.program_id(0) == 0)
        def _():
            dg_ref[...] = jnp.zeros_like(dg_ref)

        dg_ref[...] += jnp.sum(dyv * xv * r, axis=0, keepdims=True)

    blk = pl.BlockSpec((tm, w), lambda i: (i, 0))
    specs = [pl.BlockSpec((tm, w), lambda i: (i, ci)), pl.BlockSpec((1, w), lambda i: (0, 0)), blk]
    args = [x, g.reshape(1, w), dy]
    if has_res:
        specs.append(blk)
        args.append(dres)
    dx, dg = pl.pallas_call(
        body, grid=(s // tm,), in_specs=specs,
        out_specs=(blk, pl.BlockSpec((1, w), lambda i: (0, 0))),
        out_shape=(SDS((s, w), F32), SDS((1, w), F32)), name=name,
        compiler_params=_params(("arbitrary",), 16 * tm * w * 4))(*args)
    return dx, dg.reshape(w)


def _gated_rms_fwd(y, proj, g, *, name):
    s, w = y.shape
    tm = min(s, 512)

    def body(y_ref, z_ref, g_ref, o_ref):
        z = z_ref[...]
        t = y_ref[...] * (z * _sigmoid(z))
        r = lax.rsqrt(jnp.mean(t * t, axis=-1, keepdims=True) + EPS)
        o_ref[...] = (t * r * g_ref[...]).astype(o_ref.dtype)

    blk = pl.BlockSpec((tm, w), lambda i: (i, 0))
    return pl.pallas_call(
        body, grid=(s // tm,),
        in_specs=[blk, pl.BlockSpec((tm, w), lambda i: (i, P_Z // w)), pl.BlockSpec((1, w), lambda i: (0, 0))],
        out_specs=blk, out_shape=SDS((s, w), BF16), name=name,
        compiler_params=_params(("parallel",), 14 * tm * w * 4))(y, proj, g.reshape(1, w))


def _gated_rms_bwd(y, proj, g, dout, *, name):
    s, w = y.shape
    tm = min(s, 512)

    def body(y_ref, z_ref, g_ref, do_ref, dy_ref, dz_ref, dg_ref):
        z = z_ref[...]
        yv = y_ref[...]
        dov = do_ref[...]
        sg = _sigmoid(z)
        sz = z * sg
        t = yv * sz
        r = lax.rsqrt(jnp.mean(t * t, axis=-1, keepdims=True) + EPS)
        u = dov * g_ref[...]
        dt = r * u - t * (r * r * r) * jnp.mean(t * u, axis=-1, keepdims=True)
        dy_ref[...] = dt * sz
        dz_ref[...] = (dt * yv * (sg * (1.0 + z * (1.0 - sg)))).astype(dz_ref.dtype)

        @pl.when(pl.program_id(0) == 0)
        def _():
            dg_ref[...] = jnp.zeros_like(dg_ref)

        dg_ref[...] += jnp.sum(dov * t * r, axis=0, keepdims=True)

    blk = pl.BlockSpec((tm, w), lambda i: (i, 0))
    vec = pl.BlockSpec((1, w), lambda i: (0, 0))
    dy, dz, dg = pl.pallas_call(
        body, grid=(s // tm,),
        in_specs=[blk, pl.BlockSpec((tm, w), lambda i: (i, P_Z // w)), vec, blk],
        out_specs=(blk, blk, vec), out_shape=(SDS((s, w), F32), SDS((s, w), BF16), SDS((1, w), F32)), name=name,
        compiler_params=_params(("arbitrary",), 24 * tm * w * 4))(y, proj, g.reshape(1, w), dout)
    return dy, dz, dg.reshape(w)


def _final_loss(x, g, target, *, name):
    s, w = x.shape
    tm = min(s, 512)

    def body(x_ref, g_ref, t_ref, loss_ref, dx_ref, dg_ref):
        xv = x_ref[...]
        gv = g_ref[...]
        r = lax.rsqrt(jnp.mean(xv * xv, axis=-1, keepdims=True) + EPS)
        xn = xv * r
        diff = xn * gv - t_ref[...]
        dy = diff * (1.0 / w)
        u = dy * gv
        dx_ref[...] = r * u - xv * (r * r * r) * jnp.mean(xv * u, axis=-1, keepdims=True)

        @pl.when(pl.program_id(0) == 0)
        def _():
            dg_ref[...] = jnp.zeros_like(dg_ref)
            loss_ref[...] = jnp.zeros_like(loss_ref)

        dg_ref[...] += jnp.sum(dy * xn, axis=0, keepdims=True)
        part = jnp.sum(jnp.sum(diff * diff, axis=1, keepdims=True), axis=0, keepdims=True) * (0.5 / w)
        loss_ref[...] += jnp.broadcast_to(part, loss_ref.shape)

    blk = pl.BlockSpec((tm, w), lambda i: (i, 0))
    vec = pl.BlockSpec((1, w), lambda i: (0, 0))
    loss, dx, dg = pl.pallas_call(
        body, grid=(s // tm,), in_specs=[blk, vec, blk],
        out_specs=(pl.BlockSpec((1, LANES), lambda i: (0, 0)), blk, vec),
        out_shape=(SDS((1, LANES), F32), SDS((s, w), F32), SDS((1, w), F32)), name=name,
        compiler_params=_params(("arbitrary",), 16 * tm * w * 4))(x, g.reshape(1, w), target)
    return loss[0, 0], dx, dg.reshape(w)


def _shift_down(x, k):
    if k == 0:
        return x
    row = lax.broadcasted_iota(jnp.int32, x.shape, 0)
    return jnp.where(row < k, 0.0, pltpu.roll(x, k, axis=0))


def _shift_up(x, k):
    if k == 0:
        return x
    s = x.shape[0]
    row = lax.broadcasted_iota(jnp.int32, x.shape, 0)
    return jnp.where(row >= s - k, 0.0, pltpu.roll(x, s - k, axis=0))


def _conv_pre(x, w, b, kw):
    pre = b
    for j in range(kw):
        pre = pre + w[j:j + 1, :] * _shift_down(x, kw - 1 - j)
    return pre


def _conv_bwd_terms(x, w, dpre, kw):
    dx = jnp.zeros_like(x)
    dws = []
    for j in range(kw):
        dx = dx + w[j:j + 1, :] * _shift_up(dpre, kw - 1 - j)
        dws.append(jnp.sum(dpre * _shift_down(x, kw - 1 - j), axis=0, keepdims=True))
    return dx, jnp.concatenate(dws, axis=0), jnp.sum(dpre, axis=0, keepdims=True)


def _ssm_conv_fwd(proj, w, b, *, name):
    s = proj.shape[0]
    cw = 256

    def body(x_ref, w_ref, b_ref, o_ref):
        pre = _conv_pre(x_ref[...], w_ref[...], b_ref[...], SSM_CONV)
        o_ref[...] = pre * _sigmoid(pre)

    return pl.pallas_call(
        body, grid=(CONV_CH // cw,),
        in_specs=[pl.BlockSpec((s, cw), lambda j: (0, j)), pl.BlockSpec((SSM_CONV, cw), lambda j: (0, j)),
                  pl.BlockSpec((1, cw), lambda j: (0, j))],
        out_specs=pl.BlockSpec((s, cw), lambda j: (0, j)), out_shape=SDS((s, CONV_CH), F32), name=name,
        compiler_params=_params(("parallel",), 12 * s * cw * 4))(proj, w, b.reshape(1, CONV_CH))


def _ssm_conv_bwd(proj, w, b, dxbc, *, name):
    s = proj.shape[0]
    cw = 256

    def body(x_ref, w_ref, b_ref, dy_ref, dx_ref, dw_ref, db_ref):
        x = x_ref[...]
        wv = w_ref[...]
        pre = _conv_pre(x, wv, b_ref[...], SSM_CONV)
        sg = _sigmoid(pre)
        dpre = dy_ref[...] * (sg * (1.0 + pre * (1.0 - sg)))
        dx, dw, db = _conv_bwd_terms(x, wv, dpre, SSM_CONV)
        dx_ref[...] = dx.astype(dx_ref.dtype)
        dw_ref[...] = dw
        db_ref[...] = db

    col = pl.BlockSpec((s, cw), lambda j: (0, j))
    wsp = pl.BlockSpec((SSM_CONV, cw), lambda j: (0, j))
    bsp = pl.BlockSpec((1, cw), lambda j: (0, j))
    dx, dw, db = pl.pallas_call(
        body, grid=(CONV_CH // cw,), in_specs=[col, wsp, bsp, col], out_specs=(col, wsp, bsp),
        out_shape=(SDS((s, CONV_CH), BF16), SDS((SSM_CONV, CONV_CH), F32), SDS((1, CONV_CH), F32)), name=name,
        compiler_params=_params(("parallel",), 20 * s * cw * 4))(proj, w, b.reshape(1, CONV_CH), dxbc)
    return dx, dw, db.reshape(CONV_CH)


def _ffn_conv_fwd(up_g, up_v, w, b, *, name):
    s = up_g.shape[0]
    cw = 256
    nb = D_FF // cw

    def body(g_ref, v_ref, wg_ref, wv_ref, bg_ref, bv_ref, o_ref):
        gate = _conv_pre(g_ref[...], wg_ref[...], bg_ref[...], FFN_CONV)
        val = _conv_pre(v_ref[...], wv_ref[...], bv_ref[...], FFN_CONV)
        o_ref[...] = (gate * _sigmoid(gate) * val).astype(o_ref.dtype)

    col = pl.BlockSpec((s, cw), lambda j: (0, j))
    b2 = b.reshape(1, 2 * D_FF)
    return pl.pallas_call(
        body, grid=(nb,),
        in_specs=[col, col, pl.BlockSpec((FFN_CONV, cw), lambda j: (0, j)), pl.BlockSpec((FFN_CONV, cw), lambda j: (0, j + nb)),
                  pl.BlockSpec((1, cw), lambda j: (0, j)), pl.BlockSpec((1, cw), lambda j: (0, j + nb))],
        out_specs=col, out_shape=SDS((s, D_FF), BF16), name=name,
        compiler_params=_params(("parallel",), 16 * s * cw * 4))(up_g, up_v, w, w, b2, b2)


def _ffn_conv_bwd(up_g, up_v, w, b, dact, *, name):
    s = up_g.shape[0]
    cw = 256
    nb = D_FF // cw

    def body(g_ref, v_ref, wg_ref, wv_ref, bg_ref, bv_ref, da_ref, dg_ref, dv_ref, dwg_ref, dwv_ref, dbg_ref, dbv_ref):
        xg, xv = g_ref[...], v_ref[...]
        wg, wv = wg_ref[...], wv_ref[...]
        gate = _conv_pre(xg, wg, bg_ref[...], FFN_CONV)
        val = _conv_pre(xv, wv, bv_ref[...], FFN_CONV)
        da = da_ref[...].astype(F32)
        sg = _sigmoid(gate)
        dgate = da * val * (sg * (1.0 + gate * (1.0 - sg)))
        dval = da * gate * sg
        dxg, dwg, dbg = _conv_bwd_terms(xg, wg, dgate, FFN_CONV)
        dxv, dwv, dbv = _conv_bwd_terms(xv, wv, dval, FFN_CONV)
        dg_ref[...] = dxg.astype(dg_ref.dtype)
        dv_ref[...] = dxv.astype(dv_ref.dtype)
        dwg_ref[...] = dwg
        dwv_ref[...] = dwv
        dbg_ref[...] = dbg
        dbv_ref[...] = dbv

    col = pl.BlockSpec((s, cw), lambda j: (0, j))
    wsp = pl.BlockSpec((FFN_CONV, cw), lambda j: (0, j))
    bsp = pl.BlockSpec((1, cw), lambda j: (0, j))
    b2 = b.reshape(1, 2 * D_FF)
    dg, dv, dwg, dwv, dbg, dbv = pl.pallas_call(
        body, grid=(nb,),
        in_specs=[col, col, wsp, pl.BlockSpec((FFN_CONV, cw), lambda j: (0, j + nb)), bsp,
                  pl.BlockSpec((1, cw), lambda j: (0, j + nb)), col],
        out_specs=(col, col, wsp, wsp, bsp, bsp),
        out_shape=(SDS((s, D_FF), BF16), SDS((s, D_FF), BF16), SDS((FFN_CONV, D_FF), F32), SDS((FFN_CONV, D_FF), F32),
                   SDS((1, D_FF), F32), SDS((1, D_FF), F32)), name=name,
        compiler_params=_params(("parallel",), 32 * s * cw * 4))(up_g, up_v, w, w, b2, b2, dact)
    return dg, dv, jnp.concatenate([dwg, dwv], axis=1), jnp.concatenate([dbg, dbv], axis=1).reshape(2 * D_FF)


def _dot(a, b):
    return jnp.dot(a.astype(BF16), b.astype(BF16), preferred_element_type=F32)


def _dot_nt(a, b):
    return lax.dot_general(a.astype(BF16), b.astype(BF16), (((1,), (1,)), ((), ())), preferred_element_type=F32)


def _dot_tn(a, b):
    return lax.dot_general(a.astype(BF16), b.astype(BF16), (((0,), (0,)), ((), ())), preferred_element_type=F32)


def _ssd_chunk_terms(dtraw, bias, a_log):
    ell = dtraw.shape[0]
    lane = lax.broadcasted_iota(jnp.int32, dtraw.shape, 1)
    valid = lane < SSM_HEADS
    pre = dtraw + bias
    dt = jnp.where(valid, jnp.where(pre > 20.0, pre, jnp.log(1.0 + jnp.exp(jnp.minimum(pre, 20.0)))), 0.0)
    a = -jnp.exp(a_log)
    ad = dt * a
    row = lax.broadcasted_iota(jnp.int32, (ell, ell), 0)
    colm = lax.broadcasted_iota(jnp.int32, (ell, ell), 1)
    tril = row >= colm
    cs = jnp.dot(tril.astype(F32), ad, precision=HIGHEST, preferred_element_type=F32)
    cs_last = cs[ell - 1:ell, :]
    return pre, dt, a, cs, cs_last, tril


def _lane_put(col, h, shape):
    lane = lax.broadcasted_iota(jnp.int32, shape, 1)
    return jnp.where(lane == h, col, 0.0)


def _ssd_fwd(xbc, proj, dt_bias, a_log, d_skip, *, name):
    s = xbc.shape[0]
    nc = s // SSM_CHUNK
    ell, n, p = SSM_CHUNK, SSM_STATE, SSM_HEAD_DIM
    rpg = SSM_HEADS // SSM_GROUPS

    def body(x_ref, dt_ref, bias_ref, alog_ref, dskip_ref, y_ref, ps_ref, state):
        @pl.when(pl.program_id(0) == 0)
        def _():
            state[...] = jnp.zeros_like(state)

        _, dt, _, cs, cs_last, tril = _ssd_chunk_terms(dt_ref[...], bias_ref[...], alog_ref[...])
        e = jnp.exp(cs)
        ds = jnp.exp(cs_last - cs)
        cd = jnp.exp(cs_last)
        cst = cs.T
        dskip = dskip_ref[...]
        ps_ref[0] = state[...]
        for g in range(SSM_GROUPS):
            bg = x_ref[:, D_SSM + n * g:D_SSM + n * (g + 1)]
            cg = x_ref[:, D_SSM + n * (SSM_GROUPS + g):D_SSM + n * (SSM_GROUPS + g + 1)]
            cb = _dot_nt(cg, bg)
            for r in range(rpg):
                h = g * rpg + r
                hs = slice(p * h, p * (h + 1))
                xs = x_ref[:, hs]
                xd = xs * dt[:, h:h + 1]
                lmat = jnp.exp(jnp.where(tril, cs[:, h:h + 1] - cst[h:h + 1, :], -jnp.inf))
                prev = state[:, hs]
                y = _dot(cb * lmat, xd) + _dot(cg, prev) * e[:, h:h + 1] + xs * dskip[:, h:h + 1]
                y_ref[:, hs] = y
                state[:, hs] = prev * cd[:, h:h + 1] + _dot_tn(bg, xd * ds[:, h:h + 1])

    vec = pl.BlockSpec((1, LANES), lambda c: (0, 0))
    return pl.pallas_call(
        body, grid=(nc,),
        in_specs=[pl.BlockSpec((ell, CONV_CH), lambda c: (c, 0)), pl.BlockSpec((ell, LANES), lambda c: (c, P_DT // LANES)),
                  vec, vec, vec],
        out_specs=(pl.BlockSpec((ell, D_SSM), lambda c: (c, 0)), pl.BlockSpec((1, n, D_SSM), lambda c: (c, 0, 0))),
        out_shape=(SDS((s, D_SSM), F32), SDS((nc, n, D_SSM), F32)),
        scratch_shapes=[pltpu.VMEM((n, D_SSM), F32)], name=name,
        compiler_params=_params(("arbitrary",), 24 << 20))(xbc, proj, dt_bias, a_log, d_skip)


def _ssd_bwd(xbc, proj, dt_bias, a_log, d_skip, prev_states, dy, *, name):
    s = xbc.shape[0]
    nc = s // SSM_CHUNK
    ell, n, p = SSM_CHUNK, SSM_STATE, SSM_HEAD_DIM
    rpg = SSM_HEADS // SSM_GROUPS

    def body(x_ref, dt_ref, bias_ref, alog_ref, dskip_ref, ps_ref, dy_ref,
             dx_ref, ddt_ref, dalog_ref, ddskip_ref, dbias_ref, dstate):
        @pl.when(pl.program_id(0) == 0)
        def _():
            dstate[...] = jnp.zeros_like(dstate)
            dalog_ref[...] = jnp.zeros_like(dalog_ref)
            ddskip_ref[...] = jnp.zeros_like(ddskip_ref)
            dbias_ref[...] = jnp.zeros_like(dbias_ref)

        pre, dt, a, cs, cs_last, tril = _ssd_chunk_terms(dt_ref[...], bias_ref[...], alog_ref[...])
        e = jnp.exp(cs)
        ds = jnp.exp(cs_last - cs)
        cd = jnp.exp(cs_last)
        cst = cs.T
        dskip = dskip_ref[...]
        shape = (ell, LANES)
        ddt_acc = jnp.zeros(shape, F32)
        dcs_acc = jnp.zeros(shape, F32)
        dcs_rows = jnp.zeros(shape, F32)
        dlast_acc = jnp.zeros((1, LANES), F32)
        dskip_acc = jnp.zeros((1, LANES), F32)
        for g in range(SSM_GROUPS):
            bsl = slice(D_SSM + n * g, D_SSM + n * (g + 1))
            csl = slice(D_SSM + n * (SSM_GROUPS + g), D_SSM + n * (SSM_GROUPS + g + 1))
            bg = x_ref[:, bsl]
            cg = x_ref[:, csl]
            cb = _dot_nt(cg, bg)
            dcb = jnp.zeros((ell, ell), F32)
            dbg = jnp.zeros((ell, n), F32)
            dcg = jnp.zeros((ell, n), F32)
            for r in range(rpg):
                h = g * rpg + r
                hs = slice(p * h, p * (h + 1))
                xs = x_ref[:, hs]
                dyh = dy_ref[:, hs]
                dt_h, e_h, ds_h, cd_h = dt[:, h:h + 1], e[:, h:h + 1], ds[:, h:h + 1], cd[:, h:h + 1]
                prev = ps_ref[0, :, hs]
                dsn = dstate[:, hs]
                xd = xs * dt_h
                dye = dyh * e_h
                cprev = _dot(cg, prev)
                dprev = dsn * cd_h + _dot_tn(cg, dye)
                dcg = dcg + _dot_nt(dye, prev)
                dcs_h = jnp.sum(dyh * cprev, axis=1, keepdims=True) * e_h
                dcd = jnp.sum(jnp.sum(dsn * prev, axis=1, keepdims=True), axis=0, keepdims=True)
                dlast_h = dcd * cd_h
                dxdd = _dot(bg, dsn)
                dbg = dbg + _dot_nt(xd * ds_h, dsn)
                dxd = dxdd * ds_h
                tmp = jnp.sum(dxdd * xd, axis=1, keepdims=True) * ds_h
                dlast_h = dlast_h + jnp.sum(tmp, axis=0, keepdims=True)
                dcs_h = dcs_h - tmp
                lmat = jnp.exp(jnp.where(tril, cs[:, h:h + 1] - cst[h:h + 1, :], -jnp.inf))
                gm = cb * lmat
                dgm = _dot_nt(dyh, xd)
                dxd = dxd + _dot_tn(gm, dyh)
                mm = dgm * gm
                dcs_h = dcs_h + jnp.sum(mm, axis=1, keepdims=True)
                sub = lax.broadcasted_iota(jnp.int32, shape, 0)
                dcs_rows = dcs_rows + jnp.where(sub == h, jnp.sum(mm, axis=0, keepdims=True), 0.0)
                dcb = dcb + dgm * lmat
                dx_ref[:, hs] = dxd * dt_h + dyh * dskip[:, h:h + 1]
                ddt_acc = ddt_acc + _lane_put(jnp.sum(dxd * xs, axis=1, keepdims=True), h, shape)
                dcs_acc = dcs_acc + _lane_put(dcs_h, h, shape)
                dlast_acc = dlast_acc + _lane_put(dlast_h, h, (1, LANES))
                dskip_acc = dskip_acc + _lane_put(
                    jnp.sum(jnp.sum(dyh * xs, axis=1, keepdims=True), axis=0, keepdims=True), h, (1, LANES))
                dstate[:, hs] = dprev
            dx_ref[:, bsl] = dbg + _dot_tn(dcb, cg)
            dx_ref[:, csl] = dcg + _dot(dcb, bg)
        rowi = lax.broadcasted_iota(jnp.int32, shape, 0)
        dcs = dcs_acc - dcs_rows.T + jnp.where(rowi == ell - 1, dlast_acc, 0.0)
        triu = lax.broadcasted_iota(jnp.int32, (ell, ell), 0) <= lax.broadcasted_iota(jnp.int32, (ell, ell), 1)
        dad = jnp.dot(triu.astype(F32), dcs, precision=HIGHEST, preferred_element_type=F32)
        ddt = ddt_acc + dad * a
        dalog_ref[...] += jnp.sum(dad * dt, axis=0, keepdims=True) * a
        ddskip_ref[...] += dskip_acc
        lane = lax.broadcasted_iota(jnp.int32, shape, 1)
        ddraw = jnp.where(lane < SSM_HEADS, ddt * _sigmoid(pre), 0.0)
        ddt_ref[...] = ddraw.astype(ddt_ref.dtype)
        dbias_ref[...] += jnp.sum(ddraw, axis=0, keepdims=True)

    vec = pl.BlockSpec((1, LANES), lambda c: (0, 0))
    rev = lambda c: nc - 1 - c
    outs = pl.pallas_call(
        body, grid=(nc,),
        in_specs=[pl.BlockSpec((ell, CONV_CH), lambda c: (rev(c), 0)),
                  pl.BlockSpec((ell, LANES), lambda c: (rev(c), P_DT // LANES)), vec, vec, vec,
                  pl.BlockSpec((1, n, D_SSM), lambda c: (rev(c), 0, 0)),
                  pl.BlockSpec((ell, D_SSM), lambda c: (rev(c), 0))],
        out_specs=(pl.BlockSpec((ell, CONV_CH), lambda c: (rev(c), 0)), pl.BlockSpec((ell, LANES), lambda c: (rev(c), 0)),
                   vec, vec, vec),
        out_shape=(SDS((s, CONV_CH), F32), SDS((s, LANES), BF16), SDS((1, LANES), F32), SDS((1, LANES), F32),
                   SDS((1, LANES), F32)),
        scratch_shapes=[pltpu.VMEM((n, D_SSM), F32)], name=name,
        compiler_params=_params(("arbitrary",), 32 << 20))(xbc, proj, dt_bias, a_log, d_skip, prev_states, dy)
    return outs


def _rope_swap(t):
    lane = lax.broadcasted_iota(jnp.int32, t.shape, 1)
    half = QK_ROPE // 2
    lo = (lane >= QK_NOPE) & (lane < QK_NOPE + half)
    hi = (lane >= QK_NOPE + half) & (lane < QK_NOPE + QK_ROPE)
    return jnp.where(lo, pltpu.roll(t, HEAD_PAD - half, axis=1), jnp.where(hi, pltpu.roll(t, half, axis=1), 0.0))


def _mla_prep(q, kv, proj, cos, sins, *, name):
    s = q.shape[0]
    tm = min(s, 256)
    scale = (QK_NOPE + QK_ROPE) ** -0.5

    def body(q_ref, kv_ref, kr_ref, cos_ref, sin_ref, qo_ref, ko_ref, vo_ref):
        cosv, sinv = cos_ref[...], sin_ref[...]
        kr = pltpu.roll(kr_ref[...], QK_NOPE, axis=1)
        lane = lax.broadcasted_iota(jnp.int32, kr.shape, 1)
        nope = lane < QK_NOPE
        kr = jnp.where(nope, 0.0, kr)
        kpe = kr * cosv + _rope_swap(kr) * sinv
        for hp in range(MLA_HEADS // 2):
            vs = []
            for h in (2 * hp, 2 * hp + 1):
                hs = slice(HEAD_PAD * h, HEAD_PAD * (h + 1))
                qh = q_ref[:, hs]
                kvh = kv_ref[:, hs]
                qo_ref[:, hs] = ((qh * cosv + _rope_swap(qh) * sinv) * scale).astype(qo_ref.dtype)
                ko_ref[:, hs] = (jnp.where(nope, kvh, 0.0) + kpe).astype(ko_ref.dtype)
                vs.append(kvh[:, QK_NOPE:])
            vo_ref[:, 2 * V_DIM * hp:2 * V_DIM * (hp + 1)] = jnp.concatenate(vs, axis=1).astype(vo_ref.dtype)

    wide = pl.BlockSpec((tm, MLA_HEADS * HEAD_PAD), lambda i: (i, 0))
    half = pl.BlockSpec((tm, MLA_HEADS * V_DIM), lambda i: (i, 0))
    tab = pl.BlockSpec((tm, LANES), lambda i: (i, 0))
    return pl.pallas_call(
        body, grid=(s // tm,),
        in_specs=[wide, wide, pl.BlockSpec((tm, LANES), lambda i: (i, P_KR // LANES)), tab, tab],
        out_specs=(wide, wide, half),
        out_shape=(SDS((s, MLA_HEADS * HEAD_PAD), BF16), SDS((s, MLA_HEADS * HEAD_PAD), BF16),
                   SDS((s, MLA_HEADS * V_DIM), BF16)), name=name,
        compiler_params=_params(("parallel",), 32 << 20))(q, kv, proj, cos, sins)


def _mla_prep_bwd(dqr, dkr, dv, cos, sins, *, name):
    s = dqr.shape[0]
    tm = min(s, 256)
    scale = (QK_NOPE + QK_ROPE) ** -0.5

    def body(dq_ref, dk_ref, dv_ref, cos_ref, sin_ref, dqo_ref, dkv_ref, dkr_ref):
        cosv, sinv = cos_ref[...], sin_ref[...]
        lane = lax.broadcasted_iota(jnp.int32, cosv.shape, 1)
        ksum = jnp.zeros(cosv.shape, F32)
        for h in range(MLA_HEADS):
            hs = slice(HEAD_PAD * h, HEAD_PAD * (h + 1))
            d = dq_ref[:, hs]
            dk = dk_ref[:, hs]
            dqo_ref[:, hs] = ((d * cosv + _rope_swap(d * sinv)) * scale).astype(dqo_ref.dtype)
            dkv_ref[:, hs] = jnp.concatenate([dk[:, :QK_NOPE], dv_ref[:, V_DIM * h:V_DIM * (h + 1)]], axis=1).astype(dkv_ref.dtype)
            ksum = ksum + dk
        ksum = jnp.where((lane >= QK_NOPE) & (lane < QK_NOPE + QK_ROPE), ksum, 0.0)
        un = ksum * cosv + _rope_swap(ksum * sinv)
        dkr_ref[...] = pltpu.roll(un, HEAD_PAD - QK_NOPE, axis=1).astype(dkr_ref.dtype)

    wide = pl.BlockSpec((tm, MLA_HEADS * HEAD_PAD), lambda i: (i, 0))
    half = pl.BlockSpec((tm, MLA_HEADS * V_DIM), lambda i: (i, 0))
    tab = pl.BlockSpec((tm, LANES), lambda i: (i, 0))
    return pl.pallas_call(
        body, grid=(s // tm,), in_specs=[wide, wide, half, tab, tab], out_specs=(wide, wide, tab),
        out_shape=(SDS((s, MLA_HEADS * HEAD_PAD), BF16), SDS((s, MLA_HEADS * HEAD_PAD), BF16), SDS((s, LANES), BF16)),
        name=name, compiler_params=_params(("parallel",), 40 << 20))(dqr, dkr, dv, cos, sins)


FLASH_TILE = 512


def _flash_fwd(q, k, v, *, name):
    s = q.shape[0]
    t = min(s, FLASH_TILE)
    nq = s // t
    npair = MLA_HEADS // 2

    def body(q_ref, k_ref, v_ref, o_ref, lse_ref):
        i = pl.program_id(1)
        qs = [q_ref[:, HEAD_PAD * e:HEAD_PAD * (e + 1)] for e in range(2)]
        diag = lax.broadcasted_iota(jnp.int32, (t, t), 0) >= lax.broadcasted_iota(jnp.int32, (t, t), 1)

        def step(j, carry, masked):
            rows = pl.ds(pl.multiple_of(j * t, t), t)
            new = []
            for e in range(2):
                m, l, acc = carry[e]
                sc = _dot_nt(qs[e], k_ref[rows, HEAD_PAD * e:HEAD_PAD * (e + 1)])
                if masked:
                    sc = jnp.where(diag, sc, NEG)
                m_new = jnp.maximum(m, jnp.max(sc, axis=1, keepdims=True))
                pr = jnp.exp(sc - m_new)
                alpha = jnp.exp(m - m_new)
                l = alpha * l + jnp.sum(pr, axis=1, keepdims=True)
                acc = alpha * acc + _dot(pr, v_ref[rows, V_DIM * e:V_DIM * (e + 1)])
                new.append((m_new, l, acc))
            return tuple(new)

        init = tuple((jnp.full((t, 1), NEG, F32), jnp.zeros((t, 1), F32), jnp.zeros((t, V_DIM), F32)) for _ in range(2))
        carry = lax.fori_loop(0, i, functools.partial(step, masked=False), init)
        carry = step(i, carry, True)
        o_ref[...] = jnp.concatenate([acc / l for _, l, acc in carry], axis=1)
        lse_ref[0] = jnp.concatenate([jnp.broadcast_to(m + jnp.log(l), (t, V_DIM)) for m, l, _ in carry], axis=1)

    return pl.pallas_call(
        body, grid=(npair, nq),
        in_specs=[pl.BlockSpec((t, 2 * HEAD_PAD), lambda hp, i: (i, hp)), pl.BlockSpec((s, 2 * HEAD_PAD), lambda hp, i: (0, hp)),
                  pl.BlockSpec((s, 2 * V_DIM), lambda hp, i: (0, hp))],
        out_specs=(pl.BlockSpec((t, 2 * V_DIM), lambda hp, i: (i, hp)), pl.BlockSpec((1, t, LANES), lambda hp, i: (hp, i, 0))),
        out_shape=(SDS((s, MLA_HEADS * V_DIM), F32), SDS((npair, s, LANES), F32)), name=name,
        compiler_params=_params(("parallel", "parallel"), 40 << 20))(q, k, v)


def _flash_bwd(q, k, v, o, lse, do, *, name):
    s = q.shape[0]
    t = min(s, FLASH_TILE)
    nq = s // t
    npair = MLA_HEADS // 2

    def body(q_ref, k_ref, v_ref, o_ref, lse_ref, do_ref, dq_ref, dk_ref, dv_ref):
        j = pl.program_id(1)

        @pl.when(j == 0)
        def _():
            dq_ref[...] = jnp.zeros_like(dq_ref)

        qsl = [slice(HEAD_PAD * e, HEAD_PAD * (e + 1)) for e in range(2)]
        vsl = [slice(V_DIM * e, V_DIM * (e + 1)) for e in range(2)]
        ks = [k_ref[:, qsl[e]] for e in range(2)]
        vs = [v_ref[:, vsl[e]] for e in range(2)]
        diag = lax.broadcasted_iota(jnp.int32, (t, t), 0) >= lax.broadcasted_iota(jnp.int32, (t, t), 1)

        def step(i, carry, masked):
            rows = pl.ds(pl.multiple_of(i * t, t), t)
            new = []
            for e in range(2):
                dk, dv = carry[e]
                qi = q_ref[rows, qsl[e]]
                doi = do_ref[rows, vsl[e]]
                delta = jnp.sum(doi * o_ref[rows, vsl[e]], axis=1, keepdims=True)
                lse_i = lse_ref[0, rows, vsl[e]][:, 0:1]
                sc = _dot_nt(qi, ks[e])
                if masked:
                    sc = jnp.where(diag, sc, NEG)
                pr = jnp.exp(sc - lse_i)
                dv = dv + _dot_tn(pr, doi)
                dsc = (pr * (_dot_nt(doi, vs[e]) - delta)).astype(BF16)
                dk = dk + _dot_tn(dsc, qi)
                dq_ref[rows, qsl[e]] += _dot(dsc, ks[e])
                new.append((dk, dv))
            return tuple(new)

        init = tuple((jnp.zeros((t, HEAD_PAD), F32), jnp.zeros((t, V_DIM), F32)) for _ in range(2))
        carry = step(j, init, True)
        carry = lax.fori_loop(j + 1, nq, functools.partial(step, masked=False), carry)
        dk_ref[...] = jnp.concatenate([dk for dk, _ in carry], axis=1)
        dv_ref[...] = jnp.concatenate([dv for _, dv in carry], axis=1)

    full_q = pl.BlockSpec((s, 2 * HEAD_PAD), lambda hp, j: (0, hp))
    full_v = pl.BlockSpec((s, 2 * V_DIM), lambda hp, j: (0, hp))
    blk_k = pl.BlockSpec((t, 2 * HEAD_PAD), lambda hp, j: (j, hp))
    blk_v = pl.BlockSpec((t, 2 * V_DIM), lambda hp, j: (j, hp))
    return pl.pallas_call(
        body, grid=(npair, nq),
        in_specs=[full_q, blk_k, blk_v, full_v, pl.BlockSpec((1, s, LANES), lambda hp, j: (hp, 0, 0)), full_v],
        out_specs=(full_q, blk_k, blk_v),
        out_shape=(SDS((s, MLA_HEADS * HEAD_PAD), F32), SDS((s, MLA_HEADS * HEAD_PAD), F32), SDS((s, MLA_HEADS * V_DIM), F32)),
        name=name, compiler_params=_params(("parallel", "arbitrary"), 48 << 20))(q, k, v, o, lse, do)


def _mem_attn_fwd(q, k, v, *, name):
    s = q.shape[0]
    tm = min(s, 512)
    ml = k.shape[0]
    scale = MEM_HEAD_DIM ** -0.5

    def body(q_ref, k_ref, v_ref, o_ref):
        for h in range(MEM_HEADS):
            hs = slice(MEM_HEAD_DIM * h, MEM_HEAD_DIM * (h + 1))
            sc = _dot_nt(q_ref[:, hs], k_ref[:, hs]) * scale
            pr = jnp.exp(sc - jnp.max(sc, axis=1, keepdims=True))
            pr = pr / jnp.sum(pr, axis=1, keepdims=True)
            o_ref[:, hs] = _dot(pr, v_ref[:, hs]).astype(o_ref.dtype)

    blk = pl.BlockSpec((tm, D_MODEL), lambda i: (i, 0))
    kv = pl.BlockSpec((ml, D_MODEL), lambda i: (0, 0))
    return pl.pallas_call(body, grid=(s // tm,), in_specs=[blk, kv, kv], out_specs=blk,
                          out_shape=SDS((s, D_MODEL), BF16), name=name,
                          compiler_params=_params(("parallel",), 24 << 20))(q, k, v)


def _mem_attn_bwd(q, k, v, do, *, name):
    s = q.shape[0]
    tm = min(s, 512)
    ml = k.shape[0]
    scale = MEM_HEAD_DIM ** -0.5

    def body(q_ref, k_ref, v_ref, do_ref, dq_ref, dk_ref, dv_ref):
        @pl.when(pl.program_id(0) == 0)
        def _():
            dk_ref[...] = jnp.zeros_like(dk_ref)
            dv_ref[...] = jnp.zeros_like(dv_ref)

        for h in range(MEM_HEADS):
            hs = slice(MEM_HEAD_DIM * h, MEM_HEAD_DIM * (h + 1))
            qh, kh, vh, doh = q_ref[:, hs], k_ref[:, hs], v_ref[:, hs], do_ref[:, hs]
            sc = _dot_nt(qh, kh) * scale
            pr = jnp.exp(sc - jnp.max(sc, axis=1, keepdims=True))
            pr = pr / jnp.sum(pr, axis=1, keepdims=True)
            dp = _dot_nt(doh, vh)
            dsc = pr * (dp - jnp.sum(pr * dp, axis=1, keepdims=True)) * scale
            dq_ref[:, hs] = _dot(dsc, kh).astype(dq_ref.dtype)
            dk_ref[:, hs] += _dot_tn(dsc, qh)
            dv_ref[:, hs] += _dot_tn(pr, doh)

    blk = pl.BlockSpec((tm, D_MODEL), lambda i: (i, 0))
    kv = pl.BlockSpec((ml, D_MODEL), lambda i: (0, 0))
    return pl.pallas_call(body, grid=(s // tm,), in_specs=[blk, kv, kv, blk], out_specs=(blk, kv, kv),
                          out_shape=(SDS((s, D_MODEL), BF16), SDS((ml, D_MODEL), F32), SDS((ml, D_MODEL), F32)), name=name,
                          compiler_params=_params(("arbitrary",), 32 << 20))(q, k, v, do)


MATS = (("w_in", (1024, 940), 1), ("w_uq", (384, 384), 1), ("w_ukv", (256, 512), 1), ("w_out", (512, 1024), 0),
        ("w_mq", (256, 1024), 0), ("w_mk", (256, 1024), 0), ("w_mv", (256, 1024), 0), ("w_mo", (256, 1024), 0),
        ("w_up", (1024, 1408), 1), ("w_down", (704, 1024), 0), ("ssm_conv_w", (4, 512), 1), ("ffn_conv_w", (3, 1408), 1))
F32_ON_WIRE = ("ssm_conv_w", "ffn_conv_w")
SMALL = (("norm_mix", 1024), ("ssm_conv_b", 2048), ("dt_bias", 16), ("a_log", 16), ("d_skip", 16), ("ssm_norm", 1024),
         ("q_norm", 384), ("kv_norm", 256), ("attn_out_norm", 1024), ("norm_mem_q", 1024), ("norm_mem_kv", 1024),
         ("norm_ffn", 1024), ("ffn_conv_b", 5632))
PACK_COLS = 1024


def _pad_cols(t, n):
    return jnp.pad(t, ((0, 0),) * (t.ndim - 1) + ((0, n - t.shape[-1]),))


def _w_in_to_padded(t):
    z, xbc, dt, cq, ckv, kr = jnp.split(t, (1024, 3072, 3088, 3472, 3728), axis=-1)
    return jnp.concatenate([xbc, z, cq, _pad_cols(dt, LANES), ckv, _pad_cols(kr, P_IN - P_KR)], axis=-1)


def _w_in_from_padded(t):
    return jnp.concatenate([t[..., P_Z:P_Z + 1024], t[..., P_XBC:P_XBC + 2048], t[..., P_DT:P_DT + SSM_HEADS],
                            t[..., P_CQ:P_CQ + Q_LORA], t[..., P_CKV:P_CKV + KV_LORA], t[..., P_KR:P_KR + QK_ROPE]], axis=-1)


def _cols_joined(g):
    return jnp.concatenate([g[j] for j in range(N_CHIPS)], axis=-1)


def _cols_by_chip(t, dtype):
    k = t.shape[0]
    return t.reshape(k, N_CHIPS, -1).transpose(1, 0, 2).astype(dtype)


def _rows_by_chip(t):
    return t.reshape(N_CHIPS, -1, t.shape[-1])


def _layer_weights(gw):
    wl = {}
    wl["w_in"] = _w_in_to_padded(_cols_joined(gw["w_in"]))
    uq = _cols_joined(gw["w_uq"]).reshape(Q_LORA, MLA_HEADS, QK_NOPE + QK_ROPE)
    wl["w_uq"] = _pad_cols(uq, HEAD_PAD).reshape(Q_LORA, MLA_HEADS * HEAD_PAD)
    wl["w_ukv"] = _cols_joined(gw["w_ukv"])
    wl["ssm_conv_w"] = _cols_joined(gw["ssm_conv_w"])
    wl["ffn_conv_w"] = _cols_joined(gw["ffn_conv_w"])
    return wl


def _layer_fwd(x0, mem, cos, sins, gw, wl, sp, li):
    n = lambda t: f"l{li}_{t}"
    lead = ()
    sv = {"x0": x0}
    h = _rms_fwd(x0, sp["norm_mix"], name=n("mix_norm"))
    proj = _mm(h, wl["w_in"], name=n("mix_proj"))
    xbc = _ssm_conv_fwd(proj, wl["ssm_conv_w"], sp["ssm_conv_b"], name=n("ssm_conv"))
    y, pstates = _ssd_fwd(xbc, proj, sp["dt_bias"], sp["a_log"], sp["d_skip"], name=n("ssd"))
    y_ssm = _gated_rms_fwd(y, proj, sp["ssm_norm"], name=n("ssm_gate"))
    cqn = _rms_fwd(proj, sp["q_norm"], col=(Q_LORA, P_CQ // Q_LORA), name=n("q_norm"))
    ckvn = _rms_fwd(proj, sp["kv_norm"], col=(KV_LORA, P_CKV // KV_LORA), name=n("kv_norm"))
    q = _mm(cqn, wl["w_uq"], name=n("uq"))
    kv = _mm(ckvn, wl["w_ukv"], name=n("ukv"))
    qr, kr, v = _mla_prep(q, kv, proj, cos, sins, name=n("rope"))
    att, lse = _flash_fwd(qr, kr, v, name=n("flash"))
    y_att = _rms_fwd(att, sp["attn_out_norm"], name=n("att_norm"))
    x1 = _mm(y_ssm, gw["w_out"], b_lead=lead, b_rows=(0, D_SSM), res=x0, name=n("out_a"))
    x1 = _mm(y_att, gw["w_out"], b_lead=lead, b_rows=(D_SSM, D_SSM), res=x1, name=n("out_b"))
    sv.update(h=h, proj=proj, xbc=xbc, y=y, pstates=pstates, y_ssm=y_ssm, cqn=cqn, ckvn=ckvn, qr=qr, kr=kr, v=v,
              att=att, lse=lse, y_att=y_att, x1=x1)
    hq = _rms_fwd(x1, sp["norm_mem_q"], name=n("memq_norm"))
    hm = _rms_fwd(mem, sp["norm_mem_kv"], name=n("memkv_norm"))
    mq = _mm(hq, gw["w_mq"], b_lead=lead, out_dtype=BF16, name=n("mq"))
    mk = _mm(hm, gw["w_mk"], b_lead=lead, out_dtype=BF16, name=n("mk"))
    mv = _mm(hm, gw["w_mv"], b_lead=lead, out_dtype=BF16, name=n("mv"))
    mo = _mem_attn_fwd(mq, mk, mv, name=n("mem_attn"))
    x2 = _mm(mo, gw["w_mo"], b_lead=lead, res=x1, name=n("mo"))
    sv.update(hq=hq, hm=hm, mq=mq, mk=mk, mv=mv, mo=mo, x2=x2)
    hf = _rms_fwd(x2, sp["norm_ffn"], name=n("ffn_norm"))
    up_g = _mm(hf, gw["w_up"], b_lead=lead, b_chips=(0, 2), name=n("up_g"))
    up_v = _mm(hf, gw["w_up"], b_lead=lead, b_chips=(2, 2), name=n("up_v"))
    act = _ffn_conv_fwd(up_g, up_v, wl["ffn_conv_w"], sp["ffn_conv_b"], name=n("ffn_conv"))
    x3 = _mm(act, gw["w_down"], b_lead=lead, res=x2, name=n("down"))
    sv.update(hf=hf, up_g=up_g, up_v=up_v, act=act)
    return x3, sv


def _layer_bwd(dx3, mem, cos, sins, gw, wl, sp, sv, li):
    n = lambda t: f"l{li}_b_{t}"
    lead = ()
    g = {}
    dact = _mm(dx3, gw["w_down"], tb=True, b_lead=lead, out_dtype=BF16, name=n("down_dx"))
    g["w_down"] = _rows_by_chip(_mm(sv["act"], dx3, ta=True, out_dtype=BF16, name=n("down_dw")))
    dup_g, dup_v, dcw, g["ffn_conv_b"] = _ffn_conv_bwd(
        sv["up_g"], sv["up_v"], wl["ffn_conv_w"], sp["ffn_conv_b"], dact, name=n("ffn_conv"))
    g["ffn_conv_w"] = _cols_by_chip(dcw, F32)
    nsh = MATS[8][1][1]
    dhf = None
    for c4 in range(N_CHIPS):
        dhf = _mm(dup_g if c4 < 2 else dup_v, gw["w_up"], tb=True, a_col=(nsh, c4 % 2), b_lead=(c4,), res=dhf,
                  name=n(f"up{c4}_dx"))
    g["w_up"] = jnp.concatenate([_mm(sv["hf"], dup_g, ta=True, o_chips=nsh, out_dtype=BF16, name=n("upg_dw")),
                                 _mm(sv["hf"], dup_v, ta=True, o_chips=nsh, out_dtype=BF16, name=n("upv_dw"))], axis=0)
    dx2, g["norm_ffn"] = _rms_bwd(sv["x2"], sp["norm_ffn"], dhf, dx3, name=n("ffn_norm"))
    dmo = _mm(dx2, gw["w_mo"], tb=True, b_lead=lead, out_dtype=BF16, name=n("mo_dx"))
    g["w_mo"] = _rows_by_chip(_mm(sv["mo"], dx2, ta=True, out_dtype=BF16, name=n("mo_dw")))
    dmq, dmk, dmv = _mem_attn_bwd(sv["mq"], sv["mk"], sv["mv"], dmo, name=n("mem_attn"))
    dhq = _mm(dmq, gw["w_mq"], tb=True, b_lead=lead, name=n("mq_dx"))
    g["w_mq"] = _rows_by_chip(_mm(sv["hq"], dmq, ta=True, out_dtype=BF16, name=n("mq_dw")))
    dhm = _mm(dmk, gw["w_mk"], tb=True, b_lead=lead, name=n("mk_dx"))
    dhm = _mm(dmv, gw["w_mv"], tb=True, b_lead=lead, res=dhm, name=n("mv_dx"))
    g["w_mk"] = _rows_by_chip(_mm(sv["hm"], dmk, ta=True, out_dtype=BF16, name=n("mk_dw")))
    g["w_mv"] = _rows_by_chip(_mm(sv["hm"], dmv, ta=True, out_dtype=BF16, name=n("mv_dw")))
    dx1, g["norm_mem_q"] = _rms_bwd(sv["x1"], sp["norm_mem_q"], dhq, dx2, name=n("memq_norm"))
    _, g["norm_mem_kv"] = _rms_bwd(mem, sp["norm_mem_kv"], dhm, name=n("memkv_norm"))
    dy_ssm = _mm(dx1, gw["w_out"], tb=True, b_lead=lead, b_rows=(0, D_SSM), name=n("outa_dx"))
    dy_att = _mm(dx1, gw["w_out"], tb=True, b_lead=lead, b_rows=(D_SSM, D_SSM), name=n("outb_dx"))
    g["w_out"] = _rows_by_chip(jnp.concatenate([_mm(sv["y_ssm"], dx1, ta=True, out_dtype=BF16, name=n("outa_dw")),
                                                _mm(sv["y_att"], dx1, ta=True, out_dtype=BF16, name=n("outb_dw"))], axis=0))
    datt, g["attn_out_norm"] = _rms_bwd(sv["att"], sp["attn_out_norm"], dy_att, name=n("att_norm"))
    dqr, dkr, dv = _flash_bwd(sv["qr"], sv["kr"], sv["v"], sv["att"], sv["lse"], datt, name=n("flash"))
    dq, dkv, dkrope = _mla_prep_bwd(dqr, dkr, dv, cos, sins, name=n("rope"))
    duq = _mm(sv["cqn"], dq, ta=True, name=n("uq_dw")).reshape(Q_LORA, MLA_HEADS, HEAD_PAD)[..., :QK_NOPE + QK_ROPE]
    g["w_uq"] = _cols_by_chip(duq.reshape(Q_LORA, -1), BF16)
    dcqn = _mm(dq, wl["w_uq"], tb=True, name=n("uq_dx"))
    g["w_ukv"] = _cols_by_chip(_mm(sv["ckvn"], dkv, ta=True, name=n("ukv_dw")), BF16)
    dckvn = _mm(dkv, wl["w_ukv"], tb=True, name=n("ukv_dx"))
    proj = sv["proj"]
    dcq, g["q_norm"] = _rms_bwd(proj, sp["q_norm"], dcqn, col=(Q_LORA, P_CQ // Q_LORA), name=n("q_norm"))
    dckv, g["kv_norm"] = _rms_bwd(proj, sp["kv_norm"], dckvn, col=(KV_LORA, P_CKV // KV_LORA), name=n("kv_norm"))
    dy, dz, g["ssm_norm"] = _gated_rms_bwd(sv["y"], proj, sp["ssm_norm"], dy_ssm, name=n("ssm_gate"))
    dxbc, ddt, da_log, dd_skip, ddt_bias = _ssd_bwd(
        sv["xbc"], proj, sp["dt_bias"], sp["a_log"], sp["d_skip"], sv["pstates"], dy, name=n("ssd"))
    g["a_log"], g["d_skip"], g["dt_bias"] = da_log[0, :SSM_HEADS], dd_skip[0, :SSM_HEADS], ddt_bias[0, :SSM_HEADS]
    dxbc_pre, dsw, g["ssm_conv_b"] = _ssm_conv_bwd(proj, wl["ssm_conv_w"], sp["ssm_conv_b"], dxbc, name=n("ssm_conv"))
    g["ssm_conv_w"] = _cols_by_chip(dsw, F32)
    s = proj.shape[0]
    dproj = jnp.concatenate([dxbc_pre, dz, dcq.astype(BF16), ddt, dckv.astype(BF16), dkrope,
                             jnp.zeros((s, P_IN - P_KR - LANES), BF16)], axis=1)
    dh = _mm(dproj, wl["w_in"], tb=True, name=n("proj_dx"))
    g["w_in"] = _cols_by_chip(_w_in_from_padded(_mm(sv["h"], dproj, ta=True, name=n("proj_dw"))), BF16)
    dx0, g["norm_mix"] = _rms_bwd(sv["x0"], sp["norm_mix"], dh, dx1, name=n("mix_norm"))
    return dx0, g


def _chip_peers(x, y):
    return [(1 - x, y), (x, 1 - y), (1 - x, 1 - y)]


HBM_SPEC = pl.BlockSpec(memory_space=pltpu.HBM)
SEM_SPEC = pl.BlockSpec(memory_space=pltpu.SEMAPHORE)
ANY_SPEC = pl.BlockSpec(memory_space=pl.ANY)
VMEM_SPEC = pl.BlockSpec(memory_space=pltpu.VMEM)
DATAFLOW = pltpu.SideEffectType.DATAFLOW_SIDE_EFFECTING
TOKEN_SHAPE = (8, LANES)


def _exchange_start(srcs, land_shapes, src_view, dst_view, token, *, name):
    n = len(srcs)

    def body(*refs):
        s, l, tok_in = refs[:n], refs[n:2 * n], refs[2 * n]
        send_sems, recv_sems = refs[2 * n + 1], refs[2 * n + 2]
        tok_out = refs[-1]
        x, y, c = lax.axis_index("x"), lax.axis_index("y"), lax.axis_index("c")
        me = 2 * x + y
        for k, (px, py) in enumerate(_chip_peers(x, y)):
            chip = 2 * px + py
            for t in range(n):
                pltpu.make_async_remote_copy(
                    src_ref=src_view(t, s[t], chip), dst_ref=dst_view(t, l[t], me), send_sem=send_sems.at[3 * t + k],
                    recv_sem=recv_sems.at[3 * t + k], device_id=(px, py, c), device_id_type=MESH).start()
        for t in range(n):
            pltpu.make_async_copy(src_view(t, s[t], me), dst_view(t, l[t], me), send_sems.at[3 * n + t]).start()
        tok_out[...] = tok_in[...]

    hbm = lambda t: pltpu.with_memory_space_constraint(t, pltpu.HBM)
    lands = [lax.empty(l.shape, l.dtype) for l in land_shapes]
    outs = pl.pallas_call(
        body, name=name,
        out_shape=(pltpu.SemaphoreType.DMA((4 * n,)), pltpu.SemaphoreType.DMA((3 * n,)),
                   *[pltpu.HBM(l.shape, l.dtype) for l in land_shapes], SDS(TOKEN_SHAPE, F32)),
        in_specs=[HBM_SPEC] * (2 * n) + [VMEM_SPEC], out_specs=(SEM_SPEC, SEM_SPEC, *[HBM_SPEC] * n, VMEM_SPEC),
        input_output_aliases={n + t: 2 + t for t in range(n)},
        compiler_params=pltpu.CompilerParams(has_side_effects=DATAFLOW))(*[hbm(t) for t in srcs], *[hbm(t) for t in lands], token)
    return outs[0], outs[1], list(outs[2:2 + n]), outs[-1]


def _exchange_wait(srcs, lands, send_sems, recv_sems, after, src_view, dst_view, *, name):
    n = len(srcs)

    def body(*refs):
        s, l = refs[:n], refs[n:2 * n]
        send_ref, recv_ref = refs[2 * n], refs[2 * n + 1]
        x, y, c = lax.axis_index("x"), lax.axis_index("y"), lax.axis_index("c")
        me = 2 * x + y
        for k, (px, py) in enumerate(_chip_peers(x, y)):
            chip = 2 * px + py
            for t in range(n):
                cp = pltpu.make_async_remote_copy(
                    src_ref=src_view(t, s[t], chip), dst_ref=dst_view(t, l[t], chip), send_sem=send_ref.at[3 * t + k],
                    recv_sem=recv_ref.at[3 * t + k], device_id=(px, py, c), device_id_type=MESH)
                cp.wait_send()
                cp.wait_recv()
        for t in range(n):
            pltpu.make_async_copy(src_view(t, s[t], me), dst_view(t, l[t], me), send_ref.at[3 * n + t]).wait()

    outs = pl.pallas_call(
        body, name=name, out_shape=[pltpu.HBM(l.shape, l.dtype) for l in lands],
        in_specs=[HBM_SPEC] * (2 * n) + [SEM_SPEC, SEM_SPEC, ANY_SPEC], out_specs=[HBM_SPEC] * n,
        input_output_aliases={n + t: t for t in range(n)},
        compiler_params=pltpu.CompilerParams(has_side_effects=DATAFLOW))(*srcs, *lands, send_sems, recv_sems, after)
    return list(outs)


def _gather_layer_start(shards, li, token):
    src_view = lambda t, ref, chip: ref.at[li]
    dst_view = lambda t, ref, chip: ref.at[chip]
    send_sems, recv_sems, lands, token = _exchange_start(
        shards, [SDS((N_CHIPS,) + s.shape[1:], s.dtype) for s in shards], src_view, dst_view, token, name=f"gather{li}_start")
    return (shards, lands, send_sems, recv_sems, src_view, dst_view, f"gather{li}_wait"), token


def _scatter_layer_start(grads, li, token):
    view = lambda t, ref, chip: ref.at[chip]
    send_sems, recv_sems, lands, token = _exchange_start(
        grads, [SDS(g.shape, g.dtype) for g in grads], view, view, token, name=f"scatter{li}_start")
    return (grads, lands, send_sems, recv_sems, view, view, f"scatter{li}_wait"), token


def _exchange_finish(handle, after):
    srcs, lands, send_sems, recv_sems, src_view, dst_view, name = handle
    return _exchange_wait(srcs, lands, send_sems, recv_sems, after, src_view, dst_view, name=name)


def _swap_cores(bufs, *, name):
    n = len(bufs)

    def body(*refs):
        srcs, outs = refs[:n], refs[n:2 * n]
        send_sems, recv_sems = refs[2 * n:]
        x, y, c = lax.axis_index("x"), lax.axis_index("y"), lax.axis_index("c")
        cps = [pltpu.make_async_remote_copy(src_ref=srcs[t], dst_ref=outs[t], send_sem=send_sems.at[t], recv_sem=recv_sems.at[t],
                                            device_id=(x, y, 1 - c), device_id_type=MESH) for t in range(n)]
        for cp in cps:
            cp.start()
        for cp in cps:
            cp.wait()

    any_spec = pl.BlockSpec(memory_space=pl.ANY)
    return pl.pallas_call(body, in_specs=[any_spec] * n, out_specs=[any_spec] * n,
                          out_shape=[SDS(b.shape, b.dtype) for b in bufs],
                          scratch_shapes=[pltpu.SemaphoreType.DMA((n,)), pltpu.SemaphoreType.DMA((n,))], name=name)(*bufs)


def _all_gather8(src, *, name):
    def body(src_ref, out_ref, send_sems, recv_sems, local_sem):
        x, y, c = lax.axis_index("x"), lax.axis_index("y"), lax.axis_index("c")
        me = 4 * x + 2 * y + c
        mine = pltpu.make_async_copy(src_ref, out_ref.at[me], local_sem)
        mine.start()

        def peer(k):
            return (x ^ (k >> 2 & 1), y ^ (k >> 1 & 1), c ^ (k & 1))

        sends = []
        for k in range(1, N_DEV):
            cp = pltpu.make_async_remote_copy(src_ref=src_ref, dst_ref=out_ref.at[me], send_sem=send_sems.at[k - 1],
                                              recv_sem=recv_sems.at[k - 1], device_id=peer(k), device_id_type=MESH)
            cp.start()
            sends.append(cp)
        for k in range(1, N_DEV):
            px, py, pc = peer(k)
            pltpu.make_async_remote_copy(src_ref=src_ref, dst_ref=out_ref.at[4 * px + 2 * py + pc],
                                         send_sem=send_sems.at[k - 1], recv_sem=recv_sems.at[k - 1],
                                         device_id=peer(k), device_id_type=MESH).wait_recv()
        for cp in sends:
            cp.wait_send()
        mine.wait()

    any_spec = pl.BlockSpec(memory_space=pl.ANY)
    return pl.pallas_call(
        body, in_specs=[any_spec], out_specs=any_spec, out_shape=SDS((N_DEV,) + src.shape, src.dtype),
        scratch_shapes=[pltpu.SemaphoreType.DMA((N_DEV - 1,)), pltpu.SemaphoreType.DMA((N_DEV - 1,)), pltpu.SemaphoreType.DMA],
        name=name)(src)


def _adam_terms(w, g, m, v):
    m = ADAM_B1 * m + (1.0 - ADAM_B1) * g
    v = ADAM_B2 * v + (1.0 - ADAM_B2) * (g * g)
    m_hat = m / (1.0 - ADAM_B1 ** ADAM_STEP)
    v_hat = v / (1.0 - ADAM_B2 ** ADAM_STEP)
    delta = -ADAM_LR * (m_hat / (jnp.sqrt(v_hat) + ADAM_EPS) + ADAM_WD * w)
    return delta, m, v


def _adamw_shard(mine, other, w, m, v, *, name):
    d, a, b = w.shape
    tr = next((t for t in (128, 64, 32, 16) if a % t == 0), a)

    def body(*refs):
        ga, gb = refs[:d], refs[d:2 * d]
        w_ref, m_ref, v_ref, g_ref, d_ref, nm_ref, nv_ref = refs[2 * d:]

        def plane(ref):
            return ((ref[0].astype(F32) + ref[1].astype(F32)) + ref[2].astype(F32)) + ref[3].astype(F32)

        for lp in range(d):
            @pl.when(pl.program_id(0) == lp)
            def _(lp=lp):
                g = plane(ga[lp]) + plane(gb[lp])
                delta, mn, vn = _adam_terms(w_ref[...], g, m_ref[...], v_ref[...])
                g_ref[...] = g
                d_ref[...] = delta
                nm_ref[...] = mn
                nv_ref[...] = vn

    gspecs = [pl.BlockSpec((N_CHIPS, tr, b), lambda l, i, lp=lp: (0, jnp.where(l == lp, i, 0), 0)) for lp in range(d)]
    blk = pl.BlockSpec((None, tr, b), lambda l, i: (l, i, 0))
    shp = SDS((d, a, b), F32)
    return pl.pallas_call(
        body, grid=(d, a // tr), in_specs=gspecs + gspecs + [blk, blk, blk], out_specs=(blk,) * 4, out_shape=(shp,) * 4,
        name=name, compiler_params=_params(("arbitrary", "arbitrary"), 48 << 20))(*mine, *other, w, m, v)


def _adamw_small(g8, w, m, v, *, name):
    n = w.shape[1]

    def body(g8_ref, w_ref, m_ref, v_ref, g_ref, d_ref, nm_ref, nv_ref):
        g = g8_ref[0]
        for k in range(1, N_DEV):
            g = g + g8_ref[k]
        delta, mn, vn = _adam_terms(w_ref[...], g, m_ref[...], v_ref[...])
        g_ref[...] = g
        d_ref[...] = delta
        nm_ref[...] = mn
        nv_ref[...] = vn

    shp = SDS((1, n), F32)
    return pl.pallas_call(body, out_shape=(shp,) * 4, name=name, compiler_params=_params(None, 24 << 20))(g8, w, m, v)


def _rope_tables(positions):
    inv_freq = 1.0 / (ROPE_THETA ** (jnp.arange(0, QK_ROPE, 2, dtype=F32) / QK_ROPE))
    ang = positions.astype(F32)[:, None] * inv_freq
    c, s = jnp.cos(ang), jnp.sin(ang)
    n = positions.shape[0]
    pad = jnp.zeros((n, HEAD_PAD - QK_NOPE - QK_ROPE), F32)
    cos = jnp.concatenate([jnp.ones((n, QK_NOPE), F32), c, c, pad], axis=1)
    sins = jnp.concatenate([jnp.zeros((n, QK_NOPE), F32), -s, s, pad], axis=1)
    return cos, sins


def _pad_lanes(v):
    return _pad_cols(v.reshape(1, -1), LANES)


def _local_step(x, mem, positions, layer_weights, small, final_norm, loss_target, on_layer_grads, token):
    cos, sins = _rope_tables(positions)
    saved, gws, wls, sps = [], [], [], []
    h = x
    for li in range(DEPTH):
        gw = layer_weights(li, h)
        wl = _layer_weights(gw)
        sp = {k: small[k][li] for k, _ in SMALL}
        if li == 0:
            sp["norm_mix"] = sp["norm_mix"] + token[0, 0]
        for k in ("dt_bias", "a_log", "d_skip"):
            sp[k] = _pad_lanes(sp[k])
        h, sv = _layer_fwd(h, mem, cos, sins, gw, wl, sp, li)
        saved.append(sv)
        gws.append(gw)
        wls.append(wl)
        sps.append(sp)
    loss, dh, g_final = _final_loss(h, final_norm, loss_target, name="final_loss")
    grads = [None] * DEPTH
    started = None
    for li in reversed(range(DEPTH)):
        sp = sps[li]
        if started is not None:
            sp = dict(sp, ffn_conv_b=sp["ffn_conv_b"] + started[0, 0])
        dh, grads[li] = _layer_bwd(dh, mem, cos, sins, gws[li], wls[li], sp, saved[li], li)
        started = on_layer_grads(li, grads[li])
    return loss, dh, grads, g_final


def _gathered_views(lands):
    return {k: (t.reshape(-1, t.shape[-1]) if axis == 0 else t) for (k, _, axis), t in zip(MATS, lands)}


def kernel(x, mem, positions, norm_mix, w_in, ssm_conv_w, ssm_conv_b, dt_bias, a_log, d_skip, ssm_norm, q_norm, w_uq, kv_norm, w_ukv, attn_out_norm, w_out, norm_mem_q, norm_mem_kv, w_mq, w_mk, w_mv, w_mo, norm_ffn, w_up, ffn_conv_w, ffn_conv_b, w_down, final_norm, loss_target, m_norm_mix, m_w_in, m_ssm_conv_w, m_ssm_conv_b, m_dt_bias, m_a_log, m_d_skip, m_ssm_norm, m_q_norm, m_w_uq, m_kv_norm, m_w_ukv, m_attn_out_norm, m_w_out, m_norm_mem_q, m_norm_mem_kv, m_w_mq, m_w_mk, m_w_mv, m_w_mo, m_norm_ffn, m_w_up, m_ffn_conv_w, m_ffn_conv_b, m_w_down, m_final_norm, v_norm_mix, v_w_in, v_ssm_conv_w, v_ssm_conv_b, v_dt_bias, v_a_log, v_d_skip, v_ssm_norm, v_q_norm, v_w_uq, v_kv_norm, v_w_ukv, v_attn_out_norm, v_w_out, v_norm_mem_q, v_norm_mem_kv, v_w_mq, v_w_mk, v_w_mv, v_w_mo, v_norm_ffn, v_w_up, v_ffn_conv_w, v_ffn_conv_b, v_w_down, v_final_norm):
    args = dict(locals())
    names = ["norm_mix", "w_in", "ssm_conv_w", "ssm_conv_b", "dt_bias", "a_log", "d_skip", "ssm_norm", "q_norm", "w_uq",
             "kv_norm", "w_ukv", "attn_out_norm", "w_out", "norm_mem_q", "norm_mem_kv", "w_mq", "w_mk", "w_mv", "w_mo",
             "norm_ffn", "w_up", "ffn_conv_w", "ffn_conv_b", "w_down", "final_norm"]
    wts = {k: args[k] for k in names}
    mom = {k: args["m_" + k] for k in names}
    var = {k: args["v_" + k] for k in names}
    mat_names = [k for k, _, _ in MATS]

    shards = [wts[k] if k in F32_ON_WIRE else wts[k].astype(BF16) for k in mat_names]
    token = jnp.zeros(TOKEN_SHAPE, F32)
    gathers = []
    for li in range(DEPTH):
        handle, token = _gather_layer_start(shards, li, token)
        gathers.append(handle)
    small = {k: wts[k] for k, _ in SMALL}

    scatters = [None] * DEPTH

    def on_layer_grads(li, g):
        scatters[li], started = _scatter_layer_start([g[k] for k in mat_names], li, jnp.zeros(TOKEN_SHAPE, F32))
        return started

    loss, grad_x, grads, g_final = _local_step(
        x[0], mem[0], positions[0], lambda li, after: _gathered_views(_exchange_finish(gathers[li], after)), small,
        wts["final_norm"], loss_target[0], on_layer_grads, token)
    loss = lax.psum(loss, ("x", "y", "c"))

    mine = [_exchange_finish(scatters[li], grad_x) for li in range(DEPTH)]
    nm = len(mat_names)
    swapped = _swap_cores([b for layer in mine for b in layer], name="swap_cores")
    other = [swapped[li * nm:(li + 1) * nm] for li in range(DEPTH)]
    mat_out = {k: _adamw_shard([mine[li][t] for li in range(DEPTH)], [other[li][t] for li in range(DEPTH)],
                               wts[k], mom[k], var[k], name=f"adamw_{k}") for t, k in enumerate(mat_names)}

    def pack_small(get, fin):
        flat = [get(k).reshape(-1) for k, _ in SMALL] + [fin.reshape(-1)]
        n = sum(f.shape[0] for f in flat)
        return jnp.concatenate(flat + [jnp.zeros((-n % PACK_COLS,), F32)]).reshape(1, -1)

    gs = pack_small(lambda k: jnp.stack([grads[li][k] for li in range(DEPTH)]), g_final)
    g8 = _all_gather8(gs, name="gather_small_grads")
    small_out = _adamw_small(g8, pack_small(wts.get, wts["final_norm"]), pack_small(mom.get, mom["final_norm"]),
                             pack_small(var.get, var["final_norm"]), name="adamw_small")

    def unpack_small(buf):
        out, off = {}, 0
        for k, nel in SMALL:
            out[k] = buf[0, off:off + DEPTH * nel].reshape(DEPTH, nel)
            off += DEPTH * nel
        out["final_norm"] = buf[0, off:off + D_MODEL]
        return out

    small_res = [unpack_small(b) for b in small_out]
    res = []
    for kind in range(4):
        for k in names:
            res.append(small_res[kind][k] if k in small_res[kind] else mat_out[k][kind])
    return (loss, grad_x[None], *res)
```

```python
import functools
import math

import jax
import jax.numpy as jnp
from jax import lax
from jax.experimental import pallas as pl
from jax.experimental.pallas import tpu as pltpu

F32 = jnp.float32
BF16 = jnp.bfloat16
HIGHEST = lax.Precision.HIGHEST
SDS = jax.ShapeDtypeStruct
MESH = pl.DeviceIdType.MESH

D_MODEL = 1024
DEPTH = 4
EPS = 1e-6
SSM_HEADS = 16
SSM_HEAD_DIM = 64
D_SSM = 1024
SSM_GROUPS = 4
SSM_STATE = 128
SSM_CONV = 4
SSM_CHUNK = 128
CONV_CH = 2048
MLA_HEADS = 16
QK_NOPE = 64
QK_ROPE = 32
V_DIM = 64
Q_LORA = 384
KV_LORA = 256
ROPE_THETA = 10000.0
MEM_HEADS = 4
MEM_HEAD_DIM = 256
D_FF = 2816
FFN_CONV = 3
D_IN = 3760
ADAM_LR = 0.001
ADAM_B1 = 0.9
ADAM_B2 = 0.999
ADAM_EPS = 1e-08
ADAM_WD = 0.01
ADAM_STEP = 10

LANES = 128
HEAD_PAD = 128
N_CHIPS = 4
N_DEV = 8
VMEM_CAP_MB = 56

P_XBC, P_Z, P_CQ, P_DT, P_CKV, P_KR, P_IN = 0, 2048, 3072, 3456, 3584, 3840, 4096
NEG = -1e30


def _tile(n, pref):
    t = (min(n, pref) // LANES) * LANES
    while t >= LANES:
        if n % t == 0:
            return t
        t -= LANES
    return n


def _params(sem=None, vmem_bytes=None):
    kw = {}
    if sem is not None:
        kw["dimension_semantics"] = sem
    if vmem_bytes is not None:
        kw["vmem_limit_bytes"] = int(min(max(vmem_bytes, 16 << 20), VMEM_CAP_MB << 20))
    return pltpu.CompilerParams(**kw)


def _nbytes(shape, dtype):
    return math.prod(shape) * jnp.dtype(dtype).itemsize


def _hbm_call(*args, **kwargs):
    call = pl.pallas_call(*args, **kwargs)
    return lambda *ops: call(*[pltpu.with_memory_space_constraint(o, pltpu.HBM) for o in ops])


def _mm(a, b, *, ta=False, tb=False, res=None, out_dtype=F32, name, a_col=None, b_lead=(), b_rows=None,
        b_chips=None, o_chips=None):
    if ta:
        k, m = a.shape
    else:
        m, k = (a.shape[0], a.shape[1] if a_col is None else a_col[0])
    rows_b, cols_b = b.shape[-2:]
    row0 = 0
    if b_rows is not None:
        row0, rows_b = b_rows
    nlead = len(b_lead)
    if b_chips is not None:
        assert not tb
        kb, tn, n = rows_b, cols_b, b_chips[1] * cols_b
        b_blk = (None,) * (1 + nlead) + (kb, tn)
        b_map = lambda i, j: (b_chips[0] + j,) + tuple(b_lead) + (0, 0)
    elif tb:
        n, kb = rows_b, cols_b
        tn = _tile(n, 512)
        assert row0 % tn == 0
        b_blk = (None,) * nlead + (tn, kb)
        b_map = lambda i, j: tuple(b_lead) + (j + row0 // tn, 0)
    else:
        kb, n = rows_b, cols_b
        tn = o_chips if o_chips else _tile(n, 512)
        assert row0 % kb == 0
        b_blk = (None,) * nlead + (kb, tn)
        b_map = lambda i, j: tuple(b_lead) + (row0 // kb, j)
    assert k == kb, (a.shape, b.shape, ta, tb, k, kb)
    tm = _tile(m, 512)
    if ta:
        a_blk, a_map = (k, tm), (lambda i, j: (0, i))
    else:
        a_blk, a_map = (tm, k), ((lambda i, j: (i, 0)) if a_col is None else (lambda i, j: (i, a_col[1])))
    if o_chips:
        o_spec = pl.BlockSpec((None, tm, tn), lambda i, j: (j, i, 0))
        o_shape = SDS((n // tn, m, tn), out_dtype)
    else:
        o_spec = pl.BlockSpec((tm, tn), lambda i, j: (i, j))
        o_shape = SDS((m, n), out_dtype)
    dims = (((0 if ta else 1,), (1 if tb else 0,)), ((), ()))
    has_res = res is not None

    def body(*refs):
        a_ref, b_ref = refs[0], refs[1]
        o_ref = refs[-1]
        acc = lax.dot_general(a_ref[...].astype(BF16), b_ref[...].astype(BF16), dims, preferred_element_type=F32)
        if has_res:
            acc = acc + refs[2][...]
        o_ref[...] = acc.astype(o_ref.dtype)

    bb = tuple(d for d in b_blk if d is not None)
    vmem = 2 * (_nbytes(a_blk, a.dtype) + _nbytes(bb, b.dtype) + (2 if has_res else 1) * _nbytes((tm, tn), F32))
    vmem += _nbytes(a_blk, BF16) + _nbytes(bb, BF16) + 2 * _nbytes((tm, tn), F32) + (4 << 20)
    args = (a, b) + ((res,) if has_res else ())
    specs = [pl.BlockSpec(a_blk, a_map), pl.BlockSpec(b_blk, b_map)] + ([o_spec] if has_res else [])
    return _hbm_call(body, grid=(m // tm, n // tn), in_specs=specs, out_specs=o_spec, out_shape=o_shape, name=name,
                          compiler_params=_params(("parallel", "parallel"), vmem))(*args)


def _sigmoid(x):
    return 1.0 / (1.0 + jnp.exp(-x))


def _rms_fwd(x, g, *, col=None, name):
    s = x.shape[0]
    w, ci = (x.shape[1], 0) if col is None else col
    tm = min(s, 512)

    def body(x_ref, g_ref, o_ref):
        xv = x_ref[...].astype(F32)
        r = lax.rsqrt(jnp.mean(xv * xv, axis=-1, keepdims=True) + EPS)
        o_ref[...] = (xv * r * g_ref[...]).astype(o_ref.dtype)

    return _hbm_call(
        body, grid=(s // tm,),
        in_specs=[pl.BlockSpec((tm, w), lambda i: (i, ci)), pl.BlockSpec((1, w), lambda i: (0, 0))],
        out_specs=pl.BlockSpec((tm, w), lambda i: (i, 0)), out_shape=SDS((s, w), BF16), name=name,
        compiler_params=_params(("parallel",), 10 * tm * w * 4))(x, g.reshape(1, w))


def _rms_bwd(x, g, dy, dres=None, *, col=None, name):
    s = x.shape[0]
    w, ci = (x.shape[1], 0) if col is None else col
    tm = min(s, 512)
    has_res = dres is not None

    def body(*refs):
        x_ref, g_ref, dy_ref = refs[:3]
        dx_ref, dg_ref = refs[-2:]
        xv = x_ref[...].astype(F32)
        dyv = dy_ref[...].astype(F32)
        r = lax.rsqrt(jnp.mean(xv * xv, axis=-1, keepdims=True) + EPS)
        u = dyv * g_ref[...]
        dx = r * u - xv * (r * r * r) * jnp.mean(xv * u, axis=-1, keepdims=True)
        if has_res:
            dx = dx + refs[3][...]
        dx_ref[...] = dx

        @pl.when(pl.program_id(0) == 0)
        def _():
            dg_ref[...] = jnp.zeros_like(dg_ref)

        dg_ref[...] += jnp.sum(dyv * xv * r, axis=0, keepdims=True)

    blk = pl.BlockSpec((tm, w), lambda i: (i, 0))
    specs = [pl.BlockSpec((tm, w), lambda i: (i, ci)), pl.BlockSpec((1, w), lambda i: (0, 0)), blk]
    args = [x, g.reshape(1, w), dy]
    if has_res:
        specs.append(blk)
        args.append(dres)
    dx, dg = _hbm_call(
        body, grid=(s // tm,), in_specs=specs,
        out_specs=(blk, pl.BlockSpec((1, w), lambda i: (0, 0))),
        out_shape=(SDS((s, w), F32), SDS((1, w), F32)), name=name,
        compiler_params=_params(("arbitrary",), 16 * tm * w * 4))(*args)
    return dx, dg.reshape(w)


def _gated_rms_fwd(y, proj, g, *, name):
    s, w = y.shape
    tm = min(s, 512)

    def body(y_ref, z_ref, g_ref, o_ref):
        z = z_ref[...]
        t = y_ref[...] * (z * _sigmoid(z))
        r = lax.rsqrt(jnp.mean(t * t, axis=-1, keepdims=True) + EPS)
        o_ref[...] = (t * r * g_ref[...]).astype(o_ref.dtype)

    blk = pl.BlockSpec((tm, w), lambda i: (i, 0))
    return _hbm_call(
        body, grid=(s // tm,),
        in_specs=[blk, pl.BlockSpec((tm, w), lambda i: (i, P_Z // w)), pl.BlockSpec((1, w), lambda i: (0, 0))],
        out_specs=blk, out_shape=SDS((s, w), BF16), name=name,
        compiler_params=_params(("parallel",), 14 * tm * w * 4))(y, proj, g.reshape(1, w))


def _gated_rms_bwd(y, proj, g, dout, *, name):
    s, w = y.shape
    tm = min(s, 512)

    def body(y_ref, z_ref, g_ref, do_ref, dy_ref, dz_ref, dg_ref):
        z = z_ref[...]
        yv = y_ref[...]
        dov = do_ref[...]
        sg = _sigmoid(z)
        sz = z * sg
        t = yv * sz
        r = lax.rsqrt(jnp.mean(t * t, axis=-1, keepdims=True) + EPS)
        u = dov * g_ref[...]
        dt = r * u - t * (r * r * r) * jnp.mean(t * u, axis=-1, keepdims=True)
        dy_ref[...] = dt * sz
        dz_ref[...] = (dt * yv * (sg * (1.0 + z * (1.0 - sg)))).astype(dz_ref.dtype)

        @pl.when(pl.program_id(0) == 0)
        def _():
            dg_ref[...] = jnp.zeros_like(dg_ref)

        dg_ref[...] += jnp.sum(dov * t * r, axis=0, keepdims=True)

    blk = pl.BlockSpec((tm, w), lambda i: (i, 0))
    vec = pl.BlockSpec((1, w), lambda i: (0, 0))
    dy, dz, dg = _hbm_call(
        body, grid=(s // tm,),
        in_specs=[blk, pl.BlockSpec((tm, w), lambda i: (i, P_Z // w)), vec, blk],
        out_specs=(blk, blk, vec), out_shape=(SDS((s, w), F32), SDS((s, w), BF16), SDS((1, w), F32)), name=name,
        compiler_params=_params(("arbitrary",), 24 * tm * w * 4))(y, proj, g.reshape(1, w), dout)
    return dy, dz, dg.reshape(w)


def _final_loss(x, g, target, *, name):
    s, w = x.shape
    tm = min(s, 512)

    def body(x_ref, g_ref, t_ref, loss_ref, dx_ref, dg_ref):
        xv = x_ref[...]
        gv = g_ref[...]
        r = lax.rsqrt(jnp.mean(xv * xv, axis=-1, keepdims=True) + EPS)
        xn = xv * r
        diff = xn * gv - t_ref[...]
        dy = diff * (1.0 / w)
        u = dy * gv
        dx_ref[...] = r * u - xv * (r * r * r) * jnp.mean(xv * u, axis=-1, keepdims=True)

        @pl.when(pl.program_id(0) == 0)
        def _():
            dg_ref[...] = jnp.zeros_like(dg_ref)
            loss_ref[...] = jnp.zeros_like(loss_ref)

        dg_ref[...] += jnp.sum(dy * xn, axis=0, keepdims=True)
        part = jnp.sum(jnp.sum(diff * diff, axis=1, keepdims=True), axis=0, keepdims=True) * (0.5 / w)
        loss_ref[...] += jnp.broadcast_to(part, loss_ref.shape)

    blk = pl.BlockSpec((tm, w), lambda i: (i, 0))
    vec = pl.BlockSpec((1, w), lambda i: (0, 0))
    loss, dx, dg = _hbm_call(
        body, grid=(s // tm,), in_specs=[blk, vec, blk],
        out_specs=(pl.BlockSpec((1, LANES), lambda i: (0, 0)), blk, vec),
        out_shape=(SDS((1, LANES), F32), SDS((s, w), F32), SDS((1, w), F32)), name=name,
        compiler_params=_params(("arbitrary",), 16 * tm * w * 4))(x, g.reshape(1, w), target)
    return loss[0, 0], dx, dg.reshape(w)


def _shift_down(x, k):
    if k == 0:
        return x
    row = lax.broadcasted_iota(jnp.int32, x.shape, 0)
    return jnp.where(row < k, 0.0, pltpu.roll(x, k, axis=0))


def _shift_up(x, k):
    if k == 0:
        return x
    s = x.shape[0]
    row = lax.broadcasted_iota(jnp.int32, x.shape, 0)
    return jnp.where(row >= s - k, 0.0, pltpu.roll(x, s - k, axis=0))


def _conv_pre(x, w, b, kw):
    pre = b
    for j in range(kw):
        pre = pre + w[j:j + 1, :] * _shift_down(x, kw - 1 - j)
    return pre


def _conv_bwd_terms(x, w, dpre, kw):
    dx = jnp.zeros_like(x)
    dws = []
    for j in range(kw):
        dx = dx + w[j:j + 1, :] * _shift_up(dpre, kw - 1 - j)
        dws.append(jnp.sum(dpre * _shift_down(x, kw - 1 - j), axis=0, keepdims=True))
    return dx, jnp.concatenate(dws, axis=0), jnp.sum(dpre, axis=0, keepdims=True)


def _ssm_conv_fwd(proj, w, b, *, name):
    s = proj.shape[0]
    cw = 256

    def body(x_ref, w_ref, b_ref, o_ref):
        pre = _conv_pre(x_ref[...], w_ref[...], b_ref[...], SSM_CONV)
        o_ref[...] = pre * _sigmoid(pre)

    return _hbm_call(
        body, grid=(CONV_CH // cw,),
        in_specs=[pl.BlockSpec((s, cw), lambda j: (0, j)), pl.BlockSpec((SSM_CONV, cw), lambda j: (0, j)),
                  pl.BlockSpec((1, cw), lambda j: (0, j))],
        out_specs=pl.BlockSpec((s, cw), lambda j: (0, j)), out_shape=SDS((s, CONV_CH), F32), name=name,
        compiler_params=_params(("parallel",), 12 * s * cw * 4))(proj, w, b.reshape(1, CONV_CH))


def _ssm_conv_bwd(proj, w, b, dxbc, *, name):
    s = proj.shape[0]
    cw = 256

    def body(x_ref, w_ref, b_ref, dy_ref, dx_ref, dw_ref, db_ref):
        x = x_ref[...]
        wv = w_ref[...]
        pre = _conv_pre(x, wv, b_ref[...], SSM_CONV)
        sg = _sigmoid(pre)
        dpre = dy_ref[...] * (sg * (1.0 + pre * (1.0 - sg)))
        dx, dw, db = _conv_bwd_terms(x, wv, dpre, SSM_CONV)
        dx_ref[...] = dx.astype(dx_ref.dtype)
        dw_ref[...] = dw
        db_ref[...] = db

    col = pl.BlockSpec((s, cw), lambda j: (0, j))
    wsp = pl.BlockSpec((SSM_CONV, cw), lambda j: (0, j))
    bsp = pl.BlockSpec((1, cw), lambda j: (0, j))
    dx, dw, db = _hbm_call(
        body, grid=(CONV_CH // cw,), in_specs=[col, wsp, bsp, col], out_specs=(col, wsp, bsp),
        out_shape=(SDS((s, CONV_CH), BF16), SDS((SSM_CONV, CONV_CH), F32), SDS((1, CONV_CH), F32)), name=name,
        compiler_params=_params(("parallel",), 20 * s * cw * 4))(proj, w, b.reshape(1, CONV_CH), dxbc)
    return dx, dw, db.reshape(CONV_CH)


def _ffn_conv_fwd(up_g, up_v, w, b, *, name):
    s = up_g.shape[0]
    cw = 256
    nb = D_FF // cw

    def body(g_ref, v_ref, wg_ref, wv_ref, bg_ref, bv_ref, o_ref):
        gate = _conv_pre(g_ref[...], wg_ref[...], bg_ref[...], FFN_CONV)
        val = _conv_pre(v_ref[...], wv_ref[...], bv_ref[...], FFN_CONV)
        o_ref[...] = (gate * _sigmoid(gate) * val).astype(o_ref.dtype)

    col = pl.BlockSpec((s, cw), lambda j: (0, j))
    b2 = b.reshape(1, 2 * D_FF)
    return _hbm_call(
        body, grid=(nb,),
        in_specs=[col, col, pl.BlockSpec((FFN_CONV, cw), lambda j: (0, j)), pl.BlockSpec((FFN_CONV, cw), lambda j: (0, j + nb)),
                  pl.BlockSpec((1, cw), lambda j: (0, j)), pl.BlockSpec((1, cw), lambda j: (0, j + nb))],
        out_specs=col, out_shape=SDS((s, D_FF), BF16), name=name,
        compiler_params=_params(("parallel",), 16 * s * cw * 4))(up_g, up_v, w, w, b2, b2)


def _ffn_conv_bwd(up_g, up_v, w, b, dact, *, name):
    s = up_g.shape[0]
    cw = 256
    nb = D_FF // cw

    def body(g_ref, v_ref, wg_ref, wv_ref, bg_ref, bv_ref, da_ref, dg_ref, dv_ref, dwg_ref, dwv_ref, dbg_ref, dbv_ref):
        xg, xv = g_ref[...], v_ref[...]
        wg, wv = wg_ref[...], wv_ref[...]
        gate = _conv_pre(xg, wg, bg_ref[...], FFN_CONV)
        val = _conv_pre(xv, wv, bv_ref[...], FFN_CONV)
        da = da_ref[...].astype(F32)
        sg = _sigmoid(gate)
        dgate = da * val * (sg * (1.0 + gate * (1.0 - sg)))
        dval = da * gate * sg
        dxg, dwg, dbg = _conv_bwd_terms(xg, wg, dgate, FFN_CONV)
        dxv, dwv, dbv = _conv_bwd_terms(xv, wv, dval, FFN_CONV)
        dg_ref[...] = dxg.astype(dg_ref.dtype)
        dv_ref[...] = dxv.astype(dv_ref.dtype)
        dwg_ref[...] = dwg
        dwv_ref[...] = dwv
        dbg_ref[...] = dbg
        dbv_ref[...] = dbv

    col = pl.BlockSpec((s, cw), lambda j: (0, j))
    wsp = pl.BlockSpec((FFN_CONV, cw), lambda j: (0, j))
    bsp = pl.BlockSpec((1, cw), lambda j: (0, j))
    b2 = b.reshape(1, 2 * D_FF)
    dg, dv, dwg, dwv, dbg, dbv = _hbm_call(
        body, grid=(nb,),
        in_specs=[col, col, wsp, pl.BlockSpec((FFN_CONV, cw), lambda j: (0, j + nb)), bsp,
                  pl.BlockSpec((1, cw), lambda j: (0, j + nb)), col],
        out_specs=(col, col, wsp, wsp, bsp, bsp),
        out_shape=(SDS((s, D_FF), BF16), SDS((s, D_FF), BF16), SDS((FFN_CONV, D_FF), F32), SDS((FFN_CONV, D_FF), F32),
                   SDS((1, D_FF), F32), SDS((1, D_FF), F32)), name=name,
        compiler_params=_params(("parallel",), 32 * s * cw * 4))(up_g, up_v, w, w, b2, b2, dact)
    return dg, dv, jnp.concatenate([dwg, dwv], axis=1), jnp.concatenate([dbg, dbv], axis=1).reshape(2 * D_FF)


def _dot(a, b):
    return jnp.dot(a.astype(BF16), b.astype(BF16), preferred_element_type=F32)


def _dot_nt(a, b):
    return lax.dot_general(a.astype(BF16), b.astype(BF16), (((1,), (1,)), ((), ())), preferred_element_type=F32)


def _dot_tn(a, b):
    return lax.dot_general(a.astype(BF16), b.astype(BF16), (((0,), (0,)), ((), ())), preferred_element_type=F32)


def _ssd_chunk_terms(dtraw, bias, a_log):
    ell = dtraw.shape[0]
    lane = lax.broadcasted_iota(jnp.int32, dtraw.shape, 1)
    valid = lane < SSM_HEADS
    pre = dtraw + bias
    dt = jnp.where(valid, jnp.where(pre > 20.0, pre, jnp.log(1.0 + jnp.exp(jnp.minimum(pre, 20.0)))), 0.0)
    a = -jnp.exp(a_log)
    ad = dt * a
    row = lax.broadcasted_iota(jnp.int32, (ell, ell), 0)
    colm = lax.broadcasted_iota(jnp.int32, (ell, ell), 1)
    tril = row >= colm
    cs = jnp.dot(tril.astype(F32), ad, precision=HIGHEST, preferred_element_type=F32)
    cs_last = cs[ell - 1:ell, :]
    return pre, dt, a, cs, cs_last, tril


def _lane_put(col, h, shape):
    lane = lax.broadcasted_iota(jnp.int32, shape, 1)
    return jnp.where(lane == h, col, 0.0)


def _ssd_fwd(xbc, proj, dt_bias, a_log, d_skip, *, name):
    s = xbc.shape[0]
    nc = s // SSM_CHUNK
    ell, n, p = SSM_CHUNK, SSM_STATE, SSM_HEAD_DIM
    rpg = SSM_HEADS // SSM_GROUPS

    def body(x_ref, dt_ref, bias_ref, alog_ref, dskip_ref, y_ref, ps_ref, state):
        @pl.when(pl.program_id(0) == 0)
        def _():
            state[...] = jnp.zeros_like(state)

        _, dt, _, cs, cs_last, tril = _ssd_chunk_terms(dt_ref[...], bias_ref[...], alog_ref[...])
        e = jnp.exp(cs)
        ds = jnp.exp(cs_last - cs)
        cd = jnp.exp(cs_last)
        cst = cs.T
        dskip = dskip_ref[...]
        ps_ref[0] = state[...]
        for g in range(SSM_GROUPS):
            bg = x_ref[:, D_SSM + n * g:D_SSM + n * (g + 1)]
            cg = x_ref[:, D_SSM + n * (SSM_GROUPS + g):D_SSM + n * (SSM_GROUPS + g + 1)]
            cb = _dot_nt(cg, bg)
            for r in range(rpg):
                h = g * rpg + r
                hs = slice(p * h, p * (h + 1))
                xs = x_ref[:, hs]
                xd = xs * dt[:, h:h + 1]
                lmat = jnp.exp(jnp.where(tril, cs[:, h:h + 1] - cst[h:h + 1, :], -jnp.inf))
                prev = state[:, hs]
                y = _dot(cb * lmat, xd) + _dot(cg, prev) * e[:, h:h + 1] + xs * dskip[:, h:h + 1]
                y_ref[:, hs] = y
                state[:, hs] = prev * cd[:, h:h + 1] + _dot_tn(bg, xd * ds[:, h:h + 1])

    vec = pl.BlockSpec((1, LANES), lambda c: (0, 0))
    return _hbm_call(
        body, grid=(nc,),
        in_specs=[pl.BlockSpec((ell, CONV_CH), lambda c: (c, 0)), pl.BlockSpec((ell, LANES), lambda c: (c, P_DT // LANES)),
                  vec, vec, vec],
        out_specs=(pl.BlockSpec((ell, D_SSM), lambda c: (c, 0)), pl.BlockSpec((1, n, D_SSM), lambda c: (c, 0, 0))),
        out_shape=(SDS((s, D_SSM), F32), SDS((nc, n, D_SSM), F32)),
        scratch_shapes=[pltpu.VMEM((n, D_SSM), F32)], name=name,
        compiler_params=_params(("arbitrary",), 24 << 20))(xbc, proj, dt_bias, a_log, d_skip)


def _ssd_bwd(xbc, proj, dt_bias, a_log, d_skip, prev_states, dy, *, name):
    s = xbc.shape[0]
    nc = s // SSM_CHUNK
    ell, n, p = SSM_CHUNK, SSM_STATE, SSM_HEAD_DIM
    rpg = SSM_HEADS // SSM_GROUPS

    def body(x_ref, dt_ref, bias_ref, alog_ref, dskip_ref, ps_ref, dy_ref,
             dx_ref, ddt_ref, dalog_ref, ddskip_ref, dbias_ref, dstate):
        @pl.when(pl.program_id(0) == 0)
        def _():
            dstate[...] = jnp.zeros_like(dstate)
            dalog_ref[...] = jnp.zeros_like(dalog_ref)
            ddskip_ref[...] = jnp.zeros_like(ddskip_ref)
            dbias_ref[...] = jnp.zeros_like(dbias_ref)

        pre, dt, a, cs, cs_last, tril = _ssd_chunk_terms(dt_ref[...], bias_ref[...], alog_ref[...])
        e = jnp.exp(cs)
        ds = jnp.exp(cs_last - cs)
        cd = jnp.exp(cs_last)
        cst = cs.T
        dskip = dskip_ref[...]
        shape = (ell, LANES)
        ddt_acc = jnp.zeros(shape, F32)
        dcs_acc = jnp.zeros(shape, F32)
        dcs_rows = jnp.zeros(shape, F32)
        dlast_acc = jnp.zeros((1, LANES), F32)
        dskip_acc = jnp.zeros((1, LANES), F32)
        for g in range(SSM_GROUPS):
            bsl = slice(D_SSM + n * g, D_SSM + n * (g + 1))
            csl = slice(D_SSM + n * (SSM_GROUPS + g), D_SSM + n * (SSM_GROUPS + g + 1))
            bg = x_ref[:, bsl]
            cg = x_ref[:, csl]
            cb = _dot_nt(cg, bg)
            dcb = jnp.zeros((ell, ell), F32)
            dbg = jnp.zeros((ell, n), F32)
            dcg = jnp.zeros((ell, n), F32)
            for r in range(rpg):
                h = g * rpg + r
                hs = slice(p * h, p * (h + 1))
                xs = x_ref[:, hs]
                dyh = dy_ref[:, hs]
                dt_h, e_h, ds_h, cd_h = dt[:, h:h + 1], e[:, h:h + 1], ds[:, h:h + 1], cd[:, h:h + 1]
                prev = ps_ref[0, :, hs]
                dsn = dstate[:, hs]
                xd = xs * dt_h
                dye = dyh * e_h
                cprev = _dot(cg, prev)
                dprev = dsn * cd_h + _dot_tn(cg, dye)
                dcg = dcg + _dot_nt(dye, prev)
                dcs_h = jnp.sum(dyh * cprev, axis=1, keepdims=True) * e_h
                dcd = jnp.sum(jnp.sum(dsn * prev, axis=1, keepdims=True), axis=0, keepdims=True)
                dlast_h = dcd * cd_h
                dxdd = _dot(bg, dsn)
                dbg = dbg + _dot_nt(xd * ds_h, dsn)
                dxd = dxdd * ds_h
                tmp = jnp.sum(dxdd * xd, axis=1, keepdims=True) * ds_h
                dlast_h = dlast_h + jnp.sum(tmp, axis=0, keepdims=True)
                dcs_h = dcs_h - tmp
                lmat = jnp.exp(jnp.where(tril, cs[:, h:h + 1] - cst[h:h + 1, :], -jnp.inf))
                gm = cb * lmat
                dgm = _dot_nt(dyh, xd)
                dxd = dxd + _dot_tn(gm, dyh)
                mm = dgm * gm
                dcs_h = dcs_h + jnp.sum(mm, axis=1, keepdims=True)
                sub = lax.broadcasted_iota(jnp.int32, shape, 0)
                dcs_rows = dcs_rows + jnp.where(sub == h, jnp.sum(mm, axis=0, keepdims=True), 0.0)
                dcb = dcb + dgm * lmat
                dx_ref[:, hs] = dxd * dt_h + dyh * dskip[:, h:h + 1]
                ddt_acc = ddt_acc + _lane_put(jnp.sum(dxd * xs, axis=1, keepdims=True), h, shape)
                dcs_acc = dcs_acc + _lane_put(dcs_h, h, shape)
                dlast_acc = dlast_acc + _lane_put(dlast_h, h, (1, LANES))
                dskip_acc = dskip_acc + _lane_put(
                    jnp.sum(jnp.sum(dyh * xs, axis=1, keepdims=True), axis=0, keepdims=True), h, (1, LANES))
                dstate[:, hs] = dprev
            dx_ref[:, bsl] = dbg + _dot_tn(dcb, cg)
            dx_ref[:, csl] = dcg + _dot(dcb, bg)
        rowi = lax.broadcasted_iota(jnp.int32, shape, 0)
        dcs = dcs_acc - dcs_rows.T + jnp.where(rowi == ell - 1, dlast_acc, 0.0)
        triu = lax.broadcasted_iota(jnp.int32, (ell, ell), 0) <= lax.broadcasted_iota(jnp.int32, (ell, ell), 1)
        dad = jnp.dot(triu.astype(F32), dcs, precision=HIGHEST, preferred_element_type=F32)
        ddt = ddt_acc + dad * a
        dalog_ref[...] += jnp.sum(dad * dt, axis=0, keepdims=True) * a
        ddskip_ref[...] += dskip_acc
        lane = lax.broadcasted_iota(jnp.int32, shape, 1)
        ddraw = jnp.where(lane < SSM_HEADS, ddt * _sigmoid(pre), 0.0)
        ddt_ref[...] = ddraw.astype(ddt_ref.dtype)
        dbias_ref[...] += jnp.sum(ddraw, axis=0, keepdims=True)

    vec = pl.BlockSpec((1, LANES), lambda c: (0, 0))
    rev = lambda c: nc - 1 - c
    outs = _hbm_call(
        body, grid=(nc,),
        in_specs=[pl.BlockSpec((ell, CONV_CH), lambda c: (rev(c), 0)),
                  pl.BlockSpec((ell, LANES), lambda c: (rev(c), P_DT // LANES)), vec, vec, vec,
                  pl.BlockSpec((1, n, D_SSM), lambda c: (rev(c), 0, 0)),
                  pl.BlockSpec((ell, D_SSM), lambda c: (rev(c), 0))],
        out_specs=(pl.BlockSpec((ell, CONV_CH), lambda c: (rev(c), 0)), pl.BlockSpec((ell, LANES), lambda c: (rev(c), 0)),
                   vec, vec, vec),
        out_shape=(SDS((s, CONV_CH), F32), SDS((s, LANES), BF16), SDS((1, LANES), F32), SDS((1, LANES), F32),
                   SDS((1, LANES), F32)),
        scratch_shapes=[pltpu.VMEM((n, D_SSM), F32)], name=name,
        compiler_params=_params(("arbitrary",), 32 << 20))(xbc, proj, dt_bias, a_log, d_skip, prev_states, dy)
    return outs


def _rope_swap(t):
    lane = lax.broadcasted_iota(jnp.int32, t.shape, 1)
    half = QK_ROPE // 2
    lo = (lane >= QK_NOPE) & (lane < QK_NOPE + half)
    hi = (lane >= QK_NOPE + half) & (lane < QK_NOPE + QK_ROPE)
    return jnp.where(lo, pltpu.roll(t, HEAD_PAD - half, axis=1), jnp.where(hi, pltpu.roll(t, half, axis=1), 0.0))


def _mla_prep(q, kv, proj, cos, sins, *, name):
    s = q.shape[0]
    tm = min(s, 256)
    scale = (QK_NOPE + QK_ROPE) ** -0.5

    def body(q_ref, kv_ref, kr_ref, cos_ref, sin_ref, qo_ref, ko_ref, vo_ref):
        cosv, sinv = cos_ref[...], sin_ref[...]
        kr = pltpu.roll(kr_ref[...], QK_NOPE, axis=1)
        lane = lax.broadcasted_iota(jnp.int32, kr.shape, 1)
        nope = lane < QK_NOPE
        kr = jnp.where(nope, 0.0, kr)
        kpe = kr * cosv + _rope_swap(kr) * sinv
        for hp in range(MLA_HEADS // 2):
            vs = []
            for h in (2 * hp, 2 * hp + 1):
                hs = slice(HEAD_PAD * h, HEAD_PAD * (h + 1))
                qh = q_ref[:, hs]
                kvh = kv_ref[:, hs]
                qo_ref[:, hs] = ((qh * cosv + _rope_swap(qh) * sinv) * scale).astype(qo_ref.dtype)
                ko_ref[:, hs] = (jnp.where(nope, kvh, 0.0) + kpe).astype(ko_ref.dtype)
                vs.append(kvh[:, QK_NOPE:])
            vo_ref[:, 2 * V_DIM * hp:2 * V_DIM * (hp + 1)] = jnp.concatenate(vs, axis=1).astype(vo_ref.dtype)

    wide = pl.BlockSpec((tm, MLA_HEADS * HEAD_PAD), lambda i: (i, 0))
    half = pl.BlockSpec((tm, MLA_HEADS * V_DIM), lambda i: (i, 0))
    tab = pl.BlockSpec((tm, LANES), lambda i: (i, 0))
    return _hbm_call(
        body, grid=(s // tm,),
        in_specs=[wide, wide, pl.BlockSpec((tm, LANES), lambda i: (i, P_KR // LANES)), tab, tab],
        out_specs=(wide, wide, half),
        out_shape=(SDS((s, MLA_HEADS * HEAD_PAD), BF16), SDS((s, MLA_HEADS * HEAD_PAD), BF16),
                   SDS((s, MLA_HEADS * V_DIM), BF16)), name=name,
        compiler_params=_params(("parallel",), 32 << 20))(q, kv, proj, cos, sins)


def _mla_prep_bwd(dqr, dkr, dv, cos, sins, *, name):
    s = dqr.shape[0]
    tm = min(s, 256)
    scale = (QK_NOPE + QK_ROPE) ** -0.5

    def body(dq_ref, dk_ref, dv_ref, cos_ref, sin_ref, dqo_ref, dkv_ref, dkr_ref):
        cosv, sinv = cos_ref[...], sin_ref[...]
        lane = lax.broadcasted_iota(jnp.int32, cosv.shape, 1)
        ksum = jnp.zeros(cosv.shape, F32)
        for h in range(MLA_HEADS):
            hs = slice(HEAD_PAD * h, HEAD_PAD * (h + 1))
            d = dq_ref[:, hs]
            dk = dk_ref[:, hs]
            dqo_ref[:, hs] = ((d * cosv + _rope_swap(d * sinv)) * scale).astype(dqo_ref.dtype)
            dkv_ref[:, hs] = jnp.concatenate([dk[:, :QK_NOPE], dv_ref[:, V_DIM * h:V_DIM * (h + 1)]], axis=1).astype(dkv_ref.dtype)
            ksum = ksum + dk
        ksum = jnp.where((lane >= QK_NOPE) & (lane < QK_NOPE + QK_ROPE), ksum, 0.0)
        un = ksum * cosv + _rope_swap(ksum * sinv)
        dkr_ref[...] = pltpu.roll(un, HEAD_PAD - QK_NOPE, axis=1).astype(dkr_ref.dtype)

    wide = pl.BlockSpec((tm, MLA_HEADS * HEAD_PAD), lambda i: (i, 0))
    half = pl.BlockSpec((tm, MLA_HEADS * V_DIM), lambda i: (i, 0))
    tab = pl.BlockSpec((tm, LANES), lambda i: (i, 0))
    return _hbm_call(
        body, grid=(s // tm,), in_specs=[wide, wide, half, tab, tab], out_specs=(wide, wide, tab),
        out_shape=(SDS((s, MLA_HEADS * HEAD_PAD), BF16), SDS((s, MLA_HEADS * HEAD_PAD), BF16), SDS((s, LANES), BF16)),
        name=name, compiler_params=_params(("parallel",), 40 << 20))(dqr, dkr, dv, cos, sins)


FLASH_TILE = 512


def _flash_fwd(q, k, v, *, name):
    s = q.shape[0]
    t = min(s, FLASH_TILE)
    nq = s // t
    npair = MLA_HEADS // 2

    def body(q_ref, k_ref, v_ref, o_ref, lse_ref):
        i = pl.program_id(1)
        qs = [q_ref[:, HEAD_PAD * e:HEAD_PAD * (e + 1)] for e in range(2)]
        diag = lax.broadcasted_iota(jnp.int32, (t, t), 0) >= lax.broadcasted_iota(jnp.int32, (t, t), 1)

        def step(j, carry, masked):
            rows = pl.ds(pl.multiple_of(j * t, t), t)
            new = []
            for e in range(2):
                m, l, acc = carry[e]
                sc = _dot_nt(qs[e], k_ref[rows, HEAD_PAD * e:HEAD_PAD * (e + 1)])
                if masked:
                    sc = jnp.where(diag, sc, NEG)
                m_new = jnp.maximum(m, jnp.max(sc, axis=1, keepdims=True))
                pr = jnp.exp(sc - m_new)
                alpha = jnp.exp(m - m_new)
                l = alpha * l + jnp.sum(pr, axis=1, keepdims=True)
                acc = alpha * acc + _dot(pr, v_ref[rows, V_DIM * e:V_DIM * (e + 1)])
                new.append((m_new, l, acc))
            return tuple(new)

        init = tuple((jnp.full((t, 1), NEG, F32), jnp.zeros((t, 1), F32), jnp.zeros((t, V_DIM), F32)) for _ in range(2))
        carry = lax.fori_loop(0, i, functools.partial(step, masked=False), init)
        carry = step(i, carry, True)
        o_ref[...] = jnp.concatenate([acc / l for _, l, acc in carry], axis=1)
        lse_ref[0] = jnp.concatenate([jnp.broadcast_to(m + jnp.log(l), (t, V_DIM)) for m, l, _ in carry], axis=1)

    return _hbm_call(
        body, grid=(npair, nq),
        in_specs=[pl.BlockSpec((t, 2 * HEAD_PAD), lambda hp, i: (i, hp)), pl.BlockSpec((s, 2 * HEAD_PAD), lambda hp, i: (0, hp)),
                  pl.BlockSpec((s, 2 * V_DIM), lambda hp, i: (0, hp))],
        out_specs=(pl.BlockSpec((t, 2 * V_DIM), lambda hp, i: (i, hp)), pl.BlockSpec((1, t, LANES), lambda hp, i: (hp, i, 0))),
        out_shape=(SDS((s, MLA_HEADS * V_DIM), F32), SDS((npair, s, LANES), F32)), name=name,
        compiler_params=_params(("parallel", "parallel"), 40 << 20))(q, k, v)


def _flash_bwd(q, k, v, o, lse, do, *, name):
    s = q.shape[0]
    t = min(s, FLASH_TILE)
    nq = s // t
    npair = MLA_HEADS // 2

    def body(q_ref, k_ref, v_ref, o_ref, lse_ref, do_ref, dq_ref, dk_ref, dv_ref):
        j = pl.program_id(1)

        @pl.when(j == 0)
        def _():
            dq_ref[...] = jnp.zeros_like(dq_ref)

        qsl = [slice(HEAD_PAD * e, HEAD_PAD * (e + 1)) for e in range(2)]
        vsl = [slice(V_DIM * e, V_DIM * (e + 1)) for e in range(2)]
        ks = [k_ref[:, qsl[e]] for e in range(2)]
        vs = [v_ref[:, vsl[e]] for e in range(2)]
        diag = lax.broadcasted_iota(jnp.int32, (t, t), 0) >= lax.broadcasted_iota(jnp.int32, (t, t), 1)

        def step(i, carry, masked):
            rows = pl.ds(pl.multiple_of(i * t, t), t)
            new = []
            for e in range(2):
                dk, dv = carry[e]
                qi = q_ref[rows, qsl[e]]
                doi = do_ref[rows, vsl[e]]
                delta = jnp.sum(doi * o_ref[rows, vsl[e]], axis=1, keepdims=True)
                lse_i = lse_ref[0, rows, vsl[e]][:, 0:1]
                sc = _dot_nt(qi, ks[e])
                if masked:
                    sc = jnp.where(diag, sc, NEG)
                pr = jnp.exp(sc - lse_i)
                dv = dv + _dot_tn(pr, doi)
                dsc = (pr * (_dot_nt(doi, vs[e]) - delta)).astype(BF16)
                dk = dk + _dot_tn(dsc, qi)
                dq_ref[rows, qsl[e]] += _dot(dsc, ks[e])
                new.append((dk, dv))
            return tuple(new)

        init = tuple((jnp.zeros((t, HEAD_PAD), F32), jnp.zeros((t, V_DIM), F32)) for _ in range(2))
        carry = step(j, init, True)
        carry = lax.fori_loop(j + 1, nq, functools.partial(step, masked=False), carry)
        dk_ref[...] = jnp.concatenate([dk for dk, _ in carry], axis=1)
        dv_ref[...] = jnp.concatenate([dv for _, dv in carry], axis=1)

    full_q = pl.BlockSpec((s, 2 * HEAD_PAD), lambda hp, j: (0, hp))
    full_v = pl.BlockSpec((s, 2 * V_DIM), lambda hp, j: (0, hp))
    blk_k = pl.BlockSpec((t, 2 * HEAD_PAD), lambda hp, j: (j, hp))
    blk_v = pl.BlockSpec((t, 2 * V_DIM), lambda hp, j: (j, hp))
    return _hbm_call(
        body, grid=(npair, nq),
        in_specs=[full_q, blk_k, blk_v, full_v, pl.BlockSpec((1, s, LANES), lambda hp, j: (hp, 0, 0)), full_v],
        out_specs=(full_q, blk_k, blk_v),
        out_shape=(SDS((s, MLA_HEADS * HEAD_PAD), F32), SDS((s, MLA_HEADS * HEAD_PAD), F32), SDS((s, MLA_HEADS * V_DIM), F32)),
        name=name, compiler_params=_params(("parallel", "arbitrary"), 48 << 20))(q, k, v, o, lse, do)


def _mem_attn_fwd(q, k, v, *, name):
    s = q.shape[0]
    tm = min(s, 512)
    ml = k.shape[0]
    scale = MEM_HEAD_DIM ** -0.5

    def body(q_ref, k_ref, v_ref, o_ref):
        for h in range(MEM_HEADS):
            hs = slice(MEM_HEAD_DIM * h, MEM_HEAD_DIM * (h + 1))
            sc = _dot_nt(q_ref[:, hs], k_ref[:, hs]) * scale
            pr = jnp.exp(sc - jnp.max(sc, axis=1, keepdims=True))
            pr = pr / jnp.sum(pr, axis=1, keepdims=True)
            o_ref[:, hs] = _dot(pr, v_ref[:, hs]).astype(o_ref.dtype)

    blk = pl.BlockSpec((tm, D_MODEL), lambda i: (i, 0))
    kv = pl.BlockSpec((ml, D_MODEL), lambda i: (0, 0))
    return _hbm_call(body, grid=(s // tm,), in_specs=[blk, kv, kv], out_specs=blk,
                          out_shape=SDS((s, D_MODEL), BF16), name=name,
                          compiler_params=_params(("parallel",), 24 << 20))(q, k, v)


def _mem_attn_bwd(q, k, v, do, *, name):
    s = q.shape[0]
    tm = min(s, 512)
    ml = k.shape[0]
    scale = MEM_HEAD_DIM ** -0.5

    def body(q_ref, k_ref, v_ref, do_ref, dq_ref, dk_ref, dv_ref):
        @pl.when(pl.program_id(0) == 0)
        def _():
            dk_ref[...] = jnp.zeros_like(dk_ref)
            dv_ref[...] = jnp.zeros_like(dv_ref)

        for h in range(MEM_HEADS):
            hs = slice(MEM_HEAD_DIM * h, MEM_HEAD_DIM * (h + 1))
            qh, kh, vh, doh = q_ref[:, hs], k_ref[:, hs], v_ref[:, hs], do_ref[:, hs]
            sc = _dot_nt(qh, kh) * scale
            pr = jnp.exp(sc - jnp.max(sc, axis=1, keepdims=True))
            pr = pr / jnp.sum(pr, axis=1, keepdims=True)
            dp = _dot_nt(doh, vh)
            dsc = pr * (dp - jnp.sum(pr * dp, axis=1, keepdims=True)) * scale
            dq_ref[:, hs] = _dot(dsc, kh).astype(dq_ref.dtype)
            dk_ref[:, hs] += _dot_tn(dsc, qh)
            dv_ref[:, hs] += _dot_tn(pr, doh)

    blk = pl.BlockSpec((tm, D_MODEL), lambda i: (i, 0))
    kv = pl.BlockSpec((ml, D_MODEL), lambda i: (0, 0))
    return _hbm_call(body, grid=(s // tm,), in_specs=[blk, kv, kv, blk], out_specs=(blk, kv, kv),
                          out_shape=(SDS((s, D_MODEL), BF16), SDS((ml, D_MODEL), F32), SDS((ml, D_MODEL), F32)), name=name,
                          compiler_params=_params(("arbitrary",), 32 << 20))(q, k, v, do)


MATS = (("w_in", (1024, 940), 1), ("w_uq", (384, 384), 1), ("w_ukv", (256, 512), 1), ("w_out", (512, 1024), 0),
        ("ssm_conv_w", (4, 512), 1),
        ("w_mq", (256, 1024), 0), ("w_mk", (256, 1024), 0), ("w_mv", (256, 1024), 0), ("w_mo", (256, 1024), 0),
        ("w_up", (1024, 1408), 1), ("w_down", (704, 1024), 0), ("ffn_conv_w", (3, 1408), 1))
GROUPS = {"mixer": (0, 1, 2, 3, 4), "mem": (5, 6, 7, 8), "ffn": (9, 10, 11)}
UP_SHARD_COLS = 1408
F32_ON_WIRE = ("ssm_conv_w", "ffn_conv_w")
SMALL = (("norm_mix", 1024), ("ssm_conv_b", 2048), ("dt_bias", 16), ("a_log", 16), ("d_skip", 16), ("ssm_norm", 1024),
         ("q_norm", 384), ("kv_norm", 256), ("attn_out_norm", 1024), ("norm_mem_q", 1024), ("norm_mem_kv", 1024),
         ("norm_ffn", 1024), ("ffn_conv_b", 5632))
PACK_COLS = 1024


def _pad_cols(t, n):
    return jnp.pad(t, ((0, 0),) * (t.ndim - 1) + ((0, n - t.shape[-1]),))


def _w_in_to_padded(t):
    z, xbc, dt, cq, ckv, kr = jnp.split(t, (1024, 3072, 3088, 3472, 3728), axis=-1)
    return jnp.concatenate([xbc, z, cq, _pad_cols(dt, LANES), ckv, _pad_cols(kr, P_IN - P_KR)], axis=-1)


def _w_in_from_padded(t):
    return jnp.concatenate([t[..., P_Z:P_Z + 1024], t[..., P_XBC:P_XBC + 2048], t[..., P_DT:P_DT + SSM_HEADS],
                            t[..., P_CQ:P_CQ + Q_LORA], t[..., P_CKV:P_CKV + KV_LORA], t[..., P_KR:P_KR + QK_ROPE]], axis=-1)


def _cols_joined(g):
    return jnp.concatenate([g[j] for j in range(N_CHIPS)], axis=-1)


def _cols_by_chip(t, dtype):
    k = t.shape[0]
    return t.reshape(k, N_CHIPS, -1).transpose(1, 0, 2).astype(dtype)


def _rows_by_chip(t):
    return t.reshape(N_CHIPS, -1, t.shape[-1])


def _mixer_weights(gw):
    wl = {}
    wl["w_in"] = _w_in_to_padded(_cols_joined(gw["w_in"]))
    uq = _cols_joined(gw["w_uq"]).reshape(Q_LORA, MLA_HEADS, QK_NOPE + QK_ROPE)
    wl["w_uq"] = _pad_cols(uq, HEAD_PAD).reshape(Q_LORA, MLA_HEADS * HEAD_PAD)
    wl["w_ukv"] = _cols_joined(gw["w_ukv"])
    wl["ssm_conv_w"] = _cols_joined(gw["ssm_conv_w"])
    return wl


def _layer_fwd(x0, mem, cos, sins, weights, sp, li):
    n = lambda t: f"l{li}_{t}"
    lead = ()
    sv = {"x0": x0}
    gw = dict(weights("mixer", x0))
    wl = _mixer_weights(gw)
    h = _rms_fwd(x0, sp["norm_mix"], name=n("mix_norm"))
    proj = _mm(h, wl["w_in"], name=n("mix_proj"))
    xbc = _ssm_conv_fwd(proj, wl["ssm_conv_w"], sp["ssm_conv_b"], name=n("ssm_conv"))
    y, pstates = _ssd_fwd(xbc, proj, sp["dt_bias"], sp["a_log"], sp["d_skip"], name=n("ssd"))
    y_ssm = _gated_rms_fwd(y, proj, sp["ssm_norm"], name=n("ssm_gate"))
    cqn = _rms_fwd(proj, sp["q_norm"], col=(Q_LORA, P_CQ // Q_LORA), name=n("q_norm"))
    ckvn = _rms_fwd(proj, sp["kv_norm"], col=(KV_LORA, P_CKV // KV_LORA), name=n("kv_norm"))
    q = _mm(cqn, wl["w_uq"], name=n("uq"))
    kv = _mm(ckvn, wl["w_ukv"], name=n("ukv"))
    qr, kr, v = _mla_prep(q, kv, proj, cos, sins, name=n("rope"))
    att, lse = _flash_fwd(qr, kr, v, name=n("flash"))
    y_att = _rms_fwd(att, sp["attn_out_norm"], name=n("att_norm"))
    x1 = _mm(y_ssm, gw["w_out"], b_lead=lead, b_rows=(0, D_SSM), res=x0, name=n("out_a"))
    x1 = _mm(y_att, gw["w_out"], b_lead=lead, b_rows=(D_SSM, D_SSM), res=x1, name=n("out_b"))
    sv.update(h=h, proj=proj, xbc=xbc, y=y, pstates=pstates, y_ssm=y_ssm, cqn=cqn, ckvn=ckvn, qr=qr, kr=kr, v=v,
              att=att, lse=lse, y_att=y_att, x1=x1)
    gw.update(weights("mem", x1))
    hq = _rms_fwd(x1, sp["norm_mem_q"], name=n("memq_norm"))
    hm = _rms_fwd(mem, sp["norm_mem_kv"], name=n("memkv_norm"))
    mq = _mm(hq, gw["w_mq"], b_lead=lead, out_dtype=BF16, name=n("mq"))
    mk = _mm(hm, gw["w_mk"], b_lead=lead, out_dtype=BF16, name=n("mk"))
    mv = _mm(hm, gw["w_mv"], b_lead=lead, out_dtype=BF16, name=n("mv"))
    mo = _mem_attn_fwd(mq, mk, mv, name=n("mem_attn"))
    x2 = _mm(mo, gw["w_mo"], b_lead=lead, res=x1, name=n("mo"))
    sv.update(hq=hq, hm=hm, mq=mq, mk=mk, mv=mv, mo=mo, x2=x2)
    gw.update(weights("ffn", x2))
    wl["ffn_conv_w"] = _cols_joined(gw["ffn_conv_w"])
    hf = _rms_fwd(x2, sp["norm_ffn"], name=n("ffn_norm"))
    up_g = _mm(hf, gw["w_up"], b_lead=lead, b_chips=(0, 2), name=n("up_g"))
    up_v = _mm(hf, gw["w_up"], b_lead=lead, b_chips=(2, 2), name=n("up_v"))
    act = _ffn_conv_fwd(up_g, up_v, wl["ffn_conv_w"], sp["ffn_conv_b"], name=n("ffn_conv"))
    x3 = _mm(act, gw["w_down"], b_lead=lead, res=x2, name=n("down"))
    sv.update(hf=hf, up_g=up_g, up_v=up_v, act=act)
    return x3, sv, gw, wl


def _layer_bwd(dx3, mem, cos, sins, gw, wl, sp, sv, li, emit):
    n = lambda t: f"l{li}_b_{t}"
    lead = ()
    g = {}

    def after(token, v):
        return v if token is None else v + token[0, 0]

    dact = _mm(dx3, gw["w_down"], tb=True, b_lead=lead, out_dtype=BF16, name=n("down_dx"))
    g["w_down"] = _rows_by_chip(_mm(sv["act"], dx3, ta=True, out_dtype=BF16, name=n("down_dw")))
    dup_g, dup_v, dcw, g["ffn_conv_b"] = _ffn_conv_bwd(
        sv["up_g"], sv["up_v"], wl["ffn_conv_w"], sp["ffn_conv_b"], dact, name=n("ffn_conv"))
    g["ffn_conv_w"] = _cols_by_chip(dcw, F32)
    nsh = UP_SHARD_COLS
    dhf = None
    for c4 in range(N_CHIPS):
        dhf = _mm(dup_g if c4 < 2 else dup_v, gw["w_up"], tb=True, a_col=(nsh, c4 % 2), b_lead=(c4,), res=dhf,
                  name=n(f"up{c4}_dx"))
    g["w_up"] = jnp.concatenate([_mm(sv["hf"], dup_g, ta=True, o_chips=nsh, out_dtype=BF16, name=n("upg_dw")),
                                 _mm(sv["hf"], dup_v, ta=True, o_chips=nsh, out_dtype=BF16, name=n("upv_dw"))], axis=0)
    dx2, g["norm_ffn"] = _rms_bwd(sv["x2"], after(emit("ffn", g), sp["norm_ffn"]), dhf, dx3, name=n("ffn_norm"))
    dmo = _mm(dx2, gw["w_mo"], tb=True, b_lead=lead, out_dtype=BF16, name=n("mo_dx"))
    g["w_mo"] = _rows_by_chip(_mm(sv["mo"], dx2, ta=True, out_dtype=BF16, name=n("mo_dw")))
    dmq, dmk, dmv = _mem_attn_bwd(sv["mq"], sv["mk"], sv["mv"], dmo, name=n("mem_attn"))
    dhq = _mm(dmq, gw["w_mq"], tb=True, b_lead=lead, name=n("mq_dx"))
    g["w_mq"] = _rows_by_chip(_mm(sv["hq"], dmq, ta=True, out_dtype=BF16, name=n("mq_dw")))
    dhm = _mm(dmk, gw["w_mk"], tb=True, b_lead=lead, name=n("mk_dx"))
    dhm = _mm(dmv, gw["w_mv"], tb=True, b_lead=lead, res=dhm, name=n("mv_dx"))
    g["w_mk"] = _rows_by_chip(_mm(sv["hm"], dmk, ta=True, out_dtype=BF16, name=n("mk_dw")))
    g["w_mv"] = _rows_by_chip(_mm(sv["hm"], dmv, ta=True, out_dtype=BF16, name=n("mv_dw")))
    dx1, g["norm_mem_q"] = _rms_bwd(sv["x1"], after(emit("mem", g), sp["norm_mem_q"]), dhq, dx2, name=n("memq_norm"))
    _, g["norm_mem_kv"] = _rms_bwd(mem, sp["norm_mem_kv"], dhm, name=n("memkv_norm"))
    dy_ssm = _mm(dx1, gw["w_out"], tb=True, b_lead=lead, b_rows=(0, D_SSM), name=n("outa_dx"))
    dy_att = _mm(dx1, gw["w_out"], tb=True, b_lead=lead, b_rows=(D_SSM, D_SSM), name=n("outb_dx"))
    g["w_out"] = _rows_by_chip(jnp.concatenate([_mm(sv["y_ssm"], dx1, ta=True, out_dtype=BF16, name=n("outa_dw")),
                                                _mm(sv["y_att"], dx1, ta=True, out_dtype=BF16, name=n("outb_dw"))], axis=0))
    datt, g["attn_out_norm"] = _rms_bwd(sv["att"], sp["attn_out_norm"], dy_att, name=n("att_norm"))
    dqr, dkr, dv = _flash_bwd(sv["qr"], sv["kr"], sv["v"], sv["att"], sv["lse"], datt, name=n("flash"))
    dq, dkv, dkrope = _mla_prep_bwd(dqr, dkr, dv, cos, sins, name=n("rope"))
    duq = _mm(sv["cqn"], dq, ta=True, name=n("uq_dw")).reshape(Q_LORA, MLA_HEADS, HEAD_PAD)[..., :QK_NOPE + QK_ROPE]
    g["w_uq"] = _cols_by_chip(duq.reshape(Q_LORA, -1), BF16)
    dcqn = _mm(dq, wl["w_uq"], tb=True, name=n("uq_dx"))
    g["w_ukv"] = _cols_by_chip(_mm(sv["ckvn"], dkv, ta=True, name=n("ukv_dw")), BF16)
    dckvn = _mm(dkv, wl["w_ukv"], tb=True, name=n("ukv_dx"))
    proj = sv["proj"]
    dcq, g["q_norm"] = _rms_bwd(proj, sp["q_norm"], dcqn, col=(Q_LORA, P_CQ // Q_LORA), name=n("q_norm"))
    dckv, g["kv_norm"] = _rms_bwd(proj, sp["kv_norm"], dckvn, col=(KV_LORA, P_CKV // KV_LORA), name=n("kv_norm"))
    dy, dz, g["ssm_norm"] = _gated_rms_bwd(sv["y"], proj, sp["ssm_norm"], dy_ssm, name=n("ssm_gate"))
    dxbc, ddt, da_log, dd_skip, ddt_bias = _ssd_bwd(
        sv["xbc"], proj, sp["dt_bias"], sp["a_log"], sp["d_skip"], sv["pstates"], dy, name=n("ssd"))
    g["a_log"], g["d_skip"], g["dt_bias"] = da_log[0, :SSM_HEADS], dd_skip[0, :SSM_HEADS], ddt_bias[0, :SSM_HEADS]
    dxbc_pre, dsw, g["ssm_conv_b"] = _ssm_conv_bwd(proj, wl["ssm_conv_w"], sp["ssm_conv_b"], dxbc, name=n("ssm_conv"))
    g["ssm_conv_w"] = _cols_by_chip(dsw, F32)
    s = proj.shape[0]
    dproj = jnp.concatenate([dxbc_pre, dz, dcq.astype(BF16), ddt, dckv.astype(BF16), dkrope,
                             jnp.zeros((s, P_IN - P_KR - LANES), BF16)], axis=1)
    dh = _mm(dproj, wl["w_in"], tb=True, name=n("proj_dx"))
    g["w_in"] = _cols_by_chip(_w_in_from_padded(_mm(sv["h"], dproj, ta=True, name=n("proj_dw"))), BF16)
    dx0, g["norm_mix"] = _rms_bwd(sv["x0"], sp["norm_mix"], dh, dx1, name=n("mix_norm"))
    return dx0, g, emit("mixer", g)


def _chip_peers(x, y):
    return [(1 - x, y), (x, 1 - y), (1 - x, 1 - y)]


HBM_SPEC = pl.BlockSpec(memory_space=pltpu.HBM)
SEM_SPEC = pl.BlockSpec(memory_space=pltpu.SEMAPHORE)
ANY_SPEC = pl.BlockSpec(memory_space=pl.ANY)
VMEM_SPEC = pl.BlockSpec(memory_space=pltpu.VMEM)
DATAFLOW = pltpu.SideEffectType.DATAFLOW_SIDE_EFFECTING
TOKEN_SHAPE = (8, LANES)


def _exchange_start(srcs, land_shapes, src_view, dst_view, token, *, name):
    n = len(srcs)

    def body(*refs):
        s, l, tok_in = refs[:n], refs[n:2 * n], refs[2 * n]
        send_sems, recv_sems = refs[2 * n + 1], refs[2 * n + 2]
        tok_out = refs[-1]
        x, y, c = lax.axis_index("x"), lax.axis_index("y"), lax.axis_index("c")
        me = 2 * x + y
        for t in range(n):
            for k, (px, py) in enumerate(_chip_peers(x, y)):
                pltpu.make_async_remote_copy(
                    src_ref=src_view(t, s[t], 2 * px + py), dst_ref=dst_view(t, l[t], me), send_sem=send_sems.at[3 * t + k],
                    recv_sem=recv_sems.at[3 * t + k], device_id=(px, py, c), device_id_type=MESH).start()
            pltpu.make_async_copy(src_view(t, s[t], me), dst_view(t, l[t], me), send_sems.at[3 * n + t]).start()
        tok_out[...] = tok_in[...]

    hbm = lambda t: pltpu.with_memory_space_constraint(t, pltpu.HBM)
    lands = [lax.empty(l.shape, l.dtype) for l in land_shapes]
    outs = pl.pallas_call(
        body, name=name,
        out_shape=(pltpu.SemaphoreType.DMA((4 * n,)), pltpu.SemaphoreType.DMA((3 * n,)),
                   *[pltpu.HBM(l.shape, l.dtype) for l in land_shapes], SDS(TOKEN_SHAPE, F32)),
        in_specs=[HBM_SPEC] * (2 * n) + [VMEM_SPEC], out_specs=(SEM_SPEC, SEM_SPEC, *[HBM_SPEC] * n, VMEM_SPEC),
        input_output_aliases={n + t: 2 + t for t in range(n)},
        compiler_params=pltpu.CompilerParams(has_side_effects=DATAFLOW))(*[hbm(t) for t in srcs], *[hbm(t) for t in lands], token)
    return outs[0], outs[1], list(outs[2:2 + n]), outs[-1]


def _exchange_wait(srcs, lands, send_sems, recv_sems, after, src_view, dst_view, which, *, name):
    n = len(srcs)
    m = len(which)

    def body(*refs):
        s, l = refs[:m], refs[m:2 * m]
        send_ref, recv_ref = refs[2 * m], refs[2 * m + 1]
        x, y, c = lax.axis_index("x"), lax.axis_index("y"), lax.axis_index("c")
        me = 2 * x + y
        for i, t in enumerate(which):
            for k, (px, py) in enumerate(_chip_peers(x, y)):
                chip = 2 * px + py
                cp = pltpu.make_async_remote_copy(
                    src_ref=src_view(t, s[i], chip), dst_ref=dst_view(t, l[i], chip), send_sem=send_ref.at[3 * t + k],
                    recv_sem=recv_ref.at[3 * t + k], device_id=(px, py, c), device_id_type=MESH)
                cp.wait_send()
                cp.wait_recv()
            pltpu.make_async_copy(src_view(t, s[i], me), dst_view(t, l[i], me), send_ref.at[3 * n + t]).wait()

    outs = pl.pallas_call(
        body, name=name, out_shape=[pltpu.HBM(lands[t].shape, lands[t].dtype) for t in which],
        in_specs=[HBM_SPEC] * (2 * m) + [SEM_SPEC, SEM_SPEC, ANY_SPEC], out_specs=[HBM_SPEC] * m,
        input_output_aliases={m + i: i for i in range(m)},
        compiler_params=pltpu.CompilerParams(has_side_effects=DATAFLOW))(
            *[srcs[t] for t in which], *[lands[t] for t in which], send_sems, recv_sems, after)
    return list(outs)


def _gather_layer_start(shards, li, token):
    src_view = lambda t, ref, chip: ref.at[li]
    dst_view = lambda t, ref, chip: ref.at[chip]
    send_sems, recv_sems, lands, token = _exchange_start(
        shards, [SDS((N_CHIPS,) + s.shape[1:], s.dtype) for s in shards], src_view, dst_view, token, name=f"gather{li}_start")
    return (shards, lands, send_sems, recv_sems, src_view, dst_view, f"gather{li}"), token


def _scatter_start(grads, tag, token):
    view = lambda t, ref, chip: ref.at[chip]
    send_sems, recv_sems, lands, token = _exchange_start(
        grads, [SDS(g.shape, g.dtype) for g in grads], view, view, token, name=f"scatter{tag}_start")
    return (grads, lands, send_sems, recv_sems, view, view, f"scatter{tag}"), token


def _exchange_finish(handle, after, which=None, tag=""):
    srcs, lands, send_sems, recv_sems, src_view, dst_view, name = handle
    which = tuple(range(len(srcs))) if which is None else which
    return _exchange_wait(srcs, lands, send_sems, recv_sems, after, src_view, dst_view, which, name=f"{name}{tag}_wait")


def _swap_cores(bufs, *, name):
    n = len(bufs)

    def body(*refs):
        srcs, outs = refs[:n], refs[n:2 * n]
        send_sems, recv_sems = refs[2 * n:]
        x, y, c = lax.axis_index("x"), lax.axis_index("y"), lax.axis_index("c")
        cps = [pltpu.make_async_remote_copy(src_ref=srcs[t], dst_ref=outs[t], send_sem=send_sems.at[t], recv_sem=recv_sems.at[t],
                                            device_id=(x, y, 1 - c), device_id_type=MESH) for t in range(n)]
        for cp in cps:
            cp.start()
        for cp in cps:
            cp.wait()

    any_spec = pl.BlockSpec(memory_space=pl.ANY)
    return pl.pallas_call(body, in_specs=[any_spec] * n, out_specs=[any_spec] * n,
                          out_shape=[SDS(b.shape, b.dtype) for b in bufs],
                          scratch_shapes=[pltpu.SemaphoreType.DMA((n,)), pltpu.SemaphoreType.DMA((n,))], name=name)(*bufs)


def _all_gather8(src, *, name):
    def body(src_ref, out_ref, send_sems, recv_sems, local_sem):
        x, y, c = lax.axis_index("x"), lax.axis_index("y"), lax.axis_index("c")
        me = 4 * x + 2 * y + c
        mine = pltpu.make_async_copy(src_ref, out_ref.at[me], local_sem)
        mine.start()

        def peer(k):
            return (x ^ (k >> 2 & 1), y ^ (k >> 1 & 1), c ^ (k & 1))

        sends = []
        for k in range(1, N_DEV):
            cp = pltpu.make_async_remote_copy(src_ref=src_ref, dst_ref=out_ref.at[me], send_sem=send_sems.at[k - 1],
                                              recv_sem=recv_sems.at[k - 1], device_id=peer(k), device_id_type=MESH)
            cp.start()
            sends.append(cp)
        for k in range(1, N_DEV):
            px, py, pc = peer(k)
            pltpu.make_async_remote_copy(src_ref=src_ref, dst_ref=out_ref.at[4 * px + 2 * py + pc],
                                         send_sem=send_sems.at[k - 1], recv_sem=recv_sems.at[k - 1],
                                         device_id=peer(k), device_id_type=MESH).wait_recv()
        for cp in sends:
            cp.wait_send()
        mine.wait()

    any_spec = pl.BlockSpec(memory_space=pl.ANY)
    return pl.pallas_call(
        body, in_specs=[any_spec], out_specs=any_spec, out_shape=SDS((N_DEV,) + src.shape, src.dtype),
        scratch_shapes=[pltpu.SemaphoreType.DMA((N_DEV - 1,)), pltpu.SemaphoreType.DMA((N_DEV - 1,)), pltpu.SemaphoreType.DMA],
        name=name)(src)


def _adam_terms(w, g, m, v):
    m = ADAM_B1 * m + (1.0 - ADAM_B1) * g
    v = ADAM_B2 * v + (1.0 - ADAM_B2) * (g * g)
    m_hat = m / (1.0 - ADAM_B1 ** ADAM_STEP)
    v_hat = v / (1.0 - ADAM_B2 ** ADAM_STEP)
    delta = -ADAM_LR * (m_hat / (jnp.sqrt(v_hat) + ADAM_EPS) + ADAM_WD * w)
    return delta, m, v


def _adamw_shard(mine, other, w, m, v, *, name):
    d, a, b = w.shape
    tr = next((t for t in (128, 64, 32, 16) if a % t == 0), a)

    def body(*refs):
        ga, gb = refs[:d], refs[d:2 * d]
        w_ref, m_ref, v_ref, g_ref, d_ref, nm_ref, nv_ref = refs[2 * d:]

        def plane(ref):
            return ((ref[0].astype(F32) + ref[1].astype(F32)) + ref[2].astype(F32)) + ref[3].astype(F32)

        for lp in range(d):
            @pl.when(pl.program_id(0) == lp)
            def _(lp=lp):
                g = plane(ga[lp]) + plane(gb[lp])
                delta, mn, vn = _adam_terms(w_ref[...], g, m_ref[...], v_ref[...])
                g_ref[...] = g
                d_ref[...] = delta
                nm_ref[...] = mn
                nv_ref[...] = vn

    gspecs = [pl.BlockSpec((N_CHIPS, tr, b), lambda l, i, lp=lp: (0, jnp.where(l == lp, i, 0), 0)) for lp in range(d)]
    blk = pl.BlockSpec((None, tr, b), lambda l, i: (l, i, 0))
    shp = SDS((d, a, b), F32)
    return _hbm_call(
        body, grid=(d, a // tr), in_specs=gspecs + gspecs + [blk, blk, blk], out_specs=(blk,) * 4, out_shape=(shp,) * 4,
        name=name, compiler_params=_params(("arbitrary", "arbitrary"), 48 << 20))(*mine, *other, w, m, v)


def _adamw_small(g8, w, m, v, *, name):
    n = w.shape[1]

    def body(g8_ref, w_ref, m_ref, v_ref, g_ref, d_ref, nm_ref, nv_ref):
        g = g8_ref[0]
        for k in range(1, N_DEV):
            g = g + g8_ref[k]
        delta, mn, vn = _adam_terms(w_ref[...], g, m_ref[...], v_ref[...])
        g_ref[...] = g
        d_ref[...] = delta
        nm_ref[...] = mn
        nv_ref[...] = vn

    shp = SDS((1, n), F32)
    return _hbm_call(body, out_shape=(shp,) * 4, name=name, compiler_params=_params(None, 24 << 20))(g8, w, m, v)


def _rope_tables(positions):
    inv_freq = 1.0 / (ROPE_THETA ** (jnp.arange(0, QK_ROPE, 2, dtype=F32) / QK_ROPE))
    ang = positions.astype(F32)[:, None] * inv_freq
    c, s = jnp.cos(ang), jnp.sin(ang)
    n = positions.shape[0]
    pad = jnp.zeros((n, HEAD_PAD - QK_NOPE - QK_ROPE), F32)
    cos = jnp.concatenate([jnp.ones((n, QK_NOPE), F32), c, c, pad], axis=1)
    sins = jnp.concatenate([jnp.zeros((n, QK_NOPE), F32), -s, s, pad], axis=1)
    return cos, sins


def _pad_lanes(v):
    return _pad_cols(v.reshape(1, -1), LANES)


def _local_step(x, mem, positions, weights, small, final_norm, loss_target, emit, token):
    cos, sins = _rope_tables(positions)
    saved, gws, wls, sps = [], [], [], []
    h = x
    for li in range(DEPTH):
        sp = {k: small[k][li] for k, _ in SMALL}
        if li == 0:
            sp["norm_mix"] = sp["norm_mix"] + token[0, 0]
        for k in ("dt_bias", "a_log", "d_skip"):
            sp[k] = _pad_lanes(sp[k])
        h, sv, gw, wl = _layer_fwd(h, mem, cos, sins, functools.partial(weights, li), sp, li)
        saved.append(sv)
        gws.append(gw)
        wls.append(wl)
        sps.append(sp)
    loss, dh, g_final = _final_loss(h, final_norm, loss_target, name="final_loss")
    grads = [None] * DEPTH
    started = None
    for li in reversed(range(DEPTH)):
        sp = sps[li]
        if started is not None:
            sp = dict(sp, ffn_conv_b=sp["ffn_conv_b"] + started[0, 0])
        dh, grads[li], started = _layer_bwd(dh, mem, cos, sins, gws[li], wls[li], sp, saved[li], li,
                                            functools.partial(emit, li))
    return loss, dh, grads, g_final


def _gathered_views(which, lands):
    return {MATS[t][0]: (b.reshape(-1, b.shape[-1]) if MATS[t][2] == 0 else b) for t, b in zip(which, lands)}


def kernel(x, mem, positions, norm_mix, w_in, ssm_conv_w, ssm_conv_b, dt_bias, a_log, d_skip, ssm_norm, q_norm, w_uq, kv_norm, w_ukv, attn_out_norm, w_out, norm_mem_q, norm_mem_kv, w_mq, w_mk, w_mv, w_mo, norm_ffn, w_up, ffn_conv_w, ffn_conv_b, w_down, final_norm, loss_target, m_norm_mix, m_w_in, m_ssm_conv_w, m_ssm_conv_b, m_dt_bias, m_a_log, m_d_skip, m_ssm_norm, m_q_norm, m_w_uq, m_kv_norm, m_w_ukv, m_attn_out_norm, m_w_out, m_norm_mem_q, m_norm_mem_kv, m_w_mq, m_w_mk, m_w_mv, m_w_mo, m_norm_ffn, m_w_up, m_ffn_conv_w, m_ffn_conv_b, m_w_down, m_final_norm, v_norm_mix, v_w_in, v_ssm_conv_w, v_ssm_conv_b, v_dt_bias, v_a_log, v_d_skip, v_ssm_norm, v_q_norm, v_w_uq, v_kv_norm, v_w_ukv, v_attn_out_norm, v_w_out, v_norm_mem_q, v_norm_mem_kv, v_w_mq, v_w_mk, v_w_mv, v_w_mo, v_norm_ffn, v_w_up, v_ffn_conv_w, v_ffn_conv_b, v_w_down, v_final_norm):
    args = dict(locals())
    names = ["norm_mix", "w_in", "ssm_conv_w", "ssm_conv_b", "dt_bias", "a_log", "d_skip", "ssm_norm", "q_norm", "w_uq",
             "kv_norm", "w_ukv", "attn_out_norm", "w_out", "norm_mem_q", "norm_mem_kv", "w_mq", "w_mk", "w_mv", "w_mo",
             "norm_ffn", "w_up", "ffn_conv_w", "ffn_conv_b", "w_down", "final_norm"]
    wts = {k: args[k] for k in names}
    mom = {k: args["m_" + k] for k in names}
    var = {k: args["v_" + k] for k in names}
    mat_names = [k for k, _, _ in MATS]

    shards = [wts[k] if k in F32_ON_WIRE else wts[k].astype(BF16) for k in mat_names]
    token = jnp.zeros(TOKEN_SHAPE, F32)
    gathers = []
    for li in range(DEPTH):
        handle, token = _gather_layer_start(shards, li, token)
        gathers.append(handle)
    small = {k: wts[k] for k, _ in SMALL}

    def weights(li, group, after):
        which = GROUPS[group]
        return _gathered_views(which, _exchange_finish(gathers[li], after, which, tag=f"_{group}"))

    scatters = [[] for _ in range(DEPTH)]

    def emit(li, group, g):
        if li == 0:
            which = GROUPS[group]
        elif group == "mixer":
            which = tuple(range(len(MATS)))
        else:
            return None
        handle, started = _scatter_start([g[MATS[t][0]] for t in which], f"{li}_{group}", jnp.zeros(TOKEN_SHAPE, F32))
        scatters[li].append((which, handle))
        return started

    loss, grad_x, grads, g_final = _local_step(x[0], mem[0], positions[0], weights, small, wts["final_norm"],
                                               loss_target[0], emit, token)
    loss = lax.psum(loss, ("x", "y", "c"))

    nm = len(mat_names)
    mine = [[None] * nm for _ in range(DEPTH)]
    for li in range(DEPTH):
        for which, handle in scatters[li]:
            for t, b in zip(which, _exchange_finish(handle, grad_x)):
                mine[li][t] = b
    swapped = _swap_cores([b for layer in mine for b in layer], name="swap_cores")
    other = [swapped[li * nm:(li + 1) * nm] for li in range(DEPTH)]
    mat_out = {k: _adamw_shard([mine[li][t] for li in range(DEPTH)], [other[li][t] for li in range(DEPTH)],
                               wts[k], mom[k], var[k], name=f"adamw_{k}") for t, k in enumerate(mat_names)}

    def pack_small(get, fin):
        flat = [get(k).reshape(-1) for k, _ in SMALL] + [fin.reshape(-1)]
        n = sum(f.shape[0] for f in flat)
        return jnp.concatenate(flat + [jnp.zeros((-n % PACK_COLS,), F32)]).reshape(1, -1)

    gs = pack_small(lambda k: jnp.stack([grads[li][k] for li in range(DEPTH)]), g_final)
    g8 = _all_gather8(gs, name="gather_small_grads")
    small_out = _adamw_small(g8, pack_small(wts.get, wts["final_norm"]), pack_small(mom.get, mom["final_norm"]),
                             pack_small(var.get, var["final_norm"]), name="adamw_small")

    def unpack_small(buf):
        out, off = {}, 0
        for k, nel in SMALL:
            out[k] = buf[0, off:off + DEPTH * nel].reshape(DEPTH, nel)
            off += DEPTH * nel
        out["final_norm"] = buf[0, off:off + D_MODEL]
        return out

    small_res = [unpack_small(b) for b in small_out]
    res = []
    for kind in range(4):
        for k in names:
            res.append(small_res[kind][k] if k in small_res[kind] else mat_out[k][kind])
    return (loss, grad_x[None], *res)
```

```python
import functools
import math

import jax
import jax.numpy as jnp
from jax import lax
from jax.experimental import pallas as pl
from jax.experimental.pallas import tpu as pltpu

F32 = jnp.float32
BF16 = jnp.bfloat16
HIGHEST = lax.Precision.HIGHEST
SDS = jax.ShapeDtypeStruct
MESH = pl.DeviceIdType.MESH

D_MODEL = 1024
DEPTH = 4
EPS = 1e-6
SSM_HEADS = 16
SSM_HEAD_DIM = 64
D_SSM = 1024
SSM_GROUPS = 4
SSM_STATE = 128
SSM_CONV = 4
SSM_CHUNK = 128
CONV_CH = 2048
MLA_HEADS = 16
QK_NOPE = 64
QK_ROPE = 32
V_DIM = 64
Q_LORA = 384
KV_LORA = 256
ROPE_THETA = 10000.0
MEM_HEADS = 4
MEM_HEAD_DIM = 256
D_FF = 2816
FFN_CONV = 3
D_IN = 3760
ADAM_LR = 0.001
ADAM_B1 = 0.9
ADAM_B2 = 0.999
ADAM_EPS = 1e-08
ADAM_WD = 0.01
ADAM_STEP = 10

LANES = 128
HEAD_PAD = 128
N_CHIPS = 4
N_DEV = 8
VMEM_CAP_MB = 56

P_XBC, P_Z, P_CQ, P_DT, P_CKV, P_KR, P_IN = 0, 2048, 3072, 3456, 3584, 3840, 4096
NEG = -1e30


def _tile(n, pref):
    t = (min(n, pref) // LANES) * LANES
    while t >= LANES:
        if n % t == 0:
            return t
        t -= LANES
    return n


def _params(sem=None, vmem_bytes=None):
    kw = {}
    if sem is not None:
        kw["dimension_semantics"] = sem
    if vmem_bytes is not None:
        kw["vmem_limit_bytes"] = int(min(max(vmem_bytes, 16 << 20), VMEM_CAP_MB << 20))
    return pltpu.CompilerParams(**kw)


def _nbytes(shape, dtype):
    return math.prod(shape) * jnp.dtype(dtype).itemsize


def _mm(a, b, *, ta=False, tb=False, res=None, out_dtype=F32, name, a_col=None, b_lead=(), b_rows=None,
        b_chips=None, o_chips=None):
    if ta:
        k, m = a.shape
    else:
        m, k = (a.shape[0], a.shape[1] if a_col is None else a_col[0])
    rows_b, cols_b = b.shape[-2:]
    row0 = 0
    if b_rows is not None:
        row0, rows_b = b_rows
    nlead = len(b_lead)
    if b_chips is not None:
        assert not tb
        kb, tn, n = rows_b, cols_b, b_chips[1] * cols_b
        b_blk = (None,) * (1 + nlead) + (kb, tn)
        b_map = lambda i, j: (b_chips[0] + j,) + tuple(b_lead) + (0, 0)
    elif tb:
        n, kb = rows_b, cols_b
        tn = _tile(n, 512)
        assert row0 % tn == 0
        b_blk = (None,) * nlead + (tn, kb)
        b_map = lambda i, j: tuple(b_lead) + (j + row0 // tn, 0)
    else:
        kb, n = rows_b, cols_b
        tn = o_chips if o_chips else _tile(n, 512)
        assert row0 % kb == 0
        b_blk = (None,) * nlead + (kb, tn)
        b_map = lambda i, j: tuple(b_lead) + (row0 // kb, j)
    assert k == kb, (a.shape, b.shape, ta, tb, k, kb)
    tm = _tile(m, 512)
    if ta:
        a_blk, a_map = (k, tm), (lambda i, j: (0, i))
    else:
        a_blk, a_map = (tm, k), ((lambda i, j: (i, 0)) if a_col is None else (lambda i, j: (i, a_col[1])))
    if o_chips:
        o_spec = pl.BlockSpec((None, tm, tn), lambda i, j: (j, i, 0))
        o_shape = SDS((n // tn, m, tn), out_dtype)
    else:
        o_spec = pl.BlockSpec((tm, tn), lambda i, j: (i, j))
        o_shape = SDS((m, n), out_dtype)
    dims = (((0 if ta else 1,), (1 if tb else 0,)), ((), ()))
    has_res = res is not None

    def body(*refs):
        a_ref, b_ref = refs[0], refs[1]
        o_ref = refs[-1]
        acc = lax.dot_general(a_ref[...].astype(BF16), b_ref[...].astype(BF16), dims, preferred_element_type=F32)
        if has_res:
            acc = acc + refs[2][...]
        o_ref[...] = acc.astype(o_ref.dtype)

    bb = tuple(d for d in b_blk if d is not None)
    vmem = 2 * (_nbytes(a_blk, a.dtype) + _nbytes(bb, b.dtype) + (2 if has_res else 1) * _nbytes((tm, tn), F32))
    vmem += _nbytes(a_blk, BF16) + _nbytes(bb, BF16) + 2 * _nbytes((tm, tn), F32) + (4 << 20)
    args = (a, b) + ((res,) if has_res else ())
    specs = [pl.BlockSpec(a_blk, a_map), pl.BlockSpec(b_blk, b_map)] + ([o_spec] if has_res else [])
    return pl.pallas_call(body, grid=(m // tm, n // tn), in_specs=specs, out_specs=o_spec, out_shape=o_shape, name=name,
                          compiler_params=_params(("parallel", "parallel"), vmem))(*args)


def _sigmoid(x):
    return 1.0 / (1.0 + jnp.exp(-x))


def _rms_fwd(x, g, *, col=None, name):
    s = x.shape[0]
    w, ci = (x.shape[1], 0) if col is None else col
    tm = min(s, 512)

    def body(x_ref, g_ref, o_ref):
        xv = x_ref[...].astype(F32)
        r = lax.rsqrt(jnp.mean(xv * xv, axis=-1, keepdims=True) + EPS)
        o_ref[...] = (xv * r * g_ref[...]).astype(o_ref.dtype)

    return pl.pallas_call(
        body, grid=(s // tm,),
        in_specs=[pl.BlockSpec((tm, w), lambda i: (i, ci)), pl.BlockSpec((1, w), lambda i: (0, 0))],
        out_specs=pl.BlockSpec((tm, w), lambda i: (i, 0)), out_shape=SDS((s, w), BF16), name=name,
        compiler_params=_params(("parallel",), 10 * tm * w * 4))(x, g.reshape(1, w))


def _rms_bwd(x, g, dy, dres=None, *, col=None, name):
    s = x.shape[0]
    w, ci = (x.shape[1], 0) if col is None else col
    tm = min(s, 512)
    has_res = dres is not None

    def body(*refs):
        x_ref, g_ref, dy_ref = refs[:3]
        dx_ref, dxb_ref, dg_ref = refs[-3:]
        xv = x_ref[...].astype(F32)
        dyv = dy_ref[...].astype(F32)
        r = lax.rsqrt(jnp.mean(xv * xv, axis=-1, keepdims=True) + EPS)
        u = dyv * g_ref[...]
        dx = r * u - xv * (r * r * r) * jnp.mean(xv * u, axis=-1, keepdims=True)
        if has_res:
            dx = dx + refs[3][...]
        dx_ref[...] = dx
        dxb_ref[...] = dx.astype(BF16)

        @pl.when(pl.program_id(0) == 0)
        def _():
            dg_ref[...] = jnp.zeros_like(dg_ref)

        dg_ref[...] += jnp.sum(dyv * xv * r, axis=0, keepdims=True)

    blk = pl.BlockSpec((tm, w), lambda i: (i, 0))
    specs = [pl.BlockSpec((tm, w), lambda i: (i, ci)), pl.BlockSpec((1, w), lambda i: (0, 0)), blk]
    args = [x, g.reshape(1, w), dy]
    if has_res:
        specs.append(blk)
        args.append(dres)
    dx, dxb, dg = pl.pallas_call(
        body, grid=(s // tm,), in_specs=specs,
        out_specs=(blk, blk, pl.BlockSpec((1, w), lambda i: (0, 0))),
        out_shape=(SDS((s, w), F32), SDS((s, w), BF16), SDS((1, w), F32)), name=name,
        compiler_params=_params(("arbitrary",), 18 * tm * w * 4))(*args)
    return dx, dxb, dg.reshape(w)


def _gated_rms_fwd(y, proj, g, *, name):
    s, w = y.shape
    tm = min(s, 512)

    def body(y_ref, z_ref, g_ref, o_ref):
        z = z_ref[...]
        t = y_ref[...] * (z * _sigmoid(z))
        r = lax.rsqrt(jnp.mean(t * t, axis=-1, keepdims=True) + EPS)
        o_ref[...] = (t * r * g_ref[...]).astype(o_ref.dtype)

    blk = pl.BlockSpec((tm, w), lambda i: (i, 0))
    return pl.pallas_call(
        body, grid=(s // tm,),
        in_specs=[blk, pl.BlockSpec((tm, w), lambda i: (i, P_Z // w)), pl.BlockSpec((1, w), lambda i: (0, 0))],
        out_specs=blk, out_shape=SDS((s, w), BF16), name=name,
        compiler_params=_params(("parallel",), 14 * tm * w * 4))(y, proj, g.reshape(1, w))


def _gated_rms_bwd(y, proj, g, dout, *, name):
    s, w = y.shape
    tm = min(s, 512)

    def body(y_ref, z_ref, g_ref, do_ref, dy_ref, dz_ref, dg_ref):
        z = z_ref[...]
        yv = y_ref[...]
        dov = do_ref[...]
        sg = _sigmoid(z)
        sz = z * sg
        t = yv * sz
        r = lax.rsqrt(jnp.mean(t * t, axis=-1, keepdims=True) + EPS)
        u = dov * g_ref[...]
        dt = r * u - t * (r * r * r) * jnp.mean(t * u, axis=-1, keepdims=True)
        dy_ref[...] = dt * sz
        dz_ref[...] = (dt * yv * (sg * (1.0 + z * (1.0 - sg)))).astype(dz_ref.dtype)

        @pl.when(pl.program_id(0) == 0)
        def _():
            dg_ref[...] = jnp.zeros_like(dg_ref)

        dg_ref[...] += jnp.sum(dov * t * r, axis=0, keepdims=True)

    blk = pl.BlockSpec((tm, w), lambda i: (i, 0))
    vec = pl.BlockSpec((1, w), lambda i: (0, 0))
    dy, dz, dg = pl.pallas_call(
        body, grid=(s // tm,),
        in_specs=[blk, pl.BlockSpec((tm, w), lambda i: (i, P_Z // w)), vec, blk],
        out_specs=(blk, blk, vec), out_shape=(SDS((s, w), F32), SDS((s, w), BF16), SDS((1, w), F32)), name=name,
        compiler_params=_params(("arbitrary",), 24 * tm * w * 4))(y, proj, g.reshape(1, w), dout)
    return dy, dz, dg.reshape(w)


def _final_loss(x, g, target, *, name):
    s, w = x.shape
    tm = min(s, 512)

    def body(x_ref, g_ref, t_ref, loss_ref, dx_ref, dxb_ref, dg_ref):
        xv = x_ref[...]
        gv = g_ref[...]
        r = lax.rsqrt(jnp.mean(xv * xv, axis=-1, keepdims=True) + EPS)
        xn = xv * r
        diff = xn * gv - t_ref[...]
        dy = diff * (1.0 / w)
        u = dy * gv
        dx = r * u - xv * (r * r * r) * jnp.mean(xv * u, axis=-1, keepdims=True)
        dx_ref[...] = dx
        dxb_ref[...] = dx.astype(BF16)

        @pl.when(pl.program_id(0) == 0)
        def _():
            dg_ref[...] = jnp.zeros_like(dg_ref)
            loss_ref[...] = jnp.zeros_like(loss_ref)

        dg_ref[...] += jnp.sum(dy * xn, axis=0, keepdims=True)
        part = jnp.sum(jnp.sum(diff * diff, axis=1, keepdims=True), axis=0, keepdims=True) * (0.5 / w)
        loss_ref[...] += jnp.broadcast_to(part, loss_ref.shape)

    blk = pl.BlockSpec((tm, w), lambda i: (i, 0))
    vec = pl.BlockSpec((1, w), lambda i: (0, 0))
    loss, dx, dxb, dg = pl.pallas_call(
        body, grid=(s // tm,), in_specs=[blk, vec, blk],
        out_specs=(pl.BlockSpec((1, LANES), lambda i: (0, 0)), blk, blk, vec),
        out_shape=(SDS((1, LANES), F32), SDS((s, w), F32), SDS((s, w), BF16), SDS((1, w), F32)), name=name,
        compiler_params=_params(("arbitrary",), 18 * tm * w * 4))(x, g.reshape(1, w), target)
    return loss[0, 0], dx, dxb, dg.reshape(w)


def _shift_down(x, k):
    if k == 0:
        return x
    row = lax.broadcasted_iota(jnp.int32, x.shape, 0)
    return jnp.where(row < k, 0.0, pltpu.roll(x, k, axis=0))


def _shift_up(x, k):
    if k == 0:
        return x
    s = x.shape[0]
    row = lax.broadcasted_iota(jnp.int32, x.shape, 0)
    return jnp.where(row >= s - k, 0.0, pltpu.roll(x, s - k, axis=0))


def _conv_pre(x, w, b, kw):
    pre = b
    for j in range(kw):
        pre = pre + w[j:j + 1, :] * _shift_down(x, kw - 1 - j)
    return pre


def _conv_bwd_terms(x, w, dpre, kw):
    dx = jnp.zeros_like(x)
    dws = []
    for j in range(kw):
        dx = dx + w[j:j + 1, :] * _shift_up(dpre, kw - 1 - j)
        dws.append(jnp.sum(dpre * _shift_down(x, kw - 1 - j), axis=0, keepdims=True))
    return dx, jnp.concatenate(dws, axis=0), jnp.sum(dpre, axis=0, keepdims=True)


def _ssm_conv_fwd(proj, w, b, *, name):
    s = proj.shape[0]
    cw = 256

    def body(x_ref, w_ref, b_ref, o_ref):
        pre = _conv_pre(x_ref[...], w_ref[...], b_ref[...], SSM_CONV)
        o_ref[...] = pre * _sigmoid(pre)

    return pl.pallas_call(
        body, grid=(CONV_CH // cw,),
        in_specs=[pl.BlockSpec((s, cw), lambda j: (0, j)), pl.BlockSpec((SSM_CONV, cw), lambda j: (0, j)),
                  pl.BlockSpec((1, cw), lambda j: (0, j))],
        out_specs=pl.BlockSpec((s, cw), lambda j: (0, j)), out_shape=SDS((s, CONV_CH), F32), name=name,
        compiler_params=_params(("parallel",), 12 * s * cw * 4))(proj, w, b.reshape(1, CONV_CH))


def _ssm_conv_bwd(proj, w, b, dxbc, *, name):
    s = proj.shape[0]
    cw = 256

    def body(x_ref, w_ref, b_ref, dy_ref, dx_ref, dw_ref, db_ref):
        x = x_ref[...]
        wv = w_ref[...]
        pre = _conv_pre(x, wv, b_ref[...], SSM_CONV)
        sg = _sigmoid(pre)
        dpre = dy_ref[...] * (sg * (1.0 + pre * (1.0 - sg)))
        dx, dw, db = _conv_bwd_terms(x, wv, dpre, SSM_CONV)
        dx_ref[...] = dx.astype(dx_ref.dtype)
        dw_ref[...] = dw
        db_ref[...] = db

    col = pl.BlockSpec((s, cw), lambda j: (0, j))
    wsp = pl.BlockSpec((SSM_CONV, cw), lambda j: (0, j))
    bsp = pl.BlockSpec((1, cw), lambda j: (0, j))
    dx, dw, db = pl.pallas_call(
        body, grid=(CONV_CH // cw,), in_specs=[col, wsp, bsp, col], out_specs=(col, wsp, bsp),
        out_shape=(SDS((s, CONV_CH), BF16), SDS((SSM_CONV, CONV_CH), F32), SDS((1, CONV_CH), F32)), name=name,
        compiler_params=_params(("parallel",), 20 * s * cw * 4))(proj, w, b.reshape(1, CONV_CH), dxbc)
    return dx, dw, db.reshape(CONV_CH)


def _ffn_conv_fwd(up_g, up_v, w, b, *, name):
    s = up_g.shape[0]
    cw = 256
    nb = D_FF // cw

    def body(g_ref, v_ref, wg_ref, wv_ref, bg_ref, bv_ref, o_ref):
        gate = _conv_pre(g_ref[...], wg_ref[...], bg_ref[...], FFN_CONV)
        val = _conv_pre(v_ref[...], wv_ref[...], bv_ref[...], FFN_CONV)
        o_ref[...] = (gate * _sigmoid(gate) * val).astype(o_ref.dtype)

    col = pl.BlockSpec((s, cw), lambda j: (0, j))
    b2 = b.reshape(1, 2 * D_FF)
    return pl.pallas_call(
        body, grid=(nb,),
        in_specs=[col, col, pl.BlockSpec((FFN_CONV, cw), lambda j: (0, j)), pl.BlockSpec((FFN_CONV, cw), lambda j: (0, j + nb)),
                  pl.BlockSpec((1, cw), lambda j: (0, j)), pl.BlockSpec((1, cw), lambda j: (0, j + nb))],
        out_specs=col, out_shape=SDS((s, D_FF), BF16), name=name,
        compiler_params=_params(("parallel",), 16 * s * cw * 4))(up_g, up_v, w, w, b2, b2)


def _ffn_conv_bwd(up_g, up_v, w, b, dact, *, name):
    s = up_g.shape[0]
    cw = 256
    nb = D_FF // cw

    def body(g_ref, v_ref, wg_ref, wv_ref, bg_ref, bv_ref, da_ref, dg_ref, dv_ref, dwg_ref, dwv_ref, dbg_ref, dbv_ref):
        xg, xv = g_ref[...], v_ref[...]
        wg, wv = wg_ref[...], wv_ref[...]
        gate = _conv_pre(xg, wg, bg_ref[...], FFN_CONV)
        val = _conv_pre(xv, wv, bv_ref[...], FFN_CONV)
        da = da_ref[...].astype(F32)
        sg = _sigmoid(gate)
        dgate = da * val * (sg * (1.0 + gate * (1.0 - sg)))
        dval = da * gate * sg
        dxg, dwg, dbg = _conv_bwd_terms(xg, wg, dgate, FFN_CONV)
        dxv, dwv, dbv = _conv_bwd_terms(xv, wv, dval, FFN_CONV)
        dg_ref[...] = dxg.astype(dg_ref.dtype)
        dv_ref[...] = dxv.astype(dv_ref.dtype)
        dwg_ref[...] = dwg
        dwv_ref[...] = dwv
        dbg_ref[...] = dbg
        dbv_ref[...] = dbv

    col = pl.BlockSpec((s, cw), lambda j: (0, j))
    wsp = pl.BlockSpec((FFN_CONV, cw), lambda j: (0, j))
    bsp = pl.BlockSpec((1, cw), lambda j: (0, j))
    b2 = b.reshape(1, 2 * D_FF)
    dg, dv, dwg, dwv, dbg, dbv = pl.pallas_call(
        body, grid=(nb,),
        in_specs=[col, col, wsp, pl.BlockSpec((FFN_CONV, cw), lambda j: (0, j + nb)), bsp,
                  pl.BlockSpec((1, cw), lambda j: (0, j + nb)), col],
        out_specs=(col, col, wsp, wsp, bsp, bsp),
        out_shape=(SDS((s, D_FF), BF16), SDS((s, D_FF), BF16), SDS((FFN_CONV, D_FF), F32), SDS((FFN_CONV, D_FF), F32),
                   SDS((1, D_FF), F32), SDS((1, D_FF), F32)), name=name,
        compiler_params=_params(("parallel",), 32 * s * cw * 4))(up_g, up_v, w, w, b2, b2, dact)
    return dg, dv, jnp.concatenate([dwg, dwv], axis=1), jnp.concatenate([dbg, dbv], axis=1).reshape(2 * D_FF)


def _dot(a, b):
    return jnp.dot(a.astype(BF16), b.astype(BF16), preferred_element_type=F32)


def _dot_nt(a, b):
    return lax.dot_general(a.astype(BF16), b.astype(BF16), (((1,), (1,)), ((), ())), preferred_element_type=F32)


def _dot_tn(a, b):
    return lax.dot_general(a.astype(BF16), b.astype(BF16), (((0,), (0,)), ((), ())), preferred_element_type=F32)


def _ssd_chunk_terms(dtraw, bias, a_log):
    ell = dtraw.shape[0]
    lane = lax.broadcasted_iota(jnp.int32, dtraw.shape, 1)
    valid = lane < SSM_HEADS
    pre = dtraw + bias
    dt = jnp.where(valid, jnp.where(pre > 20.0, pre, jnp.log(1.0 + jnp.exp(jnp.minimum(pre, 20.0)))), 0.0)
    a = -jnp.exp(a_log)
    ad = dt * a
    row = lax.broadcasted_iota(jnp.int32, (ell, ell), 0)
    colm = lax.broadcasted_iota(jnp.int32, (ell, ell), 1)
    tril = row >= colm
    cs = jnp.dot(tril.astype(F32), ad, precision=HIGHEST, preferred_element_type=F32)
    cs_last = cs[ell - 1:ell, :]
    return pre, dt, a, cs, cs_last, tril


def _lane_put(col, h, shape):
    lane = lax.broadcasted_iota(jnp.int32, shape, 1)
    return jnp.where(lane == h, col, 0.0)


def _ssd_fwd(xbc, proj, dt_bias, a_log, d_skip, *, name):
    s = xbc.shape[0]
    nc = s // SSM_CHUNK
    ell, n, p = SSM_CHUNK, SSM_STATE, SSM_HEAD_DIM
    rpg = SSM_HEADS // SSM_GROUPS

    def body(x_ref, dt_ref, bias_ref, alog_ref, dskip_ref, y_ref, ps_ref, state):
        @pl.when(pl.program_id(0) == 0)
        def _():
            state[...] = jnp.zeros_like(state)

        _, dt, _, cs, cs_last, tril = _ssd_chunk_terms(dt_ref[...], bias_ref[...], alog_ref[...])
        e = jnp.exp(cs)
        ds = jnp.exp(cs_last - cs)
        cd = jnp.exp(cs_last)
        cst = cs.T
        dskip = dskip_ref[...]
        st = state[...]
        ps_ref[0] = st
        xv = x_ref[...]
        ys, new = [], []
        for g in range(SSM_GROUPS):
            bg = xv[:, D_SSM + n * g:D_SSM + n * (g + 1)]
            cg = xv[:, D_SSM + n * (SSM_GROUPS + g):D_SSM + n * (SSM_GROUPS + g + 1)]
            cb = _dot_nt(cg, bg)
            for r in range(rpg):
                h = g * rpg + r
                hs = slice(p * h, p * (h + 1))
                xs = xv[:, hs]
                xd = xs * dt[:, h:h + 1]
                lmat = jnp.exp(jnp.where(tril, cs[:, h:h + 1] - cst[h:h + 1, :], -jnp.inf))
                prev = st[:, hs]
                ys.append(_dot(cb * lmat, xd) + _dot(cg, prev) * e[:, h:h + 1] + xs * dskip[:, h:h + 1])
                new.append(prev * cd[:, h:h + 1] + _dot_tn(bg, xd * ds[:, h:h + 1]))
        y_ref[...] = jnp.concatenate(ys, axis=1)
        state[...] = jnp.concatenate(new, axis=1)

    vec = pl.BlockSpec((1, LANES), lambda c: (0, 0))
    return pl.pallas_call(
        body, grid=(nc,),
        in_specs=[pl.BlockSpec((ell, CONV_CH), lambda c: (c, 0)), pl.BlockSpec((ell, LANES), lambda c: (c, P_DT // LANES)),
                  vec, vec, vec],
        out_specs=(pl.BlockSpec((ell, D_SSM), lambda c: (c, 0)), pl.BlockSpec((1, n, D_SSM), lambda c: (c, 0, 0))),
        out_shape=(SDS((s, D_SSM), F32), SDS((nc, n, D_SSM), F32)),
        scratch_shapes=[pltpu.VMEM((n, D_SSM), F32)], name=name,
        compiler_params=_params(("arbitrary",), 24 << 20))(xbc, proj, dt_bias, a_log, d_skip)


def _ssd_bwd(xbc, proj, dt_bias, a_log, d_skip, prev_states, dy, *, name):
    s = xbc.shape[0]
    nc = s // SSM_CHUNK
    ell, n, p = SSM_CHUNK, SSM_STATE, SSM_HEAD_DIM
    rpg = SSM_HEADS // SSM_GROUPS

    def body(x_ref, dt_ref, bias_ref, alog_ref, dskip_ref, ps_ref, dy_ref,
             dx_ref, ddt_ref, dalog_ref, ddskip_ref, dbias_ref, dstate):
        @pl.when(pl.program_id(0) == 0)
        def _():
            dstate[...] = jnp.zeros_like(dstate)
            dalog_ref[...] = jnp.zeros_like(dalog_ref)
            ddskip_ref[...] = jnp.zeros_like(ddskip_ref)
            dbias_ref[...] = jnp.zeros_like(dbias_ref)

        pre, dt, a, cs, cs_last, tril = _ssd_chunk_terms(dt_ref[...], bias_ref[...], alog_ref[...])
        e = jnp.exp(cs)
        ds = jnp.exp(cs_last - cs)
        cd = jnp.exp(cs_last)
        cst = cs.T
        dskip = dskip_ref[...]
        shape = (ell, LANES)
        ddt_acc = jnp.zeros(shape, F32)
        dcs_acc = jnp.zeros(shape, F32)
        dcs_rows = jnp.zeros(shape, F32)
        dlast_acc = jnp.zeros((1, LANES), F32)
        dskip_acc = jnp.zeros((1, LANES), F32)
        xv, dyv, psv, dst = x_ref[...], dy_ref[...], ps_ref[0], dstate[...]
        dxs, dbs, dcs_parts, dprevs = [], [], [], []
        for g in range(SSM_GROUPS):
            bsl = slice(D_SSM + n * g, D_SSM + n * (g + 1))
            csl = slice(D_SSM + n * (SSM_GROUPS + g), D_SSM + n * (SSM_GROUPS + g + 1))
            bg = xv[:, bsl]
            cg = xv[:, csl]
            cb = _dot_nt(cg, bg)
            dcb = jnp.zeros((ell, ell), F32)
            dbg = jnp.zeros((ell, n), F32)
            dcg = jnp.zeros((ell, n), F32)
            for r in range(rpg):
                h = g * rpg + r
                hs = slice(p * h, p * (h + 1))
                xs = xv[:, hs]
                dyh = dyv[:, hs]
                dt_h, e_h, ds_h, cd_h = dt[:, h:h + 1], e[:, h:h + 1], ds[:, h:h + 1], cd[:, h:h + 1]
                prev = psv[:, hs]
                dsn = dst[:, hs]
                xd = xs * dt_h
                dye = dyh * e_h
                cprev = _dot(cg, prev)
                dprev = dsn * cd_h + _dot_tn(cg, dye)
                dcg = dcg + _dot_nt(dye, prev)
                dcs_h = jnp.sum(dyh * cprev, axis=1, keepdims=True) * e_h
                dcd = jnp.sum(jnp.sum(dsn * prev, axis=1, keepdims=True), axis=0, keepdims=True)
                dlast_h = dcd * cd_h
                dxdd = _dot(bg, dsn)
                dbg = dbg + _dot_nt(xd * ds_h, dsn)
                dxd = dxdd * ds_h
                tmp = jnp.sum(dxdd * xd, axis=1, keepdims=True) * ds_h
                dlast_h = dlast_h + jnp.sum(tmp, axis=0, keepdims=True)
                dcs_h = dcs_h - tmp
                lmat = jnp.exp(jnp.where(tril, cs[:, h:h + 1] - cst[h:h + 1, :], -jnp.inf))
                gm = cb * lmat
                dgm = _dot_nt(dyh, xd)
                dxd = dxd + _dot_tn(gm, dyh)
                mm = dgm * gm
                dcs_h = dcs_h + jnp.sum(mm, axis=1, keepdims=True)
                sub = lax.broadcasted_iota(jnp.int32, shape, 0)
                dcs_rows = dcs_rows + jnp.where(sub == h, jnp.sum(mm, axis=0, keepdims=True), 0.0)
                dcb = dcb + dgm * lmat
                dxs.append(dxd * dt_h + dyh * dskip[:, h:h + 1])
                ddt_acc = ddt_acc + _lane_put(jnp.sum(dxd * xs, axis=1, keepdims=True), h, shape)
                dcs_acc = dcs_acc + _lane_put(dcs_h, h, shape)
                dlast_acc = dlast_acc + _lane_put(dlast_h, h, (1, LANES))
                dskip_acc = dskip_acc + _lane_put(
                    jnp.sum(jnp.sum(dyh * xs, axis=1, keepdims=True), axis=0, keepdims=True), h, (1, LANES))
                dprevs.append(dprev)
            dbs.append(dbg + _dot_tn(dcb, cg))
            dcs_parts.append(dcg + _dot(dcb, bg))
        dx_ref[...] = jnp.concatenate(dxs + dbs + dcs_parts, axis=1)
        dstate[...] = jnp.concatenate(dprevs, axis=1)
        rowi = lax.broadcasted_iota(jnp.int32, shape, 0)
        dcs = dcs_acc - dcs_rows.T + jnp.where(rowi == ell - 1, dlast_acc, 0.0)
        triu = lax.broadcasted_iota(jnp.int32, (ell, ell), 0) <= lax.broadcasted_iota(jnp.int32, (ell, ell), 1)
        dad = jnp.dot(triu.astype(F32), dcs, precision=HIGHEST, preferred_element_type=F32)
        ddt = ddt_acc + dad * a
        dalog_ref[...] += jnp.sum(dad * dt, axis=0, keepdims=True) * a
        ddskip_ref[...] += dskip_acc
        lane = lax.broadcasted_iota(jnp.int32, shape, 1)
        ddraw = jnp.where(lane < SSM_HEADS, ddt * _sigmoid(pre), 0.0)
        ddt_ref[...] = ddraw.astype(ddt_ref.dtype)
        dbias_ref[...] += jnp.sum(ddraw, axis=0, keepdims=True)

    vec = pl.BlockSpec((1, LANES), lambda c: (0, 0))
    rev = lambda c: nc - 1 - c
    outs = pl.pallas_call(
        body, grid=(nc,),
        in_specs=[pl.BlockSpec((ell, CONV_CH), lambda c: (rev(c), 0)),
                  pl.BlockSpec((ell, LANES), lambda c: (rev(c), P_DT // LANES)), vec, vec, vec,
                  pl.BlockSpec((1, n, D_SSM), lambda c: (rev(c), 0, 0)),
                  pl.BlockSpec((ell, D_SSM), lambda c: (rev(c), 0))],
        out_specs=(pl.BlockSpec((ell, CONV_CH), lambda c: (rev(c), 0)), pl.BlockSpec((ell, LANES), lambda c: (rev(c), 0)),
                   vec, vec, vec),
        out_shape=(SDS((s, CONV_CH), F32), SDS((s, LANES), BF16), SDS((1, LANES), F32), SDS((1, LANES), F32),
                   SDS((1, LANES), F32)),
        scratch_shapes=[pltpu.VMEM((n, D_SSM), F32)], name=name,
        compiler_params=_params(("arbitrary",), 32 << 20))(xbc, proj, dt_bias, a_log, d_skip, prev_states, dy)
    return outs


def _rope_swap(t):
    lane = lax.broadcasted_iota(jnp.int32, t.shape, 1)
    half = QK_ROPE // 2
    lo = (lane >= QK_NOPE) & (lane < QK_NOPE + half)
    hi = (lane >= QK_NOPE + half) & (lane < QK_NOPE + QK_ROPE)
    return jnp.where(lo, pltpu.roll(t, HEAD_PAD - half, axis=1), jnp.where(hi, pltpu.roll(t, half, axis=1), 0.0))


def _mla_prep(q, kv, proj, cos, sins, *, name):
    s = q.shape[0]
    tm = min(s, 256)
    scale = (QK_NOPE + QK_ROPE) ** -0.5

    def body(q_ref, kv_ref, kr_ref, cos_ref, sin_ref, qo_ref, ko_ref, vo_ref):
        cosv, sinv = cos_ref[...], sin_ref[...]
        kr = pltpu.roll(kr_ref[...], QK_NOPE, axis=1)
        lane = lax.broadcasted_iota(jnp.int32, kr.shape, 1)
        nope = lane < QK_NOPE
        kr = jnp.where(nope, 0.0, kr)
        kpe = kr * cosv + _rope_swap(kr) * sinv
        for hp in range(MLA_HEADS // 2):
            vs = []
            for h in (2 * hp, 2 * hp + 1):
                hs = slice(HEAD_PAD * h, HEAD_PAD * (h + 1))
                qh = q_ref[:, hs]
                kvh = kv_ref[:, hs]
                qo_ref[:, hs] = ((qh * cosv + _rope_swap(qh) * sinv) * scale).astype(qo_ref.dtype)
                ko_ref[:, hs] = (jnp.where(nope, kvh, 0.0) + kpe).astype(ko_ref.dtype)
                vs.append(kvh[:, QK_NOPE:])
            vo_ref[:, 2 * V_DIM * hp:2 * V_DIM * (hp + 1)] = jnp.concatenate(vs, axis=1).astype(vo_ref.dtype)

    wide = pl.BlockSpec((tm, MLA_HEADS * HEAD_PAD), lambda i: (i, 0))
    half = pl.BlockSpec((tm, MLA_HEADS * V_DIM), lambda i: (i, 0))
    tab = pl.BlockSpec((tm, LANES), lambda i: (i, 0))
    return pl.pallas_call(
        body, grid=(s // tm,),
        in_specs=[wide, wide, pl.BlockSpec((tm, LANES), lambda i: (i, P_KR // LANES)), tab, tab],
        out_specs=(wide, wide, half),
        out_shape=(SDS((s, MLA_HEADS * HEAD_PAD), BF16), SDS((s, MLA_HEADS * HEAD_PAD), BF16),
                   SDS((s, MLA_HEADS * V_DIM), BF16)), name=name,
        compiler_params=_params(("parallel",), 32 << 20))(q, kv, proj, cos, sins)


def _mla_prep_bwd(dqr, dkr, dv, cos, sins, *, name):
    s = dqr.shape[0]
    tm = min(s, 256)
    scale = (QK_NOPE + QK_ROPE) ** -0.5

    def body(dq_ref, dk_ref, dv_ref, cos_ref, sin_ref, dqo_ref, dkv_ref, dkr_ref):
        cosv, sinv = cos_ref[...], sin_ref[...]
        lane = lax.broadcasted_iota(jnp.int32, cosv.shape, 1)
        ksum = jnp.zeros(cosv.shape, F32)
        for h in range(MLA_HEADS):
            hs = slice(HEAD_PAD * h, HEAD_PAD * (h + 1))
            d = dq_ref[:, hs]
            dk = dk_ref[:, hs]
            dqo_ref[:, hs] = ((d * cosv + _rope_swap(d * sinv)) * scale).astype(dqo_ref.dtype)
            dkv_ref[:, hs] = jnp.concatenate([dk[:, :QK_NOPE], dv_ref[:, V_DIM * h:V_DIM * (h + 1)]], axis=1).astype(dkv_ref.dtype)
            ksum = ksum + dk
        ksum = jnp.where((lane >= QK_NOPE) & (lane < QK_NOPE + QK_ROPE), ksum, 0.0)
        un = ksum * cosv + _rope_swap(ksum * sinv)
        dkr_ref[...] = pltpu.roll(un, HEAD_PAD - QK_NOPE, axis=1).astype(dkr_ref.dtype)

    wide = pl.BlockSpec((tm, MLA_HEADS * HEAD_PAD), lambda i: (i, 0))
    half = pl.BlockSpec((tm, MLA_HEADS * V_DIM), lambda i: (i, 0))
    tab = pl.BlockSpec((tm, LANES), lambda i: (i, 0))
    return pl.pallas_call(
        body, grid=(s // tm,), in_specs=[wide, wide, half, tab, tab], out_specs=(wide, wide, tab),
        out_shape=(SDS((s, MLA_HEADS * HEAD_PAD), BF16), SDS((s, MLA_HEADS * HEAD_PAD), BF16), SDS((s, LANES), BF16)),
        name=name, compiler_params=_params(("parallel",), 40 << 20))(dqr, dkr, dv, cos, sins)


FLASH_TILE = 512


def _flash_fwd(q, k, v, *, name):
    s = q.shape[0]
    t = min(s, FLASH_TILE)
    nq = s // t
    npair = MLA_HEADS // 2

    def body(q_ref, k_ref, v_ref, o_ref, lse_ref):
        i = pl.program_id(1)
        qs = [q_ref[:, HEAD_PAD * e:HEAD_PAD * (e + 1)] for e in range(2)]
        diag = lax.broadcasted_iota(jnp.int32, (t, t), 0) >= lax.broadcasted_iota(jnp.int32, (t, t), 1)

        def step(j, carry, masked):
            rows = pl.ds(pl.multiple_of(j * t, t), t)
            new = []
            for e in range(2):
                m, l, acc = carry[e]
                sc = _dot_nt(qs[e], k_ref[rows, HEAD_PAD * e:HEAD_PAD * (e + 1)])
                if masked:
                    sc = jnp.where(diag, sc, NEG)
                m_new = jnp.maximum(m, jnp.max(sc, axis=1, keepdims=True))
                pr = jnp.exp(sc - m_new)
                alpha = jnp.exp(m - m_new)
                l = alpha * l + jnp.sum(pr, axis=1, keepdims=True)
                acc = alpha * acc + _dot(pr, v_ref[rows, V_DIM * e:V_DIM * (e + 1)])
                new.append((m_new, l, acc))
            return tuple(new)

        init = tuple((jnp.full((t, 1), NEG, F32), jnp.zeros((t, 1), F32), jnp.zeros((t, V_DIM), F32)) for _ in range(2))
        carry = lax.fori_loop(0, i, functools.partial(step, masked=False), init)
        carry = step(i, carry, True)
        o_ref[...] = jnp.concatenate([acc / l for _, l, acc in carry], axis=1)
        lse_ref[0] = jnp.concatenate([jnp.broadcast_to(m + jnp.log(l), (t, V_DIM)) for m, l, _ in carry], axis=1)

    return pl.pallas_call(
        body, grid=(npair, nq),
        in_specs=[pl.BlockSpec((t, 2 * HEAD_PAD), lambda hp, i: (i, hp)), pl.BlockSpec((s, 2 * HEAD_PAD), lambda hp, i: (0, hp)),
                  pl.BlockSpec((s, 2 * V_DIM), lambda hp, i: (0, hp))],
        out_specs=(pl.BlockSpec((t, 2 * V_DIM), lambda hp, i: (i, hp)), pl.BlockSpec((1, t, LANES), lambda hp, i: (hp, i, 0))),
        out_shape=(SDS((s, MLA_HEADS * V_DIM), F32), SDS((npair, s, LANES), F32)), name=name,
        compiler_params=_params(("parallel", "parallel"), 40 << 20))(q, k, v)


def _flash_bwd(q, k, v, o, lse, do, *, name):
    s = q.shape[0]
    t = min(s, FLASH_TILE)
    nq = s // t
    npair = MLA_HEADS // 2

    def body(q_ref, k_ref, v_ref, o_ref, lse_ref, do_ref, dq_ref, dk_ref, dv_ref):
        j = pl.program_id(1)

        @pl.when(j == 0)
        def _():
            dq_ref[...] = jnp.zeros_like(dq_ref)

        qsl = [slice(HEAD_PAD * e, HEAD_PAD * (e + 1)) for e in range(2)]
        vsl = [slice(V_DIM * e, V_DIM * (e + 1)) for e in range(2)]
        ks = [k_ref[:, qsl[e]] for e in range(2)]
        vs = [v_ref[:, vsl[e]] for e in range(2)]
        diag = lax.broadcasted_iota(jnp.int32, (t, t), 0) >= lax.broadcasted_iota(jnp.int32, (t, t), 1)

        def step(i, carry, masked):
            rows = pl.ds(pl.multiple_of(i * t, t), t)
            new = []
            for e in range(2):
                dk, dv = carry[e]
                qi = q_ref[rows, qsl[e]]
                doi = do_ref[rows, vsl[e]]
                delta = jnp.sum(doi * o_ref[rows, vsl[e]], axis=1, keepdims=True)
                lse_i = lse_ref[0, rows, vsl[e]][:, 0:1]
                sc = _dot_nt(qi, ks[e])
                if masked:
                    sc = jnp.where(diag, sc, NEG)
                pr = jnp.exp(sc - lse_i)
                dv = dv + _dot_tn(pr, doi)
                dsc = (pr * (_dot_nt(doi, vs[e]) - delta)).astype(BF16)
                dk = dk + _dot_tn(dsc, qi)
                dq_ref[rows, qsl[e]] += _dot(dsc, ks[e])
                new.append((dk, dv))
            return tuple(new)

        init = tuple((jnp.zeros((t, HEAD_PAD), F32), jnp.zeros((t, V_DIM), F32)) for _ in range(2))
        carry = step(j, init, True)
        carry = lax.fori_loop(j + 1, nq, functools.partial(step, masked=False), carry)
        dk_ref[...] = jnp.concatenate([dk for dk, _ in carry], axis=1)
        dv_ref[...] = jnp.concatenate([dv for _, dv in carry], axis=1)

    full_q = pl.BlockSpec((s, 2 * HEAD_PAD), lambda hp, j: (0, hp))
    full_v = pl.BlockSpec((s, 2 * V_DIM), lambda hp, j: (0, hp))
    blk_k = pl.BlockSpec((t, 2 * HEAD_PAD), lambda hp, j: (j, hp))
    blk_v = pl.BlockSpec((t, 2 * V_DIM), lambda hp, j: (j, hp))
    return pl.pallas_call(
        body, grid=(npair, nq),
        in_specs=[full_q, blk_k, blk_v, full_v, pl.BlockSpec((1, s, LANES), lambda hp, j: (hp, 0, 0)), full_v],
        out_specs=(full_q, blk_k, blk_v),
        out_shape=(SDS((s, MLA_HEADS * HEAD_PAD), F32), SDS((s, MLA_HEADS * HEAD_PAD), F32), SDS((s, MLA_HEADS * V_DIM), F32)),
        name=name, compiler_params=_params(("parallel", "arbitrary"), 48 << 20))(q, k, v, o, lse, do)


def _mem_attn_fwd(q, k, v, *, name):
    s = q.shape[0]
    tm = min(s, 512)
    ml = k.shape[0]
    scale = MEM_HEAD_DIM ** -0.5

    def body(q_ref, k_ref, v_ref, o_ref):
        for h in range(MEM_HEADS):
            hs = slice(MEM_HEAD_DIM * h, MEM_HEAD_DIM * (h + 1))
            sc = _dot_nt(q_ref[:, hs], k_ref[:, hs]) * scale
            pr = jnp.exp(sc - jnp.max(sc, axis=1, keepdims=True))
            pr = pr / jnp.sum(pr, axis=1, keepdims=True)
            o_ref[:, hs] = _dot(pr, v_ref[:, hs]).astype(o_ref.dtype)

    blk = pl.BlockSpec((tm, D_MODEL), lambda i: (i, 0))
    kv = pl.BlockSpec((ml, D_MODEL), lambda i: (0, 0))
    return pl.pallas_call(body, grid=(s // tm,), in_specs=[blk, kv, kv], out_specs=blk,
                          out_shape=SDS((s, D_MODEL), BF16), name=name,
                          compiler_params=_params(("parallel",), 24 << 20))(q, k, v)


def _mem_attn_bwd(q, k, v, do, *, name):
    s = q.shape[0]
    tm = min(s, 512)
    ml = k.shape[0]
    scale = MEM_HEAD_DIM ** -0.5

    def body(q_ref, k_ref, v_ref, do_ref, dq_ref, dk_ref, dv_ref):
        @pl.when(pl.program_id(0) == 0)
        def _():
            dk_ref[...] = jnp.zeros_like(dk_ref)
            dv_ref[...] = jnp.zeros_like(dv_ref)

        for h in range(MEM_HEADS):
            hs = slice(MEM_HEAD_DIM * h, MEM_HEAD_DIM * (h + 1))
            qh, kh, vh, doh = q_ref[:, hs], k_ref[:, hs], v_ref[:, hs], do_ref[:, hs]
            sc = _dot_nt(qh, kh) * scale
            pr = jnp.exp(sc - jnp.max(sc, axis=1, keepdims=True))
            pr = pr / jnp.sum(pr, axis=1, keepdims=True)
            dp = _dot_nt(doh, vh)
            dsc = pr * (dp - jnp.sum(pr * dp, axis=1, keepdims=True)) * scale
            dq_ref[:, hs] = _dot(dsc, kh).astype(dq_ref.dtype)
            dk_ref[:, hs] += _dot_tn(dsc, qh)
            dv_ref[:, hs] += _dot_tn(pr, doh)

    blk = pl.BlockSpec((tm, D_MODEL), lambda i: (i, 0))
    kv = pl.BlockSpec((ml, D_MODEL), lambda i: (0, 0))
    return pl.pallas_call(body, grid=(s // tm,), in_specs=[blk, kv, kv, blk], out_specs=(blk, kv, kv),
                          out_shape=(SDS((s, D_MODEL), BF16), SDS((ml, D_MODEL), F32), SDS((ml, D_MODEL), F32)), name=name,
                          compiler_params=_params(("arbitrary",), 32 << 20))(q, k, v, do)


MATS = (("w_in", (1024, 940), 1), ("w_uq", (384, 384), 1), ("w_ukv", (256, 512), 1), ("w_out", (512, 1024), 0),
        ("ssm_conv_w", (4, 512), 1),
        ("w_mq", (256, 1024), 0), ("w_mk", (256, 1024), 0), ("w_mv", (256, 1024), 0), ("w_mo", (256, 1024), 0),
        ("w_up", (1024, 1408), 1), ("w_down", (704, 1024), 0), ("ffn_conv_w", (3, 1408), 1))
GROUPS = {"mixer": (0, 1, 2, 3, 4), "mem": (5, 6, 7, 8), "ffn": (9, 10, 11)}
UP_SHARD_COLS = 1408
F32_ON_WIRE = ("ssm_conv_w", "ffn_conv_w")
SMALL = (("norm_mix", 1024), ("ssm_conv_b", 2048), ("dt_bias", 16), ("a_log", 16), ("d_skip", 16), ("ssm_norm", 1024),
         ("q_norm", 384), ("kv_norm", 256), ("attn_out_norm", 1024), ("norm_mem_q", 1024), ("norm_mem_kv", 1024),
         ("norm_ffn", 1024), ("ffn_conv_b", 5632))
PACK_COLS = 1024


def _pad_cols(t, n):
    return jnp.pad(t, ((0, 0),) * (t.ndim - 1) + ((0, n - t.shape[-1]),))


def _w_in_to_padded(t):
    z, xbc, dt, cq, ckv, kr = jnp.split(t, (1024, 3072, 3088, 3472, 3728), axis=-1)
    return jnp.concatenate([xbc, z, cq, _pad_cols(dt, LANES), ckv, _pad_cols(kr, P_IN - P_KR)], axis=-1)


def _w_in_from_padded(t):
    return jnp.concatenate([t[..., P_Z:P_Z + 1024], t[..., P_XBC:P_XBC + 2048], t[..., P_DT:P_DT + SSM_HEADS],
                            t[..., P_CQ:P_CQ + Q_LORA], t[..., P_CKV:P_CKV + KV_LORA], t[..., P_KR:P_KR + QK_ROPE]], axis=-1)


def _cols_joined(g):
    return jnp.concatenate([g[j] for j in range(N_CHIPS)], axis=-1)


def _cols_by_chip(t, dtype):
    k = t.shape[0]
    return t.reshape(k, N_CHIPS, -1).transpose(1, 0, 2).astype(dtype)


def _rows_by_chip(t):
    return t.reshape(N_CHIPS, -1, t.shape[-1])


def _mixer_weights(gw):
    wl = {}
    wl["w_in"] = _w_in_to_padded(_cols_joined(gw["w_in"]))
    uq = _cols_joined(gw["w_uq"]).reshape(Q_LORA, MLA_HEADS, QK_NOPE + QK_ROPE)
    wl["w_uq"] = _pad_cols(uq, HEAD_PAD).reshape(Q_LORA, MLA_HEADS * HEAD_PAD)
    wl["w_ukv"] = _cols_joined(gw["w_ukv"])
    wl["ssm_conv_w"] = _cols_joined(gw["ssm_conv_w"])
    return wl


def _layer_fwd(x0, mem, cos, sins, weights, sp, li):
    n = lambda t: f"l{li}_{t}"
    lead = ()
    sv = {"x0": x0}
    gw = dict(weights("mixer", x0))
    wl = _mixer_weights(gw)
    h = _rms_fwd(x0, sp["norm_mix"], name=n("mix_norm"))
    proj = _mm(h, wl["w_in"], name=n("mix_proj"))
    xbc = _ssm_conv_fwd(proj, wl["ssm_conv_w"], sp["ssm_conv_b"], name=n("ssm_conv"))
    y, pstates = _ssd_fwd(xbc, proj, sp["dt_bias"], sp["a_log"], sp["d_skip"], name=n("ssd"))
    y_ssm = _gated_rms_fwd(y, proj, sp["ssm_norm"], name=n("ssm_gate"))
    cqn = _rms_fwd(proj, sp["q_norm"], col=(Q_LORA, P_CQ // Q_LORA), name=n("q_norm"))
    ckvn = _rms_fwd(proj, sp["kv_norm"], col=(KV_LORA, P_CKV // KV_LORA), name=n("kv_norm"))
    q = _mm(cqn, wl["w_uq"], name=n("uq"))
    kv = _mm(ckvn, wl["w_ukv"], name=n("ukv"))
    qr, kr, v = _mla_prep(q, kv, proj, cos, sins, name=n("rope"))
    att, lse = _flash_fwd(qr, kr, v, name=n("flash"))
    y_att = _rms_fwd(att, sp["attn_out_norm"], name=n("att_norm"))
    x1 = _mm(y_ssm, gw["w_out"], b_lead=lead, b_rows=(0, D_SSM), res=x0, name=n("out_a"))
    x1 = _mm(y_att, gw["w_out"], b_lead=lead, b_rows=(D_SSM, D_SSM), res=x1, name=n("out_b"))
    sv.update(h=h, proj=proj, xbc=xbc, y=y, pstates=pstates, y_ssm=y_ssm, cqn=cqn, ckvn=ckvn, qr=qr, kr=kr, v=v,
              att=att, lse=lse, y_att=y_att, x1=x1)
    gw.update(weights("mem", x1))
    hq = _rms_fwd(x1, sp["norm_mem_q"], name=n("memq_norm"))
    hm = _rms_fwd(mem, sp["norm_mem_kv"], name=n("memkv_norm"))
    mq = _mm(hq, gw["w_mq"], b_lead=lead, out_dtype=BF16, name=n("mq"))
    mk = _mm(hm, gw["w_mk"], b_lead=lead, out_dtype=BF16, name=n("mk"))
    mv = _mm(hm, gw["w_mv"], b_lead=lead, out_dtype=BF16, name=n("mv"))
    mo = _mem_attn_fwd(mq, mk, mv, name=n("mem_attn"))
    x2 = _mm(mo, gw["w_mo"], b_lead=lead, res=x1, name=n("mo"))
    sv.update(hq=hq, hm=hm, mq=mq, mk=mk, mv=mv, mo=mo, x2=x2)
    gw.update(weights("ffn", x2))
    wl["ffn_conv_w"] = _cols_joined(gw["ffn_conv_w"])
    hf = _rms_fwd(x2, sp["norm_ffn"], name=n("ffn_norm"))
    up_g = _mm(hf, gw["w_up"], b_lead=lead, b_chips=(0, 2), name=n("up_g"))
    up_v = _mm(hf, gw["w_up"], b_lead=lead, b_chips=(2, 2), name=n("up_v"))
    act = _ffn_conv_fwd(up_g, up_v, wl["ffn_conv_w"], sp["ffn_conv_b"], name=n("ffn_conv"))
    x3 = _mm(act, gw["w_down"], b_lead=lead, res=x2, name=n("down"))
    sv.update(hf=hf, up_g=up_g, up_v=up_v, act=act)
    return x3, sv, gw, wl


def _layer_bwd(dx3, dx3b, mem, cos, sins, gw, wl, sp, sv, li, emit):
    n = lambda t: f"l{li}_b_{t}"
    lead = ()
    g = {}

    def after(token, v):
        return v if token is None else v + token[0, 0]

    dact = _mm(dx3b, gw["w_down"], tb=True, b_lead=lead, out_dtype=BF16, name=n("down_dx"))
    g["w_down"] = _rows_by_chip(_mm(sv["act"], dx3b, ta=True, out_dtype=BF16, name=n("down_dw")))
    dup_g, dup_v, dcw, g["ffn_conv_b"] = _ffn_conv_bwd(
        sv["up_g"], sv["up_v"], wl["ffn_conv_w"], sp["ffn_conv_b"], dact, name=n("ffn_conv"))
    g["ffn_conv_w"] = _cols_by_chip(dcw, F32)
    nsh = UP_SHARD_COLS
    dhf = None
    for c4 in range(N_CHIPS):
        dhf = _mm(dup_g if c4 < 2 else dup_v, gw["w_up"], tb=True, a_col=(nsh, c4 % 2), b_lead=(c4,), res=dhf,
                  name=n(f"up{c4}_dx"))
    g["w_up"] = jnp.concatenate([_mm(sv["hf"], dup_g, ta=True, o_chips=nsh, out_dtype=BF16, name=n("upg_dw")),
                                 _mm(sv["hf"], dup_v, ta=True, o_chips=nsh, out_dtype=BF16, name=n("upv_dw"))], axis=0)
    dx2, dx2b, g["norm_ffn"] = _rms_bwd(sv["x2"], after(emit("ffn", g), sp["norm_ffn"]), dhf, dx3, name=n("ffn_norm"))
    dmo = _mm(dx2b, gw["w_mo"], tb=True, b_lead=lead, out_dtype=BF16, name=n("mo_dx"))
    g["w_mo"] = _rows_by_chip(_mm(sv["mo"], dx2b, ta=True, out_dtype=BF16, name=n("mo_dw")))
    dmq, dmk, dmv = _mem_attn_bwd(sv["mq"], sv["mk"], sv["mv"], dmo, name=n("mem_attn"))
    dhq = _mm(dmq, gw["w_mq"], tb=True, b_lead=lead, name=n("mq_dx"))
    g["w_mq"] = _rows_by_chip(_mm(sv["hq"], dmq, ta=True, out_dtype=BF16, name=n("mq_dw")))
    dhm = _mm(dmk, gw["w_mk"], tb=True, b_lead=lead, name=n("mk_dx"))
    dhm = _mm(dmv, gw["w_mv"], tb=True, b_lead=lead, res=dhm, name=n("mv_dx"))
    g["w_mk"] = _rows_by_chip(_mm(sv["hm"], dmk, ta=True, out_dtype=BF16, name=n("mk_dw")))
    g["w_mv"] = _rows_by_chip(_mm(sv["hm"], dmv, ta=True, out_dtype=BF16, name=n("mv_dw")))
    dx1, dx1b, g["norm_mem_q"] = _rms_bwd(sv["x1"], after(emit("mem", g), sp["norm_mem_q"]), dhq, dx2, name=n("memq_norm"))
    _, _, g["norm_mem_kv"] = _rms_bwd(mem, sp["norm_mem_kv"], dhm, name=n("memkv_norm"))
    dy_ssm = _mm(dx1b, gw["w_out"], tb=True, b_lead=lead, b_rows=(0, D_SSM), name=n("outa_dx"))
    dy_att = _mm(dx1b, gw["w_out"], tb=True, b_lead=lead, b_rows=(D_SSM, D_SSM), name=n("outb_dx"))
    g["w_out"] = _rows_by_chip(jnp.concatenate([_mm(sv["y_ssm"], dx1b, ta=True, out_dtype=BF16, name=n("outa_dw")),
                                                _mm(sv["y_att"], dx1b, ta=True, out_dtype=BF16, name=n("outb_dw"))], axis=0))
    datt, _, g["attn_out_norm"] = _rms_bwd(sv["att"], sp["attn_out_norm"], dy_att, name=n("att_norm"))
    dqr, dkr, dv = _flash_bwd(sv["qr"], sv["kr"], sv["v"], sv["att"], sv["lse"], datt, name=n("flash"))
    dq, dkv, dkrope = _mla_prep_bwd(dqr, dkr, dv, cos, sins, name=n("rope"))
    duq = _mm(sv["cqn"], dq, ta=True, name=n("uq_dw")).reshape(Q_LORA, MLA_HEADS, HEAD_PAD)[..., :QK_NOPE + QK_ROPE]
    g["w_uq"] = _cols_by_chip(duq.reshape(Q_LORA, -1), BF16)
    dcqn = _mm(dq, wl["w_uq"], tb=True, name=n("uq_dx"))
    g["w_ukv"] = _cols_by_chip(_mm(sv["ckvn"], dkv, ta=True, name=n("ukv_dw")), BF16)
    dckvn = _mm(dkv, wl["w_ukv"], tb=True, name=n("ukv_dx"))
    proj = sv["proj"]
    _, dcq, g["q_norm"] = _rms_bwd(proj, sp["q_norm"], dcqn, col=(Q_LORA, P_CQ // Q_LORA), name=n("q_norm"))
    _, dckv, g["kv_norm"] = _rms_bwd(proj, sp["kv_norm"], dckvn, col=(KV_LORA, P_CKV // KV_LORA), name=n("kv_norm"))
    dy, dz, g["ssm_norm"] = _gated_rms_bwd(sv["y"], proj, sp["ssm_norm"], dy_ssm, name=n("ssm_gate"))
    dxbc, ddt, da_log, dd_skip, ddt_bias = _ssd_bwd(
        sv["xbc"], proj, sp["dt_bias"], sp["a_log"], sp["d_skip"], sv["pstates"], dy, name=n("ssd"))
    g["a_log"], g["d_skip"], g["dt_bias"] = da_log[0, :SSM_HEADS], dd_skip[0, :SSM_HEADS], ddt_bias[0, :SSM_HEADS]
    dxbc_pre, dsw, g["ssm_conv_b"] = _ssm_conv_bwd(proj, wl["ssm_conv_w"], sp["ssm_conv_b"], dxbc, name=n("ssm_conv"))
    g["ssm_conv_w"] = _cols_by_chip(dsw, F32)
    s = proj.shape[0]
    dproj = jnp.concatenate([dxbc_pre, dz, dcq, ddt, dckv, dkrope,
                             jnp.zeros((s, P_IN - P_KR - LANES), BF16)], axis=1)
    dh = _mm(dproj, wl["w_in"], tb=True, name=n("proj_dx"))
    g["w_in"] = _cols_by_chip(_w_in_from_padded(_mm(sv["h"], dproj, ta=True, name=n("proj_dw"))), BF16)
    dx0, dx0b, g["norm_mix"] = _rms_bwd(sv["x0"], sp["norm_mix"], dh, dx1, name=n("mix_norm"))
    return dx0, dx0b, g, emit("mixer", g)


def _chip_peers(x, y):
    return [(1 - x, y), (x, 1 - y), (1 - x, 1 - y)]


HBM_SPEC = pl.BlockSpec(memory_space=pltpu.HBM)
SEM_SPEC = pl.BlockSpec(memory_space=pltpu.SEMAPHORE)
ANY_SPEC = pl.BlockSpec(memory_space=pl.ANY)
VMEM_SPEC = pl.BlockSpec(memory_space=pltpu.VMEM)
DATAFLOW = pltpu.SideEffectType.DATAFLOW_SIDE_EFFECTING
TOKEN_SHAPE = (8, LANES)


def _exchange_start(srcs, land_shapes, src_view, dst_view, token, *, name):
    n = len(srcs)

    def body(*refs):
        s, l, tok_in = refs[:n], refs[n:2 * n], refs[2 * n]
        send_sems, recv_sems = refs[2 * n + 1], refs[2 * n + 2]
        tok_out = refs[-1]
        x, y, c = lax.axis_index("x"), lax.axis_index("y"), lax.axis_index("c")
        me = 2 * x + y
        for t in range(n):
            for k, (px, py) in enumerate(_chip_peers(x, y)):
                pltpu.make_async_remote_copy(
                    src_ref=src_view(t, s[t], 2 * px + py), dst_ref=dst_view(t, l[t], me), send_sem=send_sems.at[3 * t + k],
                    recv_sem=recv_sems.at[3 * t + k], device_id=(px, py, c), device_id_type=MESH).start()
            pltpu.make_async_copy(src_view(t, s[t], me), dst_view(t, l[t], me), send_sems.at[3 * n + t]).start()
        tok_out[...] = tok_in[...]

    hbm = lambda t: pltpu.with_memory_space_constraint(t, pltpu.HBM)
    lands = [lax.empty(l.shape, l.dtype) for l in land_shapes]
    outs = pl.pallas_call(
        body, name=name,
        out_shape=(pltpu.SemaphoreType.DMA((4 * n,)), pltpu.SemaphoreType.DMA((3 * n,)),
                   *[pltpu.HBM(l.shape, l.dtype) for l in land_shapes], SDS(TOKEN_SHAPE, F32)),
        in_specs=[HBM_SPEC] * (2 * n) + [VMEM_SPEC], out_specs=(SEM_SPEC, SEM_SPEC, *[HBM_SPEC] * n, VMEM_SPEC),
        input_output_aliases={n + t: 2 + t for t in range(n)},
        compiler_params=pltpu.CompilerParams(has_side_effects=DATAFLOW))(*[hbm(t) for t in srcs], *[hbm(t) for t in lands], token)
    return outs[0], outs[1], list(outs[2:2 + n]), outs[-1]


def _exchange_wait(srcs, lands, send_sems, recv_sems, after, src_view, dst_view, which, *, name):
    n = len(srcs)
    m = len(which)

    def body(*refs):
        s, l = refs[:m], refs[m:2 * m]
        send_ref, recv_ref = refs[2 * m], refs[2 * m + 1]
        x, y, c = lax.axis_index("x"), lax.axis_index("y"), lax.axis_index("c")
        me = 2 * x + y
        for i, t in enumerate(which):
            for k, (px, py) in enumerate(_chip_peers(x, y)):
                chip = 2 * px + py
                cp = pltpu.make_async_remote_copy(
                    src_ref=src_view(t, s[i], chip), dst_ref=dst_view(t, l[i], chip), send_sem=send_ref.at[3 * t + k],
                    recv_sem=recv_ref.at[3 * t + k], device_id=(px, py, c), device_id_type=MESH)
                cp.wait_send()
                cp.wait_recv()
            pltpu.make_async_copy(src_view(t, s[i], me), dst_view(t, l[i], me), send_ref.at[3 * n + t]).wait()

    outs = pl.pallas_call(
        body, name=name, out_shape=[pltpu.HBM(lands[t].shape, lands[t].dtype) for t in which],
        in_specs=[HBM_SPEC] * (2 * m) + [SEM_SPEC, SEM_SPEC, ANY_SPEC], out_specs=[HBM_SPEC] * m,
        input_output_aliases={m + i: i for i in range(m)},
        compiler_params=pltpu.CompilerParams(has_side_effects=DATAFLOW))(
            *[srcs[t] for t in which], *[lands[t] for t in which], send_sems, recv_sems, after)
    return list(outs)


def _gather_layer_start(shards, li, token):
    src_view = lambda t, ref, chip: ref.at[li]
    dst_view = lambda t, ref, chip: ref.at[chip]
    send_sems, recv_sems, lands, token = _exchange_start(
        shards, [SDS((N_CHIPS,) + s.shape[1:], s.dtype) for s in shards], src_view, dst_view, token, name=f"gather{li}_start")
    return (shards, lands, send_sems, recv_sems, src_view, dst_view, f"gather{li}"), token


def _scatter_start(grads, tag, token):
    view = lambda t, ref, chip: ref.at[chip]
    send_sems, recv_sems, lands, token = _exchange_start(
        grads, [SDS(g.shape, g.dtype) for g in grads], view, view, token, name=f"scatter{tag}_start")
    return (grads, lands, send_sems, recv_sems, view, view, f"scatter{tag}"), token


def _exchange_finish(handle, after, which=None, tag=""):
    srcs, lands, send_sems, recv_sems, src_view, dst_view, name = handle
    which = tuple(range(len(srcs))) if which is None else which
    return _exchange_wait(srcs, lands, send_sems, recv_sems, after, src_view, dst_view, which, name=f"{name}{tag}_wait")


def _swap_cores(bufs, *, name):
    n = len(bufs)

    def body(*refs):
        srcs, outs = refs[:n], refs[n:2 * n]
        send_sems, recv_sems = refs[2 * n:]
        x, y, c = lax.axis_index("x"), lax.axis_index("y"), lax.axis_index("c")
        cps = [pltpu.make_async_remote_copy(src_ref=srcs[t], dst_ref=outs[t], send_sem=send_sems.at[t], recv_sem=recv_sems.at[t],
                                            device_id=(x, y, 1 - c), device_id_type=MESH) for t in range(n)]
        for cp in cps:
            cp.start()
        for cp in cps:
            cp.wait()

    any_spec = pl.BlockSpec(memory_space=pl.ANY)
    return pl.pallas_call(body, in_specs=[any_spec] * n, out_specs=[any_spec] * n,
                          out_shape=[SDS(b.shape, b.dtype) for b in bufs],
                          scratch_shapes=[pltpu.SemaphoreType.DMA((n,)), pltpu.SemaphoreType.DMA((n,))], name=name)(*bufs)


def _all_gather8(src, *, name):
    def body(src_ref, out_ref, send_sems, recv_sems, local_sem):
        x, y, c = lax.axis_index("x"), lax.axis_index("y"), lax.axis_index("c")
        me = 4 * x + 2 * y + c
        mine = pltpu.make_async_copy(src_ref, out_ref.at[me], local_sem)
        mine.start()

        def peer(k):
            return (x ^ (k >> 2 & 1), y ^ (k >> 1 & 1), c ^ (k & 1))

        sends = []
        for k in range(1, N_DEV):
            cp = pltpu.make_async_remote_copy(src_ref=src_ref, dst_ref=out_ref.at[me], send_sem=send_sems.at[k - 1],
                                              recv_sem=recv_sems.at[k - 1], device_id=peer(k), device_id_type=MESH)
            cp.start()
            sends.append(cp)
        for k in range(1, N_DEV):
            px, py, pc = peer(k)
            pltpu.make_async_remote_copy(src_ref=src_ref, dst_ref=out_ref.at[4 * px + 2 * py + pc],
                                         send_sem=send_sems.at[k - 1], recv_sem=recv_sems.at[k - 1],
                                         device_id=peer(k), device_id_type=MESH).wait_recv()
        for cp in sends:
            cp.wait_send()
        mine.wait()

    any_spec = pl.BlockSpec(memory_space=pl.ANY)
    return pl.pallas_call(
        body, in_specs=[any_spec], out_specs=any_spec, out_shape=SDS((N_DEV,) + src.shape, src.dtype),
        scratch_shapes=[pltpu.SemaphoreType.DMA((N_DEV - 1,)), pltpu.SemaphoreType.DMA((N_DEV - 1,)), pltpu.SemaphoreType.DMA],
        name=name)(src)


def _adam_terms(w, g, m, v):
    m = ADAM_B1 * m + (1.0 - ADAM_B1) * g
    v = ADAM_B2 * v + (1.0 - ADAM_B2) * (g * g)
    m_hat = m / (1.0 - ADAM_B1 ** ADAM_STEP)
    v_hat = v / (1.0 - ADAM_B2 ** ADAM_STEP)
    delta = -ADAM_LR * (m_hat / (jnp.sqrt(v_hat) + ADAM_EPS) + ADAM_WD * w)
    return delta, m, v


def _adamw_shard(mine, other, w, m, v, *, name):
    d, a, b = w.shape
    tr = next((t for t in (128, 64, 32, 16) if a % t == 0), a)

    def body(*refs):
        ga, gb = refs[:d], refs[d:2 * d]
        w_ref, m_ref, v_ref, g_ref, d_ref, nm_ref, nv_ref = refs[2 * d:]

        def plane(ref):
            return ((ref[0].astype(F32) + ref[1].astype(F32)) + ref[2].astype(F32)) + ref[3].astype(F32)

        for lp in range(d):
            @pl.when(pl.program_id(0) == lp)
            def _(lp=lp):
                g = plane(ga[lp]) + plane(gb[lp])
                delta, mn, vn = _adam_terms(w_ref[...], g, m_ref[...], v_ref[...])
                g_ref[...] = g
                d_ref[...] = delta
                nm_ref[...] = mn
                nv_ref[...] = vn

    gspecs = [pl.BlockSpec((N_CHIPS, tr, b), lambda l, i, lp=lp: (0, jnp.where(l == lp, i, 0), 0)) for lp in range(d)]
    blk = pl.BlockSpec((None, tr, b), lambda l, i: (l, i, 0))
    shp = SDS((d, a, b), F32)
    return pl.pallas_call(
        body, grid=(d, a // tr), in_specs=gspecs + gspecs + [blk, blk, blk], out_specs=(blk,) * 4, out_shape=(shp,) * 4,
        name=name, compiler_params=_params(("arbitrary", "arbitrary"), 48 << 20))(*mine, *other, w, m, v)


def _adamw_small(g8, w, m, v, *, name):
    n = w.shape[1]

    def body(g8_ref, w_ref, m_ref, v_ref, g_ref, d_ref, nm_ref, nv_ref):
        g = g8_ref[0]
        for k in range(1, N_DEV):
            g = g + g8_ref[k]
        delta, mn, vn = _adam_terms(w_ref[...], g, m_ref[...], v_ref[...])
        g_ref[...] = g
        d_ref[...] = delta
        nm_ref[...] = mn
        nv_ref[...] = vn

    shp = SDS((1, n), F32)
    return pl.pallas_call(body, out_shape=(shp,) * 4, name=name, compiler_params=_params(None, 24 << 20))(g8, w, m, v)


def _rope_tables(positions):
    inv_freq = 1.0 / (ROPE_THETA ** (jnp.arange(0, QK_ROPE, 2, dtype=F32) / QK_ROPE))
    ang = positions.astype(F32)[:, None] * inv_freq
    c, s = jnp.cos(ang), jnp.sin(ang)
    n = positions.shape[0]
    pad = jnp.zeros((n, HEAD_PAD - QK_NOPE - QK_ROPE), F32)
    cos = jnp.concatenate([jnp.ones((n, QK_NOPE), F32), c, c, pad], axis=1)
    sins = jnp.concatenate([jnp.zeros((n, QK_NOPE), F32), -s, s, pad], axis=1)
    return cos, sins


def _pad_lanes(v):
    return _pad_cols(v.reshape(1, -1), LANES)


def _local_step(x, mem, positions, weights, small, final_norm, loss_target, emit, token):
    cos, sins = _rope_tables(positions)
    saved, gws, wls, sps = [], [], [], []
    h = x
    for li in range(DEPTH):
        sp = {k: small[k][li] for k, _ in SMALL}
        if li == 0:
            sp["norm_mix"] = sp["norm_mix"] + token[0, 0]
        for k in ("dt_bias", "a_log", "d_skip"):
            sp[k] = _pad_lanes(sp[k])
        h, sv, gw, wl = _layer_fwd(h, mem, cos, sins, functools.partial(weights, li), sp, li)
        saved.append(sv)
        gws.append(gw)
        wls.append(wl)
        sps.append(sp)
    loss, dh, dhb, g_final = _final_loss(h, final_norm, loss_target, name="final_loss")
    grads = [None] * DEPTH
    started = None
    for li in reversed(range(DEPTH)):
        sp = sps[li]
        if started is not None:
            sp = dict(sp, ffn_conv_b=sp["ffn_conv_b"] + started[0, 0])
        dh, dhb, grads[li], started = _layer_bwd(dh, dhb, mem, cos, sins, gws[li], wls[li], sp, saved[li], li,
                                                 functools.partial(emit, li))
    return loss, dh, grads, g_final


def _gathered_views(which, lands):
    return {MATS[t][0]: (b.reshape(-1, b.shape[-1]) if MATS[t][2] == 0 else b) for t, b in zip(which, lands)}


def kernel(x, mem, positions, norm_mix, w_in, ssm_conv_w, ssm_conv_b, dt_bias, a_log, d_skip, ssm_norm, q_norm, w_uq, kv_norm, w_ukv, attn_out_norm, w_out, norm_mem_q, norm_mem_kv, w_mq, w_mk, w_mv, w_mo, norm_ffn, w_up, ffn_conv_w, ffn_conv_b, w_down, final_norm, loss_target, m_norm_mix, m_w_in, m_ssm_conv_w, m_ssm_conv_b, m_dt_bias, m_a_log, m_d_skip, m_ssm_norm, m_q_norm, m_w_uq, m_kv_norm, m_w_ukv, m_attn_out_norm, m_w_out, m_norm_mem_q, m_norm_mem_kv, m_w_mq, m_w_mk, m_w_mv, m_w_mo, m_norm_ffn, m_w_up, m_ffn_conv_w, m_ffn_conv_b, m_w_down, m_final_norm, v_norm_mix, v_w_in, v_ssm_conv_w, v_ssm_conv_b, v_dt_bias, v_a_log, v_d_skip, v_ssm_norm, v_q_norm, v_w_uq, v_kv_norm, v_w_ukv, v_attn_out_norm, v_w_out, v_norm_mem_q, v_norm_mem_kv, v_w_mq, v_w_mk, v_w_mv, v_w_mo, v_norm_ffn, v_w_up, v_ffn_conv_w, v_ffn_conv_b, v_w_down, v_final_norm):
    args = dict(locals())
    names = ["norm_mix", "w_in", "ssm_conv_w", "ssm_conv_b", "dt_bias", "a_log", "d_skip", "ssm_norm", "q_norm", "w_uq",
             "kv_norm", "w_ukv", "attn_out_norm", "w_out", "norm_mem_q", "norm_mem_kv", "w_mq", "w_mk", "w_mv", "w_mo",
             "norm_ffn", "w_up", "ffn_conv_w", "ffn_conv_b", "w_down", "final_norm"]
    wts = {k: args[k] for k in names}
    mom = {k: args["m_" + k] for k in names}
    var = {k: args["v_" + k] for k in names}
    mat_names = [k for k, _, _ in MATS]

    shards = [wts[k] if k in F32_ON_WIRE else wts[k].astype(BF16) for k in mat_names]
    token = jnp.zeros(TOKEN_SHAPE, F32)
    gathers = []
    for li in range(DEPTH):
        handle, token = _gather_layer_start(shards, li, token)
        gathers.append(handle)
    small = {k: wts[k] for k, _ in SMALL}

    def weights(li, group, after):
        which = GROUPS[group]
        return _gathered_views(which, _exchange_finish(gathers[li], after, which, tag=f"_{group}"))

    scatters = [[] for _ in range(DEPTH)]

    def emit(li, group, g):
        if li == 0:
            which = GROUPS[group]
        elif group == "mixer":
            which = tuple(range(len(MATS)))
        else:
            return None
        handle, started = _scatter_start([g[MATS[t][0]] for t in which], f"{li}_{group}", jnp.zeros(TOKEN_SHAPE, F32))
        scatters[li].append((which, handle))
        return started

    loss, grad_x, grads, g_final = _local_step(x[0], mem[0], positions[0], weights, small, wts["final_norm"],
                                               loss_target[0], emit, token)
    loss = lax.psum(loss, ("x", "y", "c"))

    nm = len(mat_names)
    mine = [[None] * nm for _ in range(DEPTH)]
    for li in range(DEPTH):
        for which, handle in scatters[li]:
            for t, b in zip(which, _exchange_finish(handle, grad_x)):
                mine[li][t] = b
    swapped = _swap_cores([b for layer in mine for b in layer], name="swap_cores")
    other = [swapped[li * nm:(li + 1) * nm] for li in range(DEPTH)]
    mat_out = {k: _adamw_shard([mine[li][t] for li in range(DEPTH)], [other[li][t] for li in range(DEPTH)],
                               wts[k], mom[k], var[k], name=f"adamw_{k}") for t, k in enumerate(mat_names)}

    def pack_small(get, fin):
        flat = [get(k).reshape(-1) for k, _ in SMALL] + [fin.reshape(-1)]
        n = sum(f.shape[0] for f in flat)
        return jnp.concatenate(flat + [jnp.zeros((-n % PACK_COLS,), F32)]).reshape(1, -1)

    gs = pack_small(lambda k: jnp.stack([grads[li][k] for li in range(DEPTH)]), g_final)
    g8 = _all_gather8(gs, name="gather_small_grads")
    small_out = _adamw_small(g8, pack_small(wts.get, wts["final_norm"]), pack_small(mom.get, mom["final_norm"]),
                             pack_small(var.get, var["final_norm"]), name="adamw_small")

    def unpack_small(buf):
        out, off = {}, 0
        for k, nel in SMALL:
            out[k] = buf[0, off:off + DEPTH * nel].reshape(DEPTH, nel)
            off += DEPTH * nel
        out["final_norm"] = buf[0, off:off + D_MODEL]
        return out

    small_res = [unpack_small(b) for b in small_out]
    res = []
    for kind in range(4):
        for k in names:
            res.append(small_res[kind][k] if k in small_res[kind] else mat_out[k][kind])
    return (loss, grad_x[None], *res)
```

```python
import functools
import math

import jax
import jax.numpy as jnp
from jax import lax
from jax.experimental import pallas as pl
from jax.experimental.pallas import tpu as pltpu

F32 = jnp.float32
BF16 = jnp.bfloat16
HIGHEST = lax.Precision.HIGHEST
SDS = jax.ShapeDtypeStruct
MESH = pl.DeviceIdType.MESH

D_MODEL = 1024
DEPTH = 4
EPS = 1e-6
SSM_HEADS = 16
SSM_HEAD_DIM = 64
D_SSM = 1024
SSM_GROUPS = 4
SSM_STATE = 128
SSM_CONV = 4
SSM_CHUNK = 128
CONV_CH = 2048
MLA_HEADS = 16
QK_NOPE = 64
QK_ROPE = 32
V_DIM = 64
Q_LORA = 384
KV_LORA = 256
ROPE_THETA = 10000.0
MEM_HEADS = 4
MEM_HEAD_DIM = 256
D_FF = 2816
FFN_CONV = 3
D_IN = 3760
ADAM_LR = 0.001
ADAM_B1 = 0.9
ADAM_B2 = 0.999
ADAM_EPS = 1e-08
ADAM_WD = 0.01
ADAM_STEP = 10

LANES = 128
HEAD_PAD = 128
N_CHIPS = 4
N_DEV = 8
VMEM_CAP_MB = 56

P_XBC, P_Z, P_CQ, P_DT, P_CKV, P_KR, P_IN = 0, 2048, 3072, 3456, 3584, 3840, 4096
NEG = -1e30


def _tile(n, pref):
    t = (min(n, pref) // LANES) * LANES
    while t >= LANES:
        if n % t == 0:
            return t
        t -= LANES
    return n


def _params(sem=None, vmem_bytes=None):
    kw = {}
    if sem is not None:
        kw["dimension_semantics"] = sem
    if vmem_bytes is not None:
        kw["vmem_limit_bytes"] = int(min(max(vmem_bytes, 16 << 20), VMEM_CAP_MB << 20))
    return pltpu.CompilerParams(**kw)


def _nbytes(shape, dtype):
    return math.prod(shape) * jnp.dtype(dtype).itemsize


def _mm(a, b, *, ta=False, tb=False, res=None, out_dtype=F32, name, a_col=None, b_lead=(), b_rows=None,
        b_chips=None, o_chips=None):
    if ta:
        k, m = a.shape
    else:
        m, k = (a.shape[0], a.shape[1] if a_col is None else a_col[0])
    rows_b, cols_b = b.shape[-2:]
    row0 = 0
    if b_rows is not None:
        row0, rows_b = b_rows
    nlead = len(b_lead)
    if b_chips is not None:
        assert not tb
        kb, tn, n = rows_b, cols_b, b_chips[1] * cols_b
        b_blk = (None,) * (1 + nlead) + (kb, tn)
        b_map = lambda i, j: (b_chips[0] + j,) + tuple(b_lead) + (0, 0)
    elif tb:
        n, kb = rows_b, cols_b
        tn = _tile(n, 512)
        assert row0 % tn == 0
        b_blk = (None,) * nlead + (tn, kb)
        b_map = lambda i, j: tuple(b_lead) + (j + row0 // tn, 0)
    else:
        kb, n = rows_b, cols_b
        tn = o_chips if o_chips else _tile(n, 512)
        assert row0 % kb == 0
        b_blk = (None,) * nlead + (kb, tn)
        b_map = lambda i, j: tuple(b_lead) + (row0 // kb, j)
    assert k == kb, (a.shape, b.shape, ta, tb, k, kb)
    tm = _tile(m, 512)
    if ta:
        a_blk, a_map = (k, tm), (lambda i, j: (0, i))
    else:
        a_blk, a_map = (tm, k), ((lambda i, j: (i, 0)) if a_col is None else (lambda i, j: (i, a_col[1])))
    if o_chips:
        o_spec = pl.BlockSpec((None, tm, tn), lambda i, j: (j, i, 0))
        o_shape = SDS((n // tn, m, tn), out_dtype)
    else:
        o_spec = pl.BlockSpec((tm, tn), lambda i, j: (i, j))
        o_shape = SDS((m, n), out_dtype)
    dims = (((0 if ta else 1,), (1 if tb else 0,)), ((), ()))
    has_res = res is not None

    def body(*refs):
        a_ref, b_ref = refs[0], refs[1]
        o_ref = refs[-1]
        acc = lax.dot_general(a_ref[...].astype(BF16), b_ref[...].astype(BF16), dims, preferred_element_type=F32)
        if has_res:
            acc = acc + refs[2][...]
        o_ref[...] = acc.astype(o_ref.dtype)

    bb = tuple(d for d in b_blk if d is not None)
    vmem = 2 * (_nbytes(a_blk, a.dtype) + _nbytes(bb, b.dtype) + (2 if has_res else 1) * _nbytes((tm, tn), F32))
    vmem += _nbytes(a_blk, BF16) + _nbytes(bb, BF16) + 2 * _nbytes((tm, tn), F32) + (4 << 20)
    args = (a, b) + ((res,) if has_res else ())
    specs = [pl.BlockSpec(a_blk, a_map), pl.BlockSpec(b_blk, b_map)] + ([o_spec] if has_res else [])
    return pl.pallas_call(body, grid=(m // tm, n // tn), in_specs=specs, out_specs=o_spec, out_shape=o_shape, name=name,
                          compiler_params=_params(("parallel", "parallel"), vmem))(*args)


def _sigmoid(x):
    return 1.0 / (1.0 + jnp.exp(-x))


def _rms_fwd(x, g, *, col=None, name):
    s = x.shape[0]
    w, ci = (x.shape[1], 0) if col is None else col
    tm = min(s, 512)

    def body(x_ref, g_ref, o_ref):
        xv = x_ref[...].astype(F32)
        r = lax.rsqrt(jnp.mean(xv * xv, axis=-1, keepdims=True) + EPS)
        o_ref[...] = (xv * r * g_ref[...]).astype(o_ref.dtype)

    return pl.pallas_call(
        body, grid=(s // tm,),
        in_specs=[pl.BlockSpec((tm, w), lambda i: (i, ci)), pl.BlockSpec((1, w), lambda i: (0, 0))],
        out_specs=pl.BlockSpec((tm, w), lambda i: (i, 0)), out_shape=SDS((s, w), BF16), name=name,
        compiler_params=_params(("parallel",), 10 * tm * w * 4))(x, g.reshape(1, w))


def _rms_bwd(x, g, dy, dres=None, *, col=None, name):
    s = x.shape[0]
    w, ci = (x.shape[1], 0) if col is None else col
    tm = min(s, 512)
    has_res = dres is not None

    def body(*refs):
        x_ref, g_ref, dy_ref = refs[:3]
        dx_ref, dxb_ref, dg_ref = refs[-3:]
        xv = x_ref[...].astype(F32)
        dyv = dy_ref[...].astype(F32)
        r = lax.rsqrt(jnp.mean(xv * xv, axis=-1, keepdims=True) + EPS)
        u = dyv * g_ref[...]
        dx = r * u - xv * (r * r * r) * jnp.mean(xv * u, axis=-1, keepdims=True)
        if has_res:
            dx = dx + refs[3][...]
        dx_ref[...] = dx
        dxb_ref[...] = dx.astype(BF16)

        @pl.when(pl.program_id(0) == 0)
        def _():
            dg_ref[...] = jnp.zeros_like(dg_ref)

        dg_ref[...] += jnp.sum(dyv * xv * r, axis=0, keepdims=True)

    blk = pl.BlockSpec((tm, w), lambda i: (i, 0))
    specs = [pl.BlockSpec((tm, w), lambda i: (i, ci)), pl.BlockSpec((1, w), lambda i: (0, 0)), blk]
    args = [x, g.reshape(1, w), dy]
    if has_res:
        specs.append(blk)
        args.append(dres)
    dx, dxb, dg = pl.pallas_call(
        body, grid=(s // tm,), in_specs=specs,
        out_specs=(blk, blk, pl.BlockSpec((1, w), lambda i: (0, 0))),
        out_shape=(SDS((s, w), F32), SDS((s, w), BF16), SDS((1, w), F32)), name=name,
        compiler_params=_params(("arbitrary",), 18 * tm * w * 4))(*args)
    return dx, dxb, dg.reshape(w)


def _gated_rms_fwd(y, proj, g, *, name):
    s, w = y.shape
    tm = min(s, 512)

    def body(y_ref, z_ref, g_ref, o_ref):
        z = z_ref[...]
        t = y_ref[...] * (z * _sigmoid(z))
        r = lax.rsqrt(jnp.mean(t * t, axis=-1, keepdims=True) + EPS)
        o_ref[...] = (t * r * g_ref[...]).astype(o_ref.dtype)

    blk = pl.BlockSpec((tm, w), lambda i: (i, 0))
    return pl.pallas_call(
        body, grid=(s // tm,),
        in_specs=[blk, pl.BlockSpec((tm, w), lambda i: (i, P_Z // w)), pl.BlockSpec((1, w), lambda i: (0, 0))],
        out_specs=blk, out_shape=SDS((s, w), BF16), name=name,
        compiler_params=_params(("parallel",), 14 * tm * w * 4))(y, proj, g.reshape(1, w))


def _gated_rms_bwd(y, proj, g, dout, *, name):
    s, w = y.shape
    tm = min(s, 512)

    def body(y_ref, z_ref, g_ref, do_ref, dy_ref, dz_ref, dg_ref):
        z = z_ref[...]
        yv = y_ref[...]
        dov = do_ref[...]
        sg = _sigmoid(z)
        sz = z * sg
        t = yv * sz
        r = lax.rsqrt(jnp.mean(t * t, axis=-1, keepdims=True) + EPS)
        u = dov * g_ref[...]
        dt = r * u - t * (r * r * r) * jnp.mean(t * u, axis=-1, keepdims=True)
        dy_ref[...] = dt * sz
        dz_ref[...] = (dt * yv * (sg * (1.0 + z * (1.0 - sg)))).astype(dz_ref.dtype)

        @pl.when(pl.program_id(0) == 0)
        def _():
            dg_ref[...] = jnp.zeros_like(dg_ref)

        dg_ref[...] += jnp.sum(dov * t * r, axis=0, keepdims=True)

    blk = pl.BlockSpec((tm, w), lambda i: (i, 0))
    vec = pl.BlockSpec((1, w), lambda i: (0, 0))
    dy, dz, dg = pl.pallas_call(
        body, grid=(s // tm,),
        in_specs=[blk, pl.BlockSpec((tm, w), lambda i: (i, P_Z // w)), vec, blk],
        out_specs=(blk, blk, vec), out_shape=(SDS((s, w), F32), SDS((s, w), BF16), SDS((1, w), F32)), name=name,
        compiler_params=_params(("arbitrary",), 24 * tm * w * 4))(y, proj, g.reshape(1, w), dout)
    return dy, dz, dg.reshape(w)


def _final_loss(x, g, target, *, name):
    s, w = x.shape
    tm = min(s, 512)

    def body(x_ref, g_ref, t_ref, loss_ref, dx_ref, dxb_ref, dg_ref):
        xv = x_ref[...]
        gv = g_ref[...]
        r = lax.rsqrt(jnp.mean(xv * xv, axis=-1, keepdims=True) + EPS)
        xn = xv * r
        diff = xn * gv - t_ref[...]
        dy = diff * (1.0 / w)
        u = dy * gv
        dx = r * u - xv * (r * r * r) * jnp.mean(xv * u, axis=-1, keepdims=True)
        dx_ref[...] = dx
        dxb_ref[...] = dx.astype(BF16)

        @pl.when(pl.program_id(0) == 0)
        def _():
            dg_ref[...] = jnp.zeros_like(dg_ref)
            loss_ref[...] = jnp.zeros_like(loss_ref)

        dg_ref[...] += jnp.sum(dy * xn, axis=0, keepdims=True)
        part = jnp.sum(jnp.sum(diff * diff, axis=1, keepdims=True), axis=0, keepdims=True) * (0.5 / w)
        loss_ref[...] += jnp.broadcast_to(part, loss_ref.shape)

    blk = pl.BlockSpec((tm, w), lambda i: (i, 0))
    vec = pl.BlockSpec((1, w), lambda i: (0, 0))
    loss, dx, dxb, dg = pl.pallas_call(
        body, grid=(s // tm,), in_specs=[blk, vec, blk],
        out_specs=(pl.BlockSpec((1, LANES), lambda i: (0, 0)), blk, blk, vec),
        out_shape=(SDS((1, LANES), F32), SDS((s, w), F32), SDS((s, w), BF16), SDS((1, w), F32)), name=name,
        compiler_params=_params(("arbitrary",), 18 * tm * w * 4))(x, g.reshape(1, w), target)
    return loss[0, 0], dx, dxb, dg.reshape(w)


def _shift_down(x, k):
    if k == 0:
        return x
    row = lax.broadcasted_iota(jnp.int32, x.shape, 0)
    return jnp.where(row < k, 0.0, pltpu.roll(x, k, axis=0))


def _shift_up(x, k):
    if k == 0:
        return x
    s = x.shape[0]
    row = lax.broadcasted_iota(jnp.int32, x.shape, 0)
    return jnp.where(row >= s - k, 0.0, pltpu.roll(x, s - k, axis=0))


def _conv_pre(x, w, b, kw):
    pre = b
    for j in range(kw):
        pre = pre + w[j:j + 1, :] * _shift_down(x, kw - 1 - j)
    return pre


def _conv_bwd_terms(x, w, dpre, kw):
    dx = jnp.zeros_like(x)
    dws = []
    for j in range(kw):
        dx = dx + w[j:j + 1, :] * _shift_up(dpre, kw - 1 - j)
        dws.append(jnp.sum(dpre * _shift_down(x, kw - 1 - j), axis=0, keepdims=True))
    return dx, jnp.concatenate(dws, axis=0), jnp.sum(dpre, axis=0, keepdims=True)


def _ssm_conv_fwd(proj, w, b, *, name):
    s = proj.shape[0]
    cw = 256

    def body(x_ref, w_ref, b_ref, o_ref):
        pre = _conv_pre(x_ref[...], w_ref[...], b_ref[...], SSM_CONV)
        o_ref[...] = pre * _sigmoid(pre)

    return pl.pallas_call(
        body, grid=(CONV_CH // cw,),
        in_specs=[pl.BlockSpec((s, cw), lambda j: (0, j)), pl.BlockSpec((SSM_CONV, cw), lambda j: (0, j)),
                  pl.BlockSpec((1, cw), lambda j: (0, j))],
        out_specs=pl.BlockSpec((s, cw), lambda j: (0, j)), out_shape=SDS((s, CONV_CH), F32), name=name,
        compiler_params=_params(("parallel",), 12 * s * cw * 4))(proj, w, b.reshape(1, CONV_CH))


def _ssm_conv_bwd(proj, w, b, dxbc, *, name):
    s = proj.shape[0]
    cw = 256

    def body(x_ref, w_ref, b_ref, dy_ref, dx_ref, dw_ref, db_ref):
        x = x_ref[...]
        wv = w_ref[...]
        pre = _conv_pre(x, wv, b_ref[...], SSM_CONV)
        sg = _sigmoid(pre)
        dpre = dy_ref[...] * (sg * (1.0 + pre * (1.0 - sg)))
        dx, dw, db = _conv_bwd_terms(x, wv, dpre, SSM_CONV)
        dx_ref[...] = dx.astype(dx_ref.dtype)
        dw_ref[...] = dw
        db_ref[...] = db

    col = pl.BlockSpec((s, cw), lambda j: (0, j))
    wsp = pl.BlockSpec((SSM_CONV, cw), lambda j: (0, j))
    bsp = pl.BlockSpec((1, cw), lambda j: (0, j))
    dx, dw, db = pl.pallas_call(
        body, grid=(CONV_CH // cw,), in_specs=[col, wsp, bsp, col], out_specs=(col, wsp, bsp),
        out_shape=(SDS((s, CONV_CH), BF16), SDS((SSM_CONV, CONV_CH), F32), SDS((1, CONV_CH), F32)), name=name,
        compiler_params=_params(("parallel",), 20 * s * cw * 4))(proj, w, b.reshape(1, CONV_CH), dxbc)
    return dx, dw, db.reshape(CONV_CH)


def _ffn_conv_fwd(up_g, up_v, w, b, *, name):
    s = up_g.shape[0]
    cw = 256
    nb = D_FF // cw

    def body(g_ref, v_ref, wg_ref, wv_ref, bg_ref, bv_ref, o_ref):
        gate = _conv_pre(g_ref[...], wg_ref[...], bg_ref[...], FFN_CONV)
        val = _conv_pre(v_ref[...], wv_ref[...], bv_ref[...], FFN_CONV)
        o_ref[...] = (gate * _sigmoid(gate) * val).astype(o_ref.dtype)

    col = pl.BlockSpec((s, cw), lambda j: (0, j))
    b2 = b.reshape(1, 2 * D_FF)
    return pl.pallas_call(
        body, grid=(nb,),
        in_specs=[col, col, pl.BlockSpec((FFN_CONV, cw), lambda j: (0, j)), pl.BlockSpec((FFN_CONV, cw), lambda j: (0, j + nb)),
                  pl.BlockSpec((1, cw), lambda j: (0, j)), pl.BlockSpec((1, cw), lambda j: (0, j + nb))],
        out_specs=col, out_shape=SDS((s, D_FF), BF16), name=name,
        compiler_params=_params(("parallel",), 16 * s * cw * 4))(up_g, up_v, w, w, b2, b2)


def _ffn_conv_bwd(up_g, up_v, w, b, dact, *, name):
    s = up_g.shape[0]
    cw = 256
    nb = D_FF // cw

    def body(g_ref, v_ref, wg_ref, wv_ref, bg_ref, bv_ref, da_ref, dg_ref, dv_ref, dwg_ref, dwv_ref, dbg_ref, dbv_ref):
        xg, xv = g_ref[...], v_ref[...]
        wg, wv = wg_ref[...], wv_ref[...]
        gate = _conv_pre(xg, wg, bg_ref[...], FFN_CONV)
        val = _conv_pre(xv, wv, bv_ref[...], FFN_CONV)
        da = da_ref[...].astype(F32)
        sg = _sigmoid(gate)
        dgate = da * val * (sg * (1.0 + gate * (1.0 - sg)))
        dval = da * gate * sg
        dxg, dwg, dbg = _conv_bwd_terms(xg, wg, dgate, FFN_CONV)
        dxv, dwv, dbv = _conv_bwd_terms(xv, wv, dval, FFN_CONV)
        dg_ref[...] = dxg.astype(dg_ref.dtype)
        dv_ref[...] = dxv.astype(dv_ref.dtype)
        dwg_ref[...] = dwg
        dwv_ref[...] = dwv
        dbg_ref[...] = dbg
        dbv_ref[...] = dbv

    col = pl.BlockSpec((s, cw), lambda j: (0, j))
    wsp = pl.BlockSpec((FFN_CONV, cw), lambda j: (0, j))
    bsp = pl.BlockSpec((1, cw), lambda j: (0, j))
    b2 = b.reshape(1, 2 * D_FF)
    dg, dv, dwg, dwv, dbg, dbv = pl.pallas_call(
        body, grid=(nb,),
        in_specs=[col, col, wsp, pl.BlockSpec((FFN_CONV, cw), lambda j: (0, j + nb)), bsp,
                  pl.BlockSpec((1, cw), lambda j: (0, j + nb)), col],
        out_specs=(col, col, wsp, wsp, bsp, bsp),
        out_shape=(SDS((s, D_FF), BF16), SDS((s, D_FF), BF16), SDS((FFN_CONV, D_FF), F32), SDS((FFN_CONV, D_FF), F32),
                   SDS((1, D_FF), F32), SDS((1, D_FF), F32)), name=name,
        compiler_params=_params(("parallel",), 32 * s * cw * 4))(up_g, up_v, w, w, b2, b2, dact)
    return dg, dv, jnp.concatenate([dwg, dwv], axis=1), jnp.concatenate([dbg, dbv], axis=1).reshape(2 * D_FF)


def _dot(a, b):
    return jnp.dot(a.astype(BF16), b.astype(BF16), preferred_element_type=F32)


def _dot_nt(a, b):
    return lax.dot_general(a.astype(BF16), b.astype(BF16), (((1,), (1,)), ((), ())), preferred_element_type=F32)


def _dot_tn(a, b):
    return lax.dot_general(a.astype(BF16), b.astype(BF16), (((0,), (0,)), ((), ())), preferred_element_type=F32)


def _ssd_chunk_terms(dtraw, bias, a_log):
    ell = dtraw.shape[0]
    lane = lax.broadcasted_iota(jnp.int32, dtraw.shape, 1)
    valid = lane < SSM_HEADS
    pre = dtraw + bias
    dt = jnp.where(valid, jnp.where(pre > 20.0, pre, jnp.log(1.0 + jnp.exp(jnp.minimum(pre, 20.0)))), 0.0)
    a = -jnp.exp(a_log)
    ad = dt * a
    row = lax.broadcasted_iota(jnp.int32, (ell, ell), 0)
    colm = lax.broadcasted_iota(jnp.int32, (ell, ell), 1)
    tril = row >= colm
    cs = jnp.dot(tril.astype(F32), ad, precision=HIGHEST, preferred_element_type=F32)
    cs_last = cs[ell - 1:ell, :]
    return pre, dt, a, cs, cs_last, tril


def _lane_put(col, h, shape):
    lane = lax.broadcasted_iota(jnp.int32, shape, 1)
    return jnp.where(lane == h, col, 0.0)


def _ssd_fwd(xbc, proj, dt_bias, a_log, d_skip, *, name):
    s = xbc.shape[0]
    nc = s // SSM_CHUNK
    ell, n, p = SSM_CHUNK, SSM_STATE, SSM_HEAD_DIM
    rpg = SSM_HEADS // SSM_GROUPS

    def body(x_ref, dt_ref, bias_ref, alog_ref, dskip_ref, y_ref, ps_ref, state):
        @pl.when(pl.program_id(0) == 0)
        def _():
            state[...] = jnp.zeros_like(state)

        _, dt, _, cs, cs_last, tril = _ssd_chunk_terms(dt_ref[...], bias_ref[...], alog_ref[...])
        e = jnp.exp(cs)
        ds = jnp.exp(cs_last - cs)
        cd = jnp.exp(cs_last)
        cst = cs.T
        dskip = dskip_ref[...]
        st = state[...]
        ps_ref[0] = st
        xv = x_ref[...]
        ys, new = [], []
        for g in range(SSM_GROUPS):
            bg = xv[:, D_SSM + n * g:D_SSM + n * (g + 1)]
            cg = xv[:, D_SSM + n * (SSM_GROUPS + g):D_SSM + n * (SSM_GROUPS + g + 1)]
            cb = _dot_nt(cg, bg)
            for r in range(rpg):
                h = g * rpg + r
                hs = slice(p * h, p * (h + 1))
                xs = xv[:, hs]
                xd = xs * dt[:, h:h + 1]
                lmat = jnp.exp(jnp.where(tril, cs[:, h:h + 1] - cst[h:h + 1, :], -jnp.inf))
                prev = st[:, hs]
                ys.append(_dot(cb * lmat, xd) + _dot(cg, prev) * e[:, h:h + 1] + xs * dskip[:, h:h + 1])
                new.append(prev * cd[:, h:h + 1] + _dot_tn(bg, xd * ds[:, h:h + 1]))
        y_ref[...] = jnp.concatenate(ys, axis=1)
        state[...] = jnp.concatenate(new, axis=1)

    vec = pl.BlockSpec((1, LANES), lambda c: (0, 0))
    return pl.pallas_call(
        body, grid=(nc,),
        in_specs=[pl.BlockSpec((ell, CONV_CH), lambda c: (c, 0)), pl.BlockSpec((ell, LANES), lambda c: (c, P_DT // LANES)),
                  vec, vec, vec],
        out_specs=(pl.BlockSpec((ell, D_SSM), lambda c: (c, 0)), pl.BlockSpec((1, n, D_SSM), lambda c: (c, 0, 0))),
        out_shape=(SDS((s, D_SSM), F32), SDS((nc, n, D_SSM), F32)),
        scratch_shapes=[pltpu.VMEM((n, D_SSM), F32)], name=name,
        compiler_params=_params(("arbitrary",), 24 << 20))(xbc, proj, dt_bias, a_log, d_skip)


def _ssd_bwd(xbc, proj, dt_bias, a_log, d_skip, prev_states, dy, *, name):
    s = xbc.shape[0]
    nc = s // SSM_CHUNK
    ell, n, p = SSM_CHUNK, SSM_STATE, SSM_HEAD_DIM
    rpg = SSM_HEADS // SSM_GROUPS

    def body(x_ref, dt_ref, bias_ref, alog_ref, dskip_ref, ps_ref, dy_ref,
             dx_ref, ddt_ref, dalog_ref, ddskip_ref, dbias_ref, dstate):
        @pl.when(pl.program_id(0) == 0)
        def _():
            dstate[...] = jnp.zeros_like(dstate)
            dalog_ref[...] = jnp.zeros_like(dalog_ref)
            ddskip_ref[...] = jnp.zeros_like(ddskip_ref)
            dbias_ref[...] = jnp.zeros_like(dbias_ref)

        pre, dt, a, cs, cs_last, tril = _ssd_chunk_terms(dt_ref[...], bias_ref[...], alog_ref[...])
        e = jnp.exp(cs)
        ds = jnp.exp(cs_last - cs)
        cd = jnp.exp(cs_last)
        cst = cs.T
        dskip = dskip_ref[...]
        shape = (ell, LANES)
        ddt_acc = jnp.zeros(shape, F32)
        dcs_acc = jnp.zeros(shape, F32)
        dcs_rows = jnp.zeros(shape, F32)
        dlast_acc = jnp.zeros((1, LANES), F32)
        dskip_acc = jnp.zeros((1, LANES), F32)
        xv, dyv, psv, dst = x_ref[...], dy_ref[...], ps_ref[0], dstate[...]
        dxs, dbs, dcs_parts, dprevs = [], [], [], []
        for g in range(SSM_GROUPS):
            bsl = slice(D_SSM + n * g, D_SSM + n * (g + 1))
            csl = slice(D_SSM + n * (SSM_GROUPS + g), D_SSM + n * (SSM_GROUPS + g + 1))
            bg = xv[:, bsl]
            cg = xv[:, csl]
            cb = _dot_nt(cg, bg)
            dcb = jnp.zeros((ell, ell), F32)
            dbg = jnp.zeros((ell, n), F32)
            dcg = jnp.zeros((ell, n), F32)
            for r in range(rpg):
                h = g * rpg + r
                hs = slice(p * h, p * (h + 1))
                xs = xv[:, hs]
                dyh = dyv[:, hs]
                dt_h, e_h, ds_h, cd_h = dt[:, h:h + 1], e[:, h:h + 1], ds[:, h:h + 1], cd[:, h:h + 1]
                prev = psv[:, hs]
                dsn = dst[:, hs]
                xd = xs * dt_h
                dye = dyh * e_h
                cprev = _dot(cg, prev)
                dprev = dsn * cd_h + _dot_tn(cg, dye)
                dcg = dcg + _dot_nt(dye, prev)
                dcs_h = jnp.sum(dyh * cprev, axis=1, keepdims=True) * e_h
                dcd = jnp.sum(jnp.sum(dsn * prev, axis=1, keepdims=True), axis=0, keepdims=True)
                dlast_h = dcd * cd_h
                dxdd = _dot(bg, dsn)
                dbg = dbg + _dot_nt(xd * ds_h, dsn)
                dxd = dxdd * ds_h
                tmp = jnp.sum(dxdd * xd, axis=1, keepdims=True) * ds_h
                dlast_h = dlast_h + jnp.sum(tmp, axis=0, keepdims=True)
                dcs_h = dcs_h - tmp
                lmat = jnp.exp(jnp.where(tril, cs[:, h:h + 1] - cst[h:h + 1, :], -jnp.inf))
                gm = cb * lmat
                dgm = _dot_nt(dyh, xd)
                dxd = dxd + _dot_tn(gm, dyh)
                mm = dgm * gm
                dcs_h = dcs_h + jnp.sum(mm, axis=1, keepdims=True)
                sub = lax.broadcasted_iota(jnp.int32, shape, 0)
                dcs_rows = dcs_rows + jnp.where(sub == h, jnp.sum(mm, axis=0, keepdims=True), 0.0)
                dcb = dcb + dgm * lmat
                dxs.append(dxd * dt_h + dyh * dskip[:, h:h + 1])
                ddt_acc = ddt_acc + _lane_put(jnp.sum(dxd * xs, axis=1, keepdims=True), h, shape)
                dcs_acc = dcs_acc + _lane_put(dcs_h, h, shape)
                dlast_acc = dlast_acc + _lane_put(dlast_h, h, (1, LANES))
                dskip_acc = dskip_acc + _lane_put(
                    jnp.sum(jnp.sum(dyh * xs, axis=1, keepdims=True), axis=0, keepdims=True), h, (1, LANES))
                dprevs.append(dprev)
            dbs.append(dbg + _dot_tn(dcb, cg))
            dcs_parts.append(dcg + _dot(dcb, bg))
        dx_ref[...] = jnp.concatenate(dxs + dbs + dcs_parts, axis=1)
        dstate[...] = jnp.concatenate(dprevs, axis=1)
        rowi = lax.broadcasted_iota(jnp.int32, shape, 0)
        dcs = dcs_acc - dcs_rows.T + jnp.where(rowi == ell - 1, dlast_acc, 0.0)
        triu = lax.broadcasted_iota(jnp.int32, (ell, ell), 0) <= lax.broadcasted_iota(jnp.int32, (ell, ell), 1)
        dad = jnp.dot(triu.astype(F32), dcs, precision=HIGHEST, preferred_element_type=F32)
        ddt = ddt_acc + dad * a
        dalog_ref[...] += jnp.sum(dad * dt, axis=0, keepdims=True) * a
        ddskip_ref[...] += dskip_acc
        lane = lax.broadcasted_iota(jnp.int32, shape, 1)
        ddraw = jnp.where(lane < SSM_HEADS, ddt * _sigmoid(pre), 0.0)
        ddt_ref[...] = ddraw.astype(ddt_ref.dtype)
        dbias_ref[...] += jnp.sum(ddraw, axis=0, keepdims=True)

    vec = pl.BlockSpec((1, LANES), lambda c: (0, 0))
    rev = lambda c: nc - 1 - c
    outs = pl.pallas_call(
        body, grid=(nc,),
        in_specs=[pl.BlockSpec((ell, CONV_CH), lambda c: (rev(c), 0)),
                  pl.BlockSpec((ell, LANES), lambda c: (rev(c), P_DT // LANES)), vec, vec, vec,
                  pl.BlockSpec((1, n, D_SSM), lambda c: (rev(c), 0, 0)),
                  pl.BlockSpec((ell, D_SSM), lambda c: (rev(c), 0))],
        out_specs=(pl.BlockSpec((ell, CONV_CH), lambda c: (rev(c), 0)), pl.BlockSpec((ell, LANES), lambda c: (rev(c), 0)),
                   vec, vec, vec),
        out_shape=(SDS((s, CONV_CH), F32), SDS((s, LANES), BF16), SDS((1, LANES), F32), SDS((1, LANES), F32),
                   SDS((1, LANES), F32)),
        scratch_shapes=[pltpu.VMEM((n, D_SSM), F32)], name=name,
        compiler_params=_params(("arbitrary",), 32 << 20))(xbc, proj, dt_bias, a_log, d_skip, prev_states, dy)
    return outs


def _rope_swap(t):
    lane = lax.broadcasted_iota(jnp.int32, t.shape, 1)
    half = QK_ROPE // 2
    lo = (lane >= QK_NOPE) & (lane < QK_NOPE + half)
    hi = (lane >= QK_NOPE + half) & (lane < QK_NOPE + QK_ROPE)
    return jnp.where(lo, pltpu.roll(t, HEAD_PAD - half, axis=1), jnp.where(hi, pltpu.roll(t, half, axis=1), 0.0))


def _mla_prep(q, kv, proj, cos, sins, *, name):
    s = q.shape[0]
    tm = min(s, 256)
    scale = (QK_NOPE + QK_ROPE) ** -0.5

    def body(q_ref, kv_ref, kr_ref, cos_ref, sin_ref, qo_ref, ko_ref, vo_ref):
        cosv, sinv = cos_ref[...], sin_ref[...]
        kr = pltpu.roll(kr_ref[...], QK_NOPE, axis=1)
        lane = lax.broadcasted_iota(jnp.int32, kr.shape, 1)
        nope = lane < QK_NOPE
        kr = jnp.where(nope, 0.0, kr)
        kpe = kr * cosv + _rope_swap(kr) * sinv
        for hp in range(MLA_HEADS // 2):
            vs = []
            for h in (2 * hp, 2 * hp + 1):
                hs = slice(HEAD_PAD * h, HEAD_PAD * (h + 1))
                qh = q_ref[:, hs]
                kvh = kv_ref[:, hs]
                qo_ref[:, hs] = ((qh * cosv + _rope_swap(qh) * sinv) * scale).astype(qo_ref.dtype)
                ko_ref[:, hs] = (jnp.where(nope, kvh, 0.0) + kpe).astype(ko_ref.dtype)
                vs.append(kvh[:, QK_NOPE:])
            vo_ref[:, 2 * V_DIM * hp:2 * V_DIM * (hp + 1)] = jnp.concatenate(vs, axis=1).astype(vo_ref.dtype)

    wide = pl.BlockSpec((tm, MLA_HEADS * HEAD_PAD), lambda i: (i, 0))
    half = pl.BlockSpec((tm, MLA_HEADS * V_DIM), lambda i: (i, 0))
    tab = pl.BlockSpec((tm, LANES), lambda i: (i, 0))
    return pl.pallas_call(
        body, grid=(s // tm,),
        in_specs=[wide, wide, pl.BlockSpec((tm, LANES), lambda i: (i, P_KR // LANES)), tab, tab],
        out_specs=(wide, wide, half),
        out_shape=(SDS((s, MLA_HEADS * HEAD_PAD), BF16), SDS((s, MLA_HEADS * HEAD_PAD), BF16),
                   SDS((s, MLA_HEADS * V_DIM), BF16)), name=name,
        compiler_params=_params(("parallel",), 32 << 20))(q, kv, proj, cos, sins)


def _mla_prep_bwd(dqr, dkr, dv, cos, sins, *, name):
    s = dqr.shape[0]
    tm = min(s, 256)
    scale = (QK_NOPE + QK_ROPE) ** -0.5

    def body(dq_ref, dk_ref, dv_ref, cos_ref, sin_ref, dqo_ref, dkv_ref, dkr_ref):
        cosv, sinv = cos_ref[...], sin_ref[...]
        lane = lax.broadcasted_iota(jnp.int32, cosv.shape, 1)
        ksum = jnp.zeros(cosv.shape, F32)
        for h in range(MLA_HEADS):
            hs = slice(HEAD_PAD * h, HEAD_PAD * (h + 1))
            d = dq_ref[:, hs]
            dk = dk_ref[:, hs]
            dqo_ref[:, hs] = ((d * cosv + _rope_swap(d * sinv)) * scale).astype(dqo_ref.dtype)
            dkv_ref[:, hs] = jnp.concatenate([dk[:, :QK_NOPE], dv_ref[:, V_DIM * h:V_DIM * (h + 1)]], axis=1).astype(dkv_ref.dtype)
            ksum = ksum + dk
        ksum = jnp.where((lane >= QK_NOPE) & (lane < QK_NOPE + QK_ROPE), ksum, 0.0)
        un = ksum * cosv + _rope_swap(ksum * sinv)
        dkr_ref[...] = pltpu.roll(un, HEAD_PAD - QK_NOPE, axis=1).astype(dkr_ref.dtype)

    wide = pl.BlockSpec((tm, MLA_HEADS * HEAD_PAD), lambda i: (i, 0))
    half = pl.BlockSpec((tm, MLA_HEADS * V_DIM), lambda i: (i, 0))
    tab = pl.BlockSpec((tm, LANES), lambda i: (i, 0))
    return pl.pallas_call(
        body, grid=(s // tm,), in_specs=[wide, wide, half, tab, tab], out_specs=(wide, wide, tab),
        out_shape=(SDS((s, MLA_HEADS * HEAD_PAD), BF16), SDS((s, MLA_HEADS * HEAD_PAD), BF16), SDS((s, LANES), BF16)),
        name=name, compiler_params=_params(("parallel",), 40 << 20))(dqr, dkr, dv, cos, sins)


FLASH_TILE = 512
FLASH_ROWS = 32


def _flash_fwd(q, k, v, *, name):
    s = q.shape[0]
    t = min(s, FLASH_TILE)
    nq = s // t
    npair = MLA_HEADS // 2

    def body(q_ref, k_ref, v_ref, o_ref, lse_ref):
        i = pl.program_id(1)
        qs = [q_ref[:, HEAD_PAD * e:HEAD_PAD * (e + 1)] for e in range(2)]
        diag = lax.broadcasted_iota(jnp.int32, (t, t), 0) >= lax.broadcasted_iota(jnp.int32, (t, t), 1)

        def step(j, carry, masked):
            rows = pl.ds(pl.multiple_of(j * t, t), t)
            new = []
            for e in range(2):
                m, l, acc = carry[e]
                sc = _dot_nt(qs[e], k_ref[rows, HEAD_PAD * e:HEAD_PAD * (e + 1)])
                if masked:
                    sc = jnp.where(diag, sc, NEG)
                m_new = jnp.maximum(m, jnp.max(sc, axis=1, keepdims=True))
                pr = jnp.exp(sc - m_new)
                alpha = jnp.exp(m - m_new)
                l = alpha * l + jnp.sum(pr, axis=1, keepdims=True)
                acc = alpha * acc + _dot(pr, v_ref[rows, V_DIM * e:V_DIM * (e + 1)])
                new.append((m_new, l, acc))
            return tuple(new)

        init = tuple((jnp.full((t, 1), NEG, F32), jnp.zeros((t, 1), F32), jnp.zeros((t, V_DIM), F32)) for _ in range(2))
        carry = lax.fori_loop(0, i, functools.partial(step, masked=False), init)
        carry = step(i, carry, True)
        o_ref[...] = jnp.concatenate([acc / l for _, l, acc in carry], axis=1)
        lse_ref[0] = jnp.concatenate([jnp.broadcast_to(m + jnp.log(l), (t, V_DIM)) for m, l, _ in carry], axis=1)

    return pl.pallas_call(
        body, grid=(npair, nq),
        in_specs=[pl.BlockSpec((t, 2 * HEAD_PAD), lambda hp, i: (i, hp)), pl.BlockSpec((s, 2 * HEAD_PAD), lambda hp, i: (0, hp)),
                  pl.BlockSpec((s, 2 * V_DIM), lambda hp, i: (0, hp))],
        out_specs=(pl.BlockSpec((t, 2 * V_DIM), lambda hp, i: (i, hp)), pl.BlockSpec((1, t, LANES), lambda hp, i: (hp, i, 0))),
        out_shape=(SDS((s, MLA_HEADS * V_DIM), F32), SDS((npair, s, LANES), F32)), name=name,
        compiler_params=_params(("parallel", "parallel"), 40 << 20))(q, k, v)


def _flash_bwd(q, k, v, o, lse, do, *, name):
    s = q.shape[0]
    t = min(s, FLASH_TILE)
    nq = s // t
    npair = MLA_HEADS // 2
    nchunk = t // FLASH_ROWS

    def valid_cols(r):
        return min(t, -(-((r + 1) * FLASH_ROWS) // LANES) * LANES)

    def body(q_ref, k_ref, v_ref, o_ref, lse_ref, do_ref, dq_ref, dk_ref, dv_ref, s_scr, dp_scr, p_scr, ds_scr, dk_acc, dv_acc):
        j = pl.program_id(1)

        @pl.when(j == 0)
        def _():
            dq_ref[...] = jnp.zeros_like(dq_ref)

        dk_acc[...] = jnp.zeros(dk_acc.shape, F32)
        dv_acc[...] = jnp.zeros(dv_acc.shape, F32)
        qsl = [slice(HEAD_PAD * e, HEAD_PAD * (e + 1)) for e in range(2)]
        vsl = [slice(V_DIM * e, V_DIM * (e + 1)) for e in range(2)]

        def step(i, carry, masked):
            rows = pl.ds(pl.multiple_of(i * t, t), t)
            for e in range(2):
                ke = k_ref[:, qsl[e]]
                qi = q_ref[rows, qsl[e]]
                doi = do_ref[rows, vsl[e]]
                delta = jnp.sum(doi * o_ref[rows, vsl[e]], axis=1, keepdims=True)
                lse_i = lse_ref[0, rows, vsl[e]][:, 0:1]
                dob = doi.astype(BF16)
                s_scr[e] = _dot_nt(qi, ke)
                dp_scr[e] = _dot_nt(dob, v_ref[:, vsl[e]])
                for r in range(nchunk):
                    rs = slice(r * FLASH_ROWS, (r + 1) * FLASH_ROWS)
                    width = valid_cols(r) if masked else t
                    sc = s_scr[e, rs, 0:width]
                    if masked:
                        row = r * FLASH_ROWS + lax.broadcasted_iota(jnp.int32, (FLASH_ROWS, width), 0)
                        sc = jnp.where(row >= lax.broadcasted_iota(jnp.int32, (FLASH_ROWS, width), 1), sc, NEG)
                    pr = jnp.exp(sc - lse_i[rs, :])
                    dsc = pr * (dp_scr[e, rs, 0:width] - delta[rs, :])
                    p_scr[e, rs, 0:width] = pr.astype(BF16)
                    ds_scr[e, rs, 0:width] = dsc.astype(BF16)
                    if width < t:
                        p_scr[e, rs, width:t] = jnp.zeros((FLASH_ROWS, t - width), BF16)
                        ds_scr[e, rs, width:t] = jnp.zeros((FLASH_ROWS, t - width), BF16)
                dv_acc[e] += _dot_tn(p_scr[e], dob)
                dk_acc[e] += _dot_tn(ds_scr[e], qi)
                dq_ref[rows, qsl[e]] += _dot(ds_scr[e], ke)
            return carry

        step(j, 0, True)
        lax.fori_loop(j + 1, nq, functools.partial(step, masked=False), 0)
        dk_ref[...] = jnp.concatenate([dk_acc[e] for e in range(2)], axis=1)
        dv_ref[...] = jnp.concatenate([dv_acc[e] for e in range(2)], axis=1)

    full_q = pl.BlockSpec((s, 2 * HEAD_PAD), lambda hp, j: (0, hp))
    full_v = pl.BlockSpec((s, 2 * V_DIM), lambda hp, j: (0, hp))
    blk_k = pl.BlockSpec((t, 2 * HEAD_PAD), lambda hp, j: (j, hp))
    blk_v = pl.BlockSpec((t, 2 * V_DIM), lambda hp, j: (j, hp))
    return pl.pallas_call(
        body, grid=(npair, nq),
        in_specs=[full_q, blk_k, blk_v, full_v, pl.BlockSpec((1, s, LANES), lambda hp, j: (hp, 0, 0)), full_v],
        out_specs=(full_q, blk_k, blk_v),
        out_shape=(SDS((s, MLA_HEADS * HEAD_PAD), F32), SDS((s, MLA_HEADS * HEAD_PAD), F32), SDS((s, MLA_HEADS * V_DIM), F32)),
        scratch_shapes=[pltpu.VMEM((2, t, t), F32), pltpu.VMEM((2, t, t), F32), pltpu.VMEM((2, t, t), BF16),
                        pltpu.VMEM((2, t, t), BF16), pltpu.VMEM((2, t, HEAD_PAD), F32), pltpu.VMEM((2, t, V_DIM), F32)],
        name=name, compiler_params=_params(("parallel", "arbitrary"), 48 << 20))(q, k, v, o, lse, do)


def _mem_attn_fwd(q, k, v, *, name):
    s = q.shape[0]
    tm = min(s, 512)
    ml = k.shape[0]
    scale = MEM_HEAD_DIM ** -0.5

    def body(q_ref, k_ref, v_ref, o_ref):
        for h in range(MEM_HEADS):
            hs = slice(MEM_HEAD_DIM * h, MEM_HEAD_DIM * (h + 1))
            sc = _dot_nt(q_ref[:, hs], k_ref[:, hs]) * scale
            pr = jnp.exp(sc - jnp.max(sc, axis=1, keepdims=True))
            pr = pr / jnp.sum(pr, axis=1, keepdims=True)
            o_ref[:, hs] = _dot(pr, v_ref[:, hs]).astype(o_ref.dtype)

    blk = pl.BlockSpec((tm, D_MODEL), lambda i: (i, 0))
    kv = pl.BlockSpec((ml, D_MODEL), lambda i: (0, 0))
    return pl.pallas_call(body, grid=(s // tm,), in_specs=[blk, kv, kv], out_specs=blk,
                          out_shape=SDS((s, D_MODEL), BF16), name=name,
                          compiler_params=_params(("parallel",), 24 << 20))(q, k, v)


def _mem_attn_bwd(q, k, v, do, *, name):
    s = q.shape[0]
    tm = min(s, 512)
    ml = k.shape[0]
    scale = MEM_HEAD_DIM ** -0.5

    def body(q_ref, k_ref, v_ref, do_ref, dq_ref, dk_ref, dv_ref):
        @pl.when(pl.program_id(0) == 0)
        def _():
            dk_ref[...] = jnp.zeros_like(dk_ref)
            dv_ref[...] = jnp.zeros_like(dv_ref)

        for h in range(MEM_HEADS):
            hs = slice(MEM_HEAD_DIM * h, MEM_HEAD_DIM * (h + 1))
            qh, kh, vh, doh = q_ref[:, hs], k_ref[:, hs], v_ref[:, hs], do_ref[:, hs]
            sc = _dot_nt(qh, kh) * scale
            pr = jnp.exp(sc - jnp.max(sc, axis=1, keepdims=True))
            pr = pr / jnp.sum(pr, axis=1, keepdims=True)
            dp = _dot_nt(doh, vh)
            dsc = pr * (dp - jnp.sum(pr * dp, axis=1, keepdims=True)) * scale
            dq_ref[:, hs] = _dot(dsc, kh).astype(dq_ref.dtype)
            dk_ref[:, hs] += _dot_tn(dsc, qh)
            dv_ref[:, hs] += _dot_tn(pr, doh)

    blk = pl.BlockSpec((tm, D_MODEL), lambda i: (i, 0))
    kv = pl.BlockSpec((ml, D_MODEL), lambda i: (0, 0))
    return pl.pallas_call(body, grid=(s // tm,), in_specs=[blk, kv, kv, blk], out_specs=(blk, kv, kv),
                          out_shape=(SDS((s, D_MODEL), BF16), SDS((ml, D_MODEL), F32), SDS((ml, D_MODEL), F32)), name=name,
                          compiler_params=_params(("arbitrary",), 32 << 20))(q, k, v, do)


MATS = (("w_in", (1024, 940), 1), ("w_uq", (384, 384), 1), ("w_ukv", (256, 512), 1), ("w_out", (512, 1024), 0),
        ("ssm_conv_w", (4, 512), 1),
        ("w_mq", (256, 1024), 0), ("w_mk", (256, 1024), 0), ("w_mv", (256, 1024), 0), ("w_mo", (256, 1024), 0),
        ("w_up", (1024, 1408), 1), ("w_down", (704, 1024), 0), ("ffn_conv_w", (3, 1408), 1))
GROUPS = {"proj": (0,), "mixer": (1, 2, 3, 4), "mem": (5, 6, 7, 8), "ffn": (9, 10, 11)}
UP_SHARD_COLS = 1408
F32_ON_WIRE = ("ssm_conv_w", "ffn_conv_w")
SMALL = (("norm_mix", 1024), ("ssm_conv_b", 2048), ("dt_bias", 16), ("a_log", 16), ("d_skip", 16), ("ssm_norm", 1024),
         ("q_norm", 384), ("kv_norm", 256), ("attn_out_norm", 1024), ("norm_mem_q", 1024), ("norm_mem_kv", 1024),
         ("norm_ffn", 1024), ("ffn_conv_b", 5632))
PACK_COLS = 1024


def _pad_cols(t, n):
    return jnp.pad(t, ((0, 0),) * (t.ndim - 1) + ((0, n - t.shape[-1]),))


def _w_in_to_padded(t):
    z, xbc, dt, cq, ckv, kr = jnp.split(t, (1024, 3072, 3088, 3472, 3728), axis=-1)
    return jnp.concatenate([xbc, z, cq, _pad_cols(dt, LANES), ckv, _pad_cols(kr, P_IN - P_KR)], axis=-1)


def _w_in_from_padded(t):
    return jnp.concatenate([t[..., P_Z:P_Z + 1024], t[..., P_XBC:P_XBC + 2048], t[..., P_DT:P_DT + SSM_HEADS],
                            t[..., P_CQ:P_CQ + Q_LORA], t[..., P_CKV:P_CKV + KV_LORA], t[..., P_KR:P_KR + QK_ROPE]], axis=-1)


def _cols_joined(g):
    return jnp.concatenate([g[j] for j in range(N_CHIPS)], axis=-1)


def _cols_by_chip(t, dtype):
    k = t.shape[0]
    return t.reshape(k, N_CHIPS, -1).transpose(1, 0, 2).astype(dtype)


def _rows_by_chip(t):
    return t.reshape(N_CHIPS, -1, t.shape[-1])


def _mixer_weights(gw):
    wl = {}
    uq = _cols_joined(gw["w_uq"]).reshape(Q_LORA, MLA_HEADS, QK_NOPE + QK_ROPE)
    wl["w_uq"] = _pad_cols(uq, HEAD_PAD).reshape(Q_LORA, MLA_HEADS * HEAD_PAD)
    wl["w_ukv"] = _cols_joined(gw["w_ukv"])
    wl["ssm_conv_w"] = _cols_joined(gw["ssm_conv_w"])
    return wl


def _layer_fwd(x0, mem, cos, sins, weights, sp, li):
    n = lambda t: f"l{li}_{t}"
    lead = ()
    sv = {"x0": x0}
    gw = dict(weights("proj", x0))
    w_in = _w_in_to_padded(_cols_joined(gw["w_in"]))
    h = _rms_fwd(x0, sp["norm_mix"], name=n("mix_norm"))
    proj = _mm(h, w_in, name=n("mix_proj"))
    gw.update(weights("mixer", proj))
    wl = dict(_mixer_weights(gw), w_in=w_in)
    xbc = _ssm_conv_fwd(proj, wl["ssm_conv_w"], sp["ssm_conv_b"], name=n("ssm_conv"))
    y, pstates = _ssd_fwd(xbc, proj, sp["dt_bias"], sp["a_log"], sp["d_skip"], name=n("ssd"))
    y_ssm = _gated_rms_fwd(y, proj, sp["ssm_norm"], name=n("ssm_gate"))
    cqn = _rms_fwd(proj, sp["q_norm"], col=(Q_LORA, P_CQ // Q_LORA), name=n("q_norm"))
    ckvn = _rms_fwd(proj, sp["kv_norm"], col=(KV_LORA, P_CKV // KV_LORA), name=n("kv_norm"))
    q = _mm(cqn, wl["w_uq"], name=n("uq"))
    kv = _mm(ckvn, wl["w_ukv"], name=n("ukv"))
    qr, kr, v = _mla_prep(q, kv, proj, cos, sins, name=n("rope"))
    att, lse = _flash_fwd(qr, kr, v, name=n("flash"))
    y_att = _rms_fwd(att, sp["attn_out_norm"], name=n("att_norm"))
    x1 = _mm(y_ssm, gw["w_out"], b_lead=lead, b_rows=(0, D_SSM), res=x0, name=n("out_a"))
    x1 = _mm(y_att, gw["w_out"], b_lead=lead, b_rows=(D_SSM, D_SSM), res=x1, name=n("out_b"))
    sv.update(h=h, proj=proj, xbc=xbc, y=y, pstates=pstates, y_ssm=y_ssm, cqn=cqn, ckvn=ckvn, qr=qr, kr=kr, v=v,
              att=att, lse=lse, y_att=y_att, x1=x1)
    gw.update(weights("mem", x1))
    hq = _rms_fwd(x1, sp["norm_mem_q"], name=n("memq_norm"))
    hm = _rms_fwd(mem, sp["norm_mem_kv"], name=n("memkv_norm"))
    mq = _mm(hq, gw["w_mq"], b_lead=lead, out_dtype=BF16, name=n("mq"))
    mk = _mm(hm, gw["w_mk"], b_lead=lead, out_dtype=BF16, name=n("mk"))
    mv = _mm(hm, gw["w_mv"], b_lead=lead, out_dtype=BF16, name=n("mv"))
    mo = _mem_attn_fwd(mq, mk, mv, name=n("mem_attn"))
    x2 = _mm(mo, gw["w_mo"], b_lead=lead, res=x1, name=n("mo"))
    sv.update(hq=hq, hm=hm, mq=mq, mk=mk, mv=mv, mo=mo, x2=x2)
    gw.update(weights("ffn", x2))
    wl["ffn_conv_w"] = _cols_joined(gw["ffn_conv_w"])
    hf = _rms_fwd(x2, sp["norm_ffn"], name=n("ffn_norm"))
    up_g = _mm(hf, gw["w_up"], b_lead=lead, b_chips=(0, 2), name=n("up_g"))
    up_v = _mm(hf, gw["w_up"], b_lead=lead, b_chips=(2, 2), name=n("up_v"))
    act = _ffn_conv_fwd(up_g, up_v, wl["ffn_conv_w"], sp["ffn_conv_b"], name=n("ffn_conv"))
    x3 = _mm(act, gw["w_down"], b_lead=lead, res=x2, name=n("down"))
    sv.update(hf=hf, up_g=up_g, up_v=up_v, act=act)
    return x3, sv, gw, wl


def _layer_bwd(dx3, dx3b, mem, cos, sins, gw, wl, sp, sv, li, emit):
    n = lambda t: f"l{li}_b_{t}"
    lead = ()
    g = {}

    def after(token, v):
        return v if token is None else v + token[0, 0]

    dact = _mm(dx3b, gw["w_down"], tb=True, b_lead=lead, out_dtype=BF16, name=n("down_dx"))
    g["w_down"] = _rows_by_chip(_mm(sv["act"], dx3b, ta=True, out_dtype=BF16, name=n("down_dw")))
    dup_g, dup_v, dcw, g["ffn_conv_b"] = _ffn_conv_bwd(
        sv["up_g"], sv["up_v"], wl["ffn_conv_w"], sp["ffn_conv_b"], dact, name=n("ffn_conv"))
    g["ffn_conv_w"] = _cols_by_chip(dcw, F32)
    nsh = UP_SHARD_COLS
    dhf = None
    for c4 in range(N_CHIPS):
        dhf = _mm(dup_g if c4 < 2 else dup_v, gw["w_up"], tb=True, a_col=(nsh, c4 % 2), b_lead=(c4,), res=dhf,
                  name=n(f"up{c4}_dx"))
    g["w_up"] = jnp.concatenate([_mm(sv["hf"], dup_g, ta=True, o_chips=nsh, out_dtype=BF16, name=n("upg_dw")),
                                 _mm(sv["hf"], dup_v, ta=True, o_chips=nsh, out_dtype=BF16, name=n("upv_dw"))], axis=0)
    dx2, dx2b, g["norm_ffn"] = _rms_bwd(sv["x2"], after(emit("ffn", g), sp["norm_ffn"]), dhf, dx3, name=n("ffn_norm"))
    dmo = _mm(dx2b, gw["w_mo"], tb=True, b_lead=lead, out_dtype=BF16, name=n("mo_dx"))
    g["w_mo"] = _rows_by_chip(_mm(sv["mo"], dx2b, ta=True, out_dtype=BF16, name=n("mo_dw")))
    dmq, dmk, dmv = _mem_attn_bwd(sv["mq"], sv["mk"], sv["mv"], dmo, name=n("mem_attn"))
    dhq = _mm(dmq, gw["w_mq"], tb=True, b_lead=lead, name=n("mq_dx"))
    g["w_mq"] = _rows_by_chip(_mm(sv["hq"], dmq, ta=True, out_dtype=BF16, name=n("mq_dw")))
    dhm = _mm(dmk, gw["w_mk"], tb=True, b_lead=lead, name=n("mk_dx"))
    dhm = _mm(dmv, gw["w_mv"], tb=True, b_lead=lead, res=dhm, name=n("mv_dx"))
    g["w_mk"] = _rows_by_chip(_mm(sv["hm"], dmk, ta=True, out_dtype=BF16, name=n("mk_dw")))
    g["w_mv"] = _rows_by_chip(_mm(sv["hm"], dmv, ta=True, out_dtype=BF16, name=n("mv_dw")))
    dx1, dx1b, g["norm_mem_q"] = _rms_bwd(sv["x1"], after(emit("mem", g), sp["norm_mem_q"]), dhq, dx2, name=n("memq_norm"))
    _, _, g["norm_mem_kv"] = _rms_bwd(mem, sp["norm_mem_kv"], dhm, name=n("memkv_norm"))
    dy_ssm = _mm(dx1b, gw["w_out"], tb=True, b_lead=lead, b_rows=(0, D_SSM), name=n("outa_dx"))
    dy_att = _mm(dx1b, gw["w_out"], tb=True, b_lead=lead, b_rows=(D_SSM, D_SSM), name=n("outb_dx"))
    g["w_out"] = _rows_by_chip(jnp.concatenate([_mm(sv["y_ssm"], dx1b, ta=True, out_dtype=BF16, name=n("outa_dw")),
                                                _mm(sv["y_att"], dx1b, ta=True, out_dtype=BF16, name=n("outb_dw"))], axis=0))
    datt, _, g["attn_out_norm"] = _rms_bwd(sv["att"], sp["attn_out_norm"], dy_att, name=n("att_norm"))
    dqr, dkr, dv = _flash_bwd(sv["qr"], sv["kr"], sv["v"], sv["att"], sv["lse"], datt, name=n("flash"))
    dq, dkv, dkrope = _mla_prep_bwd(dqr, dkr, dv, cos, sins, name=n("rope"))
    duq = _mm(sv["cqn"], dq, ta=True, name=n("uq_dw")).reshape(Q_LORA, MLA_HEADS, HEAD_PAD)[..., :QK_NOPE + QK_ROPE]
    g["w_uq"] = _cols_by_chip(duq.reshape(Q_LORA, -1), BF16)
    dcqn = _mm(dq, wl["w_uq"], tb=True, name=n("uq_dx"))
    g["w_ukv"] = _cols_by_chip(_mm(sv["ckvn"], dkv, ta=True, name=n("ukv_dw")), BF16)
    dckvn = _mm(dkv, wl["w_ukv"], tb=True, name=n("ukv_dx"))
    proj = sv["proj"]
    _, dcq, g["q_norm"] = _rms_bwd(proj, sp["q_norm"], dcqn, col=(Q_LORA, P_CQ // Q_LORA), name=n("q_norm"))
    _, dckv, g["kv_norm"] = _rms_bwd(proj, sp["kv_norm"], dckvn, col=(KV_LORA, P_CKV // KV_LORA), name=n("kv_norm"))
    dy, dz, g["ssm_norm"] = _gated_rms_bwd(sv["y"], proj, sp["ssm_norm"], dy_ssm, name=n("ssm_gate"))
    dxbc, ddt, da_log, dd_skip, ddt_bias = _ssd_bwd(
        sv["xbc"], proj, sp["dt_bias"], sp["a_log"], sp["d_skip"], sv["pstates"], dy, name=n("ssd"))
    g["a_log"], g["d_skip"], g["dt_bias"] = da_log[0, :SSM_HEADS], dd_skip[0, :SSM_HEADS], ddt_bias[0, :SSM_HEADS]
    dxbc_pre, dsw, g["ssm_conv_b"] = _ssm_conv_bwd(proj, wl["ssm_conv_w"], sp["ssm_conv_b"], dxbc, name=n("ssm_conv"))
    g["ssm_conv_w"] = _cols_by_chip(dsw, F32)
    started = emit("mixer", g)
    s = proj.shape[0]
    dproj = jnp.concatenate([dxbc_pre, dz, dcq, ddt, dckv, dkrope,
                             jnp.zeros((s, P_IN - P_KR - LANES), BF16)], axis=1)
    dh = _mm(dproj, wl["w_in"], tb=True, name=n("proj_dx"))
    g["w_in"] = _cols_by_chip(_w_in_from_padded(_mm(sv["h"], dproj, ta=True, name=n("proj_dw"))), BF16)
    dx0, dx0b, g["norm_mix"] = _rms_bwd(sv["x0"], after(started, sp["norm_mix"]), dh, dx1, name=n("mix_norm"))
    return dx0, dx0b, g, emit("proj", g)


def _chip_peers(x, y):
    return [(1 - x, y), (x, 1 - y), (1 - x, 1 - y)]


HBM_SPEC = pl.BlockSpec(memory_space=pltpu.HBM)
SEM_SPEC = pl.BlockSpec(memory_space=pltpu.SEMAPHORE)
ANY_SPEC = pl.BlockSpec(memory_space=pl.ANY)
VMEM_SPEC = pl.BlockSpec(memory_space=pltpu.VMEM)
DATAFLOW = pltpu.SideEffectType.DATAFLOW_SIDE_EFFECTING
TOKEN_SHAPE = (8, LANES)


def _exchange_start(srcs, land_shapes, src_view, dst_view, token, *, name):
    n = len(srcs)

    def body(*refs):
        s, l, tok_in = refs[:n], refs[n:2 * n], refs[2 * n]
        send_sems, recv_sems = refs[2 * n + 1], refs[2 * n + 2]
        tok_out = refs[-1]
        x, y, c = lax.axis_index("x"), lax.axis_index("y"), lax.axis_index("c")
        me = 2 * x + y
        for t in range(n):
            for k, (px, py) in enumerate(_chip_peers(x, y)):
                pltpu.make_async_remote_copy(
                    src_ref=src_view(t, s[t], 2 * px + py), dst_ref=dst_view(t, l[t], me), send_sem=send_sems.at[3 * t + k],
                    recv_sem=recv_sems.at[3 * t + k], device_id=(px, py, c), device_id_type=MESH).start()
            pltpu.make_async_copy(src_view(t, s[t], me), dst_view(t, l[t], me), send_sems.at[3 * n + t]).start()
        tok_out[...] = tok_in[...]

    hbm = lambda t: pltpu.with_memory_space_constraint(t, pltpu.HBM)
    lands = [lax.empty(l.shape, l.dtype) for l in land_shapes]
    outs = pl.pallas_call(
        body, name=name,
        out_shape=(pltpu.SemaphoreType.DMA((4 * n,)), pltpu.SemaphoreType.DMA((3 * n,)),
                   *[pltpu.HBM(l.shape, l.dtype) for l in land_shapes], SDS(TOKEN_SHAPE, F32)),
        in_specs=[HBM_SPEC] * (2 * n) + [VMEM_SPEC], out_specs=(SEM_SPEC, SEM_SPEC, *[HBM_SPEC] * n, VMEM_SPEC),
        input_output_aliases={n + t: 2 + t for t in range(n)},
        compiler_params=pltpu.CompilerParams(has_side_effects=DATAFLOW))(*[hbm(t) for t in srcs], *[hbm(t) for t in lands], token)
    return outs[0], outs[1], list(outs[2:2 + n]), outs[-1]


def _exchange_wait(srcs, lands, send_sems, recv_sems, after, src_view, dst_view, which, *, name):
    n = len(srcs)
    m = len(which)

    def body(*refs):
        s, l = refs[:m], refs[m:2 * m]
        send_ref, recv_ref = refs[2 * m], refs[2 * m + 1]
        x, y, c = lax.axis_index("x"), lax.axis_index("y"), lax.axis_index("c")
        me = 2 * x + y
        for i, t in enumerate(which):
            for k, (px, py) in enumerate(_chip_peers(x, y)):
                chip = 2 * px + py
                cp = pltpu.make_async_remote_copy(
                    src_ref=src_view(t, s[i], chip), dst_ref=dst_view(t, l[i], chip), send_sem=send_ref.at[3 * t + k],
                    recv_sem=recv_ref.at[3 * t + k], device_id=(px, py, c), device_id_type=MESH)
                cp.wait_send()
                cp.wait_recv()
            pltpu.make_async_copy(src_view(t, s[i], me), dst_view(t, l[i], me), send_ref.at[3 * n + t]).wait()

    outs = pl.pallas_call(
        body, name=name, out_shape=[pltpu.HBM(lands[t].shape, lands[t].dtype) for t in which],
        in_specs=[HBM_SPEC] * (2 * m) + [SEM_SPEC, SEM_SPEC, ANY_SPEC], out_specs=[HBM_SPEC] * m,
        input_output_aliases={m + i: i for i in range(m)},
        compiler_params=pltpu.CompilerParams(has_side_effects=DATAFLOW))(
            *[srcs[t] for t in which], *[lands[t] for t in which], send_sems, recv_sems, after)
    return list(outs)


def _gather_layer_start(shards, li, token):
    src_view = lambda t, ref, chip: ref.at[li]
    dst_view = lambda t, ref, chip: ref.at[chip]
    send_sems, recv_sems, lands, token = _exchange_start(
        shards, [SDS((N_CHIPS,) + s.shape[1:], s.dtype) for s in shards], src_view, dst_view, token, name=f"gather{li}_start")
    return (shards, lands, send_sems, recv_sems, src_view, dst_view, f"gather{li}"), token


def _scatter_start(grads, tag, token):
    view = lambda t, ref, chip: ref.at[chip]
    send_sems, recv_sems, lands, token = _exchange_start(
        grads, [SDS(g.shape, g.dtype) for g in grads], view, view, token, name=f"scatter{tag}_start")
    return (grads, lands, send_sems, recv_sems, view, view, f"scatter{tag}"), token


def _exchange_finish(handle, after, which=None, tag=""):
    srcs, lands, send_sems, recv_sems, src_view, dst_view, name = handle
    which = tuple(range(len(srcs))) if which is None else which
    return _exchange_wait(srcs, lands, send_sems, recv_sems, after, src_view, dst_view, which, name=f"{name}{tag}_wait")


def _swap_start(bufs, token, *, name):
    n = len(bufs)

    def body(*refs):
        s, l, tok_in = refs[:n], refs[n:2 * n], refs[2 * n]
        send_sems, recv_sems = refs[2 * n + 1], refs[2 * n + 2]
        x, y, c = lax.axis_index("x"), lax.axis_index("y"), lax.axis_index("c")
        for t in range(n):
            pltpu.make_async_remote_copy(src_ref=s[t], dst_ref=l[t], send_sem=send_sems.at[t], recv_sem=recv_sems.at[t],
                                         device_id=(x, y, 1 - c), device_id_type=MESH).start()
        refs[-1][...] = tok_in[...]

    hbm = lambda t: pltpu.with_memory_space_constraint(t, pltpu.HBM)
    lands = [lax.empty(b.shape, b.dtype) for b in bufs]
    outs = pl.pallas_call(
        body, name=f"{name}_start",
        out_shape=(pltpu.SemaphoreType.DMA((n,)), pltpu.SemaphoreType.DMA((n,)),
                   *[pltpu.HBM(b.shape, b.dtype) for b in bufs], SDS(TOKEN_SHAPE, F32)),
        in_specs=[HBM_SPEC] * (2 * n) + [VMEM_SPEC], out_specs=(SEM_SPEC, SEM_SPEC, *[HBM_SPEC] * n, VMEM_SPEC),
        input_output_aliases={n + t: 2 + t for t in range(n)},
        compiler_params=pltpu.CompilerParams(has_side_effects=DATAFLOW))(*[hbm(t) for t in bufs], *[hbm(t) for t in lands], token)
    return (bufs, list(outs[2:2 + n]), outs[0], outs[1], name), outs[-1]


def _swap_wait(handle, after):
    bufs, lands, send_sems, recv_sems, name = handle
    n = len(bufs)

    def body(*refs):
        s, l = refs[:n], refs[n:2 * n]
        send_ref, recv_ref = refs[2 * n], refs[2 * n + 1]
        x, y, c = lax.axis_index("x"), lax.axis_index("y"), lax.axis_index("c")
        for t in range(n):
            cp = pltpu.make_async_remote_copy(src_ref=s[t], dst_ref=l[t], send_sem=send_ref.at[t], recv_sem=recv_ref.at[t],
                                              device_id=(x, y, 1 - c), device_id_type=MESH)
            cp.wait_send()
            cp.wait_recv()

    outs = pl.pallas_call(
        body, name=f"{name}_wait", out_shape=[pltpu.HBM(b.shape, b.dtype) for b in bufs],
        in_specs=[HBM_SPEC] * (2 * n) + [SEM_SPEC, SEM_SPEC, ANY_SPEC], out_specs=[HBM_SPEC] * n,
        input_output_aliases={n + t: t for t in range(n)},
        compiler_params=pltpu.CompilerParams(has_side_effects=DATAFLOW))(*bufs, *lands, send_sems, recv_sems, after)
    return list(outs)


def _all_gather8(src, *, name):
    def body(src_ref, out_ref, send_sems, recv_sems, local_sem):
        x, y, c = lax.axis_index("x"), lax.axis_index("y"), lax.axis_index("c")
        me = 4 * x + 2 * y + c
        mine = pltpu.make_async_copy(src_ref, out_ref.at[me], local_sem)
        mine.start()

        def peer(k):
            return (x ^ (k >> 2 & 1), y ^ (k >> 1 & 1), c ^ (k & 1))

        sends = []
        for k in range(1, N_DEV):
            cp = pltpu.make_async_remote_copy(src_ref=src_ref, dst_ref=out_ref.at[me], send_sem=send_sems.at[k - 1],
                                              recv_sem=recv_sems.at[k - 1], device_id=peer(k), device_id_type=MESH)
            cp.start()
            sends.append(cp)
        for k in range(1, N_DEV):
            px, py, pc = peer(k)
            pltpu.make_async_remote_copy(src_ref=src_ref, dst_ref=out_ref.at[4 * px + 2 * py + pc],
                                         send_sem=send_sems.at[k - 1], recv_sem=recv_sems.at[k - 1],
                                         device_id=peer(k), device_id_type=MESH).wait_recv()
        for cp in sends:
            cp.wait_send()
        mine.wait()

    any_spec = pl.BlockSpec(memory_space=pl.ANY)
    return pl.pallas_call(
        body, in_specs=[any_spec], out_specs=any_spec, out_shape=SDS((N_DEV,) + src.shape, src.dtype),
        scratch_shapes=[pltpu.SemaphoreType.DMA((N_DEV - 1,)), pltpu.SemaphoreType.DMA((N_DEV - 1,)), pltpu.SemaphoreType.DMA],
        name=name)(src)


def _adam_terms(w, g, m, v):
    m = ADAM_B1 * m + (1.0 - ADAM_B1) * g
    v = ADAM_B2 * v + (1.0 - ADAM_B2) * (g * g)
    m_hat = m / (1.0 - ADAM_B1 ** ADAM_STEP)
    v_hat = v / (1.0 - ADAM_B2 ** ADAM_STEP)
    delta = -ADAM_LR * (m_hat / (jnp.sqrt(v_hat) + ADAM_EPS) + ADAM_WD * w)
    return delta, m, v


def _adamw_shard(mine, other, w, m, v, *, name):
    d, a, b = w.shape
    tr = next((t for t in (128, 64, 32, 16) if a % t == 0), a)

    def body(*refs):
        ga, gb = refs[:d], refs[d:2 * d]
        w_ref, m_ref, v_ref, g_ref, d_ref, nm_ref, nv_ref = refs[2 * d:]

        def plane(ref):
            return ((ref[0].astype(F32) + ref[1].astype(F32)) + ref[2].astype(F32)) + ref[3].astype(F32)

        for lp in range(d):
            @pl.when(pl.program_id(0) == lp)
            def _(lp=lp):
                g = plane(ga[lp]) + plane(gb[lp])
                delta, mn, vn = _adam_terms(w_ref[...], g, m_ref[...], v_ref[...])
                g_ref[...] = g
                d_ref[...] = delta
                nm_ref[...] = mn
                nv_ref[...] = vn

    gspecs = [pl.BlockSpec((N_CHIPS, tr, b), lambda l, i, lp=lp: (0, jnp.where(l == lp, i, 0), 0)) for lp in range(d)]
    blk = pl.BlockSpec((None, tr, b), lambda l, i: (l, i, 0))
    shp = SDS((d, a, b), F32)
    return pl.pallas_call(
        body, grid=(d, a // tr), in_specs=gspecs + gspecs + [blk, blk, blk], out_specs=(blk,) * 4, out_shape=(shp,) * 4,
        name=name, compiler_params=_params(("arbitrary", "arbitrary"), 48 << 20))(*mine, *other, w, m, v)


def _adamw_small(g8, w, m, v, *, name):
    n = w.shape[1]

    def body(g8_ref, w_ref, m_ref, v_ref, g_ref, d_ref, nm_ref, nv_ref):
        g = g8_ref[0]
        for k in range(1, N_DEV):
            g = g + g8_ref[k]
        delta, mn, vn = _adam_terms(w_ref[...], g, m_ref[...], v_ref[...])
        g_ref[...] = g
        d_ref[...] = delta
        nm_ref[...] = mn
        nv_ref[...] = vn

    shp = SDS((1, n), F32)
    return pl.pallas_call(body, out_shape=(shp,) * 4, name=name, compiler_params=_params(None, 24 << 20))(g8, w, m, v)


def _rope_tables(positions):
    inv_freq = 1.0 / (ROPE_THETA ** (jnp.arange(0, QK_ROPE, 2, dtype=F32) / QK_ROPE))
    ang = positions.astype(F32)[:, None] * inv_freq
    c, s = jnp.cos(ang), jnp.sin(ang)
    n = positions.shape[0]
    pad = jnp.zeros((n, HEAD_PAD - QK_NOPE - QK_ROPE), F32)
    cos = jnp.concatenate([jnp.ones((n, QK_NOPE), F32), c, c, pad], axis=1)
    sins = jnp.concatenate([jnp.zeros((n, QK_NOPE), F32), -s, s, pad], axis=1)
    return cos, sins


def _pad_lanes(v):
    return _pad_cols(v.reshape(1, -1), LANES)


def _local_step(x, mem, positions, weights, small, final_norm, loss_target, emit, token):
    cos, sins = _rope_tables(positions)
    saved, gws, wls, sps = [], [], [], []
    h = x
    for li in range(DEPTH):
        sp = {k: small[k][li] for k, _ in SMALL}
        if li == 0:
            sp["norm_mix"] = sp["norm_mix"] + token[0, 0]
        for k in ("dt_bias", "a_log", "d_skip"):
            sp[k] = _pad_lanes(sp[k])
        h, sv, gw, wl = _layer_fwd(h, mem, cos, sins, functools.partial(weights, li), sp, li)
        saved.append(sv)
        gws.append(gw)
        wls.append(wl)
        sps.append(sp)
    loss, dh, dhb, g_final = _final_loss(h, final_norm, loss_target, name="final_loss")
    grads = [None] * DEPTH
    started = None
    for li in reversed(range(DEPTH)):
        sp = sps[li]
        if started is not None:
            sp = dict(sp, ffn_conv_b=sp["ffn_conv_b"] + started[0, 0])
        dh, dhb, grads[li], started = _layer_bwd(dh, dhb, mem, cos, sins, gws[li], wls[li], sp, saved[li], li,
                                                 functools.partial(emit, li))
    return loss, dh, grads, g_final


def _gathered_views(which, lands):
    return {MATS[t][0]: (b.reshape(-1, b.shape[-1]) if MATS[t][2] == 0 else b) for t, b in zip(which, lands)}


def kernel(x, mem, positions, norm_mix, w_in, ssm_conv_w, ssm_conv_b, dt_bias, a_log, d_skip, ssm_norm, q_norm, w_uq, kv_norm, w_ukv, attn_out_norm, w_out, norm_mem_q, norm_mem_kv, w_mq, w_mk, w_mv, w_mo, norm_ffn, w_up, ffn_conv_w, ffn_conv_b, w_down, final_norm, loss_target, m_norm_mix, m_w_in, m_ssm_conv_w, m_ssm_conv_b, m_dt_bias, m_a_log, m_d_skip, m_ssm_norm, m_q_norm, m_w_uq, m_kv_norm, m_w_ukv, m_attn_out_norm, m_w_out, m_norm_mem_q, m_norm_mem_kv, m_w_mq, m_w_mk, m_w_mv, m_w_mo, m_norm_ffn, m_w_up, m_ffn_conv_w, m_ffn_conv_b, m_w_down, m_final_norm, v_norm_mix, v_w_in, v_ssm_conv_w, v_ssm_conv_b, v_dt_bias, v_a_log, v_d_skip, v_ssm_norm, v_q_norm, v_w_uq, v_kv_norm, v_w_ukv, v_attn_out_norm, v_w_out, v_norm_mem_q, v_norm_mem_kv, v_w_mq, v_w_mk, v_w_mv, v_w_mo, v_norm_ffn, v_w_up, v_ffn_conv_w, v_ffn_conv_b, v_w_down, v_final_norm):
    args = dict(locals())
    names = ["norm_mix", "w_in", "ssm_conv_w", "ssm_conv_b", "dt_bias", "a_log", "d_skip", "ssm_norm", "q_norm", "w_uq",
             "kv_norm", "w_ukv", "attn_out_norm", "w_out", "norm_mem_q", "norm_mem_kv", "w_mq", "w_mk", "w_mv", "w_mo",
             "norm_ffn", "w_up", "ffn_conv_w", "ffn_conv_b", "w_down", "final_norm"]
    wts = {k: args[k] for k in names}
    mom = {k: args["m_" + k] for k in names}
    var = {k: args["v_" + k] for k in names}
    mat_names = [k for k, _, _ in MATS]

    shards = [wts[k] if k in F32_ON_WIRE else wts[k].astype(BF16) for k in mat_names]
    token = jnp.zeros(TOKEN_SHAPE, F32)
    gathers = []
    for li in range(DEPTH):
        handle, token = _gather_layer_start(shards, li, token)
        gathers.append(handle)
    small = {k: wts[k] for k, _ in SMALL}

    def weights(li, group, after):
        which = GROUPS[group]
        return _gathered_views(which, _exchange_finish(gathers[li], after, which, tag=f"_{group}"))

    scatters = [[] for _ in range(DEPTH)]
    nm = len(mat_names)
    mine = [[None] * nm for _ in range(DEPTH)]
    swaps = [None] * DEPTH

    def swap_layer(li, after):
        for which, handle in scatters[li]:
            for t, b in zip(which, _exchange_finish(handle, after)):
                mine[li][t] = b
        swaps[li], started = _swap_start(mine[li], jnp.zeros(TOKEN_SHAPE, F32), name=f"swap{li}")
        return started

    def emit(li, group, g):
        last = group == "proj"
        if li == 0:
            which = GROUPS[group]
        elif last:
            which = tuple(range(nm))
        else:
            return None
        handle, started = _scatter_start([g[MATS[t][0]] for t in which], f"{li}_{group}", jnp.zeros(TOKEN_SHAPE, F32))
        scatters[li].append((which, handle))
        if last and li + 1 < DEPTH:
            started = started + swap_layer(li + 1, g["norm_mix"])
        return started

    loss, grad_x, grads, g_final = _local_step(x[0], mem[0], positions[0], weights, small, wts["final_norm"],
                                               loss_target[0], emit, token)
    loss = lax.psum(loss, ("x", "y", "c"))

    swap_layer(0, grad_x)
    other = [_swap_wait(swaps[li], grad_x) for li in range(DEPTH)]
    mat_out = {k: _adamw_shard([mine[li][t] for li in range(DEPTH)], [other[li][t] for li in range(DEPTH)],
                               wts[k], mom[k], var[k], name=f"adamw_{k}") for t, k in enumerate(mat_names)}

    def pack_small(get, fin):
        flat = [get(k).reshape(-1) for k, _ in SMALL] + [fin.reshape(-1)]
        n = sum(f.shape[0] for f in flat)
        return jnp.concatenate(flat + [jnp.zeros((-n % PACK_COLS,), F32)]).reshape(1, -1)

    gs = pack_small(lambda k: jnp.stack([grads[li][k] for li in range(DEPTH)]), g_final)
    g8 = _all_gather8(gs, name="gather_small_grads")
    small_out = _adamw_small(g8, pack_small(wts.get, wts["final_norm"]), pack_small(mom.get, mom["final_norm"]),
                             pack_small(var.get, var["final_norm"]), name="adamw_small")

    def unpack_small(buf):
        out, off = {}, 0
        for k, nel in SMALL:
            out[k] = buf[0, off:off + DEPTH * nel].reshape(DEPTH, nel)
            off += DEPTH * nel
        out["final_norm"] = buf[0, off:off + D_MODEL]
        return out

    small_res = [unpack_small(b) for b in small_out]
    res = []
    for kind in range(4):
        for k in names:
            res.append(small_res[kind][k] if k in small_res[kind] else mat_out[k][kind])
    return (loss, grad_x[None], *res)
```

```python
import functools
import math

import jax
import jax.numpy as jnp
from jax import lax
from jax.experimental import pallas as pl
from jax.experimental.pallas import tpu as pltpu

F32 = jnp.float32
BF16 = jnp.bfloat16
HIGHEST = lax.Precision.HIGHEST
SDS = jax.ShapeDtypeStruct
MESH = pl.DeviceIdType.MESH

D_MODEL = 1024
DEPTH = 4
EPS = 1e-6
SSM_HEADS = 16
SSM_HEAD_DIM = 64
D_SSM = 1024
SSM_GROUPS = 4
SSM_STATE = 128
SSM_CONV = 4
SSM_CHUNK = 128
CONV_CH = 2048
MLA_HEADS = 16
QK_NOPE = 64
QK_ROPE = 32
V_DIM = 64
Q_LORA = 384
KV_LORA = 256
ROPE_THETA = 10000.0
MEM_HEADS = 4
MEM_HEAD_DIM = 256
D_FF = 2816
FFN_CONV = 3
D_IN = 3760
ADAM_LR = 0.001
ADAM_B1 = 0.9
ADAM_B2 = 0.999
ADAM_EPS = 1e-08
ADAM_WD = 0.01
ADAM_STEP = 10

LANES = 128
HEAD_PAD = 128
N_CHIPS = 4
N_DEV = 8
VMEM_CAP_MB = 56

P_XBC, P_Z, P_CQ, P_DT, P_CKV, P_KR, P_IN = 0, 2048, 3072, 3456, 3584, 3840, 4096
NEG = -1e30


def _tile(n, pref):
    t = (min(n, pref) // LANES) * LANES
    while t >= LANES:
        if n % t == 0:
            return t
        t -= LANES
    return n


def _params(sem=None, vmem_bytes=None):
    kw = {}
    if sem is not None:
        kw["dimension_semantics"] = sem
    if vmem_bytes is not None:
        kw["vmem_limit_bytes"] = int(min(max(vmem_bytes, 16 << 20), VMEM_CAP_MB << 20))
    return pltpu.CompilerParams(**kw)


def _nbytes(shape, dtype):
    return math.prod(shape) * jnp.dtype(dtype).itemsize


def _mm(a, b, *, ta=False, tb=False, res=None, out_dtype=F32, name, a_col=None, b_lead=(), b_rows=None,
        b_chips=None, o_chips=None):
    if ta:
        k, m = a.shape
    else:
        m, k = (a.shape[0], a.shape[1] if a_col is None else a_col[0])
    rows_b, cols_b = b.shape[-2:]
    row0 = 0
    if b_rows is not None:
        row0, rows_b = b_rows
    nlead = len(b_lead)
    if b_chips is not None:
        assert not tb
        kb, tn, n = rows_b, cols_b, b_chips[1] * cols_b
        b_blk = (None,) * (1 + nlead) + (kb, tn)
        b_map = lambda i, j: (b_chips[0] + j,) + tuple(b_lead) + (0, 0)
    elif tb:
        n, kb = rows_b, cols_b
        tn = _tile(n, 512)
        assert row0 % tn == 0
        b_blk = (None,) * nlead + (tn, kb)
        b_map = lambda i, j: tuple(b_lead) + (j + row0 // tn, 0)
    else:
        kb, n = rows_b, cols_b
        tn = o_chips if o_chips else _tile(n, 512)
        assert row0 % kb == 0
        b_blk = (None,) * nlead + (kb, tn)
        b_map = lambda i, j: tuple(b_lead) + (row0 // kb, j)
    assert k == kb, (a.shape, b.shape, ta, tb, k, kb)
    tm = _tile(m, 512)
    if ta:
        a_blk, a_map = (k, tm), (lambda i, j: (0, i))
    else:
        a_blk, a_map = (tm, k), ((lambda i, j: (i, 0)) if a_col is None else (lambda i, j: (i, a_col[1])))
    if o_chips:
        o_spec = pl.BlockSpec((None, tm, tn), lambda i, j: (j, i, 0))
        o_shape = SDS((n // tn, m, tn), out_dtype)
    else:
        o_spec = pl.BlockSpec((tm, tn), lambda i, j: (i, j))
        o_shape = SDS((m, n), out_dtype)
    dims = (((0 if ta else 1,), (1 if tb else 0,)), ((), ()))
    has_res = res is not None

    def body(*refs):
        a_ref, b_ref = refs[0], refs[1]
        o_ref = refs[-1]
        acc = lax.dot_general(a_ref[...].astype(BF16), b_ref[...].astype(BF16), dims, preferred_element_type=F32)
        if has_res:
            acc = acc + refs[2][...]
        o_ref[...] = acc.astype(o_ref.dtype)

    bb = tuple(d for d in b_blk if d is not None)
    vmem = 2 * (_nbytes(a_blk, a.dtype) + _nbytes(bb, b.dtype) + (2 if has_res else 1) * _nbytes((tm, tn), F32))
    vmem += _nbytes(a_blk, BF16) + _nbytes(bb, BF16) + 2 * _nbytes((tm, tn), F32) + (4 << 20)
    args = (a, b) + ((res,) if has_res else ())
    specs = [pl.BlockSpec(a_blk, a_map), pl.BlockSpec(b_blk, b_map)] + ([o_spec] if has_res else [])
    return pl.pallas_call(body, grid=(m // tm, n // tn), in_specs=specs, out_specs=o_spec, out_shape=o_shape, name=name,
                          compiler_params=_params(("parallel", "parallel"), vmem))(*args)


def _sigmoid(x):
    return 1.0 / (1.0 + jnp.exp(-x))


def _rms_fwd(x, g, *, col=None, name):
    s = x.shape[0]
    w, ci = (x.shape[1], 0) if col is None else col
    tm = min(s, 512)

    def body(x_ref, g_ref, o_ref):
        xv = x_ref[...].astype(F32)
        r = lax.rsqrt(jnp.mean(xv * xv, axis=-1, keepdims=True) + EPS)
        o_ref[...] = (xv * r * g_ref[...]).astype(o_ref.dtype)

    return pl.pallas_call(
        body, grid=(s // tm,),
        in_specs=[pl.BlockSpec((tm, w), lambda i: (i, ci)), pl.BlockSpec((1, w), lambda i: (0, 0))],
        out_specs=pl.BlockSpec((tm, w), lambda i: (i, 0)), out_shape=SDS((s, w), BF16), name=name,
        compiler_params=_params(("parallel",), 10 * tm * w * 4))(x, g.reshape(1, w))


def _rms_bwd(x, g, dy, dres=None, *, col=None, name):
    s = x.shape[0]
    w, ci = (x.shape[1], 0) if col is None else col
    tm = min(s, 512)
    has_res = dres is not None

    def body(*refs):
        x_ref, g_ref, dy_ref = refs[:3]
        dx_ref, dxb_ref, dg_ref = refs[-3:]
        xv = x_ref[...].astype(F32)
        dyv = dy_ref[...].astype(F32)
        r = lax.rsqrt(jnp.mean(xv * xv, axis=-1, keepdims=True) + EPS)
        u = dyv * g_ref[...]
        dx = r * u - xv * (r * r * r) * jnp.mean(xv * u, axis=-1, keepdims=True)
        if has_res:
            dx = dx + refs[3][...]
        dx_ref[...] = dx
        dxb_ref[...] = dx.astype(BF16)

        @pl.when(pl.program_id(0) == 0)
        def _():
            dg_ref[...] = jnp.zeros_like(dg_ref)

        dg_ref[...] += jnp.sum(dyv * xv * r, axis=0, keepdims=True)

    blk = pl.BlockSpec((tm, w), lambda i: (i, 0))
    specs = [pl.BlockSpec((tm, w), lambda i: (i, ci)), pl.BlockSpec((1, w), lambda i: (0, 0)), blk]
    args = [x, g.reshape(1, w), dy]
    if has_res:
        specs.append(blk)
        args.append(dres)
    dx, dxb, dg = pl.pallas_call(
        body, grid=(s // tm,), in_specs=specs,
        out_specs=(blk, blk, pl.BlockSpec((1, w), lambda i: (0, 0))),
        out_shape=(SDS((s, w), F32), SDS((s, w), BF16), SDS((1, w), F32)), name=name,
        compiler_params=_params(("arbitrary",), 18 * tm * w * 4))(*args)
    return dx, dxb, dg.reshape(w)


def _gated_rms_fwd(y, proj, g, *, name):
    s, w = y.shape
    tm = min(s, 512)

    def body(y_ref, z_ref, g_ref, o_ref):
        z = z_ref[...]
        t = y_ref[...] * (z * _sigmoid(z))
        r = lax.rsqrt(jnp.mean(t * t, axis=-1, keepdims=True) + EPS)
        o_ref[...] = (t * r * g_ref[...]).astype(o_ref.dtype)

    blk = pl.BlockSpec((tm, w), lambda i: (i, 0))
    return pl.pallas_call(
        body, grid=(s // tm,),
        in_specs=[blk, pl.BlockSpec((tm, w), lambda i: (i, P_Z // w)), pl.BlockSpec((1, w), lambda i: (0, 0))],
        out_specs=blk, out_shape=SDS((s, w), BF16), name=name,
        compiler_params=_params(("parallel",), 14 * tm * w * 4))(y, proj, g.reshape(1, w))


def _gated_rms_bwd(y, proj, g, dout, *, name):
    s, w = y.shape
    tm = min(s, 512)

    def body(y_ref, z_ref, g_ref, do_ref, dy_ref, dz_ref, dg_ref):
        z = z_ref[...]
        yv = y_ref[...]
        dov = do_ref[...]
        sg = _sigmoid(z)
        sz = z * sg
        t = yv * sz
        r = lax.rsqrt(jnp.mean(t * t, axis=-1, keepdims=True) + EPS)
        u = dov * g_ref[...]
        dt = r * u - t * (r * r * r) * jnp.mean(t * u, axis=-1, keepdims=True)
        dy_ref[...] = dt * sz
        dz_ref[...] = (dt * yv * (sg * (1.0 + z * (1.0 - sg)))).astype(dz_ref.dtype)

        @pl.when(pl.program_id(0) == 0)
        def _():
            dg_ref[...] = jnp.zeros_like(dg_ref)

        dg_ref[...] += jnp.sum(dov * t * r, axis=0, keepdims=True)

    blk = pl.BlockSpec((tm, w), lambda i: (i, 0))
    vec = pl.BlockSpec((1, w), lambda i: (0, 0))
    dy, dz, dg = pl.pallas_call(
        body, grid=(s // tm,),
        in_specs=[blk, pl.BlockSpec((tm, w), lambda i: (i, P_Z // w)), vec, blk],
        out_specs=(blk, blk, vec), out_shape=(SDS((s, w), F32), SDS((s, w), BF16), SDS((1, w), F32)), name=name,
        compiler_params=_params(("arbitrary",), 24 * tm * w * 4))(y, proj, g.reshape(1, w), dout)
    return dy, dz, dg.reshape(w)


def _final_loss(x, g, target, *, name):
    s, w = x.shape
    tm = min(s, 512)

    def body(x_ref, g_ref, t_ref, loss_ref, dx_ref, dxb_ref, dg_ref):
        xv = x_ref[...]
        gv = g_ref[...]
        r = lax.rsqrt(jnp.mean(xv * xv, axis=-1, keepdims=True) + EPS)
        xn = xv * r
        diff = xn * gv - t_ref[...]
        dy = diff * (1.0 / w)
        u = dy * gv
        dx = r * u - xv * (r * r * r) * jnp.mean(xv * u, axis=-1, keepdims=True)
        dx_ref[...] = dx
        dxb_ref[...] = dx.astype(BF16)

        @pl.when(pl.program_id(0) == 0)
        def _():
            dg_ref[...] = jnp.zeros_like(dg_ref)
            loss_ref[...] = jnp.zeros_like(loss_ref)

        dg_ref[...] += jnp.sum(dy * xn, axis=0, keepdims=True)
        part = jnp.sum(jnp.sum(diff * diff, axis=1, keepdims=True), axis=0, keepdims=True) * (0.5 / w)
        loss_ref[...] += jnp.broadcast_to(part, loss_ref.shape)

    blk = pl.BlockSpec((tm, w), lambda i: (i, 0))
    vec = pl.BlockSpec((1, w), lambda i: (0, 0))
    loss, dx, dxb, dg = pl.pallas_call(
        body, grid=(s // tm,), in_specs=[blk, vec, blk],
        out_specs=(pl.BlockSpec((1, LANES), lambda i: (0, 0)), blk, blk, vec),
        out_shape=(SDS((1, LANES), F32), SDS((s, w), F32), SDS((s, w), BF16), SDS((1, w), F32)), name=name,
        compiler_params=_params(("arbitrary",), 18 * tm * w * 4))(x, g.reshape(1, w), target)
    return loss[0, 0], dx, dxb, dg.reshape(w)


def _shift_down(x, k):
    if k == 0:
        return x
    row = lax.broadcasted_iota(jnp.int32, x.shape, 0)
    return jnp.where(row < k, 0.0, pltpu.roll(x, k, axis=0))


def _shift_up(x, k):
    if k == 0:
        return x
    s = x.shape[0]
    row = lax.broadcasted_iota(jnp.int32, x.shape, 0)
    return jnp.where(row >= s - k, 0.0, pltpu.roll(x, s - k, axis=0))


def _conv_pre(x, w, b, kw):
    pre = b
    for j in range(kw):
        pre = pre + w[j:j + 1, :] * _shift_down(x, kw - 1 - j)
    return pre


def _conv_bwd_terms(x, w, dpre, kw):
    dx = jnp.zeros_like(x)
    dws = []
    for j in range(kw):
        dx = dx + w[j:j + 1, :] * _shift_up(dpre, kw - 1 - j)
        dws.append(jnp.sum(dpre * _shift_down(x, kw - 1 - j), axis=0, keepdims=True))
    return dx, jnp.concatenate(dws, axis=0), jnp.sum(dpre, axis=0, keepdims=True)


def _ssm_conv_fwd(proj, w, b, *, name):
    s = proj.shape[0]
    cw = 256

    def body(x_ref, w_ref, b_ref, o_ref):
        pre = _conv_pre(x_ref[...], w_ref[...], b_ref[...], SSM_CONV)
        o_ref[...] = pre * _sigmoid(pre)

    return pl.pallas_call(
        body, grid=(CONV_CH // cw,),
        in_specs=[pl.BlockSpec((s, cw), lambda j: (0, j)), pl.BlockSpec((SSM_CONV, cw), lambda j: (0, j)),
                  pl.BlockSpec((1, cw), lambda j: (0, j))],
        out_specs=pl.BlockSpec((s, cw), lambda j: (0, j)), out_shape=SDS((s, CONV_CH), F32), name=name,
        compiler_params=_params(("parallel",), 12 * s * cw * 4))(proj, w, b.reshape(1, CONV_CH))


def _ssm_conv_bwd(proj, w, b, dxbc, *, name):
    s = proj.shape[0]
    cw = 256

    def body(x_ref, w_ref, b_ref, dy_ref, dx_ref, dw_ref, db_ref):
        x = x_ref[...]
        wv = w_ref[...]
        pre = _conv_pre(x, wv, b_ref[...], SSM_CONV)
        sg = _sigmoid(pre)
        dpre = dy_ref[...] * (sg * (1.0 + pre * (1.0 - sg)))
        dx, dw, db = _conv_bwd_terms(x, wv, dpre, SSM_CONV)
        dx_ref[...] = dx.astype(dx_ref.dtype)
        dw_ref[...] = dw
        db_ref[...] = db

    col = pl.BlockSpec((s, cw), lambda j: (0, j))
    wsp = pl.BlockSpec((SSM_CONV, cw), lambda j: (0, j))
    bsp = pl.BlockSpec((1, cw), lambda j: (0, j))
    dx, dw, db = pl.pallas_call(
        body, grid=(CONV_CH // cw,), in_specs=[col, wsp, bsp, col], out_specs=(col, wsp, bsp),
        out_shape=(SDS((s, CONV_CH), BF16), SDS((SSM_CONV, CONV_CH), F32), SDS((1, CONV_CH), F32)), name=name,
        compiler_params=_params(("parallel",), 20 * s * cw * 4))(proj, w, b.reshape(1, CONV_CH), dxbc)
    return dx, dw, db.reshape(CONV_CH)


def _ffn_conv_fwd(up_g, up_v, w, b, *, name):
    s = up_g.shape[0]
    cw = 256
    nb = D_FF // cw

    def body(g_ref, v_ref, wg_ref, wv_ref, bg_ref, bv_ref, o_ref):
        gate = _conv_pre(g_ref[...], wg_ref[...], bg_ref[...], FFN_CONV)
        val = _conv_pre(v_ref[...], wv_ref[...], bv_ref[...], FFN_CONV)
        o_ref[...] = (gate * _sigmoid(gate) * val).astype(o_ref.dtype)

    col = pl.BlockSpec((s, cw), lambda j: (0, j))
    b2 = b.reshape(1, 2 * D_FF)
    return pl.pallas_call(
        body, grid=(nb,),
        in_specs=[col, col, pl.BlockSpec((FFN_CONV, cw), lambda j: (0, j)), pl.BlockSpec((FFN_CONV, cw), lambda j: (0, j + nb)),
                  pl.BlockSpec((1, cw), lambda j: (0, j)), pl.BlockSpec((1, cw), lambda j: (0, j + nb))],
        out_specs=col, out_shape=SDS((s, D_FF), BF16), name=name,
        compiler_params=_params(("parallel",), 16 * s * cw * 4))(up_g, up_v, w, w, b2, b2)


def _ffn_conv_bwd(up_g, up_v, w, b, dact, *, name):
    s = up_g.shape[0]
    cw = 256
    nb = D_FF // cw

    def body(g_ref, v_ref, wg_ref, wv_ref, bg_ref, bv_ref, da_ref, dg_ref, dv_ref, dwg_ref, dwv_ref, dbg_ref, dbv_ref):
        xg, xv = g_ref[...], v_ref[...]
        wg, wv = wg_ref[...], wv_ref[...]
        gate = _conv_pre(xg, wg, bg_ref[...], FFN_CONV)
        val = _conv_pre(xv, wv, bv_ref[...], FFN_CONV)
        da = da_ref[...].astype(F32)
        sg = _sigmoid(gate)
        dgate = da * val * (sg * (1.0 + gate * (1.0 - sg)))
        dval = da * gate * sg
        dxg, dwg, dbg = _conv_bwd_terms(xg, wg, dgate, FFN_CONV)
        dxv, dwv, dbv = _conv_bwd_terms(xv, wv, dval, FFN_CONV)
        dg_ref[...] = dxg.astype(dg_ref.dtype)
        dv_ref[...] = dxv.astype(dv_ref.dtype)
        dwg_ref[...] = dwg
        dwv_ref[...] = dwv
        dbg_ref[...] = dbg
        dbv_ref[...] = dbv

    col = pl.BlockSpec((s, cw), lambda j: (0, j))
    wsp = pl.BlockSpec((FFN_CONV, cw), lambda j: (0, j))
    bsp = pl.BlockSpec((1, cw), lambda j: (0, j))
    b2 = b.reshape(1, 2 * D_FF)
    dg, dv, dwg, dwv, dbg, dbv = pl.pallas_call(
        body, grid=(nb,),
        in_specs=[col, col, wsp, pl.BlockSpec((FFN_CONV, cw), lambda j: (0, j + nb)), bsp,
                  pl.BlockSpec((1, cw), lambda j: (0, j + nb)), col],
        out_specs=(col, col, wsp, wsp, bsp, bsp),
        out_shape=(SDS((s, D_FF), BF16), SDS((s, D_FF), BF16), SDS((FFN_CONV, D_FF), F32), SDS((FFN_CONV, D_FF), F32),
                   SDS((1, D_FF), F32), SDS((1, D_FF), F32)), name=name,
        compiler_params=_params(("parallel",), 32 * s * cw * 4))(up_g, up_v, w, w, b2, b2, dact)
    return dg, dv, jnp.concatenate([dwg, dwv], axis=1), jnp.concatenate([dbg, dbv], axis=1).reshape(2 * D_FF)


def _dot(a, b):
    return jnp.dot(a.astype(BF16), b.astype(BF16), preferred_element_type=F32)


def _dot_nt(a, b):
    return lax.dot_general(a.astype(BF16), b.astype(BF16), (((1,), (1,)), ((), ())), preferred_element_type=F32)


def _dot_tn(a, b):
    return lax.dot_general(a.astype(BF16), b.astype(BF16), (((0,), (0,)), ((), ())), preferred_element_type=F32)


def _ssd_chunk_terms(dtraw, bias, a_log):
    ell = dtraw.shape[0]
    lane = lax.broadcasted_iota(jnp.int32, dtraw.shape, 1)
    valid = lane < SSM_HEADS
    pre = dtraw + bias
    dt = jnp.where(valid, jnp.where(pre > 20.0, pre, jnp.log(1.0 + jnp.exp(jnp.minimum(pre, 20.0)))), 0.0)
    a = -jnp.exp(a_log)
    ad = dt * a
    row = lax.broadcasted_iota(jnp.int32, (ell, ell), 0)
    colm = lax.broadcasted_iota(jnp.int32, (ell, ell), 1)
    tril = row >= colm
    cs = jnp.dot(tril.astype(F32), ad, precision=HIGHEST, preferred_element_type=F32)
    cs_last = cs[ell - 1:ell, :]
    return pre, dt, a, cs, cs_last, tril


def _head_expand():
    h = lax.broadcasted_iota(jnp.int32, (LANES, D_SSM), 0)
    c = lax.broadcasted_iota(jnp.int32, (LANES, D_SSM), 1)
    return (c // SSM_HEAD_DIM == h).astype(F32)


def _ssd_fwd(xbc, proj, dt_bias, a_log, d_skip, *, name):
    s = xbc.shape[0]
    nc = s // SSM_CHUNK
    ell, n, p = SSM_CHUNK, SSM_STATE, SSM_HEAD_DIM
    rpg = SSM_HEADS // SSM_GROUPS
    gw = rpg * p

    def body(x_ref, dt_ref, bias_ref, alog_ref, dskip_ref, ex_ref, y_ref, ps_ref, state):
        @pl.when(pl.program_id(0) == 0)
        def _():
            state[...] = jnp.zeros_like(state)

        _, dt, _, cs, cs_last, tril = _ssd_chunk_terms(dt_ref[...], bias_ref[...], alog_ref[...])
        cst = cs.T
        ex = ex_ref[...]
        spread = lambda v: jnp.dot(v, ex, precision=HIGHEST, preferred_element_type=F32)
        dt_x, e_x, ds_x = spread(dt), spread(jnp.exp(cs)), spread(jnp.exp(cs_last - cs))
        cd_x = spread(jnp.broadcast_to(jnp.exp(cs_last), (8, LANES)))[0:1, :]
        dskip_x = spread(jnp.broadcast_to(dskip_ref[...], (8, LANES)))[0:1, :]
        st = state[...]
        ps_ref[0] = st
        xv = x_ref[...]
        xs_all = xv[:, 0:D_SSM]
        xd_all = xs_all * dt_x
        xdd_all = xd_all * ds_x
        lane_g = lax.broadcasted_iota(jnp.int32, (ell, gw), 1)
        ys, new = [], []
        for g in range(SSM_GROUPS):
            gs = slice(gw * g, gw * (g + 1))
            bg = xv[:, D_SSM + n * g:D_SSM + n * (g + 1)]
            cg = xv[:, D_SSM + n * (SSM_GROUPS + g):D_SSM + n * (SSM_GROUPS + g + 1)]
            cb = _dot_nt(cg, bg)
            xd_g, prev_g = xd_all[:, gs], st[:, gs]
            y_g = _dot(cg, prev_g) * e_x[:, gs] + xs_all[:, gs] * dskip_x[:, gs]
            for r in range(rpg):
                h = g * rpg + r
                lmat = jnp.exp(jnp.where(tril, cs[:, h:h + 1] - cst[h:h + 1, :], -jnp.inf))
                y_g = y_g + jnp.where((lane_g >= p * r) & (lane_g < p * (r + 1)), _dot(cb * lmat, xd_g), 0.0)
            ys.append(y_g)
            new.append(prev_g * cd_x[:, gs] + _dot(bg.T, xdd_all[:, gs]))
        y_ref[...] = jnp.concatenate(ys, axis=1)
        state[...] = jnp.concatenate(new, axis=1)

    vec = pl.BlockSpec((1, LANES), lambda c: (0, 0))
    return pl.pallas_call(
        body, grid=(nc,),
        in_specs=[pl.BlockSpec((ell, CONV_CH), lambda c: (c, 0)), pl.BlockSpec((ell, LANES), lambda c: (c, P_DT // LANES)),
                  vec, vec, vec, pl.BlockSpec((LANES, D_SSM), lambda c: (0, 0))],
        out_specs=(pl.BlockSpec((ell, D_SSM), lambda c: (c, 0)), pl.BlockSpec((1, n, D_SSM), lambda c: (c, 0, 0))),
        out_shape=(SDS((s, D_SSM), F32), SDS((nc, n, D_SSM), F32)),
        scratch_shapes=[pltpu.VMEM((n, D_SSM), F32)], name=name,
        compiler_params=_params(("arbitrary",), 32 << 20))(xbc, proj, dt_bias, a_log, d_skip, _head_expand())


def _ssd_bwd(xbc, proj, dt_bias, a_log, d_skip, prev_states, dy, *, name):
    s = xbc.shape[0]
    nc = s // SSM_CHUNK
    ell, n, p = SSM_CHUNK, SSM_STATE, SSM_HEAD_DIM
    rpg = SSM_HEADS // SSM_GROUPS
    gw = rpg * p

    def body(x_ref, dt_ref, bias_ref, alog_ref, dskip_ref, ps_ref, dy_ref, ex_ref, ext_ref,
             dx_ref, ddt_ref, dalog_ref, ddskip_ref, dbias_ref, dstate):
        @pl.when(pl.program_id(0) == 0)
        def _():
            dstate[...] = jnp.zeros_like(dstate)
            dalog_ref[...] = jnp.zeros_like(dalog_ref)
            ddskip_ref[...] = jnp.zeros_like(ddskip_ref)
            dbias_ref[...] = jnp.zeros_like(dbias_ref)

        pre, dt, a, cs, cs_last, tril = _ssd_chunk_terms(dt_ref[...], bias_ref[...], alog_ref[...])
        e = jnp.exp(cs)
        ds = jnp.exp(cs_last - cs)
        cd = jnp.exp(cs_last)
        cst = cs.T
        shape = (ell, LANES)
        ex, ext = ex_ref[...], ext_ref[...]
        spread = lambda v: jnp.dot(v, ex, precision=HIGHEST, preferred_element_type=F32)
        gather = lambda v: jnp.dot(v, ext, precision=HIGHEST, preferred_element_type=F32)
        dt_x, e_x, ds_x = spread(dt), spread(e), spread(ds)
        cd_x = spread(jnp.broadcast_to(cd, (8, LANES)))[0:1, :]
        dskip_x = spread(jnp.broadcast_to(dskip_ref[...], (8, LANES)))[0:1, :]
        xv, dyv, psv, dst = x_ref[...], dy_ref[...], ps_ref[0], dstate[...]
        xs_all = xv[:, 0:D_SSM]
        xd_all = xs_all * dt_x
        dye_all = dyv * e_x
        xdd_all = xd_all * ds_x
        triu = lax.broadcasted_iota(jnp.int32, (ell, ell), 0) <= lax.broadcasted_iota(jnp.int32, (ell, ell), 1)
        lane_g = lax.broadcasted_iota(jnp.int32, (ell, gw), 1)
        lane = lax.broadcasted_iota(jnp.int32, shape, 1)
        sub = lax.broadcasted_iota(jnp.int32, shape, 0)
        dcs_acc = jnp.zeros(shape, F32)
        dcs_rows = jnp.zeros(shape, F32)
        dxs, dbs, dcs_parts, dprevs, prod_a, prod_b, prod_c, prod_e = [], [], [], [], [], [], [], []
        for g in range(SSM_GROUPS):
            gs = slice(gw * g, gw * (g + 1))
            bg = xv[:, D_SSM + n * g:D_SSM + n * (g + 1)]
            cg = xv[:, D_SSM + n * (SSM_GROUPS + g):D_SSM + n * (SSM_GROUPS + g + 1)]
            cb = _dot_nt(cg, bg)
            cbt = _dot_nt(bg, cg)
            xs_g, dy_g, xd_g, dye_g, xdd_g = xs_all[:, gs], dyv[:, gs], xd_all[:, gs], dye_all[:, gs], xdd_all[:, gs]
            prev_g, dsn_g = psv[:, gs], dst[:, gs]
            cprev_g = _dot(cg, prev_g)
            dprevs.append(dsn_g * cd_x[:, gs] + _dot(cg.T, dye_g))
            dcg = _dot_nt(dye_g, prev_g)
            dxdd_g = _dot(bg, dsn_g)
            dbg = _dot_nt(xdd_g, dsn_g)
            dxd_g = dxdd_g * ds_x[:, gs]
            prod_a.append(dy_g * cprev_g)
            prod_b.append(dxdd_g * xd_g)
            prod_e.append(jnp.sum(dsn_g * prev_g, axis=0, keepdims=True))
            dcb = jnp.zeros((ell, ell), F32)
            for r in range(rpg):
                h = g * rpg + r
                mine = (lane_g >= p * r) & (lane_g < p * (r + 1))
                lmat = jnp.exp(jnp.where(tril, cs[:, h:h + 1] - cst[h:h + 1, :], -jnp.inf))
                lmat_t = jnp.exp(jnp.where(triu, cst[h:h + 1, :] - cs[:, h:h + 1], -jnp.inf))
                dgm = _dot_nt(jnp.where(mine, dy_g, 0.0), xd_g)
                dxd_g = dxd_g + jnp.where(mine, _dot(cbt * lmat_t, dy_g), 0.0)
                mm = dgm * (cb * lmat)
                dcs_acc = dcs_acc + jnp.where(lane == h, jnp.sum(mm, axis=1, keepdims=True), 0.0)
                dcs_rows = dcs_rows + jnp.where(sub == h, jnp.sum(mm, axis=0, keepdims=True), 0.0)
                dcb = dcb + dgm * lmat
            dxs.append(dxd_g * dt_x[:, gs] + dy_g * dskip_x[:, gs])
            prod_c.append(dxd_g * xs_g)
            dbs.append(dbg + _dot_tn(dcb, cg))
            dcs_parts.append(dcg + _dot(dcb, bg))
        dx_ref[...] = jnp.concatenate(dxs + dbs + dcs_parts, axis=1)
        dstate[...] = jnp.concatenate(dprevs, axis=1)
        sum_a = gather(jnp.concatenate(prod_a, axis=1))
        sum_b = gather(jnp.concatenate(prod_b, axis=1))
        sum_c = gather(jnp.concatenate(prod_c, axis=1))
        sum_d = gather(dyv * xs_all)
        dcd = gather(jnp.broadcast_to(jnp.concatenate(prod_e, axis=1), (8, D_SSM)))[0:1, :]
        tmp = sum_b * ds
        dlast = dcd * cd + jnp.sum(tmp, axis=0, keepdims=True)
        dcs = dcs_acc + sum_a * e - tmp - dcs_rows.T + jnp.where(sub == ell - 1, dlast, 0.0)
        dad = jnp.dot(triu.astype(F32), dcs, precision=HIGHEST, preferred_element_type=F32)
        ddt = sum_c + dad * a
        dalog_ref[...] += jnp.sum(dad * dt, axis=0, keepdims=True) * a
        ddskip_ref[...] += jnp.sum(sum_d, axis=0, keepdims=True)
        ddraw = jnp.where(lane < SSM_HEADS, ddt * _sigmoid(pre), 0.0)
        ddt_ref[...] = ddraw.astype(ddt_ref.dtype)
        dbias_ref[...] += jnp.sum(ddraw, axis=0, keepdims=True)

    vec = pl.BlockSpec((1, LANES), lambda c: (0, 0))
    rev = lambda c: nc - 1 - c
    ex = _head_expand()
    outs = pl.pallas_call(
        body, grid=(nc,),
        in_specs=[pl.BlockSpec((ell, CONV_CH), lambda c: (rev(c), 0)),
                  pl.BlockSpec((ell, LANES), lambda c: (rev(c), P_DT // LANES)), vec, vec, vec,
                  pl.BlockSpec((1, n, D_SSM), lambda c: (rev(c), 0, 0)),
                  pl.BlockSpec((ell, D_SSM), lambda c: (rev(c), 0)),
                  pl.BlockSpec((LANES, D_SSM), lambda c: (0, 0)), pl.BlockSpec((D_SSM, LANES), lambda c: (0, 0))],
        out_specs=(pl.BlockSpec((ell, CONV_CH), lambda c: (rev(c), 0)), pl.BlockSpec((ell, LANES), lambda c: (rev(c), 0)),
                   vec, vec, vec),
        out_shape=(SDS((s, CONV_CH), F32), SDS((s, LANES), BF16), SDS((1, LANES), F32), SDS((1, LANES), F32),
                   SDS((1, LANES), F32)),
        scratch_shapes=[pltpu.VMEM((n, D_SSM), F32)], name=name,
        compiler_params=_params(("arbitrary",), 40 << 20))(xbc, proj, dt_bias, a_log, d_skip, prev_states, dy, ex, ex.T)
    return outs


def _rope_swap(t):
    lane = lax.broadcasted_iota(jnp.int32, t.shape, 1)
    half = QK_ROPE // 2
    lo = (lane >= QK_NOPE) & (lane < QK_NOPE + half)
    hi = (lane >= QK_NOPE + half) & (lane < QK_NOPE + QK_ROPE)
    return jnp.where(lo, pltpu.roll(t, HEAD_PAD - half, axis=1), jnp.where(hi, pltpu.roll(t, half, axis=1), 0.0))


def _mla_prep(q, kv, proj, cos, sins, *, name):
    s = q.shape[0]
    tm = min(s, 256)
    scale = (QK_NOPE + QK_ROPE) ** -0.5

    def body(q_ref, kv_ref, kr_ref, cos_ref, sin_ref, qo_ref, ko_ref, vo_ref):
        cosv, sinv = cos_ref[...], sin_ref[...]
        kr = pltpu.roll(kr_ref[...], QK_NOPE, axis=1)
        lane = lax.broadcasted_iota(jnp.int32, kr.shape, 1)
        nope = lane < QK_NOPE
        kr = jnp.where(nope, 0.0, kr)
        kpe = kr * cosv + _rope_swap(kr) * sinv
        for hp in range(MLA_HEADS // 2):
            vs = []
            for h in (2 * hp, 2 * hp + 1):
                hs = slice(HEAD_PAD * h, HEAD_PAD * (h + 1))
                qh = q_ref[:, hs]
                kvh = kv_ref[:, hs]
                qo_ref[:, hs] = ((qh * cosv + _rope_swap(qh) * sinv) * scale).astype(qo_ref.dtype)
                ko_ref[:, hs] = (jnp.where(nope, kvh, 0.0) + kpe).astype(ko_ref.dtype)
                vs.append(kvh[:, QK_NOPE:])
            vo_ref[:, 2 * V_DIM * hp:2 * V_DIM * (hp + 1)] = jnp.concatenate(vs, axis=1).astype(vo_ref.dtype)

    wide = pl.BlockSpec((tm, MLA_HEADS * HEAD_PAD), lambda i: (i, 0))
    half = pl.BlockSpec((tm, MLA_HEADS * V_DIM), lambda i: (i, 0))
    tab = pl.BlockSpec((tm, LANES), lambda i: (i, 0))
    return pl.pallas_call(
        body, grid=(s // tm,),
        in_specs=[wide, wide, pl.BlockSpec((tm, LANES), lambda i: (i, P_KR // LANES)), tab, tab],
        out_specs=(wide, wide, half),
        out_shape=(SDS((s, MLA_HEADS * HEAD_PAD), BF16), SDS((s, MLA_HEADS * HEAD_PAD), BF16),
                   SDS((s, MLA_HEADS * V_DIM), BF16)), name=name,
        compiler_params=_params(("parallel",), 32 << 20))(q, kv, proj, cos, sins)


def _mla_prep_bwd(dqr, dkr, dv, cos, sins, *, name):
    s = dqr.shape[0]
    tm = min(s, 256)
    scale = (QK_NOPE + QK_ROPE) ** -0.5

    def body(dq_ref, dk_ref, dv_ref, cos_ref, sin_ref, dqo_ref, dkv_ref, dkr_ref):
        cosv, sinv = cos_ref[...], sin_ref[...]
        lane = lax.broadcasted_iota(jnp.int32, cosv.shape, 1)
        ksum = jnp.zeros(cosv.shape, F32)
        for h in range(MLA_HEADS):
            hs = slice(HEAD_PAD * h, HEAD_PAD * (h + 1))
            d = dq_ref[:, hs]
            dk = dk_ref[:, hs]
            dqo_ref[:, hs] = ((d * cosv + _rope_swap(d * sinv)) * scale).astype(dqo_ref.dtype)
            dkv_ref[:, hs] = jnp.concatenate([dk[:, :QK_NOPE], dv_ref[:, V_DIM * h:V_DIM * (h + 1)]], axis=1).astype(dkv_ref.dtype)
            ksum = ksum + dk
        ksum = jnp.where((lane >= QK_NOPE) & (lane < QK_NOPE + QK_ROPE), ksum, 0.0)
        un = ksum * cosv + _rope_swap(ksum * sinv)
        dkr_ref[...] = pltpu.roll(un, HEAD_PAD - QK_NOPE, axis=1).astype(dkr_ref.dtype)

    wide = pl.BlockSpec((tm, MLA_HEADS * HEAD_PAD), lambda i: (i, 0))
    half = pl.BlockSpec((tm, MLA_HEADS * V_DIM), lambda i: (i, 0))
    tab = pl.BlockSpec((tm, LANES), lambda i: (i, 0))
    return pl.pallas_call(
        body, grid=(s // tm,), in_specs=[wide, wide, half, tab, tab], out_specs=(wide, wide, tab),
        out_shape=(SDS((s, MLA_HEADS * HEAD_PAD), BF16), SDS((s, MLA_HEADS * HEAD_PAD), BF16), SDS((s, LANES), BF16)),
        name=name, compiler_params=_params(("parallel",), 40 << 20))(dqr, dkr, dv, cos, sins)


FLASH_TILE = 512
FLASH_ROWS = 32


def _flash_fwd(q, k, v, *, name):
    s = q.shape[0]
    t = min(s, FLASH_TILE)
    nq = s // t
    npair = MLA_HEADS // 2

    def body(q_ref, k_ref, v_ref, o_ref, lse_ref):
        i = pl.program_id(1)
        qs = [q_ref[:, HEAD_PAD * e:HEAD_PAD * (e + 1)] for e in range(2)]
        diag = lax.broadcasted_iota(jnp.int32, (t, t), 0) >= lax.broadcasted_iota(jnp.int32, (t, t), 1)

        def step(j, carry, masked):
            rows = pl.ds(pl.multiple_of(j * t, t), t)
            new = []
            for e in range(2):
                m, l, acc = carry[e]
                sc = _dot_nt(qs[e], k_ref[rows, HEAD_PAD * e:HEAD_PAD * (e + 1)])
                if masked:
                    sc = jnp.where(diag, sc, NEG)
                m_new = jnp.maximum(m, jnp.max(sc, axis=1, keepdims=True))
                pr = jnp.exp(sc - m_new)
                alpha = jnp.exp(m - m_new)
                l = alpha * l + jnp.sum(pr, axis=1, keepdims=True)
                acc = alpha * acc + _dot(pr, v_ref[rows, V_DIM * e:V_DIM * (e + 1)])
                new.append((m_new, l, acc))
            return tuple(new)

        init = tuple((jnp.full((t, 1), NEG, F32), jnp.zeros((t, 1), F32), jnp.zeros((t, V_DIM), F32)) for _ in range(2))
        carry = lax.fori_loop(0, i, functools.partial(step, masked=False), init)
        carry = step(i, carry, True)
        o_ref[...] = jnp.concatenate([acc / l for _, l, acc in carry], axis=1)
        lse_ref[0] = jnp.concatenate([jnp.broadcast_to(m + jnp.log(l), (t, V_DIM)) for m, l, _ in carry], axis=1)

    return pl.pallas_call(
        body, grid=(npair, nq),
        in_specs=[pl.BlockSpec((t, 2 * HEAD_PAD), lambda hp, i: (i, hp)), pl.BlockSpec((s, 2 * HEAD_PAD), lambda hp, i: (0, hp)),
                  pl.BlockSpec((s, 2 * V_DIM), lambda hp, i: (0, hp))],
        out_specs=(pl.BlockSpec((t, 2 * V_DIM), lambda hp, i: (i, hp)), pl.BlockSpec((1, t, LANES), lambda hp, i: (hp, i, 0))),
        out_shape=(SDS((s, MLA_HEADS * V_DIM), F32), SDS((npair, s, LANES), F32)), name=name,
        compiler_params=_params(("parallel", "parallel"), 40 << 20))(q, k, v)


def _flash_bwd(q, k, v, o, lse, do, *, name):
    s = q.shape[0]
    t = min(s, FLASH_TILE)
    nq = s // t
    npair = MLA_HEADS // 2
    nchunk = t // FLASH_ROWS

    def valid_cols(r):
        return min(t, -(-((r + 1) * FLASH_ROWS) // LANES) * LANES)

    def body(q_ref, k_ref, v_ref, o_ref, lse_ref, do_ref, dq_ref, dk_ref, dv_ref, s_scr, dp_scr, p_scr, ds_scr, dk_acc, dv_acc):
        j = pl.program_id(1)

        @pl.when(j == 0)
        def _():
            dq_ref[...] = jnp.zeros_like(dq_ref)

        dk_acc[...] = jnp.zeros(dk_acc.shape, F32)
        dv_acc[...] = jnp.zeros(dv_acc.shape, F32)
        qsl = [slice(HEAD_PAD * e, HEAD_PAD * (e + 1)) for e in range(2)]
        vsl = [slice(V_DIM * e, V_DIM * (e + 1)) for e in range(2)]

        def step(i, carry, masked):
            rows = pl.ds(pl.multiple_of(i * t, t), t)
            for e in range(2):
                ke = k_ref[:, qsl[e]]
                qi = q_ref[rows, qsl[e]]
                doi = do_ref[rows, vsl[e]]
                delta = jnp.sum(doi * o_ref[rows, vsl[e]], axis=1, keepdims=True)
                lse_i = lse_ref[0, rows, vsl[e]][:, 0:1]
                dob = doi.astype(BF16)
                s_scr[e] = _dot_nt(qi, ke)
                dp_scr[e] = _dot_nt(dob, v_ref[:, vsl[e]])
                for r in range(nchunk):
                    rs = slice(r * FLASH_ROWS, (r + 1) * FLASH_ROWS)
                    width = valid_cols(r) if masked else t
                    sc = s_scr[e, rs, 0:width]
                    if masked:
                        row = r * FLASH_ROWS + lax.broadcasted_iota(jnp.int32, (FLASH_ROWS, width), 0)
                        sc = jnp.where(row >= lax.broadcasted_iota(jnp.int32, (FLASH_ROWS, width), 1), sc, NEG)
                    pr = jnp.exp(sc - lse_i[rs, :])
                    dsc = pr * (dp_scr[e, rs, 0:width] - delta[rs, :])
                    p_scr[e, rs, 0:width] = pr.astype(BF16)
                    ds_scr[e, rs, 0:width] = dsc.astype(BF16)
                    if width < t:
                        p_scr[e, rs, width:t] = jnp.zeros((FLASH_ROWS, t - width), BF16)
                        ds_scr[e, rs, width:t] = jnp.zeros((FLASH_ROWS, t - width), BF16)
                dv_acc[e] += _dot_tn(p_scr[e], dob)
                dk_acc[e] += _dot_tn(ds_scr[e], qi)
                dq_ref[rows, qsl[e]] += _dot(ds_scr[e], ke)
            return carry

        step(j, 0, True)
        lax.fori_loop(j + 1, nq, functools.partial(step, masked=False), 0)
        dk_ref[...] = jnp.concatenate([dk_acc[e] for e in range(2)], axis=1)
        dv_ref[...] = jnp.concatenate([dv_acc[e] for e in range(2)], axis=1)

    full_q = pl.BlockSpec((s, 2 * HEAD_PAD), lambda hp, j: (0, hp))
    full_v = pl.BlockSpec((s, 2 * V_DIM), lambda hp, j: (0, hp))
    blk_k = pl.BlockSpec((t, 2 * HEAD_PAD), lambda hp, j: (j, hp))
    blk_v = pl.BlockSpec((t, 2 * V_DIM), lambda hp, j: (j, hp))
    return pl.pallas_call(
        body, grid=(npair, nq),
        in_specs=[full_q, blk_k, blk_v, full_v, pl.BlockSpec((1, s, LANES), lambda hp, j: (hp, 0, 0)), full_v],
        out_specs=(full_q, blk_k, blk_v),
        out_shape=(SDS((s, MLA_HEADS * HEAD_PAD), F32), SDS((s, MLA_HEADS * HEAD_PAD), F32), SDS((s, MLA_HEADS * V_DIM), F32)),
        scratch_shapes=[pltpu.VMEM((2, t, t), F32), pltpu.VMEM((2, t, t), F32), pltpu.VMEM((2, t, t), BF16),
                        pltpu.VMEM((2, t, t), BF16), pltpu.VMEM((2, t, HEAD_PAD), F32), pltpu.VMEM((2, t, V_DIM), F32)],
        name=name, compiler_params=_params(("parallel", "arbitrary"), 48 << 20))(q, k, v, o, lse, do)


def _mem_attn_fwd(q, k, v, *, name):
    s = q.shape[0]
    tm = min(s, 512)
    ml = k.shape[0]
    scale = MEM_HEAD_DIM ** -0.5

    def body(q_ref, k_ref, v_ref, o_ref):
        for h in range(MEM_HEADS):
            hs = slice(MEM_HEAD_DIM * h, MEM_HEAD_DIM * (h + 1))
            sc = _dot_nt(q_ref[:, hs], k_ref[:, hs]) * scale
            pr = jnp.exp(sc - jnp.max(sc, axis=1, keepdims=True))
            pr = pr / jnp.sum(pr, axis=1, keepdims=True)
            o_ref[:, hs] = _dot(pr, v_ref[:, hs]).astype(o_ref.dtype)

    blk = pl.BlockSpec((tm, D_MODEL), lambda i: (i, 0))
    kv = pl.BlockSpec((ml, D_MODEL), lambda i: (0, 0))
    return pl.pallas_call(body, grid=(s // tm,), in_specs=[blk, kv, kv], out_specs=blk,
                          out_shape=SDS((s, D_MODEL), BF16), name=name,
                          compiler_params=_params(("parallel",), 24 << 20))(q, k, v)


def _mem_attn_bwd(q, k, v, do, *, name):
    s = q.shape[0]
    tm = min(s, 512)
    ml = k.shape[0]
    scale = MEM_HEAD_DIM ** -0.5

    def body(q_ref, k_ref, v_ref, do_ref, dq_ref, dk_ref, dv_ref):
        @pl.when(pl.program_id(0) == 0)
        def _():
            dk_ref[...] = jnp.zeros_like(dk_ref)
            dv_ref[...] = jnp.zeros_like(dv_ref)

        for h in range(MEM_HEADS):
            hs = slice(MEM_HEAD_DIM * h, MEM_HEAD_DIM * (h + 1))
            qh, kh, vh, doh = q_ref[:, hs], k_ref[:, hs], v_ref[:, hs], do_ref[:, hs]
            sc = _dot_nt(qh, kh) * scale
            pr = jnp.exp(sc - jnp.max(sc, axis=1, keepdims=True))
            pr = pr / jnp.sum(pr, axis=1, keepdims=True)
            dp = _dot_nt(doh, vh)
            dsc = pr * (dp - jnp.sum(pr * dp, axis=1, keepdims=True)) * scale
            dq_ref[:, hs] = _dot(dsc, kh).astype(dq_ref.dtype)
            dk_ref[:, hs] += _dot_tn(dsc, qh)
            dv_ref[:, hs] += _dot_tn(pr, doh)

    blk = pl.BlockSpec((tm, D_MODEL), lambda i: (i, 0))
    kv = pl.BlockSpec((ml, D_MODEL), lambda i: (0, 0))
    return pl.pallas_call(body, grid=(s // tm,), in_specs=[blk, kv, kv, blk], out_specs=(blk, kv, kv),
                          out_shape=(SDS((s, D_MODEL), BF16), SDS((ml, D_MODEL), F32), SDS((ml, D_MODEL), F32)), name=name,
                          compiler_params=_params(("arbitrary",), 32 << 20))(q, k, v, do)


MATS = (("w_in", (1024, 940), 1), ("w_uq", (384, 384), 1), ("w_ukv", (256, 512), 1), ("w_out", (512, 1024), 0),
        ("ssm_conv_w", (4, 512), 1),
        ("w_mq", (256, 1024), 0), ("w_mk", (256, 1024), 0), ("w_mv", (256, 1024), 0), ("w_mo", (256, 1024), 0),
        ("w_up", (1024, 1408), 1), ("w_down", (704, 1024), 0), ("ffn_conv_w", (3, 1408), 1))
GROUPS = {"proj": (0,), "mixer": (1, 2, 3, 4), "mem": (5, 6, 7, 8), "ffn": (9, 10, 11)}
UP_SHARD_COLS = 1408
F32_ON_WIRE = ("ssm_conv_w", "ffn_conv_w")
SMALL = (("norm_mix", 1024), ("ssm_conv_b", 2048), ("dt_bias", 16), ("a_log", 16), ("d_skip", 16), ("ssm_norm", 1024),
         ("q_norm", 384), ("kv_norm", 256), ("attn_out_norm", 1024), ("norm_mem_q", 1024), ("norm_mem_kv", 1024),
         ("norm_ffn", 1024), ("ffn_conv_b", 5632))
PACK_COLS = 1024


def _pad_cols(t, n):
    return jnp.pad(t, ((0, 0),) * (t.ndim - 1) + ((0, n - t.shape[-1]),))


def _w_in_to_padded(t):
    z, xbc, dt, cq, ckv, kr = jnp.split(t, (1024, 3072, 3088, 3472, 3728), axis=-1)
    return jnp.concatenate([xbc, z, cq, _pad_cols(dt, LANES), ckv, _pad_cols(kr, P_IN - P_KR)], axis=-1)


def _w_in_from_padded(t):
    return jnp.concatenate([t[..., P_Z:P_Z + 1024], t[..., P_XBC:P_XBC + 2048], t[..., P_DT:P_DT + SSM_HEADS],
                            t[..., P_CQ:P_CQ + Q_LORA], t[..., P_CKV:P_CKV + KV_LORA], t[..., P_KR:P_KR + QK_ROPE]], axis=-1)


def _cols_joined(g):
    return jnp.concatenate([g[j] for j in range(N_CHIPS)], axis=-1)


def _cols_by_chip(t, dtype):
    k = t.shape[0]
    return t.reshape(k, N_CHIPS, -1).transpose(1, 0, 2).astype(dtype)


def _rows_by_chip(t):
    return t.reshape(N_CHIPS, -1, t.shape[-1])


def _mixer_weights(gw):
    wl = {}
    uq = _cols_joined(gw["w_uq"]).reshape(Q_LORA, MLA_HEADS, QK_NOPE + QK_ROPE)
    wl["w_uq"] = _pad_cols(uq, HEAD_PAD).reshape(Q_LORA, MLA_HEADS * HEAD_PAD)
    wl["w_ukv"] = _cols_joined(gw["w_ukv"])
    wl["ssm_conv_w"] = _cols_joined(gw["ssm_conv_w"])
    return wl


def _layer_fwd(x0, mem, cos, sins, weights, sp, li):
    n = lambda t: f"l{li}_{t}"
    lead = ()
    sv = {"x0": x0}
    gw = dict(weights("proj", x0))
    w_in = _w_in_to_padded(_cols_joined(gw["w_in"]))
    h = _rms_fwd(x0, sp["norm_mix"], name=n("mix_norm"))
    proj = _mm(h, w_in, name=n("mix_proj"))
    gw.update(weights("mixer", proj))
    wl = dict(_mixer_weights(gw), w_in=w_in)
    xbc = _ssm_conv_fwd(proj, wl["ssm_conv_w"], sp["ssm_conv_b"], name=n("ssm_conv"))
    y, pstates = _ssd_fwd(xbc, proj, sp["dt_bias"], sp["a_log"], sp["d_skip"], name=n("ssd"))
    y_ssm = _gated_rms_fwd(y, proj, sp["ssm_norm"], name=n("ssm_gate"))
    cqn = _rms_fwd(proj, sp["q_norm"], col=(Q_LORA, P_CQ // Q_LORA), name=n("q_norm"))
    ckvn = _rms_fwd(proj, sp["kv_norm"], col=(KV_LORA, P_CKV // KV_LORA), name=n("kv_norm"))
    q = _mm(cqn, wl["w_uq"], name=n("uq"))
    kv = _mm(ckvn, wl["w_ukv"], name=n("ukv"))
    qr, kr, v = _mla_prep(q, kv, proj, cos, sins, name=n("rope"))
    att, lse = _flash_fwd(qr, kr, v, name=n("flash"))
    y_att = _rms_fwd(att, sp["attn_out_norm"], name=n("att_norm"))
    x1 = _mm(y_ssm, gw["w_out"], b_lead=lead, b_rows=(0, D_SSM), res=x0, name=n("out_a"))
    x1 = _mm(y_att, gw["w_out"], b_lead=lead, b_rows=(D_SSM, D_SSM), res=x1, name=n("out_b"))
    sv.update(h=h, proj=proj, xbc=xbc, y=y, pstates=pstates, y_ssm=y_ssm, cqn=cqn, ckvn=ckvn, qr=qr, kr=kr, v=v,
              att=att, lse=lse, y_att=y_att, x1=x1)
    gw.update(weights("mem", x1))
    hq = _rms_fwd(x1, sp["norm_mem_q"], name=n("memq_norm"))
    hm = _rms_fwd(mem, sp["norm_mem_kv"], name=n("memkv_norm"))
    mq = _mm(hq, gw["w_mq"], b_lead=lead, out_dtype=BF16, name=n("mq"))
    mk = _mm(hm, gw["w_mk"], b_lead=lead, out_dtype=BF16, name=n("mk"))
    mv = _mm(hm, gw["w_mv"], b_lead=lead, out_dtype=BF16, name=n("mv"))
    mo = _mem_attn_fwd(mq, mk, mv, name=n("mem_attn"))
    x2 = _mm(mo, gw["w_mo"], b_lead=lead, res=x1, name=n("mo"))
    sv.update(hq=hq, hm=hm, mq=mq, mk=mk, mv=mv, mo=mo, x2=x2)
    gw.update(weights("ffn", x2))
    wl["ffn_conv_w"] = _cols_joined(gw["ffn_conv_w"])
    hf = _rms_fwd(x2, sp["norm_ffn"], name=n("ffn_norm"))
    up_g = _mm(hf, gw["w_up"], b_lead=lead, b_chips=(0, 2), name=n("up_g"))
    up_v = _mm(hf, gw["w_up"], b_lead=lead, b_chips=(2, 2), name=n("up_v"))
    act = _ffn_conv_fwd(up_g, up_v, wl["ffn_conv_w"], sp["ffn_conv_b"], name=n("ffn_conv"))
    x3 = _mm(act, gw["w_down"], b_lead=lead, res=x2, name=n("down"))
    sv.update(hf=hf, up_g=up_g, up_v=up_v, act=act)
    return x3, sv, gw, wl


def _layer_bwd(dx3, dx3b, mem, cos, sins, gw, wl, sp, sv, li, emit):
    n = lambda t: f"l{li}_b_{t}"
    lead = ()
    g = {}

    def after(token, v):
        return v if token is None else v + token[0, 0]

    dact = _mm(dx3b, gw["w_down"], tb=True, b_lead=lead, out_dtype=BF16, name=n("down_dx"))
    g["w_down"] = _rows_by_chip(_mm(sv["act"], dx3b, ta=True, out_dtype=BF16, name=n("down_dw")))
    dup_g, dup_v, dcw, g["ffn_conv_b"] = _ffn_conv_bwd(
        sv["up_g"], sv["up_v"], wl["ffn_conv_w"], sp["ffn_conv_b"], dact, name=n("ffn_conv"))
    g["ffn_conv_w"] = _cols_by_chip(dcw, F32)
    nsh = UP_SHARD_COLS
    dhf = None
    for c4 in range(N_CHIPS):
        dhf = _mm(dup_g if c4 < 2 else dup_v, gw["w_up"], tb=True, a_col=(nsh, c4 % 2), b_lead=(c4,), res=dhf,
                  name=n(f"up{c4}_dx"))
    g["w_up"] = jnp.concatenate([_mm(sv["hf"], dup_g, ta=True, o_chips=nsh, out_dtype=BF16, name=n("upg_dw")),
                                 _mm(sv["hf"], dup_v, ta=True, o_chips=nsh, out_dtype=BF16, name=n("upv_dw"))], axis=0)
    dx2, dx2b, g["norm_ffn"] = _rms_bwd(sv["x2"], after(emit("ffn", g), sp["norm_ffn"]), dhf, dx3, name=n("ffn_norm"))
    dmo = _mm(dx2b, gw["w_mo"], tb=True, b_lead=lead, out_dtype=BF16, name=n("mo_dx"))
    g["w_mo"] = _rows_by_chip(_mm(sv["mo"], dx2b, ta=True, out_dtype=BF16, name=n("mo_dw")))
    dmq, dmk, dmv = _mem_attn_bwd(sv["mq"], sv["mk"], sv["mv"], dmo, name=n("mem_attn"))
    dhq = _mm(dmq, gw["w_mq"], tb=True, b_lead=lead, name=n("mq_dx"))
    g["w_mq"] = _rows_by_chip(_mm(sv["hq"], dmq, ta=True, out_dtype=BF16, name=n("mq_dw")))
    dhm = _mm(dmk, gw["w_mk"], tb=True, b_lead=lead, name=n("mk_dx"))
    dhm = _mm(dmv, gw["w_mv"], tb=True, b_lead=lead, res=dhm, name=n("mv_dx"))
    g["w_mk"] = _rows_by_chip(_mm(sv["hm"], dmk, ta=True, out_dtype=BF16, name=n("mk_dw")))
    g["w_mv"] = _rows_by_chip(_mm(sv["hm"], dmv, ta=True, out_dtype=BF16, name=n("mv_dw")))
    dx1, dx1b, g["norm_mem_q"] = _rms_bwd(sv["x1"], after(emit("mem", g), sp["norm_mem_q"]), dhq, dx2, name=n("memq_norm"))
    _, _, g["norm_mem_kv"] = _rms_bwd(mem, sp["norm_mem_kv"], dhm, name=n("memkv_norm"))
    dy_ssm = _mm(dx1b, gw["w_out"], tb=True, b_lead=lead, b_rows=(0, D_SSM), name=n("outa_dx"))
    dy_att = _mm(dx1b, gw["w_out"], tb=True, b_lead=lead, b_rows=(D_SSM, D_SSM), name=n("outb_dx"))
    g["w_out"] = _rows_by_chip(jnp.concatenate([_mm(sv["y_ssm"], dx1b, ta=True, out_dtype=BF16, name=n("outa_dw")),
                                                _mm(sv["y_att"], dx1b, ta=True, out_dtype=BF16, name=n("outb_dw"))], axis=0))
    datt, _, g["attn_out_norm"] = _rms_bwd(sv["att"], sp["attn_out_norm"], dy_att, name=n("att_norm"))
    dqr, dkr, dv = _flash_bwd(sv["qr"], sv["kr"], sv["v"], sv["att"], sv["lse"], datt, name=n("flash"))
    dq, dkv, dkrope = _mla_prep_bwd(dqr, dkr, dv, cos, sins, name=n("rope"))
    duq = _mm(sv["cqn"], dq, ta=True, name=n("uq_dw")).reshape(Q_LORA, MLA_HEADS, HEAD_PAD)[..., :QK_NOPE + QK_ROPE]
    g["w_uq"] = _cols_by_chip(duq.reshape(Q_LORA, -1), BF16)
    dcqn = _mm(dq, wl["w_uq"], tb=True, name=n("uq_dx"))
    g["w_ukv"] = _cols_by_chip(_mm(sv["ckvn"], dkv, ta=True, name=n("ukv_dw")), BF16)
    dckvn = _mm(dkv, wl["w_ukv"], tb=True, name=n("ukv_dx"))
    proj = sv["proj"]
    _, dcq, g["q_norm"] = _rms_bwd(proj, sp["q_norm"], dcqn, col=(Q_LORA, P_CQ // Q_LORA), name=n("q_norm"))
    _, dckv, g["kv_norm"] = _rms_bwd(proj, sp["kv_norm"], dckvn, col=(KV_LORA, P_CKV // KV_LORA), name=n("kv_norm"))
    dy, dz, g["ssm_norm"] = _gated_rms_bwd(sv["y"], proj, sp["ssm_norm"], dy_ssm, name=n("ssm_gate"))
    dxbc, ddt, da_log, dd_skip, ddt_bias = _ssd_bwd(
        sv["xbc"], proj, sp["dt_bias"], sp["a_log"], sp["d_skip"], sv["pstates"], dy, name=n("ssd"))
    g["a_log"], g["d_skip"], g["dt_bias"] = da_log[0, :SSM_HEADS], dd_skip[0, :SSM_HEADS], ddt_bias[0, :SSM_HEADS]
    dxbc_pre, dsw, g["ssm_conv_b"] = _ssm_conv_bwd(proj, wl["ssm_conv_w"], sp["ssm_conv_b"], dxbc, name=n("ssm_conv"))
    g["ssm_conv_w"] = _cols_by_chip(dsw, F32)
    started = emit("mixer", g)
    s = proj.shape[0]
    dproj = jnp.concatenate([dxbc_pre, dz, dcq, ddt, dckv, dkrope,
                             jnp.zeros((s, P_IN - P_KR - LANES), BF16)], axis=1)
    dh = _mm(dproj, wl["w_in"], tb=True, name=n("proj_dx"))
    g["w_in"] = _cols_by_chip(_w_in_from_padded(_mm(sv["h"], dproj, ta=True, name=n("proj_dw"))), BF16)
    dx0, dx0b, g["norm_mix"] = _rms_bwd(sv["x0"], after(started, sp["norm_mix"]), dh, dx1, name=n("mix_norm"))
    return dx0, dx0b, g, emit("proj", g)


def _chip_peers(x, y):
    return [(1 - x, y), (x, 1 - y), (1 - x, 1 - y)]


HBM_SPEC = pl.BlockSpec(memory_space=pltpu.HBM)
SEM_SPEC = pl.BlockSpec(memory_space=pltpu.SEMAPHORE)
ANY_SPEC = pl.BlockSpec(memory_space=pl.ANY)
VMEM_SPEC = pl.BlockSpec(memory_space=pltpu.VMEM)
DATAFLOW = pltpu.SideEffectType.DATAFLOW_SIDE_EFFECTING
TOKEN_SHAPE = (8, LANES)


def _exchange_start(srcs, land_shapes, src_view, dst_view, token, *, name):
    n = len(srcs)

    def body(*refs):
        s, l, tok_in = refs[:n], refs[n:2 * n], refs[2 * n]
        send_sems, recv_sems = refs[2 * n + 1], refs[2 * n + 2]
        tok_out = refs[-1]
        x, y, c = lax.axis_index("x"), lax.axis_index("y"), lax.axis_index("c")
        me = 2 * x + y
        for t in range(n):
            for k, (px, py) in enumerate(_chip_peers(x, y)):
                pltpu.make_async_remote_copy(
                    src_ref=src_view(t, s[t], 2 * px + py), dst_ref=dst_view(t, l[t], me), send_sem=send_sems.at[3 * t + k],
                    recv_sem=recv_sems.at[3 * t + k], device_id=(px, py, c), device_id_type=MESH).start()
            pltpu.make_async_copy(src_view(t, s[t], me), dst_view(t, l[t], me), send_sems.at[3 * n + t]).start()
        tok_out[...] = tok_in[...]

    hbm = lambda t: pltpu.with_memory_space_constraint(t, pltpu.HBM)
    lands = [lax.empty(l.shape, l.dtype) for l in land_shapes]
    outs = pl.pallas_call(
        body, name=name,
        out_shape=(pltpu.SemaphoreType.DMA((4 * n,)), pltpu.SemaphoreType.DMA((3 * n,)),
                   *[pltpu.HBM(l.shape, l.dtype) for l in land_shapes], SDS(TOKEN_SHAPE, F32)),
        in_specs=[HBM_SPEC] * (2 * n) + [VMEM_SPEC], out_specs=(SEM_SPEC, SEM_SPEC, *[HBM_SPEC] * n, VMEM_SPEC),
        input_output_aliases={n + t: 2 + t for t in range(n)},
        compiler_params=pltpu.CompilerParams(has_side_effects=DATAFLOW))(*[hbm(t) for t in srcs], *[hbm(t) for t in lands], token)
    return outs[0], outs[1], list(outs[2:2 + n]), outs[-1]


def _exchange_wait(srcs, lands, send_sems, recv_sems, after, src_view, dst_view, which, *, name):
    n = len(srcs)
    m = len(which)

    def body(*refs):
        s, l = refs[:m], refs[m:2 * m]
        send_ref, recv_ref = refs[2 * m], refs[2 * m + 1]
        x, y, c = lax.axis_index("x"), lax.axis_index("y"), lax.axis_index("c")
        me = 2 * x + y
        for i, t in enumerate(which):
            for k, (px, py) in enumerate(_chip_peers(x, y)):
                chip = 2 * px + py
                cp = pltpu.make_async_remote_copy(
                    src_ref=src_view(t, s[i], chip), dst_ref=dst_view(t, l[i], chip), send_sem=send_ref.at[3 * t + k],
                    recv_sem=recv_ref.at[3 * t + k], device_id=(px, py, c), device_id_type=MESH)
                cp.wait_send()
                cp.wait_recv()
            pltpu.make_async_copy(src_view(t, s[i], me), dst_view(t, l[i], me), send_ref.at[3 * n + t]).wait()

    outs = pl.pallas_call(
        body, name=name, out_shape=[pltpu.HBM(lands[t].shape, lands[t].dtype) for t in which],
        in_specs=[HBM_SPEC] * (2 * m) + [SEM_SPEC, SEM_SPEC, ANY_SPEC], out_specs=[HBM_SPEC] * m,
        input_output_aliases={m + i: i for i in range(m)},
        compiler_params=pltpu.CompilerParams(has_side_effects=DATAFLOW))(
            *[srcs[t] for t in which], *[lands[t] for t in which], send_sems, recv_sems, after)
    return list(outs)


def _gather_layer_start(shards, li, token):
    src_view = lambda t, ref, chip: ref.at[li]
    dst_view = lambda t, ref, chip: ref.at[chip]
    send_sems, recv_sems, lands, token = _exchange_start(
        shards, [SDS((N_CHIPS,) + s.shape[1:], s.dtype) for s in shards], src_view, dst_view, token, name=f"gather{li}_start")
    return (shards, lands, send_sems, recv_sems, src_view, dst_view, f"gather{li}"), token


def _scatter_start(grads, tag, token):
    view = lambda t, ref, chip: ref.at[chip]
    send_sems, recv_sems, lands, token = _exchange_start(
        grads, [SDS(g.shape, g.dtype) for g in grads], view, view, token, name=f"scatter{tag}_start")
    return (grads, lands, send_sems, recv_sems, view, view, f"scatter{tag}"), token


def _exchange_finish(handle, after, which=None, tag=""):
    srcs, lands, send_sems, recv_sems, src_view, dst_view, name = handle
    which = tuple(range(len(srcs))) if which is None else which
    return _exchange_wait(srcs, lands, send_sems, recv_sems, after, src_view, dst_view, which, name=f"{name}{tag}_wait")


def _swap_start(bufs, token, *, name):
    n = len(bufs)

    def body(*refs):
        s, l, tok_in = refs[:n], refs[n:2 * n], refs[2 * n]
        send_sems, recv_sems = refs[2 * n + 1], refs[2 * n + 2]
        x, y, c = lax.axis_index("x"), lax.axis_index("y"), lax.axis_index("c")
        for t in range(n):
            pltpu.make_async_remote_copy(src_ref=s[t], dst_ref=l[t], send_sem=send_sems.at[t], recv_sem=recv_sems.at[t],
                                         device_id=(x, y, 1 - c), device_id_type=MESH).start()
        refs[-1][...] = tok_in[...]

    hbm = lambda t: pltpu.with_memory_space_constraint(t, pltpu.HBM)
    lands = [lax.empty(b.shape, b.dtype) for b in bufs]
    outs = pl.pallas_call(
        body, name=f"{name}_start",
        out_shape=(pltpu.SemaphoreType.DMA((n,)), pltpu.SemaphoreType.DMA((n,)),
                   *[pltpu.HBM(b.shape, b.dtype) for b in bufs], SDS(TOKEN_SHAPE, F32)),
        in_specs=[HBM_SPEC] * (2 * n) + [VMEM_SPEC], out_specs=(SEM_SPEC, SEM_SPEC, *[HBM_SPEC] * n, VMEM_SPEC),
        input_output_aliases={n + t: 2 + t for t in range(n)},
        compiler_params=pltpu.CompilerParams(has_side_effects=DATAFLOW))(*[hbm(t) for t in bufs], *[hbm(t) for t in lands], token)
    return (bufs, list(outs[2:2 + n]), outs[0], outs[1], name), outs[-1]


def _swap_wait(handle, after):
    bufs, lands, send_sems, recv_sems, name = handle
    n = len(bufs)

    def body(*refs):
        s, l = refs[:n], refs[n:2 * n]
        send_ref, recv_ref = refs[2 * n], refs[2 * n + 1]
        x, y, c = lax.axis_index("x"), lax.axis_index("y"), lax.axis_index("c")
        for t in range(n):
            cp = pltpu.make_async_remote_copy(src_ref=s[t], dst_ref=l[t], send_sem=send_ref.at[t], recv_sem=recv_ref.at[t],
                                              device_id=(x, y, 1 - c), device_id_type=MESH)
            cp.wait_send()
            cp.wait_recv()

    outs = pl.pallas_call(
        body, name=f"{name}_wait", out_shape=[pltpu.HBM(b.shape, b.dtype) for b in bufs],
        in_specs=[HBM_SPEC] * (2 * n) + [SEM_SPEC, SEM_SPEC, ANY_SPEC], out_specs=[HBM_SPEC] * n,
        input_output_aliases={n + t: t for t in range(n)},
        compiler_params=pltpu.CompilerParams(has_side_effects=DATAFLOW))(*bufs, *lands, send_sems, recv_sems, after)
    return list(outs)


def _all_gather8(src, *, name):
    def body(src_ref, out_ref, send_sems, recv_sems, local_sem):
        x, y, c = lax.axis_index("x"), lax.axis_index("y"), lax.axis_index("c")
        me = 4 * x + 2 * y + c
        mine = pltpu.make_async_copy(src_ref, out_ref.at[me], local_sem)
        mine.start()

        def peer(k):
            return (x ^ (k >> 2 & 1), y ^ (k >> 1 & 1), c ^ (k & 1))

        sends = []
        for k in range(1, N_DEV):
            cp = pltpu.make_async_remote_copy(src_ref=src_ref, dst_ref=out_ref.at[me], send_sem=send_sems.at[k - 1],
                                              recv_sem=recv_sems.at[k - 1], device_id=peer(k), device_id_type=MESH)
            cp.start()
            sends.append(cp)
        for k in range(1, N_DEV):
            px, py, pc = peer(k)
            pltpu.make_async_remote_copy(src_ref=src_ref, dst_ref=out_ref.at[4 * px + 2 * py + pc],
                                         send_sem=send_sems.at[k - 1], recv_sem=recv_sems.at[k - 1],
                                         device_id=peer(k), device_id_type=MESH).wait_recv()
        for cp in sends:
            cp.wait_send()
        mine.wait()

    any_spec = pl.BlockSpec(memory_space=pl.ANY)
    return pl.pallas_call(
        body, in_specs=[any_spec], out_specs=any_spec, out_shape=SDS((N_DEV,) + src.shape, src.dtype),
        scratch_shapes=[pltpu.SemaphoreType.DMA((N_DEV - 1,)), pltpu.SemaphoreType.DMA((N_DEV - 1,)), pltpu.SemaphoreType.DMA],
        name=name)(src)


def _adam_terms(w, g, m, v):
    m = ADAM_B1 * m + (1.0 - ADAM_B1) * g
    v = ADAM_B2 * v + (1.0 - ADAM_B2) * (g * g)
    m_hat = m / (1.0 - ADAM_B1 ** ADAM_STEP)
    v_hat = v / (1.0 - ADAM_B2 ** ADAM_STEP)
    delta = -ADAM_LR * (m_hat / (jnp.sqrt(v_hat) + ADAM_EPS) + ADAM_WD * w)
    return delta, m, v


def _adamw_shard(mine, other, w, m, v, *, name):
    d, a, b = w.shape
    tr = next((t for t in (128, 64, 32, 16) if a % t == 0), a)

    def body(*refs):
        ga, gb = refs[:d], refs[d:2 * d]
        w_ref, m_ref, v_ref, g_ref, d_ref, nm_ref, nv_ref = refs[2 * d:]

        def plane(ref):
            return ((ref[0].astype(F32) + ref[1].astype(F32)) + ref[2].astype(F32)) + ref[3].astype(F32)

        for lp in range(d):
            @pl.when(pl.program_id(0) == lp)
            def _(lp=lp):
                g = plane(ga[lp]) + plane(gb[lp])
                delta, mn, vn = _adam_terms(w_ref[...], g, m_ref[...], v_ref[...])
                g_ref[...] = g
                d_ref[...] = delta
                nm_ref[...] = mn
                nv_ref[...] = vn

    gspecs = [pl.BlockSpec((N_CHIPS, tr, b), lambda l, i, lp=lp: (0, jnp.where(l == lp, i, 0), 0)) for lp in range(d)]
    blk = pl.BlockSpec((None, tr, b), lambda l, i: (l, i, 0))
    shp = SDS((d, a, b), F32)
    return pl.pallas_call(
        body, grid=(d, a // tr), in_specs=gspecs + gspecs + [blk, blk, blk], out_specs=(blk,) * 4, out_shape=(shp,) * 4,
        name=name, compiler_params=_params(("arbitrary", "arbitrary"), 48 << 20))(*mine, *other, w, m, v)


def _adamw_small(g8, w, m, v, *, name):
    n = w.shape[1]

    def body(g8_ref, w_ref, m_ref, v_ref, g_ref, d_ref, nm_ref, nv_ref):
        g = g8_ref[0]
        for k in range(1, N_DEV):
            g = g + g8_ref[k]
        delta, mn, vn = _adam_terms(w_ref[...], g, m_ref[...], v_ref[...])
        g_ref[...] = g
        d_ref[...] = delta
        nm_ref[...] = mn
        nv_ref[...] = vn

    shp = SDS((1, n), F32)
    return pl.pallas_call(body, out_shape=(shp,) * 4, name=name, compiler_params=_params(None, 24 << 20))(g8, w, m, v)


def _rope_tables(positions):
    inv_freq = 1.0 / (ROPE_THETA ** (jnp.arange(0, QK_ROPE, 2, dtype=F32) / QK_ROPE))
    ang = positions.astype(F32)[:, None] * inv_freq
    c, s = jnp.cos(ang), jnp.sin(ang)
    n = positions.shape[0]
    pad = jnp.zeros((n, HEAD_PAD - QK_NOPE - QK_ROPE), F32)
    cos = jnp.concatenate([jnp.ones((n, QK_NOPE), F32), c, c, pad], axis=1)
    sins = jnp.concatenate([jnp.zeros((n, QK_NOPE), F32), -s, s, pad], axis=1)
    return cos, sins


def _pad_lanes(v):
    return _pad_cols(v.reshape(1, -1), LANES)


def _local_step(x, mem, positions, weights, small, final_norm, loss_target, emit, token):
    cos, sins = _rope_tables(positions)
    saved, gws, wls, sps = [], [], [], []
    h = x
    for li in range(DEPTH):
        sp = {k: small[k][li] for k, _ in SMALL}
        if li == 0:
            sp["norm_mix"] = sp["norm_mix"] + token[0, 0]
        for k in ("dt_bias", "a_log", "d_skip"):
            sp[k] = _pad_lanes(sp[k])
        h, sv, gw, wl = _layer_fwd(h, mem, cos, sins, functools.partial(weights, li), sp, li)
        saved.append(sv)
        gws.append(gw)
        wls.append(wl)
        sps.append(sp)
    loss, dh, dhb, g_final = _final_loss(h, final_norm, loss_target, name="final_loss")
    grads = [None] * DEPTH
    started = None
    for li in reversed(range(DEPTH)):
        sp = sps[li]
        if started is not None:
            sp = dict(sp, ffn_conv_b=sp["ffn_conv_b"] + started[0, 0])
        dh, dhb, grads[li], started = _layer_bwd(dh, dhb, mem, cos, sins, gws[li], wls[li], sp, saved[li], li,
                                                 functools.partial(emit, li))
    return loss, dh, grads, g_final


def _gathered_views(which, lands):
    return {MATS[t][0]: (b.reshape(-1, b.shape[-1]) if MATS[t][2] == 0 else b) for t, b in zip(which, lands)}


def kernel(x, mem, positions, norm_mix, w_in, ssm_conv_w, ssm_conv_b, dt_bias, a_log, d_skip, ssm_norm, q_norm, w_uq, kv_norm, w_ukv, attn_out_norm, w_out, norm_mem_q, norm_mem_kv, w_mq, w_mk, w_mv, w_mo, norm_ffn, w_up, ffn_conv_w, ffn_conv_b, w_down, final_norm, loss_target, m_norm_mix, m_w_in, m_ssm_conv_w, m_ssm_conv_b, m_dt_bias, m_a_log, m_d_skip, m_ssm_norm, m_q_norm, m_w_uq, m_kv_norm, m_w_ukv, m_attn_out_norm, m_w_out, m_norm_mem_q, m_norm_mem_kv, m_w_mq, m_w_mk, m_w_mv, m_w_mo, m_norm_ffn, m_w_up, m_ffn_conv_w, m_ffn_conv_b, m_w_down, m_final_norm, v_norm_mix, v_w_in, v_ssm_conv_w, v_ssm_conv_b, v_dt_bias, v_a_log, v_d_skip, v_ssm_norm, v_q_norm, v_w_uq, v_kv_norm, v_w_ukv, v_attn_out_norm, v_w_out, v_norm_mem_q, v_norm_mem_kv, v_w_mq, v_w_mk, v_w_mv, v_w_mo, v_norm_ffn, v_w_up, v_ffn_conv_w, v_ffn_conv_b, v_w_down, v_final_norm):
    args = dict(locals())
    names = ["norm_mix", "w_in", "ssm_conv_w", "ssm_conv_b", "dt_bias", "a_log", "d_skip", "ssm_norm", "q_norm", "w_uq",
             "kv_norm", "w_ukv", "attn_out_norm", "w_out", "norm_mem_q", "norm_mem_kv", "w_mq", "w_mk", "w_mv", "w_mo",
             "norm_ffn", "w_up", "ffn_conv_w", "ffn_conv_b", "w_down", "final_norm"]
    wts = {k: args[k] for k in names}
    mom = {k: args["m_" + k] for k in names}
    var = {k: args["v_" + k] for k in names}
    mat_names = [k for k, _, _ in MATS]

    shards = [wts[k] if k in F32_ON_WIRE else wts[k].astype(BF16) for k in mat_names]
    token = jnp.zeros(TOKEN_SHAPE, F32)
    gathers = []
    for li in range(DEPTH):
        handle, token = _gather_layer_start(shards, li, token)
        gathers.append(handle)
    small = {k: wts[k] for k, _ in SMALL}

    def weights(li, group, after):
        which = GROUPS[group]
        return _gathered_views(which, _exchange_finish(gathers[li], after, which, tag=f"_{group}"))

    scatters = [[] for _ in range(DEPTH)]
    nm = len(mat_names)
    mine = [[None] * nm for _ in range(DEPTH)]
    swaps = [None] * DEPTH

    def swap_layer(li, after):
        for which, handle in scatters[li]:
            for t, b in zip(which, _exchange_finish(handle, after)):
                mine[li][t] = b
        swaps[li], started = _swap_start(mine[li], jnp.zeros(TOKEN_SHAPE, F32), name=f"swap{li}")
        return started

    def emit(li, group, g):
        last = group == "proj"
        if li == 0:
            which = GROUPS[group]
        elif last:
            which = tuple(range(nm))
        else:
            return None
        handle, started = _scatter_start([g[MATS[t][0]] for t in which], f"{li}_{group}", jnp.zeros(TOKEN_SHAPE, F32))
        scatters[li].append((which, handle))
        if li + 1 < DEPTH and group == ("mixer" if li == 0 else "proj"):
            started = started + swap_layer(li + 1, g["ssm_conv_w"])
        return started

    loss, grad_x, grads, g_final = _local_step(x[0], mem[0], positions[0], weights, small, wts["final_norm"],
                                               loss_target[0], emit, token)
    loss = lax.psum(loss, ("x", "y", "c"))

    swap_layer(0, grad_x)
    other = [_swap_wait(swaps[li], grad_x) for li in range(DEPTH)]
    mat_out = {k: _adamw_shard([mine[li][t] for li in range(DEPTH)], [other[li][t] for li in range(DEPTH)],
                               wts[k], mom[k], var[k], name=f"adamw_{k}") for t, k in enumerate(mat_names)}

    def pack_small(get, fin):
        flat = [get(k).reshape(-1) for k, _ in SMALL] + [fin.reshape(-1)]
        n = sum(f.shape[0] for f in flat)
        return jnp.concatenate(flat + [jnp.zeros((-n % PACK_COLS,), F32)]).reshape(1, -1)

    gs = pack_small(lambda k: jnp.stack([grads[li][k] for li in range(DEPTH)]), g_final)
    g8 = _all_gather8(gs, name="gather_small_grads")
    small_out = _adamw_small(g8, pack_small(wts.get, wts["final_norm"]), pack_small(mom.get, mom["final_norm"]),
                             pack_small(var.get, var["final_norm"]), name="adamw_small")

    def unpack_small(buf):
        out, off = {}, 0
        for k, nel in SMALL:
            out[k] = buf[0, off:off + DEPTH * nel].reshape(DEPTH, nel)
            off += DEPTH * nel
        out["final_norm"] = buf[0, off:off + D_MODEL]
        return out

    small_res = [unpack_small(b) for b in small_out]
    res = []
    for kind in range(4):
        for k in names:
            res.append(small_res[kind][k] if k in small_res[kind] else mat_out[k][kind])
    return (loss, grad_x[None], *res)
```

```python
import functools
import math

import jax
import jax.numpy as jnp
from jax import lax
from jax.experimental import pallas as pl
from jax.experimental.pallas import tpu as pltpu

F32 = jnp.float32
BF16 = jnp.bfloat16
HIGHEST = lax.Precision.HIGHEST
SDS = jax.ShapeDtypeStruct
MESH = pl.DeviceIdType.MESH

D_MODEL = 1024
DEPTH = 4
EPS = 1e-6
SSM_HEADS = 16
SSM_HEAD_DIM = 64
D_SSM = 1024
SSM_GROUPS = 4
SSM_STATE = 128
SSM_CONV = 4
SSM_CHUNK = 128
CONV_CH = 2048
MLA_HEADS = 16
QK_NOPE = 64
QK_ROPE = 32
V_DIM = 64
Q_LORA = 384
KV_LORA = 256
ROPE_THETA = 10000.0
MEM_HEADS = 4
MEM_HEAD_DIM = 256
D_FF = 2816
FFN_CONV = 3
D_IN = 3760
ADAM_LR = 0.001
ADAM_B1 = 0.9
ADAM_B2 = 0.999
ADAM_EPS = 1e-08
ADAM_WD = 0.01
ADAM_STEP = 10

LANES = 128
HEAD_PAD = 128
N_CHIPS = 4
N_DEV = 8
VMEM_CAP_MB = 56

P_XBC, P_Z, P_CQ, P_DT, P_CKV, P_KR, P_IN = 0, 2048, 3072, 3456, 3584, 3840, 4096
NEG = -1e30


def _tile(n, pref):
    t = (min(n, pref) // LANES) * LANES
    while t >= LANES:
        if n % t == 0:
            return t
        t -= LANES
    return n


def _params(sem=None, vmem_bytes=None):
    kw = {}
    if sem is not None:
        kw["dimension_semantics"] = sem
    if vmem_bytes is not None:
        kw["vmem_limit_bytes"] = int(min(max(vmem_bytes, 16 << 20), VMEM_CAP_MB << 20))
    return pltpu.CompilerParams(**kw)


def _nbytes(shape, dtype):
    return math.prod(shape) * jnp.dtype(dtype).itemsize


def _mm(a, b, *, ta=False, tb=False, res=None, out_dtype=F32, name, a_col=None, b_lead=(), b_rows=None,
        b_chips=None, o_chips=None):
    if ta:
        k, m = a.shape
    else:
        m, k = (a.shape[0], a.shape[1] if a_col is None else a_col[0])
    rows_b, cols_b = b.shape[-2:]
    row0 = 0
    if b_rows is not None:
        row0, rows_b = b_rows
    nlead = len(b_lead)
    if b_chips is not None:
        assert not tb
        kb, tn, n = rows_b, cols_b, b_chips[1] * cols_b
        b_blk = (None,) * (1 + nlead) + (kb, tn)
        b_map = lambda i, j: (b_chips[0] + j,) + tuple(b_lead) + (0, 0)
    elif tb:
        n, kb = rows_b, cols_b
        tn = _tile(n, 512)
        assert row0 % tn == 0
        b_blk = (None,) * nlead + (tn, kb)
        b_map = lambda i, j: tuple(b_lead) + (j + row0 // tn, 0)
    else:
        kb, n = rows_b, cols_b
        tn = o_chips if o_chips else _tile(n, 512)
        assert row0 % kb == 0
        b_blk = (None,) * nlead + (kb, tn)
        b_map = lambda i, j: tuple(b_lead) + (row0 // kb, j)
    assert k == kb, (a.shape, b.shape, ta, tb, k, kb)
    tm = _tile(m, 512)
    if ta:
        a_blk, a_map = (k, tm), (lambda i, j: (0, i))
    else:
        a_blk, a_map = (tm, k), ((lambda i, j: (i, 0)) if a_col is None else (lambda i, j: (i, a_col[1])))
    if o_chips:
        o_spec = pl.BlockSpec((None, tm, tn), lambda i, j: (j, i, 0))
        o_shape = SDS((n // tn, m, tn), out_dtype)
    else:
        o_spec = pl.BlockSpec((tm, tn), lambda i, j: (i, j))
        o_shape = SDS((m, n), out_dtype)
    dims = (((0 if ta else 1,), (1 if tb else 0,)), ((), ()))
    has_res = res is not None

    def body(*refs):
        a_ref, b_ref = refs[0], refs[1]
        o_ref = refs[-1]
        acc = lax.dot_general(a_ref[...].astype(BF16), b_ref[...].astype(BF16), dims, preferred_element_type=F32)
        if has_res:
            acc = acc + refs[2][...]
        o_ref[...] = acc.astype(o_ref.dtype)

    bb = tuple(d for d in b_blk if d is not None)
    vmem = 2 * (_nbytes(a_blk, a.dtype) + _nbytes(bb, b.dtype) + (2 if has_res else 1) * _nbytes((tm, tn), F32))
    vmem += _nbytes(a_blk, BF16) + _nbytes(bb, BF16) + 2 * _nbytes((tm, tn), F32) + (4 << 20)
    args = (a, b) + ((res,) if has_res else ())
    specs = [pl.BlockSpec(a_blk, a_map), pl.BlockSpec(b_blk, b_map)] + ([o_spec] if has_res else [])
    return pl.pallas_call(body, grid=(m // tm, n // tn), in_specs=specs, out_specs=o_spec, out_shape=o_shape, name=name,
                          compiler_params=_params(("parallel", "parallel"), vmem))(*args)


def _sigmoid(x):
    return 1.0 / (1.0 + jnp.exp(-x))


def _rms_fwd(x, g, *, col=None, name):
    s = x.shape[0]
    w, ci = (x.shape[1], 0) if col is None else col
    tm = min(s, 512)

    def body(x_ref, g_ref, o_ref):
        xv = x_ref[...].astype(F32)
        r = lax.rsqrt(jnp.mean(xv * xv, axis=-1, keepdims=True) + EPS)
        o_ref[...] = (xv * r * g_ref[...]).astype(o_ref.dtype)

    return pl.pallas_call(
        body, grid=(s // tm,),
        in_specs=[pl.BlockSpec((tm, w), lambda i: (i, ci)), pl.BlockSpec((1, w), lambda i: (0, 0))],
        out_specs=pl.BlockSpec((tm, w), lambda i: (i, 0)), out_shape=SDS((s, w), BF16), name=name,
        compiler_params=_params(("parallel",), 10 * tm * w * 4))(x, g.reshape(1, w))


def _rms_bwd(x, g, dy, dres=None, *, col=None, name):
    s = x.shape[0]
    w, ci = (x.shape[1], 0) if col is None else col
    tm = min(s, 512)
    has_res = dres is not None

    def body(*refs):
        x_ref, g_ref, dy_ref = refs[:3]
        dx_ref, dxb_ref, dg_ref = refs[-3:]
        xv = x_ref[...].astype(F32)
        dyv = dy_ref[...].astype(F32)
        r = lax.rsqrt(jnp.mean(xv * xv, axis=-1, keepdims=True) + EPS)
        u = dyv * g_ref[...]
        dx = r * u - xv * (r * r * r) * jnp.mean(xv * u, axis=-1, keepdims=True)
        if has_res:
            dx = dx + refs[3][...]
        dx_ref[...] = dx
        dxb_ref[...] = dx.astype(BF16)

        @pl.when(pl.program_id(0) == 0)
        def _():
            dg_ref[...] = jnp.zeros_like(dg_ref)

        dg_ref[...] += jnp.sum(dyv * xv * r, axis=0, keepdims=True)

    blk = pl.BlockSpec((tm, w), lambda i: (i, 0))
    specs = [pl.BlockSpec((tm, w), lambda i: (i, ci)), pl.BlockSpec((1, w), lambda i: (0, 0)), blk]
    args = [x, g.reshape(1, w), dy]
    if has_res:
        specs.append(blk)
        args.append(dres)
    dx, dxb, dg = pl.pallas_call(
        body, grid=(s // tm,), in_specs=specs,
        out_specs=(blk, blk, pl.BlockSpec((1, w), lambda i: (0, 0))),
        out_shape=(SDS((s, w), F32), SDS((s, w), BF16), SDS((1, w), F32)), name=name,
        compiler_params=_params(("arbitrary",), 18 * tm * w * 4))(*args)
    return dx, dxb, dg.reshape(w)


def _gated_rms_fwd(y, proj, g, *, name):
    s, w = y.shape
    tm = min(s, 512)

    def body(y_ref, z_ref, g_ref, o_ref):
        z = z_ref[...]
        t = y_ref[...] * (z * _sigmoid(z))
        r = lax.rsqrt(jnp.mean(t * t, axis=-1, keepdims=True) + EPS)
        o_ref[...] = (t * r * g_ref[...]).astype(o_ref.dtype)

    blk = pl.BlockSpec((tm, w), lambda i: (i, 0))
    return pl.pallas_call(
        body, grid=(s // tm,),
        in_specs=[blk, pl.BlockSpec((tm, w), lambda i: (i, P_Z // w)), pl.BlockSpec((1, w), lambda i: (0, 0))],
        out_specs=blk, out_shape=SDS((s, w), BF16), name=name,
        compiler_params=_params(("parallel",), 14 * tm * w * 4))(y, proj, g.reshape(1, w))


def _gated_rms_bwd(y, proj, g, dout, *, name):
    s, w = y.shape
    tm = min(s, 512)

    def body(y_ref, z_ref, g_ref, do_ref, dy_ref, dz_ref, dg_ref):
        z = z_ref[...]
        yv = y_ref[...]
        dov = do_ref[...]
        sg = _sigmoid(z)
        sz = z * sg
        t = yv * sz
        r = lax.rsqrt(jnp.mean(t * t, axis=-1, keepdims=True) + EPS)
        u = dov * g_ref[...]
        dt = r * u - t * (r * r * r) * jnp.mean(t * u, axis=-1, keepdims=True)
        dy_ref[...] = dt * sz
        dz_ref[...] = (dt * yv * (sg * (1.0 + z * (1.0 - sg)))).astype(dz_ref.dtype)

        @pl.when(pl.program_id(0) == 0)
        def _():
            dg_ref[...] = jnp.zeros_like(dg_ref)

        dg_ref[...] += jnp.sum(dov * t * r, axis=0, keepdims=True)

    blk = pl.BlockSpec((tm, w), lambda i: (i, 0))
    vec = pl.BlockSpec((1, w), lambda i: (0, 0))
    dy, dz, dg = pl.pallas_call(
        body, grid=(s // tm,),
        in_specs=[blk, pl.BlockSpec((tm, w), lambda i: (i, P_Z // w)), vec, blk],
        out_specs=(blk, blk, vec), out_shape=(SDS((s, w), F32), SDS((s, w), BF16), SDS((1, w), F32)), name=name,
        compiler_params=_params(("arbitrary",), 24 * tm * w * 4))(y, proj, g.reshape(1, w), dout)
    return dy, dz, dg.reshape(w)


def _final_loss(x, g, target, *, name):
    s, w = x.shape
    tm = min(s, 512)

    def body(x_ref, g_ref, t_ref, loss_ref, dx_ref, dxb_ref, dg_ref):
        xv = x_ref[...]
        gv = g_ref[...]
        r = lax.rsqrt(jnp.mean(xv * xv, axis=-1, keepdims=True) + EPS)
        xn = xv * r
        diff = xn * gv - t_ref[...]
        dy = diff * (1.0 / w)
        u = dy * gv
        dx = r * u - xv * (r * r * r) * jnp.mean(xv * u, axis=-1, keepdims=True)
        dx_ref[...] = dx
        dxb_ref[...] = dx.astype(BF16)

        @pl.when(pl.program_id(0) == 0)
        def _():
            dg_ref[...] = jnp.zeros_like(dg_ref)
            loss_ref[...] = jnp.zeros_like(loss_ref)

        dg_ref[...] += jnp.sum(dy * xn, axis=0, keepdims=True)
        part = jnp.sum(jnp.sum(diff * diff, axis=1, keepdims=True), axis=0, keepdims=True) * (0.5 / w)
        loss_ref[...] += jnp.broadcast_to(part, loss_ref.shape)

    blk = pl.BlockSpec((tm, w), lambda i: (i, 0))
    vec = pl.BlockSpec((1, w), lambda i: (0, 0))
    loss, dx, dxb, dg = pl.pallas_call(
        body, grid=(s // tm,), in_specs=[blk, vec, blk],
        out_specs=(pl.BlockSpec((1, LANES), lambda i: (0, 0)), blk, blk, vec),
        out_shape=(SDS((1, LANES), F32), SDS((s, w), F32), SDS((s, w), BF16), SDS((1, w), F32)), name=name,
        compiler_params=_params(("arbitrary",), 18 * tm * w * 4))(x, g.reshape(1, w), target)
    return loss[0, 0], dx, dxb, dg.reshape(w)


def _shift_down(x, k):
    if k == 0:
        return x
    row = lax.broadcasted_iota(jnp.int32, x.shape, 0)
    return jnp.where(row < k, 0.0, pltpu.roll(x, k, axis=0))


def _shift_up(x, k):
    if k == 0:
        return x
    s = x.shape[0]
    row = lax.broadcasted_iota(jnp.int32, x.shape, 0)
    return jnp.where(row >= s - k, 0.0, pltpu.roll(x, s - k, axis=0))


def _conv_pre(x, w, b, kw):
    pre = b
    for j in range(kw):
        pre = pre + w[j:j + 1, :] * _shift_down(x, kw - 1 - j)
    return pre


def _conv_bwd_terms(x, w, dpre, kw):
    dx = jnp.zeros_like(x)
    dws = []
    for j in range(kw):
        dx = dx + w[j:j + 1, :] * _shift_up(dpre, kw - 1 - j)
        dws.append(jnp.sum(dpre * _shift_down(x, kw - 1 - j), axis=0, keepdims=True))
    return dx, jnp.concatenate(dws, axis=0), jnp.sum(dpre, axis=0, keepdims=True)


def _ssm_conv_fwd(proj, w, b, *, name):
    s = proj.shape[0]
    cw = 256

    def body(x_ref, w_ref, b_ref, o_ref):
        pre = _conv_pre(x_ref[...], w_ref[...], b_ref[...], SSM_CONV)
        o_ref[...] = pre * _sigmoid(pre)

    return pl.pallas_call(
        body, grid=(CONV_CH // cw,),
        in_specs=[pl.BlockSpec((s, cw), lambda j: (0, j)), pl.BlockSpec((SSM_CONV, cw), lambda j: (0, j)),
                  pl.BlockSpec((1, cw), lambda j: (0, j))],
        out_specs=pl.BlockSpec((s, cw), lambda j: (0, j)), out_shape=SDS((s, CONV_CH), F32), name=name,
        compiler_params=_params(("parallel",), 12 * s * cw * 4))(proj, w, b.reshape(1, CONV_CH))


def _ssm_conv_bwd(proj, w, b, dxbc, *, name):
    s = proj.shape[0]
    cw = 256

    def body(x_ref, w_ref, b_ref, dy_ref, dx_ref, dw_ref, db_ref):
        x = x_ref[...]
        wv = w_ref[...]
        pre = _conv_pre(x, wv, b_ref[...], SSM_CONV)
        sg = _sigmoid(pre)
        dpre = dy_ref[...] * (sg * (1.0 + pre * (1.0 - sg)))
        dx, dw, db = _conv_bwd_terms(x, wv, dpre, SSM_CONV)
        dx_ref[...] = dx.astype(dx_ref.dtype)
        dw_ref[...] = dw
        db_ref[...] = db

    col = pl.BlockSpec((s, cw), lambda j: (0, j))
    wsp = pl.BlockSpec((SSM_CONV, cw), lambda j: (0, j))
    bsp = pl.BlockSpec((1, cw), lambda j: (0, j))
    dx, dw, db = pl.pallas_call(
        body, grid=(CONV_CH // cw,), in_specs=[col, wsp, bsp, col], out_specs=(col, wsp, bsp),
        out_shape=(SDS((s, CONV_CH), BF16), SDS((SSM_CONV, CONV_CH), F32), SDS((1, CONV_CH), F32)), name=name,
        compiler_params=_params(("parallel",), 20 * s * cw * 4))(proj, w, b.reshape(1, CONV_CH), dxbc)
    return dx, dw, db.reshape(CONV_CH)


def _ffn_conv_fwd(up_g, up_v, w, b, *, name):
    s = up_g.shape[0]
    cw = 256
    nb = D_FF // cw

    def body(g_ref, v_ref, wg_ref, wv_ref, bg_ref, bv_ref, o_ref):
        gate = _conv_pre(g_ref[...], wg_ref[...], bg_ref[...], FFN_CONV)
        val = _conv_pre(v_ref[...], wv_ref[...], bv_ref[...], FFN_CONV)
        o_ref[...] = (gate * _sigmoid(gate) * val).astype(o_ref.dtype)

    col = pl.BlockSpec((s, cw), lambda j: (0, j))
    b2 = b.reshape(1, 2 * D_FF)
    return pl.pallas_call(
        body, grid=(nb,),
        in_specs=[col, col, pl.BlockSpec((FFN_CONV, cw), lambda j: (0, j)), pl.BlockSpec((FFN_CONV, cw), lambda j: (0, j + nb)),
                  pl.BlockSpec((1, cw), lambda j: (0, j)), pl.BlockSpec((1, cw), lambda j: (0, j + nb))],
        out_specs=col, out_shape=SDS((s, D_FF), BF16), name=name,
        compiler_params=_params(("parallel",), 16 * s * cw * 4))(up_g, up_v, w, w, b2, b2)


def _ffn_conv_bwd(up_g, up_v, w, b, dact, *, name):
    s = up_g.shape[0]
    cw = 256
    nb = D_FF // cw

    def body(g_ref, v_ref, wg_ref, wv_ref, bg_ref, bv_ref, da_ref, dg_ref, dv_ref, dwg_ref, dwv_ref, dbg_ref, dbv_ref):
        xg, xv = g_ref[...], v_ref[...]
        wg, wv = wg_ref[...], wv_ref[...]
        gate = _conv_pre(xg, wg, bg_ref[...], FFN_CONV)
        val = _conv_pre(xv, wv, bv_ref[...], FFN_CONV)
        da = da_ref[...].astype(F32)
        sg = _sigmoid(gate)
        dgate = da * val * (sg * (1.0 + gate * (1.0 - sg)))
        dval = da * gate * sg
        dxg, dwg, dbg = _conv_bwd_terms(xg, wg, dgate, FFN_CONV)
        dxv, dwv, dbv = _conv_bwd_terms(xv, wv, dval, FFN_CONV)
        dg_ref[...] = dxg.astype(dg_ref.dtype)
        dv_ref[...] = dxv.astype(dv_ref.dtype)
        dwg_ref[...] = dwg
        dwv_ref[...] = dwv
        dbg_ref[...] = dbg
        dbv_ref[...] = dbv

    col = pl.BlockSpec((s, cw), lambda j: (0, j))
    wsp = pl.BlockSpec((FFN_CONV, cw), lambda j: (0, j))
    bsp = pl.BlockSpec((1, cw), lambda j: (0, j))
    b2 = b.reshape(1, 2 * D_FF)
    dg, dv, dwg, dwv, dbg, dbv = pl.pallas_call(
        body, grid=(nb,),
        in_specs=[col, col, wsp, pl.BlockSpec((FFN_CONV, cw), lambda j: (0, j + nb)), bsp,
                  pl.BlockSpec((1, cw), lambda j: (0, j + nb)), col],
        out_specs=(col, col, wsp, wsp, bsp, bsp),
        out_shape=(SDS((s, D_FF), BF16), SDS((s, D_FF), BF16), SDS((FFN_CONV, D_FF), F32), SDS((FFN_CONV, D_FF), F32),
                   SDS((1, D_FF), F32), SDS((1, D_FF), F32)), name=name,
        compiler_params=_params(("parallel",), 32 * s * cw * 4))(up_g, up_v, w, w, b2, b2, dact)
    return dg, dv, jnp.concatenate([dwg, dwv], axis=1), jnp.concatenate([dbg, dbv], axis=1).reshape(2 * D_FF)


def _dot(a, b):
    return jnp.dot(a.astype(BF16), b.astype(BF16), preferred_element_type=F32)


def _dot_nt(a, b):
    return lax.dot_general(a.astype(BF16), b.astype(BF16), (((1,), (1,)), ((), ())), preferred_element_type=F32)


def _dot_tn(a, b):
    return lax.dot_general(a.astype(BF16), b.astype(BF16), (((0,), (0,)), ((), ())), preferred_element_type=F32)


def _ssd_chunk_terms(dtraw, bias, a_log):
    ell = dtraw.shape[0]
    lane = lax.broadcasted_iota(jnp.int32, dtraw.shape, 1)
    valid = lane < SSM_HEADS
    pre = dtraw + bias
    dt = jnp.where(valid, jnp.where(pre > 20.0, pre, jnp.log(1.0 + jnp.exp(jnp.minimum(pre, 20.0)))), 0.0)
    a = -jnp.exp(a_log)
    ad = dt * a
    row = lax.broadcasted_iota(jnp.int32, (ell, ell), 0)
    colm = lax.broadcasted_iota(jnp.int32, (ell, ell), 1)
    tril = row >= colm
    cs = jnp.dot(tril.astype(F32), ad, precision=HIGHEST, preferred_element_type=F32)
    cs_last = cs[ell - 1:ell, :]
    return pre, dt, a, cs, cs_last, tril


def _head_expand():
    h = lax.broadcasted_iota(jnp.int32, (LANES, D_SSM), 0)
    c = lax.broadcasted_iota(jnp.int32, (LANES, D_SSM), 1)
    return (c // SSM_HEAD_DIM == h).astype(F32)


def _ssd_fwd(xbc, proj, dt_bias, a_log, d_skip, *, name):
    s = xbc.shape[0]
    nc = s // SSM_CHUNK
    ell, n, p = SSM_CHUNK, SSM_STATE, SSM_HEAD_DIM
    rpg = SSM_HEADS // SSM_GROUPS
    gw = rpg * p

    def body(x_ref, dt_ref, bias_ref, alog_ref, dskip_ref, ex_ref, y_ref, ps_ref, state):
        @pl.when(pl.program_id(0) == 0)
        def _():
            state[...] = jnp.zeros_like(state)

        _, dt, _, cs, cs_last, tril = _ssd_chunk_terms(dt_ref[...], bias_ref[...], alog_ref[...])
        cst = cs.T
        ex = ex_ref[...]
        spread = lambda v: jnp.dot(v, ex, precision=HIGHEST, preferred_element_type=F32)
        dt_x, e_x, ds_x = spread(dt), spread(jnp.exp(cs)), spread(jnp.exp(cs_last - cs))
        cd_x = spread(jnp.broadcast_to(jnp.exp(cs_last), (8, LANES)))[0:1, :]
        dskip_x = spread(jnp.broadcast_to(dskip_ref[...], (8, LANES)))[0:1, :]
        st = state[...]
        ps_ref[0] = st
        xv = x_ref[...]
        xs_all = xv[:, 0:D_SSM]
        xd_all = xs_all * dt_x
        xdd_all = xd_all * ds_x
        lane_g = lax.broadcasted_iota(jnp.int32, (ell, gw), 1)
        ys, new = [], []
        for g in range(SSM_GROUPS):
            gs = slice(gw * g, gw * (g + 1))
            bg = xv[:, D_SSM + n * g:D_SSM + n * (g + 1)]
            cg = xv[:, D_SSM + n * (SSM_GROUPS + g):D_SSM + n * (SSM_GROUPS + g + 1)]
            cb = _dot_nt(cg, bg)
            xd_g, prev_g = xd_all[:, gs], st[:, gs]
            y_g = _dot(cg, prev_g) * e_x[:, gs] + xs_all[:, gs] * dskip_x[:, gs]
            for r in range(rpg):
                h = g * rpg + r
                lmat = jnp.exp(jnp.where(tril, cs[:, h:h + 1] - cst[h:h + 1, :], -jnp.inf))
                y_g = y_g + jnp.where((lane_g >= p * r) & (lane_g < p * (r + 1)), _dot(cb * lmat, xd_g), 0.0)
            ys.append(y_g)
            new.append(prev_g * cd_x[:, gs] + _dot(bg.T, xdd_all[:, gs]))
        y_ref[...] = jnp.concatenate(ys, axis=1)
        state[...] = jnp.concatenate(new, axis=1)

    vec = pl.BlockSpec((1, LANES), lambda c: (0, 0))
    return pl.pallas_call(
        body, grid=(nc,),
        in_specs=[pl.BlockSpec((ell, CONV_CH), lambda c: (c, 0)), pl.BlockSpec((ell, LANES), lambda c: (c, P_DT // LANES)),
                  vec, vec, vec, pl.BlockSpec((LANES, D_SSM), lambda c: (0, 0))],
        out_specs=(pl.BlockSpec((ell, D_SSM), lambda c: (c, 0)), pl.BlockSpec((1, n, D_SSM), lambda c: (c, 0, 0))),
        out_shape=(SDS((s, D_SSM), F32), SDS((nc, n, D_SSM), F32)),
        scratch_shapes=[pltpu.VMEM((n, D_SSM), F32)], name=name,
        compiler_params=_params(("arbitrary",), 32 << 20))(xbc, proj, dt_bias, a_log, d_skip, _head_expand())


def _ssd_bwd(xbc, proj, dt_bias, a_log, d_skip, prev_states, dy, *, name):
    s = xbc.shape[0]
    nc = s // SSM_CHUNK
    ell, n, p = SSM_CHUNK, SSM_STATE, SSM_HEAD_DIM
    rpg = SSM_HEADS // SSM_GROUPS
    gw = rpg * p

    def body(x_ref, dt_ref, bias_ref, alog_ref, dskip_ref, ps_ref, dy_ref, ex_ref, ext_ref,
             dx_ref, ddt_ref, dalog_ref, ddskip_ref, dbias_ref, dstate):
        @pl.when(pl.program_id(0) == 0)
        def _():
            dstate[...] = jnp.zeros_like(dstate)
            dalog_ref[...] = jnp.zeros_like(dalog_ref)
            ddskip_ref[...] = jnp.zeros_like(ddskip_ref)
            dbias_ref[...] = jnp.zeros_like(dbias_ref)

        pre, dt, a, cs, cs_last, tril = _ssd_chunk_terms(dt_ref[...], bias_ref[...], alog_ref[...])
        e = jnp.exp(cs)
        ds = jnp.exp(cs_last - cs)
        cd = jnp.exp(cs_last)
        cst = cs.T
        shape = (ell, LANES)
        ex, ext = ex_ref[...], ext_ref[...]
        spread = lambda v: jnp.dot(v, ex, precision=HIGHEST, preferred_element_type=F32)
        gather = lambda v: jnp.dot(v, ext, precision=HIGHEST, preferred_element_type=F32)
        dt_x, e_x, ds_x = spread(dt), spread(e), spread(ds)
        cd_x = spread(jnp.broadcast_to(cd, (8, LANES)))[0:1, :]
        dskip_x = spread(jnp.broadcast_to(dskip_ref[...], (8, LANES)))[0:1, :]
        xv, dyv, psv, dst = x_ref[...], dy_ref[...], ps_ref[0], dstate[...]
        xs_all = xv[:, 0:D_SSM]
        xd_all = xs_all * dt_x
        dye_all = dyv * e_x
        xdd_all = xd_all * ds_x
        triu = lax.broadcasted_iota(jnp.int32, (ell, ell), 0) <= lax.broadcasted_iota(jnp.int32, (ell, ell), 1)
        lane_g = lax.broadcasted_iota(jnp.int32, (ell, gw), 1)
        lane = lax.broadcasted_iota(jnp.int32, shape, 1)
        sub = lax.broadcasted_iota(jnp.int32, shape, 0)
        dcs_acc = jnp.zeros(shape, F32)
        dcs_rows = jnp.zeros(shape, F32)
        dxs, dbs, dcs_parts, dprevs, prod_a, prod_b, prod_c, prod_e = [], [], [], [], [], [], [], []
        for g in range(SSM_GROUPS):
            gs = slice(gw * g, gw * (g + 1))
            bg = xv[:, D_SSM + n * g:D_SSM + n * (g + 1)]
            cg = xv[:, D_SSM + n * (SSM_GROUPS + g):D_SSM + n * (SSM_GROUPS + g + 1)]
            cb = _dot_nt(cg, bg)
            cbt = _dot_nt(bg, cg)
            xs_g, dy_g, xd_g, dye_g, xdd_g = xs_all[:, gs], dyv[:, gs], xd_all[:, gs], dye_all[:, gs], xdd_all[:, gs]
            prev_g, dsn_g = psv[:, gs], dst[:, gs]
            cprev_g = _dot(cg, prev_g)
            dprevs.append(dsn_g * cd_x[:, gs] + _dot(cg.T, dye_g))
            dcg = _dot_nt(dye_g, prev_g)
            dxdd_g = _dot(bg, dsn_g)
            dbg = _dot_nt(xdd_g, dsn_g)
            dxd_g = dxdd_g * ds_x[:, gs]
            prod_a.append(dy_g * cprev_g)
            prod_b.append(dxdd_g * xd_g)
            prod_e.append(jnp.sum(dsn_g * prev_g, axis=0, keepdims=True))
            dcb = jnp.zeros((ell, ell), F32)
            for r in range(rpg):
                h = g * rpg + r
                mine = (lane_g >= p * r) & (lane_g < p * (r + 1))
                lmat = jnp.exp(jnp.where(tril, cs[:, h:h + 1] - cst[h:h + 1, :], -jnp.inf))
                lmat_t = jnp.exp(jnp.where(triu, cst[h:h + 1, :] - cs[:, h:h + 1], -jnp.inf))
                dgm = _dot_nt(jnp.where(mine, dy_g, 0.0), xd_g)
                dxd_g = dxd_g + jnp.where(mine, _dot(cbt * lmat_t, dy_g), 0.0)
                mm = dgm * (cb * lmat)
                dcs_acc = dcs_acc + jnp.where(lane == h, jnp.sum(mm, axis=1, keepdims=True), 0.0)
                dcs_rows = dcs_rows + jnp.where(sub == h, jnp.sum(mm, axis=0, keepdims=True), 0.0)
                dcb = dcb + dgm * lmat
            dxs.append(dxd_g * dt_x[:, gs] + dy_g * dskip_x[:, gs])
            prod_c.append(dxd_g * xs_g)
            dbs.append(dbg + _dot_tn(dcb, cg))
            dcs_parts.append(dcg + _dot(dcb, bg))
        dx_ref[...] = jnp.concatenate(dxs + dbs + dcs_parts, axis=1)
        dstate[...] = jnp.concatenate(dprevs, axis=1)
        sum_a = gather(jnp.concatenate(prod_a, axis=1))
        sum_b = gather(jnp.concatenate(prod_b, axis=1))
        sum_c = gather(jnp.concatenate(prod_c, axis=1))
        sum_d = gather(dyv * xs_all)
        dcd = gather(jnp.broadcast_to(jnp.concatenate(prod_e, axis=1), (8, D_SSM)))[0:1, :]
        tmp = sum_b * ds
        dlast = dcd * cd + jnp.sum(tmp, axis=0, keepdims=True)
        dcs = dcs_acc + sum_a * e - tmp - dcs_rows.T + jnp.where(sub == ell - 1, dlast, 0.0)
        dad = jnp.dot(triu.astype(F32), dcs, precision=HIGHEST, preferred_element_type=F32)
        ddt = sum_c + dad * a
        dalog_ref[...] += jnp.sum(dad * dt, axis=0, keepdims=True) * a
        ddskip_ref[...] += jnp.sum(sum_d, axis=0, keepdims=True)
        ddraw = jnp.where(lane < SSM_HEADS, ddt * _sigmoid(pre), 0.0)
        ddt_ref[...] = ddraw.astype(ddt_ref.dtype)
        dbias_ref[...] += jnp.sum(ddraw, axis=0, keepdims=True)

    vec = pl.BlockSpec((1, LANES), lambda c: (0, 0))
    rev = lambda c: nc - 1 - c
    ex = _head_expand()
    outs = pl.pallas_call(
        body, grid=(nc,),
        in_specs=[pl.BlockSpec((ell, CONV_CH), lambda c: (rev(c), 0)),
                  pl.BlockSpec((ell, LANES), lambda c: (rev(c), P_DT // LANES)), vec, vec, vec,
                  pl.BlockSpec((1, n, D_SSM), lambda c: (rev(c), 0, 0)),
                  pl.BlockSpec((ell, D_SSM), lambda c: (rev(c), 0)),
                  pl.BlockSpec((LANES, D_SSM), lambda c: (0, 0)), pl.BlockSpec((D_SSM, LANES), lambda c: (0, 0))],
        out_specs=(pl.BlockSpec((ell, CONV_CH), lambda c: (rev(c), 0)), pl.BlockSpec((ell, LANES), lambda c: (rev(c), 0)),
                   vec, vec, vec),
        out_shape=(SDS((s, CONV_CH), F32), SDS((s, LANES), BF16), SDS((1, LANES), F32), SDS((1, LANES), F32),
                   SDS((1, LANES), F32)),
        scratch_shapes=[pltpu.VMEM((n, D_SSM), F32)], name=name,
        compiler_params=_params(("arbitrary",), 40 << 20))(xbc, proj, dt_bias, a_log, d_skip, prev_states, dy, ex, ex.T)
    return outs


def _rope_swap(t):
    lane = lax.broadcasted_iota(jnp.int32, t.shape, 1)
    half = QK_ROPE // 2
    lo = (lane >= QK_NOPE) & (lane < QK_NOPE + half)
    hi = (lane >= QK_NOPE + half) & (lane < QK_NOPE + QK_ROPE)
    return jnp.where(lo, pltpu.roll(t, HEAD_PAD - half, axis=1), jnp.where(hi, pltpu.roll(t, half, axis=1), 0.0))


def _mla_prep(q, kv, proj, cos, sins, *, name):
    s = q.shape[0]
    tm = min(s, 256)
    scale = (QK_NOPE + QK_ROPE) ** -0.5

    def body(q_ref, kv_ref, kr_ref, cos_ref, sin_ref, qo_ref, ko_ref, vo_ref):
        cosv, sinv = cos_ref[...], sin_ref[...]
        kr = pltpu.roll(kr_ref[...], QK_NOPE, axis=1)
        lane = lax.broadcasted_iota(jnp.int32, kr.shape, 1)
        nope = lane < QK_NOPE
        kr = jnp.where(nope, 0.0, kr)
        kpe = kr * cosv + _rope_swap(kr) * sinv
        for hp in range(MLA_HEADS // 2):
            vs = []
            for h in (2 * hp, 2 * hp + 1):
                hs = slice(HEAD_PAD * h, HEAD_PAD * (h + 1))
                qh = q_ref[:, hs]
                kvh = kv_ref[:, hs]
                qo_ref[:, hs] = ((qh * cosv + _rope_swap(qh) * sinv) * scale).astype(qo_ref.dtype)
                ko_ref[:, hs] = (jnp.where(nope, kvh, 0.0) + kpe).astype(ko_ref.dtype)
                vs.append(kvh[:, QK_NOPE:])
            vo_ref[:, 2 * V_DIM * hp:2 * V_DIM * (hp + 1)] = jnp.concatenate(vs, axis=1).astype(vo_ref.dtype)

    wide = pl.BlockSpec((tm, MLA_HEADS * HEAD_PAD), lambda i: (i, 0))
    half = pl.BlockSpec((tm, MLA_HEADS * V_DIM), lambda i: (i, 0))
    tab = pl.BlockSpec((tm, LANES), lambda i: (i, 0))
    return pl.pallas_call(
        body, grid=(s // tm,),
        in_specs=[wide, wide, pl.BlockSpec((tm, LANES), lambda i: (i, P_KR // LANES)), tab, tab],
        out_specs=(wide, wide, half),
        out_shape=(SDS((s, MLA_HEADS * HEAD_PAD), BF16), SDS((s, MLA_HEADS * HEAD_PAD), BF16),
                   SDS((s, MLA_HEADS * V_DIM), BF16)), name=name,
        compiler_params=_params(("parallel",), 32 << 20))(q, kv, proj, cos, sins)


def _mla_prep_bwd(dqr, dkr, dv, cos, sins, *, name):
    s = dqr.shape[0]
    tm = min(s, 256)
    scale = (QK_NOPE + QK_ROPE) ** -0.5

    def body(dq_ref, dk_ref, dv_ref, cos_ref, sin_ref, dqo_ref, dkv_ref, dkr_ref):
        cosv, sinv = cos_ref[...], sin_ref[...]
        lane = lax.broadcasted_iota(jnp.int32, cosv.shape, 1)
        ksum = jnp.zeros(cosv.shape, F32)
        for h in range(MLA_HEADS):
            hs = slice(HEAD_PAD * h, HEAD_PAD * (h + 1))
            d = dq_ref[:, hs]
            dk = dk_ref[:, hs]
            dqo_ref[:, hs] = ((d * cosv + _rope_swap(d * sinv)) * scale).astype(dqo_ref.dtype)
            dkv_ref[:, hs] = jnp.concatenate([dk[:, :QK_NOPE], dv_ref[:, V_DIM * h:V_DIM * (h + 1)]], axis=1).astype(dkv_ref.dtype)
            ksum = ksum + dk
        ksum = jnp.where((lane >= QK_NOPE) & (lane < QK_NOPE + QK_ROPE), ksum, 0.0)
        un = ksum * cosv + _rope_swap(ksum * sinv)
        dkr_ref[...] = pltpu.roll(un, HEAD_PAD - QK_NOPE, axis=1).astype(dkr_ref.dtype)

    wide = pl.BlockSpec((tm, MLA_HEADS * HEAD_PAD), lambda i: (i, 0))
    half = pl.BlockSpec((tm, MLA_HEADS * V_DIM), lambda i: (i, 0))
    tab = pl.BlockSpec((tm, LANES), lambda i: (i, 0))
    return pl.pallas_call(
        body, grid=(s // tm,), in_specs=[wide, wide, half, tab, tab], out_specs=(wide, wide, tab),
        out_shape=(SDS((s, MLA_HEADS * HEAD_PAD), BF16), SDS((s, MLA_HEADS * HEAD_PAD), BF16), SDS((s, LANES), BF16)),
        name=name, compiler_params=_params(("parallel",), 40 << 20))(dqr, dkr, dv, cos, sins)


FLASH_TILE = 512
FLASH_ROWS = 32


def _flash_fwd(q, k, v, *, name):
    s = q.shape[0]
    t = min(s, FLASH_TILE)
    nq = s // t
    npair = MLA_HEADS // 2

    def body(q_ref, k_ref, v_ref, o_ref, lse_ref):
        i = pl.program_id(1)
        qs = [q_ref[:, HEAD_PAD * e:HEAD_PAD * (e + 1)] for e in range(2)]
        diag = lax.broadcasted_iota(jnp.int32, (t, t), 0) >= lax.broadcasted_iota(jnp.int32, (t, t), 1)

        def step(j, carry, masked):
            rows = pl.ds(pl.multiple_of(j * t, t), t)
            new = []
            for e in range(2):
                m, l, acc = carry[e]
                sc = _dot_nt(qs[e], k_ref[rows, HEAD_PAD * e:HEAD_PAD * (e + 1)])
                if masked:
                    sc = jnp.where(diag, sc, NEG)
                m_new = jnp.maximum(m, jnp.max(sc, axis=1, keepdims=True))
                pr = jnp.exp(sc - m_new)
                alpha = jnp.exp(m - m_new)
                l = alpha * l + jnp.sum(pr, axis=1, keepdims=True)
                acc = alpha * acc + _dot(pr, v_ref[rows, V_DIM * e:V_DIM * (e + 1)])
                new.append((m_new, l, acc))
            return tuple(new)

        init = tuple((jnp.full((t, 1), NEG, F32), jnp.zeros((t, 1), F32), jnp.zeros((t, V_DIM), F32)) for _ in range(2))
        carry = lax.fori_loop(0, i, functools.partial(step, masked=False), init)
        carry = step(i, carry, True)
        o_ref[...] = jnp.concatenate([acc / l for _, l, acc in carry], axis=1)
        lse_ref[0] = jnp.concatenate([jnp.broadcast_to(m + jnp.log(l), (t, V_DIM)) for m, l, _ in carry], axis=1)

    return pl.pallas_call(
        body, grid=(npair, nq),
        in_specs=[pl.BlockSpec((t, 2 * HEAD_PAD), lambda hp, i: (i, hp)), pl.BlockSpec((s, 2 * HEAD_PAD), lambda hp, i: (0, hp)),
                  pl.BlockSpec((s, 2 * V_DIM), lambda hp, i: (0, hp))],
        out_specs=(pl.BlockSpec((t, 2 * V_DIM), lambda hp, i: (i, hp)), pl.BlockSpec((1, t, LANES), lambda hp, i: (hp, i, 0))),
        out_shape=(SDS((s, MLA_HEADS * V_DIM), F32), SDS((npair, s, LANES), F32)), name=name,
        compiler_params=_params(("parallel", "parallel"), 40 << 20))(q, k, v)


def _flash_bwd(q, k, v, o, lse, do, *, name):
    s = q.shape[0]
    t = min(s, FLASH_TILE)
    nq = s // t
    npair = MLA_HEADS // 2
    nchunk = t // FLASH_ROWS

    def valid_cols(r):
        return min(t, -(-((r + 1) * FLASH_ROWS) // LANES) * LANES)

    def body(q_ref, k_ref, v_ref, o_ref, lse_ref, do_ref, dq_ref, dk_ref, dv_ref, s_scr, dp_scr, p_scr, ds_scr, dk_acc, dv_acc):
        j = pl.program_id(1)

        @pl.when(j == 0)
        def _():
            dq_ref[...] = jnp.zeros_like(dq_ref)

        dk_acc[...] = jnp.zeros(dk_acc.shape, F32)
        dv_acc[...] = jnp.zeros(dv_acc.shape, F32)
        qsl = [slice(HEAD_PAD * e, HEAD_PAD * (e + 1)) for e in range(2)]
        vsl = [slice(V_DIM * e, V_DIM * (e + 1)) for e in range(2)]

        def step(i, carry, masked):
            rows = pl.ds(pl.multiple_of(i * t, t), t)
            for e in range(2):
                ke = k_ref[:, qsl[e]]
                qi = q_ref[rows, qsl[e]]
                doi = do_ref[rows, vsl[e]]
                delta = jnp.sum(doi * o_ref[rows, vsl[e]], axis=1, keepdims=True)
                lse_i = lse_ref[0, rows, vsl[e]][:, 0:1]
                dob = doi.astype(BF16)
                s_scr[e] = _dot_nt(qi, ke)
                dp_scr[e] = _dot_nt(dob, v_ref[:, vsl[e]])
                for r in range(nchunk):
                    rs = slice(r * FLASH_ROWS, (r + 1) * FLASH_ROWS)
                    width = valid_cols(r) if masked else t
                    sc = s_scr[e, rs, 0:width]
                    if masked:
                        row = r * FLASH_ROWS + lax.broadcasted_iota(jnp.int32, (FLASH_ROWS, width), 0)
                        sc = jnp.where(row >= lax.broadcasted_iota(jnp.int32, (FLASH_ROWS, width), 1), sc, NEG)
                    pr = jnp.exp(sc - lse_i[rs, :])
                    dsc = pr * (dp_scr[e, rs, 0:width] - delta[rs, :])
                    p_scr[e, rs, 0:width] = pr.astype(BF16)
                    ds_scr[e, rs, 0:width] = dsc.astype(BF16)
                    if width < t:
                        p_scr[e, rs, width:t] = jnp.zeros((FLASH_ROWS, t - width), BF16)
                        ds_scr[e, rs, width:t] = jnp.zeros((FLASH_ROWS, t - width), BF16)
                dv_acc[e] += _dot_tn(p_scr[e], dob)
                dk_acc[e] += _dot_tn(ds_scr[e], qi)
                dq_ref[rows, qsl[e]] += _dot(ds_scr[e], ke)
            return carry

        step(j, 0, True)
        lax.fori_loop(j + 1, nq, functools.partial(step, masked=False), 0)
        dk_ref[...] = jnp.concatenate([dk_acc[e] for e in range(2)], axis=1)
        dv_ref[...] = jnp.concatenate([dv_acc[e] for e in range(2)], axis=1)

    full_q = pl.BlockSpec((s, 2 * HEAD_PAD), lambda hp, j: (0, hp))
    full_v = pl.BlockSpec((s, 2 * V_DIM), lambda hp, j: (0, hp))
    blk_k = pl.BlockSpec((t, 2 * HEAD_PAD), lambda hp, j: (j, hp))
    blk_v = pl.BlockSpec((t, 2 * V_DIM), lambda hp, j: (j, hp))
    return pl.pallas_call(
        body, grid=(npair, nq),
        in_specs=[full_q, blk_k, blk_v, full_v, pl.BlockSpec((1, s, LANES), lambda hp, j: (hp, 0, 0)), full_v],
        out_specs=(full_q, blk_k, blk_v),
        out_shape=(SDS((s, MLA_HEADS * HEAD_PAD), F32), SDS((s, MLA_HEADS * HEAD_PAD), F32), SDS((s, MLA_HEADS * V_DIM), F32)),
        scratch_shapes=[pltpu.VMEM((2, t, t), F32), pltpu.VMEM((2, t, t), F32), pltpu.VMEM((2, t, t), BF16),
                        pltpu.VMEM((2, t, t), BF16), pltpu.VMEM((2, t, HEAD_PAD), F32), pltpu.VMEM((2, t, V_DIM), F32)],
        name=name, compiler_params=_params(("parallel", "arbitrary"), 48 << 20))(q, k, v, o, lse, do)


def _mem_attn_fwd(q, k, v, *, name):
    s = q.shape[0]
    tm = min(s, 512)
    ml = k.shape[0]
    scale = MEM_HEAD_DIM ** -0.5

    def body(q_ref, k_ref, v_ref, o_ref):
        for h in range(MEM_HEADS):
            hs = slice(MEM_HEAD_DIM * h, MEM_HEAD_DIM * (h + 1))
            sc = _dot_nt(q_ref[:, hs], k_ref[:, hs]) * scale
            pr = jnp.exp(sc - jnp.max(sc, axis=1, keepdims=True))
            pr = pr / jnp.sum(pr, axis=1, keepdims=True)
            o_ref[:, hs] = _dot(pr, v_ref[:, hs]).astype(o_ref.dtype)

    blk = pl.BlockSpec((tm, D_MODEL), lambda i: (i, 0))
    kv = pl.BlockSpec((ml, D_MODEL), lambda i: (0, 0))
    return pl.pallas_call(body, grid=(s // tm,), in_specs=[blk, kv, kv], out_specs=blk,
                          out_shape=SDS((s, D_MODEL), BF16), name=name,
                          compiler_params=_params(("parallel",), 24 << 20))(q, k, v)


def _mem_attn_bwd(q, k, v, do, *, name):
    s = q.shape[0]
    tm = min(s, 512)
    ml = k.shape[0]
    scale = MEM_HEAD_DIM ** -0.5

    def body(q_ref, k_ref, v_ref, do_ref, dq_ref, dk_ref, dv_ref):
        @pl.when(pl.program_id(0) == 0)
        def _():
            dk_ref[...] = jnp.zeros_like(dk_ref)
            dv_ref[...] = jnp.zeros_like(dv_ref)

        for h in range(MEM_HEADS):
            hs = slice(MEM_HEAD_DIM * h, MEM_HEAD_DIM * (h + 1))
            qh, kh, vh, doh = q_ref[:, hs], k_ref[:, hs], v_ref[:, hs], do_ref[:, hs]
            sc = _dot_nt(qh, kh) * scale
            pr = jnp.exp(sc - jnp.max(sc, axis=1, keepdims=True))
            pr = pr / jnp.sum(pr, axis=1, keepdims=True)
            dp = _dot_nt(doh, vh)
            dsc = pr * (dp - jnp.sum(pr * dp, axis=1, keepdims=True)) * scale
            dq_ref[:, hs] = _dot(dsc, kh).astype(dq_ref.dtype)
            dk_ref[:, hs] += _dot_tn(dsc, qh)
            dv_ref[:, hs] += _dot_tn(pr, doh)

    blk = pl.BlockSpec((tm, D_MODEL), lambda i: (i, 0))
    kv = pl.BlockSpec((ml, D_MODEL), lambda i: (0, 0))
    return pl.pallas_call(body, grid=(s // tm,), in_specs=[blk, kv, kv, blk], out_specs=(blk, kv, kv),
                          out_shape=(SDS((s, D_MODEL), BF16), SDS((ml, D_MODEL), F32), SDS((ml, D_MODEL), F32)), name=name,
                          compiler_params=_params(("arbitrary",), 32 << 20))(q, k, v, do)


MATS = (("w_in", (1024, 940), 1), ("w_uq", (384, 384), 1), ("w_ukv", (256, 512), 1), ("w_out", (512, 1024), 0),
        ("ssm_conv_w", (4, 512), 1),
        ("w_mq", (256, 1024), 0), ("w_mk", (256, 1024), 0), ("w_mv", (256, 1024), 0), ("w_mo", (256, 1024), 0),
        ("w_up", (1024, 1408), 1), ("w_down", (704, 1024), 0), ("ffn_conv_w", (3, 1408), 1))
GROUPS = {"proj": (0,), "mixer": (1, 2, 3, 4), "mem": (5, 6, 7, 8), "ffn": (9, 10, 11)}
UP_SHARD_COLS = 1408
F32_ON_WIRE = ("ssm_conv_w", "ffn_conv_w")
SMALL = (("norm_mix", 1024), ("ssm_conv_b", 2048), ("dt_bias", 16), ("a_log", 16), ("d_skip", 16), ("ssm_norm", 1024),
         ("q_norm", 384), ("kv_norm", 256), ("attn_out_norm", 1024), ("norm_mem_q", 1024), ("norm_mem_kv", 1024),
         ("norm_ffn", 1024), ("ffn_conv_b", 5632))
PACK_COLS = 1024


def _pad_cols(t, n):
    return jnp.pad(t, ((0, 0),) * (t.ndim - 1) + ((0, n - t.shape[-1]),))


def _w_in_to_padded(t):
    z, xbc, dt, cq, ckv, kr = jnp.split(t, (1024, 3072, 3088, 3472, 3728), axis=-1)
    return jnp.concatenate([xbc, z, cq, _pad_cols(dt, LANES), ckv, _pad_cols(kr, P_IN - P_KR)], axis=-1)


def _w_in_from_padded(t):
    return jnp.concatenate([t[..., P_Z:P_Z + 1024], t[..., P_XBC:P_XBC + 2048], t[..., P_DT:P_DT + SSM_HEADS],
                            t[..., P_CQ:P_CQ + Q_LORA], t[..., P_CKV:P_CKV + KV_LORA], t[..., P_KR:P_KR + QK_ROPE]], axis=-1)


def _cols_joined(g):
    return jnp.concatenate([g[j] for j in range(N_CHIPS)], axis=-1)


def _cols_by_chip(t, dtype):
    k = t.shape[0]
    return t.reshape(k, N_CHIPS, -1).transpose(1, 0, 2).astype(dtype)


def _rows_by_chip(t):
    return t.reshape(N_CHIPS, -1, t.shape[-1])


def _mixer_weights(gw):
    wl = {}
    uq = _cols_joined(gw["w_uq"]).reshape(Q_LORA, MLA_HEADS, QK_NOPE + QK_ROPE)
    wl["w_uq"] = _pad_cols(uq, HEAD_PAD).reshape(Q_LORA, MLA_HEADS * HEAD_PAD)
    wl["w_ukv"] = _cols_joined(gw["w_ukv"])
    wl["ssm_conv_w"] = _cols_joined(gw["ssm_conv_w"])
    return wl


def _layer_fwd(x0, mem, cos, sins, weights, sp, li):
    n = lambda t: f"l{li}_{t}"
    lead = ()
    sv = {"x0": x0}
    gw = dict(weights("proj", x0))
    w_in = _w_in_to_padded(_cols_joined(gw["w_in"]))
    h = _rms_fwd(x0, sp["norm_mix"], name=n("mix_norm"))
    in_hbm = lambda t: pltpu.with_memory_space_constraint(t, pltpu.HBM)
    proj = in_hbm(_mm(h, w_in, name=n("mix_proj")))
    gw.update(weights("mixer", proj))
    wl = dict(_mixer_weights(gw), w_in=w_in)
    xbc = in_hbm(_ssm_conv_fwd(proj, wl["ssm_conv_w"], sp["ssm_conv_b"], name=n("ssm_conv")))
    y, pstates = _ssd_fwd(xbc, proj, sp["dt_bias"], sp["a_log"], sp["d_skip"], name=n("ssd"))
    y_ssm = _gated_rms_fwd(y, proj, sp["ssm_norm"], name=n("ssm_gate"))
    cqn = _rms_fwd(proj, sp["q_norm"], col=(Q_LORA, P_CQ // Q_LORA), name=n("q_norm"))
    ckvn = _rms_fwd(proj, sp["kv_norm"], col=(KV_LORA, P_CKV // KV_LORA), name=n("kv_norm"))
    q = in_hbm(_mm(cqn, wl["w_uq"], name=n("uq")))
    kv = in_hbm(_mm(ckvn, wl["w_ukv"], name=n("ukv")))
    qr, kr, v = _mla_prep(q, kv, proj, cos, sins, name=n("rope"))
    att, lse = _flash_fwd(qr, kr, v, name=n("flash"))
    y_att = _rms_fwd(att, sp["attn_out_norm"], name=n("att_norm"))
    x1 = _mm(y_ssm, gw["w_out"], b_lead=lead, b_rows=(0, D_SSM), res=x0, name=n("out_a"))
    x1 = _mm(y_att, gw["w_out"], b_lead=lead, b_rows=(D_SSM, D_SSM), res=x1, name=n("out_b"))
    sv.update(h=h, proj=proj, xbc=xbc, y=y, pstates=pstates, y_ssm=y_ssm, cqn=cqn, ckvn=ckvn, qr=qr, kr=kr, v=v,
              att=att, lse=lse, y_att=y_att, x1=x1)
    gw.update(weights("mem", x1))
    hq = _rms_fwd(x1, sp["norm_mem_q"], name=n("memq_norm"))
    hm = _rms_fwd(mem, sp["norm_mem_kv"], name=n("memkv_norm"))
    mq = _mm(hq, gw["w_mq"], b_lead=lead, out_dtype=BF16, name=n("mq"))
    mk = _mm(hm, gw["w_mk"], b_lead=lead, out_dtype=BF16, name=n("mk"))
    mv = _mm(hm, gw["w_mv"], b_lead=lead, out_dtype=BF16, name=n("mv"))
    mo = _mem_attn_fwd(mq, mk, mv, name=n("mem_attn"))
    x2 = _mm(mo, gw["w_mo"], b_lead=lead, res=x1, name=n("mo"))
    sv.update(hq=hq, hm=hm, mq=mq, mk=mk, mv=mv, mo=mo, x2=x2)
    gw.update(weights("ffn", x2))
    wl["ffn_conv_w"] = _cols_joined(gw["ffn_conv_w"])
    hf = _rms_fwd(x2, sp["norm_ffn"], name=n("ffn_norm"))
    up_g = _mm(hf, gw["w_up"], b_lead=lead, b_chips=(0, 2), name=n("up_g"))
    up_v = _mm(hf, gw["w_up"], b_lead=lead, b_chips=(2, 2), name=n("up_v"))
    act = _ffn_conv_fwd(up_g, up_v, wl["ffn_conv_w"], sp["ffn_conv_b"], name=n("ffn_conv"))
    x3 = _mm(act, gw["w_down"], b_lead=lead, res=x2, name=n("down"))
    sv.update(hf=hf, up_g=up_g, up_v=up_v, act=act)
    return x3, sv, gw, wl


def _layer_bwd(dx3, dx3b, mem, cos, sins, gw, wl, sp, sv, li, emit):
    n = lambda t: f"l{li}_b_{t}"
    lead = ()
    g = {}

    def after(token, v):
        return v if token is None else v + token[0, 0]

    dact = _mm(dx3b, gw["w_down"], tb=True, b_lead=lead, out_dtype=BF16, name=n("down_dx"))
    g["w_down"] = _rows_by_chip(_mm(sv["act"], dx3b, ta=True, out_dtype=BF16, name=n("down_dw")))
    dup_g, dup_v, dcw, g["ffn_conv_b"] = _ffn_conv_bwd(
        sv["up_g"], sv["up_v"], wl["ffn_conv_w"], sp["ffn_conv_b"], dact, name=n("ffn_conv"))
    g["ffn_conv_w"] = _cols_by_chip(dcw, F32)
    nsh = UP_SHARD_COLS
    dhf = None
    for c4 in range(N_CHIPS):
        dhf = _mm(dup_g if c4 < 2 else dup_v, gw["w_up"], tb=True, a_col=(nsh, c4 % 2), b_lead=(c4,), res=dhf,
                  name=n(f"up{c4}_dx"))
    g["w_up"] = jnp.concatenate([_mm(sv["hf"], dup_g, ta=True, o_chips=nsh, out_dtype=BF16, name=n("upg_dw")),
                                 _mm(sv["hf"], dup_v, ta=True, o_chips=nsh, out_dtype=BF16, name=n("upv_dw"))], axis=0)
    dx2, dx2b, g["norm_ffn"] = _rms_bwd(sv["x2"], after(emit("ffn", g), sp["norm_ffn"]), dhf, dx3, name=n("ffn_norm"))
    dmo = _mm(dx2b, gw["w_mo"], tb=True, b_lead=lead, out_dtype=BF16, name=n("mo_dx"))
    g["w_mo"] = _rows_by_chip(_mm(sv["mo"], dx2b, ta=True, out_dtype=BF16, name=n("mo_dw")))
    dmq, dmk, dmv = _mem_attn_bwd(sv["mq"], sv["mk"], sv["mv"], dmo, name=n("mem_attn"))
    dhq = _mm(dmq, gw["w_mq"], tb=True, b_lead=lead, name=n("mq_dx"))
    g["w_mq"] = _rows_by_chip(_mm(sv["hq"], dmq, ta=True, out_dtype=BF16, name=n("mq_dw")))
    dhm = _mm(dmk, gw["w_mk"], tb=True, b_lead=lead, name=n("mk_dx"))
    dhm = _mm(dmv, gw["w_mv"], tb=True, b_lead=lead, res=dhm, name=n("mv_dx"))
    g["w_mk"] = _rows_by_chip(_mm(sv["hm"], dmk, ta=True, out_dtype=BF16, name=n("mk_dw")))
    g["w_mv"] = _rows_by_chip(_mm(sv["hm"], dmv, ta=True, out_dtype=BF16, name=n("mv_dw")))
    dx1, dx1b, g["norm_mem_q"] = _rms_bwd(sv["x1"], after(emit("mem", g), sp["norm_mem_q"]), dhq, dx2, name=n("memq_norm"))
    _, _, g["norm_mem_kv"] = _rms_bwd(mem, sp["norm_mem_kv"], dhm, name=n("memkv_norm"))
    dy_ssm = _mm(dx1b, gw["w_out"], tb=True, b_lead=lead, b_rows=(0, D_SSM), name=n("outa_dx"))
    dy_att = _mm(dx1b, gw["w_out"], tb=True, b_lead=lead, b_rows=(D_SSM, D_SSM), name=n("outb_dx"))
    g["w_out"] = _rows_by_chip(jnp.concatenate([_mm(sv["y_ssm"], dx1b, ta=True, out_dtype=BF16, name=n("outa_dw")),
                                                _mm(sv["y_att"], dx1b, ta=True, out_dtype=BF16, name=n("outb_dw"))], axis=0))
    datt, _, g["attn_out_norm"] = _rms_bwd(sv["att"], sp["attn_out_norm"], dy_att, name=n("att_norm"))
    dqr, dkr, dv = _flash_bwd(sv["qr"], sv["kr"], sv["v"], sv["att"], sv["lse"], datt, name=n("flash"))
    dq, dkv, dkrope = _mla_prep_bwd(dqr, dkr, dv, cos, sins, name=n("rope"))
    duq = _mm(sv["cqn"], dq, ta=True, name=n("uq_dw")).reshape(Q_LORA, MLA_HEADS, HEAD_PAD)[..., :QK_NOPE + QK_ROPE]
    g["w_uq"] = _cols_by_chip(duq.reshape(Q_LORA, -1), BF16)
    dcqn = _mm(dq, wl["w_uq"], tb=True, name=n("uq_dx"))
    g["w_ukv"] = _cols_by_chip(_mm(sv["ckvn"], dkv, ta=True, name=n("ukv_dw")), BF16)
    dckvn = _mm(dkv, wl["w_ukv"], tb=True, name=n("ukv_dx"))
    proj = sv["proj"]
    _, dcq, g["q_norm"] = _rms_bwd(proj, sp["q_norm"], dcqn, col=(Q_LORA, P_CQ // Q_LORA), name=n("q_norm"))
    _, dckv, g["kv_norm"] = _rms_bwd(proj, sp["kv_norm"], dckvn, col=(KV_LORA, P_CKV // KV_LORA), name=n("kv_norm"))
    dy, dz, g["ssm_norm"] = _gated_rms_bwd(sv["y"], proj, sp["ssm_norm"], dy_ssm, name=n("ssm_gate"))
    dxbc, ddt, da_log, dd_skip, ddt_bias = _ssd_bwd(
        sv["xbc"], proj, sp["dt_bias"], sp["a_log"], sp["d_skip"], sv["pstates"], dy, name=n("ssd"))
    g["a_log"], g["d_skip"], g["dt_bias"] = da_log[0, :SSM_HEADS], dd_skip[0, :SSM_HEADS], ddt_bias[0, :SSM_HEADS]
    dxbc_pre, dsw, g["ssm_conv_b"] = _ssm_conv_bwd(proj, wl["ssm_conv_w"], sp["ssm_conv_b"], dxbc, name=n("ssm_conv"))
    g["ssm_conv_w"] = _cols_by_chip(dsw, F32)
    started = emit("mixer", g)
    s = proj.shape[0]
    dproj = jnp.concatenate([dxbc_pre, dz, dcq, ddt, dckv, dkrope,
                             jnp.zeros((s, P_IN - P_KR - LANES), BF16)], axis=1)
    dh = _mm(dproj, wl["w_in"], tb=True, name=n("proj_dx"))
    g["w_in"] = _cols_by_chip(_w_in_from_padded(_mm(sv["h"], dproj, ta=True, name=n("proj_dw"))), BF16)
    dx0, dx0b, g["norm_mix"] = _rms_bwd(sv["x0"], after(started, sp["norm_mix"]), dh, dx1, name=n("mix_norm"))
    return dx0, dx0b, g, emit("proj", g)


def _chip_peers(x, y):
    return [(1 - x, y), (x, 1 - y), (1 - x, 1 - y)]


HBM_SPEC = pl.BlockSpec(memory_space=pltpu.HBM)
SEM_SPEC = pl.BlockSpec(memory_space=pltpu.SEMAPHORE)
ANY_SPEC = pl.BlockSpec(memory_space=pl.ANY)
VMEM_SPEC = pl.BlockSpec(memory_space=pltpu.VMEM)
DATAFLOW = pltpu.SideEffectType.DATAFLOW_SIDE_EFFECTING
TOKEN_SHAPE = (8, LANES)


def _exchange_start(srcs, land_shapes, src_view, dst_view, token, *, name):
    n = len(srcs)

    def body(*refs):
        s, l, tok_in = refs[:n], refs[n:2 * n], refs[2 * n]
        send_sems, recv_sems = refs[2 * n + 1], refs[2 * n + 2]
        tok_out = refs[-1]
        x, y, c = lax.axis_index("x"), lax.axis_index("y"), lax.axis_index("c")
        me = 2 * x + y
        for t in range(n):
            for k, (px, py) in enumerate(_chip_peers(x, y)):
                pltpu.make_async_remote_copy(
                    src_ref=src_view(t, s[t], 2 * px + py), dst_ref=dst_view(t, l[t], me), send_sem=send_sems.at[3 * t + k],
                    recv_sem=recv_sems.at[3 * t + k], device_id=(px, py, c), device_id_type=MESH).start()
            pltpu.make_async_copy(src_view(t, s[t], me), dst_view(t, l[t], me), send_sems.at[3 * n + t]).start()
        tok_out[...] = tok_in[...]

    hbm = lambda t: pltpu.with_memory_space_constraint(t, pltpu.HBM)
    lands = [lax.empty(l.shape, l.dtype) for l in land_shapes]
    outs = pl.pallas_call(
        body, name=name,
        out_shape=(pltpu.SemaphoreType.DMA((4 * n,)), pltpu.SemaphoreType.DMA((3 * n,)),
                   *[pltpu.HBM(l.shape, l.dtype) for l in land_shapes], SDS(TOKEN_SHAPE, F32)),
        in_specs=[HBM_SPEC] * (2 * n) + [VMEM_SPEC], out_specs=(SEM_SPEC, SEM_SPEC, *[HBM_SPEC] * n, VMEM_SPEC),
        input_output_aliases={n + t: 2 + t for t in range(n)},
        compiler_params=pltpu.CompilerParams(has_side_effects=DATAFLOW))(*[hbm(t) for t in srcs], *[hbm(t) for t in lands], token)
    return outs[0], outs[1], list(outs[2:2 + n]), outs[-1]


def _exchange_wait(srcs, lands, send_sems, recv_sems, after, src_view, dst_view, which, *, name):
    n = len(srcs)
    m = len(which)

    def body(*refs):
        s, l = refs[:m], refs[m:2 * m]
        send_ref, recv_ref = refs[2 * m], refs[2 * m + 1]
        x, y, c = lax.axis_index("x"), lax.axis_index("y"), lax.axis_index("c")
        me = 2 * x + y
        for i, t in enumerate(which):
            for k, (px, py) in enumerate(_chip_peers(x, y)):
                chip = 2 * px + py
                cp = pltpu.make_async_remote_copy(
                    src_ref=src_view(t, s[i], chip), dst_ref=dst_view(t, l[i], chip), send_sem=send_ref.at[3 * t + k],
                    recv_sem=recv_ref.at[3 * t + k], device_id=(px, py, c), device_id_type=MESH)
                cp.wait_send()
                cp.wait_recv()
            pltpu.make_async_copy(src_view(t, s[i], me), dst_view(t, l[i], me), send_ref.at[3 * n + t]).wait()

    outs = pl.pallas_call(
        body, name=name, out_shape=[pltpu.HBM(lands[t].shape, lands[t].dtype) for t in which],
        in_specs=[HBM_SPEC] * (2 * m) + [SEM_SPEC, SEM_SPEC, ANY_SPEC], out_specs=[HBM_SPEC] * m,
        input_output_aliases={m + i: i for i in range(m)},
        compiler_params=pltpu.CompilerParams(has_side_effects=DATAFLOW))(
            *[srcs[t] for t in which], *[lands[t] for t in which], send_sems, recv_sems, after)
    return list(outs)


def _gather_layer_start(shards, li, token):
    src_view = lambda t, ref, chip: ref.at[li]
    dst_view = lambda t, ref, chip: ref.at[chip]
    send_sems, recv_sems, lands, token = _exchange_start(
        shards, [SDS((N_CHIPS,) + s.shape[1:], s.dtype) for s in shards], src_view, dst_view, token, name=f"gather{li}_start")
    return (shards, lands, send_sems, recv_sems, src_view, dst_view, f"gather{li}"), token


def _scatter_start(grads, tag, token):
    view = lambda t, ref, chip: ref.at[chip]
    send_sems, recv_sems, lands, token = _exchange_start(
        grads, [SDS(g.shape, g.dtype) for g in grads], view, view, token, name=f"scatter{tag}_start")
    return (grads, lands, send_sems, recv_sems, view, view, f"scatter{tag}"), token


def _exchange_finish(handle, after, which=None, tag=""):
    srcs, lands, send_sems, recv_sems, src_view, dst_view, name = handle
    which = tuple(range(len(srcs))) if which is None else which
    return _exchange_wait(srcs, lands, send_sems, recv_sems, after, src_view, dst_view, which, name=f"{name}{tag}_wait")


def _swap_start(bufs, token, *, name):
    n = len(bufs)

    def body(*refs):
        s, l, tok_in = refs[:n], refs[n:2 * n], refs[2 * n]
        send_sems, recv_sems = refs[2 * n + 1], refs[2 * n + 2]
        x, y, c = lax.axis_index("x"), lax.axis_index("y"), lax.axis_index("c")
        for t in range(n):
            pltpu.make_async_remote_copy(src_ref=s[t], dst_ref=l[t], send_sem=send_sems.at[t], recv_sem=recv_sems.at[t],
                                         device_id=(x, y, 1 - c), device_id_type=MESH).start()
        refs[-1][...] = tok_in[...]

    hbm = lambda t: pltpu.with_memory_space_constraint(t, pltpu.HBM)
    lands = [lax.empty(b.shape, b.dtype) for b in bufs]
    outs = pl.pallas_call(
        body, name=f"{name}_start",
        out_shape=(pltpu.SemaphoreType.DMA((n,)), pltpu.SemaphoreType.DMA((n,)),
                   *[pltpu.HBM(b.shape, b.dtype) for b in bufs], SDS(TOKEN_SHAPE, F32)),
        in_specs=[HBM_SPEC] * (2 * n) + [VMEM_SPEC], out_specs=(SEM_SPEC, SEM_SPEC, *[HBM_SPEC] * n, VMEM_SPEC),
        input_output_aliases={n + t: 2 + t for t in range(n)},
        compiler_params=pltpu.CompilerParams(has_side_effects=DATAFLOW))(*[hbm(t) for t in bufs], *[hbm(t) for t in lands], token)
    return (bufs, list(outs[2:2 + n]), outs[0], outs[1], name), outs[-1]


def _swap_wait(handle, after):
    bufs, lands, send_sems, recv_sems, name = handle
    n = len(bufs)

    def body(*refs):
        s, l = refs[:n], refs[n:2 * n]
        send_ref, recv_ref = refs[2 * n], refs[2 * n + 1]
        x, y, c = lax.axis_index("x"), lax.axis_index("y"), lax.axis_index("c")
        for t in range(n):
            cp = pltpu.make_async_remote_copy(src_ref=s[t], dst_ref=l[t], send_sem=send_ref.at[t], recv_sem=recv_ref.at[t],
                                              device_id=(x, y, 1 - c), device_id_type=MESH)
            cp.wait_send()
            cp.wait_recv()

    outs = pl.pallas_call(
        body, name=f"{name}_wait", out_shape=[pltpu.HBM(b.shape, b.dtype) for b in bufs],
        in_specs=[HBM_SPEC] * (2 * n) + [SEM_SPEC, SEM_SPEC, ANY_SPEC], out_specs=[HBM_SPEC] * n,
        input_output_aliases={n + t: t for t in range(n)},
        compiler_params=pltpu.CompilerParams(has_side_effects=DATAFLOW))(*bufs, *lands, send_sems, recv_sems, after)
    return list(outs)


def _all_gather8(src, *, name):
    def body(src_ref, out_ref, send_sems, recv_sems, local_sem):
        x, y, c = lax.axis_index("x"), lax.axis_index("y"), lax.axis_index("c")
        me = 4 * x + 2 * y + c
        mine = pltpu.make_async_copy(src_ref, out_ref.at[me], local_sem)
        mine.start()

        def peer(k):
            return (x ^ (k >> 2 & 1), y ^ (k >> 1 & 1), c ^ (k & 1))

        sends = []
        for k in range(1, N_DEV):
            cp = pltpu.make_async_remote_copy(src_ref=src_ref, dst_ref=out_ref.at[me], send_sem=send_sems.at[k - 1],
                                              recv_sem=recv_sems.at[k - 1], device_id=peer(k), device_id_type=MESH)
            cp.start()
            sends.append(cp)
        for k in range(1, N_DEV):
            px, py, pc = peer(k)
            pltpu.make_async_remote_copy(src_ref=src_ref, dst_ref=out_ref.at[4 * px + 2 * py + pc],
                                         send_sem=send_sems.at[k - 1], recv_sem=recv_sems.at[k - 1],
                                         device_id=peer(k), device_id_type=MESH).wait_recv()
        for cp in sends:
            cp.wait_send()
        mine.wait()

    any_spec = pl.BlockSpec(memory_space=pl.ANY)
    return pl.pallas_call(
        body, in_specs=[any_spec], out_specs=any_spec, out_shape=SDS((N_DEV,) + src.shape, src.dtype),
        scratch_shapes=[pltpu.SemaphoreType.DMA((N_DEV - 1,)), pltpu.SemaphoreType.DMA((N_DEV - 1,)), pltpu.SemaphoreType.DMA],
        name=name)(src)


def _adam_terms(w, g, m, v):
    m = ADAM_B1 * m + (1.0 - ADAM_B1) * g
    v = ADAM_B2 * v + (1.0 - ADAM_B2) * (g * g)
    m_hat = m / (1.0 - ADAM_B1 ** ADAM_STEP)
    v_hat = v / (1.0 - ADAM_B2 ** ADAM_STEP)
    delta = -ADAM_LR * (m_hat / (jnp.sqrt(v_hat) + ADAM_EPS) + ADAM_WD * w)
    return delta, m, v


def _adamw_shard(mine, other, w, m, v, *, name):
    d, a, b = w.shape
    tr = next((t for t in (128, 64, 32, 16) if a % t == 0), a)

    def body(*refs):
        ga, gb = refs[:d], refs[d:2 * d]
        w_ref, m_ref, v_ref, g_ref, d_ref, nm_ref, nv_ref = refs[2 * d:]

        def plane(ref):
            return ((ref[0].astype(F32) + ref[1].astype(F32)) + ref[2].astype(F32)) + ref[3].astype(F32)

        for lp in range(d):
            @pl.when(pl.program_id(0) == lp)
            def _(lp=lp):
                g = plane(ga[lp]) + plane(gb[lp])
                delta, mn, vn = _adam_terms(w_ref[...], g, m_ref[...], v_ref[...])
                g_ref[...] = g
                d_ref[...] = delta
                nm_ref[...] = mn
                nv_ref[...] = vn

    gspecs = [pl.BlockSpec((N_CHIPS, tr, b), lambda l, i, lp=lp: (0, jnp.where(l == lp, i, 0), 0)) for lp in range(d)]
    blk = pl.BlockSpec((None, tr, b), lambda l, i: (l, i, 0))
    shp = SDS((d, a, b), F32)
    return pl.pallas_call(
        body, grid=(d, a // tr), in_specs=gspecs + gspecs + [blk, blk, blk], out_specs=(blk,) * 4, out_shape=(shp,) * 4,
        name=name, compiler_params=_params(("arbitrary", "arbitrary"), 48 << 20))(*mine, *other, w, m, v)


def _adamw_small(g8, w, m, v, *, name):
    n = w.shape[1]

    def body(g8_ref, w_ref, m_ref, v_ref, g_ref, d_ref, nm_ref, nv_ref):
        g = g8_ref[0]
        for k in range(1, N_DEV):
            g = g + g8_ref[k]
        delta, mn, vn = _adam_terms(w_ref[...], g, m_ref[...], v_ref[...])
        g_ref[...] = g
        d_ref[...] = delta
        nm_ref[...] = mn
        nv_ref[...] = vn

    shp = SDS((1, n), F32)
    return pl.pallas_call(body, out_shape=(shp,) * 4, name=name, compiler_params=_params(None, 24 << 20))(g8, w, m, v)


def _rope_tables(positions):
    inv_freq = 1.0 / (ROPE_THETA ** (jnp.arange(0, QK_ROPE, 2, dtype=F32) / QK_ROPE))
    ang = positions.astype(F32)[:, None] * inv_freq
    c, s = jnp.cos(ang), jnp.sin(ang)
    n = positions.shape[0]
    pad = jnp.zeros((n, HEAD_PAD - QK_NOPE - QK_ROPE), F32)
    cos = jnp.concatenate([jnp.ones((n, QK_NOPE), F32), c, c, pad], axis=1)
    sins = jnp.concatenate([jnp.zeros((n, QK_NOPE), F32), -s, s, pad], axis=1)
    return cos, sins


def _pad_lanes(v):
    return _pad_cols(v.reshape(1, -1), LANES)


def _local_step(x, mem, positions, weights, small, final_norm, loss_target, emit, token):
    cos, sins = _rope_tables(positions)
    saved, gws, wls, sps = [], [], [], []
    h = x
    for li in range(DEPTH):
        sp = {k: small[k][li] for k, _ in SMALL}
        if li == 0:
            sp["norm_mix"] = sp["norm_mix"] + token[0, 0]
        for k in ("dt_bias", "a_log", "d_skip"):
            sp[k] = _pad_lanes(sp[k])
        h, sv, gw, wl = _layer_fwd(h, mem, cos, sins, functools.partial(weights, li), sp, li)
        saved.append(sv)
        gws.append(gw)
        wls.append(wl)
        sps.append(sp)
    loss, dh, dhb, g_final = _final_loss(h, final_norm, loss_target, name="final_loss")
    grads = [None] * DEPTH
    started = None
    for li in reversed(range(DEPTH)):
        sp = sps[li]
        if started is not None:
            sp = dict(sp, ffn_conv_b=sp["ffn_conv_b"] + started[0, 0])
        dh, dhb, grads[li], started = _layer_bwd(dh, dhb, mem, cos, sins, gws[li], wls[li], sp, saved[li], li,
                                                 functools.partial(emit, li))
    return loss, dh, grads, g_final


def _gathered_views(which, lands):
    return {MATS[t][0]: (b.reshape(-1, b.shape[-1]) if MATS[t][2] == 0 else b) for t, b in zip(which, lands)}


def kernel(x, mem, positions, norm_mix, w_in, ssm_conv_w, ssm_conv_b, dt_bias, a_log, d_skip, ssm_norm, q_norm, w_uq, kv_norm, w_ukv, attn_out_norm, w_out, norm_mem_q, norm_mem_kv, w_mq, w_mk, w_mv, w_mo, norm_ffn, w_up, ffn_conv_w, ffn_conv_b, w_down, final_norm, loss_target, m_norm_mix, m_w_in, m_ssm_conv_w, m_ssm_conv_b, m_dt_bias, m_a_log, m_d_skip, m_ssm_norm, m_q_norm, m_w_uq, m_kv_norm, m_w_ukv, m_attn_out_norm, m_w_out, m_norm_mem_q, m_norm_mem_kv, m_w_mq, m_w_mk, m_w_mv, m_w_mo, m_norm_ffn, m_w_up, m_ffn_conv_w, m_ffn_conv_b, m_w_down, m_final_norm, v_norm_mix, v_w_in, v_ssm_conv_w, v_ssm_conv_b, v_dt_bias, v_a_log, v_d_skip, v_ssm_norm, v_q_norm, v_w_uq, v_kv_norm, v_w_ukv, v_attn_out_norm, v_w_out, v_norm_mem_q, v_norm_mem_kv, v_w_mq, v_w_mk, v_w_mv, v_w_mo, v_norm_ffn, v_w_up, v_ffn_conv_w, v_ffn_conv_b, v_w_down, v_final_norm):
    args = dict(locals())
    names = ["norm_mix", "w_in", "ssm_conv_w", "ssm_conv_b", "dt_bias", "a_log", "d_skip", "ssm_norm", "q_norm", "w_uq",
             "kv_norm", "w_ukv", "attn_out_norm", "w_out", "norm_mem_q", "norm_mem_kv", "w_mq", "w_mk", "w_mv", "w_mo",
             "norm_ffn", "w_up", "ffn_conv_w", "ffn_conv_b", "w_down", "final_norm"]
    wts = {k: args[k] for k in names}
    mom = {k: args["m_" + k] for k in names}
    var = {k: args["v_" + k] for k in names}
    mat_names = [k for k, _, _ in MATS]

    shards = [wts[k] if k in F32_ON_WIRE else wts[k].astype(BF16) for k in mat_names]
    token = jnp.zeros(TOKEN_SHAPE, F32)
    gathers = []
    for li in range(DEPTH):
        handle, token = _gather_layer_start(shards, li, token)
        gathers.append(handle)
    small = {k: wts[k] for k, _ in SMALL}

    def weights(li, group, after):
        which = GROUPS[group]
        return _gathered_views(which, _exchange_finish(gathers[li], after, which, tag=f"_{group}"))

    scatters = [[] for _ in range(DEPTH)]
    nm = len(mat_names)
    mine = [[None] * nm for _ in range(DEPTH)]
    swaps = [None] * DEPTH

    def swap_layer(li, after):
        for which, handle in scatters[li]:
            for t, b in zip(which, _exchange_finish(handle, after)):
                mine[li][t] = b
        swaps[li], started = _swap_start(mine[li], jnp.zeros(TOKEN_SHAPE, F32), name=f"swap{li}")
        return started

    def emit(li, group, g):
        last = group == "proj"
        if li == 0:
            which = GROUPS[group]
        elif last:
            which = tuple(range(nm))
        else:
            return None
        handle, started = _scatter_start([g[MATS[t][0]] for t in which], f"{li}_{group}", jnp.zeros(TOKEN_SHAPE, F32))
        scatters[li].append((which, handle))
        if li + 1 < DEPTH and group == ("mixer" if li == 0 else "proj"):
            started = started + swap_layer(li + 1, g["ssm_conv_w"])
        return started

    loss, grad_x, grads, g_final = _local_step(x[0], mem[0], positions[0], weights, small, wts["final_norm"],
                                               loss_target[0], emit, token)
    loss = lax.psum(loss, ("x", "y", "c"))

    swap_layer(0, grad_x)
    other = [_swap_wait(swaps[li], grad_x) for li in range(DEPTH)]
    mat_out = {k: _adamw_shard([mine[li][t] for li in range(DEPTH)], [other[li][t] for li in range(DEPTH)],
                               wts[k], mom[k], var[k], name=f"adamw_{k}") for t, k in enumerate(mat_names)}

    def pack_small(get, fin):
        flat = [get(k).reshape(-1) for k, _ in SMALL] + [fin.reshape(-1)]
        n = sum(f.shape[0] for f in flat)
        return jnp.concatenate(flat + [jnp.zeros((-n % PACK_COLS,), F32)]).reshape(1, -1)

    gs = pack_small(lambda k: jnp.stack([grads[li][k] for li in range(DEPTH)]), g_final)
    g8 = _all_gather8(gs, name="gather_small_grads")
    small_out = _adamw_small(g8, pack_small(wts.get, wts["final_norm"]), pack_small(mom.get, mom["final_norm"]),
                             pack_small(var.get, var["final_norm"]), name="adamw_small")

    def unpack_small(buf):
        out, off = {}, 0
        for k, nel in SMALL:
            out[k] = buf[0, off:off + DEPTH * nel].reshape(DEPTH, nel)
            off += DEPTH * nel
        out["final_norm"] = buf[0, off:off + D_MODEL]
        return out

    small_res = [unpack_small(b) for b in small_out]
    res = []
    for kind in range(4):
        for k in names:
            res.append(small_res[kind][k] if k in small_res[kind] else mat_out[k][kind])
    return (loss, grad_x[None], *res)
```

```python
import functools
import math

import jax
import jax.numpy as jnp
from jax import lax
from jax.experimental import pallas as pl
from jax.experimental.pallas import tpu as pltpu

F32 = jnp.float32
BF16 = jnp.bfloat16
HIGHEST = lax.Precision.HIGHEST
SDS = jax.ShapeDtypeStruct
MESH = pl.DeviceIdType.MESH

D_MODEL = 1024
DEPTH = 4
EPS = 1e-6
SSM_HEADS = 16
SSM_HEAD_DIM = 64
D_SSM = 1024
SSM_GROUPS = 4
SSM_STATE = 128
SSM_CONV = 4
SSM_CHUNK = 128
CONV_CH = 2048
MLA_HEADS = 16
QK_NOPE = 64
QK_ROPE = 32
V_DIM = 64
Q_LORA = 384
KV_LORA = 256
ROPE_THETA = 10000.0
MEM_HEADS = 4
MEM_HEAD_DIM = 256
D_FF = 2816
FFN_CONV = 3
D_IN = 3760
ADAM_LR = 0.001
ADAM_B1 = 0.9
ADAM_B2 = 0.999
ADAM_EPS = 1e-08
ADAM_WD = 0.01
ADAM_STEP = 10

LANES = 128
HEAD_PAD = 128
N_CHIPS = 4
N_DEV = 8
VMEM_CAP_MB = 56

P_XBC, P_Z, P_CQ, P_DT, P_CKV, P_KR, P_IN = 0, 2048, 3072, 3456, 3584, 3840, 4096
NEG = -1e30


def _tile(n, pref):
    t = (min(n, pref) // LANES) * LANES
    while t >= LANES:
        if n % t == 0:
            return t
        t -= LANES
    return n


def _params(sem=None, vmem_bytes=None):
    kw = {}
    if sem is not None:
        kw["dimension_semantics"] = sem
    if vmem_bytes is not None:
        kw["vmem_limit_bytes"] = int(min(max(vmem_bytes, 16 << 20), VMEM_CAP_MB << 20))
    return pltpu.CompilerParams(**kw)


def _nbytes(shape, dtype):
    return math.prod(shape) * jnp.dtype(dtype).itemsize


def _mm(a, b, *, ta=False, tb=False, res=None, out_dtype=F32, name, a_col=None, b_lead=(), b_rows=None,
        b_chips=None, o_chips=None):
    if ta:
        k, m = a.shape
    else:
        m, k = (a.shape[0], a.shape[1] if a_col is None else a_col[0])
    rows_b, cols_b = b.shape[-2:]
    row0 = 0
    if b_rows is not None:
        row0, rows_b = b_rows
    nlead = len(b_lead)
    if b_chips is not None:
        assert not tb
        kb, tn, n = rows_b, cols_b, b_chips[1] * cols_b
        b_blk = (None,) * (1 + nlead) + (kb, tn)
        b_map = lambda i, j: (b_chips[0] + j,) + tuple(b_lead) + (0, 0)
    elif tb:
        n, kb = rows_b, cols_b
        tn = _tile(n, 512)
        assert row0 % tn == 0
        b_blk = (None,) * nlead + (tn, kb)
        b_map = lambda i, j: tuple(b_lead) + (j + row0 // tn, 0)
    else:
        kb, n = rows_b, cols_b
        tn = o_chips if o_chips else _tile(n, 512)
        assert row0 % kb == 0
        b_blk = (None,) * nlead + (kb, tn)
        b_map = lambda i, j: tuple(b_lead) + (row0 // kb, j)
    assert k == kb, (a.shape, b.shape, ta, tb, k, kb)
    tm = _tile(m, 512)
    if ta:
        a_blk, a_map = (k, tm), (lambda i, j: (0, i))
    else:
        a_blk, a_map = (tm, k), ((lambda i, j: (i, 0)) if a_col is None else (lambda i, j: (i, a_col[1])))
    if o_chips:
        o_spec = pl.BlockSpec((None, tm, tn), lambda i, j: (j, i, 0))
        o_shape = SDS((n // tn, m, tn), out_dtype)
    else:
        o_spec = pl.BlockSpec((tm, tn), lambda i, j: (i, j))
        o_shape = SDS((m, n), out_dtype)
    dims = (((0 if ta else 1,), (1 if tb else 0,)), ((), ()))
    has_res = res is not None

    def body(*refs):
        a_ref, b_ref = refs[0], refs[1]
        o_ref = refs[-1]
        acc = lax.dot_general(a_ref[...].astype(BF16), b_ref[...].astype(BF16), dims, preferred_element_type=F32)
        if has_res:
            acc = acc + refs[2][...]
        o_ref[...] = acc.astype(o_ref.dtype)

    bb = tuple(d for d in b_blk if d is not None)
    vmem = 2 * (_nbytes(a_blk, a.dtype) + _nbytes(bb, b.dtype) + (2 if has_res else 1) * _nbytes((tm, tn), F32))
    vmem += _nbytes(a_blk, BF16) + _nbytes(bb, BF16) + 2 * _nbytes((tm, tn), F32) + (4 << 20)
    args = (a, b) + ((res,) if has_res else ())
    specs = [pl.BlockSpec(a_blk, a_map), pl.BlockSpec(b_blk, b_map)] + ([o_spec] if has_res else [])
    return pl.pallas_call(body, grid=(m // tm, n // tn), in_specs=specs, out_specs=o_spec, out_shape=o_shape, name=name,
                          compiler_params=_params(("parallel", "parallel"), vmem))(*args)


def _sigmoid(x):
    return 1.0 / (1.0 + jnp.exp(-x))


def _rms_fwd(x, g, *, col=None, name):
    s = x.shape[0]
    w, ci = (x.shape[1], 0) if col is None else col
    tm = min(s, 512)

    def body(x_ref, g_ref, o_ref):
        xv = x_ref[...].astype(F32)
        r = lax.rsqrt(jnp.mean(xv * xv, axis=-1, keepdims=True) + EPS)
        o_ref[...] = (xv * r * g_ref[...]).astype(o_ref.dtype)

    return pl.pallas_call(
        body, grid=(s // tm,),
        in_specs=[pl.BlockSpec((tm, w), lambda i: (i, ci)), pl.BlockSpec((1, w), lambda i: (0, 0))],
        out_specs=pl.BlockSpec((tm, w), lambda i: (i, 0)), out_shape=SDS((s, w), BF16), name=name,
        compiler_params=_params(("parallel",), 10 * tm * w * 4))(x, g.reshape(1, w))


def _rms_bwd(x, g, dy, dres=None, *, col=None, name):
    s = x.shape[0]
    w, ci = (x.shape[1], 0) if col is None else col
    tm = min(s, 512)
    has_res = dres is not None

    def body(*refs):
        x_ref, g_ref, dy_ref = refs[:3]
        dx_ref, dxb_ref, dg_ref = refs[-3:]
        xv = x_ref[...].astype(F32)
        dyv = dy_ref[...].astype(F32)
        r = lax.rsqrt(jnp.mean(xv * xv, axis=-1, keepdims=True) + EPS)
        u = dyv * g_ref[...]
        dx = r * u - xv * (r * r * r) * jnp.mean(xv * u, axis=-1, keepdims=True)
        if has_res:
            dx = dx + refs[3][...]
        dx_ref[...] = dx
        dxb_ref[...] = dx.astype(BF16)

        @pl.when(pl.program_id(0) == 0)
        def _():
            dg_ref[...] = jnp.zeros_like(dg_ref)

        dg_ref[...] += jnp.sum(dyv * xv * r, axis=0, keepdims=True)

    blk = pl.BlockSpec((tm, w), lambda i: (i, 0))
    specs = [pl.BlockSpec((tm, w), lambda i: (i, ci)), pl.BlockSpec((1, w), lambda i: (0, 0)), blk]
    args = [x, g.reshape(1, w), dy]
    if has_res:
        specs.append(blk)
        args.append(dres)
    dx, dxb, dg = pl.pallas_call(
        body, grid=(s // tm,), in_specs=specs,
        out_specs=(blk, blk, pl.BlockSpec((1, w), lambda i: (0, 0))),
        out_shape=(SDS((s, w), F32), SDS((s, w), BF16), SDS((1, w), F32)), name=name,
        compiler_params=_params(("arbitrary",), 18 * tm * w * 4))(*args)
    return dx, dxb, dg.reshape(w)


def _gated_rms_fwd(y, proj, g, *, name):
    s, w = y.shape
    tm = min(s, 512)

    def body(y_ref, z_ref, g_ref, o_ref):
        z = z_ref[...]
        t = y_ref[...] * (z * _sigmoid(z))
        r = lax.rsqrt(jnp.mean(t * t, axis=-1, keepdims=True) + EPS)
        o_ref[...] = (t * r * g_ref[...]).astype(o_ref.dtype)

    blk = pl.BlockSpec((tm, w), lambda i: (i, 0))
    return pl.pallas_call(
        body, grid=(s // tm,),
        in_specs=[blk, pl.BlockSpec((tm, w), lambda i: (i, P_Z // w)), pl.BlockSpec((1, w), lambda i: (0, 0))],
        out_specs=blk, out_shape=SDS((s, w), BF16), name=name,
        compiler_params=_params(("parallel",), 14 * tm * w * 4))(y, proj, g.reshape(1, w))


def _gated_rms_bwd(y, proj, g, dout, *, name):
    s, w = y.shape
    tm = min(s, 512)

    def body(y_ref, z_ref, g_ref, do_ref, dy_ref, dz_ref, dg_ref):
        z = z_ref[...]
        yv = y_ref[...]
        dov = do_ref[...]
        sg = _sigmoid(z)
        sz = z * sg
        t = yv * sz
        r = lax.rsqrt(jnp.mean(t * t, axis=-1, keepdims=True) + EPS)
        u = dov * g_ref[...]
        dt = r * u - t * (r * r * r) * jnp.mean(t * u, axis=-1, keepdims=True)
        dy_ref[...] = dt * sz
        dz_ref[...] = (dt * yv * (sg * (1.0 + z * (1.0 - sg)))).astype(dz_ref.dtype)

        @pl.when(pl.program_id(0) == 0)
        def _():
            dg_ref[...] = jnp.zeros_like(dg_ref)

        dg_ref[...] += jnp.sum(dov * t * r, axis=0, keepdims=True)

    blk = pl.BlockSpec((tm, w), lambda i: (i, 0))
    vec = pl.BlockSpec((1, w), lambda i: (0, 0))
    dy, dz, dg = pl.pallas_call(
        body, grid=(s // tm,),
        in_specs=[blk, pl.BlockSpec((tm, w), lambda i: (i, P_Z // w)), vec, blk],
        out_specs=(blk, blk, vec), out_shape=(SDS((s, w), F32), SDS((s, w), BF16), SDS((1, w), F32)), name=name,
        compiler_params=_params(("arbitrary",), 24 * tm * w * 4))(y, proj, g.reshape(1, w), dout)
    return dy, dz, dg.reshape(w)


def _final_loss(x, g, target, *, name):
    s, w = x.shape
    tm = min(s, 512)

    def body(x_ref, g_ref, t_ref, loss_ref, dx_ref, dxb_ref, dg_ref):
        xv = x_ref[...]
        gv = g_ref[...]
        r = lax.rsqrt(jnp.mean(xv * xv, axis=-1, keepdims=True) + EPS)
        xn = xv * r
        diff = xn * gv - t_ref[...]
        dy = diff * (1.0 / w)
        u = dy * gv
        dx = r * u - xv * (r * r * r) * jnp.mean(xv * u, axis=-1, keepdims=True)
        dx_ref[...] = dx
        dxb_ref[...] = dx.astype(BF16)

        @pl.when(pl.program_id(0) == 0)
        def _():
            dg_ref[...] = jnp.zeros_like(dg_ref)
            loss_ref[...] = jnp.zeros_like(loss_ref)

        dg_ref[...] += jnp.sum(dy * xn, axis=0, keepdims=True)
        part = jnp.sum(jnp.sum(diff * diff, axis=1, keepdims=True), axis=0, keepdims=True) * (0.5 / w)
        loss_ref[...] += jnp.broadcast_to(part, loss_ref.shape)

    blk = pl.BlockSpec((tm, w), lambda i: (i, 0))
    vec = pl.BlockSpec((1, w), lambda i: (0, 0))
    loss, dx, dxb, dg = pl.pallas_call(
        body, grid=(s // tm,), in_specs=[blk, vec, blk],
        out_specs=(pl.BlockSpec((1, LANES), lambda i: (0, 0)), blk, blk, vec),
        out_shape=(SDS((1, LANES), F32), SDS((s, w), F32), SDS((s, w), BF16), SDS((1, w), F32)), name=name,
        compiler_params=_params(("arbitrary",), 18 * tm * w * 4))(x, g.reshape(1, w), target)
    return loss[0, 0], dx, dxb, dg.reshape(w)


def _shift_down(x, k):
    if k == 0:
        return x
    row = lax.broadcasted_iota(jnp.int32, x.shape, 0)
    return jnp.where(row < k, 0.0, pltpu.roll(x, k, axis=0))


def _shift_up(x, k):
    if k == 0:
        return x
    s = x.shape[0]
    row = lax.broadcasted_iota(jnp.int32, x.shape, 0)
    return jnp.where(row >= s - k, 0.0, pltpu.roll(x, s - k, axis=0))


def _conv_pre(x, w, b, kw):
    pre = b
    for j in range(kw):
        pre = pre + w[j:j + 1, :] * _shift_down(x, kw - 1 - j)
    return pre


def _conv_bwd_terms(x, w, dpre, kw):
    dx = jnp.zeros_like(x)
    dws = []
    for j in range(kw):
        dx = dx + w[j:j + 1, :] * _shift_up(dpre, kw - 1 - j)
        dws.append(jnp.sum(dpre * _shift_down(x, kw - 1 - j), axis=0, keepdims=True))
    return dx, jnp.concatenate(dws, axis=0), jnp.sum(dpre, axis=0, keepdims=True)


def _ssm_conv_fwd(proj, w, b, *, name):
    s = proj.shape[0]
    cw = 256

    def body(x_ref, w_ref, b_ref, o_ref):
        pre = _conv_pre(x_ref[...], w_ref[...], b_ref[...], SSM_CONV)
        o_ref[...] = pre * _sigmoid(pre)

    return pl.pallas_call(
        body, grid=(CONV_CH // cw,),
        in_specs=[pl.BlockSpec((s, cw), lambda j: (0, j)), pl.BlockSpec((SSM_CONV, cw), lambda j: (0, j)),
                  pl.BlockSpec((1, cw), lambda j: (0, j))],
        out_specs=pl.BlockSpec((s, cw), lambda j: (0, j)), out_shape=SDS((s, CONV_CH), F32), name=name,
        compiler_params=_params(("parallel",), 12 * s * cw * 4))(proj, w, b.reshape(1, CONV_CH))


def _ssm_conv_bwd(proj, w, b, dxbc, *, name):
    s = proj.shape[0]
    cw = 256

    def body(x_ref, w_ref, b_ref, dy_ref, dx_ref, dw_ref, db_ref):
        x = x_ref[...]
        wv = w_ref[...]
        pre = _conv_pre(x, wv, b_ref[...], SSM_CONV)
        sg = _sigmoid(pre)
        dpre = dy_ref[...] * (sg * (1.0 + pre * (1.0 - sg)))
        dx, dw, db = _conv_bwd_terms(x, wv, dpre, SSM_CONV)
        dx_ref[...] = dx.astype(dx_ref.dtype)
        dw_ref[...] = dw
        db_ref[...] = db

    col = pl.BlockSpec((s, cw), lambda j: (0, j))
    wsp = pl.BlockSpec((SSM_CONV, cw), lambda j: (0, j))
    bsp = pl.BlockSpec((1, cw), lambda j: (0, j))
    dx, dw, db = pl.pallas_call(
        body, grid=(CONV_CH // cw,), in_specs=[col, wsp, bsp, col], out_specs=(col, wsp, bsp),
        out_shape=(SDS((s, CONV_CH), BF16), SDS((SSM_CONV, CONV_CH), F32), SDS((1, CONV_CH), F32)), name=name,
        compiler_params=_params(("parallel",), 20 * s * cw * 4))(proj, w, b.reshape(1, CONV_CH), dxbc)
    return dx, dw, db.reshape(CONV_CH)


def _ffn_conv_fwd(up_g, up_v, w, b, *, name):
    s = up_g.shape[0]
    cw = 256
    nb = D_FF // cw

    def body(g_ref, v_ref, wg_ref, wv_ref, bg_ref, bv_ref, o_ref):
        gate = _conv_pre(g_ref[...], wg_ref[...], bg_ref[...], FFN_CONV)
        val = _conv_pre(v_ref[...], wv_ref[...], bv_ref[...], FFN_CONV)
        o_ref[...] = (gate * _sigmoid(gate) * val).astype(o_ref.dtype)

    col = pl.BlockSpec((s, cw), lambda j: (0, j))
    b2 = b.reshape(1, 2 * D_FF)
    return pl.pallas_call(
        body, grid=(nb,),
        in_specs=[col, col, pl.BlockSpec((FFN_CONV, cw), lambda j: (0, j)), pl.BlockSpec((FFN_CONV, cw), lambda j: (0, j + nb)),
                  pl.BlockSpec((1, cw), lambda j: (0, j)), pl.BlockSpec((1, cw), lambda j: (0, j + nb))],
        out_specs=col, out_shape=SDS((s, D_FF), BF16), name=name,
        compiler_params=_params(("parallel",), 16 * s * cw * 4))(up_g, up_v, w, w, b2, b2)


def _ffn_conv_bwd(up_g, up_v, w, b, dact, *, name):
    s = up_g.shape[0]
    cw = 256
    nb = D_FF // cw

    def body(g_ref, v_ref, wg_ref, wv_ref, bg_ref, bv_ref, da_ref, dg_ref, dv_ref, dwg_ref, dwv_ref, dbg_ref, dbv_ref):
        xg, xv = g_ref[...], v_ref[...]
        wg, wv = wg_ref[...], wv_ref[...]
        gate = _conv_pre(xg, wg, bg_ref[...], FFN_CONV)
        val = _conv_pre(xv, wv, bv_ref[...], FFN_CONV)
        da = da_ref[...].astype(F32)
        sg = _sigmoid(gate)
        dgate = da * val * (sg * (1.0 + gate * (1.0 - sg)))
        dval = da * gate * sg
        dxg, dwg, dbg = _conv_bwd_terms(xg, wg, dgate, FFN_CONV)
        dxv, dwv, dbv = _conv_bwd_terms(xv, wv, dval, FFN_CONV)
        dg_ref[...] = dxg.astype(dg_ref.dtype)
        dv_ref[...] = dxv.astype(dv_ref.dtype)
        dwg_ref[...] = dwg
        dwv_ref[...] = dwv
        dbg_ref[...] = dbg
        dbv_ref[...] = dbv

    col = pl.BlockSpec((s, cw), lambda j: (0, j))
    wsp = pl.BlockSpec((FFN_CONV, cw), lambda j: (0, j))
    bsp = pl.BlockSpec((1, cw), lambda j: (0, j))
    b2 = b.reshape(1, 2 * D_FF)
    dg, dv, dwg, dwv, dbg, dbv = pl.pallas_call(
        body, grid=(nb,),
        in_specs=[col, col, wsp, pl.BlockSpec((FFN_CONV, cw), lambda j: (0, j + nb)), bsp,
                  pl.BlockSpec((1, cw), lambda j: (0, j + nb)), col],
        out_specs=(col, col, wsp, wsp, bsp, bsp),
        out_shape=(SDS((s, D_FF), BF16), SDS((s, D_FF), BF16), SDS((FFN_CONV, D_FF), F32), SDS((FFN_CONV, D_FF), F32),
                   SDS((1, D_FF), F32), SDS((1, D_FF), F32)), name=name,
        compiler_params=_params(("parallel",), 32 * s * cw * 4))(up_g, up_v, w, w, b2, b2, dact)
    return dg, dv, jnp.concatenate([dwg, dwv], axis=1), jnp.concatenate([dbg, dbv], axis=1).reshape(2 * D_FF)


def _dot(a, b):
    return jnp.dot(a.astype(BF16), b.astype(BF16), preferred_element_type=F32)


def _dot_nt(a, b):
    return lax.dot_general(a.astype(BF16), b.astype(BF16), (((1,), (1,)), ((), ())), preferred_element_type=F32)


def _dot_tn(a, b):
    return lax.dot_general(a.astype(BF16), b.astype(BF16), (((0,), (0,)), ((), ())), preferred_element_type=F32)


def _ssd_chunk_terms(dtraw, bias, a_log):
    ell = dtraw.shape[0]
    lane = lax.broadcasted_iota(jnp.int32, dtraw.shape, 1)
    valid = lane < SSM_HEADS
    pre = dtraw + bias
    dt = jnp.where(valid, jnp.where(pre > 20.0, pre, jnp.log(1.0 + jnp.exp(jnp.minimum(pre, 20.0)))), 0.0)
    a = -jnp.exp(a_log)
    ad = dt * a
    row = lax.broadcasted_iota(jnp.int32, (ell, ell), 0)
    colm = lax.broadcasted_iota(jnp.int32, (ell, ell), 1)
    tril = row >= colm
    cs = jnp.dot(tril.astype(F32), ad, precision=HIGHEST, preferred_element_type=F32)
    cs_last = cs[ell - 1:ell, :]
    return pre, dt, a, cs, cs_last, tril


def _head_expand():
    h = lax.broadcasted_iota(jnp.int32, (LANES, D_SSM), 0)
    c = lax.broadcasted_iota(jnp.int32, (LANES, D_SSM), 1)
    return (c // SSM_HEAD_DIM == h).astype(F32)


def _ssd_fwd(xbc, proj, dt_bias, a_log, d_skip, *, name):
    s = xbc.shape[0]
    nc = s // SSM_CHUNK
    ell, n, p = SSM_CHUNK, SSM_STATE, SSM_HEAD_DIM
    rpg = SSM_HEADS // SSM_GROUPS
    gw = rpg * p

    def body(x_ref, dt_ref, bias_ref, alog_ref, dskip_ref, ex_ref, y_ref, ps_ref, state):
        @pl.when(pl.program_id(0) == 0)
        def _():
            state[...] = jnp.zeros_like(state)

        _, dt, _, cs, cs_last, tril = _ssd_chunk_terms(dt_ref[...], bias_ref[...], alog_ref[...])
        cst = cs.T
        ex = ex_ref[...]
        spread = lambda v: jnp.dot(v, ex, precision=HIGHEST, preferred_element_type=F32)
        dt_x, e_x, ds_x = spread(dt), spread(jnp.exp(cs)), spread(jnp.exp(cs_last - cs))
        cd_x = spread(jnp.broadcast_to(jnp.exp(cs_last), (8, LANES)))[0:1, :]
        dskip_x = spread(jnp.broadcast_to(dskip_ref[...], (8, LANES)))[0:1, :]
        st = state[...]
        ps_ref[0] = st
        xv = x_ref[...]
        xs_all = xv[:, 0:D_SSM]
        xd_all = xs_all * dt_x
        xdd_all = xd_all * ds_x
        lane_g = lax.broadcasted_iota(jnp.int32, (ell, gw), 1)
        ys, new = [], []
        for g in range(SSM_GROUPS):
            gs = slice(gw * g, gw * (g + 1))
            bg = xv[:, D_SSM + n * g:D_SSM + n * (g + 1)]
            cg = xv[:, D_SSM + n * (SSM_GROUPS + g):D_SSM + n * (SSM_GROUPS + g + 1)]
            cb = _dot_nt(cg, bg)
            xd_g, prev_g = xd_all[:, gs], st[:, gs]
            y_g = _dot(cg, prev_g) * e_x[:, gs] + xs_all[:, gs] * dskip_x[:, gs]
            for r in range(rpg):
                h = g * rpg + r
                lmat = jnp.exp(jnp.where(tril, cs[:, h:h + 1] - cst[h:h + 1, :], -jnp.inf))
                y_g = y_g + jnp.where((lane_g >= p * r) & (lane_g < p * (r + 1)), _dot(cb * lmat, xd_g), 0.0)
            ys.append(y_g)
            new.append(prev_g * cd_x[:, gs] + _dot(bg.T, xdd_all[:, gs]))
        y_ref[...] = jnp.concatenate(ys, axis=1)
        state[...] = jnp.concatenate(new, axis=1)

    vec = pl.BlockSpec((1, LANES), lambda c: (0, 0))
    return pl.pallas_call(
        body, grid=(nc,),
        in_specs=[pl.BlockSpec((ell, CONV_CH), lambda c: (c, 0)), pl.BlockSpec((ell, LANES), lambda c: (c, P_DT // LANES)),
                  vec, vec, vec, pl.BlockSpec((LANES, D_SSM), lambda c: (0, 0))],
        out_specs=(pl.BlockSpec((ell, D_SSM), lambda c: (c, 0)), pl.BlockSpec((1, n, D_SSM), lambda c: (c, 0, 0))),
        out_shape=(SDS((s, D_SSM), F32), SDS((nc, n, D_SSM), F32)),
        scratch_shapes=[pltpu.VMEM((n, D_SSM), F32)], name=name,
        compiler_params=_params(("arbitrary",), 32 << 20))(xbc, proj, dt_bias, a_log, d_skip, _head_expand())


def _ssd_bwd(xbc, proj, dt_bias, a_log, d_skip, prev_states, dy, *, name):
    s = xbc.shape[0]
    nc = s // SSM_CHUNK
    ell, n, p = SSM_CHUNK, SSM_STATE, SSM_HEAD_DIM
    rpg = SSM_HEADS // SSM_GROUPS
    gw = rpg * p

    def body(x_ref, dt_ref, bias_ref, alog_ref, dskip_ref, ps_ref, dy_ref, ex_ref, ext_ref,
             dx_ref, ddt_ref, dalog_ref, ddskip_ref, dbias_ref, dstate):
        @pl.when(pl.program_id(0) == 0)
        def _():
            dstate[...] = jnp.zeros_like(dstate)
            dalog_ref[...] = jnp.zeros_like(dalog_ref)
            ddskip_ref[...] = jnp.zeros_like(ddskip_ref)
            dbias_ref[...] = jnp.zeros_like(dbias_ref)

        pre, dt, a, cs, cs_last, tril = _ssd_chunk_terms(dt_ref[...], bias_ref[...], alog_ref[...])
        e = jnp.exp(cs)
        ds = jnp.exp(cs_last - cs)
        cd = jnp.exp(cs_last)
        cst = cs.T
        shape = (ell, LANES)
        ex, ext = ex_ref[...], ext_ref[...]
        spread = lambda v: jnp.dot(v, ex, precision=HIGHEST, preferred_element_type=F32)
        gather = lambda v: jnp.dot(v, ext, precision=HIGHEST, preferred_element_type=F32)
        dt_x, e_x, ds_x = spread(dt), spread(e), spread(ds)
        cd_x = spread(jnp.broadcast_to(cd, (8, LANES)))[0:1, :]
        dskip_x = spread(jnp.broadcast_to(dskip_ref[...], (8, LANES)))[0:1, :]
        xv, dyv, psv, dst = x_ref[...], dy_ref[...], ps_ref[0], dstate[...]
        xs_all = xv[:, 0:D_SSM]
        xd_all = xs_all * dt_x
        dye_all = dyv * e_x
        xdd_all = xd_all * ds_x
        triu = lax.broadcasted_iota(jnp.int32, (ell, ell), 0) <= lax.broadcasted_iota(jnp.int32, (ell, ell), 1)
        lane_g = lax.broadcasted_iota(jnp.int32, (ell, gw), 1)
        lane = lax.broadcasted_iota(jnp.int32, shape, 1)
        sub = lax.broadcasted_iota(jnp.int32, shape, 0)
        dcs_acc = jnp.zeros(shape, F32)
        dcs_rows = jnp.zeros(shape, F32)
        dxs, dbs, dcs_parts, dprevs, prod_a, prod_b, prod_c, prod_e = [], [], [], [], [], [], [], []
        for g in range(SSM_GROUPS):
            gs = slice(gw * g, gw * (g + 1))
            bg = xv[:, D_SSM + n * g:D_SSM + n * (g + 1)]
            cg = xv[:, D_SSM + n * (SSM_GROUPS + g):D_SSM + n * (SSM_GROUPS + g + 1)]
            cb = _dot_nt(cg, bg)
            cbt = _dot_nt(bg, cg)
            xs_g, dy_g, xd_g, dye_g, xdd_g = xs_all[:, gs], dyv[:, gs], xd_all[:, gs], dye_all[:, gs], xdd_all[:, gs]
            prev_g, dsn_g = psv[:, gs], dst[:, gs]
            cprev_g = _dot(cg, prev_g)
            dprevs.append(dsn_g * cd_x[:, gs] + _dot(cg.T, dye_g))
            dcg = _dot_nt(dye_g, prev_g)
            dxdd_g = _dot(bg, dsn_g)
            dbg = _dot_nt(xdd_g, dsn_g)
            dxd_g = dxdd_g * ds_x[:, gs]
            prod_a.append(dy_g * cprev_g)
            prod_b.append(dxdd_g * xd_g)
            prod_e.append(jnp.sum(dsn_g * prev_g, axis=0, keepdims=True))
            dcb = jnp.zeros((ell, ell), F32)
            for r in range(rpg):
                h = g * rpg + r
                mine = (lane_g >= p * r) & (lane_g < p * (r + 1))
                lmat = jnp.exp(jnp.where(tril, cs[:, h:h + 1] - cst[h:h + 1, :], -jnp.inf))
                lmat_t = jnp.exp(jnp.where(triu, cst[h:h + 1, :] - cs[:, h:h + 1], -jnp.inf))
                dgm = _dot_nt(jnp.where(mine, dy_g, 0.0), xd_g)
                dxd_g = dxd_g + jnp.where(mine, _dot(cbt * lmat_t, dy_g), 0.0)
                mm = dgm * (cb * lmat)
                dcs_acc = dcs_acc + jnp.where(lane == h, jnp.sum(mm, axis=1, keepdims=True), 0.0)
                dcs_rows = dcs_rows + jnp.where(sub == h, jnp.sum(mm, axis=0, keepdims=True), 0.0)
                dcb = dcb + dgm * lmat
            dxs.append(dxd_g * dt_x[:, gs] + dy_g * dskip_x[:, gs])
            prod_c.append(dxd_g * xs_g)
            dbs.append(dbg + _dot_tn(dcb, cg))
            dcs_parts.append(dcg + _dot(dcb, bg))
        dx_ref[...] = jnp.concatenate(dxs + dbs + dcs_parts, axis=1)
        dstate[...] = jnp.concatenate(dprevs, axis=1)
        sum_a = gather(jnp.concatenate(prod_a, axis=1))
        sum_b = gather(jnp.concatenate(prod_b, axis=1))
        sum_c = gather(jnp.concatenate(prod_c, axis=1))
        sum_d = gather(dyv * xs_all)
        dcd = gather(jnp.broadcast_to(jnp.concatenate(prod_e, axis=1), (8, D_SSM)))[0:1, :]
        tmp = sum_b * ds
        dlast = dcd * cd + jnp.sum(tmp, axis=0, keepdims=True)
        dcs = dcs_acc + sum_a * e - tmp - dcs_rows.T + jnp.where(sub == ell - 1, dlast, 0.0)
        dad = jnp.dot(triu.astype(F32), dcs, precision=HIGHEST, preferred_element_type=F32)
        ddt = sum_c + dad * a
        dalog_ref[...] += jnp.sum(dad * dt, axis=0, keepdims=True) * a
        ddskip_ref[...] += jnp.sum(sum_d, axis=0, keepdims=True)
        ddraw = jnp.where(lane < SSM_HEADS, ddt * _sigmoid(pre), 0.0)
        ddt_ref[...] = ddraw.astype(ddt_ref.dtype)
        dbias_ref[...] += jnp.sum(ddraw, axis=0, keepdims=True)

    vec = pl.BlockSpec((1, LANES), lambda c: (0, 0))
    rev = lambda c: nc - 1 - c
    ex = _head_expand()
    outs = pl.pallas_call(
        body, grid=(nc,),
        in_specs=[pl.BlockSpec((ell, CONV_CH), lambda c: (rev(c), 0)),
                  pl.BlockSpec((ell, LANES), lambda c: (rev(c), P_DT // LANES)), vec, vec, vec,
                  pl.BlockSpec((1, n, D_SSM), lambda c: (rev(c), 0, 0)),
                  pl.BlockSpec((ell, D_SSM), lambda c: (rev(c), 0)),
                  pl.BlockSpec((LANES, D_SSM), lambda c: (0, 0)), pl.BlockSpec((D_SSM, LANES), lambda c: (0, 0))],
        out_specs=(pl.BlockSpec((ell, CONV_CH), lambda c: (rev(c), 0)), pl.BlockSpec((ell, LANES), lambda c: (rev(c), 0)),
                   vec, vec, vec),
        out_shape=(SDS((s, CONV_CH), F32), SDS((s, LANES), BF16), SDS((1, LANES), F32), SDS((1, LANES), F32),
                   SDS((1, LANES), F32)),
        scratch_shapes=[pltpu.VMEM((n, D_SSM), F32)], name=name,
        compiler_params=_params(("arbitrary",), 40 << 20))(xbc, proj, dt_bias, a_log, d_skip, prev_states, dy, ex, ex.T)
    return outs


def _rope_swap(t):
    lane = lax.broadcasted_iota(jnp.int32, t.shape, 1)
    half = QK_ROPE // 2
    lo = (lane >= QK_NOPE) & (lane < QK_NOPE + half)
    hi = (lane >= QK_NOPE + half) & (lane < QK_NOPE + QK_ROPE)
    return jnp.where(lo, pltpu.roll(t, HEAD_PAD - half, axis=1), jnp.where(hi, pltpu.roll(t, half, axis=1), 0.0))


def _mla_prep(q, kv, proj, cos, sins, *, name):
    s = q.shape[0]
    tm = min(s, 256)
    scale = (QK_NOPE + QK_ROPE) ** -0.5

    def body(q_ref, kv_ref, kr_ref, cos_ref, sin_ref, qo_ref, ko_ref, vo_ref):
        cosv, sinv = cos_ref[...], sin_ref[...]
        kr = pltpu.roll(kr_ref[...], QK_NOPE, axis=1)
        lane = lax.broadcasted_iota(jnp.int32, kr.shape, 1)
        nope = lane < QK_NOPE
        kr = jnp.where(nope, 0.0, kr)
        kpe = kr * cosv + _rope_swap(kr) * sinv
        for hp in range(MLA_HEADS // 2):
            vs = []
            for h in (2 * hp, 2 * hp + 1):
                hs = slice(HEAD_PAD * h, HEAD_PAD * (h + 1))
                qh = q_ref[:, hs]
                kvh = kv_ref[:, hs]
                qo_ref[:, hs] = ((qh * cosv + _rope_swap(qh) * sinv) * scale).astype(qo_ref.dtype)
                ko_ref[:, hs] = (jnp.where(nope, kvh, 0.0) + kpe).astype(ko_ref.dtype)
                vs.append(kvh[:, QK_NOPE:])
            vo_ref[:, 2 * V_DIM * hp:2 * V_DIM * (hp + 1)] = jnp.concatenate(vs, axis=1).astype(vo_ref.dtype)

    wide = pl.BlockSpec((tm, MLA_HEADS * HEAD_PAD), lambda i: (i, 0))
    half = pl.BlockSpec((tm, MLA_HEADS * V_DIM), lambda i: (i, 0))
    tab = pl.BlockSpec((tm, LANES), lambda i: (i, 0))
    return pl.pallas_call(
        body, grid=(s // tm,),
        in_specs=[wide, wide, pl.BlockSpec((tm, LANES), lambda i: (i, P_KR // LANES)), tab, tab],
        out_specs=(wide, wide, half),
        out_shape=(SDS((s, MLA_HEADS * HEAD_PAD), BF16), SDS((s, MLA_HEADS * HEAD_PAD), BF16),
                   SDS((s, MLA_HEADS * V_DIM), BF16)), name=name,
        compiler_params=_params(("parallel",), 32 << 20))(q, kv, proj, cos, sins)


def _mla_prep_bwd(dqr, dkr, dv, cos, sins, *, name):
    s = dqr.shape[0]
    tm = min(s, 256)
    scale = (QK_NOPE + QK_ROPE) ** -0.5

    def body(dq_ref, dk_ref, dv_ref, cos_ref, sin_ref, dqo_ref, dkv_ref, dkr_ref):
        cosv, sinv = cos_ref[...], sin_ref[...]
        lane = lax.broadcasted_iota(jnp.int32, cosv.shape, 1)
        ksum = jnp.zeros(cosv.shape, F32)
        for h in range(MLA_HEADS):
            hs = slice(HEAD_PAD * h, HEAD_PAD * (h + 1))
            d = dq_ref[:, hs]
            dk = dk_ref[:, hs]
            dqo_ref[:, hs] = ((d * cosv + _rope_swap(d * sinv)) * scale).astype(dqo_ref.dtype)
            dkv_ref[:, hs] = jnp.concatenate([dk[:, :QK_NOPE], dv_ref[:, V_DIM * h:V_DIM * (h + 1)]], axis=1).astype(dkv_ref.dtype)
            ksum = ksum + dk
        ksum = jnp.where((lane >= QK_NOPE) & (lane < QK_NOPE + QK_ROPE), ksum, 0.0)
        un = ksum * cosv + _rope_swap(ksum * sinv)
        dkr_ref[...] = pltpu.roll(un, HEAD_PAD - QK_NOPE, axis=1).astype(dkr_ref.dtype)

    wide = pl.BlockSpec((tm, MLA_HEADS * HEAD_PAD), lambda i: (i, 0))
    half = pl.BlockSpec((tm, MLA_HEADS * V_DIM), lambda i: (i, 0))
    tab = pl.BlockSpec((tm, LANES), lambda i: (i, 0))
    return pl.pallas_call(
        body, grid=(s // tm,), in_specs=[wide, wide, half, tab, tab], out_specs=(wide, wide, tab),
        out_shape=(SDS((s, MLA_HEADS * HEAD_PAD), BF16), SDS((s, MLA_HEADS * HEAD_PAD), BF16), SDS((s, LANES), BF16)),
        name=name, compiler_params=_params(("parallel",), 40 << 20))(dqr, dkr, dv, cos, sins)


FLASH_TILE = 512
FLASH_ROWS = 32


def _flash_fwd(q, k, v, *, name):
    s = q.shape[0]
    t = min(s, FLASH_TILE)
    nq = s // t
    npair = MLA_HEADS // 2

    def body(q_ref, k_ref, v_ref, o_ref, lse_ref):
        i = pl.program_id(1)
        qs = [q_ref[:, HEAD_PAD * e:HEAD_PAD * (e + 1)] for e in range(2)]
        diag = lax.broadcasted_iota(jnp.int32, (t, t), 0) >= lax.broadcasted_iota(jnp.int32, (t, t), 1)

        def step(j, carry, masked):
            rows = pl.ds(pl.multiple_of(j * t, t), t)
            new = []
            for e in range(2):
                m, l, acc = carry[e]
                sc = _dot_nt(qs[e], k_ref[rows, HEAD_PAD * e:HEAD_PAD * (e + 1)])
                if masked:
                    sc = jnp.where(diag, sc, NEG)
                m_new = jnp.maximum(m, jnp.max(sc, axis=1, keepdims=True))
                pr = jnp.exp(sc - m_new)
                alpha = jnp.exp(m - m_new)
                l = alpha * l + jnp.sum(pr, axis=1, keepdims=True)
                acc = alpha * acc + _dot(pr, v_ref[rows, V_DIM * e:V_DIM * (e + 1)])
                new.append((m_new, l, acc))
            return tuple(new)

        init = tuple((jnp.full((t, 1), NEG, F32), jnp.zeros((t, 1), F32), jnp.zeros((t, V_DIM), F32)) for _ in range(2))
        carry = lax.fori_loop(0, i, functools.partial(step, masked=False), init)
        carry = step(i, carry, True)
        o_ref[...] = jnp.concatenate([acc / l for _, l, acc in carry], axis=1)
        lse_ref[0] = jnp.concatenate([jnp.broadcast_to(m + jnp.log(l), (t, V_DIM)) for m, l, _ in carry], axis=1)

    return pl.pallas_call(
        body, grid=(npair, nq),
        in_specs=[pl.BlockSpec((t, 2 * HEAD_PAD), lambda hp, i: (i, hp)), pl.BlockSpec((s, 2 * HEAD_PAD), lambda hp, i: (0, hp)),
                  pl.BlockSpec((s, 2 * V_DIM), lambda hp, i: (0, hp))],
        out_specs=(pl.BlockSpec((t, 2 * V_DIM), lambda hp, i: (i, hp)), pl.BlockSpec((1, t, LANES), lambda hp, i: (hp, i, 0))),
        out_shape=(SDS((s, MLA_HEADS * V_DIM), F32), SDS((npair, s, LANES), F32)), name=name,
        compiler_params=_params(("parallel", "parallel"), 40 << 20))(q, k, v)


def _flash_bwd(q, k, v, o, lse, do, *, name):
    s = q.shape[0]
    t = min(s, FLASH_TILE)
    nq = s // t
    npair = MLA_HEADS // 2
    nchunk = t // FLASH_ROWS

    def valid_cols(r):
        return min(t, -(-((r + 1) * FLASH_ROWS) // LANES) * LANES)

    def body(q_ref, k_ref, v_ref, o_ref, lse_ref, do_ref, dq_ref, dk_ref, dv_ref, s_scr, dp_scr, p_scr, ds_scr, dk_acc, dv_acc):
        j = pl.program_id(1)

        @pl.when(j == 0)
        def _():
            dq_ref[...] = jnp.zeros_like(dq_ref)

        dk_acc[...] = jnp.zeros(dk_acc.shape, F32)
        dv_acc[...] = jnp.zeros(dv_acc.shape, F32)
        qsl = [slice(HEAD_PAD * e, HEAD_PAD * (e + 1)) for e in range(2)]
        vsl = [slice(V_DIM * e, V_DIM * (e + 1)) for e in range(2)]

        def step(i, carry, masked):
            rows = pl.ds(pl.multiple_of(i * t, t), t)
            for e in range(2):
                ke = k_ref[:, qsl[e]]
                qi = q_ref[rows, qsl[e]]
                doi = do_ref[rows, vsl[e]]
                delta = jnp.sum(doi * o_ref[rows, vsl[e]], axis=1, keepdims=True)
                lse_i = lse_ref[0, rows, vsl[e]][:, 0:1]
                dob = doi.astype(BF16)
                s_scr[e] = _dot_nt(qi, ke)
                dp_scr[e] = _dot_nt(dob, v_ref[:, vsl[e]])
                for r in range(nchunk):
                    rs = slice(r * FLASH_ROWS, (r + 1) * FLASH_ROWS)
                    width = valid_cols(r) if masked else t
                    sc = s_scr[e, rs, 0:width]
                    if masked:
                        row = r * FLASH_ROWS + lax.broadcasted_iota(jnp.int32, (FLASH_ROWS, width), 0)
                        sc = jnp.where(row >= lax.broadcasted_iota(jnp.int32, (FLASH_ROWS, width), 1), sc, NEG)
                    pr = jnp.exp(sc - lse_i[rs, :])
                    dsc = pr * (dp_scr[e, rs, 0:width] - delta[rs, :])
                    p_scr[e, rs, 0:width] = pr.astype(BF16)
                    ds_scr[e, rs, 0:width] = dsc.astype(BF16)
                    if width < t:
                        p_scr[e, rs, width:t] = jnp.zeros((FLASH_ROWS, t - width), BF16)
                        ds_scr[e, rs, width:t] = jnp.zeros((FLASH_ROWS, t - width), BF16)
                dv_acc[e] += _dot_tn(p_scr[e], dob)
                dk_acc[e] += _dot_tn(ds_scr[e], qi)
                dq_ref[rows, qsl[e]] += _dot(ds_scr[e], ke)
            return carry

        step(j, 0, True)
        lax.fori_loop(j + 1, nq, functools.partial(step, masked=False), 0)
        dk_ref[...] = jnp.concatenate([dk_acc[e] for e in range(2)], axis=1)
        dv_ref[...] = jnp.concatenate([dv_acc[e] for e in range(2)], axis=1)

    full_q = pl.BlockSpec((s, 2 * HEAD_PAD), lambda hp, j: (0, hp))
    full_v = pl.BlockSpec((s, 2 * V_DIM), lambda hp, j: (0, hp))
    blk_k = pl.BlockSpec((t, 2 * HEAD_PAD), lambda hp, j: (j, hp))
    blk_v = pl.BlockSpec((t, 2 * V_DIM), lambda hp, j: (j, hp))
    return pl.pallas_call(
        body, grid=(npair, nq),
        in_specs=[full_q, blk_k, blk_v, full_v, pl.BlockSpec((1, s, LANES), lambda hp, j: (hp, 0, 0)), full_v],
        out_specs=(full_q, blk_k, blk_v),
        out_shape=(SDS((s, MLA_HEADS * HEAD_PAD), F32), SDS((s, MLA_HEADS * HEAD_PAD), F32), SDS((s, MLA_HEADS * V_DIM), F32)),
        scratch_shapes=[pltpu.VMEM((2, t, t), F32), pltpu.VMEM((2, t, t), F32), pltpu.VMEM((2, t, t), BF16),
                        pltpu.VMEM((2, t, t), BF16), pltpu.VMEM((2, t, HEAD_PAD), F32), pltpu.VMEM((2, t, V_DIM), F32)],
        name=name, compiler_params=_params(("parallel", "arbitrary"), 48 << 20))(q, k, v, o, lse, do)


def _mem_attn_fwd(q, k, v, *, name):
    s = q.shape[0]
    tm = min(s, 512)
    ml = k.shape[0]
    scale = MEM_HEAD_DIM ** -0.5

    def body(q_ref, k_ref, v_ref, o_ref):
        for h in range(MEM_HEADS):
            hs = slice(MEM_HEAD_DIM * h, MEM_HEAD_DIM * (h + 1))
            sc = _dot_nt(q_ref[:, hs], k_ref[:, hs]) * scale
            pr = jnp.exp(sc - jnp.max(sc, axis=1, keepdims=True))
            pr = pr / jnp.sum(pr, axis=1, keepdims=True)
            o_ref[:, hs] = _dot(pr, v_ref[:, hs]).astype(o_ref.dtype)

    blk = pl.BlockSpec((tm, D_MODEL), lambda i: (i, 0))
    kv = pl.BlockSpec((ml, D_MODEL), lambda i: (0, 0))
    return pl.pallas_call(body, grid=(s // tm,), in_specs=[blk, kv, kv], out_specs=blk,
                          out_shape=SDS((s, D_MODEL), BF16), name=name,
                          compiler_params=_params(("parallel",), 24 << 20))(q, k, v)


def _mem_attn_bwd(q, k, v, do, *, name):
    s = q.shape[0]
    tm = min(s, 512)
    ml = k.shape[0]
    scale = MEM_HEAD_DIM ** -0.5

    def body(q_ref, k_ref, v_ref, do_ref, dq_ref, dk_ref, dv_ref):
        @pl.when(pl.program_id(0) == 0)
        def _():
            dk_ref[...] = jnp.zeros_like(dk_ref)
            dv_ref[...] = jnp.zeros_like(dv_ref)

        for h in range(MEM_HEADS):
            hs = slice(MEM_HEAD_DIM * h, MEM_HEAD_DIM * (h + 1))
            qh, kh, vh, doh = q_ref[:, hs], k_ref[:, hs], v_ref[:, hs], do_ref[:, hs]
            sc = _dot_nt(qh, kh) * scale
            pr = jnp.exp(sc - jnp.max(sc, axis=1, keepdims=True))
            pr = pr / jnp.sum(pr, axis=1, keepdims=True)
            dp = _dot_nt(doh, vh)
            dsc = pr * (dp - jnp.sum(pr * dp, axis=1, keepdims=True)) * scale
            dq_ref[:, hs] = _dot(dsc, kh).astype(dq_ref.dtype)
            dk_ref[:, hs] += _dot_tn(dsc, qh)
            dv_ref[:, hs] += _dot_tn(pr, doh)

    blk = pl.BlockSpec((tm, D_MODEL), lambda i: (i, 0))
    kv = pl.BlockSpec((ml, D_MODEL), lambda i: (0, 0))
    return pl.pallas_call(body, grid=(s // tm,), in_specs=[blk, kv, kv, blk], out_specs=(blk, kv, kv),
                          out_shape=(SDS((s, D_MODEL), BF16), SDS((ml, D_MODEL), F32), SDS((ml, D_MODEL), F32)), name=name,
                          compiler_params=_params(("arbitrary",), 32 << 20))(q, k, v, do)


MATS = (("w_in", (1024, 940), 1), ("w_uq", (384, 384), 1), ("w_ukv", (256, 512), 1), ("w_out", (512, 1024), 0),
        ("ssm_conv_w", (4, 512), 1),
        ("w_mq", (256, 1024), 0), ("w_mk", (256, 1024), 0), ("w_mv", (256, 1024), 0), ("w_mo", (256, 1024), 0),
        ("w_up", (1024, 1408), 1), ("w_down", (704, 1024), 0), ("ffn_conv_w", (3, 1408), 1))
GROUPS = {"proj": (0,), "mixer": (1, 2, 3, 4), "mem": (5, 6, 7, 8), "ffn": (9, 10, 11)}
UP_SHARD_COLS = 1408
F32_ON_WIRE = ("ssm_conv_w", "ffn_conv_w")
SMALL = (("norm_mix", 1024), ("ssm_conv_b", 2048), ("dt_bias", 16), ("a_log", 16), ("d_skip", 16), ("ssm_norm", 1024),
         ("q_norm", 384), ("kv_norm", 256), ("attn_out_norm", 1024), ("norm_mem_q", 1024), ("norm_mem_kv", 1024),
         ("norm_ffn", 1024), ("ffn_conv_b", 5632))
PACK_COLS = 1024


def _pad_cols(t, n):
    return jnp.pad(t, ((0, 0),) * (t.ndim - 1) + ((0, n - t.shape[-1]),))


def _w_in_to_padded(t):
    z, xbc, dt, cq, ckv, kr = jnp.split(t, (1024, 3072, 3088, 3472, 3728), axis=-1)
    return jnp.concatenate([xbc, z, cq, _pad_cols(dt, LANES), ckv, _pad_cols(kr, P_IN - P_KR)], axis=-1)


def _w_in_from_padded(t):
    return jnp.concatenate([t[..., P_Z:P_Z + 1024], t[..., P_XBC:P_XBC + 2048], t[..., P_DT:P_DT + SSM_HEADS],
                            t[..., P_CQ:P_CQ + Q_LORA], t[..., P_CKV:P_CKV + KV_LORA], t[..., P_KR:P_KR + QK_ROPE]], axis=-1)


def _cols_joined(g):
    return jnp.concatenate([g[j] for j in range(N_CHIPS)], axis=-1)


def _cols_by_chip(t, dtype):
    k = t.shape[0]
    return t.reshape(k, N_CHIPS, -1).transpose(1, 0, 2).astype(dtype)


def _rows_by_chip(t):
    return t.reshape(N_CHIPS, -1, t.shape[-1])


def _mixer_weights(gw):
    wl = {}
    uq = _cols_joined(gw["w_uq"]).reshape(Q_LORA, MLA_HEADS, QK_NOPE + QK_ROPE)
    wl["w_uq"] = _pad_cols(uq, HEAD_PAD).reshape(Q_LORA, MLA_HEADS * HEAD_PAD)
    wl["w_ukv"] = _cols_joined(gw["w_ukv"])
    wl["ssm_conv_w"] = _cols_joined(gw["ssm_conv_w"])
    return wl


def _layer_fwd(x0, mem, cos, sins, weights, sp, li):
    n = lambda t: f"l{li}_{t}"
    lead = ()
    sv = {"x0": x0}
    gw = dict(weights("proj", x0))
    w_in = _w_in_to_padded(_cols_joined(gw["w_in"]))
    h = _rms_fwd(x0, sp["norm_mix"], name=n("mix_norm"))
    in_hbm = lambda t: pltpu.with_memory_space_constraint(t, pltpu.HBM)
    proj = in_hbm(_mm(h, w_in, name=n("mix_proj")))
    gw.update(weights("mixer", proj))
    wl = dict(_mixer_weights(gw), w_in=w_in)
    xbc = in_hbm(_ssm_conv_fwd(proj, wl["ssm_conv_w"], sp["ssm_conv_b"], name=n("ssm_conv")))
    y, pstates = _ssd_fwd(xbc, proj, sp["dt_bias"], sp["a_log"], sp["d_skip"], name=n("ssd"))
    y_ssm = _gated_rms_fwd(y, proj, sp["ssm_norm"], name=n("ssm_gate"))
    cqn = _rms_fwd(proj, sp["q_norm"], col=(Q_LORA, P_CQ // Q_LORA), name=n("q_norm"))
    ckvn = _rms_fwd(proj, sp["kv_norm"], col=(KV_LORA, P_CKV // KV_LORA), name=n("kv_norm"))
    q = in_hbm(_mm(cqn, wl["w_uq"], name=n("uq")))
    kv = in_hbm(_mm(ckvn, wl["w_ukv"], name=n("ukv")))
    qr, kr, v = _mla_prep(q, kv, proj, cos, sins, name=n("rope"))
    att, lse = _flash_fwd(qr, kr, v, name=n("flash"))
    y_att = _rms_fwd(att, sp["attn_out_norm"], name=n("att_norm"))
    x1 = _mm(y_ssm, gw["w_out"], b_lead=lead, b_rows=(0, D_SSM), res=x0, name=n("out_a"))
    x1 = _mm(y_att, gw["w_out"], b_lead=lead, b_rows=(D_SSM, D_SSM), res=x1, name=n("out_b"))
    sv.update(h=h, proj=proj, xbc=xbc, y=y, pstates=pstates, y_ssm=y_ssm, cqn=cqn, ckvn=ckvn, qr=qr, kr=kr, v=v,
              att=att, lse=lse, y_att=y_att, x1=x1)
    gw.update(weights("mem", x1))
    hq = _rms_fwd(x1, sp["norm_mem_q"], name=n("memq_norm"))
    hm = _rms_fwd(mem, sp["norm_mem_kv"], name=n("memkv_norm"))
    mq = _mm(hq, gw["w_mq"], b_lead=lead, out_dtype=BF16, name=n("mq"))
    mk = _mm(hm, gw["w_mk"], b_lead=lead, out_dtype=BF16, name=n("mk"))
    mv = _mm(hm, gw["w_mv"], b_lead=lead, out_dtype=BF16, name=n("mv"))
    mo = _mem_attn_fwd(mq, mk, mv, name=n("mem_attn"))
    x2 = _mm(mo, gw["w_mo"], b_lead=lead, res=x1, name=n("mo"))
    sv.update(hq=hq, hm=hm, mq=mq, mk=mk, mv=mv, mo=mo, x2=x2)
    gw.update(weights("ffn", x2))
    gw["w_up"] = in_hbm(gw["w_up"])
    wl["ffn_conv_w"] = _cols_joined(gw["ffn_conv_w"])
    hf = _rms_fwd(x2, sp["norm_ffn"], name=n("ffn_norm"))
    up_g = _mm(hf, gw["w_up"], b_lead=lead, b_chips=(0, 2), name=n("up_g"))
    up_v = _mm(hf, gw["w_up"], b_lead=lead, b_chips=(2, 2), name=n("up_v"))
    act = _ffn_conv_fwd(up_g, up_v, wl["ffn_conv_w"], sp["ffn_conv_b"], name=n("ffn_conv"))
    x3 = _mm(act, gw["w_down"], b_lead=lead, res=x2, name=n("down"))
    sv.update(hf=hf, up_g=up_g, up_v=up_v, act=act)
    return x3, sv, gw, wl


def _layer_bwd(dx3, dx3b, mem, cos, sins, gw, wl, sp, sv, li, emit):
    n = lambda t: f"l{li}_b_{t}"
    lead = ()
    g = {}

    def after(token, v):
        return v if token is None else v + token[0, 0]

    dact = _mm(dx3b, gw["w_down"], tb=True, b_lead=lead, out_dtype=BF16, name=n("down_dx"))
    g["w_down"] = _rows_by_chip(_mm(sv["act"], dx3b, ta=True, out_dtype=BF16, name=n("down_dw")))
    dup_g, dup_v, dcw, g["ffn_conv_b"] = _ffn_conv_bwd(
        sv["up_g"], sv["up_v"], wl["ffn_conv_w"], sp["ffn_conv_b"], dact, name=n("ffn_conv"))
    g["ffn_conv_w"] = _cols_by_chip(dcw, F32)
    nsh = UP_SHARD_COLS
    dhf = None
    for c4 in range(N_CHIPS):
        dhf = _mm(dup_g if c4 < 2 else dup_v, gw["w_up"], tb=True, a_col=(nsh, c4 % 2), b_lead=(c4,), res=dhf,
                  name=n(f"up{c4}_dx"))
    g["w_up"] = jnp.concatenate([_mm(sv["hf"], dup_g, ta=True, o_chips=nsh, out_dtype=BF16, name=n("upg_dw")),
                                 _mm(sv["hf"], dup_v, ta=True, o_chips=nsh, out_dtype=BF16, name=n("upv_dw"))], axis=0)
    dx2, dx2b, g["norm_ffn"] = _rms_bwd(sv["x2"], after(emit("ffn", g), sp["norm_ffn"]), dhf, dx3, name=n("ffn_norm"))
    dmo = _mm(dx2b, gw["w_mo"], tb=True, b_lead=lead, out_dtype=BF16, name=n("mo_dx"))
    g["w_mo"] = _rows_by_chip(_mm(sv["mo"], dx2b, ta=True, out_dtype=BF16, name=n("mo_dw")))
    dmq, dmk, dmv = _mem_attn_bwd(sv["mq"], sv["mk"], sv["mv"], dmo, name=n("mem_attn"))
    dhq = _mm(dmq, gw["w_mq"], tb=True, b_lead=lead, name=n("mq_dx"))
    g["w_mq"] = _rows_by_chip(_mm(sv["hq"], dmq, ta=True, out_dtype=BF16, name=n("mq_dw")))
    dhm = _mm(dmk, gw["w_mk"], tb=True, b_lead=lead, name=n("mk_dx"))
    dhm = _mm(dmv, gw["w_mv"], tb=True, b_lead=lead, res=dhm, name=n("mv_dx"))
    g["w_mk"] = _rows_by_chip(_mm(sv["hm"], dmk, ta=True, out_dtype=BF16, name=n("mk_dw")))
    g["w_mv"] = _rows_by_chip(_mm(sv["hm"], dmv, ta=True, out_dtype=BF16, name=n("mv_dw")))
    dx1, dx1b, g["norm_mem_q"] = _rms_bwd(sv["x1"], after(emit("mem", g), sp["norm_mem_q"]), dhq, dx2, name=n("memq_norm"))
    _, _, g["norm_mem_kv"] = _rms_bwd(mem, sp["norm_mem_kv"], dhm, name=n("memkv_norm"))
    dy_ssm = _mm(dx1b, gw["w_out"], tb=True, b_lead=lead, b_rows=(0, D_SSM), name=n("outa_dx"))
    dy_att = _mm(dx1b, gw["w_out"], tb=True, b_lead=lead, b_rows=(D_SSM, D_SSM), name=n("outb_dx"))
    g["w_out"] = _rows_by_chip(jnp.concatenate([_mm(sv["y_ssm"], dx1b, ta=True, out_dtype=BF16, name=n("outa_dw")),
                                                _mm(sv["y_att"], dx1b, ta=True, out_dtype=BF16, name=n("outb_dw"))], axis=0))
    datt, _, g["attn_out_norm"] = _rms_bwd(sv["att"], sp["attn_out_norm"], dy_att, name=n("att_norm"))
    dqr, dkr, dv = _flash_bwd(sv["qr"], sv["kr"], sv["v"], sv["att"], sv["lse"], datt, name=n("flash"))
    dq, dkv, dkrope = _mla_prep_bwd(dqr, dkr, dv, cos, sins, name=n("rope"))
    duq = _mm(sv["cqn"], dq, ta=True, name=n("uq_dw")).reshape(Q_LORA, MLA_HEADS, HEAD_PAD)[..., :QK_NOPE + QK_ROPE]
    g["w_uq"] = _cols_by_chip(duq.reshape(Q_LORA, -1), BF16)
    dcqn = _mm(dq, wl["w_uq"], tb=True, name=n("uq_dx"))
    g["w_ukv"] = _cols_by_chip(_mm(sv["ckvn"], dkv, ta=True, name=n("ukv_dw")), BF16)
    dckvn = _mm(dkv, wl["w_ukv"], tb=True, name=n("ukv_dx"))
    proj = sv["proj"]
    _, dcq, g["q_norm"] = _rms_bwd(proj, sp["q_norm"], dcqn, col=(Q_LORA, P_CQ // Q_LORA), name=n("q_norm"))
    _, dckv, g["kv_norm"] = _rms_bwd(proj, sp["kv_norm"], dckvn, col=(KV_LORA, P_CKV // KV_LORA), name=n("kv_norm"))
    dy, dz, g["ssm_norm"] = _gated_rms_bwd(sv["y"], proj, sp["ssm_norm"], dy_ssm, name=n("ssm_gate"))
    dxbc, ddt, da_log, dd_skip, ddt_bias = _ssd_bwd(
        sv["xbc"], proj, sp["dt_bias"], sp["a_log"], sp["d_skip"], sv["pstates"], dy, name=n("ssd"))
    g["a_log"], g["d_skip"], g["dt_bias"] = da_log[0, :SSM_HEADS], dd_skip[0, :SSM_HEADS], ddt_bias[0, :SSM_HEADS]
    dxbc_pre, dsw, g["ssm_conv_b"] = _ssm_conv_bwd(proj, wl["ssm_conv_w"], sp["ssm_conv_b"], dxbc, name=n("ssm_conv"))
    g["ssm_conv_w"] = _cols_by_chip(dsw, F32)
    started = emit("mixer", g)
    s = proj.shape[0]
    dproj = jnp.concatenate([dxbc_pre, dz, dcq, ddt, dckv, dkrope,
                             jnp.zeros((s, P_IN - P_KR - LANES), BF16)], axis=1)
    dh = _mm(dproj, wl["w_in"], tb=True, name=n("proj_dx"))
    g["w_in"] = _cols_by_chip(_w_in_from_padded(_mm(sv["h"], dproj, ta=True, name=n("proj_dw"))), BF16)
    dx0, dx0b, g["norm_mix"] = _rms_bwd(sv["x0"], after(started, sp["norm_mix"]), dh, dx1, name=n("mix_norm"))
    return dx0, dx0b, g, emit("proj", g)


def _chip_peers(x, y):
    return [(1 - x, y), (x, 1 - y), (1 - x, 1 - y)]


HBM_SPEC = pl.BlockSpec(memory_space=pltpu.HBM)
SEM_SPEC = pl.BlockSpec(memory_space=pltpu.SEMAPHORE)
ANY_SPEC = pl.BlockSpec(memory_space=pl.ANY)
VMEM_SPEC = pl.BlockSpec(memory_space=pltpu.VMEM)
DATAFLOW = pltpu.SideEffectType.DATAFLOW_SIDE_EFFECTING
TOKEN_SHAPE = (8, LANES)


def _exchange_start(srcs, land_shapes, src_view, dst_view, token, *, name):
    n = len(srcs)

    def body(*refs):
        s, l, tok_in = refs[:n], refs[n:2 * n], refs[2 * n]
        send_sems, recv_sems = refs[2 * n + 1], refs[2 * n + 2]
        tok_out = refs[-1]
        x, y, c = lax.axis_index("x"), lax.axis_index("y"), lax.axis_index("c")
        me = 2 * x + y
        for t in range(n):
            for k, (px, py) in enumerate(_chip_peers(x, y)):
                pltpu.make_async_remote_copy(
                    src_ref=src_view(t, s[t], 2 * px + py), dst_ref=dst_view(t, l[t], me), send_sem=send_sems.at[3 * t + k],
                    recv_sem=recv_sems.at[3 * t + k], device_id=(px, py, c), device_id_type=MESH).start()
            pltpu.make_async_copy(src_view(t, s[t], me), dst_view(t, l[t], me), send_sems.at[3 * n + t]).start()
        tok_out[...] = tok_in[...]

    hbm = lambda t: pltpu.with_memory_space_constraint(t, pltpu.HBM)
    lands = [lax.empty(l.shape, l.dtype) for l in land_shapes]
    outs = pl.pallas_call(
        body, name=name,
        out_shape=(pltpu.SemaphoreType.DMA((4 * n,)), pltpu.SemaphoreType.DMA((3 * n,)),
                   *[pltpu.HBM(l.shape, l.dtype) for l in land_shapes], SDS(TOKEN_SHAPE, F32)),
        in_specs=[HBM_SPEC] * (2 * n) + [VMEM_SPEC], out_specs=(SEM_SPEC, SEM_SPEC, *[HBM_SPEC] * n, VMEM_SPEC),
        input_output_aliases={n + t: 2 + t for t in range(n)},
        compiler_params=pltpu.CompilerParams(has_side_effects=DATAFLOW))(*[hbm(t) for t in srcs], *[hbm(t) for t in lands], token)
    return outs[0], outs[1], list(outs[2:2 + n]), outs[-1]


def _exchange_wait(srcs, lands, send_sems, recv_sems, after, src_view, dst_view, which, *, name):
    n = len(srcs)
    m = len(which)

    def body(*refs):
        s, l = refs[:m], refs[m:2 * m]
        send_ref, recv_ref = refs[2 * m], refs[2 * m + 1]
        x, y, c = lax.axis_index("x"), lax.axis_index("y"), lax.axis_index("c")
        me = 2 * x + y
        for i, t in enumerate(which):
            for k, (px, py) in enumerate(_chip_peers(x, y)):
                chip = 2 * px + py
                cp = pltpu.make_async_remote_copy(
                    src_ref=src_view(t, s[i], chip), dst_ref=dst_view(t, l[i], chip), send_sem=send_ref.at[3 * t + k],
                    recv_sem=recv_ref.at[3 * t + k], device_id=(px, py, c), device_id_type=MESH)
                cp.wait_send()
                cp.wait_recv()
            pltpu.make_async_copy(src_view(t, s[i], me), dst_view(t, l[i], me), send_ref.at[3 * n + t]).wait()

    outs = pl.pallas_call(
        body, name=name, out_shape=[pltpu.HBM(lands[t].shape, lands[t].dtype) for t in which],
        in_specs=[HBM_SPEC] * (2 * m) + [SEM_SPEC, SEM_SPEC, ANY_SPEC], out_specs=[HBM_SPEC] * m,
        input_output_aliases={m + i: i for i in range(m)},
        compiler_params=pltpu.CompilerParams(has_side_effects=DATAFLOW))(
            *[srcs[t] for t in which], *[lands[t] for t in which], send_sems, recv_sems, after)
    return list(outs)


def _gather_layer_start(shards, li, token, tag=""):
    src_view = lambda t, ref, chip: ref.at[li]
    dst_view = lambda t, ref, chip: ref.at[chip]
    send_sems, recv_sems, lands, token = _exchange_start(
        shards, [SDS((N_CHIPS,) + s.shape[1:], s.dtype) for s in shards], src_view, dst_view, token,
        name=f"gather{li}{tag}_start")
    return (shards, lands, send_sems, recv_sems, src_view, dst_view, f"gather{li}{tag}"), token


def _scatter_start(grads, tag, token):
    view = lambda t, ref, chip: ref.at[chip]
    send_sems, recv_sems, lands, token = _exchange_start(
        grads, [SDS(g.shape, g.dtype) for g in grads], view, view, token, name=f"scatter{tag}_start")
    return (grads, lands, send_sems, recv_sems, view, view, f"scatter{tag}"), token


def _exchange_finish(handle, after, which=None, tag=""):
    srcs, lands, send_sems, recv_sems, src_view, dst_view, name = handle
    which = tuple(range(len(srcs))) if which is None else which
    return _exchange_wait(srcs, lands, send_sems, recv_sems, after, src_view, dst_view, which, name=f"{name}{tag}_wait")


def _swap_start(bufs, token, *, name):
    n = len(bufs)

    def body(*refs):
        s, l, tok_in = refs[:n], refs[n:2 * n], refs[2 * n]
        send_sems, recv_sems = refs[2 * n + 1], refs[2 * n + 2]
        x, y, c = lax.axis_index("x"), lax.axis_index("y"), lax.axis_index("c")
        for t in range(n):
            pltpu.make_async_remote_copy(src_ref=s[t], dst_ref=l[t], send_sem=send_sems.at[t], recv_sem=recv_sems.at[t],
                                         device_id=(x, y, 1 - c), device_id_type=MESH).start()
        refs[-1][...] = tok_in[...]

    hbm = lambda t: pltpu.with_memory_space_constraint(t, pltpu.HBM)
    lands = [lax.empty(b.shape, b.dtype) for b in bufs]
    outs = pl.pallas_call(
        body, name=f"{name}_start",
        out_shape=(pltpu.SemaphoreType.DMA((n,)), pltpu.SemaphoreType.DMA((n,)),
                   *[pltpu.HBM(b.shape, b.dtype) for b in bufs], SDS(TOKEN_SHAPE, F32)),
        in_specs=[HBM_SPEC] * (2 * n) + [VMEM_SPEC], out_specs=(SEM_SPEC, SEM_SPEC, *[HBM_SPEC] * n, VMEM_SPEC),
        input_output_aliases={n + t: 2 + t for t in range(n)},
        compiler_params=pltpu.CompilerParams(has_side_effects=DATAFLOW))(*[hbm(t) for t in bufs], *[hbm(t) for t in lands], token)
    return (bufs, list(outs[2:2 + n]), outs[0], outs[1], name), outs[-1]


def _swap_wait(handle, after):
    bufs, lands, send_sems, recv_sems, name = handle
    n = len(bufs)

    def body(*refs):
        s, l = refs[:n], refs[n:2 * n]
        send_ref, recv_ref = refs[2 * n], refs[2 * n + 1]
        x, y, c = lax.axis_index("x"), lax.axis_index("y"), lax.axis_index("c")
        for t in range(n):
            cp = pltpu.make_async_remote_copy(src_ref=s[t], dst_ref=l[t], send_sem=send_ref.at[t], recv_sem=recv_ref.at[t],
                                              device_id=(x, y, 1 - c), device_id_type=MESH)
            cp.wait_send()
            cp.wait_recv()

    outs = pl.pallas_call(
        body, name=f"{name}_wait", out_shape=[pltpu.HBM(b.shape, b.dtype) for b in bufs],
        in_specs=[HBM_SPEC] * (2 * n) + [SEM_SPEC, SEM_SPEC, ANY_SPEC], out_specs=[HBM_SPEC] * n,
        input_output_aliases={n + t: t for t in range(n)},
        compiler_params=pltpu.CompilerParams(has_side_effects=DATAFLOW))(*bufs, *lands, send_sems, recv_sems, after)
    return list(outs)


def _all_gather8(src, *, name):
    def body(src_ref, out_ref, send_sems, recv_sems, local_sem):
        x, y, c = lax.axis_index("x"), lax.axis_index("y"), lax.axis_index("c")
        me = 4 * x + 2 * y + c
        mine = pltpu.make_async_copy(src_ref, out_ref.at[me], local_sem)
        mine.start()

        def peer(k):
            return (x ^ (k >> 2 & 1), y ^ (k >> 1 & 1), c ^ (k & 1))

        sends = []
        for k in range(1, N_DEV):
            cp = pltpu.make_async_remote_copy(src_ref=src_ref, dst_ref=out_ref.at[me], send_sem=send_sems.at[k - 1],
                                              recv_sem=recv_sems.at[k - 1], device_id=peer(k), device_id_type=MESH)
            cp.start()
            sends.append(cp)
        for k in range(1, N_DEV):
            px, py, pc = peer(k)
            pltpu.make_async_remote_copy(src_ref=src_ref, dst_ref=out_ref.at[4 * px + 2 * py + pc],
                                         send_sem=send_sems.at[k - 1], recv_sem=recv_sems.at[k - 1],
                                         device_id=peer(k), device_id_type=MESH).wait_recv()
        for cp in sends:
            cp.wait_send()
        mine.wait()

    any_spec = pl.BlockSpec(memory_space=pl.ANY)
    return pl.pallas_call(
        body, in_specs=[any_spec], out_specs=any_spec, out_shape=SDS((N_DEV,) + src.shape, src.dtype),
        scratch_shapes=[pltpu.SemaphoreType.DMA((N_DEV - 1,)), pltpu.SemaphoreType.DMA((N_DEV - 1,)), pltpu.SemaphoreType.DMA],
        name=name)(src)


def _adam_terms(w, g, m, v):
    m = ADAM_B1 * m + (1.0 - ADAM_B1) * g
    v = ADAM_B2 * v + (1.0 - ADAM_B2) * (g * g)
    m_hat = m / (1.0 - ADAM_B1 ** ADAM_STEP)
    v_hat = v / (1.0 - ADAM_B2 ** ADAM_STEP)
    delta = -ADAM_LR * (m_hat / (jnp.sqrt(v_hat) + ADAM_EPS) + ADAM_WD * w)
    return delta, m, v


def _adamw_shard(mine, other, w, m, v, *, name):
    d, a, b = w.shape
    tr = next((t for t in (128, 64, 32, 16) if a % t == 0), a)

    def body(*refs):
        ga, gb = refs[:d], refs[d:2 * d]
        w_ref, m_ref, v_ref, g_ref, d_ref, nm_ref, nv_ref = refs[2 * d:]

        def plane(ref):
            return ((ref[0].astype(F32) + ref[1].astype(F32)) + ref[2].astype(F32)) + ref[3].astype(F32)

        for lp in range(d):
            @pl.when(pl.program_id(0) == lp)
            def _(lp=lp):
                g = plane(ga[lp]) + plane(gb[lp])
                delta, mn, vn = _adam_terms(w_ref[...], g, m_ref[...], v_ref[...])
                g_ref[...] = g
                d_ref[...] = delta
                nm_ref[...] = mn
                nv_ref[...] = vn

    gspecs = [pl.BlockSpec((N_CHIPS, tr, b), lambda l, i, lp=lp: (0, jnp.where(l == lp, i, 0), 0)) for lp in range(d)]
    blk = pl.BlockSpec((None, tr, b), lambda l, i: (l, i, 0))
    shp = SDS((d, a, b), F32)
    return pl.pallas_call(
        body, grid=(d, a // tr), in_specs=gspecs + gspecs + [blk, blk, blk], out_specs=(blk,) * 4, out_shape=(shp,) * 4,
        name=name, compiler_params=_params(("arbitrary", "arbitrary"), 48 << 20))(*mine, *other, w, m, v)


def _adamw_small(g8, w, m, v, *, name):
    n = w.shape[1]

    def body(g8_ref, w_ref, m_ref, v_ref, g_ref, d_ref, nm_ref, nv_ref):
        g = g8_ref[0]
        for k in range(1, N_DEV):
            g = g + g8_ref[k]
        delta, mn, vn = _adam_terms(w_ref[...], g, m_ref[...], v_ref[...])
        g_ref[...] = g
        d_ref[...] = delta
        nm_ref[...] = mn
        nv_ref[...] = vn

    shp = SDS((1, n), F32)
    return pl.pallas_call(body, out_shape=(shp,) * 4, name=name, compiler_params=_params(None, 24 << 20))(g8, w, m, v)


def _rope_tables(positions):
    inv_freq = 1.0 / (ROPE_THETA ** (jnp.arange(0, QK_ROPE, 2, dtype=F32) / QK_ROPE))
    ang = positions.astype(F32)[:, None] * inv_freq
    c, s = jnp.cos(ang), jnp.sin(ang)
    n = positions.shape[0]
    pad = jnp.zeros((n, HEAD_PAD - QK_NOPE - QK_ROPE), F32)
    cos = jnp.concatenate([jnp.ones((n, QK_NOPE), F32), c, c, pad], axis=1)
    sins = jnp.concatenate([jnp.zeros((n, QK_NOPE), F32), -s, s, pad], axis=1)
    return cos, sins


def _pad_lanes(v):
    return _pad_cols(v.reshape(1, -1), LANES)


def _local_step(x, mem, positions, weights, small, final_norm, loss_target, emit, token):
    cos, sins = _rope_tables(positions)
    saved, gws, wls, sps = [], [], [], []
    h = x
    for li in range(DEPTH):
        sp = {k: small[k][li] for k, _ in SMALL}
        if li == 0:
            sp["norm_mix"] = sp["norm_mix"] + token[0, 0]
        for k in ("dt_bias", "a_log", "d_skip"):
            sp[k] = _pad_lanes(sp[k])
        h, sv, gw, wl = _layer_fwd(h, mem, cos, sins, functools.partial(weights, li), sp, li)
        saved.append(sv)
        gws.append(gw)
        wls.append(wl)
        sps.append(sp)
    loss, dh, dhb, g_final = _final_loss(h, final_norm, loss_target, name="final_loss")
    grads = [None] * DEPTH
    started = None
    for li in reversed(range(DEPTH)):
        sp = sps[li]
        if started is not None:
            sp = dict(sp, ffn_conv_b=sp["ffn_conv_b"] + started[0, 0])
        dh, dhb, grads[li], started = _layer_bwd(dh, dhb, mem, cos, sins, gws[li], wls[li], sp, saved[li], li,
                                                 functools.partial(emit, li))
    return loss, dh, grads, g_final


def _gathered_views(which, lands):
    return {MATS[t][0]: (b.reshape(-1, b.shape[-1]) if MATS[t][2] == 0 else b) for t, b in zip(which, lands)}


def kernel(x, mem, positions, norm_mix, w_in, ssm_conv_w, ssm_conv_b, dt_bias, a_log, d_skip, ssm_norm, q_norm, w_uq, kv_norm, w_ukv, attn_out_norm, w_out, norm_mem_q, norm_mem_kv, w_mq, w_mk, w_mv, w_mo, norm_ffn, w_up, ffn_conv_w, ffn_conv_b, w_down, final_norm, loss_target, m_norm_mix, m_w_in, m_ssm_conv_w, m_ssm_conv_b, m_dt_bias, m_a_log, m_d_skip, m_ssm_norm, m_q_norm, m_w_uq, m_kv_norm, m_w_ukv, m_attn_out_norm, m_w_out, m_norm_mem_q, m_norm_mem_kv, m_w_mq, m_w_mk, m_w_mv, m_w_mo, m_norm_ffn, m_w_up, m_ffn_conv_w, m_ffn_conv_b, m_w_down, m_final_norm, v_norm_mix, v_w_in, v_ssm_conv_w, v_ssm_conv_b, v_dt_bias, v_a_log, v_d_skip, v_ssm_norm, v_q_norm, v_w_uq, v_kv_norm, v_w_ukv, v_attn_out_norm, v_w_out, v_norm_mem_q, v_norm_mem_kv, v_w_mq, v_w_mk, v_w_mv, v_w_mo, v_norm_ffn, v_w_up, v_ffn_conv_w, v_ffn_conv_b, v_w_down, v_final_norm):
    args = dict(locals())
    names = ["norm_mix", "w_in", "ssm_conv_w", "ssm_conv_b", "dt_bias", "a_log", "d_skip", "ssm_norm", "q_norm", "w_uq",
             "kv_norm", "w_ukv", "attn_out_norm", "w_out", "norm_mem_q", "norm_mem_kv", "w_mq", "w_mk", "w_mv", "w_mo",
             "norm_ffn", "w_up", "ffn_conv_w", "ffn_conv_b", "w_down", "final_norm"]
    wts = {k: args[k] for k in names}
    mom = {k: args["m_" + k] for k in names}
    var = {k: args["v_" + k] for k in names}
    mat_names = [k for k, _, _ in MATS]

    shards = [wts[k] if k in F32_ON_WIRE else wts[k].astype(BF16) for k in mat_names]
    token = jnp.zeros(TOKEN_SHAPE, F32)
    first, token = _gather_layer_start(shards[:1], 0, token, tag="_first")
    gathers = []
    for li in range(DEPTH):
        handle, token = _gather_layer_start(shards[1:] if li == 0 else shards, li, token)
        gathers.append(handle)
    small = {k: wts[k] for k, _ in SMALL}

    def weights(li, group, after):
        which = GROUPS[group]
        if li > 0:
            return _gathered_views(which, _exchange_finish(gathers[li], after, which, tag=f"_{group}"))
        if group == "proj":
            return _gathered_views(which, _exchange_finish(first, after))
        return _gathered_views(which, _exchange_finish(gathers[0], after, tuple(t - 1 for t in which), tag=f"_{group}"))

    scatters = [[] for _ in range(DEPTH)]
    nm = len(mat_names)
    mine = [[None] * nm for _ in range(DEPTH)]
    swaps = [None] * DEPTH

    def swap_layer(li, after):
        for which, handle in scatters[li]:
            for t, b in zip(which, _exchange_finish(handle, after)):
                mine[li][t] = b
        swaps[li], started = _swap_start(mine[li], jnp.zeros(TOKEN_SHAPE, F32), name=f"swap{li}")
        return started

    def emit(li, group, g):
        last = group == "proj"
        if li == 0:
            which = GROUPS[group]
        elif last:
            which = tuple(range(nm))
        else:
            return None
        handle, started = _scatter_start([g[MATS[t][0]] for t in which], f"{li}_{group}", jnp.zeros(TOKEN_SHAPE, F32))
        scatters[li].append((which, handle))
        if li + 1 < DEPTH and group == ("mixer" if li == 0 else "proj"):
            started = started + swap_layer(li + 1, g["ssm_conv_w"])
        return started

    loss, grad_x, grads, g_final = _local_step(x[0], mem[0], positions[0], weights, small, wts["final_norm"],
                                               loss_target[0], emit, token)
    loss = lax.psum(loss, ("x", "y", "c"))

    w_in_t = GROUPS["proj"][0]
    others = [t for t in range(nm) if t != w_in_t]
    for which, handle in scatters[0]:
        if which != GROUPS["proj"]:
            for t, b in zip(which, _exchange_finish(handle, grad_x)):
                mine[0][t] = b
    swaps[0], _ = _swap_start([mine[0][t] for t in others], jnp.zeros(TOKEN_SHAPE, F32), name="swap0")
    other = [dict(zip(others, _swap_wait(swaps[0], grad_x)))] + [
        dict(enumerate(_swap_wait(swaps[li], grad_x))) for li in range(1, DEPTH)]

    def adamw(t):
        k = mat_names[t]
        return _adamw_shard([mine[li][t] for li in range(DEPTH)], [other[li][t] for li in range(DEPTH)],
                            wts[k], mom[k], var[k], name=f"adamw_{k}")

    mat_out = {mat_names[t]: adamw(t) for t in others}
    done = mat_out[mat_names[others[-1]]][0]
    (last_handle,) = [handle for which, handle in scatters[0] if which == GROUPS["proj"]]
    (mine[0][w_in_t],) = _exchange_finish(last_handle, done)
    last_swap, _ = _swap_start([mine[0][w_in_t]], jnp.zeros(TOKEN_SHAPE, F32), name="swap0_last")
    (other[0][w_in_t],) = _swap_wait(last_swap, done)
    mat_out[mat_names[w_in_t]] = adamw(w_in_t)

    def pack_small(get, fin):
        flat = [get(k).reshape(-1) for k, _ in SMALL] + [fin.reshape(-1)]
        n = sum(f.shape[0] for f in flat)
        return jnp.concatenate(flat + [jnp.zeros((-n % PACK_COLS,), F32)]).reshape(1, -1)

    gs = pack_small(lambda k: jnp.stack([grads[li][k] for li in range(DEPTH)]), g_final)
    g8 = _all_gather8(gs, name="gather_small_grads")
    small_out = _adamw_small(g8, pack_small(wts.get, wts["final_norm"]), pack_small(mom.get, mom["final_norm"]),
                             pack_small(var.get, var["final_norm"]), name="adamw_small")

    def unpack_small(buf):
        out, off = {}, 0
        for k, nel in SMALL:
            out[k] = buf[0, off:off + DEPTH * nel].reshape(DEPTH, nel)
            off += DEPTH * nel
        out["final_norm"] = buf[0, off:off + D_MODEL]
        return out

    small_res = [unpack_small(b) for b in small_out]
    res = []
    for kind in range(4):
        for k in names:
            res.append(small_res[kind][k] if k in small_res[kind] else mat_out[k][kind])
    return (loss, grad_x[None], *res)
```

```python
import functools
import math

import jax
import jax.numpy as jnp
from jax import lax
from jax.experimental import pallas as pl
from jax.experimental.pallas import tpu as pltpu

F32 = jnp.float32
BF16 = jnp.bfloat16
HIGHEST = lax.Precision.HIGHEST
SDS = jax.ShapeDtypeStruct
MESH = pl.DeviceIdType.MESH

D_MODEL = 1024
DEPTH = 4
EPS = 1e-6
SSM_HEADS = 16
SSM_HEAD_DIM = 64
D_SSM = 1024
SSM_GROUPS = 4
SSM_STATE = 128
SSM_CONV = 4
SSM_CHUNK = 128
CONV_CH = 2048
MLA_HEADS = 16
QK_NOPE = 64
QK_ROPE = 32
V_DIM = 64
Q_LORA = 384
KV_LORA = 256
ROPE_THETA = 10000.0
MEM_HEADS = 4
MEM_HEAD_DIM = 256
D_FF = 2816
FFN_CONV = 3
D_IN = 3760
ADAM_LR = 0.001
ADAM_B1 = 0.9
ADAM_B2 = 0.999
ADAM_EPS = 1e-08
ADAM_WD = 0.01
ADAM_STEP = 10

LANES = 128
HEAD_PAD = 128
N_CHIPS = 4
N_DEV = 8
VMEM_CAP_MB = 56

P_XBC, P_Z, P_CQ, P_DT, P_CKV, P_KR, P_IN = 0, 2048, 3072, 3456, 3584, 3840, 4096
NEG = -1e30


def _tile(n, pref):
    t = (min(n, pref) // LANES) * LANES
    while t >= LANES:
        if n % t == 0:
            return t
        t -= LANES
    return n


def _params(sem=None, vmem_bytes=None):
    kw = {}
    if sem is not None:
        kw["dimension_semantics"] = sem
    if vmem_bytes is not None:
        kw["vmem_limit_bytes"] = int(min(max(vmem_bytes, 16 << 20), VMEM_CAP_MB << 20))
    return pltpu.CompilerParams(**kw)


def _nbytes(shape, dtype):
    return math.prod(shape) * jnp.dtype(dtype).itemsize


def _mm(a, b, *, ta=False, tb=False, res=None, out_dtype=F32, name, a_col=None, b_lead=(), b_rows=None,
        b_chips=None, o_chips=None):
    if ta:
        k, m = a.shape
    else:
        m, k = (a.shape[0], a.shape[1] if a_col is None else a_col[0])
    rows_b, cols_b = b.shape[-2:]
    row0 = 0
    if b_rows is not None:
        row0, rows_b = b_rows
    nlead = len(b_lead)
    if b_chips is not None:
        assert not tb
        kb, tn, n = rows_b, cols_b, b_chips[1] * cols_b
        b_blk = (None,) * (1 + nlead) + (kb, tn)
        b_map = lambda i, j: (b_chips[0] + j,) + tuple(b_lead) + (0, 0)
    elif tb:
        n, kb = rows_b, cols_b
        tn = _tile(n, 512)
        assert row0 % tn == 0
        b_blk = (None,) * nlead + (tn, kb)
        b_map = lambda i, j: tuple(b_lead) + (j + row0 // tn, 0)
    else:
        kb, n = rows_b, cols_b
        tn = o_chips if o_chips else _tile(n, 512)
        assert row0 % kb == 0
        b_blk = (None,) * nlead + (kb, tn)
        b_map = lambda i, j: tuple(b_lead) + (row0 // kb, j)
    assert k == kb, (a.shape, b.shape, ta, tb, k, kb)
    tm = _tile(m, 512)
    if ta:
        a_blk, a_map = (k, tm), (lambda i, j: (0, i))
    else:
        a_blk, a_map = (tm, k), ((lambda i, j: (i, 0)) if a_col is None else (lambda i, j: (i, a_col[1])))
    if o_chips:
        o_spec = pl.BlockSpec((None, tm, tn), lambda i, j: (j, i, 0))
        o_shape = SDS((n // tn, m, tn), out_dtype)
    else:
        o_spec = pl.BlockSpec((tm, tn), lambda i, j: (i, j))
        o_shape = SDS((m, n), out_dtype)
    dims = (((0 if ta else 1,), (1 if tb else 0,)), ((), ()))
    has_res = res is not None

    def body(*refs):
        a_ref, b_ref = refs[0], refs[1]
        o_ref = refs[-1]
        acc = lax.dot_general(a_ref[...].astype(BF16), b_ref[...].astype(BF16), dims, preferred_element_type=F32)
        if has_res:
            acc = acc + refs[2][...]
        o_ref[...] = acc.astype(o_ref.dtype)

    bb = tuple(d for d in b_blk if d is not None)
    vmem = 2 * (_nbytes(a_blk, a.dtype) + _nbytes(bb, b.dtype) + (2 if has_res else 1) * _nbytes((tm, tn), F32))
    vmem += _nbytes(a_blk, BF16) + _nbytes(bb, BF16) + 2 * _nbytes((tm, tn), F32) + (4 << 20)
    args = (a, b) + ((res,) if has_res else ())
    specs = [pl.BlockSpec(a_blk, a_map), pl.BlockSpec(b_blk, b_map)] + ([o_spec] if has_res else [])
    return pl.pallas_call(body, grid=(m // tm, n // tn), in_specs=specs, out_specs=o_spec, out_shape=o_shape, name=name,
                          compiler_params=_params(("parallel", "parallel"), vmem))(*args)


def _sigmoid(x):
    return 1.0 / (1.0 + jnp.exp(-x))


def _rms_fwd(x, g, *, col=None, name):
    s = x.shape[0]
    w, ci = (x.shape[1], 0) if col is None else col
    tm = min(s, 512)

    def body(x_ref, g_ref, o_ref):
        xv = x_ref[...].astype(F32)
        r = lax.rsqrt(jnp.mean(xv * xv, axis=-1, keepdims=True) + EPS)
        o_ref[...] = (xv * r * g_ref[...]).astype(o_ref.dtype)

    return pl.pallas_call(
        body, grid=(s // tm,),
        in_specs=[pl.BlockSpec((tm, w), lambda i: (i, ci)), pl.BlockSpec((1, w), lambda i: (0, 0))],
        out_specs=pl.BlockSpec((tm, w), lambda i: (i, 0)), out_shape=SDS((s, w), BF16), name=name,
        compiler_params=_params(("parallel",), 10 * tm * w * 4))(x, g.reshape(1, w))


def _rms_bwd(x, g, dy, dres=None, *, col=None, name):
    s = x.shape[0]
    w, ci = (x.shape[1], 0) if col is None else col
    tm = min(s, 512)
    has_res = dres is not None

    def body(*refs):
        x_ref, g_ref, dy_ref = refs[:3]
        dx_ref, dxb_ref, dg_ref = refs[-3:]
        xv = x_ref[...].astype(F32)
        dyv = dy_ref[...].astype(F32)
        r = lax.rsqrt(jnp.mean(xv * xv, axis=-1, keepdims=True) + EPS)
        u = dyv * g_ref[...]
        dx = r * u - xv * (r * r * r) * jnp.mean(xv * u, axis=-1, keepdims=True)
        if has_res:
            dx = dx + refs[3][...]
        dx_ref[...] = dx
        dxb_ref[...] = dx.astype(BF16)

        @pl.when(pl.program_id(0) == 0)
        def _():
            dg_ref[...] = jnp.zeros_like(dg_ref)

        dg_ref[...] += jnp.sum(dyv * xv * r, axis=0, keepdims=True)

    blk = pl.BlockSpec((tm, w), lambda i: (i, 0))
    specs = [pl.BlockSpec((tm, w), lambda i: (i, ci)), pl.BlockSpec((1, w), lambda i: (0, 0)), blk]
    args = [x, g.reshape(1, w), dy]
    if has_res:
        specs.append(blk)
        args.append(dres)
    dx, dxb, dg = pl.pallas_call(
        body, grid=(s // tm,), in_specs=specs,
        out_specs=(blk, blk, pl.BlockSpec((1, w), lambda i: (0, 0))),
        out_shape=(SDS((s, w), F32), SDS((s, w), BF16), SDS((1, w), F32)), name=name,
        compiler_params=_params(("arbitrary",), 18 * tm * w * 4))(*args)
    return dx, dxb, dg.reshape(w)


def _gated_rms_fwd(y, proj, g, *, name):
    s, w = y.shape
    tm = min(s, 512)

    def body(y_ref, z_ref, g_ref, o_ref):
        z = z_ref[...]
        t = y_ref[...] * (z * _sigmoid(z))
        r = lax.rsqrt(jnp.mean(t * t, axis=-1, keepdims=True) + EPS)
        o_ref[...] = (t * r * g_ref[...]).astype(o_ref.dtype)

    blk = pl.BlockSpec((tm, w), lambda i: (i, 0))
    return pl.pallas_call(
        body, grid=(s // tm,),
        in_specs=[blk, pl.BlockSpec((tm, w), lambda i: (i, P_Z // w)), pl.BlockSpec((1, w), lambda i: (0, 0))],
        out_specs=blk, out_shape=SDS((s, w), BF16), name=name,
        compiler_params=_params(("parallel",), 14 * tm * w * 4))(y, proj, g.reshape(1, w))


def _gated_rms_bwd(y, proj, g, dout, *, name):
    s, w = y.shape
    tm = min(s, 512)

    def body(y_ref, z_ref, g_ref, do_ref, dy_ref, dz_ref, dg_ref):
        z = z_ref[...]
        yv = y_ref[...]
        dov = do_ref[...]
        sg = _sigmoid(z)
        sz = z * sg
        t = yv * sz
        r = lax.rsqrt(jnp.mean(t * t, axis=-1, keepdims=True) + EPS)
        u = dov * g_ref[...]
        dt = r * u - t * (r * r * r) * jnp.mean(t * u, axis=-1, keepdims=True)
        dy_ref[...] = dt * sz
        dz_ref[...] = (dt * yv * (sg * (1.0 + z * (1.0 - sg)))).astype(dz_ref.dtype)

        @pl.when(pl.program_id(0) == 0)
        def _():
            dg_ref[...] = jnp.zeros_like(dg_ref)

        dg_ref[...] += jnp.sum(dov * t * r, axis=0, keepdims=True)

    blk = pl.BlockSpec((tm, w), lambda i: (i, 0))
    vec = pl.BlockSpec((1, w), lambda i: (0, 0))
    dy, dz, dg = pl.pallas_call(
        body, grid=(s // tm,),
        in_specs=[blk, pl.BlockSpec((tm, w), lambda i: (i, P_Z // w)), vec, blk],
        out_specs=(blk, blk, vec), out_shape=(SDS((s, w), F32), SDS((s, w), BF16), SDS((1, w), F32)), name=name,
        compiler_params=_params(("arbitrary",), 24 * tm * w * 4))(y, proj, g.reshape(1, w), dout)
    return dy, dz, dg.reshape(w)


def _final_loss(x, g, target, *, name):
    s, w = x.shape
    tm = min(s, 512)

    def body(x_ref, g_ref, t_ref, loss_ref, dx_ref, dxb_ref, dg_ref):
        xv = x_ref[...]
        gv = g_ref[...]
        r = lax.rsqrt(jnp.mean(xv * xv, axis=-1, keepdims=True) + EPS)
        xn = xv * r
        diff = xn * gv - t_ref[...]
        dy = diff * (1.0 / w)
        u = dy * gv
        dx = r * u - xv * (r * r * r) * jnp.mean(xv * u, axis=-1, keepdims=True)
        dx_ref[...] = dx
        dxb_ref[...] = dx.astype(BF16)

        @pl.when(pl.program_id(0) == 0)
        def _():
            dg_ref[...] = jnp.zeros_like(dg_ref)
            loss_ref[...] = jnp.zeros_like(loss_ref)

        dg_ref[...] += jnp.sum(dy * xn, axis=0, keepdims=True)
        part = jnp.sum(jnp.sum(diff * diff, axis=1, keepdims=True), axis=0, keepdims=True) * (0.5 / w)
        loss_ref[...] += jnp.broadcast_to(part, loss_ref.shape)

    blk = pl.BlockSpec((tm, w), lambda i: (i, 0))
    vec = pl.BlockSpec((1, w), lambda i: (0, 0))
    loss, dx, dxb, dg = pl.pallas_call(
        body, grid=(s // tm,), in_specs=[blk, vec, blk],
        out_specs=(pl.BlockSpec((1, LANES), lambda i: (0, 0)), blk, blk, vec),
        out_shape=(SDS((1, LANES), F32), SDS((s, w), F32), SDS((s, w), BF16), SDS((1, w), F32)), name=name,
        compiler_params=_params(("arbitrary",), 18 * tm * w * 4))(x, g.reshape(1, w), target)
    return loss[0, 0], dx, dxb, dg.reshape(w)


def _shift_down(x, k):
    if k == 0:
        return x
    row = lax.broadcasted_iota(jnp.int32, x.shape, 0)
    return jnp.where(row < k, 0.0, pltpu.roll(x, k, axis=0))


def _shift_up(x, k):
    if k == 0:
        return x
    s = x.shape[0]
    row = lax.broadcasted_iota(jnp.int32, x.shape, 0)
    return jnp.where(row >= s - k, 0.0, pltpu.roll(x, s - k, axis=0))


def _conv_pre(x, w, b, kw):
    pre = b
    for j in range(kw):
        pre = pre + w[j:j + 1, :] * _shift_down(x, kw - 1 - j)
    return pre


def _conv_bwd_terms(x, w, dpre, kw):
    dx = jnp.zeros_like(x)
    dws = []
    for j in range(kw):
        dx = dx + w[j:j + 1, :] * _shift_up(dpre, kw - 1 - j)
        dws.append(jnp.sum(dpre * _shift_down(x, kw - 1 - j), axis=0, keepdims=True))
    return dx, jnp.concatenate(dws, axis=0), jnp.sum(dpre, axis=0, keepdims=True)


def _ssm_conv_fwd(proj, w, b, *, name):
    s = proj.shape[0]
    cw = 256

    def body(x_ref, w_ref, b_ref, o_ref):
        pre = _conv_pre(x_ref[...], w_ref[...], b_ref[...], SSM_CONV)
        o_ref[...] = pre * _sigmoid(pre)

    return pl.pallas_call(
        body, grid=(CONV_CH // cw,),
        in_specs=[pl.BlockSpec((s, cw), lambda j: (0, j)), pl.BlockSpec((SSM_CONV, cw), lambda j: (0, j)),
                  pl.BlockSpec((1, cw), lambda j: (0, j))],
        out_specs=pl.BlockSpec((s, cw), lambda j: (0, j)), out_shape=SDS((s, CONV_CH), F32), name=name,
        compiler_params=_params(("parallel",), 12 * s * cw * 4))(proj, w, b.reshape(1, CONV_CH))


def _ssm_conv_bwd(proj, w, b, dxbc, *, name):
    s = proj.shape[0]
    cw = 256

    def body(x_ref, w_ref, b_ref, dy_ref, dx_ref, dw_ref, db_ref):
        x = x_ref[...]
        wv = w_ref[...]
        pre = _conv_pre(x, wv, b_ref[...], SSM_CONV)
        sg = _sigmoid(pre)
        dpre = dy_ref[...] * (sg * (1.0 + pre * (1.0 - sg)))
        dx, dw, db = _conv_bwd_terms(x, wv, dpre, SSM_CONV)
        dx_ref[...] = dx.astype(dx_ref.dtype)
        dw_ref[...] = dw
        db_ref[...] = db

    col = pl.BlockSpec((s, cw), lambda j: (0, j))
    wsp = pl.BlockSpec((SSM_CONV, cw), lambda j: (0, j))
    bsp = pl.BlockSpec((1, cw), lambda j: (0, j))
    dx, dw, db = pl.pallas_call(
        body, grid=(CONV_CH // cw,), in_specs=[col, wsp, bsp, col], out_specs=(col, wsp, bsp),
        out_shape=(SDS((s, CONV_CH), BF16), SDS((SSM_CONV, CONV_CH), F32), SDS((1, CONV_CH), F32)), name=name,
        compiler_params=_params(("parallel",), 20 * s * cw * 4))(proj, w, b.reshape(1, CONV_CH), dxbc)
    return dx, dw, db.reshape(CONV_CH)


def _ffn_conv_fwd(up_g, up_v, w, b, *, name):
    s = up_g.shape[0]
    cw = 256
    nb = D_FF // cw

    def body(g_ref, v_ref, wg_ref, wv_ref, bg_ref, bv_ref, o_ref):
        gate = _conv_pre(g_ref[...], wg_ref[...], bg_ref[...], FFN_CONV)
        val = _conv_pre(v_ref[...], wv_ref[...], bv_ref[...], FFN_CONV)
        o_ref[...] = (gate * _sigmoid(gate) * val).astype(o_ref.dtype)

    col = pl.BlockSpec((s, cw), lambda j: (0, j))
    b2 = b.reshape(1, 2 * D_FF)
    return pl.pallas_call(
        body, grid=(nb,),
        in_specs=[col, col, pl.BlockSpec((FFN_CONV, cw), lambda j: (0, j)), pl.BlockSpec((FFN_CONV, cw), lambda j: (0, j + nb)),
                  pl.BlockSpec((1, cw), lambda j: (0, j)), pl.BlockSpec((1, cw), lambda j: (0, j + nb))],
        out_specs=col, out_shape=SDS((s, D_FF), BF16), name=name,
        compiler_params=_params(("parallel",), 16 * s * cw * 4))(up_g, up_v, w, w, b2, b2)


def _ffn_conv_bwd(up_g, up_v, w, b, dact, *, name):
    s = up_g.shape[0]
    cw = 256
    nb = D_FF // cw

    def body(g_ref, v_ref, wg_ref, wv_ref, bg_ref, bv_ref, da_ref, dg_ref, dv_ref, dwg_ref, dwv_ref, dbg_ref, dbv_ref):
        xg, xv = g_ref[...], v_ref[...]
        wg, wv = wg_ref[...], wv_ref[...]
        gate = _conv_pre(xg, wg, bg_ref[...], FFN_CONV)
        val = _conv_pre(xv, wv, bv_ref[...], FFN_CONV)
        da = da_ref[...].astype(F32)
        sg = _sigmoid(gate)
        dgate = da * val * (sg * (1.0 + gate * (1.0 - sg)))
        dval = da * gate * sg
        dxg, dwg, dbg = _conv_bwd_terms(xg, wg, dgate, FFN_CONV)
        dxv, dwv, dbv = _conv_bwd_terms(xv, wv, dval, FFN_CONV)
        dg_ref[...] = dxg.astype(dg_ref.dtype)
        dv_ref[...] = dxv.astype(dv_ref.dtype)
        dwg_ref[...] = dwg
        dwv_ref[...] = dwv
        dbg_ref[...] = dbg
        dbv_ref[...] = dbv

    col = pl.BlockSpec((s, cw), lambda j: (0, j))
    wsp = pl.BlockSpec((FFN_CONV, cw), lambda j: (0, j))
    bsp = pl.BlockSpec((1, cw), lambda j: (0, j))
    b2 = b.reshape(1, 2 * D_FF)
    dg, dv, dwg, dwv, dbg, dbv = pl.pallas_call(
        body, grid=(nb,),
        in_specs=[col, col, wsp, pl.BlockSpec((FFN_CONV, cw), lambda j: (0, j + nb)), bsp,
                  pl.BlockSpec((1, cw), lambda j: (0, j + nb)), col],
        out_specs=(col, col, wsp, wsp, bsp, bsp),
        out_shape=(SDS((s, D_FF), BF16), SDS((s, D_FF), BF16), SDS((FFN_CONV, D_FF), F32), SDS((FFN_CONV, D_FF), F32),
                   SDS((1, D_FF), F32), SDS((1, D_FF), F32)), name=name,
        compiler_params=_params(("parallel",), 32 * s * cw * 4))(up_g, up_v, w, w, b2, b2, dact)
    return dg, dv, jnp.concatenate([dwg, dwv], axis=1), jnp.concatenate([dbg, dbv], axis=1).reshape(2 * D_FF)


def _dot(a, b):
    return jnp.dot(a.astype(BF16), b.astype(BF16), preferred_element_type=F32)


def _dot_nt(a, b):
    return lax.dot_general(a.astype(BF16), b.astype(BF16), (((1,), (1,)), ((), ())), preferred_element_type=F32)


def _dot_tn(a, b):
    return lax.dot_general(a.astype(BF16), b.astype(BF16), (((0,), (0,)), ((), ())), preferred_element_type=F32)


def _ssd_chunk_terms(dtraw, bias, a_log):
    ell = dtraw.shape[0]
    lane = lax.broadcasted_iota(jnp.int32, dtraw.shape, 1)
    valid = lane < SSM_HEADS
    pre = dtraw + bias
    dt = jnp.where(valid, jnp.where(pre > 20.0, pre, jnp.log(1.0 + jnp.exp(jnp.minimum(pre, 20.0)))), 0.0)
    a = -jnp.exp(a_log)
    ad = dt * a
    row = lax.broadcasted_iota(jnp.int32, (ell, ell), 0)
    colm = lax.broadcasted_iota(jnp.int32, (ell, ell), 1)
    tril = row >= colm
    cs = jnp.dot(tril.astype(F32), ad, precision=HIGHEST, preferred_element_type=F32)
    cs_last = cs[ell - 1:ell, :]
    return pre, dt, a, cs, cs_last, tril


def _head_expand():
    h = lax.broadcasted_iota(jnp.int32, (LANES, D_SSM), 0)
    c = lax.broadcasted_iota(jnp.int32, (LANES, D_SSM), 1)
    return (c // SSM_HEAD_DIM == h).astype(F32)


def _ssd_fwd(xbc, proj, dt_bias, a_log, d_skip, *, name):
    s = xbc.shape[0]
    nc = s // SSM_CHUNK
    ell, n, p = SSM_CHUNK, SSM_STATE, SSM_HEAD_DIM
    rpg = SSM_HEADS // SSM_GROUPS
    gw = rpg * p

    def body(x_ref, dt_ref, bias_ref, alog_ref, dskip_ref, ex_ref, y_ref, ps_ref, state):
        @pl.when(pl.program_id(0) == 0)
        def _():
            state[...] = jnp.zeros_like(state)

        _, dt, _, cs, cs_last, tril = _ssd_chunk_terms(dt_ref[...], bias_ref[...], alog_ref[...])
        cst = cs.T
        ex = ex_ref[...]
        spread = lambda v: jnp.dot(v, ex, precision=HIGHEST, preferred_element_type=F32)
        dt_x, e_x, ds_x = spread(dt), spread(jnp.exp(cs)), spread(jnp.exp(cs_last - cs))
        cd_x = spread(jnp.broadcast_to(jnp.exp(cs_last), (8, LANES)))[0:1, :]
        dskip_x = spread(jnp.broadcast_to(dskip_ref[...], (8, LANES)))[0:1, :]
        st = state[...]
        ps_ref[0] = st
        xv = x_ref[...]
        xs_all = xv[:, 0:D_SSM]
        xd_all = xs_all * dt_x
        xdd_all = xd_all * ds_x
        lane_g = lax.broadcasted_iota(jnp.int32, (ell, gw), 1)
        ys, new = [], []
        for g in range(SSM_GROUPS):
            gs = slice(gw * g, gw * (g + 1))
            bg = xv[:, D_SSM + n * g:D_SSM + n * (g + 1)]
            cg = xv[:, D_SSM + n * (SSM_GROUPS + g):D_SSM + n * (SSM_GROUPS + g + 1)]
            cb = _dot_nt(cg, bg)
            xd_g, prev_g = xd_all[:, gs], st[:, gs]
            y_g = _dot(cg, prev_g) * e_x[:, gs] + xs_all[:, gs] * dskip_x[:, gs]
            for r in range(rpg):
                h = g * rpg + r
                lmat = jnp.exp(jnp.where(tril, cs[:, h:h + 1] - cst[h:h + 1, :], -jnp.inf))
                y_g = y_g + jnp.where((lane_g >= p * r) & (lane_g < p * (r + 1)), _dot(cb * lmat, xd_g), 0.0)
            ys.append(y_g)
            new.append(prev_g * cd_x[:, gs] + _dot(bg.T, xdd_all[:, gs]))
        y_ref[...] = jnp.concatenate(ys, axis=1)
        state[...] = jnp.concatenate(new, axis=1)

    vec = pl.BlockSpec((1, LANES), lambda c: (0, 0))
    return pl.pallas_call(
        body, grid=(nc,),
        in_specs=[pl.BlockSpec((ell, CONV_CH), lambda c: (c, 0)), pl.BlockSpec((ell, LANES), lambda c: (c, P_DT // LANES)),
                  vec, vec, vec, pl.BlockSpec((LANES, D_SSM), lambda c: (0, 0))],
        out_specs=(pl.BlockSpec((ell, D_SSM), lambda c: (c, 0)), pl.BlockSpec((1, n, D_SSM), lambda c: (c, 0, 0))),
        out_shape=(SDS((s, D_SSM), F32), SDS((nc, n, D_SSM), F32)),
        scratch_shapes=[pltpu.VMEM((n, D_SSM), F32)], name=name,
        compiler_params=_params(("arbitrary",), 32 << 20))(xbc, proj, dt_bias, a_log, d_skip, _head_expand())


def _ssd_bwd(xbc, proj, dt_bias, a_log, d_skip, prev_states, dy, *, name):
    s = xbc.shape[0]
    nc = s // SSM_CHUNK
    ell, n, p = SSM_CHUNK, SSM_STATE, SSM_HEAD_DIM
    rpg = SSM_HEADS // SSM_GROUPS
    gw = rpg * p

    def body(x_ref, dt_ref, bias_ref, alog_ref, dskip_ref, ps_ref, dy_ref, ex_ref, ext_ref,
             dx_ref, ddt_ref, dalog_ref, ddskip_ref, dbias_ref, dstate):
        @pl.when(pl.program_id(0) == 0)
        def _():
            dstate[...] = jnp.zeros_like(dstate)
            dalog_ref[...] = jnp.zeros_like(dalog_ref)
            ddskip_ref[...] = jnp.zeros_like(ddskip_ref)
            dbias_ref[...] = jnp.zeros_like(dbias_ref)

        pre, dt, a, cs, cs_last, tril = _ssd_chunk_terms(dt_ref[...], bias_ref[...], alog_ref[...])
        e = jnp.exp(cs)
        ds = jnp.exp(cs_last - cs)
        cd = jnp.exp(cs_last)
        cst = cs.T
        shape = (ell, LANES)
        ex, ext = ex_ref[...], ext_ref[...]
        spread = lambda v: jnp.dot(v, ex, precision=HIGHEST, preferred_element_type=F32)
        gather = lambda v: jnp.dot(v, ext, precision=HIGHEST, preferred_element_type=F32)
        dt_x, e_x, ds_x = spread(dt), spread(e), spread(ds)
        cd_x = spread(jnp.broadcast_to(cd, (8, LANES)))[0:1, :]
        dskip_x = spread(jnp.broadcast_to(dskip_ref[...], (8, LANES)))[0:1, :]
        xv, dyv, psv, dst = x_ref[...], dy_ref[...], ps_ref[0], dstate[...]
        xs_all = xv[:, 0:D_SSM]
        xd_all = xs_all * dt_x
        dye_all = dyv * e_x
        xdd_all = xd_all * ds_x
        triu = lax.broadcasted_iota(jnp.int32, (ell, ell), 0) <= lax.broadcasted_iota(jnp.int32, (ell, ell), 1)
        lane_g = lax.broadcasted_iota(jnp.int32, (ell, gw), 1)
        lane = lax.broadcasted_iota(jnp.int32, shape, 1)
        sub = lax.broadcasted_iota(jnp.int32, shape, 0)
        dcs_acc = jnp.zeros(shape, F32)
        dcs_rows = jnp.zeros(shape, F32)
        dxs, dbs, dcs_parts, dprevs, prod_a, prod_b, prod_c, prod_e = [], [], [], [], [], [], [], []
        for g in range(SSM_GROUPS):
            gs = slice(gw * g, gw * (g + 1))
            bg = xv[:, D_SSM + n * g:D_SSM + n * (g + 1)]
            cg = xv[:, D_SSM + n * (SSM_GROUPS + g):D_SSM + n * (SSM_GROUPS + g + 1)]
            cb = _dot_nt(cg, bg)
            cbt = _dot_nt(bg, cg)
            xs_g, dy_g, xd_g, dye_g, xdd_g = xs_all[:, gs], dyv[:, gs], xd_all[:, gs], dye_all[:, gs], xdd_all[:, gs]
            prev_g, dsn_g = psv[:, gs], dst[:, gs]
            cprev_g = _dot(cg, prev_g)
            dprevs.append(dsn_g * cd_x[:, gs] + _dot(cg.T, dye_g))
            dcg = _dot_nt(dye_g, prev_g)
            dxdd_g = _dot(bg, dsn_g)
            dbg = _dot_nt(xdd_g, dsn_g)
            dxd_g = dxdd_g * ds_x[:, gs]
            prod_a.append(dy_g * cprev_g)
            prod_b.append(dxdd_g * xd_g)
            prod_e.append(jnp.sum(dsn_g * prev_g, axis=0, keepdims=True))
            dcb = jnp.zeros((ell, ell), F32)
            for r in range(rpg):
                h = g * rpg + r
                mine = (lane_g >= p * r) & (lane_g < p * (r + 1))
                lmat = jnp.exp(jnp.where(tril, cs[:, h:h + 1] - cst[h:h + 1, :], -jnp.inf))
                lmat_t = jnp.exp(jnp.where(triu, cst[h:h + 1, :] - cs[:, h:h + 1], -jnp.inf))
                dgm = _dot_nt(jnp.where(mine, dy_g, 0.0), xd_g)
                dxd_g = dxd_g + jnp.where(mine, _dot(cbt * lmat_t, dy_g), 0.0)
                mm = dgm * (cb * lmat)
                dcs_acc = dcs_acc + jnp.where(lane == h, jnp.sum(mm, axis=1, keepdims=True), 0.0)
                dcs_rows = dcs_rows + jnp.where(sub == h, jnp.sum(mm, axis=0, keepdims=True), 0.0)
                dcb = dcb + dgm * lmat
            dxs.append(dxd_g * dt_x[:, gs] + dy_g * dskip_x[:, gs])
            prod_c.append(dxd_g * xs_g)
            dbs.append(dbg + _dot_tn(dcb, cg))
            dcs_parts.append(dcg + _dot(dcb, bg))
        dx_ref[...] = jnp.concatenate(dxs + dbs + dcs_parts, axis=1)
        dstate[...] = jnp.concatenate(dprevs, axis=1)
        sum_a = gather(jnp.concatenate(prod_a, axis=1))
        sum_b = gather(jnp.concatenate(prod_b, axis=1))
        sum_c = gather(jnp.concatenate(prod_c, axis=1))
        sum_d = gather(dyv * xs_all)
        dcd = gather(jnp.broadcast_to(jnp.concatenate(prod_e, axis=1), (8, D_SSM)))[0:1, :]
        tmp = sum_b * ds
        dlast = dcd * cd + jnp.sum(tmp, axis=0, keepdims=True)
        dcs = dcs_acc + sum_a * e - tmp - dcs_rows.T + jnp.where(sub == ell - 1, dlast, 0.0)
        dad = jnp.dot(triu.astype(F32), dcs, precision=HIGHEST, preferred_element_type=F32)
        ddt = sum_c + dad * a
        dalog_ref[...] += jnp.sum(dad * dt, axis=0, keepdims=True) * a
        ddskip_ref[...] += jnp.sum(sum_d, axis=0, keepdims=True)
        ddraw = jnp.where(lane < SSM_HEADS, ddt * _sigmoid(pre), 0.0)
        ddt_ref[...] = ddraw.astype(ddt_ref.dtype)
        dbias_ref[...] += jnp.sum(ddraw, axis=0, keepdims=True)

    vec = pl.BlockSpec((1, LANES), lambda c: (0, 0))
    rev = lambda c: nc - 1 - c
    ex = _head_expand()
    outs = pl.pallas_call(
        body, grid=(nc,),
        in_specs=[pl.BlockSpec((ell, CONV_CH), lambda c: (rev(c), 0)),
                  pl.BlockSpec((ell, LANES), lambda c: (rev(c), P_DT // LANES)), vec, vec, vec,
                  pl.BlockSpec((1, n, D_SSM), lambda c: (rev(c), 0, 0)),
                  pl.BlockSpec((ell, D_SSM), lambda c: (rev(c), 0)),
                  pl.BlockSpec((LANES, D_SSM), lambda c: (0, 0)), pl.BlockSpec((D_SSM, LANES), lambda c: (0, 0))],
        out_specs=(pl.BlockSpec((ell, CONV_CH), lambda c: (rev(c), 0)), pl.BlockSpec((ell, LANES), lambda c: (rev(c), 0)),
                   vec, vec, vec),
        out_shape=(SDS((s, CONV_CH), F32), SDS((s, LANES), BF16), SDS((1, LANES), F32), SDS((1, LANES), F32),
                   SDS((1, LANES), F32)),
        scratch_shapes=[pltpu.VMEM((n, D_SSM), F32)], name=name,
        compiler_params=_params(("arbitrary",), 40 << 20))(xbc, proj, dt_bias, a_log, d_skip, prev_states, dy, ex, ex.T)
    return outs


def _rope_swap(t):
    lane = lax.broadcasted_iota(jnp.int32, t.shape, 1)
    half = QK_ROPE // 2
    lo = (lane >= QK_NOPE) & (lane < QK_NOPE + half)
    hi = (lane >= QK_NOPE + half) & (lane < QK_NOPE + QK_ROPE)
    return jnp.where(lo, pltpu.roll(t, HEAD_PAD - half, axis=1), jnp.where(hi, pltpu.roll(t, half, axis=1), 0.0))


def _mla_prep(q, kv, proj, cos, sins, *, name):
    s = q.shape[0]
    tm = min(s, 256)
    scale = (QK_NOPE + QK_ROPE) ** -0.5

    def body(q_ref, kv_ref, kr_ref, cos_ref, sin_ref, qo_ref, ko_ref, vo_ref):
        cosv, sinv = cos_ref[...], sin_ref[...]
        kr = pltpu.roll(kr_ref[...], QK_NOPE, axis=1)
        lane = lax.broadcasted_iota(jnp.int32, kr.shape, 1)
        nope = lane < QK_NOPE
        kr = jnp.where(nope, 0.0, kr)
        kpe = kr * cosv + _rope_swap(kr) * sinv
        for hp in range(MLA_HEADS // 2):
            vs = []
            for h in (2 * hp, 2 * hp + 1):
                hs = slice(HEAD_PAD * h, HEAD_PAD * (h + 1))
                qh = q_ref[:, hs]
                kvh = kv_ref[:, hs]
                qo_ref[:, hs] = ((qh * cosv + _rope_swap(qh) * sinv) * scale).astype(qo_ref.dtype)
                ko_ref[:, hs] = (jnp.where(nope, kvh, 0.0) + kpe).astype(ko_ref.dtype)
                vs.append(kvh[:, QK_NOPE:])
            vo_ref[:, 2 * V_DIM * hp:2 * V_DIM * (hp + 1)] = jnp.concatenate(vs, axis=1).astype(vo_ref.dtype)

    wide = pl.BlockSpec((tm, MLA_HEADS * HEAD_PAD), lambda i: (i, 0))
    half = pl.BlockSpec((tm, MLA_HEADS * V_DIM), lambda i: (i, 0))
    tab = pl.BlockSpec((tm, LANES), lambda i: (i, 0))
    return pl.pallas_call(
        body, grid=(s // tm,),
        in_specs=[wide, wide, pl.BlockSpec((tm, LANES), lambda i: (i, P_KR // LANES)), tab, tab],
        out_specs=(wide, wide, half),
        out_shape=(SDS((s, MLA_HEADS * HEAD_PAD), BF16), SDS((s, MLA_HEADS * HEAD_PAD), BF16),
                   SDS((s, MLA_HEADS * V_DIM), BF16)), name=name,
        compiler_params=_params(("parallel",), 32 << 20))(q, kv, proj, cos, sins)


def _mla_prep_bwd(dqr, dkr, dv, cos, sins, *, name):
    s = dqr.shape[0]
    tm = min(s, 256)
    scale = (QK_NOPE + QK_ROPE) ** -0.5

    def body(dq_ref, dk_ref, dv_ref, cos_ref, sin_ref, dqo_ref, dkv_ref, dkr_ref):
        cosv, sinv = cos_ref[...], sin_ref[...]
        lane = lax.broadcasted_iota(jnp.int32, cosv.shape, 1)
        ksum = jnp.zeros(cosv.shape, F32)
        for h in range(MLA_HEADS):
            hs = slice(HEAD_PAD * h, HEAD_PAD * (h + 1))
            d = dq_ref[:, hs]
            dk = dk_ref[:, hs]
            dqo_ref[:, hs] = ((d * cosv + _rope_swap(d * sinv)) * scale).astype(dqo_ref.dtype)
            dkv_ref[:, hs] = jnp.concatenate([dk[:, :QK_NOPE], dv_ref[:, V_DIM * h:V_DIM * (h + 1)]], axis=1).astype(dkv_ref.dtype)
            ksum = ksum + dk
        ksum = jnp.where((lane >= QK_NOPE) & (lane < QK_NOPE + QK_ROPE), ksum, 0.0)
        un = ksum * cosv + _rope_swap(ksum * sinv)
        dkr_ref[...] = pltpu.roll(un, HEAD_PAD - QK_NOPE, axis=1).astype(dkr_ref.dtype)

    wide = pl.BlockSpec((tm, MLA_HEADS * HEAD_PAD), lambda i: (i, 0))
    half = pl.BlockSpec((tm, MLA_HEADS * V_DIM), lambda i: (i, 0))
    tab = pl.BlockSpec((tm, LANES), lambda i: (i, 0))
    return pl.pallas_call(
        body, grid=(s // tm,), in_specs=[wide, wide, half, tab, tab], out_specs=(wide, wide, tab),
        out_shape=(SDS((s, MLA_HEADS * HEAD_PAD), BF16), SDS((s, MLA_HEADS * HEAD_PAD), BF16), SDS((s, LANES), BF16)),
        name=name, compiler_params=_params(("parallel",), 40 << 20))(dqr, dkr, dv, cos, sins)


FLASH_TILE = 512
FLASH_ROWS = 32


def _flash_fwd(q, k, v, *, name):
    s = q.shape[0]
    t = min(s, FLASH_TILE)
    nq = s // t
    npair = MLA_HEADS // 2

    def body(q_ref, k_ref, v_ref, o_ref, lse_ref):
        i = pl.program_id(1)
        qs = [q_ref[:, HEAD_PAD * e:HEAD_PAD * (e + 1)] for e in range(2)]
        diag = lax.broadcasted_iota(jnp.int32, (t, t), 0) >= lax.broadcasted_iota(jnp.int32, (t, t), 1)

        def step(j, carry, masked):
            rows = pl.ds(pl.multiple_of(j * t, t), t)
            new = []
            for e in range(2):
                m, l, acc = carry[e]
                sc = _dot_nt(qs[e], k_ref[rows, HEAD_PAD * e:HEAD_PAD * (e + 1)])
                if masked:
                    sc = jnp.where(diag, sc, NEG)
                m_new = jnp.maximum(m, jnp.max(sc, axis=1, keepdims=True))
                pr = jnp.exp(sc - m_new)
                alpha = jnp.exp(m - m_new)
                l = alpha * l + jnp.sum(pr, axis=1, keepdims=True)
                acc = alpha * acc + _dot(pr, v_ref[rows, V_DIM * e:V_DIM * (e + 1)])
                new.append((m_new, l, acc))
            return tuple(new)

        init = tuple((jnp.full((t, 1), NEG, F32), jnp.zeros((t, 1), F32), jnp.zeros((t, V_DIM), F32)) for _ in range(2))
        carry = lax.fori_loop(0, i, functools.partial(step, masked=False), init)
        carry = step(i, carry, True)
        o_ref[...] = jnp.concatenate([acc / l for _, l, acc in carry], axis=1)
        lse_ref[0] = jnp.concatenate([jnp.broadcast_to(m + jnp.log(l), (t, V_DIM)) for m, l, _ in carry], axis=1)

    return pl.pallas_call(
        body, grid=(npair, nq),
        in_specs=[pl.BlockSpec((t, 2 * HEAD_PAD), lambda hp, i: (i, hp)), pl.BlockSpec((s, 2 * HEAD_PAD), lambda hp, i: (0, hp)),
                  pl.BlockSpec((s, 2 * V_DIM), lambda hp, i: (0, hp))],
        out_specs=(pl.BlockSpec((t, 2 * V_DIM), lambda hp, i: (i, hp)), pl.BlockSpec((1, t, LANES), lambda hp, i: (hp, i, 0))),
        out_shape=(SDS((s, MLA_HEADS * V_DIM), F32), SDS((npair, s, LANES), F32)), name=name,
        compiler_params=_params(("parallel", "parallel"), 40 << 20))(q, k, v)


def _flash_bwd(q, k, v, o, lse, do, *, name):
    s = q.shape[0]
    t = min(s, FLASH_TILE)
    nq = s // t
    npair = MLA_HEADS // 2
    nchunk = t // FLASH_ROWS

    def valid_cols(r):
        return min(t, -(-((r + 1) * FLASH_ROWS) // LANES) * LANES)

    def body(q_ref, k_ref, v_ref, o_ref, lse_ref, do_ref, dq_ref, dk_ref, dv_ref, s_scr, dp_scr, p_scr, ds_scr, dk_acc, dv_acc):
        j = pl.program_id(1)

        @pl.when(j == 0)
        def _():
            dq_ref[...] = jnp.zeros_like(dq_ref)

        dk_acc[...] = jnp.zeros(dk_acc.shape, F32)
        dv_acc[...] = jnp.zeros(dv_acc.shape, F32)
        qsl = [slice(HEAD_PAD * e, HEAD_PAD * (e + 1)) for e in range(2)]
        vsl = [slice(V_DIM * e, V_DIM * (e + 1)) for e in range(2)]

        def step(i, carry, masked):
            rows = pl.ds(pl.multiple_of(i * t, t), t)
            for e in range(2):
                ke = k_ref[:, qsl[e]]
                qi = q_ref[rows, qsl[e]]
                doi = do_ref[rows, vsl[e]]
                delta = jnp.sum(doi * o_ref[rows, vsl[e]], axis=1, keepdims=True)
                lse_i = lse_ref[0, rows, vsl[e]][:, 0:1]
                dob = doi.astype(BF16)
                s_scr[e] = _dot_nt(qi, ke)
                dp_scr[e] = _dot_nt(dob, v_ref[:, vsl[e]])
                for r in range(nchunk):
                    rs = slice(r * FLASH_ROWS, (r + 1) * FLASH_ROWS)
                    width = valid_cols(r) if masked else t
                    sc = s_scr[e, rs, 0:width]
                    if masked:
                        row = r * FLASH_ROWS + lax.broadcasted_iota(jnp.int32, (FLASH_ROWS, width), 0)
                        sc = jnp.where(row >= lax.broadcasted_iota(jnp.int32, (FLASH_ROWS, width), 1), sc, NEG)
                    pr = jnp.exp(sc - lse_i[rs, :])
                    dsc = pr * (dp_scr[e, rs, 0:width] - delta[rs, :])
                    p_scr[e, rs, 0:width] = pr.astype(BF16)
                    ds_scr[e, rs, 0:width] = dsc.astype(BF16)
                    if width < t:
                        p_scr[e, rs, width:t] = jnp.zeros((FLASH_ROWS, t - width), BF16)
                        ds_scr[e, rs, width:t] = jnp.zeros((FLASH_ROWS, t - width), BF16)
                dv_acc[e] += _dot_tn(p_scr[e], dob)
                dk_acc[e] += _dot_tn(ds_scr[e], qi)
                dq_ref[rows, qsl[e]] += _dot(ds_scr[e], ke)
            return carry

        step(j, 0, True)
        lax.fori_loop(j + 1, nq, functools.partial(step, masked=False), 0)
        dk_ref[...] = jnp.concatenate([dk_acc[e] for e in range(2)], axis=1)
        dv_ref[...] = jnp.concatenate([dv_acc[e] for e in range(2)], axis=1)

    full_q = pl.BlockSpec((s, 2 * HEAD_PAD), lambda hp, j: (0, hp))
    full_v = pl.BlockSpec((s, 2 * V_DIM), lambda hp, j: (0, hp))
    blk_k = pl.BlockSpec((t, 2 * HEAD_PAD), lambda hp, j: (j, hp))
    blk_v = pl.BlockSpec((t, 2 * V_DIM), lambda hp, j: (j, hp))
    return pl.pallas_call(
        body, grid=(npair, nq),
        in_specs=[full_q, blk_k, blk_v, full_v, pl.BlockSpec((1, s, LANES), lambda hp, j: (hp, 0, 0)), full_v],
        out_specs=(full_q, blk_k, blk_v),
        out_shape=(SDS((s, MLA_HEADS * HEAD_PAD), F32), SDS((s, MLA_HEADS * HEAD_PAD), F32), SDS((s, MLA_HEADS * V_DIM), F32)),
        scratch_shapes=[pltpu.VMEM((2, t, t), F32), pltpu.VMEM((2, t, t), F32), pltpu.VMEM((2, t, t), BF16),
                        pltpu.VMEM((2, t, t), BF16), pltpu.VMEM((2, t, HEAD_PAD), F32), pltpu.VMEM((2, t, V_DIM), F32)],
        name=name, compiler_params=_params(("parallel", "arbitrary"), 48 << 20))(q, k, v, o, lse, do)


def _mem_attn_fwd(q, k, v, *, name):
    s = q.shape[0]
    tm = min(s, 512)
    ml = k.shape[0]
    scale = MEM_HEAD_DIM ** -0.5

    def body(q_ref, k_ref, v_ref, o_ref):
        for h in range(MEM_HEADS):
            hs = slice(MEM_HEAD_DIM * h, MEM_HEAD_DIM * (h + 1))
            sc = _dot_nt(q_ref[:, hs], k_ref[:, hs]) * scale
            pr = jnp.exp(sc - jnp.max(sc, axis=1, keepdims=True))
            pr = pr / jnp.sum(pr, axis=1, keepdims=True)
            o_ref[:, hs] = _dot(pr, v_ref[:, hs]).astype(o_ref.dtype)

    blk = pl.BlockSpec((tm, D_MODEL), lambda i: (i, 0))
    kv = pl.BlockSpec((ml, D_MODEL), lambda i: (0, 0))
    return pl.pallas_call(body, grid=(s // tm,), in_specs=[blk, kv, kv], out_specs=blk,
                          out_shape=SDS((s, D_MODEL), BF16), name=name,
                          compiler_params=_params(("parallel",), 24 << 20))(q, k, v)


def _mem_attn_bwd(q, k, v, do, *, name):
    s = q.shape[0]
    tm = min(s, 512)
    ml = k.shape[0]
    scale = MEM_HEAD_DIM ** -0.5

    def body(q_ref, k_ref, v_ref, do_ref, dq_ref, dk_ref, dv_ref):
        @pl.when(pl.program_id(0) == 0)
        def _():
            dk_ref[...] = jnp.zeros_like(dk_ref)
            dv_ref[...] = jnp.zeros_like(dv_ref)

        for h in range(MEM_HEADS):
            hs = slice(MEM_HEAD_DIM * h, MEM_HEAD_DIM * (h + 1))
            qh, kh, vh, doh = q_ref[:, hs], k_ref[:, hs], v_ref[:, hs], do_ref[:, hs]
            sc = _dot_nt(qh, kh) * scale
            pr = jnp.exp(sc - jnp.max(sc, axis=1, keepdims=True))
            pr = pr / jnp.sum(pr, axis=1, keepdims=True)
            dp = _dot_nt(doh, vh)
            dsc = pr * (dp - jnp.sum(pr * dp, axis=1, keepdims=True)) * scale
            dq_ref[:, hs] = _dot(dsc, kh).astype(dq_ref.dtype)
            dk_ref[:, hs] += _dot_tn(dsc, qh)
            dv_ref[:, hs] += _dot_tn(pr, doh)

    blk = pl.BlockSpec((tm, D_MODEL), lambda i: (i, 0))
    kv = pl.BlockSpec((ml, D_MODEL), lambda i: (0, 0))
    return pl.pallas_call(body, grid=(s // tm,), in_specs=[blk, kv, kv, blk], out_specs=(blk, kv, kv),
                          out_shape=(SDS((s, D_MODEL), BF16), SDS((ml, D_MODEL), F32), SDS((ml, D_MODEL), F32)), name=name,
                          compiler_params=_params(("arbitrary",), 32 << 20))(q, k, v, do)


MATS = (("w_in", (1024, 940), 1), ("w_uq", (384, 384), 1), ("w_ukv", (256, 512), 1), ("w_out", (512, 1024), 0),
        ("ssm_conv_w", (4, 512), 1),
        ("w_mq", (256, 1024), 0), ("w_mk", (256, 1024), 0), ("w_mv", (256, 1024), 0), ("w_mo", (256, 1024), 0),
        ("w_up", (1024, 1408), 1), ("w_down", (704, 1024), 0), ("ffn_conv_w", (3, 1408), 1))
GROUPS = {"proj": (0,), "mixer": (1, 2, 3, 4), "mem": (5, 6, 7, 8), "ffn": (9, 10, 11)}
UP_SHARD_COLS = 1408
F32_ON_WIRE = ("ssm_conv_w", "ffn_conv_w")
SMALL = (("norm_mix", 1024), ("ssm_conv_b", 2048), ("dt_bias", 16), ("a_log", 16), ("d_skip", 16), ("ssm_norm", 1024),
         ("q_norm", 384), ("kv_norm", 256), ("attn_out_norm", 1024), ("norm_mem_q", 1024), ("norm_mem_kv", 1024),
         ("norm_ffn", 1024), ("ffn_conv_b", 5632))
PACK_COLS = 1024


def _pad_cols(t, n):
    return jnp.pad(t, ((0, 0),) * (t.ndim - 1) + ((0, n - t.shape[-1]),))


def _w_in_to_padded(t):
    z, xbc, dt, cq, ckv, kr = jnp.split(t, (1024, 3072, 3088, 3472, 3728), axis=-1)
    return jnp.concatenate([xbc, z, cq, _pad_cols(dt, LANES), ckv, _pad_cols(kr, P_IN - P_KR)], axis=-1)


def _w_in_from_padded(t):
    return jnp.concatenate([t[..., P_Z:P_Z + 1024], t[..., P_XBC:P_XBC + 2048], t[..., P_DT:P_DT + SSM_HEADS],
                            t[..., P_CQ:P_CQ + Q_LORA], t[..., P_CKV:P_CKV + KV_LORA], t[..., P_KR:P_KR + QK_ROPE]], axis=-1)


def _cols_joined(g):
    return jnp.concatenate([g[j] for j in range(N_CHIPS)], axis=-1)


def _cols_by_chip(t, dtype):
    k = t.shape[0]
    return t.reshape(k, N_CHIPS, -1).transpose(1, 0, 2).astype(dtype)


def _rows_by_chip(t):
    return t.reshape(N_CHIPS, -1, t.shape[-1])


def _mixer_weights(gw):
    wl = {}
    uq = _cols_joined(gw["w_uq"]).reshape(Q_LORA, MLA_HEADS, QK_NOPE + QK_ROPE)
    wl["w_uq"] = _pad_cols(uq, HEAD_PAD).reshape(Q_LORA, MLA_HEADS * HEAD_PAD)
    wl["w_ukv"] = _cols_joined(gw["w_ukv"])
    wl["ssm_conv_w"] = _cols_joined(gw["ssm_conv_w"])
    return wl


def _layer_fwd(x0, mem, cos, sins, weights, sp, li):
    n = lambda t: f"l{li}_{t}"
    lead = ()
    sv = {"x0": x0}
    gw = dict(weights("proj", x0))
    w_in = _w_in_to_padded(_cols_joined(gw["w_in"]))
    h = _rms_fwd(x0, sp["norm_mix"], name=n("mix_norm"))
    in_hbm = lambda t: pltpu.with_memory_space_constraint(t, pltpu.HBM)
    proj = in_hbm(_mm(h, w_in, name=n("mix_proj")))
    gw.update(weights("mixer", proj))
    wl = dict(_mixer_weights(gw), w_in=w_in)
    xbc = in_hbm(_ssm_conv_fwd(proj, wl["ssm_conv_w"], sp["ssm_conv_b"], name=n("ssm_conv")))
    y, pstates = _ssd_fwd(xbc, proj, sp["dt_bias"], sp["a_log"], sp["d_skip"], name=n("ssd"))
    y_ssm = _gated_rms_fwd(y, proj, sp["ssm_norm"], name=n("ssm_gate"))
    cqn = _rms_fwd(proj, sp["q_norm"], col=(Q_LORA, P_CQ // Q_LORA), name=n("q_norm"))
    ckvn = _rms_fwd(proj, sp["kv_norm"], col=(KV_LORA, P_CKV // KV_LORA), name=n("kv_norm"))
    q = in_hbm(_mm(cqn, wl["w_uq"], name=n("uq")))
    kv = in_hbm(_mm(ckvn, wl["w_ukv"], name=n("ukv")))
    qr, kr, v = _mla_prep(q, kv, proj, cos, sins, name=n("rope"))
    att, lse = _flash_fwd(qr, kr, v, name=n("flash"))
    y_att = _rms_fwd(att, sp["attn_out_norm"], name=n("att_norm"))
    x1 = _mm(y_ssm, gw["w_out"], b_lead=lead, b_rows=(0, D_SSM), res=x0, name=n("out_a"))
    x1 = _mm(y_att, gw["w_out"], b_lead=lead, b_rows=(D_SSM, D_SSM), res=x1, name=n("out_b"))
    sv.update(h=h, proj=proj, xbc=xbc, y=y, pstates=pstates, y_ssm=y_ssm, cqn=cqn, ckvn=ckvn, qr=qr, kr=kr, v=v,
              att=att, lse=lse, y_att=y_att, x1=x1)
    gw.update(weights("mem", x1))
    hq = _rms_fwd(x1, sp["norm_mem_q"], name=n("memq_norm"))
    hm = _rms_fwd(mem, sp["norm_mem_kv"], name=n("memkv_norm"))
    mq = _mm(hq, gw["w_mq"], b_lead=lead, out_dtype=BF16, name=n("mq"))
    mk = _mm(hm, gw["w_mk"], b_lead=lead, out_dtype=BF16, name=n("mk"))
    mv = _mm(hm, gw["w_mv"], b_lead=lead, out_dtype=BF16, name=n("mv"))
    mo = _mem_attn_fwd(mq, mk, mv, name=n("mem_attn"))
    x2 = _mm(mo, gw["w_mo"], b_lead=lead, res=x1, name=n("mo"))
    sv.update(hq=hq, hm=hm, mq=mq, mk=mk, mv=mv, mo=mo, x2=x2)
    gw.update(weights("ffn", x2))
    wl["ffn_conv_w"] = _cols_joined(gw["ffn_conv_w"])
    hf = _rms_fwd(x2, sp["norm_ffn"], name=n("ffn_norm"))
    up_g = _mm(hf, gw["w_up"], b_lead=lead, b_chips=(0, 2), name=n("up_g"))
    up_v = _mm(hf, gw["w_up"], b_lead=lead, b_chips=(2, 2), name=n("up_v"))
    act = _ffn_conv_fwd(up_g, up_v, wl["ffn_conv_w"], sp["ffn_conv_b"], name=n("ffn_conv"))
    x3 = _mm(act, gw["w_down"], b_lead=lead, res=x2, name=n("down"))
    sv.update(hf=hf, up_g=up_g, up_v=up_v, act=act)
    return x3, sv, gw, wl


def _layer_bwd(dx3, dx3b, mem, cos, sins, gw, wl, sp, sv, li, emit):
    n = lambda t: f"l{li}_b_{t}"
    lead = ()
    g = {}

    def after(token, v):
        return v if token is None else v + token[0, 0]

    dact = _mm(dx3b, gw["w_down"], tb=True, b_lead=lead, out_dtype=BF16, name=n("down_dx"))
    g["w_down"] = _rows_by_chip(_mm(sv["act"], dx3b, ta=True, out_dtype=BF16, name=n("down_dw")))
    dup_g, dup_v, dcw, g["ffn_conv_b"] = _ffn_conv_bwd(
        sv["up_g"], sv["up_v"], wl["ffn_conv_w"], sp["ffn_conv_b"], dact, name=n("ffn_conv"))
    g["ffn_conv_w"] = _cols_by_chip(dcw, F32)
    nsh = UP_SHARD_COLS
    dhf = None
    for c4 in range(N_CHIPS):
        dhf = _mm(dup_g if c4 < 2 else dup_v, gw["w_up"], tb=True, a_col=(nsh, c4 % 2), b_lead=(c4,), res=dhf,
                  name=n(f"up{c4}_dx"))
    g["w_up"] = jnp.concatenate([_mm(sv["hf"], dup_g, ta=True, o_chips=nsh, out_dtype=BF16, name=n("upg_dw")),
                                 _mm(sv["hf"], dup_v, ta=True, o_chips=nsh, out_dtype=BF16, name=n("upv_dw"))], axis=0)
    dx2, dx2b, g["norm_ffn"] = _rms_bwd(sv["x2"], after(emit("ffn", g), sp["norm_ffn"]), dhf, dx3, name=n("ffn_norm"))
    dmo = _mm(dx2b, gw["w_mo"], tb=True, b_lead=lead, out_dtype=BF16, name=n("mo_dx"))
    g["w_mo"] = _rows_by_chip(_mm(sv["mo"], dx2b, ta=True, out_dtype=BF16, name=n("mo_dw")))
    dmq, dmk, dmv = _mem_attn_bwd(sv["mq"], sv["mk"], sv["mv"], dmo, name=n("mem_attn"))
    dhq = _mm(dmq, gw["w_mq"], tb=True, b_lead=lead, name=n("mq_dx"))
    g["w_mq"] = _rows_by_chip(_mm(sv["hq"], dmq, ta=True, out_dtype=BF16, name=n("mq_dw")))
    dhm = _mm(dmk, gw["w_mk"], tb=True, b_lead=lead, name=n("mk_dx"))
    dhm = _mm(dmv, gw["w_mv"], tb=True, b_lead=lead, res=dhm, name=n("mv_dx"))
    g["w_mk"] = _rows_by_chip(_mm(sv["hm"], dmk, ta=True, out_dtype=BF16, name=n("mk_dw")))
    g["w_mv"] = _rows_by_chip(_mm(sv["hm"], dmv, ta=True, out_dtype=BF16, name=n("mv_dw")))
    dx1, dx1b, g["norm_mem_q"] = _rms_bwd(sv["x1"], after(emit("mem", g), sp["norm_mem_q"]), dhq, dx2, name=n("memq_norm"))
    _, _, g["norm_mem_kv"] = _rms_bwd(mem, sp["norm_mem_kv"], dhm, name=n("memkv_norm"))
    dy_ssm = _mm(dx1b, gw["w_out"], tb=True, b_lead=lead, b_rows=(0, D_SSM), name=n("outa_dx"))
    dy_att = _mm(dx1b, gw["w_out"], tb=True, b_lead=lead, b_rows=(D_SSM, D_SSM), name=n("outb_dx"))
    g["w_out"] = _rows_by_chip(jnp.concatenate([_mm(sv["y_ssm"], dx1b, ta=True, out_dtype=BF16, name=n("outa_dw")),
                                                _mm(sv["y_att"], dx1b, ta=True, out_dtype=BF16, name=n("outb_dw"))], axis=0))
    datt, _, g["attn_out_norm"] = _rms_bwd(sv["att"], sp["attn_out_norm"], dy_att, name=n("att_norm"))
    dqr, dkr, dv = _flash_bwd(sv["qr"], sv["kr"], sv["v"], sv["att"], sv["lse"], datt, name=n("flash"))
    dq, dkv, dkrope = _mla_prep_bwd(dqr, dkr, dv, cos, sins, name=n("rope"))
    duq = _mm(sv["cqn"], dq, ta=True, name=n("uq_dw")).reshape(Q_LORA, MLA_HEADS, HEAD_PAD)[..., :QK_NOPE + QK_ROPE]
    g["w_uq"] = _cols_by_chip(duq.reshape(Q_LORA, -1), BF16)
    dcqn = _mm(dq, wl["w_uq"], tb=True, name=n("uq_dx"))
    g["w_ukv"] = _cols_by_chip(_mm(sv["ckvn"], dkv, ta=True, name=n("ukv_dw")), BF16)
    dckvn = _mm(dkv, wl["w_ukv"], tb=True, name=n("ukv_dx"))
    proj = sv["proj"]
    _, dcq, g["q_norm"] = _rms_bwd(proj, sp["q_norm"], dcqn, col=(Q_LORA, P_CQ // Q_LORA), name=n("q_norm"))
    _, dckv, g["kv_norm"] = _rms_bwd(proj, sp["kv_norm"], dckvn, col=(KV_LORA, P_CKV // KV_LORA), name=n("kv_norm"))
    dy, dz, g["ssm_norm"] = _gated_rms_bwd(sv["y"], proj, sp["ssm_norm"], dy_ssm, name=n("ssm_gate"))
    dxbc, ddt, da_log, dd_skip, ddt_bias = _ssd_bwd(
        sv["xbc"], proj, sp["dt_bias"], sp["a_log"], sp["d_skip"], sv["pstates"], dy, name=n("ssd"))
    g["a_log"], g["d_skip"], g["dt_bias"] = da_log[0, :SSM_HEADS], dd_skip[0, :SSM_HEADS], ddt_bias[0, :SSM_HEADS]
    dxbc_pre, dsw, g["ssm_conv_b"] = _ssm_conv_bwd(proj, wl["ssm_conv_w"], sp["ssm_conv_b"], dxbc, name=n("ssm_conv"))
    g["ssm_conv_w"] = _cols_by_chip(dsw, F32)
    started = emit("mixer", g)
    s = proj.shape[0]
    dproj = jnp.concatenate([dxbc_pre, dz, dcq, ddt, dckv, dkrope,
                             jnp.zeros((s, P_IN - P_KR - LANES), BF16)], axis=1)
    dh = _mm(dproj, wl["w_in"], tb=True, name=n("proj_dx"))
    g["w_in"] = _cols_by_chip(_w_in_from_padded(_mm(sv["h"], dproj, ta=True, name=n("proj_dw"))), BF16)
    dx0, dx0b, g["norm_mix"] = _rms_bwd(sv["x0"], after(started, sp["norm_mix"]), dh, dx1, name=n("mix_norm"))
    return dx0, dx0b, g, emit("proj", g)


def _chip_peers(x, y):
    return [(1 - x, y), (x, 1 - y), (1 - x, 1 - y)]


HBM_SPEC = pl.BlockSpec(memory_space=pltpu.HBM)
SEM_SPEC = pl.BlockSpec(memory_space=pltpu.SEMAPHORE)
ANY_SPEC = pl.BlockSpec(memory_space=pl.ANY)
VMEM_SPEC = pl.BlockSpec(memory_space=pltpu.VMEM)
DATAFLOW = pltpu.SideEffectType.DATAFLOW_SIDE_EFFECTING
TOKEN_SHAPE = (8, LANES)


def _exchange_start(srcs, land_shapes, src_view, dst_view, token, *, name):
    n = len(srcs)

    def body(*refs):
        s, l, tok_in = refs[:n], refs[n:2 * n], refs[2 * n]
        send_sems, recv_sems = refs[2 * n + 1], refs[2 * n + 2]
        tok_out = refs[-1]
        x, y, c = lax.axis_index("x"), lax.axis_index("y"), lax.axis_index("c")
        me = 2 * x + y
        for t in range(n):
            for k, (px, py) in enumerate(_chip_peers(x, y)):
                pltpu.make_async_remote_copy(
                    src_ref=src_view(t, s[t], 2 * px + py), dst_ref=dst_view(t, l[t], me), send_sem=send_sems.at[3 * t + k],
                    recv_sem=recv_sems.at[3 * t + k], device_id=(px, py, c), device_id_type=MESH).start()
            pltpu.make_async_copy(src_view(t, s[t], me), dst_view(t, l[t], me), send_sems.at[3 * n + t]).start()
        tok_out[...] = tok_in[...]

    hbm = lambda t: pltpu.with_memory_space_constraint(t, pltpu.HBM)
    lands = [lax.empty(l.shape, l.dtype) for l in land_shapes]
    outs = pl.pallas_call(
        body, name=name,
        out_shape=(pltpu.SemaphoreType.DMA((4 * n,)), pltpu.SemaphoreType.DMA((3 * n,)),
                   *[pltpu.HBM(l.shape, l.dtype) for l in land_shapes], SDS(TOKEN_SHAPE, F32)),
        in_specs=[HBM_SPEC] * (2 * n) + [VMEM_SPEC], out_specs=(SEM_SPEC, SEM_SPEC, *[HBM_SPEC] * n, VMEM_SPEC),
        input_output_aliases={n + t: 2 + t for t in range(n)},
        compiler_params=pltpu.CompilerParams(has_side_effects=DATAFLOW))(*[hbm(t) for t in srcs], *[hbm(t) for t in lands], token)
    return outs[0], outs[1], list(outs[2:2 + n]), outs[-1]


def _exchange_wait(srcs, lands, send_sems, recv_sems, after, src_view, dst_view, which, *, name):
    n = len(srcs)
    m = len(which)

    def body(*refs):
        s, l = refs[:m], refs[m:2 * m]
        send_ref, recv_ref = refs[2 * m], refs[2 * m + 1]
        x, y, c = lax.axis_index("x"), lax.axis_index("y"), lax.axis_index("c")
        me = 2 * x + y
        for i, t in enumerate(which):
            for k, (px, py) in enumerate(_chip_peers(x, y)):
                chip = 2 * px + py
                cp = pltpu.make_async_remote_copy(
                    src_ref=src_view(t, s[i], chip), dst_ref=dst_view(t, l[i], chip), send_sem=send_ref.at[3 * t + k],
                    recv_sem=recv_ref.at[3 * t + k], device_id=(px, py, c), device_id_type=MESH)
                cp.wait_send()
                cp.wait_recv()
            pltpu.make_async_copy(src_view(t, s[i], me), dst_view(t, l[i], me), send_ref.at[3 * n + t]).wait()

    outs = pl.pallas_call(
        body, name=name, out_shape=[pltpu.HBM(lands[t].shape, lands[t].dtype) for t in which],
        in_specs=[HBM_SPEC] * (2 * m) + [SEM_SPEC, SEM_SPEC, ANY_SPEC], out_specs=[HBM_SPEC] * m,
        input_output_aliases={m + i: i for i in range(m)},
        compiler_params=pltpu.CompilerParams(has_side_effects=DATAFLOW))(
            *[srcs[t] for t in which], *[lands[t] for t in which], send_sems, recv_sems, after)
    return list(outs)


def _gather_layer_start(shards, li, token, tag=""):
    src_view = lambda t, ref, chip: ref.at[li]
    dst_view = lambda t, ref, chip: ref.at[chip]
    send_sems, recv_sems, lands, token = _exchange_start(
        shards, [SDS((N_CHIPS,) + s.shape[1:], s.dtype) for s in shards], src_view, dst_view, token,
        name=f"gather{li}{tag}_start")
    return (shards, lands, send_sems, recv_sems, src_view, dst_view, f"gather{li}{tag}"), token


def _scatter_start(grads, tag, token):
    view = lambda t, ref, chip: ref.at[chip]
    send_sems, recv_sems, lands, token = _exchange_start(
        grads, [SDS(g.shape, g.dtype) for g in grads], view, view, token, name=f"scatter{tag}_start")
    return (grads, lands, send_sems, recv_sems, view, view, f"scatter{tag}"), token


def _exchange_finish(handle, after, which=None, tag=""):
    srcs, lands, send_sems, recv_sems, src_view, dst_view, name = handle
    which = tuple(range(len(srcs))) if which is None else which
    return _exchange_wait(srcs, lands, send_sems, recv_sems, after, src_view, dst_view, which, name=f"{name}{tag}_wait")


def _swap_start(bufs, token, *, name):
    n = len(bufs)

    def body(*refs):
        s, l, tok_in = refs[:n], refs[n:2 * n], refs[2 * n]
        send_sems, recv_sems = refs[2 * n + 1], refs[2 * n + 2]
        x, y, c = lax.axis_index("x"), lax.axis_index("y"), lax.axis_index("c")
        for t in range(n):
            pltpu.make_async_remote_copy(src_ref=s[t], dst_ref=l[t], send_sem=send_sems.at[t], recv_sem=recv_sems.at[t],
                                         device_id=(x, y, 1 - c), device_id_type=MESH).start()
        refs[-1][...] = tok_in[...]

    hbm = lambda t: pltpu.with_memory_space_constraint(t, pltpu.HBM)
    lands = [lax.empty(b.shape, b.dtype) for b in bufs]
    outs = pl.pallas_call(
        body, name=f"{name}_start",
        out_shape=(pltpu.SemaphoreType.DMA((n,)), pltpu.SemaphoreType.DMA((n,)),
                   *[pltpu.HBM(b.shape, b.dtype) for b in bufs], SDS(TOKEN_SHAPE, F32)),
        in_specs=[HBM_SPEC] * (2 * n) + [VMEM_SPEC], out_specs=(SEM_SPEC, SEM_SPEC, *[HBM_SPEC] * n, VMEM_SPEC),
        input_output_aliases={n + t: 2 + t for t in range(n)},
        compiler_params=pltpu.CompilerParams(has_side_effects=DATAFLOW))(*[hbm(t) for t in bufs], *[hbm(t) for t in lands], token)
    return (bufs, list(outs[2:2 + n]), outs[0], outs[1], name), outs[-1]


def _swap_wait(handle, after):
    bufs, lands, send_sems, recv_sems, name = handle
    n = len(bufs)

    def body(*refs):
        s, l = refs[:n], refs[n:2 * n]
        send_ref, recv_ref = refs[2 * n], refs[2 * n + 1]
        x, y, c = lax.axis_index("x"), lax.axis_index("y"), lax.axis_index("c")
        for t in range(n):
            cp = pltpu.make_async_remote_copy(src_ref=s[t], dst_ref=l[t], send_sem=send_ref.at[t], recv_sem=recv_ref.at[t],
                                              device_id=(x, y, 1 - c), device_id_type=MESH)
            cp.wait_send()
            cp.wait_recv()

    outs = pl.pallas_call(
        body, name=f"{name}_wait", out_shape=[pltpu.HBM(b.shape, b.dtype) for b in bufs],
        in_specs=[HBM_SPEC] * (2 * n) + [SEM_SPEC, SEM_SPEC, ANY_SPEC], out_specs=[HBM_SPEC] * n,
        input_output_aliases={n + t: t for t in range(n)},
        compiler_params=pltpu.CompilerParams(has_side_effects=DATAFLOW))(*bufs, *lands, send_sems, recv_sems, after)
    return list(outs)


def _all_gather8(src, *, name):
    def body(src_ref, out_ref, send_sems, recv_sems, local_sem):
        x, y, c = lax.axis_index("x"), lax.axis_index("y"), lax.axis_index("c")
        me = 4 * x + 2 * y + c
        mine = pltpu.make_async_copy(src_ref, out_ref.at[me], local_sem)
        mine.start()

        def peer(k):
            return (x ^ (k >> 2 & 1), y ^ (k >> 1 & 1), c ^ (k & 1))

        sends = []
        for k in range(1, N_DEV):
            cp = pltpu.make_async_remote_copy(src_ref=src_ref, dst_ref=out_ref.at[me], send_sem=send_sems.at[k - 1],
                                              recv_sem=recv_sems.at[k - 1], device_id=peer(k), device_id_type=MESH)
            cp.start()
            sends.append(cp)
        for k in range(1, N_DEV):
            px, py, pc = peer(k)
            pltpu.make_async_remote_copy(src_ref=src_ref, dst_ref=out_ref.at[4 * px + 2 * py + pc],
                                         send_sem=send_sems.at[k - 1], recv_sem=recv_sems.at[k - 1],
                                         device_id=peer(k), device_id_type=MESH).wait_recv()
        for cp in sends:
            cp.wait_send()
        mine.wait()

    any_spec = pl.BlockSpec(memory_space=pl.ANY)
    return pl.pallas_call(
        body, in_specs=[any_spec], out_specs=any_spec, out_shape=SDS((N_DEV,) + src.shape, src.dtype),
        scratch_shapes=[pltpu.SemaphoreType.DMA((N_DEV - 1,)), pltpu.SemaphoreType.DMA((N_DEV - 1,)), pltpu.SemaphoreType.DMA],
        name=name)(src)


def _adam_terms(w, g, m, v):
    m = ADAM_B1 * m + (1.0 - ADAM_B1) * g
    v = ADAM_B2 * v + (1.0 - ADAM_B2) * (g * g)
    m_hat = m / (1.0 - ADAM_B1 ** ADAM_STEP)
    v_hat = v / (1.0 - ADAM_B2 ** ADAM_STEP)
    delta = -ADAM_LR * (m_hat / (jnp.sqrt(v_hat) + ADAM_EPS) + ADAM_WD * w)
    return delta, m, v


def _adamw_shard(mine, other, w, m, v, *, name):
    d, a, b = w.shape
    tr = next((t for t in (128, 64, 32, 16) if a % t == 0), a)

    def body(*refs):
        ga, gb = refs[:d], refs[d:2 * d]
        w_ref, m_ref, v_ref, g_ref, d_ref, nm_ref, nv_ref = refs[2 * d:]

        def plane(ref):
            return ((ref[0].astype(F32) + ref[1].astype(F32)) + ref[2].astype(F32)) + ref[3].astype(F32)

        for lp in range(d):
            @pl.when(pl.program_id(0) == lp)
            def _(lp=lp):
                g = plane(ga[lp]) + plane(gb[lp])
                delta, mn, vn = _adam_terms(w_ref[...], g, m_ref[...], v_ref[...])
                g_ref[...] = g
                d_ref[...] = delta
                nm_ref[...] = mn
                nv_ref[...] = vn

    gspecs = [pl.BlockSpec((N_CHIPS, tr, b), lambda l, i, lp=lp: (0, jnp.where(l == lp, i, 0), 0)) for lp in range(d)]
    blk = pl.BlockSpec((None, tr, b), lambda l, i: (l, i, 0))
    shp = SDS((d, a, b), F32)
    return pl.pallas_call(
        body, grid=(d, a // tr), in_specs=gspecs + gspecs + [blk, blk, blk], out_specs=(blk,) * 4, out_shape=(shp,) * 4,
        name=name, compiler_params=_params(("arbitrary", "arbitrary"), 48 << 20))(*mine, *other, w, m, v)


def _adamw_small(g8, w, m, v, *, name):
    n = w.shape[1]

    def body(g8_ref, w_ref, m_ref, v_ref, g_ref, d_ref, nm_ref, nv_ref):
        g = g8_ref[0]
        for k in range(1, N_DEV):
            g = g + g8_ref[k]
        delta, mn, vn = _adam_terms(w_ref[...], g, m_ref[...], v_ref[...])
        g_ref[...] = g
        d_ref[...] = delta
        nm_ref[...] = mn
        nv_ref[...] = vn

    shp = SDS((1, n), F32)
    return pl.pallas_call(body, out_shape=(shp,) * 4, name=name, compiler_params=_params(None, 24 << 20))(g8, w, m, v)


def _rope_tables(positions):
    inv_freq = 1.0 / (ROPE_THETA ** (jnp.arange(0, QK_ROPE, 2, dtype=F32) / QK_ROPE))
    ang = positions.astype(F32)[:, None] * inv_freq
    c, s = jnp.cos(ang), jnp.sin(ang)
    n = positions.shape[0]
    pad = jnp.zeros((n, HEAD_PAD - QK_NOPE - QK_ROPE), F32)
    cos = jnp.concatenate([jnp.ones((n, QK_NOPE), F32), c, c, pad], axis=1)
    sins = jnp.concatenate([jnp.zeros((n, QK_NOPE), F32), -s, s, pad], axis=1)
    return cos, sins


def _pad_lanes(v):
    return _pad_cols(v.reshape(1, -1), LANES)


def _local_step(x, mem, positions, weights, small, final_norm, loss_target, emit, token):
    cos, sins = _rope_tables(positions)
    saved, gws, wls, sps = [], [], [], []
    h = x
    for li in range(DEPTH):
        sp = {k: small[k][li] for k, _ in SMALL}
        if li == 0:
            sp["norm_mix"] = sp["norm_mix"] + token[0, 0]
        for k in ("dt_bias", "a_log", "d_skip"):
            sp[k] = _pad_lanes(sp[k])
        h, sv, gw, wl = _layer_fwd(h, mem, cos, sins, functools.partial(weights, li), sp, li)
        saved.append(sv)
        gws.append(gw)
        wls.append(wl)
        sps.append(sp)
    loss, dh, dhb, g_final = _final_loss(h, final_norm, loss_target, name="final_loss")
    grads = [None] * DEPTH
    started = None
    for li in reversed(range(DEPTH)):
        sp = sps[li]
        if started is not None:
            sp = dict(sp, ffn_conv_b=sp["ffn_conv_b"] + started[0, 0])
        dh, dhb, grads[li], started = _layer_bwd(dh, dhb, mem, cos, sins, gws[li], wls[li], sp, saved[li], li,
                                                 functools.partial(emit, li))
    return loss, dh, grads, g_final


def _gathered_views(which, lands):
    return {MATS[t][0]: (b.reshape(-1, b.shape[-1]) if MATS[t][2] == 0 else b) for t, b in zip(which, lands)}


def kernel(x, mem, positions, norm_mix, w_in, ssm_conv_w, ssm_conv_b, dt_bias, a_log, d_skip, ssm_norm, q_norm, w_uq, kv_norm, w_ukv, attn_out_norm, w_out, norm_mem_q, norm_mem_kv, w_mq, w_mk, w_mv, w_mo, norm_ffn, w_up, ffn_conv_w, ffn_conv_b, w_down, final_norm, loss_target, m_norm_mix, m_w_in, m_ssm_conv_w, m_ssm_conv_b, m_dt_bias, m_a_log, m_d_skip, m_ssm_norm, m_q_norm, m_w_uq, m_kv_norm, m_w_ukv, m_attn_out_norm, m_w_out, m_norm_mem_q, m_norm_mem_kv, m_w_mq, m_w_mk, m_w_mv, m_w_mo, m_norm_ffn, m_w_up, m_ffn_conv_w, m_ffn_conv_b, m_w_down, m_final_norm, v_norm_mix, v_w_in, v_ssm_conv_w, v_ssm_conv_b, v_dt_bias, v_a_log, v_d_skip, v_ssm_norm, v_q_norm, v_w_uq, v_kv_norm, v_w_ukv, v_attn_out_norm, v_w_out, v_norm_mem_q, v_norm_mem_kv, v_w_mq, v_w_mk, v_w_mv, v_w_mo, v_norm_ffn, v_w_up, v_ffn_conv_w, v_ffn_conv_b, v_w_down, v_final_norm):
    args = dict(locals())
    names = ["norm_mix", "w_in", "ssm_conv_w", "ssm_conv_b", "dt_bias", "a_log", "d_skip", "ssm_norm", "q_norm", "w_uq",
             "kv_norm", "w_ukv", "attn_out_norm", "w_out", "norm_mem_q", "norm_mem_kv", "w_mq", "w_mk", "w_mv", "w_mo",
             "norm_ffn", "w_up", "ffn_conv_w", "ffn_conv_b", "w_down", "final_norm"]
    wts = {k: args[k] for k in names}
    mom = {k: args["m_" + k] for k in names}
    var = {k: args["v_" + k] for k in names}
    mat_names = [k for k, _, _ in MATS]

    shards = [wts[k] if k in F32_ON_WIRE else wts[k].astype(BF16) for k in mat_names]
    token = jnp.zeros(TOKEN_SHAPE, F32)
    first, token = _gather_layer_start(shards[:1], 0, token, tag="_first")
    gathers = []
    for li in range(DEPTH):
        handle, token = _gather_layer_start(shards[1:] if li == 0 else shards, li, token)
        gathers.append(handle)
    small = {k: wts[k] for k, _ in SMALL}

    def weights(li, group, after):
        which = GROUPS[group]
        if li > 0:
            return _gathered_views(which, _exchange_finish(gathers[li], after, which, tag=f"_{group}"))
        if group == "proj":
            return _gathered_views(which, _exchange_finish(first, after))
        return _gathered_views(which, _exchange_finish(gathers[0], after, tuple(t - 1 for t in which), tag=f"_{group}"))

    scatters = [[] for _ in range(DEPTH)]
    nm = len(mat_names)
    mine = [[None] * nm for _ in range(DEPTH)]
    swaps = [None] * DEPTH
    last_started = [None]

    def swap_layer(li, after):
        for which, handle in scatters[li]:
            for t, b in zip(which, _exchange_finish(handle, after)):
                mine[li][t] = b
        swaps[li], started = _swap_start(mine[li], jnp.zeros(TOKEN_SHAPE, F32), name=f"swap{li}")
        return started

    def emit(li, group, g):
        last = group == "proj"
        if li == 0:
            which = GROUPS[group]
        elif last:
            which = tuple(range(nm))
        else:
            return None
        handle, started = _scatter_start([g[MATS[t][0]] for t in which], f"{li}_{group}", jnp.zeros(TOKEN_SHAPE, F32))
        scatters[li].append((which, handle))
        last_started[0] = started
        if li + 1 < DEPTH and group == ("mixer" if li == 0 else "proj"):
            started = started + swap_layer(li + 1, g["ssm_conv_w"])
        return started

    loss, grad_x, grads, g_final = _local_step(x[0], mem[0], positions[0], weights, small, wts["final_norm"],
                                               loss_target[0], emit, token)
    loss = lax.psum(loss, ("x", "y", "c"))

    w_in_t = GROUPS["proj"][0]
    others = [t for t in range(nm) if t != w_in_t]
    for which, handle in scatters[0]:
        if which != GROUPS["proj"]:
            for t, b in zip(which, _exchange_finish(handle, grad_x)):
                mine[0][t] = b
    swaps[0], _ = _swap_start([mine[0][t] for t in others], last_started[0], name="swap0")
    other = [dict(zip(others, _swap_wait(swaps[0], grad_x)))] + [
        dict(enumerate(_swap_wait(swaps[li], grad_x))) for li in range(1, DEPTH)]

    def adamw(t):
        k = mat_names[t]
        return _adamw_shard([mine[li][t] for li in range(DEPTH)], [other[li][t] for li in range(DEPTH)],
                            wts[k], mom[k], var[k], name=f"adamw_{k}")

    mat_out = {mat_names[t]: adamw(t) for t in others}
    done = mat_out[mat_names[others[-1]]][0]
    (last_handle,) = [handle for which, handle in scatters[0] if which == GROUPS["proj"]]
    (mine[0][w_in_t],) = _exchange_finish(last_handle, done)
    last_swap, _ = _swap_start([mine[0][w_in_t]], jnp.zeros(TOKEN_SHAPE, F32), name="swap0_last")
    (other[0][w_in_t],) = _swap_wait(last_swap, done)
    mat_out[mat_names[w_in_t]] = adamw(w_in_t)

    def pack_small(get, fin):
        flat = [get(k).reshape(-1) for k, _ in SMALL] + [fin.reshape(-1)]
        n = sum(f.shape[0] for f in flat)
        return jnp.concatenate(flat + [jnp.zeros((-n % PACK_COLS,), F32)]).reshape(1, -1)

    gs = pack_small(lambda k: jnp.stack([grads[li][k] for li in range(DEPTH)]), g_final)
    g8 = _all_gather8(gs, name="gather_small_grads")
    small_out = _adamw_small(g8, pack_small(wts.get, wts["final_norm"]), pack_small(mom.get, mom["final_norm"]),
                             pack_small(var.get, var["final_norm"]), name="adamw_small")

    def unpack_small(buf):
        out, off = {}, 0
        for k, nel in SMALL:
            out[k] = buf[0, off:off + DEPTH * nel].reshape(DEPTH, nel)
            off += DEPTH * nel
        out["final_norm"] = buf[0, off:off + D_MODEL]
        return out

    small_res = [unpack_small(b) for b in small_out]
    res = []
    for kind in range(4):
        for k in names:
            res.append(small_res[kind][k] if k in small_res[kind] else mat_out[k][kind])
    return (loss, grad_x[None], *res)
```

```python
import functools
import math

import jax
import jax.numpy as jnp
from jax import lax
from jax.experimental import pallas as pl
from jax.experimental.pallas import tpu as pltpu

F32 = jnp.float32
BF16 = jnp.bfloat16
HIGHEST = lax.Precision.HIGHEST
SDS = jax.ShapeDtypeStruct
MESH = pl.DeviceIdType.MESH

D_MODEL = 1024
DEPTH = 4
EPS = 1e-6
SSM_HEADS = 16
SSM_HEAD_DIM = 64
D_SSM = 1024
SSM_GROUPS = 4
SSM_STATE = 128
SSM_CONV = 4
SSM_CHUNK = 128
CONV_CH = 2048
MLA_HEADS = 16
QK_NOPE = 64
QK_ROPE = 32
V_DIM = 64
Q_LORA = 384
KV_LORA = 256
ROPE_THETA = 10000.0
MEM_HEADS = 4
MEM_HEAD_DIM = 256
D_FF = 2816
FFN_CONV = 3
D_IN = 3760
ADAM_LR = 0.001
ADAM_B1 = 0.9
ADAM_B2 = 0.999
ADAM_EPS = 1e-08
ADAM_WD = 0.01
ADAM_STEP = 10

LANES = 128
HEAD_PAD = 128
N_CHIPS = 4
N_DEV = 8
VMEM_CAP_MB = 56

P_XBC, P_Z, P_CQ, P_DT, P_CKV, P_KR, P_IN = 0, 2048, 3072, 3456, 3584, 3840, 4096
NEG = -1e30


def _tile(n, pref):
    t = (min(n, pref) // LANES) * LANES
    while t >= LANES:
        if n % t == 0:
            return t
        t -= LANES
    return n


def _params(sem=None, vmem_bytes=None):
    kw = {}
    if sem is not None:
        kw["dimension_semantics"] = sem
    if vmem_bytes is not None:
        kw["vmem_limit_bytes"] = int(min(max(vmem_bytes, 16 << 20), VMEM_CAP_MB << 20))
    return pltpu.CompilerParams(**kw)


def _nbytes(shape, dtype):
    return math.prod(shape) * jnp.dtype(dtype).itemsize


def _mm(a, b, *, ta=False, tb=False, res=None, out_dtype=F32, name, a_col=None, b_lead=(), b_rows=None,
        b_chips=None, o_chips=None):
    if ta:
        k, m = a.shape
    else:
        m, k = (a.shape[0], a.shape[1] if a_col is None else a_col[0])
    rows_b, cols_b = b.shape[-2:]
    row0 = 0
    if b_rows is not None:
        row0, rows_b = b_rows
    nlead = len(b_lead)
    if b_chips is not None:
        assert not tb
        kb, tn, n = rows_b, cols_b, b_chips[1] * cols_b
        b_blk = (None,) * (1 + nlead) + (kb, tn)
        b_map = lambda i, j: (b_chips[0] + j,) + tuple(b_lead) + (0, 0)
    elif tb:
        n, kb = rows_b, cols_b
        tn = _tile(n, 512)
        assert row0 % tn == 0
        b_blk = (None,) * nlead + (tn, kb)
        b_map = lambda i, j: tuple(b_lead) + (j + row0 // tn, 0)
    else:
        kb, n = rows_b, cols_b
        tn = o_chips if o_chips else _tile(n, 512)
        assert row0 % kb == 0
        b_blk = (None,) * nlead + (kb, tn)
        b_map = lambda i, j: tuple(b_lead) + (row0 // kb, j)
    assert k == kb, (a.shape, b.shape, ta, tb, k, kb)
    tm = _tile(m, 512)
    if ta:
        a_blk, a_map = (k, tm), (lambda i, j: (0, i))
    else:
        a_blk, a_map = (tm, k), ((lambda i, j: (i, 0)) if a_col is None else (lambda i, j: (i, a_col[1])))
    if o_chips:
        o_spec = pl.BlockSpec((None, tm, tn), lambda i, j: (j, i, 0))
        o_shape = SDS((n // tn, m, tn), out_dtype)
    else:
        o_spec = pl.BlockSpec((tm, tn), lambda i, j: (i, j))
        o_shape = SDS((m, n), out_dtype)
    dims = (((0 if ta else 1,), (1 if tb else 0,)), ((), ()))
    has_res = res is not None

    def body(*refs):
        a_ref, b_ref = refs[0], refs[1]
        o_ref = refs[-1]
        acc = lax.dot_general(a_ref[...].astype(BF16), b_ref[...].astype(BF16), dims, preferred_element_type=F32)
        if has_res:
            acc = acc + refs[2][...]
        o_ref[...] = acc.astype(o_ref.dtype)

    bb = tuple(d for d in b_blk if d is not None)
    vmem = 2 * (_nbytes(a_blk, a.dtype) + _nbytes(bb, b.dtype) + (2 if has_res else 1) * _nbytes((tm, tn), F32))
    vmem += _nbytes(a_blk, BF16) + _nbytes(bb, BF16) + 2 * _nbytes((tm, tn), F32) + (4 << 20)
    args = (a, b) + ((res,) if has_res else ())
    specs = [pl.BlockSpec(a_blk, a_map), pl.BlockSpec(b_blk, b_map)] + ([o_spec] if has_res else [])
    return pl.pallas_call(body, grid=(m // tm, n // tn), in_specs=specs, out_specs=o_spec, out_shape=o_shape, name=name,
                          compiler_params=_params(("parallel", "parallel"), vmem))(*args)


def _sigmoid(x):
    return 1.0 / (1.0 + jnp.exp(-x))


def _rms_fwd(x, g, *, col=None, name):
    s = x.shape[0]
    w, ci = (x.shape[1], 0) if col is None else col
    tm = min(s, 512)

    def body(x_ref, g_ref, o_ref):
        xv = x_ref[...].astype(F32)
        r = lax.rsqrt(jnp.mean(xv * xv, axis=-1, keepdims=True) + EPS)
        o_ref[...] = (xv * r * g_ref[...]).astype(o_ref.dtype)

    return pl.pallas_call(
        body, grid=(s // tm,),
        in_specs=[pl.BlockSpec((tm, w), lambda i: (i, ci)), pl.BlockSpec((1, w), lambda i: (0, 0))],
        out_specs=pl.BlockSpec((tm, w), lambda i: (i, 0)), out_shape=SDS((s, w), BF16), name=name,
        compiler_params=_params(("parallel",), 10 * tm * w * 4))(x, g.reshape(1, w))


def _rms_bwd(x, g, dy, dres=None, *, col=None, name):
    s = x.shape[0]
    w, ci = (x.shape[1], 0) if col is None else col
    tm = min(s, 512)
    has_res = dres is not None

    def body(*refs):
        x_ref, g_ref, dy_ref = refs[:3]
        dx_ref, dxb_ref, dg_ref = refs[-3:]
        xv = x_ref[...].astype(F32)
        dyv = dy_ref[...].astype(F32)
        r = lax.rsqrt(jnp.mean(xv * xv, axis=-1, keepdims=True) + EPS)
        u = dyv * g_ref[...]
        dx = r * u - xv * (r * r * r) * jnp.mean(xv * u, axis=-1, keepdims=True)
        if has_res:
            dx = dx + refs[3][...]
        dx_ref[...] = dx
        dxb_ref[...] = dx.astype(BF16)

        @pl.when(pl.program_id(0) == 0)
        def _():
            dg_ref[...] = jnp.zeros_like(dg_ref)

        dg_ref[...] += jnp.sum(dyv * xv * r, axis=0, keepdims=True)

    blk = pl.BlockSpec((tm, w), lambda i: (i, 0))
    specs = [pl.BlockSpec((tm, w), lambda i: (i, ci)), pl.BlockSpec((1, w), lambda i: (0, 0)), blk]
    args = [x, g.reshape(1, w), dy]
    if has_res:
        specs.append(blk)
        args.append(dres)
    dx, dxb, dg = pl.pallas_call(
        body, grid=(s // tm,), in_specs=specs,
        out_specs=(blk, blk, pl.BlockSpec((1, w), lambda i: (0, 0))),
        out_shape=(SDS((s, w), F32), SDS((s, w), BF16), SDS((1, w), F32)), name=name,
        compiler_params=_params(("arbitrary",), 18 * tm * w * 4))(*args)
    return dx, dxb, dg.reshape(w)


def _gated_rms_fwd(y, proj, g, *, name):
    s, w = y.shape
    tm = min(s, 512)

    def body(y_ref, z_ref, g_ref, o_ref):
        z = z_ref[...]
        t = y_ref[...] * (z * _sigmoid(z))
        r = lax.rsqrt(jnp.mean(t * t, axis=-1, keepdims=True) + EPS)
        o_ref[...] = (t * r * g_ref[...]).astype(o_ref.dtype)

    blk = pl.BlockSpec((tm, w), lambda i: (i, 0))
    return pl.pallas_call(
        body, grid=(s // tm,),
        in_specs=[blk, pl.BlockSpec((tm, w), lambda i: (i, P_Z // w)), pl.BlockSpec((1, w), lambda i: (0, 0))],
        out_specs=blk, out_shape=SDS((s, w), BF16), name=name,
        compiler_params=_params(("parallel",), 14 * tm * w * 4))(y, proj, g.reshape(1, w))


def _gated_rms_bwd(y, proj, g, dout, *, name):
    s, w = y.shape
    tm = min(s, 512)

    def body(y_ref, z_ref, g_ref, do_ref, dy_ref, dz_ref, dg_ref):
        z = z_ref[...]
        yv = y_ref[...]
        dov = do_ref[...]
        sg = _sigmoid(z)
        sz = z * sg
        t = yv * sz
        r = lax.rsqrt(jnp.mean(t * t, axis=-1, keepdims=True) + EPS)
        u = dov * g_ref[...]
        dt = r * u - t * (r * r * r) * jnp.mean(t * u, axis=-1, keepdims=True)
        dy_ref[...] = dt * sz
        dz_ref[...] = (dt * yv * (sg * (1.0 + z * (1.0 - sg)))).astype(dz_ref.dtype)

        @pl.when(pl.program_id(0) == 0)
        def _():
            dg_ref[...] = jnp.zeros_like(dg_ref)

        dg_ref[...] += jnp.sum(dov * t * r, axis=0, keepdims=True)

    blk = pl.BlockSpec((tm, w), lambda i: (i, 0))
    vec = pl.BlockSpec((1, w), lambda i: (0, 0))
    dy, dz, dg = pl.pallas_call(
        body, grid=(s // tm,),
        in_specs=[blk, pl.BlockSpec((tm, w), lambda i: (i, P_Z // w)), vec, blk],
        out_specs=(blk, blk, vec), out_shape=(SDS((s, w), F32), SDS((s, w), BF16), SDS((1, w), F32)), name=name,
        compiler_params=_params(("arbitrary",), 24 * tm * w * 4))(y, proj, g.reshape(1, w), dout)
    return dy, dz, dg.reshape(w)


def _final_loss(x, g, target, *, name):
    s, w = x.shape
    tm = min(s, 512)

    def body(x_ref, g_ref, t_ref, loss_ref, dx_ref, dxb_ref, dg_ref):
        xv = x_ref[...]
        gv = g_ref[...]
        r = lax.rsqrt(jnp.mean(xv * xv, axis=-1, keepdims=True) + EPS)
        xn = xv * r
        diff = xn * gv - t_ref[...]
        dy = diff * (1.0 / w)
        u = dy * gv
        dx = r * u - xv * (r * r * r) * jnp.mean(xv * u, axis=-1, keepdims=True)
        dx_ref[...] = dx
        dxb_ref[...] = dx.astype(BF16)

        @pl.when(pl.program_id(0) == 0)
        def _():
            dg_ref[...] = jnp.zeros_like(dg_ref)
            loss_ref[...] = jnp.zeros_like(loss_ref)

        dg_ref[...] += jnp.sum(dy * xn, axis=0, keepdims=True)
        part = jnp.sum(jnp.sum(diff * diff, axis=1, keepdims=True), axis=0, keepdims=True) * (0.5 / w)
        loss_ref[...] += jnp.broadcast_to(part, loss_ref.shape)

    blk = pl.BlockSpec((tm, w), lambda i: (i, 0))
    vec = pl.BlockSpec((1, w), lambda i: (0, 0))
    loss, dx, dxb, dg = pl.pallas_call(
        body, grid=(s // tm,), in_specs=[blk, vec, blk],
        out_specs=(pl.BlockSpec((1, LANES), lambda i: (0, 0)), blk, blk, vec),
        out_shape=(SDS((1, LANES), F32), SDS((s, w), F32), SDS((s, w), BF16), SDS((1, w), F32)), name=name,
        compiler_params=_params(("arbitrary",), 18 * tm * w * 4))(x, g.reshape(1, w), target)
    return loss[0, 0], dx, dxb, dg.reshape(w)


def _shift_down(x, k):
    if k == 0:
        return x
    row = lax.broadcasted_iota(jnp.int32, x.shape, 0)
    return jnp.where(row < k, 0.0, pltpu.roll(x, k, axis=0))


def _shift_up(x, k):
    if k == 0:
        return x
    s = x.shape[0]
    row = lax.broadcasted_iota(jnp.int32, x.shape, 0)
    return jnp.where(row >= s - k, 0.0, pltpu.roll(x, s - k, axis=0))


def _conv_pre(x, w, b, kw):
    pre = b
    for j in range(kw):
        pre = pre + w[j:j + 1, :] * _shift_down(x, kw - 1 - j)
    return pre


def _conv_bwd_terms(x, w, dpre, kw):
    dx = jnp.zeros_like(x)
    dws = []
    for j in range(kw):
        dx = dx + w[j:j + 1, :] * _shift_up(dpre, kw - 1 - j)
        dws.append(jnp.sum(dpre * _shift_down(x, kw - 1 - j), axis=0, keepdims=True))
    return dx, jnp.concatenate(dws, axis=0), jnp.sum(dpre, axis=0, keepdims=True)


def _ssm_conv_fwd(proj, w, b, *, name):
    s = proj.shape[0]
    cw = 256

    def body(x_ref, w_ref, b_ref, o_ref):
        pre = _conv_pre(x_ref[...], w_ref[...], b_ref[...], SSM_CONV)
        o_ref[...] = pre * _sigmoid(pre)

    return pl.pallas_call(
        body, grid=(CONV_CH // cw,),
        in_specs=[pl.BlockSpec((s, cw), lambda j: (0, j)), pl.BlockSpec((SSM_CONV, cw), lambda j: (0, j)),
                  pl.BlockSpec((1, cw), lambda j: (0, j))],
        out_specs=pl.BlockSpec((s, cw), lambda j: (0, j)), out_shape=SDS((s, CONV_CH), F32), name=name,
        compiler_params=_params(("parallel",), 12 * s * cw * 4))(proj, w, b.reshape(1, CONV_CH))


def _ssm_conv_bwd(proj, w, b, dxbc, *, name):
    s = proj.shape[0]
    cw = 256

    def body(x_ref, w_ref, b_ref, dy_ref, dx_ref, dw_ref, db_ref):
        x = x_ref[...]
        wv = w_ref[...]
        pre = _conv_pre(x, wv, b_ref[...], SSM_CONV)
        sg = _sigmoid(pre)
        dpre = dy_ref[...] * (sg * (1.0 + pre * (1.0 - sg)))
        dx, dw, db = _conv_bwd_terms(x, wv, dpre, SSM_CONV)
        dx_ref[...] = dx.astype(dx_ref.dtype)
        dw_ref[...] = dw
        db_ref[...] = db

    col = pl.BlockSpec((s, cw), lambda j: (0, j))
    wsp = pl.BlockSpec((SSM_CONV, cw), lambda j: (0, j))
    bsp = pl.BlockSpec((1, cw), lambda j: (0, j))
    dx, dw, db = pl.pallas_call(
        body, grid=(CONV_CH // cw,), in_specs=[col, wsp, bsp, col], out_specs=(col, wsp, bsp),
        out_shape=(SDS((s, CONV_CH), BF16), SDS((SSM_CONV, CONV_CH), F32), SDS((1, CONV_CH), F32)), name=name,
        compiler_params=_params(("parallel",), 20 * s * cw * 4))(proj, w, b.reshape(1, CONV_CH), dxbc)
    return dx, dw, db.reshape(CONV_CH)


def _ffn_conv_fwd(up_g, up_v, w, b, *, name):
    s = up_g.shape[0]
    cw = 256
    nb = D_FF // cw

    def body(g_ref, v_ref, wg_ref, wv_ref, bg_ref, bv_ref, o_ref):
        gate = _conv_pre(g_ref[...], wg_ref[...], bg_ref[...], FFN_CONV)
        val = _conv_pre(v_ref[...], wv_ref[...], bv_ref[...], FFN_CONV)
        o_ref[...] = (gate * _sigmoid(gate) * val).astype(o_ref.dtype)

    col = pl.BlockSpec((s, cw), lambda j: (0, j))
    b2 = b.reshape(1, 2 * D_FF)
    return pl.pallas_call(
        body, grid=(nb,),
        in_specs=[col, col, pl.BlockSpec((FFN_CONV, cw), lambda j: (0, j)), pl.BlockSpec((FFN_CONV, cw), lambda j: (0, j + nb)),
                  pl.BlockSpec((1, cw), lambda j: (0, j)), pl.BlockSpec((1, cw), lambda j: (0, j + nb))],
        out_specs=col, out_shape=SDS((s, D_FF), BF16), name=name,
        compiler_params=_params(("parallel",), 16 * s * cw * 4))(up_g, up_v, w, w, b2, b2)


def _ffn_conv_bwd(up_g, up_v, w, b, dact, *, name):
    s = up_g.shape[0]
    cw = 256
    nb = D_FF // cw

    def body(g_ref, v_ref, wg_ref, wv_ref, bg_ref, bv_ref, da_ref, dg_ref, dv_ref, dwg_ref, dwv_ref, dbg_ref, dbv_ref):
        xg, xv = g_ref[...], v_ref[...]
        wg, wv = wg_ref[...], wv_ref[...]
        gate = _conv_pre(xg, wg, bg_ref[...], FFN_CONV)
        val = _conv_pre(xv, wv, bv_ref[...], FFN_CONV)
        da = da_ref[...].astype(F32)
        sg = _sigmoid(gate)
        dgate = da * val * (sg * (1.0 + gate * (1.0 - sg)))
        dval = da * gate * sg
        dxg, dwg, dbg = _conv_bwd_terms(xg, wg, dgate, FFN_CONV)
        dxv, dwv, dbv = _conv_bwd_terms(xv, wv, dval, FFN_CONV)
        dg_ref[...] = dxg.astype(dg_ref.dtype)
        dv_ref[...] = dxv.astype(dv_ref.dtype)
        dwg_ref[...] = dwg
        dwv_ref[...] = dwv
        dbg_ref[...] = dbg
        dbv_ref[...] = dbv

    col = pl.BlockSpec((s, cw), lambda j: (0, j))
    wsp = pl.BlockSpec((FFN_CONV, cw), lambda j: (0, j))
    bsp = pl.BlockSpec((1, cw), lambda j: (0, j))
    b2 = b.reshape(1, 2 * D_FF)
    dg, dv, dwg, dwv, dbg, dbv = pl.pallas_call(
        body, grid=(nb,),
        in_specs=[col, col, wsp, pl.BlockSpec((FFN_CONV, cw), lambda j: (0, j + nb)), bsp,
                  pl.BlockSpec((1, cw), lambda j: (0, j + nb)), col],
        out_specs=(col, col, wsp, wsp, bsp, bsp),
        out_shape=(SDS((s, D_FF), BF16), SDS((s, D_FF), BF16), SDS((FFN_CONV, D_FF), F32), SDS((FFN_CONV, D_FF), F32),
                   SDS((1, D_FF), F32), SDS((1, D_FF), F32)), name=name,
        compiler_params=_params(("parallel",), 32 * s * cw * 4))(up_g, up_v, w, w, b2, b2, dact)
    return dg, dv, jnp.concatenate([dwg, dwv], axis=1), jnp.concatenate([dbg, dbv], axis=1).reshape(2 * D_FF)


def _dot(a, b):
    return jnp.dot(a.astype(BF16), b.astype(BF16), preferred_element_type=F32)


def _dot_nt(a, b):
    return lax.dot_general(a.astype(BF16), b.astype(BF16), (((1,), (1,)), ((), ())), preferred_element_type=F32)


def _dot_tn(a, b):
    return lax.dot_general(a.astype(BF16), b.astype(BF16), (((0,), (0,)), ((), ())), preferred_element_type=F32)


def _ssd_chunk_terms(dtraw, bias, a_log):
    ell = dtraw.shape[0]
    lane = lax.broadcasted_iota(jnp.int32, dtraw.shape, 1)
    valid = lane < SSM_HEADS
    pre = dtraw + bias
    dt = jnp.where(valid, jnp.where(pre > 20.0, pre, jnp.log(1.0 + jnp.exp(jnp.minimum(pre, 20.0)))), 0.0)
    a = -jnp.exp(a_log)
    ad = dt * a
    row = lax.broadcasted_iota(jnp.int32, (ell, ell), 0)
    colm = lax.broadcasted_iota(jnp.int32, (ell, ell), 1)
    tril = row >= colm
    cs = jnp.dot(tril.astype(F32), ad, precision=HIGHEST, preferred_element_type=F32)
    cs_last = cs[ell - 1:ell, :]
    return pre, dt, a, cs, cs_last, tril


def _head_expand():
    h = lax.broadcasted_iota(jnp.int32, (LANES, D_SSM), 0)
    c = lax.broadcasted_iota(jnp.int32, (LANES, D_SSM), 1)
    return (c // SSM_HEAD_DIM == h).astype(F32)


def _ssd_fwd(xbc, proj, dt_bias, a_log, d_skip, *, name):
    s = xbc.shape[0]
    nc = s // SSM_CHUNK
    ell, n, p = SSM_CHUNK, SSM_STATE, SSM_HEAD_DIM
    rpg = SSM_HEADS // SSM_GROUPS
    gw = rpg * p

    def body(x_ref, dt_ref, bias_ref, alog_ref, dskip_ref, ex_ref, y_ref, ps_ref, state):
        @pl.when(pl.program_id(0) == 0)
        def _():
            state[...] = jnp.zeros_like(state)

        _, dt, _, cs, cs_last, tril = _ssd_chunk_terms(dt_ref[...], bias_ref[...], alog_ref[...])
        cst = cs.T
        ex = ex_ref[...]
        spread = lambda v: jnp.dot(v, ex, precision=HIGHEST, preferred_element_type=F32)
        dt_x, e_x, ds_x = spread(dt), spread(jnp.exp(cs)), spread(jnp.exp(cs_last - cs))
        cd_x = spread(jnp.broadcast_to(jnp.exp(cs_last), (8, LANES)))[0:1, :]
        dskip_x = spread(jnp.broadcast_to(dskip_ref[...], (8, LANES)))[0:1, :]
        st = state[...]
        ps_ref[0] = st
        xv = x_ref[...]
        xs_all = xv[:, 0:D_SSM]
        xd_all = xs_all * dt_x
        xdd_all = xd_all * ds_x
        lane_g = lax.broadcasted_iota(jnp.int32, (ell, gw), 1)
        ys, new = [], []
        for g in range(SSM_GROUPS):
            gs = slice(gw * g, gw * (g + 1))
            bg = xv[:, D_SSM + n * g:D_SSM + n * (g + 1)]
            cg = xv[:, D_SSM + n * (SSM_GROUPS + g):D_SSM + n * (SSM_GROUPS + g + 1)]
            cb = _dot_nt(cg, bg)
            xd_g, prev_g = xd_all[:, gs], st[:, gs]
            y_g = _dot(cg, prev_g) * e_x[:, gs] + xs_all[:, gs] * dskip_x[:, gs]
            for r in range(rpg):
                h = g * rpg + r
                lmat = jnp.exp(jnp.where(tril, cs[:, h:h + 1] - cst[h:h + 1, :], -jnp.inf))
                y_g = y_g + jnp.where((lane_g >= p * r) & (lane_g < p * (r + 1)), _dot(cb * lmat, xd_g), 0.0)
            ys.append(y_g)
            new.append(prev_g * cd_x[:, gs] + _dot(bg.T, xdd_all[:, gs]))
        y_ref[...] = jnp.concatenate(ys, axis=1)
        state[...] = jnp.concatenate(new, axis=1)

    vec = pl.BlockSpec((1, LANES), lambda c: (0, 0))
    return pl.pallas_call(
        body, grid=(nc,),
        in_specs=[pl.BlockSpec((ell, CONV_CH), lambda c: (c, 0)), pl.BlockSpec((ell, LANES), lambda c: (c, P_DT // LANES)),
                  vec, vec, vec, pl.BlockSpec((LANES, D_SSM), lambda c: (0, 0))],
        out_specs=(pl.BlockSpec((ell, D_SSM), lambda c: (c, 0)), pl.BlockSpec((1, n, D_SSM), lambda c: (c, 0, 0))),
        out_shape=(SDS((s, D_SSM), F32), SDS((nc, n, D_SSM), F32)),
        scratch_shapes=[pltpu.VMEM((n, D_SSM), F32)], name=name,
        compiler_params=_params(("arbitrary",), 32 << 20))(xbc, proj, dt_bias, a_log, d_skip, _head_expand())


def _ssd_bwd(xbc, proj, dt_bias, a_log, d_skip, prev_states, dy, *, name):
    s = xbc.shape[0]
    nc = s // SSM_CHUNK
    ell, n, p = SSM_CHUNK, SSM_STATE, SSM_HEAD_DIM
    rpg = SSM_HEADS // SSM_GROUPS
    gw = rpg * p

    def body(x_ref, dt_ref, bias_ref, alog_ref, dskip_ref, ps_ref, dy_ref, ex_ref, ext_ref,
             dx_ref, ddt_ref, dalog_ref, ddskip_ref, dbias_ref, dstate):
        @pl.when(pl.program_id(0) == 0)
        def _():
            dstate[...] = jnp.zeros_like(dstate)
            dalog_ref[...] = jnp.zeros_like(dalog_ref)
            ddskip_ref[...] = jnp.zeros_like(ddskip_ref)
            dbias_ref[...] = jnp.zeros_like(dbias_ref)

        pre, dt, a, cs, cs_last, tril = _ssd_chunk_terms(dt_ref[...], bias_ref[...], alog_ref[...])
        e = jnp.exp(cs)
        ds = jnp.exp(cs_last - cs)
        cd = jnp.exp(cs_last)
        cst = cs.T
        shape = (ell, LANES)
        ex, ext = ex_ref[...], ext_ref[...]
        spread = lambda v: jnp.dot(v, ex, precision=HIGHEST, preferred_element_type=F32)
        gather = lambda v: jnp.dot(v, ext, precision=HIGHEST, preferred_element_type=F32)
        dt_x, e_x, ds_x = spread(dt), spread(e), spread(ds)
        cd_x = spread(jnp.broadcast_to(cd, (8, LANES)))[0:1, :]
        dskip_x = spread(jnp.broadcast_to(dskip_ref[...], (8, LANES)))[0:1, :]
        xv, dyv, psv, dst = x_ref[...], dy_ref[...], ps_ref[0], dstate[...]
        xs_all = xv[:, 0:D_SSM]
        xd_all = xs_all * dt_x
        dye_all = dyv * e_x
        xdd_all = xd_all * ds_x
        triu = lax.broadcasted_iota(jnp.int32, (ell, ell), 0) <= lax.broadcasted_iota(jnp.int32, (ell, ell), 1)
        lane_g = lax.broadcasted_iota(jnp.int32, (ell, gw), 1)
        lane = lax.broadcasted_iota(jnp.int32, shape, 1)
        sub = lax.broadcasted_iota(jnp.int32, shape, 0)
        dcs_acc = jnp.zeros(shape, F32)
        dcs_rows = jnp.zeros(shape, F32)
        dxs, dbs, dcs_parts, dprevs, prod_a, prod_b, prod_c, prod_e = [], [], [], [], [], [], [], []
        for g in range(SSM_GROUPS):
            gs = slice(gw * g, gw * (g + 1))
            bg = xv[:, D_SSM + n * g:D_SSM + n * (g + 1)]
            cg = xv[:, D_SSM + n * (SSM_GROUPS + g):D_SSM + n * (SSM_GROUPS + g + 1)]
            cb = _dot_nt(cg, bg)
            cbt = _dot_nt(bg, cg)
            xs_g, dy_g, xd_g, dye_g, xdd_g = xs_all[:, gs], dyv[:, gs], xd_all[:, gs], dye_all[:, gs], xdd_all[:, gs]
            prev_g, dsn_g = psv[:, gs], dst[:, gs]
            cprev_g = _dot(cg, prev_g)
            dprevs.append(dsn_g * cd_x[:, gs] + _dot(cg.T, dye_g))
            dcg = _dot_nt(dye_g, prev_g)
            dxdd_g = _dot(bg, dsn_g)
            dbg = _dot_nt(xdd_g, dsn_g)
            dxd_g = dxdd_g * ds_x[:, gs]
            prod_a.append(dy_g * cprev_g)
            prod_b.append(dxdd_g * xd_g)
            prod_e.append(jnp.sum(dsn_g * prev_g, axis=0, keepdims=True))
            dcb = jnp.zeros((ell, ell), F32)
            for r in range(rpg):
                h = g * rpg + r
                mine = (lane_g >= p * r) & (lane_g < p * (r + 1))
                lmat = jnp.exp(jnp.where(tril, cs[:, h:h + 1] - cst[h:h + 1, :], -jnp.inf))
                lmat_t = jnp.exp(jnp.where(triu, cst[h:h + 1, :] - cs[:, h:h + 1], -jnp.inf))
                dgm = _dot_nt(jnp.where(mine, dy_g, 0.0), xd_g)
                dxd_g = dxd_g + jnp.where(mine, _dot(cbt * lmat_t, dy_g), 0.0)
                mm = dgm * (cb * lmat)
                dcs_acc = dcs_acc + jnp.where(lane == h, jnp.sum(mm, axis=1, keepdims=True), 0.0)
                dcs_rows = dcs_rows + jnp.where(sub == h, jnp.sum(mm, axis=0, keepdims=True), 0.0)
                dcb = dcb + dgm * lmat
            dxs.append(dxd_g * dt_x[:, gs] + dy_g * dskip_x[:, gs])
            prod_c.append(dxd_g * xs_g)
            dbs.append(dbg + _dot_tn(dcb, cg))
            dcs_parts.append(dcg + _dot(dcb, bg))
        dx_ref[...] = jnp.concatenate(dxs + dbs + dcs_parts, axis=1)
        dstate[...] = jnp.concatenate(dprevs, axis=1)
        sum_a = gather(jnp.concatenate(prod_a, axis=1))
        sum_b = gather(jnp.concatenate(prod_b, axis=1))
        sum_c = gather(jnp.concatenate(prod_c, axis=1))
        sum_d = gather(dyv * xs_all)
        dcd = gather(jnp.broadcast_to(jnp.concatenate(prod_e, axis=1), (8, D_SSM)))[0:1, :]
        tmp = sum_b * ds
        dlast = dcd * cd + jnp.sum(tmp, axis=0, keepdims=True)
        dcs = dcs_acc + sum_a * e - tmp - dcs_rows.T + jnp.where(sub == ell - 1, dlast, 0.0)
        dad = jnp.dot(triu.astype(F32), dcs, precision=HIGHEST, preferred_element_type=F32)
        ddt = sum_c + dad * a
        dalog_ref[...] += jnp.sum(dad * dt, axis=0, keepdims=True) * a
        ddskip_ref[...] += jnp.sum(sum_d, axis=0, keepdims=True)
        ddraw = jnp.where(lane < SSM_HEADS, ddt * _sigmoid(pre), 0.0)
        ddt_ref[...] = ddraw.astype(ddt_ref.dtype)
        dbias_ref[...] += jnp.sum(ddraw, axis=0, keepdims=True)

    vec = pl.BlockSpec((1, LANES), lambda c: (0, 0))
    rev = lambda c: nc - 1 - c
    ex = _head_expand()
    outs = pl.pallas_call(
        body, grid=(nc,),
        in_specs=[pl.BlockSpec((ell, CONV_CH), lambda c: (rev(c), 0)),
                  pl.BlockSpec((ell, LANES), lambda c: (rev(c), P_DT // LANES)), vec, vec, vec,
                  pl.BlockSpec((1, n, D_SSM), lambda c: (rev(c), 0, 0)),
                  pl.BlockSpec((ell, D_SSM), lambda c: (rev(c), 0)),
                  pl.BlockSpec((LANES, D_SSM), lambda c: (0, 0)), pl.BlockSpec((D_SSM, LANES), lambda c: (0, 0))],
        out_specs=(pl.BlockSpec((ell, CONV_CH), lambda c: (rev(c), 0)), pl.BlockSpec((ell, LANES), lambda c: (rev(c), 0)),
                   vec, vec, vec),
        out_shape=(SDS((s, CONV_CH), F32), SDS((s, LANES), BF16), SDS((1, LANES), F32), SDS((1, LANES), F32),
                   SDS((1, LANES), F32)),
        scratch_shapes=[pltpu.VMEM((n, D_SSM), F32)], name=name,
        compiler_params=_params(("arbitrary",), 40 << 20))(xbc, proj, dt_bias, a_log, d_skip, prev_states, dy, ex, ex.T)
    return outs


def _rope_swap(t):
    lane = lax.broadcasted_iota(jnp.int32, t.shape, 1)
    half = QK_ROPE // 2
    lo = (lane >= QK_NOPE) & (lane < QK_NOPE + half)
    hi = (lane >= QK_NOPE + half) & (lane < QK_NOPE + QK_ROPE)
    return jnp.where(lo, pltpu.roll(t, HEAD_PAD - half, axis=1), jnp.where(hi, pltpu.roll(t, half, axis=1), 0.0))


def _mla_prep(q, kv, proj, cos, sins, *, name):
    s = q.shape[0]
    tm = min(s, 256)
    scale = (QK_NOPE + QK_ROPE) ** -0.5

    def body(q_ref, kv_ref, kr_ref, cos_ref, sin_ref, qo_ref, ko_ref, vo_ref):
        cosv, sinv = cos_ref[...], sin_ref[...]
        kr = pltpu.roll(kr_ref[...], QK_NOPE, axis=1)
        lane = lax.broadcasted_iota(jnp.int32, kr.shape, 1)
        nope = lane < QK_NOPE
        kr = jnp.where(nope, 0.0, kr)
        kpe = kr * cosv + _rope_swap(kr) * sinv
        for hp in range(MLA_HEADS // 2):
            vs = []
            for h in (2 * hp, 2 * hp + 1):
                hs = slice(HEAD_PAD * h, HEAD_PAD * (h + 1))
                qh = q_ref[:, hs]
                kvh = kv_ref[:, hs]
                qo_ref[:, hs] = ((qh * cosv + _rope_swap(qh) * sinv) * scale).astype(qo_ref.dtype)
                ko_ref[:, hs] = (jnp.where(nope, kvh, 0.0) + kpe).astype(ko_ref.dtype)
                vs.append(kvh[:, QK_NOPE:])
            vo_ref[:, 2 * V_DIM * hp:2 * V_DIM * (hp + 1)] = jnp.concatenate(vs, axis=1).astype(vo_ref.dtype)

    wide = pl.BlockSpec((tm, MLA_HEADS * HEAD_PAD), lambda i: (i, 0))
    half = pl.BlockSpec((tm, MLA_HEADS * V_DIM), lambda i: (i, 0))
    tab = pl.BlockSpec((tm, LANES), lambda i: (i, 0))
    return pl.pallas_call(
        body, grid=(s // tm,),
        in_specs=[wide, wide, pl.BlockSpec((tm, LANES), lambda i: (i, P_KR // LANES)), tab, tab],
        out_specs=(wide, wide, half),
        out_shape=(SDS((s, MLA_HEADS * HEAD_PAD), BF16), SDS((s, MLA_HEADS * HEAD_PAD), BF16),
                   SDS((s, MLA_HEADS * V_DIM), BF16)), name=name,
        compiler_params=_params(("parallel",), 32 << 20))(q, kv, proj, cos, sins)


def _mla_prep_bwd(dqr, dkr, dv, cos, sins, *, name):
    s = dqr.shape[0]
    tm = min(s, 256)
    scale = (QK_NOPE + QK_ROPE) ** -0.5

    def body(dq_ref, dk_ref, dv_ref, cos_ref, sin_ref, dqo_ref, dkv_ref, dkr_ref):
        cosv, sinv = cos_ref[...], sin_ref[...]
        lane = lax.broadcasted_iota(jnp.int32, cosv.shape, 1)
        ksum = jnp.zeros(cosv.shape, F32)
        for h in range(MLA_HEADS):
            hs = slice(HEAD_PAD * h, HEAD_PAD * (h + 1))
            d = dq_ref[:, hs]
            dk = dk_ref[:, hs]
            dqo_ref[:, hs] = ((d * cosv + _rope_swap(d * sinv)) * scale).astype(dqo_ref.dtype)
            dkv_ref[:, hs] = jnp.concatenate([dk[:, :QK_NOPE], dv_ref[:, V_DIM * h:V_DIM * (h + 1)]], axis=1).astype(dkv_ref.dtype)
            ksum = ksum + dk
        ksum = jnp.where((lane >= QK_NOPE) & (lane < QK_NOPE + QK_ROPE), ksum, 0.0)
        un = ksum * cosv + _rope_swap(ksum * sinv)
        dkr_ref[...] = pltpu.roll(un, HEAD_PAD - QK_NOPE, axis=1).astype(dkr_ref.dtype)

    wide = pl.BlockSpec((tm, MLA_HEADS * HEAD_PAD), lambda i: (i, 0))
    half = pl.BlockSpec((tm, MLA_HEADS * V_DIM), lambda i: (i, 0))
    tab = pl.BlockSpec((tm, LANES), lambda i: (i, 0))
    return pl.pallas_call(
        body, grid=(s // tm,), in_specs=[wide, wide, half, tab, tab], out_specs=(wide, wide, tab),
        out_shape=(SDS((s, MLA_HEADS * HEAD_PAD), BF16), SDS((s, MLA_HEADS * HEAD_PAD), BF16), SDS((s, LANES), BF16)),
        name=name, compiler_params=_params(("parallel",), 40 << 20))(dqr, dkr, dv, cos, sins)


FLASH_TILE = 512
FLASH_ROWS = 32


def _flash_fwd(q, k, v, *, name):
    s = q.shape[0]
    t = min(s, FLASH_TILE)
    nq = s // t
    npair = MLA_HEADS // 2

    def body(q_ref, k_ref, v_ref, o_ref, lse_ref):
        i = pl.program_id(1)
        qs = [q_ref[:, HEAD_PAD * e:HEAD_PAD * (e + 1)] for e in range(2)]
        diag = lax.broadcasted_iota(jnp.int32, (t, t), 0) >= lax.broadcasted_iota(jnp.int32, (t, t), 1)

        def step(j, carry, masked):
            rows = pl.ds(pl.multiple_of(j * t, t), t)
            new = []
            for e in range(2):
                m, l, acc = carry[e]
                sc = _dot_nt(qs[e], k_ref[rows, HEAD_PAD * e:HEAD_PAD * (e + 1)])
                if masked:
                    sc = jnp.where(diag, sc, NEG)
                m_new = jnp.maximum(m, jnp.max(sc, axis=1, keepdims=True))
                pr = jnp.exp(sc - m_new)
                alpha = jnp.exp(m - m_new)
                l = alpha * l + jnp.sum(pr, axis=1, keepdims=True)
                acc = alpha * acc + _dot(pr, v_ref[rows, V_DIM * e:V_DIM * (e + 1)])
                new.append((m_new, l, acc))
            return tuple(new)

        init = tuple((jnp.full((t, 1), NEG, F32), jnp.zeros((t, 1), F32), jnp.zeros((t, V_DIM), F32)) for _ in range(2))
        carry = lax.fori_loop(0, i, functools.partial(step, masked=False), init)
        carry = step(i, carry, True)
        o_ref[...] = jnp.concatenate([acc / l for _, l, acc in carry], axis=1)
        lse_ref[0] = jnp.concatenate([jnp.broadcast_to(m + jnp.log(l), (t, V_DIM)) for m, l, _ in carry], axis=1)

    return pl.pallas_call(
        body, grid=(npair, nq),
        in_specs=[pl.BlockSpec((t, 2 * HEAD_PAD), lambda hp, i: (i, hp)), pl.BlockSpec((s, 2 * HEAD_PAD), lambda hp, i: (0, hp)),
                  pl.BlockSpec((s, 2 * V_DIM), lambda hp, i: (0, hp))],
        out_specs=(pl.BlockSpec((t, 2 * V_DIM), lambda hp, i: (i, hp)), pl.BlockSpec((1, t, LANES), lambda hp, i: (hp, i, 0))),
        out_shape=(SDS((s, MLA_HEADS * V_DIM), F32), SDS((npair, s, LANES), F32)), name=name,
        compiler_params=_params(("parallel", "parallel"), 40 << 20))(q, k, v)


def _flash_bwd(q, k, v, o, lse, do, *, name):
    s = q.shape[0]
    t = min(s, FLASH_TILE)
    nq = s // t
    npair = MLA_HEADS // 2
    nchunk = t // FLASH_ROWS

    def valid_cols(r):
        return min(t, -(-((r + 1) * FLASH_ROWS) // LANES) * LANES)

    def body(q_ref, k_ref, v_ref, o_ref, lse_ref, do_ref, dq_ref, dk_ref, dv_ref, s_scr, dp_scr, p_scr, ds_scr, dk_acc, dv_acc):
        j = pl.program_id(1)

        @pl.when(j == 0)
        def _():
            dq_ref[...] = jnp.zeros_like(dq_ref)

        dk_acc[...] = jnp.zeros(dk_acc.shape, F32)
        dv_acc[...] = jnp.zeros(dv_acc.shape, F32)
        qsl = [slice(HEAD_PAD * e, HEAD_PAD * (e + 1)) for e in range(2)]
        vsl = [slice(V_DIM * e, V_DIM * (e + 1)) for e in range(2)]

        def step(i, carry, masked):
            rows = pl.ds(pl.multiple_of(i * t, t), t)
            for e in range(2):
                ke = k_ref[:, qsl[e]]
                qi = q_ref[rows, qsl[e]]
                doi = do_ref[rows, vsl[e]]
                delta = jnp.sum(doi * o_ref[rows, vsl[e]], axis=1, keepdims=True)
                lse_i = lse_ref[0, rows, vsl[e]][:, 0:1]
                dob = doi.astype(BF16)
                s_scr[e] = _dot_nt(qi, ke)
                dp_scr[e] = _dot_nt(dob, v_ref[:, vsl[e]])
                for r in range(nchunk):
                    rs = slice(r * FLASH_ROWS, (r + 1) * FLASH_ROWS)
                    width = valid_cols(r) if masked else t
                    sc = s_scr[e, rs, 0:width]
                    if masked:
                        row = r * FLASH_ROWS + lax.broadcasted_iota(jnp.int32, (FLASH_ROWS, width), 0)
                        sc = jnp.where(row >= lax.broadcasted_iota(jnp.int32, (FLASH_ROWS, width), 1), sc, NEG)
                    pr = jnp.exp(sc - lse_i[rs, :])
                    dsc = pr * (dp_scr[e, rs, 0:width] - delta[rs, :])
                    p_scr[e, rs, 0:width] = pr.astype(BF16)
                    ds_scr[e, rs, 0:width] = dsc.astype(BF16)
                    if width < t:
                        p_scr[e, rs, width:t] = jnp.zeros((FLASH_ROWS, t - width), BF16)
                        ds_scr[e, rs, width:t] = jnp.zeros((FLASH_ROWS, t - width), BF16)
                dv_acc[e] += _dot_tn(p_scr[e], dob)
                dk_acc[e] += _dot_tn(ds_scr[e], qi)
                dq_ref[rows, qsl[e]] += _dot(ds_scr[e], ke)
            return carry

        step(j, 0, True)
        lax.fori_loop(j + 1, nq, functools.partial(step, masked=False), 0)
        dk_ref[...] = jnp.concatenate([dk_acc[e] for e in range(2)], axis=1)
        dv_ref[...] = jnp.concatenate([dv_acc[e] for e in range(2)], axis=1)

    full_q = pl.BlockSpec((s, 2 * HEAD_PAD), lambda hp, j: (0, hp))
    full_v = pl.BlockSpec((s, 2 * V_DIM), lambda hp, j: (0, hp))
    blk_k = pl.BlockSpec((t, 2 * HEAD_PAD), lambda hp, j: (j, hp))
    blk_v = pl.BlockSpec((t, 2 * V_DIM), lambda hp, j: (j, hp))
    return pl.pallas_call(
        body, grid=(npair, nq),
        in_specs=[full_q, blk_k, blk_v, full_v, pl.BlockSpec((1, s, LANES), lambda hp, j: (hp, 0, 0)), full_v],
        out_specs=(full_q, blk_k, blk_v),
        out_shape=(SDS((s, MLA_HEADS * HEAD_PAD), F32), SDS((s, MLA_HEADS * HEAD_PAD), F32), SDS((s, MLA_HEADS * V_DIM), F32)),
        scratch_shapes=[pltpu.VMEM((2, t, t), F32), pltpu.VMEM((2, t, t), F32), pltpu.VMEM((2, t, t), BF16),
                        pltpu.VMEM((2, t, t), BF16), pltpu.VMEM((2, t, HEAD_PAD), F32), pltpu.VMEM((2, t, V_DIM), F32)],
        name=name, compiler_params=_params(("parallel", "arbitrary"), 48 << 20))(q, k, v, o, lse, do)


def _mem_attn_fwd(q, k, v, *, name):
    s = q.shape[0]
    tm = min(s, 512)
    ml = k.shape[0]
    scale = MEM_HEAD_DIM ** -0.5

    def body(q_ref, k_ref, v_ref, o_ref):
        for h in range(MEM_HEADS):
            hs = slice(MEM_HEAD_DIM * h, MEM_HEAD_DIM * (h + 1))
            sc = _dot_nt(q_ref[:, hs], k_ref[:, hs]) * scale
            pr = jnp.exp(sc - jnp.max(sc, axis=1, keepdims=True))
            pr = pr / jnp.sum(pr, axis=1, keepdims=True)
            o_ref[:, hs] = _dot(pr, v_ref[:, hs]).astype(o_ref.dtype)

    blk = pl.BlockSpec((tm, D_MODEL), lambda i: (i, 0))
    kv = pl.BlockSpec((ml, D_MODEL), lambda i: (0, 0))
    return pl.pallas_call(body, grid=(s // tm,), in_specs=[blk, kv, kv], out_specs=blk,
                          out_shape=SDS((s, D_MODEL), BF16), name=name,
                          compiler_params=_params(("parallel",), 24 << 20))(q, k, v)


def _mem_attn_bwd(q, k, v, do, *, name):
    s = q.shape[0]
    tm = min(s, 512)
    ml = k.shape[0]
    scale = MEM_HEAD_DIM ** -0.5

    def body(q_ref, k_ref, v_ref, do_ref, dq_ref, dk_ref, dv_ref):
        @pl.when(pl.program_id(0) == 0)
        def _():
            dk_ref[...] = jnp.zeros_like(dk_ref)
            dv_ref[...] = jnp.zeros_like(dv_ref)

        for h in range(MEM_HEADS):
            hs = slice(MEM_HEAD_DIM * h, MEM_HEAD_DIM * (h + 1))
            qh, kh, vh, doh = q_ref[:, hs], k_ref[:, hs], v_ref[:, hs], do_ref[:, hs]
            sc = _dot_nt(qh, kh) * scale
            pr = jnp.exp(sc - jnp.max(sc, axis=1, keepdims=True))
            pr = pr / jnp.sum(pr, axis=1, keepdims=True)
            dp = _dot_nt(doh, vh)
            dsc = pr * (dp - jnp.sum(pr * dp, axis=1, keepdims=True)) * scale
            dq_ref[:, hs] = _dot(dsc, kh).astype(dq_ref.dtype)
            dk_ref[:, hs] += _dot_tn(dsc, qh)
            dv_ref[:, hs] += _dot_tn(pr, doh)

    blk = pl.BlockSpec((tm, D_MODEL), lambda i: (i, 0))
    kv = pl.BlockSpec((ml, D_MODEL), lambda i: (0, 0))
    return pl.pallas_call(body, grid=(s // tm,), in_specs=[blk, kv, kv, blk], out_specs=(blk, kv, kv),
                          out_shape=(SDS((s, D_MODEL), BF16), SDS((ml, D_MODEL), F32), SDS((ml, D_MODEL), F32)), name=name,
                          compiler_params=_params(("arbitrary",), 32 << 20))(q, k, v, do)


MATS = (("w_in", (1024, 940), 1), ("w_uq", (384, 384), 1), ("w_ukv", (256, 512), 1), ("w_out", (512, 1024), 0),
        ("ssm_conv_w", (4, 512), 1),
        ("w_mq", (256, 1024), 0), ("w_mk", (256, 1024), 0), ("w_mv", (256, 1024), 0), ("w_mo", (256, 1024), 0),
        ("w_up", (1024, 1408), 1), ("w_down", (704, 1024), 0), ("ffn_conv_w", (3, 1408), 1))
GROUPS = {"proj": (0,), "mixer": (1, 2, 3, 4), "mem": (5, 6, 7, 8), "ffn": (9, 10, 11)}
UP_SHARD_COLS = 1408
F32_ON_WIRE = ("ssm_conv_w", "ffn_conv_w")
SMALL = (("norm_mix", 1024), ("ssm_conv_b", 2048), ("dt_bias", 16), ("a_log", 16), ("d_skip", 16), ("ssm_norm", 1024),
         ("q_norm", 384), ("kv_norm", 256), ("attn_out_norm", 1024), ("norm_mem_q", 1024), ("norm_mem_kv", 1024),
         ("norm_ffn", 1024), ("ffn_conv_b", 5632))
PACK_COLS = 1024


def _pad_cols(t, n):
    return jnp.pad(t, ((0, 0),) * (t.ndim - 1) + ((0, n - t.shape[-1]),))


def _w_in_to_padded(t):
    z, xbc, dt, cq, ckv, kr = jnp.split(t, (1024, 3072, 3088, 3472, 3728), axis=-1)
    return jnp.concatenate([xbc, z, cq, _pad_cols(dt, LANES), ckv, _pad_cols(kr, P_IN - P_KR)], axis=-1)


def _w_in_from_padded(t):
    return jnp.concatenate([t[..., P_Z:P_Z + 1024], t[..., P_XBC:P_XBC + 2048], t[..., P_DT:P_DT + SSM_HEADS],
                            t[..., P_CQ:P_CQ + Q_LORA], t[..., P_CKV:P_CKV + KV_LORA], t[..., P_KR:P_KR + QK_ROPE]], axis=-1)


def _cols_joined(g):
    return jnp.concatenate([g[j] for j in range(N_CHIPS)], axis=-1)


def _cols_by_chip(t, dtype):
    k = t.shape[0]
    return t.reshape(k, N_CHIPS, -1).transpose(1, 0, 2).astype(dtype)


def _rows_by_chip(t):
    return t.reshape(N_CHIPS, -1, t.shape[-1])


def _mixer_weights(gw):
    wl = {}
    uq = _cols_joined(gw["w_uq"]).reshape(Q_LORA, MLA_HEADS, QK_NOPE + QK_ROPE)
    wl["w_uq"] = _pad_cols(uq, HEAD_PAD).reshape(Q_LORA, MLA_HEADS * HEAD_PAD)
    wl["w_ukv"] = _cols_joined(gw["w_ukv"])
    wl["ssm_conv_w"] = _cols_joined(gw["ssm_conv_w"])
    return wl


def _layer_fwd(x0, mem, cos, sins, weights, sp, li):
    n = lambda t: f"l{li}_{t}"
    lead = ()
    sv = {"x0": x0}
    gw = dict(weights("proj", x0))
    w_in = _w_in_to_padded(_cols_joined(gw["w_in"]))
    h = _rms_fwd(x0, sp["norm_mix"], name=n("mix_norm"))
    in_hbm = lambda t: pltpu.with_memory_space_constraint(t, pltpu.HBM)
    proj = in_hbm(_mm(h, w_in, name=n("mix_proj")))
    gw.update(weights("mixer", proj))
    wl = dict(_mixer_weights(gw), w_in=w_in)
    xbc = in_hbm(_ssm_conv_fwd(proj, wl["ssm_conv_w"], sp["ssm_conv_b"], name=n("ssm_conv")))
    y, pstates = _ssd_fwd(xbc, proj, sp["dt_bias"], sp["a_log"], sp["d_skip"], name=n("ssd"))
    y_ssm = _gated_rms_fwd(y, proj, sp["ssm_norm"], name=n("ssm_gate"))
    cqn = _rms_fwd(proj, sp["q_norm"], col=(Q_LORA, P_CQ // Q_LORA), name=n("q_norm"))
    ckvn = _rms_fwd(proj, sp["kv_norm"], col=(KV_LORA, P_CKV // KV_LORA), name=n("kv_norm"))
    q = in_hbm(_mm(cqn, wl["w_uq"], name=n("uq")))
    kv = in_hbm(_mm(ckvn, wl["w_ukv"], name=n("ukv")))
    qr, kr, v = _mla_prep(q, kv, proj, cos, sins, name=n("rope"))
    att, lse = _flash_fwd(qr, kr, v, name=n("flash"))
    y_att = _rms_fwd(att, sp["attn_out_norm"], name=n("att_norm"))
    x1 = _mm(y_ssm, gw["w_out"], b_lead=lead, b_rows=(0, D_SSM), res=x0, name=n("out_a"))
    x1 = _mm(y_att, gw["w_out"], b_lead=lead, b_rows=(D_SSM, D_SSM), res=x1, name=n("out_b"))
    sv.update(h=h, proj=proj, xbc=xbc, y=y, pstates=pstates, y_ssm=y_ssm, cqn=cqn, ckvn=ckvn, qr=qr, kr=kr, v=v,
              att=att, lse=lse, y_att=y_att, x1=x1)
    gw.update(weights("mem", x1))
    hq = _rms_fwd(x1, sp["norm_mem_q"], name=n("memq_norm"))
    hm = _rms_fwd(mem, sp["norm_mem_kv"], name=n("memkv_norm"))
    mq = _mm(hq, gw["w_mq"], b_lead=lead, out_dtype=BF16, name=n("mq"))
    mk = _mm(hm, gw["w_mk"], b_lead=lead, out_dtype=BF16, name=n("mk"))
    mv = _mm(hm, gw["w_mv"], b_lead=lead, out_dtype=BF16, name=n("mv"))
    mo = _mem_attn_fwd(mq, mk, mv, name=n("mem_attn"))
    x2 = _mm(mo, gw["w_mo"], b_lead=lead, res=x1, name=n("mo"))
    sv.update(hq=hq, hm=hm, mq=mq, mk=mk, mv=mv, mo=mo, x2=x2)
    gw.update(weights("ffn", x2))
    wl["ffn_conv_w"] = _cols_joined(gw["ffn_conv_w"])
    hf = _rms_fwd(x2, sp["norm_ffn"], name=n("ffn_norm"))
    up_g = _mm(hf, gw["w_up"], b_lead=lead, b_chips=(0, 2), name=n("up_g"))
    up_v = _mm(hf, gw["w_up"], b_lead=lead, b_chips=(2, 2), name=n("up_v"))
    act = _ffn_conv_fwd(up_g, up_v, wl["ffn_conv_w"], sp["ffn_conv_b"], name=n("ffn_conv"))
    x3 = _mm(act, gw["w_down"], b_lead=lead, res=x2, name=n("down"))
    sv.update(hf=hf, up_g=up_g, up_v=up_v, act=act)
    return x3, sv, gw, wl


def _layer_bwd(dx3, dx3b, mem, cos, sins, gw, wl, sp, sv, li, emit):
    n = lambda t: f"l{li}_b_{t}"
    lead = ()
    g = {}

    def after(token, v):
        return v if token is None else v + token[0, 0]

    dact = _mm(dx3b, gw["w_down"], tb=True, b_lead=lead, out_dtype=BF16, name=n("down_dx"))
    g["w_down"] = _rows_by_chip(_mm(sv["act"], dx3b, ta=True, out_dtype=BF16, name=n("down_dw")))
    dup_g, dup_v, dcw, g["ffn_conv_b"] = _ffn_conv_bwd(
        sv["up_g"], sv["up_v"], wl["ffn_conv_w"], sp["ffn_conv_b"], dact, name=n("ffn_conv"))
    g["ffn_conv_w"] = _cols_by_chip(dcw, F32)
    nsh = UP_SHARD_COLS
    dhf = None
    for c4 in range(N_CHIPS):
        dhf = _mm(dup_g if c4 < 2 else dup_v, gw["w_up"], tb=True, a_col=(nsh, c4 % 2), b_lead=(c4,), res=dhf,
                  name=n(f"up{c4}_dx"))
    g["w_up"] = jnp.concatenate([_mm(sv["hf"], dup_g, ta=True, o_chips=nsh, out_dtype=BF16, name=n("upg_dw")),
                                 _mm(sv["hf"], dup_v, ta=True, o_chips=nsh, out_dtype=BF16, name=n("upv_dw"))], axis=0)
    dx2, dx2b, g["norm_ffn"] = _rms_bwd(sv["x2"], after(emit("ffn", g), sp["norm_ffn"]), dhf, dx3, name=n("ffn_norm"))
    dmo = _mm(dx2b, gw["w_mo"], tb=True, b_lead=lead, out_dtype=BF16, name=n("mo_dx"))
    g["w_mo"] = _rows_by_chip(_mm(sv["mo"], dx2b, ta=True, out_dtype=BF16, name=n("mo_dw")))
    dmq, dmk, dmv = _mem_attn_bwd(sv["mq"], sv["mk"], sv["mv"], dmo, name=n("mem_attn"))
    dhq = _mm(dmq, gw["w_mq"], tb=True, b_lead=lead, name=n("mq_dx"))
    g["w_mq"] = _rows_by_chip(_mm(sv["hq"], dmq, ta=True, out_dtype=BF16, name=n("mq_dw")))
    dhm = _mm(dmk, gw["w_mk"], tb=True, b_lead=lead, name=n("mk_dx"))
    dhm = _mm(dmv, gw["w_mv"], tb=True, b_lead=lead, res=dhm, name=n("mv_dx"))
    g["w_mk"] = _rows_by_chip(_mm(sv["hm"], dmk, ta=True, out_dtype=BF16, name=n("mk_dw")))
    g["w_mv"] = _rows_by_chip(_mm(sv["hm"], dmv, ta=True, out_dtype=BF16, name=n("mv_dw")))
    dx1, dx1b, g["norm_mem_q"] = _rms_bwd(sv["x1"], after(emit("mem", g), sp["norm_mem_q"]), dhq, dx2, name=n("memq_norm"))
    _, _, g["norm_mem_kv"] = _rms_bwd(mem, sp["norm_mem_kv"], dhm, name=n("memkv_norm"))
    dy_ssm = _mm(dx1b, gw["w_out"], tb=True, b_lead=lead, b_rows=(0, D_SSM), name=n("outa_dx"))
    dy_att = _mm(dx1b, gw["w_out"], tb=True, b_lead=lead, b_rows=(D_SSM, D_SSM), name=n("outb_dx"))
    g["w_out"] = _rows_by_chip(jnp.concatenate([_mm(sv["y_ssm"], dx1b, ta=True, out_dtype=BF16, name=n("outa_dw")),
                                                _mm(sv["y_att"], dx1b, ta=True, out_dtype=BF16, name=n("outb_dw"))], axis=0))
    datt, _, g["attn_out_norm"] = _rms_bwd(sv["att"], sp["attn_out_norm"], dy_att, name=n("att_norm"))
    dqr, dkr, dv = _flash_bwd(sv["qr"], sv["kr"], sv["v"], sv["att"], sv["lse"], datt, name=n("flash"))
    dq, dkv, dkrope = _mla_prep_bwd(dqr, dkr, dv, cos, sins, name=n("rope"))
    duq = _mm(sv["cqn"], dq, ta=True, name=n("uq_dw")).reshape(Q_LORA, MLA_HEADS, HEAD_PAD)[..., :QK_NOPE + QK_ROPE]
    g["w_uq"] = _cols_by_chip(duq.reshape(Q_LORA, -1), BF16)
    dcqn = _mm(dq, wl["w_uq"], tb=True, name=n("uq_dx"))
    g["w_ukv"] = _cols_by_chip(_mm(sv["ckvn"], dkv, ta=True, name=n("ukv_dw")), BF16)
    dckvn = _mm(dkv, wl["w_ukv"], tb=True, name=n("ukv_dx"))
    proj = sv["proj"]
    _, dcq, g["q_norm"] = _rms_bwd(proj, sp["q_norm"], dcqn, col=(Q_LORA, P_CQ // Q_LORA), name=n("q_norm"))
    _, dckv, g["kv_norm"] = _rms_bwd(proj, sp["kv_norm"], dckvn, col=(KV_LORA, P_CKV // KV_LORA), name=n("kv_norm"))
    dy, dz, g["ssm_norm"] = _gated_rms_bwd(sv["y"], proj, sp["ssm_norm"], dy_ssm, name=n("ssm_gate"))
    dxbc, ddt, da_log, dd_skip, ddt_bias = _ssd_bwd(
        sv["xbc"], proj, sp["dt_bias"], sp["a_log"], sp["d_skip"], sv["pstates"], dy, name=n("ssd"))
    g["a_log"], g["d_skip"], g["dt_bias"] = da_log[0, :SSM_HEADS], dd_skip[0, :SSM_HEADS], ddt_bias[0, :SSM_HEADS]
    dxbc_pre, dsw, g["ssm_conv_b"] = _ssm_conv_bwd(proj, wl["ssm_conv_w"], sp["ssm_conv_b"], dxbc, name=n("ssm_conv"))
    g["ssm_conv_w"] = _cols_by_chip(dsw, F32)
    started = emit("mixer", g)
    s = proj.shape[0]
    dproj = jnp.concatenate([dxbc_pre, dz, dcq, ddt, dckv, dkrope,
                             jnp.zeros((s, P_IN - P_KR - LANES), BF16)], axis=1)
    dh = _mm(dproj, wl["w_in"], tb=True, name=n("proj_dx"))
    g["w_in"] = _cols_by_chip(_w_in_from_padded(_mm(sv["h"], dproj, ta=True, name=n("proj_dw"))), BF16)
    dx0, dx0b, g["norm_mix"] = _rms_bwd(sv["x0"], after(started, sp["norm_mix"]), dh, dx1, name=n("mix_norm"))
    return dx0, dx0b, g, emit("proj", g)


def _chip_peers(x, y):
    return [(1 - x, y), (x, 1 - y), (1 - x, 1 - y)]


HBM_SPEC = pl.BlockSpec(memory_space=pltpu.HBM)
SEM_SPEC = pl.BlockSpec(memory_space=pltpu.SEMAPHORE)
ANY_SPEC = pl.BlockSpec(memory_space=pl.ANY)
VMEM_SPEC = pl.BlockSpec(memory_space=pltpu.VMEM)
DATAFLOW = pltpu.SideEffectType.DATAFLOW_SIDE_EFFECTING
TOKEN_SHAPE = (8, LANES)


def _exchange_start(srcs, land_shapes, src_view, dst_view, token, *, name):
    n = len(srcs)

    def body(*refs):
        s, l, tok_in = refs[:n], refs[n:2 * n], refs[2 * n]
        send_sems, recv_sems = refs[2 * n + 1], refs[2 * n + 2]
        tok_out = refs[-1]
        x, y, c = lax.axis_index("x"), lax.axis_index("y"), lax.axis_index("c")
        me = 2 * x + y
        for t in range(n):
            for k, (px, py) in enumerate(_chip_peers(x, y)):
                pltpu.make_async_remote_copy(
                    src_ref=src_view(t, s[t], 2 * px + py), dst_ref=dst_view(t, l[t], me), send_sem=send_sems.at[3 * t + k],
                    recv_sem=recv_sems.at[3 * t + k], device_id=(px, py, c), device_id_type=MESH).start()
            pltpu.make_async_copy(src_view(t, s[t], me), dst_view(t, l[t], me), send_sems.at[3 * n + t]).start()
        tok_out[...] = tok_in[...]

    hbm = lambda t: pltpu.with_memory_space_constraint(t, pltpu.HBM)
    lands = [lax.empty(l.shape, l.dtype) for l in land_shapes]
    outs = pl.pallas_call(
        body, name=name,
        out_shape=(pltpu.SemaphoreType.DMA((4 * n,)), pltpu.SemaphoreType.DMA((3 * n,)),
                   *[pltpu.HBM(l.shape, l.dtype) for l in land_shapes], SDS(TOKEN_SHAPE, F32)),
        in_specs=[HBM_SPEC] * (2 * n) + [VMEM_SPEC], out_specs=(SEM_SPEC, SEM_SPEC, *[HBM_SPEC] * n, VMEM_SPEC),
        input_output_aliases={n + t: 2 + t for t in range(n)},
        compiler_params=pltpu.CompilerParams(has_side_effects=DATAFLOW))(*[hbm(t) for t in srcs], *[hbm(t) for t in lands], token)
    return outs[0], outs[1], list(outs[2:2 + n]), outs[-1]


def _exchange_wait(srcs, lands, send_sems, recv_sems, after, src_view, dst_view, which, *, name):
    n = len(srcs)
    m = len(which)

    def body(*refs):
        s, l = refs[:m], refs[m:2 * m]
        send_ref, recv_ref = refs[2 * m], refs[2 * m + 1]
        x, y, c = lax.axis_index("x"), lax.axis_index("y"), lax.axis_index("c")
        me = 2 * x + y
        for i, t in enumerate(which):
            for k, (px, py) in enumerate(_chip_peers(x, y)):
                chip = 2 * px + py
                cp = pltpu.make_async_remote_copy(
                    src_ref=src_view(t, s[i], chip), dst_ref=dst_view(t, l[i], chip), send_sem=send_ref.at[3 * t + k],
                    recv_sem=recv_ref.at[3 * t + k], device_id=(px, py, c), device_id_type=MESH)
                cp.wait_send()
                cp.wait_recv()
            pltpu.make_async_copy(src_view(t, s[i], me), dst_view(t, l[i], me), send_ref.at[3 * n + t]).wait()

    outs = pl.pallas_call(
        body, name=name, out_shape=[pltpu.HBM(lands[t].shape, lands[t].dtype) for t in which],
        in_specs=[HBM_SPEC] * (2 * m) + [SEM_SPEC, SEM_SPEC, ANY_SPEC], out_specs=[HBM_SPEC] * m,
        input_output_aliases={m + i: i for i in range(m)},
        compiler_params=pltpu.CompilerParams(has_side_effects=DATAFLOW))(
            *[srcs[t] for t in which], *[lands[t] for t in which], send_sems, recv_sems, after)
    return list(outs)


def _gather_layer_start(shards, li, token, tag=""):
    src_view = lambda t, ref, chip: ref.at[li]
    dst_view = lambda t, ref, chip: ref.at[chip]
    send_sems, recv_sems, lands, token = _exchange_start(
        shards, [SDS((N_CHIPS,) + s.shape[1:], s.dtype) for s in shards], src_view, dst_view, token,
        name=f"gather{li}{tag}_start")
    return (shards, lands, send_sems, recv_sems, src_view, dst_view, f"gather{li}{tag}"), token


def _scatter_start(grads, tag, token):
    view = lambda t, ref, chip: ref.at[chip]
    send_sems, recv_sems, lands, token = _exchange_start(
        grads, [SDS(g.shape, g.dtype) for g in grads], view, view, token, name=f"scatter{tag}_start")
    return (grads, lands, send_sems, recv_sems, view, view, f"scatter{tag}"), token


def _exchange_finish(handle, after, which=None, tag=""):
    srcs, lands, send_sems, recv_sems, src_view, dst_view, name = handle
    which = tuple(range(len(srcs))) if which is None else which
    return _exchange_wait(srcs, lands, send_sems, recv_sems, after, src_view, dst_view, which, name=f"{name}{tag}_wait")


def _swap_start(bufs, token, *, name):
    n = len(bufs)

    def body(*refs):
        s, l, tok_in = refs[:n], refs[n:2 * n], refs[2 * n]
        send_sems, recv_sems = refs[2 * n + 1], refs[2 * n + 2]
        x, y, c = lax.axis_index("x"), lax.axis_index("y"), lax.axis_index("c")
        for t in range(n):
            pltpu.make_async_remote_copy(src_ref=s[t], dst_ref=l[t], send_sem=send_sems.at[t], recv_sem=recv_sems.at[t],
                                         device_id=(x, y, 1 - c), device_id_type=MESH).start()
        refs[-1][...] = tok_in[...]

    hbm = lambda t: pltpu.with_memory_space_constraint(t, pltpu.HBM)
    lands = [lax.empty(b.shape, b.dtype) for b in bufs]
    outs = pl.pallas_call(
        body, name=f"{name}_start",
        out_shape=(pltpu.SemaphoreType.DMA((n,)), pltpu.SemaphoreType.DMA((n,)),
                   *[pltpu.HBM(b.shape, b.dtype) for b in bufs], SDS(TOKEN_SHAPE, F32)),
        in_specs=[HBM_SPEC] * (2 * n) + [VMEM_SPEC], out_specs=(SEM_SPEC, SEM_SPEC, *[HBM_SPEC] * n, VMEM_SPEC),
        input_output_aliases={n + t: 2 + t for t in range(n)},
        compiler_params=pltpu.CompilerParams(has_side_effects=DATAFLOW))(*[hbm(t) for t in bufs], *[hbm(t) for t in lands], token)
    return (bufs, list(outs[2:2 + n]), outs[0], outs[1], name), outs[-1]


def _swap_wait(handle, after):
    bufs, lands, send_sems, recv_sems, name = handle
    n = len(bufs)

    def body(*refs):
        s, l = refs[:n], refs[n:2 * n]
        send_ref, recv_ref = refs[2 * n], refs[2 * n + 1]
        x, y, c = lax.axis_index("x"), lax.axis_index("y"), lax.axis_index("c")
        for t in range(n):
            cp = pltpu.make_async_remote_copy(src_ref=s[t], dst_ref=l[t], send_sem=send_ref.at[t], recv_sem=recv_ref.at[t],
                                              device_id=(x, y, 1 - c), device_id_type=MESH)
            cp.wait_send()
            cp.wait_recv()

    outs = pl.pallas_call(
        body, name=f"{name}_wait", out_shape=[pltpu.HBM(b.shape, b.dtype) for b in bufs],
        in_specs=[HBM_SPEC] * (2 * n) + [SEM_SPEC, SEM_SPEC, ANY_SPEC], out_specs=[HBM_SPEC] * n,
        input_output_aliases={n + t: t for t in range(n)},
        compiler_params=pltpu.CompilerParams(has_side_effects=DATAFLOW))(*bufs, *lands, send_sems, recv_sems, after)
    return list(outs)


def _all_gather8(src, *, name):
    def body(src_ref, out_ref, send_sems, recv_sems, local_sem):
        x, y, c = lax.axis_index("x"), lax.axis_index("y"), lax.axis_index("c")
        me = 4 * x + 2 * y + c
        mine = pltpu.make_async_copy(src_ref, out_ref.at[me], local_sem)
        mine.start()

        def peer(k):
            return (x ^ (k >> 2 & 1), y ^ (k >> 1 & 1), c ^ (k & 1))

        sends = []
        for k in range(1, N_DEV):
            cp = pltpu.make_async_remote_copy(src_ref=src_ref, dst_ref=out_ref.at[me], send_sem=send_sems.at[k - 1],
                                              recv_sem=recv_sems.at[k - 1], device_id=peer(k), device_id_type=MESH)
            cp.start()
            sends.append(cp)
        for k in range(1, N_DEV):
            px, py, pc = peer(k)
            pltpu.make_async_remote_copy(src_ref=src_ref, dst_ref=out_ref.at[4 * px + 2 * py + pc],
                                         send_sem=send_sems.at[k - 1], recv_sem=recv_sems.at[k - 1],
                                         device_id=peer(k), device_id_type=MESH).wait_recv()
        for cp in sends:
            cp.wait_send()
        mine.wait()

    any_spec = pl.BlockSpec(memory_space=pl.ANY)
    return pl.pallas_call(
        body, in_specs=[any_spec], out_specs=any_spec, out_shape=SDS((N_DEV,) + src.shape, src.dtype),
        scratch_shapes=[pltpu.SemaphoreType.DMA((N_DEV - 1,)), pltpu.SemaphoreType.DMA((N_DEV - 1,)), pltpu.SemaphoreType.DMA],
        name=name)(src)


def _adam_terms(w, g, m, v):
    m = ADAM_B1 * m + (1.0 - ADAM_B1) * g
    v = ADAM_B2 * v + (1.0 - ADAM_B2) * (g * g)
    m_hat = m / (1.0 - ADAM_B1 ** ADAM_STEP)
    v_hat = v / (1.0 - ADAM_B2 ** ADAM_STEP)
    delta = -ADAM_LR * (m_hat / (jnp.sqrt(v_hat) + ADAM_EPS) + ADAM_WD * w)
    return delta, m, v


def _adamw_shard(mine, other, w, m, v, *, name):
    d, a, b = w.shape
    tr = next((t for t in (128, 64, 32, 16) if a % t == 0), a)

    def body(*refs):
        ga, gb = refs[:d], refs[d:2 * d]
        w_ref, m_ref, v_ref, g_ref, d_ref, nm_ref, nv_ref = refs[2 * d:]

        def plane(ref):
            return ((ref[0].astype(F32) + ref[1].astype(F32)) + ref[2].astype(F32)) + ref[3].astype(F32)

        for lp in range(d):
            @pl.when(pl.program_id(0) == lp)
            def _(lp=lp):
                g = plane(ga[lp]) + plane(gb[lp])
                delta, mn, vn = _adam_terms(w_ref[...], g, m_ref[...], v_ref[...])
                g_ref[...] = g
                d_ref[...] = delta
                nm_ref[...] = mn
                nv_ref[...] = vn

    gspecs = [pl.BlockSpec((N_CHIPS, tr, b), lambda l, i, lp=lp: (0, jnp.where(l == lp, i, 0), 0)) for lp in range(d)]
    blk = pl.BlockSpec((None, tr, b), lambda l, i: (l, i, 0))
    shp = SDS((d, a, b), F32)
    return pl.pallas_call(
        body, grid=(d, a // tr), in_specs=gspecs + gspecs + [blk, blk, blk], out_specs=(blk,) * 4, out_shape=(shp,) * 4,
        name=name, compiler_params=_params(("arbitrary", "arbitrary"), 48 << 20))(*mine, *other, w, m, v)


def _adamw_small(g8, w, m, v, *, name):
    n = w.shape[1]

    def body(g8_ref, w_ref, m_ref, v_ref, g_ref, d_ref, nm_ref, nv_ref):
        g = g8_ref[0]
        for k in range(1, N_DEV):
            g = g + g8_ref[k]
        delta, mn, vn = _adam_terms(w_ref[...], g, m_ref[...], v_ref[...])
        g_ref[...] = g
        d_ref[...] = delta
        nm_ref[...] = mn
        nv_ref[...] = vn

    shp = SDS((1, n), F32)
    return pl.pallas_call(body, out_shape=(shp,) * 4, name=name, compiler_params=_params(None, 24 << 20))(g8, w, m, v)


def _rope_tables(positions):
    inv_freq = 1.0 / (ROPE_THETA ** (jnp.arange(0, QK_ROPE, 2, dtype=F32) / QK_ROPE))
    ang = positions.astype(F32)[:, None] * inv_freq
    c, s = jnp.cos(ang), jnp.sin(ang)
    n = positions.shape[0]
    pad = jnp.zeros((n, HEAD_PAD - QK_NOPE - QK_ROPE), F32)
    cos = jnp.concatenate([jnp.ones((n, QK_NOPE), F32), c, c, pad], axis=1)
    sins = jnp.concatenate([jnp.zeros((n, QK_NOPE), F32), -s, s, pad], axis=1)
    return cos, sins


def _pad_lanes(v):
    return _pad_cols(v.reshape(1, -1), LANES)


def _local_step(x, mem, positions, weights, small, final_norm, loss_target, emit, token):
    cos, sins = _rope_tables(positions)
    saved, gws, wls, sps = [], [], [], []
    h = x
    for li in range(DEPTH):
        sp = {k: small[k][li] for k, _ in SMALL}
        if li == 0:
            sp["norm_mix"] = sp["norm_mix"] + token[0, 0]
        for k in ("dt_bias", "a_log", "d_skip"):
            sp[k] = _pad_lanes(sp[k])
        h, sv, gw, wl = _layer_fwd(h, mem, cos, sins, functools.partial(weights, li), sp, li)
        saved.append(sv)
        gws.append(gw)
        wls.append(wl)
        sps.append(sp)
    loss, dh, dhb, g_final = _final_loss(h, final_norm, loss_target, name="final_loss")
    grads = [None] * DEPTH
    started = None
    for li in reversed(range(DEPTH)):
        sp = sps[li]
        if started is not None:
            sp = dict(sp, ffn_conv_b=sp["ffn_conv_b"] + started[0, 0])
        dh, dhb, grads[li], started = _layer_bwd(dh, dhb, mem, cos, sins, gws[li], wls[li], sp, saved[li], li,
                                                 functools.partial(emit, li))
    return loss, dh, grads, g_final


def _gathered_views(which, lands):
    return {MATS[t][0]: (b.reshape(-1, b.shape[-1]) if MATS[t][2] == 0 else b) for t, b in zip(which, lands)}


def kernel(x, mem, positions, norm_mix, w_in, ssm_conv_w, ssm_conv_b, dt_bias, a_log, d_skip, ssm_norm, q_norm, w_uq, kv_norm, w_ukv, attn_out_norm, w_out, norm_mem_q, norm_mem_kv, w_mq, w_mk, w_mv, w_mo, norm_ffn, w_up, ffn_conv_w, ffn_conv_b, w_down, final_norm, loss_target, m_norm_mix, m_w_in, m_ssm_conv_w, m_ssm_conv_b, m_dt_bias, m_a_log, m_d_skip, m_ssm_norm, m_q_norm, m_w_uq, m_kv_norm, m_w_ukv, m_attn_out_norm, m_w_out, m_norm_mem_q, m_norm_mem_kv, m_w_mq, m_w_mk, m_w_mv, m_w_mo, m_norm_ffn, m_w_up, m_ffn_conv_w, m_ffn_conv_b, m_w_down, m_final_norm, v_norm_mix, v_w_in, v_ssm_conv_w, v_ssm_conv_b, v_dt_bias, v_a_log, v_d_skip, v_ssm_norm, v_q_norm, v_w_uq, v_kv_norm, v_w_ukv, v_attn_out_norm, v_w_out, v_norm_mem_q, v_norm_mem_kv, v_w_mq, v_w_mk, v_w_mv, v_w_mo, v_norm_ffn, v_w_up, v_ffn_conv_w, v_ffn_conv_b, v_w_down, v_final_norm):
    args = dict(locals())
    names = ["norm_mix", "w_in", "ssm_conv_w", "ssm_conv_b", "dt_bias", "a_log", "d_skip", "ssm_norm", "q_norm", "w_uq",
             "kv_norm", "w_ukv", "attn_out_norm", "w_out", "norm_mem_q", "norm_mem_kv", "w_mq", "w_mk", "w_mv", "w_mo",
             "norm_ffn", "w_up", "ffn_conv_w", "ffn_conv_b", "w_down", "final_norm"]
    wts = {k: args[k] for k in names}
    mom = {k: args["m_" + k] for k in names}
    var = {k: args["v_" + k] for k in names}
    mat_names = [k for k, _, _ in MATS]

    shards = [wts[k] if k in F32_ON_WIRE else wts[k].astype(BF16) for k in mat_names]
    token = jnp.zeros(TOKEN_SHAPE, F32)
    first, token = _gather_layer_start(shards[:1], 0, token, tag="_first")
    gathers = []
    for li in range(DEPTH):
        handle, token = _gather_layer_start(shards[1:] if li == 0 else shards, li, token)
        gathers.append(handle)
    small = {k: wts[k] for k, _ in SMALL}

    def weights(li, group, after):
        which = GROUPS[group]
        if li > 0:
            return _gathered_views(which, _exchange_finish(gathers[li], after, which, tag=f"_{group}"))
        if group == "proj":
            return _gathered_views(which, _exchange_finish(first, after))
        return _gathered_views(which, _exchange_finish(gathers[0], after, tuple(t - 1 for t in which), tag=f"_{group}"))

    scatters = [[] for _ in range(DEPTH)]
    nm = len(mat_names)
    mine = [[None] * nm for _ in range(DEPTH)]
    swaps = [None] * DEPTH
    last_started = [None]

    def swap_layer(li, after):
        for which, handle in scatters[li]:
            for t, b in zip(which, _exchange_finish(handle, after)):
                mine[li][t] = b
        swaps[li], started = _swap_start(mine[li], jnp.zeros(TOKEN_SHAPE, F32), name=f"swap{li}")
        return started

    def emit(li, group, g):
        last = group == "proj"
        if li == 0:
            which = GROUPS[group]
        elif last:
            which = tuple(range(nm))
        else:
            return None
        handle, started = _scatter_start([g[MATS[t][0]] for t in which], f"{li}_{group}", jnp.zeros(TOKEN_SHAPE, F32))
        scatters[li].append((which, handle))
        last_started[0] = started
        if li + 1 < DEPTH and group == ("mixer" if li == 0 else "proj"):
            started = started + swap_layer(li + 1, g["ssm_conv_w"])
        return started

    loss, grad_x, grads, g_final = _local_step(x[0], mem[0], positions[0], weights, small, wts["final_norm"],
                                               loss_target[0], emit, token)
    loss = lax.psum(loss, ("x", "y", "c"))

    w_in_t = GROUPS["proj"][0]
    others = [t for t in range(nm) if t != w_in_t]
    for which, handle in scatters[0]:
        if which != GROUPS["proj"]:
            for t, b in zip(which, _exchange_finish(handle, grad_x)):
                mine[0][t] = b
    swaps[0], _ = _swap_start([mine[0][t] for t in others], last_started[0], name="swap0")
    other = [dict(zip(others, _swap_wait(swaps[0], grad_x)))] + [
        dict(enumerate(_swap_wait(swaps[li], grad_x))) for li in range(1, DEPTH)]

    def adamw(t):
        k = mat_names[t]
        return _adamw_shard([mine[li][t] for li in range(DEPTH)], [other[li][t] for li in range(DEPTH)],
                            wts[k], mom[k], var[k], name=f"adamw_{k}")

    mat_out = {mat_names[t]: adamw(t) for t in others}
    done = sum(mat_out[mat_names[t]][0][0, 0, :1] for t in others)
    (last_handle,) = [handle for which, handle in scatters[0] if which == GROUPS["proj"]]
    (mine[0][w_in_t],) = _exchange_finish(last_handle, done)
    last_swap, _ = _swap_start([mine[0][w_in_t]], jnp.zeros(TOKEN_SHAPE, F32), name="swap0_last")
    (other[0][w_in_t],) = _swap_wait(last_swap, done)
    mat_out[mat_names[w_in_t]] = adamw(w_in_t)

    def pack_small(get, fin):
        flat = [get(k).reshape(-1) for k, _ in SMALL] + [fin.reshape(-1)]
        n = sum(f.shape[0] for f in flat)
        return jnp.concatenate(flat + [jnp.zeros((-n % PACK_COLS,), F32)]).reshape(1, -1)

    gs = pack_small(lambda k: jnp.stack([grads[li][k] for li in range(DEPTH)]), g_final)
    g8 = _all_gather8(gs, name="gather_small_grads")
    small_out = _adamw_small(g8, pack_small(wts.get, wts["final_norm"]), pack_small(mom.get, mom["final_norm"]),
                             pack_small(var.get, var["final_norm"]), name="adamw_small")

    def unpack_small(buf):
        out, off = {}, 0
        for k, nel in SMALL:
            out[k] = buf[0, off:off + DEPTH * nel].reshape(DEPTH, nel)
            off += DEPTH * nel
        out["final_norm"] = buf[0, off:off + D_MODEL]
        return out

    small_res = [unpack_small(b) for b in small_out]
    res = []
    for kind in range(4):
        for k in names:
            res.append(small_res[kind][k] if k in small_res[kind] else mat_out[k][kind])
    return (loss, grad_x[None], *res)
```

```python
import functools
import math

import jax
import jax.numpy as jnp
from jax import lax
from jax.experimental import pallas as pl
from jax.experimental.pallas import tpu as pltpu

F32 = jnp.float32
BF16 = jnp.bfloat16
HIGHEST = lax.Precision.HIGHEST
SDS = jax.ShapeDtypeStruct
MESH = pl.DeviceIdType.MESH

D_MODEL = 1024
DEPTH = 4
EPS = 1e-6
SSM_HEADS = 16
SSM_HEAD_DIM = 64
D_SSM = 1024
SSM_GROUPS = 4
SSM_STATE = 128
SSM_CONV = 4
SSM_CHUNK = 128
CONV_CH = 2048
MLA_HEADS = 16
QK_NOPE = 64
QK_ROPE = 32
V_DIM = 64
Q_LORA = 384
KV_LORA = 256
ROPE_THETA = 10000.0
MEM_HEADS = 4
MEM_HEAD_DIM = 256
D_FF = 2816
FFN_CONV = 3
D_IN = 3760
ADAM_LR = 0.001
ADAM_B1 = 0.9
ADAM_B2 = 0.999
ADAM_EPS = 1e-08
ADAM_WD = 0.01
ADAM_STEP = 10

LANES = 128
HEAD_PAD = 128
N_CHIPS = 4
N_DEV = 8
VMEM_CAP_MB = 56

P_XBC, P_Z, P_CQ, P_DT, P_CKV, P_KR, P_IN = 0, 2048, 3072, 3456, 3584, 3840, 4096
NEG = -1e30


def _tile(n, pref):
    t = (min(n, pref) // LANES) * LANES
    while t >= LANES:
        if n % t == 0:
            return t
        t -= LANES
    return n


def _params(sem=None, vmem_bytes=None):
    kw = {}
    if sem is not None:
        kw["dimension_semantics"] = sem
    if vmem_bytes is not None:
        kw["vmem_limit_bytes"] = int(min(max(vmem_bytes, 16 << 20), VMEM_CAP_MB << 20))
    return pltpu.CompilerParams(**kw)


def _nbytes(shape, dtype):
    return math.prod(shape) * jnp.dtype(dtype).itemsize


def _mm(a, b, *, ta=False, tb=False, res=None, out_dtype=F32, name, a_col=None, b_lead=(), b_rows=None,
        b_chips=None, o_chips=None):
    if ta:
        k, m = a.shape
    else:
        m, k = (a.shape[0], a.shape[1] if a_col is None else a_col[0])
    rows_b, cols_b = b.shape[-2:]
    row0 = 0
    if b_rows is not None:
        row0, rows_b = b_rows
    nlead = len(b_lead)
    if b_chips is not None:
        assert not tb
        kb, tn, n = rows_b, cols_b, b_chips[1] * cols_b
        b_blk = (None,) * (1 + nlead) + (kb, tn)
        b_map = lambda i, j: (b_chips[0] + j,) + tuple(b_lead) + (0, 0)
    elif tb:
        n, kb = rows_b, cols_b
        tn = _tile(n, 512)
        assert row0 % tn == 0
        b_blk = (None,) * nlead + (tn, kb)
        b_map = lambda i, j: tuple(b_lead) + (j + row0 // tn, 0)
    else:
        kb, n = rows_b, cols_b
        tn = o_chips if o_chips else _tile(n, 512)
        assert row0 % kb == 0
        b_blk = (None,) * nlead + (kb, tn)
        b_map = lambda i, j: tuple(b_lead) + (row0 // kb, j)
    assert k == kb, (a.shape, b.shape, ta, tb, k, kb)
    tm = _tile(m, 512)
    if ta:
        a_blk, a_map = (k, tm), (lambda i, j: (0, i))
    else:
        a_blk, a_map = (tm, k), ((lambda i, j: (i, 0)) if a_col is None else (lambda i, j: (i, a_col[1])))
    if o_chips:
        o_spec = pl.BlockSpec((None, tm, tn), lambda i, j: (j, i, 0))
        o_shape = SDS((n // tn, m, tn), out_dtype)
    else:
        o_spec = pl.BlockSpec((tm, tn), lambda i, j: (i, j))
        o_shape = SDS((m, n), out_dtype)
    dims = (((0 if ta else 1,), (1 if tb else 0,)), ((), ()))
    has_res = res is not None

    def body(*refs):
        a_ref, b_ref = refs[0], refs[1]
        o_ref = refs[-1]
        acc = lax.dot_general(a_ref[...].astype(BF16), b_ref[...].astype(BF16), dims, preferred_element_type=F32)
        if has_res:
            acc = acc + refs[2][...]
        o_ref[...] = acc.astype(o_ref.dtype)

    bb = tuple(d for d in b_blk if d is not None)
    vmem = 2 * (_nbytes(a_blk, a.dtype) + _nbytes(bb, b.dtype) + (2 if has_res else 1) * _nbytes((tm, tn), F32))
    vmem += _nbytes(a_blk, BF16) + _nbytes(bb, BF16) + 2 * _nbytes((tm, tn), F32) + (4 << 20)
    args = (a, b) + ((res,) if has_res else ())
    specs = [pl.BlockSpec(a_blk, a_map), pl.BlockSpec(b_blk, b_map)] + ([o_spec] if has_res else [])
    return pl.pallas_call(body, grid=(m // tm, n // tn), in_specs=specs, out_specs=o_spec, out_shape=o_shape, name=name,
                          compiler_params=_params(("parallel", "parallel"), vmem))(*args)


def _sigmoid(x):
    return 1.0 / (1.0 + jnp.exp(-x))


def _rms_fwd(x, g, *, col=None, name):
    s = x.shape[0]
    w, ci = (x.shape[1], 0) if col is None else col
    tm = min(s, 512)

    def body(x_ref, g_ref, o_ref):
        xv = x_ref[...].astype(F32)
        r = lax.rsqrt(jnp.mean(xv * xv, axis=-1, keepdims=True) + EPS)
        o_ref[...] = (xv * r * g_ref[...]).astype(o_ref.dtype)

    return pl.pallas_call(
        body, grid=(s // tm,),
        in_specs=[pl.BlockSpec((tm, w), lambda i: (i, ci)), pl.BlockSpec((1, w), lambda i: (0, 0))],
        out_specs=pl.BlockSpec((tm, w), lambda i: (i, 0)), out_shape=SDS((s, w), BF16), name=name,
        compiler_params=_params(("parallel",), 10 * tm * w * 4))(x, g.reshape(1, w))


def _rms_bwd(x, g, dy, dres=None, *, col=None, name):
    s = x.shape[0]
    w, ci = (x.shape[1], 0) if col is None else col
    tm = min(s, 512)
    has_res = dres is not None

    def body(*refs):
        x_ref, g_ref, dy_ref = refs[:3]
        dx_ref, dxb_ref, dg_ref = refs[-3:]
        xv = x_ref[...].astype(F32)
        dyv = dy_ref[...].astype(F32)
        r = lax.rsqrt(jnp.mean(xv * xv, axis=-1, keepdims=True) + EPS)
        u = dyv * g_ref[...]
        dx = r * u - xv * (r * r * r) * jnp.mean(xv * u, axis=-1, keepdims=True)
        if has_res:
            dx = dx + refs[3][...]
        dx_ref[...] = dx
        dxb_ref[...] = dx.astype(BF16)

        @pl.when(pl.program_id(0) == 0)
        def _():
            dg_ref[...] = jnp.zeros_like(dg_ref)

        dg_ref[...] += jnp.sum(dyv * xv * r, axis=0, keepdims=True)

    blk = pl.BlockSpec((tm, w), lambda i: (i, 0))
    specs = [pl.BlockSpec((tm, w), lambda i: (i, ci)), pl.BlockSpec((1, w), lambda i: (0, 0)), blk]
    args = [x, g.reshape(1, w), dy]
    if has_res:
        specs.append(blk)
        args.append(dres)
    dx, dxb, dg = pl.pallas_call(
        body, grid=(s // tm,), in_specs=specs,
        out_specs=(blk, blk, pl.BlockSpec((1, w), lambda i: (0, 0))),
        out_shape=(SDS((s, w), F32), SDS((s, w), BF16), SDS((1, w), F32)), name=name,
        compiler_params=_params(("arbitrary",), 18 * tm * w * 4))(*args)
    return dx, dxb, dg.reshape(w)


def _gated_rms_fwd(y, proj, g, *, name):
    s, w = y.shape
    tm = min(s, 512)

    def body(y_ref, z_ref, g_ref, o_ref):
        z = z_ref[...]
        t = y_ref[...] * (z * _sigmoid(z))
        r = lax.rsqrt(jnp.mean(t * t, axis=-1, keepdims=True) + EPS)
        o_ref[...] = (t * r * g_ref[...]).astype(o_ref.dtype)

    blk = pl.BlockSpec((tm, w), lambda i: (i, 0))
    return pl.pallas_call(
        body, grid=(s // tm,),
        in_specs=[blk, pl.BlockSpec((tm, w), lambda i: (i, P_Z // w)), pl.BlockSpec((1, w), lambda i: (0, 0))],
        out_specs=blk, out_shape=SDS((s, w), BF16), name=name,
        compiler_params=_params(("parallel",), 14 * tm * w * 4))(y, proj, g.reshape(1, w))


def _gated_rms_bwd(y, proj, g, dout, *, name):
    s, w = y.shape
    tm = min(s, 512)

    def body(y_ref, z_ref, g_ref, do_ref, dy_ref, dz_ref, dg_ref):
        z = z_ref[...]
        yv = y_ref[...]
        dov = do_ref[...]
        sg = _sigmoid(z)
        sz = z * sg
        t = yv * sz
        r = lax.rsqrt(jnp.mean(t * t, axis=-1, keepdims=True) + EPS)
        u = dov * g_ref[...]
        dt = r * u - t * (r * r * r) * jnp.mean(t * u, axis=-1, keepdims=True)
        dy_ref[...] = dt * sz
        dz_ref[...] = (dt * yv * (sg * (1.0 + z * (1.0 - sg)))).astype(dz_ref.dtype)

        @pl.when(pl.program_id(0) == 0)
        def _():
            dg_ref[...] = jnp.zeros_like(dg_ref)

        dg_ref[...] += jnp.sum(dov * t * r, axis=0, keepdims=True)

    blk = pl.BlockSpec((tm, w), lambda i: (i, 0))
    vec = pl.BlockSpec((1, w), lambda i: (0, 0))
    dy, dz, dg = pl.pallas_call(
        body, grid=(s // tm,),
        in_specs=[blk, pl.BlockSpec((tm, w), lambda i: (i, P_Z // w)), vec, blk],
        out_specs=(blk, blk, vec), out_shape=(SDS((s, w), F32), SDS((s, w), BF16), SDS((1, w), F32)), name=name,
        compiler_params=_params(("arbitrary",), 24 * tm * w * 4))(y, proj, g.reshape(1, w), dout)
    return dy, dz, dg.reshape(w)


def _final_loss(x, g, target, *, name):
    s, w = x.shape
    tm = min(s, 512)

    def body(x_ref, g_ref, t_ref, loss_ref, dx_ref, dxb_ref, dg_ref):
        xv = x_ref[...]
        gv = g_ref[...]
        r = lax.rsqrt(jnp.mean(xv * xv, axis=-1, keepdims=True) + EPS)
        xn = xv * r
        diff = xn * gv - t_ref[...]
        dy = diff * (1.0 / w)
        u = dy * gv
        dx = r * u - xv * (r * r * r) * jnp.mean(xv * u, axis=-1, keepdims=True)
        dx_ref[...] = dx
        dxb_ref[...] = dx.astype(BF16)

        @pl.when(pl.program_id(0) == 0)
        def _():
            dg_ref[...] = jnp.zeros_like(dg_ref)
            loss_ref[...] = jnp.zeros_like(loss_ref)

        dg_ref[...] += jnp.sum(dy * xn, axis=0, keepdims=True)
        part = jnp.sum(jnp.sum(diff * diff, axis=1, keepdims=True), axis=0, keepdims=True) * (0.5 / w)
        loss_ref[...] += jnp.broadcast_to(part, loss_ref.shape)

    blk = pl.BlockSpec((tm, w), lambda i: (i, 0))
    vec = pl.BlockSpec((1, w), lambda i: (0, 0))
    loss, dx, dxb, dg = pl.pallas_call(
        body, grid=(s // tm,), in_specs=[blk, vec, blk],
        out_specs=(pl.BlockSpec((1, LANES), lambda i: (0, 0)), blk, blk, vec),
        out_shape=(SDS((1, LANES), F32), SDS((s, w), F32), SDS((s, w), BF16), SDS((1, w), F32)), name=name,
        compiler_params=_params(("arbitrary",), 18 * tm * w * 4))(x, g.reshape(1, w), target)
    return loss[0, 0], dx, dxb, dg.reshape(w)


def _shift_down(x, k):
    if k == 0:
        return x
    row = lax.broadcasted_iota(jnp.int32, x.shape, 0)
    return jnp.where(row < k, 0.0, pltpu.roll(x, k, axis=0))


def _shift_up(x, k):
    if k == 0:
        return x
    s = x.shape[0]
    row = lax.broadcasted_iota(jnp.int32, x.shape, 0)
    return jnp.where(row >= s - k, 0.0, pltpu.roll(x, s - k, axis=0))


def _conv_pre(x, w, b, kw):
    pre = b
    for j in range(kw):
        pre = pre + w[j:j + 1, :] * _shift_down(x, kw - 1 - j)
    return pre


def _conv_bwd_terms(x, w, dpre, kw):
    dx = jnp.zeros_like(x)
    dws = []
    for j in range(kw):
        dx = dx + w[j:j + 1, :] * _shift_up(dpre, kw - 1 - j)
        dws.append(jnp.sum(dpre * _shift_down(x, kw - 1 - j), axis=0, keepdims=True))
    return dx, jnp.concatenate(dws, axis=0), jnp.sum(dpre, axis=0, keepdims=True)


def _ssm_conv_fwd(proj, w, b, *, name):
    s = proj.shape[0]
    cw = 256

    def body(x_ref, w_ref, b_ref, o_ref):
        pre = _conv_pre(x_ref[...], w_ref[...], b_ref[...], SSM_CONV)
        o_ref[...] = pre * _sigmoid(pre)

    return pl.pallas_call(
        body, grid=(CONV_CH // cw,),
        in_specs=[pl.BlockSpec((s, cw), lambda j: (0, j)), pl.BlockSpec((SSM_CONV, cw), lambda j: (0, j)),
                  pl.BlockSpec((1, cw), lambda j: (0, j))],
        out_specs=pl.BlockSpec((s, cw), lambda j: (0, j)), out_shape=SDS((s, CONV_CH), F32), name=name,
        compiler_params=_params(("parallel",), 12 * s * cw * 4))(proj, w, b.reshape(1, CONV_CH))


def _ssm_conv_bwd(proj, w, b, dxbc, *, name):
    s = proj.shape[0]
    cw = 256

    def body(x_ref, w_ref, b_ref, dy_ref, dx_ref, dw_ref, db_ref):
        x = x_ref[...]
        wv = w_ref[...]
        pre = _conv_pre(x, wv, b_ref[...], SSM_CONV)
        sg = _sigmoid(pre)
        dpre = dy_ref[...] * (sg * (1.0 + pre * (1.0 - sg)))
        dx, dw, db = _conv_bwd_terms(x, wv, dpre, SSM_CONV)
        dx_ref[...] = dx.astype(dx_ref.dtype)
        dw_ref[...] = dw
        db_ref[...] = db

    col = pl.BlockSpec((s, cw), lambda j: (0, j))
    wsp = pl.BlockSpec((SSM_CONV, cw), lambda j: (0, j))
    bsp = pl.BlockSpec((1, cw), lambda j: (0, j))
    dx, dw, db = pl.pallas_call(
        body, grid=(CONV_CH // cw,), in_specs=[col, wsp, bsp, col], out_specs=(col, wsp, bsp),
        out_shape=(SDS((s, CONV_CH), BF16), SDS((SSM_CONV, CONV_CH), F32), SDS((1, CONV_CH), F32)), name=name,
        compiler_params=_params(("parallel",), 20 * s * cw * 4))(proj, w, b.reshape(1, CONV_CH), dxbc)
    return dx, dw, db.reshape(CONV_CH)


def _ffn_conv_fwd(up_g, up_v, w, b, *, name):
    s = up_g.shape[0]
    cw = 256
    nb = D_FF // cw

    def body(g_ref, v_ref, wg_ref, wv_ref, bg_ref, bv_ref, o_ref):
        gate = _conv_pre(g_ref[...], wg_ref[...], bg_ref[...], FFN_CONV)
        val = _conv_pre(v_ref[...], wv_ref[...], bv_ref[...], FFN_CONV)
        o_ref[...] = (gate * _sigmoid(gate) * val).astype(o_ref.dtype)

    col = pl.BlockSpec((s, cw), lambda j: (0, j))
    b2 = b.reshape(1, 2 * D_FF)
    return pl.pallas_call(
        body, grid=(nb,),
        in_specs=[col, col, pl.BlockSpec((FFN_CONV, cw), lambda j: (0, j)), pl.BlockSpec((FFN_CONV, cw), lambda j: (0, j + nb)),
                  pl.BlockSpec((1, cw), lambda j: (0, j)), pl.BlockSpec((1, cw), lambda j: (0, j + nb))],
        out_specs=col, out_shape=SDS((s, D_FF), BF16), name=name,
        compiler_params=_params(("parallel",), 16 * s * cw * 4))(up_g, up_v, w, w, b2, b2)


def _ffn_conv_bwd(up_g, up_v, w, b, dact, *, name):
    s = up_g.shape[0]
    cw = 256
    nb = D_FF // cw

    def body(g_ref, v_ref, wg_ref, wv_ref, bg_ref, bv_ref, da_ref, dg_ref, dv_ref, dwg_ref, dwv_ref, dbg_ref, dbv_ref):
        xg, xv = g_ref[...], v_ref[...]
        wg, wv = wg_ref[...], wv_ref[...]
        gate = _conv_pre(xg, wg, bg_ref[...], FFN_CONV)
        val = _conv_pre(xv, wv, bv_ref[...], FFN_CONV)
        da = da_ref[...].astype(F32)
        sg = _sigmoid(gate)
        dgate = da * val * (sg * (1.0 + gate * (1.0 - sg)))
        dval = da * gate * sg
        dxg, dwg, dbg = _conv_bwd_terms(xg, wg, dgate, FFN_CONV)
        dxv, dwv, dbv = _conv_bwd_terms(xv, wv, dval, FFN_CONV)
        dg_ref[...] = dxg.astype(dg_ref.dtype)
        dv_ref[...] = dxv.astype(dv_ref.dtype)
        dwg_ref[...] = dwg
        dwv_ref[...] = dwv
        dbg_ref[...] = dbg
        dbv_ref[...] = dbv

    col = pl.BlockSpec((s, cw), lambda j: (0, j))
    wsp = pl.BlockSpec((FFN_CONV, cw), lambda j: (0, j))
    bsp = pl.BlockSpec((1, cw), lambda j: (0, j))
    b2 = b.reshape(1, 2 * D_FF)
    dg, dv, dwg, dwv, dbg, dbv = pl.pallas_call(
        body, grid=(nb,),
        in_specs=[col, col, wsp, pl.BlockSpec((FFN_CONV, cw), lambda j: (0, j + nb)), bsp,
                  pl.BlockSpec((1, cw), lambda j: (0, j + nb)), col],
        out_specs=(col, col, wsp, wsp, bsp, bsp),
        out_shape=(SDS((s, D_FF), BF16), SDS((s, D_FF), BF16), SDS((FFN_CONV, D_FF), F32), SDS((FFN_CONV, D_FF), F32),
                   SDS((1, D_FF), F32), SDS((1, D_FF), F32)), name=name,
        compiler_params=_params(("parallel",), 32 * s * cw * 4))(up_g, up_v, w, w, b2, b2, dact)
    return dg, dv, jnp.concatenate([dwg, dwv], axis=1), jnp.concatenate([dbg, dbv], axis=1).reshape(2 * D_FF)


def _dot(a, b):
    return jnp.dot(a.astype(BF16), b.astype(BF16), preferred_element_type=F32)


def _dot_nt(a, b):
    return lax.dot_general(a.astype(BF16), b.astype(BF16), (((1,), (1,)), ((), ())), preferred_element_type=F32)


def _dot_tn(a, b):
    return lax.dot_general(a.astype(BF16), b.astype(BF16), (((0,), (0,)), ((), ())), preferred_element_type=F32)


def _ssd_chunk_terms(dtraw, bias, a_log):
    ell = dtraw.shape[0]
    lane = lax.broadcasted_iota(jnp.int32, dtraw.shape, 1)
    valid = lane < SSM_HEADS
    pre = dtraw + bias
    dt = jnp.where(valid, jnp.where(pre > 20.0, pre, jnp.log(1.0 + jnp.exp(jnp.minimum(pre, 20.0)))), 0.0)
    a = -jnp.exp(a_log)
    ad = dt * a
    row = lax.broadcasted_iota(jnp.int32, (ell, ell), 0)
    colm = lax.broadcasted_iota(jnp.int32, (ell, ell), 1)
    tril = row >= colm
    cs = jnp.dot(tril.astype(F32), ad, precision=HIGHEST, preferred_element_type=F32)
    cs_last = cs[ell - 1:ell, :]
    return pre, dt, a, cs, cs_last, tril


def _head_expand():
    h = lax.broadcasted_iota(jnp.int32, (LANES, D_SSM), 0)
    c = lax.broadcasted_iota(jnp.int32, (LANES, D_SSM), 1)
    return (c // SSM_HEAD_DIM == h).astype(F32)


def _ssd_fwd(xbc, proj, dt_bias, a_log, d_skip, *, name):
    s = xbc.shape[0]
    nc = s // SSM_CHUNK
    ell, n, p = SSM_CHUNK, SSM_STATE, SSM_HEAD_DIM
    rpg = SSM_HEADS // SSM_GROUPS
    gw = rpg * p

    def body(x_ref, dt_ref, bias_ref, alog_ref, dskip_ref, ex_ref, y_ref, ps_ref, state):
        @pl.when(pl.program_id(0) == 0)
        def _():
            state[...] = jnp.zeros_like(state)

        _, dt, _, cs, cs_last, tril = _ssd_chunk_terms(dt_ref[...], bias_ref[...], alog_ref[...])
        cst = cs.T
        ex = ex_ref[...]
        spread = lambda v: jnp.dot(v, ex, precision=HIGHEST, preferred_element_type=F32)
        dt_x, e_x, ds_x = spread(dt), spread(jnp.exp(cs)), spread(jnp.exp(cs_last - cs))
        cd_x = spread(jnp.broadcast_to(jnp.exp(cs_last), (8, LANES)))[0:1, :]
        dskip_x = spread(jnp.broadcast_to(dskip_ref[...], (8, LANES)))[0:1, :]
        st = state[...]
        ps_ref[0] = st
        xv = x_ref[...]
        xs_all = xv[:, 0:D_SSM]
        xd_all = xs_all * dt_x
        xdd_all = xd_all * ds_x
        lane_g = lax.broadcasted_iota(jnp.int32, (ell, gw), 1)
        ys, new = [], []
        for g in range(SSM_GROUPS):
            gs = slice(gw * g, gw * (g + 1))
            bg = xv[:, D_SSM + n * g:D_SSM + n * (g + 1)]
            cg = xv[:, D_SSM + n * (SSM_GROUPS + g):D_SSM + n * (SSM_GROUPS + g + 1)]
            cb = _dot_nt(cg, bg)
            xd_g, prev_g = xd_all[:, gs], st[:, gs]
            y_g = _dot(cg, prev_g) * e_x[:, gs] + xs_all[:, gs] * dskip_x[:, gs]
            for r in range(rpg):
                h = g * rpg + r
                lmat = jnp.exp(jnp.where(tril, cs[:, h:h + 1] - cst[h:h + 1, :], -jnp.inf))
                y_g = y_g + jnp.where((lane_g >= p * r) & (lane_g < p * (r + 1)), _dot(cb * lmat, xd_g), 0.0)
            ys.append(y_g)
            new.append(prev_g * cd_x[:, gs] + _dot(bg.T, xdd_all[:, gs]))
        y_ref[...] = jnp.concatenate(ys, axis=1)
        state[...] = jnp.concatenate(new, axis=1)

    vec = pl.BlockSpec((1, LANES), lambda c: (0, 0))
    return pl.pallas_call(
        body, grid=(nc,),
        in_specs=[pl.BlockSpec((ell, CONV_CH), lambda c: (c, 0)), pl.BlockSpec((ell, LANES), lambda c: (c, P_DT // LANES)),
                  vec, vec, vec, pl.BlockSpec((LANES, D_SSM), lambda c: (0, 0))],
        out_specs=(pl.BlockSpec((ell, D_SSM), lambda c: (c, 0)), pl.BlockSpec((1, n, D_SSM), lambda c: (c, 0, 0))),
        out_shape=(SDS((s, D_SSM), F32), SDS((nc, n, D_SSM), F32)),
        scratch_shapes=[pltpu.VMEM((n, D_SSM), F32)], name=name,
        compiler_params=_params(("arbitrary",), 32 << 20))(xbc, proj, dt_bias, a_log, d_skip, _head_expand())


def _ssd_bwd(xbc, proj, dt_bias, a_log, d_skip, prev_states, dy, *, name):
    s = xbc.shape[0]
    nc = s // SSM_CHUNK
    ell, n, p = SSM_CHUNK, SSM_STATE, SSM_HEAD_DIM
    rpg = SSM_HEADS // SSM_GROUPS
    gw = rpg * p

    def body(x_ref, dt_ref, bias_ref, alog_ref, dskip_ref, ps_ref, dy_ref, ex_ref, ext_ref,
             dx_ref, ddt_ref, dalog_ref, ddskip_ref, dbias_ref, dstate):
        @pl.when(pl.program_id(0) == 0)
        def _():
            dstate[...] = jnp.zeros_like(dstate)
            dalog_ref[...] = jnp.zeros_like(dalog_ref)
            ddskip_ref[...] = jnp.zeros_like(ddskip_ref)
            dbias_ref[...] = jnp.zeros_like(dbias_ref)

        pre, dt, a, cs, cs_last, tril = _ssd_chunk_terms(dt_ref[...], bias_ref[...], alog_ref[...])
        e = jnp.exp(cs)
        ds = jnp.exp(cs_last - cs)
        cd = jnp.exp(cs_last)
        cst = cs.T
        shape = (ell, LANES)
        ex, ext = ex_ref[...], ext_ref[...]
        spread = lambda v: jnp.dot(v, ex, precision=HIGHEST, preferred_element_type=F32)
        gather = lambda v: jnp.dot(v, ext, precision=HIGHEST, preferred_element_type=F32)
        dt_x, e_x, ds_x = spread(dt), spread(e), spread(ds)
        cd_x = spread(jnp.broadcast_to(cd, (8, LANES)))[0:1, :]
        dskip_x = spread(jnp.broadcast_to(dskip_ref[...], (8, LANES)))[0:1, :]
        xv, dyv, psv, dst = x_ref[...], dy_ref[...], ps_ref[0], dstate[...]
        xs_all = xv[:, 0:D_SSM]
        xd_all = xs_all * dt_x
        dye_all = dyv * e_x
        xdd_all = xd_all * ds_x
        triu = lax.broadcasted_iota(jnp.int32, (ell, ell), 0) <= lax.broadcasted_iota(jnp.int32, (ell, ell), 1)
        lane_g = lax.broadcasted_iota(jnp.int32, (ell, gw), 1)
        lane = lax.broadcasted_iota(jnp.int32, shape, 1)
        sub = lax.broadcasted_iota(jnp.int32, shape, 0)
        dcs_acc = jnp.zeros(shape, F32)
        dcs_rows = jnp.zeros(shape, F32)
        dxs, dbs, dcs_parts, dprevs, prod_a, prod_b, prod_c, prod_e = [], [], [], [], [], [], [], []
        for g in range(SSM_GROUPS):
            gs = slice(gw * g, gw * (g + 1))
            bg = xv[:, D_SSM + n * g:D_SSM + n * (g + 1)]
            cg = xv[:, D_SSM + n * (SSM_GROUPS + g):D_SSM + n * (SSM_GROUPS + g + 1)]
            cb = _dot_nt(cg, bg)
            cbt = _dot_nt(bg, cg)
            xs_g, dy_g, xd_g, dye_g, xdd_g = xs_all[:, gs], dyv[:, gs], xd_all[:, gs], dye_all[:, gs], xdd_all[:, gs]
            prev_g, dsn_g = psv[:, gs], dst[:, gs]
            cprev_g = _dot(cg, prev_g)
            dprevs.append(dsn_g * cd_x[:, gs] + _dot(cg.T, dye_g))
            dcg = _dot_nt(dye_g, prev_g)
            dxdd_g = _dot(bg, dsn_g)
            dbg = _dot_nt(xdd_g, dsn_g)
            dxd_g = dxdd_g * ds_x[:, gs]
            prod_a.append(dy_g * cprev_g)
            prod_b.append(dxdd_g * xd_g)
            prod_e.append(jnp.sum(dsn_g * prev_g, axis=0, keepdims=True))
            dcb = jnp.zeros((ell, ell), F32)
            for r in range(rpg):
                h = g * rpg + r
                mine = (lane_g >= p * r) & (lane_g < p * (r + 1))
                lmat = jnp.exp(jnp.where(tril, cs[:, h:h + 1] - cst[h:h + 1, :], -jnp.inf))
                lmat_t = jnp.exp(jnp.where(triu, cst[h:h + 1, :] - cs[:, h:h + 1], -jnp.inf))
                dgm = _dot_nt(jnp.where(mine, dy_g, 0.0), xd_g)
                dxd_g = dxd_g + jnp.where(mine, _dot(cbt * lmat_t, dy_g), 0.0)
                mm = dgm * (cb * lmat)
                dcs_acc = dcs_acc + jnp.where(lane == h, jnp.sum(mm, axis=1, keepdims=True), 0.0)
                dcs_rows = dcs_rows + jnp.where(sub == h, jnp.sum(mm, axis=0, keepdims=True), 0.0)
                dcb = dcb + dgm * lmat
            dxs.append(dxd_g * dt_x[:, gs] + dy_g * dskip_x[:, gs])
            prod_c.append(dxd_g * xs_g)
            dbs.append(dbg + _dot_tn(dcb, cg))
            dcs_parts.append(dcg + _dot(dcb, bg))
        dx_ref[...] = jnp.concatenate(dxs + dbs + dcs_parts, axis=1)
        dstate[...] = jnp.concatenate(dprevs, axis=1)
        sum_a = gather(jnp.concatenate(prod_a, axis=1))
        sum_b = gather(jnp.concatenate(prod_b, axis=1))
        sum_c = gather(jnp.concatenate(prod_c, axis=1))
        sum_d = gather(dyv * xs_all)
        dcd = gather(jnp.broadcast_to(jnp.concatenate(prod_e, axis=1), (8, D_SSM)))[0:1, :]
        tmp = sum_b * ds
        dlast = dcd * cd + jnp.sum(tmp, axis=0, keepdims=True)
        dcs = dcs_acc + sum_a * e - tmp - dcs_rows.T + jnp.where(sub == ell - 1, dlast, 0.0)
        dad = jnp.dot(triu.astype(F32), dcs, precision=HIGHEST, preferred_element_type=F32)
        ddt = sum_c + dad * a
        dalog_ref[...] += jnp.sum(dad * dt, axis=0, keepdims=True) * a
        ddskip_ref[...] += jnp.sum(sum_d, axis=0, keepdims=True)
        ddraw = jnp.where(lane < SSM_HEADS, ddt * _sigmoid(pre), 0.0)
        ddt_ref[...] = ddraw.astype(ddt_ref.dtype)
        dbias_ref[...] += jnp.sum(ddraw, axis=0, keepdims=True)

    vec = pl.BlockSpec((1, LANES), lambda c: (0, 0))
    rev = lambda c: nc - 1 - c
    ex = _head_expand()
    outs = pl.pallas_call(
        body, grid=(nc,),
        in_specs=[pl.BlockSpec((ell, CONV_CH), lambda c: (rev(c), 0)),
                  pl.BlockSpec((ell, LANES), lambda c: (rev(c), P_DT // LANES)), vec, vec, vec,
                  pl.BlockSpec((1, n, D_SSM), lambda c: (rev(c), 0, 0)),
                  pl.BlockSpec((ell, D_SSM), lambda c: (rev(c), 0)),
                  pl.BlockSpec((LANES, D_SSM), lambda c: (0, 0)), pl.BlockSpec((D_SSM, LANES), lambda c: (0, 0))],
        out_specs=(pl.BlockSpec((ell, CONV_CH), lambda c: (rev(c), 0)), pl.BlockSpec((ell, LANES), lambda c: (rev(c), 0)),
                   vec, vec, vec),
        out_shape=(SDS((s, CONV_CH), F32), SDS((s, LANES), BF16), SDS((1, LANES), F32), SDS((1, LANES), F32),
                   SDS((1, LANES), F32)),
        scratch_shapes=[pltpu.VMEM((n, D_SSM), F32)], name=name,
        compiler_params=_params(("arbitrary",), 40 << 20))(xbc, proj, dt_bias, a_log, d_skip, prev_states, dy, ex, ex.T)
    return outs


def _rope_swap(t):
    lane = lax.broadcasted_iota(jnp.int32, t.shape, 1)
    half = QK_ROPE // 2
    lo = (lane >= QK_NOPE) & (lane < QK_NOPE + half)
    hi = (lane >= QK_NOPE + half) & (lane < QK_NOPE + QK_ROPE)
    return jnp.where(lo, pltpu.roll(t, HEAD_PAD - half, axis=1), jnp.where(hi, pltpu.roll(t, half, axis=1), 0.0))


def _mla_prep(q, kv, proj, cos, sins, *, name):
    s = q.shape[0]
    tm = min(s, 256)
    scale = (QK_NOPE + QK_ROPE) ** -0.5

    def body(q_ref, kv_ref, kr_ref, cos_ref, sin_ref, qo_ref, ko_ref, vo_ref):
        cosv, sinv = cos_ref[...], sin_ref[...]
        kr = pltpu.roll(kr_ref[...], QK_NOPE, axis=1)
        lane = lax.broadcasted_iota(jnp.int32, kr.shape, 1)
        nope = lane < QK_NOPE
        kr = jnp.where(nope, 0.0, kr)
        kpe = kr * cosv + _rope_swap(kr) * sinv
        for hp in range(MLA_HEADS // 2):
            vs = []
            for h in (2 * hp, 2 * hp + 1):
                hs = slice(HEAD_PAD * h, HEAD_PAD * (h + 1))
                qh = q_ref[:, hs]
                kvh = kv_ref[:, hs]
                qo_ref[:, hs] = ((qh * cosv + _rope_swap(qh) * sinv) * scale).astype(qo_ref.dtype)
                ko_ref[:, hs] = (jnp.where(nope, kvh, 0.0) + kpe).astype(ko_ref.dtype)
                vs.append(kvh[:, QK_NOPE:])
            vo_ref[:, 2 * V_DIM * hp:2 * V_DIM * (hp + 1)] = jnp.concatenate(vs, axis=1).astype(vo_ref.dtype)

    wide = pl.BlockSpec((tm, MLA_HEADS * HEAD_PAD), lambda i: (i, 0))
    half = pl.BlockSpec((tm, MLA_HEADS * V_DIM), lambda i: (i, 0))
    tab = pl.BlockSpec((tm, LANES), lambda i: (i, 0))
    return pl.pallas_call(
        body, grid=(s // tm,),
        in_specs=[wide, wide, pl.BlockSpec((tm, LANES), lambda i: (i, P_KR // LANES)), tab, tab],
        out_specs=(wide, wide, half),
        out_shape=(SDS((s, MLA_HEADS * HEAD_PAD), BF16), SDS((s, MLA_HEADS * HEAD_PAD), BF16),
                   SDS((s, MLA_HEADS * V_DIM), BF16)), name=name,
        compiler_params=_params(("parallel",), 32 << 20))(q, kv, proj, cos, sins)


def _mla_prep_bwd(dqr, dkr, dv, cos, sins, *, name):
    s = dqr.shape[0]
    tm = min(s, 256)
    scale = (QK_NOPE + QK_ROPE) ** -0.5

    def body(dq_ref, dk_ref, dv_ref, cos_ref, sin_ref, dqo_ref, dkv_ref, dkr_ref):
        cosv, sinv = cos_ref[...], sin_ref[...]
        lane = lax.broadcasted_iota(jnp.int32, cosv.shape, 1)
        ksum = jnp.zeros(cosv.shape, F32)
        for h in range(MLA_HEADS):
            hs = slice(HEAD_PAD * h, HEAD_PAD * (h + 1))
            d = dq_ref[:, hs]
            dk = dk_ref[:, hs]
            dqo_ref[:, hs] = ((d * cosv + _rope_swap(d * sinv)) * scale).astype(dqo_ref.dtype)
            dkv_ref[:, hs] = jnp.concatenate([dk[:, :QK_NOPE], dv_ref[:, V_DIM * h:V_DIM * (h + 1)]], axis=1).astype(dkv_ref.dtype)
            ksum = ksum + dk
        ksum = jnp.where((lane >= QK_NOPE) & (lane < QK_NOPE + QK_ROPE), ksum, 0.0)
        un = ksum * cosv + _rope_swap(ksum * sinv)
        dkr_ref[...] = pltpu.roll(un, HEAD_PAD - QK_NOPE, axis=1).astype(dkr_ref.dtype)

    wide = pl.BlockSpec((tm, MLA_HEADS * HEAD_PAD), lambda i: (i, 0))
    half = pl.BlockSpec((tm, MLA_HEADS * V_DIM), lambda i: (i, 0))
    tab = pl.BlockSpec((tm, LANES), lambda i: (i, 0))
    return pl.pallas_call(
        body, grid=(s // tm,), in_specs=[wide, wide, half, tab, tab], out_specs=(wide, wide, tab),
        out_shape=(SDS((s, MLA_HEADS * HEAD_PAD), BF16), SDS((s, MLA_HEADS * HEAD_PAD), BF16), SDS((s, LANES), BF16)),
        name=name, compiler_params=_params(("parallel",), 40 << 20))(dqr, dkr, dv, cos, sins)


FLASH_TILE = 512
FLASH_ROWS = 32


def _flash_fwd(q, k, v, *, name):
    s = q.shape[0]
    t = min(s, FLASH_TILE)
    nq = s // t
    npair = MLA_HEADS // 2

    def body(q_ref, k_ref, v_ref, o_ref, lse_ref):
        i = pl.program_id(1)
        qs = [q_ref[:, HEAD_PAD * e:HEAD_PAD * (e + 1)] for e in range(2)]
        diag = lax.broadcasted_iota(jnp.int32, (t, t), 0) >= lax.broadcasted_iota(jnp.int32, (t, t), 1)

        def step(j, carry, masked):
            rows = pl.ds(pl.multiple_of(j * t, t), t)
            new = []
            for e in range(2):
                m, l, acc = carry[e]
                sc = _dot_nt(qs[e], k_ref[rows, HEAD_PAD * e:HEAD_PAD * (e + 1)])
                if masked:
                    sc = jnp.where(diag, sc, NEG)
                m_new = jnp.maximum(m, jnp.max(sc, axis=1, keepdims=True))
                pr = jnp.exp(sc - m_new)
                alpha = jnp.exp(m - m_new)
                l = alpha * l + jnp.sum(pr, axis=1, keepdims=True)
                acc = alpha * acc + _dot(pr, v_ref[rows, V_DIM * e:V_DIM * (e + 1)])
                new.append((m_new, l, acc))
            return tuple(new)

        init = tuple((jnp.full((t, 1), NEG, F32), jnp.zeros((t, 1), F32), jnp.zeros((t, V_DIM), F32)) for _ in range(2))
        carry = lax.fori_loop(0, i, functools.partial(step, masked=False), init)
        carry = step(i, carry, True)
        o_ref[...] = jnp.concatenate([acc / l for _, l, acc in carry], axis=1)
        lse_ref[0] = jnp.concatenate([jnp.broadcast_to(m + jnp.log(l), (t, V_DIM)) for m, l, _ in carry], axis=1)

    return pl.pallas_call(
        body, grid=(npair, nq),
        in_specs=[pl.BlockSpec((t, 2 * HEAD_PAD), lambda hp, i: (i, hp)), pl.BlockSpec((s, 2 * HEAD_PAD), lambda hp, i: (0, hp)),
                  pl.BlockSpec((s, 2 * V_DIM), lambda hp, i: (0, hp))],
        out_specs=(pl.BlockSpec((t, 2 * V_DIM), lambda hp, i: (i, hp)), pl.BlockSpec((1, t, LANES), lambda hp, i: (hp, i, 0))),
        out_shape=(SDS((s, MLA_HEADS * V_DIM), F32), SDS((npair, s, LANES), F32)), name=name,
        compiler_params=_params(("parallel", "parallel"), 40 << 20))(q, k, v)


def _flash_bwd(q, k, v, o, lse, do, *, name):
    s = q.shape[0]
    t = min(s, FLASH_TILE)
    nq = s // t
    npair = MLA_HEADS // 2
    nchunk = t // FLASH_ROWS

    def valid_cols(r):
        return min(t, -(-((r + 1) * FLASH_ROWS) // LANES) * LANES)

    def body(q_ref, k_ref, v_ref, o_ref, lse_ref, do_ref, dq_ref, dk_ref, dv_ref, s_scr, dp_scr, p_scr, ds_scr, dk_acc, dv_acc):
        j = pl.program_id(1)

        @pl.when(j == 0)
        def _():
            dq_ref[...] = jnp.zeros_like(dq_ref)

        dk_acc[...] = jnp.zeros(dk_acc.shape, F32)
        dv_acc[...] = jnp.zeros(dv_acc.shape, F32)
        qsl = [slice(HEAD_PAD * e, HEAD_PAD * (e + 1)) for e in range(2)]
        vsl = [slice(V_DIM * e, V_DIM * (e + 1)) for e in range(2)]

        def step(i, carry, masked):
            rows = pl.ds(pl.multiple_of(i * t, t), t)
            for e in range(2):
                ke = k_ref[:, qsl[e]]
                qi = q_ref[rows, qsl[e]]
                doi = do_ref[rows, vsl[e]]
                delta = jnp.sum(doi * o_ref[rows, vsl[e]], axis=1, keepdims=True)
                lse_i = lse_ref[0, rows, vsl[e]][:, 0:1]
                dob = doi.astype(BF16)
                s_scr[e] = _dot_nt(qi, ke)
                dp_scr[e] = _dot_nt(dob, v_ref[:, vsl[e]])
                for r in range(nchunk):
                    rs = slice(r * FLASH_ROWS, (r + 1) * FLASH_ROWS)
                    width = valid_cols(r) if masked else t
                    sc = s_scr[e, rs, 0:width]
                    if masked:
                        row = r * FLASH_ROWS + lax.broadcasted_iota(jnp.int32, (FLASH_ROWS, width), 0)
                        sc = jnp.where(row >= lax.broadcasted_iota(jnp.int32, (FLASH_ROWS, width), 1), sc, NEG)
                    pr = jnp.exp(sc - lse_i[rs, :])
                    dsc = pr * (dp_scr[e, rs, 0:width] - delta[rs, :])
                    p_scr[e, rs, 0:width] = pr.astype(BF16)
                    ds_scr[e, rs, 0:width] = dsc.astype(BF16)
                    if width < t:
                        p_scr[e, rs, width:t] = jnp.zeros((FLASH_ROWS, t - width), BF16)
                        ds_scr[e, rs, width:t] = jnp.zeros((FLASH_ROWS, t - width), BF16)
                dv_acc[e] += _dot_tn(p_scr[e], dob)
                dk_acc[e] += _dot_tn(ds_scr[e], qi)
                dq_ref[rows, qsl[e]] += _dot(ds_scr[e], ke)
            return carry

        step(j, 0, True)
        lax.fori_loop(j + 1, nq, functools.partial(step, masked=False), 0)
        dk_ref[...] = jnp.concatenate([dk_acc[e] for e in range(2)], axis=1)
        dv_ref[...] = jnp.concatenate([dv_acc[e] for e in range(2)], axis=1)

    full_q = pl.BlockSpec((s, 2 * HEAD_PAD), lambda hp, j: (0, hp))
    full_v = pl.BlockSpec((s, 2 * V_DIM), lambda hp, j: (0, hp))
    blk_k = pl.BlockSpec((t, 2 * HEAD_PAD), lambda hp, j: (j, hp))
    blk_v = pl.BlockSpec((t, 2 * V_DIM), lambda hp, j: (j, hp))
    return pl.pallas_call(
        body, grid=(npair, nq),
        in_specs=[full_q, blk_k, blk_v, full_v, pl.BlockSpec((1, s, LANES), lambda hp, j: (hp, 0, 0)), full_v],
        out_specs=(full_q, blk_k, blk_v),
        out_shape=(SDS((s, MLA_HEADS * HEAD_PAD), F32), SDS((s, MLA_HEADS * HEAD_PAD), F32), SDS((s, MLA_HEADS * V_DIM), F32)),
        scratch_shapes=[pltpu.VMEM((2, t, t), F32), pltpu.VMEM((2, t, t), F32), pltpu.VMEM((2, t, t), BF16),
                        pltpu.VMEM((2, t, t), BF16), pltpu.VMEM((2, t, HEAD_PAD), F32), pltpu.VMEM((2, t, V_DIM), F32)],
        name=name, compiler_params=_params(("parallel", "arbitrary"), 48 << 20))(q, k, v, o, lse, do)


def _mem_attn_fwd(q, k, v, *, name):
    s = q.shape[0]
    tm = min(s, 512)
    ml = k.shape[0]
    scale = MEM_HEAD_DIM ** -0.5

    def body(q_ref, k_ref, v_ref, o_ref):
        for h in range(MEM_HEADS):
            hs = slice(MEM_HEAD_DIM * h, MEM_HEAD_DIM * (h + 1))
            sc = _dot_nt(q_ref[:, hs], k_ref[:, hs]) * scale
            pr = jnp.exp(sc - jnp.max(sc, axis=1, keepdims=True))
            pr = pr / jnp.sum(pr, axis=1, keepdims=True)
            o_ref[:, hs] = _dot(pr, v_ref[:, hs]).astype(o_ref.dtype)

    blk = pl.BlockSpec((tm, D_MODEL), lambda i: (i, 0))
    kv = pl.BlockSpec((ml, D_MODEL), lambda i: (0, 0))
    return pl.pallas_call(body, grid=(s // tm,), in_specs=[blk, kv, kv], out_specs=blk,
                          out_shape=SDS((s, D_MODEL), BF16), name=name,
                          compiler_params=_params(("parallel",), 24 << 20))(q, k, v)


def _mem_attn_bwd(q, k, v, do, *, name):
    s = q.shape[0]
    tm = min(s, 512)
    ml = k.shape[0]
    scale = MEM_HEAD_DIM ** -0.5

    def body(q_ref, k_ref, v_ref, do_ref, dq_ref, dk_ref, dv_ref):
        @pl.when(pl.program_id(0) == 0)
        def _():
            dk_ref[...] = jnp.zeros_like(dk_ref)
            dv_ref[...] = jnp.zeros_like(dv_ref)

        for h in range(MEM_HEADS):
            hs = slice(MEM_HEAD_DIM * h, MEM_HEAD_DIM * (h + 1))
            qh, kh, vh, doh = q_ref[:, hs], k_ref[:, hs], v_ref[:, hs], do_ref[:, hs]
            sc = _dot_nt(qh, kh) * scale
            pr = jnp.exp(sc - jnp.max(sc, axis=1, keepdims=True))
            pr = pr / jnp.sum(pr, axis=1, keepdims=True)
            dp = _dot_nt(doh, vh)
            dsc = pr * (dp - jnp.sum(pr * dp, axis=1, keepdims=True)) * scale
            dq_ref[:, hs] = _dot(dsc, kh).astype(dq_ref.dtype)
            dk_ref[:, hs] += _dot_tn(dsc, qh)
            dv_ref[:, hs] += _dot_tn(pr, doh)

    blk = pl.BlockSpec((tm, D_MODEL), lambda i: (i, 0))
    kv = pl.BlockSpec((ml, D_MODEL), lambda i: (0, 0))
    return pl.pallas_call(body, grid=(s // tm,), in_specs=[blk, kv, kv, blk], out_specs=(blk, kv, kv),
                          out_shape=(SDS((s, D_MODEL), BF16), SDS((ml, D_MODEL), F32), SDS((ml, D_MODEL), F32)), name=name,
                          compiler_params=_params(("arbitrary",), 32 << 20))(q, k, v, do)


MATS = (("w_in", (1024, 940), 1), ("w_uq", (384, 384), 1), ("w_ukv", (256, 512), 1), ("w_out", (512, 1024), 0),
        ("ssm_conv_w", (4, 512), 1),
        ("w_mq", (256, 1024), 0), ("w_mk", (256, 1024), 0), ("w_mv", (256, 1024), 0), ("w_mo", (256, 1024), 0),
        ("w_up", (1024, 1408), 1), ("w_down", (704, 1024), 0), ("ffn_conv_w", (3, 1408), 1))
GROUPS = {"proj": (0,), "mixer": (1, 2, 3, 4), "mem": (5, 6, 7, 8), "ffn": (9, 10, 11)}
UP_SHARD_COLS = 1408
F32_ON_WIRE = ("ssm_conv_w", "ffn_conv_w")
SMALL = (("norm_mix", 1024), ("ssm_conv_b", 2048), ("dt_bias", 16), ("a_log", 16), ("d_skip", 16), ("ssm_norm", 1024),
         ("q_norm", 384), ("kv_norm", 256), ("attn_out_norm", 1024), ("norm_mem_q", 1024), ("norm_mem_kv", 1024),
         ("norm_ffn", 1024), ("ffn_conv_b", 5632))
PACK_COLS = 1024


def _pad_cols(t, n):
    return jnp.pad(t, ((0, 0),) * (t.ndim - 1) + ((0, n - t.shape[-1]),))


def _w_in_to_padded(t):
    z, xbc, dt, cq, ckv, kr = jnp.split(t, (1024, 3072, 3088, 3472, 3728), axis=-1)
    return jnp.concatenate([xbc, z, cq, _pad_cols(dt, LANES), ckv, _pad_cols(kr, P_IN - P_KR)], axis=-1)


def _w_in_from_padded(t):
    return jnp.concatenate([t[..., P_Z:P_Z + 1024], t[..., P_XBC:P_XBC + 2048], t[..., P_DT:P_DT + SSM_HEADS],
                            t[..., P_CQ:P_CQ + Q_LORA], t[..., P_CKV:P_CKV + KV_LORA], t[..., P_KR:P_KR + QK_ROPE]], axis=-1)


def _cols_joined(g):
    return jnp.concatenate([g[j] for j in range(N_CHIPS)], axis=-1)


def _cols_by_chip(t, dtype):
    k = t.shape[0]
    return t.reshape(k, N_CHIPS, -1).transpose(1, 0, 2).astype(dtype)


def _rows_by_chip(t):
    return t.reshape(N_CHIPS, -1, t.shape[-1])


def _mixer_weights(gw):
    wl = {}
    uq = _cols_joined(gw["w_uq"]).reshape(Q_LORA, MLA_HEADS, QK_NOPE + QK_ROPE)
    wl["w_uq"] = _pad_cols(uq, HEAD_PAD).reshape(Q_LORA, MLA_HEADS * HEAD_PAD)
    wl["w_ukv"] = _cols_joined(gw["w_ukv"])
    wl["ssm_conv_w"] = _cols_joined(gw["ssm_conv_w"])
    return wl


def _layer_fwd(x0, mem, cos, sins, weights, sp, li):
    n = lambda t: f"l{li}_{t}"
    lead = ()
    sv = {"x0": x0}
    gw = dict(weights("proj", x0))
    w_in = _w_in_to_padded(_cols_joined(gw["w_in"]))
    h = _rms_fwd(x0, sp["norm_mix"], name=n("mix_norm"))
    in_hbm = lambda t: pltpu.with_memory_space_constraint(t, pltpu.HBM)
    proj = in_hbm(_mm(h, w_in, name=n("mix_proj")))
    gw.update(weights("mixer", proj))
    wl = dict(_mixer_weights(gw), w_in=w_in)
    xbc = in_hbm(_ssm_conv_fwd(proj, wl["ssm_conv_w"], sp["ssm_conv_b"], name=n("ssm_conv")))
    y, pstates = _ssd_fwd(xbc, proj, sp["dt_bias"], sp["a_log"], sp["d_skip"], name=n("ssd"))
    y = in_hbm(y)
    y_ssm = _gated_rms_fwd(y, proj, sp["ssm_norm"], name=n("ssm_gate"))
    cqn = _rms_fwd(proj, sp["q_norm"], col=(Q_LORA, P_CQ // Q_LORA), name=n("q_norm"))
    ckvn = _rms_fwd(proj, sp["kv_norm"], col=(KV_LORA, P_CKV // KV_LORA), name=n("kv_norm"))
    q = in_hbm(_mm(cqn, wl["w_uq"], name=n("uq")))
    kv = in_hbm(_mm(ckvn, wl["w_ukv"], name=n("ukv")))
    qr, kr, v = _mla_prep(q, kv, proj, cos, sins, name=n("rope"))
    att, lse = _flash_fwd(qr, kr, v, name=n("flash"))
    att = in_hbm(att)
    y_att = _rms_fwd(att, sp["attn_out_norm"], name=n("att_norm"))
    x1 = _mm(y_ssm, gw["w_out"], b_lead=lead, b_rows=(0, D_SSM), res=x0, name=n("out_a"))
    x1 = in_hbm(_mm(y_att, gw["w_out"], b_lead=lead, b_rows=(D_SSM, D_SSM), res=x1, name=n("out_b")))
    sv.update(h=h, proj=proj, xbc=xbc, y=y, pstates=pstates, y_ssm=y_ssm, cqn=cqn, ckvn=ckvn, qr=qr, kr=kr, v=v,
              att=att, lse=lse, y_att=y_att, x1=x1)
    gw.update(weights("mem", x1))
    hq = _rms_fwd(x1, sp["norm_mem_q"], name=n("memq_norm"))
    hm = _rms_fwd(mem, sp["norm_mem_kv"], name=n("memkv_norm"))
    mq = _mm(hq, gw["w_mq"], b_lead=lead, out_dtype=BF16, name=n("mq"))
    mk = _mm(hm, gw["w_mk"], b_lead=lead, out_dtype=BF16, name=n("mk"))
    mv = _mm(hm, gw["w_mv"], b_lead=lead, out_dtype=BF16, name=n("mv"))
    mo = _mem_attn_fwd(mq, mk, mv, name=n("mem_attn"))
    x2 = in_hbm(_mm(mo, gw["w_mo"], b_lead=lead, res=x1, name=n("mo")))
    sv.update(hq=hq, hm=hm, mq=mq, mk=mk, mv=mv, mo=mo, x2=x2)
    gw.update(weights("ffn", x2))
    wl["ffn_conv_w"] = _cols_joined(gw["ffn_conv_w"])
    hf = _rms_fwd(x2, sp["norm_ffn"], name=n("ffn_norm"))
    up_g = _mm(hf, gw["w_up"], b_lead=lead, b_chips=(0, 2), name=n("up_g"))
    up_v = _mm(hf, gw["w_up"], b_lead=lead, b_chips=(2, 2), name=n("up_v"))
    act = _ffn_conv_fwd(up_g, up_v, wl["ffn_conv_w"], sp["ffn_conv_b"], name=n("ffn_conv"))
    x3 = in_hbm(_mm(act, gw["w_down"], b_lead=lead, res=x2, name=n("down")))
    sv.update(hf=hf, up_g=up_g, up_v=up_v, act=act)
    return x3, sv, gw, wl


def _layer_bwd(dx3, dx3b, mem, cos, sins, gw, wl, sp, sv, li, emit):
    n = lambda t: f"l{li}_b_{t}"
    lead = ()
    g = {}

    def after(token, v):
        return v if token is None else v + token[0, 0]

    in_hbm = lambda t: pltpu.with_memory_space_constraint(t, pltpu.HBM)
    dact = _mm(dx3b, gw["w_down"], tb=True, b_lead=lead, out_dtype=BF16, name=n("down_dx"))
    g["w_down"] = _rows_by_chip(_mm(sv["act"], dx3b, ta=True, out_dtype=BF16, name=n("down_dw")))
    dup_g, dup_v, dcw, g["ffn_conv_b"] = _ffn_conv_bwd(
        sv["up_g"], sv["up_v"], wl["ffn_conv_w"], sp["ffn_conv_b"], dact, name=n("ffn_conv"))
    g["ffn_conv_w"] = _cols_by_chip(dcw, F32)
    nsh = UP_SHARD_COLS
    dhf = None
    for c4 in range(N_CHIPS):
        dhf = _mm(dup_g if c4 < 2 else dup_v, gw["w_up"], tb=True, a_col=(nsh, c4 % 2), b_lead=(c4,), res=dhf,
                  name=n(f"up{c4}_dx"))
    g["w_up"] = jnp.concatenate([_mm(sv["hf"], dup_g, ta=True, o_chips=nsh, out_dtype=BF16, name=n("upg_dw")),
                                 _mm(sv["hf"], dup_v, ta=True, o_chips=nsh, out_dtype=BF16, name=n("upv_dw"))], axis=0)
    dx2, dx2b, g["norm_ffn"] = _rms_bwd(sv["x2"], after(emit("ffn", g), sp["norm_ffn"]), dhf, dx3, name=n("ffn_norm"))
    dmo = _mm(dx2b, gw["w_mo"], tb=True, b_lead=lead, out_dtype=BF16, name=n("mo_dx"))
    g["w_mo"] = _rows_by_chip(_mm(sv["mo"], dx2b, ta=True, out_dtype=BF16, name=n("mo_dw")))
    dmq, dmk, dmv = _mem_attn_bwd(sv["mq"], sv["mk"], sv["mv"], dmo, name=n("mem_attn"))
    dhq = in_hbm(_mm(dmq, gw["w_mq"], tb=True, b_lead=lead, name=n("mq_dx")))
    g["w_mq"] = _rows_by_chip(_mm(sv["hq"], dmq, ta=True, out_dtype=BF16, name=n("mq_dw")))
    dhm = _mm(dmk, gw["w_mk"], tb=True, b_lead=lead, name=n("mk_dx"))
    dhm = _mm(dmv, gw["w_mv"], tb=True, b_lead=lead, res=dhm, name=n("mv_dx"))
    g["w_mk"] = _rows_by_chip(_mm(sv["hm"], dmk, ta=True, out_dtype=BF16, name=n("mk_dw")))
    g["w_mv"] = _rows_by_chip(_mm(sv["hm"], dmv, ta=True, out_dtype=BF16, name=n("mv_dw")))
    dx1, dx1b, g["norm_mem_q"] = _rms_bwd(sv["x1"], after(emit("mem", g), sp["norm_mem_q"]), dhq, dx2, name=n("memq_norm"))
    _, _, g["norm_mem_kv"] = _rms_bwd(mem, sp["norm_mem_kv"], dhm, name=n("memkv_norm"))
    dy_ssm = in_hbm(_mm(dx1b, gw["w_out"], tb=True, b_lead=lead, b_rows=(0, D_SSM), name=n("outa_dx")))
    dy_att = in_hbm(_mm(dx1b, gw["w_out"], tb=True, b_lead=lead, b_rows=(D_SSM, D_SSM), name=n("outb_dx")))
    g["w_out"] = _rows_by_chip(jnp.concatenate([_mm(sv["y_ssm"], dx1b, ta=True, out_dtype=BF16, name=n("outa_dw")),
                                                _mm(sv["y_att"], dx1b, ta=True, out_dtype=BF16, name=n("outb_dw"))], axis=0))
    datt, _, g["attn_out_norm"] = _rms_bwd(sv["att"], sp["attn_out_norm"], dy_att, name=n("att_norm"))
    dqr, dkr, dv = [in_hbm(t) for t in _flash_bwd(sv["qr"], sv["kr"], sv["v"], sv["att"], sv["lse"], in_hbm(datt), name=n("flash"))]
    dq, dkv, dkrope = _mla_prep_bwd(dqr, dkr, dv, cos, sins, name=n("rope"))
    duq = _mm(sv["cqn"], dq, ta=True, name=n("uq_dw")).reshape(Q_LORA, MLA_HEADS, HEAD_PAD)[..., :QK_NOPE + QK_ROPE]
    g["w_uq"] = _cols_by_chip(duq.reshape(Q_LORA, -1), BF16)
    dcqn = _mm(dq, wl["w_uq"], tb=True, name=n("uq_dx"))
    g["w_ukv"] = _cols_by_chip(_mm(sv["ckvn"], dkv, ta=True, name=n("ukv_dw")), BF16)
    dckvn = _mm(dkv, wl["w_ukv"], tb=True, name=n("ukv_dx"))
    proj = sv["proj"]
    _, dcq, g["q_norm"] = _rms_bwd(proj, sp["q_norm"], dcqn, col=(Q_LORA, P_CQ // Q_LORA), name=n("q_norm"))
    _, dckv, g["kv_norm"] = _rms_bwd(proj, sp["kv_norm"], dckvn, col=(KV_LORA, P_CKV // KV_LORA), name=n("kv_norm"))
    dy, dz, g["ssm_norm"] = _gated_rms_bwd(sv["y"], proj, sp["ssm_norm"], dy_ssm, name=n("ssm_gate"))
    dxbc, ddt, da_log, dd_skip, ddt_bias = _ssd_bwd(
        sv["xbc"], proj, sp["dt_bias"], sp["a_log"], sp["d_skip"], sv["pstates"], in_hbm(dy), name=n("ssd"))
    dxbc = in_hbm(dxbc)
    g["a_log"], g["d_skip"], g["dt_bias"] = da_log[0, :SSM_HEADS], dd_skip[0, :SSM_HEADS], ddt_bias[0, :SSM_HEADS]
    dxbc_pre, dsw, g["ssm_conv_b"] = _ssm_conv_bwd(proj, wl["ssm_conv_w"], sp["ssm_conv_b"], dxbc, name=n("ssm_conv"))
    g["ssm_conv_w"] = _cols_by_chip(dsw, F32)
    started = emit("mixer", g)
    s = proj.shape[0]
    dproj = jnp.concatenate([dxbc_pre, dz, dcq, ddt, dckv, dkrope,
                             jnp.zeros((s, P_IN - P_KR - LANES), BF16)], axis=1)
    dh = in_hbm(_mm(dproj, wl["w_in"], tb=True, name=n("proj_dx")))
    g["w_in"] = _cols_by_chip(_w_in_from_padded(_mm(sv["h"], dproj, ta=True, name=n("proj_dw"))), BF16)
    dx0, dx0b, g["norm_mix"] = _rms_bwd(sv["x0"], after(started, sp["norm_mix"]), dh, dx1, name=n("mix_norm"))
    return dx0, dx0b, g, emit("proj", g)


def _chip_peers(x, y):
    return [(1 - x, y), (x, 1 - y), (1 - x, 1 - y)]


HBM_SPEC = pl.BlockSpec(memory_space=pltpu.HBM)
SEM_SPEC = pl.BlockSpec(memory_space=pltpu.SEMAPHORE)
ANY_SPEC = pl.BlockSpec(memory_space=pl.ANY)
VMEM_SPEC = pl.BlockSpec(memory_space=pltpu.VMEM)
DATAFLOW = pltpu.SideEffectType.DATAFLOW_SIDE_EFFECTING
TOKEN_SHAPE = (8, LANES)


def _exchange_start(srcs, land_shapes, src_view, dst_view, token, *, name):
    n = len(srcs)

    def body(*refs):
        s, l, tok_in = refs[:n], refs[n:2 * n], refs[2 * n]
        send_sems, recv_sems = refs[2 * n + 1], refs[2 * n + 2]
        tok_out = refs[-1]
        x, y, c = lax.axis_index("x"), lax.axis_index("y"), lax.axis_index("c")
        me = 2 * x + y
        for t in range(n):
            for k, (px, py) in enumerate(_chip_peers(x, y)):
                pltpu.make_async_remote_copy(
                    src_ref=src_view(t, s[t], 2 * px + py), dst_ref=dst_view(t, l[t], me), send_sem=send_sems.at[3 * t + k],
                    recv_sem=recv_sems.at[3 * t + k], device_id=(px, py, c), device_id_type=MESH).start()
            pltpu.make_async_copy(src_view(t, s[t], me), dst_view(t, l[t], me), send_sems.at[3 * n + t]).start()
        tok_out[...] = tok_in[...]

    hbm = lambda t: pltpu.with_memory_space_constraint(t, pltpu.HBM)
    lands = [lax.empty(l.shape, l.dtype) for l in land_shapes]
    outs = pl.pallas_call(
        body, name=name,
        out_shape=(pltpu.SemaphoreType.DMA((4 * n,)), pltpu.SemaphoreType.DMA((3 * n,)),
                   *[pltpu.HBM(l.shape, l.dtype) for l in land_shapes], SDS(TOKEN_SHAPE, F32)),
        in_specs=[HBM_SPEC] * (2 * n) + [VMEM_SPEC], out_specs=(SEM_SPEC, SEM_SPEC, *[HBM_SPEC] * n, VMEM_SPEC),
        input_output_aliases={n + t: 2 + t for t in range(n)},
        compiler_params=pltpu.CompilerParams(has_side_effects=DATAFLOW))(*[hbm(t) for t in srcs], *[hbm(t) for t in lands], token)
    return outs[0], outs[1], list(outs[2:2 + n]), outs[-1]


def _exchange_wait(srcs, lands, send_sems, recv_sems, after, src_view, dst_view, which, *, name):
    n = len(srcs)
    m = len(which)

    def body(*refs):
        s, l = refs[:m], refs[m:2 * m]
        send_ref, recv_ref = refs[2 * m], refs[2 * m + 1]
        x, y, c = lax.axis_index("x"), lax.axis_index("y"), lax.axis_index("c")
        me = 2 * x + y
        for i, t in enumerate(which):
            for k, (px, py) in enumerate(_chip_peers(x, y)):
                chip = 2 * px + py
                cp = pltpu.make_async_remote_copy(
                    src_ref=src_view(t, s[i], chip), dst_ref=dst_view(t, l[i], chip), send_sem=send_ref.at[3 * t + k],
                    recv_sem=recv_ref.at[3 * t + k], device_id=(px, py, c), device_id_type=MESH)
                cp.wait_send()
                cp.wait_recv()
            pltpu.make_async_copy(src_view(t, s[i], me), dst_view(t, l[i], me), send_ref.at[3 * n + t]).wait()

    outs = pl.pallas_call(
        body, name=name, out_shape=[pltpu.HBM(lands[t].shape, lands[t].dtype) for t in which],
        in_specs=[HBM_SPEC] * (2 * m) + [SEM_SPEC, SEM_SPEC, ANY_SPEC], out_specs=[HBM_SPEC] * m,
        input_output_aliases={m + i: i for i in range(m)},
        compiler_params=pltpu.CompilerParams(has_side_effects=DATAFLOW))(
            *[srcs[t] for t in which], *[lands[t] for t in which], send_sems, recv_sems, after)
    return list(outs)


def _gather_layer_start(shards, li, token, tag=""):
    src_view = lambda t, ref, chip: ref.at[li]
    dst_view = lambda t, ref, chip: ref.at[chip]
    send_sems, recv_sems, lands, token = _exchange_start(
        shards, [SDS((N_CHIPS,) + s.shape[1:], s.dtype) for s in shards], src_view, dst_view, token,
        name=f"gather{li}{tag}_start")
    return (shards, lands, send_sems, recv_sems, src_view, dst_view, f"gather{li}{tag}"), token


def _scatter_start(grads, tag, token):
    view = lambda t, ref, chip: ref.at[chip]
    send_sems, recv_sems, lands, token = _exchange_start(
        grads, [SDS(g.shape, g.dtype) for g in grads], view, view, token, name=f"scatter{tag}_start")
    return (grads, lands, send_sems, recv_sems, view, view, f"scatter{tag}"), token


def _exchange_finish(handle, after, which=None, tag=""):
    srcs, lands, send_sems, recv_sems, src_view, dst_view, name = handle
    which = tuple(range(len(srcs))) if which is None else which
    return _exchange_wait(srcs, lands, send_sems, recv_sems, after, src_view, dst_view, which, name=f"{name}{tag}_wait")


def _swap_start(bufs, token, *, name):
    n = len(bufs)

    def body(*refs):
        s, l, tok_in = refs[:n], refs[n:2 * n], refs[2 * n]
        send_sems, recv_sems = refs[2 * n + 1], refs[2 * n + 2]
        x, y, c = lax.axis_index("x"), lax.axis_index("y"), lax.axis_index("c")
        for t in range(n):
            pltpu.make_async_remote_copy(src_ref=s[t], dst_ref=l[t], send_sem=send_sems.at[t], recv_sem=recv_sems.at[t],
                                         device_id=(x, y, 1 - c), device_id_type=MESH).start()
        refs[-1][...] = tok_in[...]

    hbm = lambda t: pltpu.with_memory_space_constraint(t, pltpu.HBM)
    lands = [lax.empty(b.shape, b.dtype) for b in bufs]
    outs = pl.pallas_call(
        body, name=f"{name}_start",
        out_shape=(pltpu.SemaphoreType.DMA((n,)), pltpu.SemaphoreType.DMA((n,)),
                   *[pltpu.HBM(b.shape, b.dtype) for b in bufs], SDS(TOKEN_SHAPE, F32)),
        in_specs=[HBM_SPEC] * (2 * n) + [VMEM_SPEC], out_specs=(SEM_SPEC, SEM_SPEC, *[HBM_SPEC] * n, VMEM_SPEC),
        input_output_aliases={n + t: 2 + t for t in range(n)},
        compiler_params=pltpu.CompilerParams(has_side_effects=DATAFLOW))(*[hbm(t) for t in bufs], *[hbm(t) for t in lands], token)
    return (bufs, list(outs[2:2 + n]), outs[0], outs[1], name), outs[-1]


def _swap_wait(handle, after):
    bufs, lands, send_sems, recv_sems, name = handle
    n = len(bufs)

    def body(*refs):
        s, l = refs[:n], refs[n:2 * n]
        send_ref, recv_ref = refs[2 * n], refs[2 * n + 1]
        x, y, c = lax.axis_index("x"), lax.axis_index("y"), lax.axis_index("c")
        for t in range(n):
            cp = pltpu.make_async_remote_copy(src_ref=s[t], dst_ref=l[t], send_sem=send_ref.at[t], recv_sem=recv_ref.at[t],
                                              device_id=(x, y, 1 - c), device_id_type=MESH)
            cp.wait_send()
            cp.wait_recv()

    outs = pl.pallas_call(
        body, name=f"{name}_wait", out_shape=[pltpu.HBM(b.shape, b.dtype) for b in bufs],
        in_specs=[HBM_SPEC] * (2 * n) + [SEM_SPEC, SEM_SPEC, ANY_SPEC], out_specs=[HBM_SPEC] * n,
        input_output_aliases={n + t: t for t in range(n)},
        compiler_params=pltpu.CompilerParams(has_side_effects=DATAFLOW))(*bufs, *lands, send_sems, recv_sems, after)
    return list(outs)


def _all_gather8(src, *, name):
    def body(src_ref, out_ref, send_sems, recv_sems, local_sem):
        x, y, c = lax.axis_index("x"), lax.axis_index("y"), lax.axis_index("c")
        me = 4 * x + 2 * y + c
        mine = pltpu.make_async_copy(src_ref, out_ref.at[me], local_sem)
        mine.start()

        def peer(k):
            return (x ^ (k >> 2 & 1), y ^ (k >> 1 & 1), c ^ (k & 1))

        sends = []
        for k in range(1, N_DEV):
            cp = pltpu.make_async_remote_copy(src_ref=src_ref, dst_ref=out_ref.at[me], send_sem=send_sems.at[k - 1],
                                              recv_sem=recv_sems.at[k - 1], device_id=peer(k), device_id_type=MESH)
            cp.start()
            sends.append(cp)
        for k in range(1, N_DEV):
            px, py, pc = peer(k)
            pltpu.make_async_remote_copy(src_ref=src_ref, dst_ref=out_ref.at[4 * px + 2 * py + pc],
                                         send_sem=send_sems.at[k - 1], recv_sem=recv_sems.at[k - 1],
                                         device_id=peer(k), device_id_type=MESH).wait_recv()
        for cp in sends:
            cp.wait_send()
        mine.wait()

    any_spec = pl.BlockSpec(memory_space=pl.ANY)
    return pl.pallas_call(
        body, in_specs=[any_spec], out_specs=any_spec, out_shape=SDS((N_DEV,) + src.shape, src.dtype),
        scratch_shapes=[pltpu.SemaphoreType.DMA((N_DEV - 1,)), pltpu.SemaphoreType.DMA((N_DEV - 1,)), pltpu.SemaphoreType.DMA],
        name=name)(src)


def _adam_terms(w, g, m, v):
    m = ADAM_B1 * m + (1.0 - ADAM_B1) * g
    v = ADAM_B2 * v + (1.0 - ADAM_B2) * (g * g)
    m_hat = m / (1.0 - ADAM_B1 ** ADAM_STEP)
    v_hat = v / (1.0 - ADAM_B2 ** ADAM_STEP)
    delta = -ADAM_LR * (m_hat / (jnp.sqrt(v_hat) + ADAM_EPS) + ADAM_WD * w)
    return delta, m, v


def _adamw_shard(mine, other, w, m, v, *, name):
    d, a, b = w.shape
    tr = next((t for t in (128, 64, 32, 16) if a % t == 0), a)

    def body(*refs):
        ga, gb = refs[:d], refs[d:2 * d]
        w_ref, m_ref, v_ref, g_ref, d_ref, nm_ref, nv_ref = refs[2 * d:]

        def plane(ref):
            return ((ref[0].astype(F32) + ref[1].astype(F32)) + ref[2].astype(F32)) + ref[3].astype(F32)

        for lp in range(d):
            @pl.when(pl.program_id(0) == lp)
            def _(lp=lp):
                g = plane(ga[lp]) + plane(gb[lp])
                delta, mn, vn = _adam_terms(w_ref[...], g, m_ref[...], v_ref[...])
                g_ref[...] = g
                d_ref[...] = delta
                nm_ref[...] = mn
                nv_ref[...] = vn

    gspecs = [pl.BlockSpec((N_CHIPS, tr, b), lambda l, i, lp=lp: (0, jnp.where(l == lp, i, 0), 0)) for lp in range(d)]
    blk = pl.BlockSpec((None, tr, b), lambda l, i: (l, i, 0))
    shp = SDS((d, a, b), F32)
    return pl.pallas_call(
        body, grid=(d, a // tr), in_specs=gspecs + gspecs + [blk, blk, blk], out_specs=(blk,) * 4, out_shape=(shp,) * 4,
        name=name, compiler_params=_params(("arbitrary", "arbitrary"), 48 << 20))(*mine, *other, w, m, v)


def _adamw_small(g8, w, m, v, *, name):
    n = w.shape[1]

    def body(g8_ref, w_ref, m_ref, v_ref, g_ref, d_ref, nm_ref, nv_ref):
        g = g8_ref[0]
        for k in range(1, N_DEV):
            g = g + g8_ref[k]
        delta, mn, vn = _adam_terms(w_ref[...], g, m_ref[...], v_ref[...])
        g_ref[...] = g
        d_ref[...] = delta
        nm_ref[...] = mn
        nv_ref[...] = vn

    shp = SDS((1, n), F32)
    return pl.pallas_call(body, out_shape=(shp,) * 4, name=name, compiler_params=_params(None, 24 << 20))(g8, w, m, v)


def _rope_tables(positions):
    inv_freq = 1.0 / (ROPE_THETA ** (jnp.arange(0, QK_ROPE, 2, dtype=F32) / QK_ROPE))
    ang = positions.astype(F32)[:, None] * inv_freq
    c, s = jnp.cos(ang), jnp.sin(ang)
    n = positions.shape[0]
    pad = jnp.zeros((n, HEAD_PAD - QK_NOPE - QK_ROPE), F32)
    cos = jnp.concatenate([jnp.ones((n, QK_NOPE), F32), c, c, pad], axis=1)
    sins = jnp.concatenate([jnp.zeros((n, QK_NOPE), F32), -s, s, pad], axis=1)
    return cos, sins


def _pad_lanes(v):
    return _pad_cols(v.reshape(1, -1), LANES)


def _local_step(x, mem, positions, weights, small, final_norm, loss_target, emit, token):
    cos, sins = _rope_tables(positions)
    saved, gws, wls, sps = [], [], [], []
    h = x
    for li in range(DEPTH):
        sp = {k: small[k][li] for k, _ in SMALL}
        if li == 0:
            sp["norm_mix"] = sp["norm_mix"] + token[0, 0]
        for k in ("dt_bias", "a_log", "d_skip"):
            sp[k] = _pad_lanes(sp[k])
        h, sv, gw, wl = _layer_fwd(h, mem, cos, sins, functools.partial(weights, li), sp, li)
        saved.append(sv)
        gws.append(gw)
        wls.append(wl)
        sps.append(sp)
    loss, dh, dhb, g_final = _final_loss(h, final_norm, loss_target, name="final_loss")
    grads = [None] * DEPTH
    started = None
    for li in reversed(range(DEPTH)):
        sp = sps[li]
        if started is not None:
            sp = dict(sp, ffn_conv_b=sp["ffn_conv_b"] + started[0, 0])
        dh, dhb, grads[li], started = _layer_bwd(dh, dhb, mem, cos, sins, gws[li], wls[li], sp, saved[li], li,
                                                 functools.partial(emit, li))
    return loss, dh, grads, g_final


def _gathered_views(which, lands):
    return {MATS[t][0]: (b.reshape(-1, b.shape[-1]) if MATS[t][2] == 0 else b) for t, b in zip(which, lands)}


def kernel(x, mem, positions, norm_mix, w_in, ssm_conv_w, ssm_conv_b, dt_bias, a_log, d_skip, ssm_norm, q_norm, w_uq, kv_norm, w_ukv, attn_out_norm, w_out, norm_mem_q, norm_mem_kv, w_mq, w_mk, w_mv, w_mo, norm_ffn, w_up, ffn_conv_w, ffn_conv_b, w_down, final_norm, loss_target, m_norm_mix, m_w_in, m_ssm_conv_w, m_ssm_conv_b, m_dt_bias, m_a_log, m_d_skip, m_ssm_norm, m_q_norm, m_w_uq, m_kv_norm, m_w_ukv, m_attn_out_norm, m_w_out, m_norm_mem_q, m_norm_mem_kv, m_w_mq, m_w_mk, m_w_mv, m_w_mo, m_norm_ffn, m_w_up, m_ffn_conv_w, m_ffn_conv_b, m_w_down, m_final_norm, v_norm_mix, v_w_in, v_ssm_conv_w, v_ssm_conv_b, v_dt_bias, v_a_log, v_d_skip, v_ssm_norm, v_q_norm, v_w_uq, v_kv_norm, v_w_ukv, v_attn_out_norm, v_w_out, v_norm_mem_q, v_norm_mem_kv, v_w_mq, v_w_mk, v_w_mv, v_w_mo, v_norm_ffn, v_w_up, v_ffn_conv_w, v_ffn_conv_b, v_w_down, v_final_norm):
    args = dict(locals())
    names = ["norm_mix", "w_in", "ssm_conv_w", "ssm_conv_b", "dt_bias", "a_log", "d_skip", "ssm_norm", "q_norm", "w_uq",
             "kv_norm", "w_ukv", "attn_out_norm", "w_out", "norm_mem_q", "norm_mem_kv", "w_mq", "w_mk", "w_mv", "w_mo",
             "norm_ffn", "w_up", "ffn_conv_w", "ffn_conv_b", "w_down", "final_norm"]
    wts = {k: args[k] for k in names}
    mom = {k: args["m_" + k] for k in names}
    var = {k: args["v_" + k] for k in names}
    mat_names = [k for k, _, _ in MATS]

    shards = [wts[k] if k in F32_ON_WIRE else wts[k].astype(BF16) for k in mat_names]
    token = jnp.zeros(TOKEN_SHAPE, F32)
    first, token = _gather_layer_start(shards[:1], 0, token, tag="_first")
    gathers = []
    for li in range(DEPTH):
        handle, token = _gather_layer_start(shards[1:] if li == 0 else shards, li, token)
        gathers.append(handle)
    small = {k: wts[k] for k, _ in SMALL}

    def weights(li, group, after):
        which = GROUPS[group]
        if li > 0:
            return _gathered_views(which, _exchange_finish(gathers[li], after, which, tag=f"_{group}"))
        if group == "proj":
            return _gathered_views(which, _exchange_finish(first, after))
        return _gathered_views(which, _exchange_finish(gathers[0], after, tuple(t - 1 for t in which), tag=f"_{group}"))

    scatters = [[] for _ in range(DEPTH)]
    nm = len(mat_names)
    mine = [[None] * nm for _ in range(DEPTH)]
    swaps = [None] * DEPTH
    last_started = [None]

    def swap_layer(li, after):
        for which, handle in scatters[li]:
            for t, b in zip(which, _exchange_finish(handle, after)):
                mine[li][t] = b
        swaps[li], started = _swap_start(mine[li], jnp.zeros(TOKEN_SHAPE, F32), name=f"swap{li}")
        return started

    def emit(li, group, g):
        last = group == "proj"
        if li == 0:
            which = GROUPS[group]
        elif last:
            which = tuple(range(nm))
        else:
            return None
        handle, started = _scatter_start([g[MATS[t][0]] for t in which], f"{li}_{group}", jnp.zeros(TOKEN_SHAPE, F32))
        scatters[li].append((which, handle))
        last_started[0] = started
        if li + 1 < DEPTH and group == ("mixer" if li == 0 else "proj"):
            started = started + swap_layer(li + 1, g["ssm_conv_w"])
        return started

    loss, grad_x, grads, g_final = _local_step(x[0], mem[0], positions[0], weights, small, wts["final_norm"],
                                               loss_target[0], emit, token)
    loss = lax.psum(loss, ("x", "y", "c"))

    w_in_t = GROUPS["proj"][0]
    others = [t for t in range(nm) if t != w_in_t]
    for which, handle in scatters[0]:
        if which != GROUPS["proj"]:
            for t, b in zip(which, _exchange_finish(handle, grad_x)):
                mine[0][t] = b
    swaps[0], _ = _swap_start([mine[0][t] for t in others], last_started[0], name="swap0")
    other = [dict(zip(others, _swap_wait(swaps[0], grad_x)))] + [
        dict(enumerate(_swap_wait(swaps[li], grad_x))) for li in range(1, DEPTH)]

    def adamw(t):
        k = mat_names[t]
        return _adamw_shard([mine[li][t] for li in range(DEPTH)], [other[li][t] for li in range(DEPTH)],
                            wts[k], mom[k], var[k], name=f"adamw_{k}")

    mat_out = {mat_names[t]: adamw(t) for t in others}
    done = sum(mat_out[mat_names[t]][0][0, 0, :1] for t in others)
    (last_handle,) = [handle for which, handle in scatters[0] if which == GROUPS["proj"]]
    (mine[0][w_in_t],) = _exchange_finish(last_handle, done)
    last_swap, _ = _swap_start([mine[0][w_in_t]], jnp.zeros(TOKEN_SHAPE, F32), name="swap0_last")
    (other[0][w_in_t],) = _swap_wait(last_swap, done)
    mat_out[mat_names[w_in_t]] = adamw(w_in_t)

    def pack_small(get, fin):
        flat = [get(k).reshape(-1) for k, _ in SMALL] + [fin.reshape(-1)]
        n = sum(f.shape[0] for f in flat)
        return jnp.concatenate(flat + [jnp.zeros((-n % PACK_COLS,), F32)]).reshape(1, -1)

    gs = pack_small(lambda k: jnp.stack([grads[li][k] for li in range(DEPTH)]), g_final)
    g8 = _all_gather8(gs, name="gather_small_grads")
    small_out = _adamw_small(g8, pack_small(wts.get, wts["final_norm"]), pack_small(mom.get, mom["final_norm"]),
                             pack_small(var.get, var["final_norm"]), name="adamw_small")

    def unpack_small(buf):
        out, off = {}, 0
        for k, nel in SMALL:
            out[k] = buf[0, off:off + DEPTH * nel].reshape(DEPTH, nel)
            off += DEPTH * nel
        out["final_norm"] = buf[0, off:off + D_MODEL]
        return out

    small_res = [unpack_small(b) for b in small_out]
    res = []
    for kind in range(4):
        for k in names:
            res.append(small_res[kind][k] if k in small_res[kind] else mat_out[k][kind])
    return (loss, grad_x[None], *res)
```

```python
import functools
import math

import jax
import jax.numpy as jnp
from jax import lax
from jax.experimental import pallas as pl
from jax.experimental.pallas import tpu as pltpu

F32 = jnp.float32
BF16 = jnp.bfloat16
HIGHEST = lax.Precision.HIGHEST
SDS = jax.ShapeDtypeStruct
MESH = pl.DeviceIdType.MESH

D_MODEL = 1024
DEPTH = 4
EPS = 1e-6
SSM_HEADS = 16
SSM_HEAD_DIM = 64
D_SSM = 1024
SSM_GROUPS = 4
SSM_STATE = 128
SSM_CONV = 4
SSM_CHUNK = 128
CONV_CH = 2048
MLA_HEADS = 16
QK_NOPE = 64
QK_ROPE = 32
V_DIM = 64
Q_LORA = 384
KV_LORA = 256
ROPE_THETA = 10000.0
MEM_HEADS = 4
MEM_HEAD_DIM = 256
D_FF = 2816
FFN_CONV = 3
D_IN = 3760
ADAM_LR = 0.001
ADAM_B1 = 0.9
ADAM_B2 = 0.999
ADAM_EPS = 1e-08
ADAM_WD = 0.01
ADAM_STEP = 10

LANES = 128
HEAD_PAD = 128
N_CHIPS = 4
N_DEV = 8
VMEM_CAP_MB = 56

P_XBC, P_Z, P_CQ, P_DT, P_CKV, P_KR, P_IN = 0, 2048, 3072, 3456, 3584, 3840, 4096
NEG = -1e30


def _tile(n, pref):
    t = (min(n, pref) // LANES) * LANES
    while t >= LANES:
        if n % t == 0:
            return t
        t -= LANES
    return n


def _params(sem=None, vmem_bytes=None):
    kw = {}
    if sem is not None:
        kw["dimension_semantics"] = sem
    if vmem_bytes is not None:
        kw["vmem_limit_bytes"] = int(min(max(vmem_bytes, 16 << 20), VMEM_CAP_MB << 20))
    return pltpu.CompilerParams(**kw)


def _nbytes(shape, dtype):
    return math.prod(shape) * jnp.dtype(dtype).itemsize


def _mm(a, b, *, ta=False, tb=False, res=None, out_dtype=F32, name, a_col=None, b_lead=(), b_rows=None,
        b_chips=None, o_chips=None):
    if ta:
        k, m = a.shape
    else:
        m, k = (a.shape[0], a.shape[1] if a_col is None else a_col[0])
    rows_b, cols_b = b.shape[-2:]
    row0 = 0
    if b_rows is not None:
        row0, rows_b = b_rows
    nlead = len(b_lead)
    if b_chips is not None:
        assert not tb
        kb, tn, n = rows_b, cols_b, b_chips[1] * cols_b
        b_blk = (None,) * (1 + nlead) + (kb, tn)
        b_map = lambda i, j: (b_chips[0] + j,) + tuple(b_lead) + (0, 0)
    elif tb:
        n, kb = rows_b, cols_b
        tn = _tile(n, 512)
        assert row0 % tn == 0
        b_blk = (None,) * nlead + (tn, kb)
        b_map = lambda i, j: tuple(b_lead) + (j + row0 // tn, 0)
    else:
        kb, n = rows_b, cols_b
        tn = o_chips if o_chips else _tile(n, 512)
        assert row0 % kb == 0
        b_blk = (None,) * nlead + (kb, tn)
        b_map = lambda i, j: tuple(b_lead) + (row0 // kb, j)
    assert k == kb, (a.shape, b.shape, ta, tb, k, kb)
    tm = _tile(m, 512)
    if ta:
        a_blk, a_map = (k, tm), (lambda i, j: (0, i))
    else:
        a_blk, a_map = (tm, k), ((lambda i, j: (i, 0)) if a_col is None else (lambda i, j: (i, a_col[1])))
    if o_chips:
        o_spec = pl.BlockSpec((None, tm, tn), lambda i, j: (j, i, 0))
        o_shape = SDS((n // tn, m, tn), out_dtype)
    else:
        o_spec = pl.BlockSpec((tm, tn), lambda i, j: (i, j))
        o_shape = SDS((m, n), out_dtype)
    dims = (((0 if ta else 1,), (1 if tb else 0,)), ((), ()))
    has_res = res is not None

    def body(*refs):
        a_ref, b_ref = refs[0], refs[1]
        o_ref = refs[-1]
        acc = lax.dot_general(a_ref[...].astype(BF16), b_ref[...].astype(BF16), dims, preferred_element_type=F32)
        if has_res:
            acc = acc + refs[2][...]
        o_ref[...] = acc.astype(o_ref.dtype)

    bb = tuple(d for d in b_blk if d is not None)
    vmem = 2 * (_nbytes(a_blk, a.dtype) + _nbytes(bb, b.dtype) + (2 if has_res else 1) * _nbytes((tm, tn), F32))
    vmem += _nbytes(a_blk, BF16) + _nbytes(bb, BF16) + 2 * _nbytes((tm, tn), F32) + (4 << 20)
    args = (a, b) + ((res,) if has_res else ())
    specs = [pl.BlockSpec(a_blk, a_map), pl.BlockSpec(b_blk, b_map)] + ([o_spec] if has_res else [])
    return pl.pallas_call(body, grid=(m // tm, n // tn), in_specs=specs, out_specs=o_spec, out_shape=o_shape, name=name,
                          compiler_params=_params(("parallel", "parallel"), vmem))(*args)


def _sigmoid(x):
    return 1.0 / (1.0 + jnp.exp(-x))


def _rms_fwd(x, g, *, col=None, name):
    s = x.shape[0]
    w, ci = (x.shape[1], 0) if col is None else col
    tm = min(s, 512)

    def body(x_ref, g_ref, o_ref):
        xv = x_ref[...].astype(F32)
        r = lax.rsqrt(jnp.mean(xv * xv, axis=-1, keepdims=True) + EPS)
        o_ref[...] = (xv * r * g_ref[...]).astype(o_ref.dtype)

    return pl.pallas_call(
        body, grid=(s // tm,),
        in_specs=[pl.BlockSpec((tm, w), lambda i: (i, ci)), pl.BlockSpec((1, w), lambda i: (0, 0))],
        out_specs=pl.BlockSpec((tm, w), lambda i: (i, 0)), out_shape=SDS((s, w), BF16), name=name,
        compiler_params=_params(("parallel",), 10 * tm * w * 4))(x, g.reshape(1, w))


def _rms_bwd(x, g, dy, dres=None, *, col=None, name):
    s = x.shape[0]
    w, ci = (x.shape[1], 0) if col is None else col
    tm = min(s, 512)
    has_res = dres is not None

    def body(*refs):
        x_ref, g_ref, dy_ref = refs[:3]
        dx_ref, dxb_ref, dg_ref = refs[-3:]
        xv = x_ref[...].astype(F32)
        dyv = dy_ref[...].astype(F32)
        r = lax.rsqrt(jnp.mean(xv * xv, axis=-1, keepdims=True) + EPS)
        u = dyv * g_ref[...]
        dx = r * u - xv * (r * r * r) * jnp.mean(xv * u, axis=-1, keepdims=True)
        if has_res:
            dx = dx + refs[3][...]
        dx_ref[...] = dx
        dxb_ref[...] = dx.astype(BF16)

        @pl.when(pl.program_id(0) == 0)
        def _():
            dg_ref[...] = jnp.zeros_like(dg_ref)

        dg_ref[...] += jnp.sum(dyv * xv * r, axis=0, keepdims=True)

    blk = pl.BlockSpec((tm, w), lambda i: (i, 0))
    specs = [pl.BlockSpec((tm, w), lambda i: (i, ci)), pl.BlockSpec((1, w), lambda i: (0, 0)), blk]
    args = [x, g.reshape(1, w), dy]
    if has_res:
        specs.append(blk)
        args.append(dres)
    dx, dxb, dg = pl.pallas_call(
        body, grid=(s // tm,), in_specs=specs,
        out_specs=(blk, blk, pl.BlockSpec((1, w), lambda i: (0, 0))),
        out_shape=(SDS((s, w), F32), SDS((s, w), BF16), SDS((1, w), F32)), name=name,
        compiler_params=_params(("arbitrary",), 18 * tm * w * 4))(*args)
    return dx, dxb, dg.reshape(w)


def _gated_rms_fwd(y, proj, g, *, name):
    s, w = y.shape
    tm = min(s, 512)

    def body(y_ref, z_ref, g_ref, o_ref):
        z = z_ref[...]
        t = y_ref[...] * (z * _sigmoid(z))
        r = lax.rsqrt(jnp.mean(t * t, axis=-1, keepdims=True) + EPS)
        o_ref[...] = (t * r * g_ref[...]).astype(o_ref.dtype)

    blk = pl.BlockSpec((tm, w), lambda i: (i, 0))
    return pl.pallas_call(
        body, grid=(s // tm,),
        in_specs=[blk, pl.BlockSpec((tm, w), lambda i: (i, P_Z // w)), pl.BlockSpec((1, w), lambda i: (0, 0))],
        out_specs=blk, out_shape=SDS((s, w), BF16), name=name,
        compiler_params=_params(("parallel",), 14 * tm * w * 4))(y, proj, g.reshape(1, w))


def _gated_rms_bwd(y, proj, g, dout, *, name):
    s, w = y.shape
    tm = min(s, 512)

    def body(y_ref, z_ref, g_ref, do_ref, dy_ref, dz_ref, dg_ref):
        z = z_ref[...]
        yv = y_ref[...]
        dov = do_ref[...]
        sg = _sigmoid(z)
        sz = z * sg
        t = yv * sz
        r = lax.rsqrt(jnp.mean(t * t, axis=-1, keepdims=True) + EPS)
        u = dov * g_ref[...]
        dt = r * u - t * (r * r * r) * jnp.mean(t * u, axis=-1, keepdims=True)
        dy_ref[...] = dt * sz
        dz_ref[...] = (dt * yv * (sg * (1.0 + z * (1.0 - sg)))).astype(dz_ref.dtype)

        @pl.when(pl.program_id(0) == 0)
        def _():
            dg_ref[...] = jnp.zeros_like(dg_ref)

        dg_ref[...] += jnp.sum(dov * t * r, axis=0, keepdims=True)

    blk = pl.BlockSpec((tm, w), lambda i: (i, 0))
    vec = pl.BlockSpec((1, w), lambda i: (0, 0))
    dy, dz, dg = pl.pallas_call(
        body, grid=(s // tm,),
        in_specs=[blk, pl.BlockSpec((tm, w), lambda i: (i, P_Z // w)), vec, blk],
        out_specs=(blk, blk, vec), out_shape=(SDS((s, w), F32), SDS((s, w), BF16), SDS((1, w), F32)), name=name,
        compiler_params=_params(("arbitrary",), 24 * tm * w * 4))(y, proj, g.reshape(1, w), dout)
    return dy, dz, dg.reshape(w)


def _final_loss(x, g, target, *, name):
    s, w = x.shape
    tm = min(s, 512)

    def body(x_ref, g_ref, t_ref, loss_ref, dx_ref, dxb_ref, dg_ref):
        xv = x_ref[...]
        gv = g_ref[...]
        r = lax.rsqrt(jnp.mean(xv * xv, axis=-1, keepdims=True) + EPS)
        xn = xv * r
        diff = xn * gv - t_ref[...]
        dy = diff * (1.0 / w)
        u = dy * gv
        dx = r * u - xv * (r * r * r) * jnp.mean(xv * u, axis=-1, keepdims=True)
        dx_ref[...] = dx
        dxb_ref[...] = dx.astype(BF16)

        @pl.when(pl.program_id(0) == 0)
        def _():
            dg_ref[...] = jnp.zeros_like(dg_ref)
            loss_ref[...] = jnp.zeros_like(loss_ref)

        dg_ref[...] += jnp.sum(dy * xn, axis=0, keepdims=True)
        part = jnp.sum(jnp.sum(diff * diff, axis=1, keepdims=True), axis=0, keepdims=True) * (0.5 / w)
        loss_ref[...] += jnp.broadcast_to(part, loss_ref.shape)

    blk = pl.BlockSpec((tm, w), lambda i: (i, 0))
    vec = pl.BlockSpec((1, w), lambda i: (0, 0))
    loss, dx, dxb, dg = pl.pallas_call(
        body, grid=(s // tm,), in_specs=[blk, vec, blk],
        out_specs=(pl.BlockSpec((1, LANES), lambda i: (0, 0)), blk, blk, vec),
        out_shape=(SDS((1, LANES), F32), SDS((s, w), F32), SDS((s, w), BF16), SDS((1, w), F32)), name=name,
        compiler_params=_params(("arbitrary",), 18 * tm * w * 4))(x, g.reshape(1, w), target)
    return loss[0, 0], dx, dxb, dg.reshape(w)


def _shift_down(x, k):
    if k == 0:
        return x
    row = lax.broadcasted_iota(jnp.int32, x.shape, 0)
    return jnp.where(row < k, 0.0, pltpu.roll(x, k, axis=0))


def _shift_up(x, k):
    if k == 0:
        return x
    s = x.shape[0]
    row = lax.broadcasted_iota(jnp.int32, x.shape, 0)
    return jnp.where(row >= s - k, 0.0, pltpu.roll(x, s - k, axis=0))


def _conv_pre(x, w, b, kw):
    pre = b
    for j in range(kw):
        pre = pre + w[j:j + 1, :] * _shift_down(x, kw - 1 - j)
    return pre


def _conv_bwd_terms(x, w, dpre, kw):
    dx = jnp.zeros_like(x)
    dws = []
    for j in range(kw):
        dx = dx + w[j:j + 1, :] * _shift_up(dpre, kw - 1 - j)
        dws.append(jnp.sum(dpre * _shift_down(x, kw - 1 - j), axis=0, keepdims=True))
    return dx, jnp.concatenate(dws, axis=0), jnp.sum(dpre, axis=0, keepdims=True)


def _ssm_conv_fwd(proj, w, b, *, name):
    s = proj.shape[0]
    cw = 256

    def body(x_ref, w_ref, b_ref, o_ref):
        pre = _conv_pre(x_ref[...], w_ref[...], b_ref[...], SSM_CONV)
        o_ref[...] = pre * _sigmoid(pre)

    return pl.pallas_call(
        body, grid=(CONV_CH // cw,),
        in_specs=[pl.BlockSpec((s, cw), lambda j: (0, j)), pl.BlockSpec((SSM_CONV, cw), lambda j: (0, j)),
                  pl.BlockSpec((1, cw), lambda j: (0, j))],
        out_specs=pl.BlockSpec((s, cw), lambda j: (0, j)), out_shape=SDS((s, CONV_CH), F32), name=name,
        compiler_params=_params(("parallel",), 12 * s * cw * 4))(proj, w, b.reshape(1, CONV_CH))


def _ssm_conv_bwd(proj, w, b, dxbc, *, name):
    s = proj.shape[0]
    cw = 256

    def body(x_ref, w_ref, b_ref, dy_ref, dx_ref, dw_ref, db_ref):
        x = x_ref[...]
        wv = w_ref[...]
        pre = _conv_pre(x, wv, b_ref[...], SSM_CONV)
        sg = _sigmoid(pre)
        dpre = dy_ref[...] * (sg * (1.0 + pre * (1.0 - sg)))
        dx, dw, db = _conv_bwd_terms(x, wv, dpre, SSM_CONV)
        dx_ref[...] = dx.astype(dx_ref.dtype)
        dw_ref[...] = dw
        db_ref[...] = db

    col = pl.BlockSpec((s, cw), lambda j: (0, j))
    wsp = pl.BlockSpec((SSM_CONV, cw), lambda j: (0, j))
    bsp = pl.BlockSpec((1, cw), lambda j: (0, j))
    dx, dw, db = pl.pallas_call(
        body, grid=(CONV_CH // cw,), in_specs=[col, wsp, bsp, col], out_specs=(col, wsp, bsp),
        out_shape=(SDS((s, CONV_CH), BF16), SDS((SSM_CONV, CONV_CH), F32), SDS((1, CONV_CH), F32)), name=name,
        compiler_params=_params(("parallel",), 20 * s * cw * 4))(proj, w, b.reshape(1, CONV_CH), dxbc)
    return dx, dw, db.reshape(CONV_CH)


def _ffn_conv_fwd(up_g, up_v, w, b, *, name):
    s = up_g.shape[0]
    cw = 256
    nb = D_FF // cw

    def body(g_ref, v_ref, wg_ref, wv_ref, bg_ref, bv_ref, o_ref):
        gate = _conv_pre(g_ref[...], wg_ref[...], bg_ref[...], FFN_CONV)
        val = _conv_pre(v_ref[...], wv_ref[...], bv_ref[...], FFN_CONV)
        o_ref[...] = (gate * _sigmoid(gate) * val).astype(o_ref.dtype)

    col = pl.BlockSpec((s, cw), lambda j: (0, j))
    b2 = b.reshape(1, 2 * D_FF)
    return pl.pallas_call(
        body, grid=(nb,),
        in_specs=[col, col, pl.BlockSpec((FFN_CONV, cw), lambda j: (0, j)), pl.BlockSpec((FFN_CONV, cw), lambda j: (0, j + nb)),
                  pl.BlockSpec((1, cw), lambda j: (0, j)), pl.BlockSpec((1, cw), lambda j: (0, j + nb))],
        out_specs=col, out_shape=SDS((s, D_FF), BF16), name=name,
        compiler_params=_params(("parallel",), 16 * s * cw * 4))(up_g, up_v, w, w, b2, b2)


def _ffn_conv_bwd(up_g, up_v, w, b, dact, *, name):
    s = up_g.shape[0]
    cw = 256
    nb = D_FF // cw

    def body(g_ref, v_ref, wg_ref, wv_ref, bg_ref, bv_ref, da_ref, dg_ref, dv_ref, dwg_ref, dwv_ref, dbg_ref, dbv_ref):
        xg, xv = g_ref[...], v_ref[...]
        wg, wv = wg_ref[...], wv_ref[...]
        gate = _conv_pre(xg, wg, bg_ref[...], FFN_CONV)
        val = _conv_pre(xv, wv, bv_ref[...], FFN_CONV)
        da = da_ref[...].astype(F32)
        sg = _sigmoid(gate)
        dgate = da * val * (sg * (1.0 + gate * (1.0 - sg)))
        dval = da * gate * sg
        dxg, dwg, dbg = _conv_bwd_terms(xg, wg, dgate, FFN_CONV)
        dxv, dwv, dbv = _conv_bwd_terms(xv, wv, dval, FFN_CONV)
        dg_ref[...] = dxg.astype(dg_ref.dtype)
        dv_ref[...] = dxv.astype(dv_ref.dtype)
        dwg_ref[...] = dwg
        dwv_ref[...] = dwv
        dbg_ref[...] = dbg
        dbv_ref[...] = dbv

    col = pl.BlockSpec((s, cw), lambda j: (0, j))
    wsp = pl.BlockSpec((FFN_CONV, cw), lambda j: (0, j))
    bsp = pl.BlockSpec((1, cw), lambda j: (0, j))
    b2 = b.reshape(1, 2 * D_FF)
    dg, dv, dwg, dwv, dbg, dbv = pl.pallas_call(
        body, grid=(nb,),
        in_specs=[col, col, wsp, pl.BlockSpec((FFN_CONV, cw), lambda j: (0, j + nb)), bsp,
                  pl.BlockSpec((1, cw), lambda j: (0, j + nb)), col],
        out_specs=(col, col, wsp, wsp, bsp, bsp),
        out_shape=(SDS((s, D_FF), BF16), SDS((s, D_FF), BF16), SDS((FFN_CONV, D_FF), F32), SDS((FFN_CONV, D_FF), F32),
                   SDS((1, D_FF), F32), SDS((1, D_FF), F32)), name=name,
        compiler_params=_params(("parallel",), 32 * s * cw * 4))(up_g, up_v, w, w, b2, b2, dact)
    return dg, dv, jnp.concatenate([dwg, dwv], axis=1), jnp.concatenate([dbg, dbv], axis=1).reshape(2 * D_FF)


def _dot(a, b):
    return jnp.dot(a.astype(BF16), b.astype(BF16), preferred_element_type=F32)


def _dot_nt(a, b):
    return lax.dot_general(a.astype(BF16), b.astype(BF16), (((1,), (1,)), ((), ())), preferred_element_type=F32)


def _dot_tn(a, b):
    return lax.dot_general(a.astype(BF16), b.astype(BF16), (((0,), (0,)), ((), ())), preferred_element_type=F32)


def _ssd_chunk_terms(dtraw, bias, a_log):
    ell = dtraw.shape[0]
    lane = lax.broadcasted_iota(jnp.int32, dtraw.shape, 1)
    valid = lane < SSM_HEADS
    pre = dtraw + bias
    dt = jnp.where(valid, jnp.where(pre > 20.0, pre, jnp.log(1.0 + jnp.exp(jnp.minimum(pre, 20.0)))), 0.0)
    a = -jnp.exp(a_log)
    ad = dt * a
    row = lax.broadcasted_iota(jnp.int32, (ell, ell), 0)
    colm = lax.broadcasted_iota(jnp.int32, (ell, ell), 1)
    tril = row >= colm
    cs = jnp.dot(tril.astype(F32), ad, precision=HIGHEST, preferred_element_type=F32)
    cs_last = cs[ell - 1:ell, :]
    return pre, dt, a, cs, cs_last, tril


def _head_expand():
    h = lax.broadcasted_iota(jnp.int32, (LANES, D_SSM), 0)
    c = lax.broadcasted_iota(jnp.int32, (LANES, D_SSM), 1)
    return (c // SSM_HEAD_DIM == h).astype(F32)


def _ssd_fwd(xbc, proj, dt_bias, a_log, d_skip, *, name):
    s = xbc.shape[0]
    nc = s // SSM_CHUNK
    ell, n, p = SSM_CHUNK, SSM_STATE, SSM_HEAD_DIM
    rpg = SSM_HEADS // SSM_GROUPS
    gw = rpg * p

    def body(x_ref, dt_ref, bias_ref, alog_ref, dskip_ref, ex_ref, y_ref, ps_ref, state):
        @pl.when(pl.program_id(0) == 0)
        def _():
            state[...] = jnp.zeros_like(state)

        _, dt, _, cs, cs_last, tril = _ssd_chunk_terms(dt_ref[...], bias_ref[...], alog_ref[...])
        cst = cs.T
        ex = ex_ref[...]
        spread = lambda v: jnp.dot(v, ex, precision=HIGHEST, preferred_element_type=F32)
        dt_x, e_x, ds_x = spread(dt), spread(jnp.exp(cs)), spread(jnp.exp(cs_last - cs))
        cd_x = spread(jnp.broadcast_to(jnp.exp(cs_last), (8, LANES)))[0:1, :]
        dskip_x = spread(jnp.broadcast_to(dskip_ref[...], (8, LANES)))[0:1, :]
        st = state[...]
        ps_ref[0] = st
        xv = x_ref[...]
        xs_all = xv[:, 0:D_SSM]
        xd_all = xs_all * dt_x
        xdd_all = xd_all * ds_x
        lane_g = lax.broadcasted_iota(jnp.int32, (ell, gw), 1)
        ys, new = [], []
        for g in range(SSM_GROUPS):
            gs = slice(gw * g, gw * (g + 1))
            bg = xv[:, D_SSM + n * g:D_SSM + n * (g + 1)]
            cg = xv[:, D_SSM + n * (SSM_GROUPS + g):D_SSM + n * (SSM_GROUPS + g + 1)]
            cb = _dot_nt(cg, bg)
            xd_g, prev_g = xd_all[:, gs], st[:, gs]
            y_g = _dot(cg, prev_g) * e_x[:, gs] + xs_all[:, gs] * dskip_x[:, gs]
            for r in range(rpg):
                h = g * rpg + r
                lmat = jnp.exp(jnp.where(tril, cs[:, h:h + 1] - cst[h:h + 1, :], -jnp.inf))
                y_g = y_g + jnp.where((lane_g >= p * r) & (lane_g < p * (r + 1)), _dot(cb * lmat, xd_g), 0.0)
            ys.append(y_g)
            new.append(prev_g * cd_x[:, gs] + _dot(bg.T, xdd_all[:, gs]))
        y_ref[...] = jnp.concatenate(ys, axis=1)
        state[...] = jnp.concatenate(new, axis=1)

    vec = pl.BlockSpec((1, LANES), lambda c: (0, 0))
    return pl.pallas_call(
        body, grid=(nc,),
        in_specs=[pl.BlockSpec((ell, CONV_CH), lambda c: (c, 0)), pl.BlockSpec((ell, LANES), lambda c: (c, P_DT // LANES)),
                  vec, vec, vec, pl.BlockSpec((LANES, D_SSM), lambda c: (0, 0))],
        out_specs=(pl.BlockSpec((ell, D_SSM), lambda c: (c, 0)), pl.BlockSpec((1, n, D_SSM), lambda c: (c, 0, 0))),
        out_shape=(SDS((s, D_SSM), F32), SDS((nc, n, D_SSM), F32)),
        scratch_shapes=[pltpu.VMEM((n, D_SSM), F32)], name=name,
        compiler_params=_params(("arbitrary",), 32 << 20))(xbc, proj, dt_bias, a_log, d_skip, _head_expand())


def _ssd_bwd(xbc, proj, dt_bias, a_log, d_skip, prev_states, dy, *, name):
    s = xbc.shape[0]
    nc = s // SSM_CHUNK
    ell, n, p = SSM_CHUNK, SSM_STATE, SSM_HEAD_DIM
    rpg = SSM_HEADS // SSM_GROUPS
    gw = rpg * p

    def body(x_ref, dt_ref, bias_ref, alog_ref, dskip_ref, ps_ref, dy_ref, ex_ref, ext_ref,
             dx_ref, ddt_ref, dalog_ref, ddskip_ref, dbias_ref, dstate):
        @pl.when(pl.program_id(0) == 0)
        def _():
            dstate[...] = jnp.zeros_like(dstate)
            dalog_ref[...] = jnp.zeros_like(dalog_ref)
            ddskip_ref[...] = jnp.zeros_like(ddskip_ref)
            dbias_ref[...] = jnp.zeros_like(dbias_ref)

        pre, dt, a, cs, cs_last, tril = _ssd_chunk_terms(dt_ref[...], bias_ref[...], alog_ref[...])
        e = jnp.exp(cs)
        ds = jnp.exp(cs_last - cs)
        cd = jnp.exp(cs_last)
        cst = cs.T
        shape = (ell, LANES)
        ex, ext = ex_ref[...], ext_ref[...]
        spread = lambda v: jnp.dot(v, ex, precision=HIGHEST, preferred_element_type=F32)
        gather = lambda v: jnp.dot(v, ext, precision=HIGHEST, preferred_element_type=F32)
        dt_x, e_x, ds_x = spread(dt), spread(e), spread(ds)
        cd_x = spread(jnp.broadcast_to(cd, (8, LANES)))[0:1, :]
        dskip_x = spread(jnp.broadcast_to(dskip_ref[...], (8, LANES)))[0:1, :]
        xv, dyv, psv, dst = x_ref[...], dy_ref[...], ps_ref[0], dstate[...]
        xs_all = xv[:, 0:D_SSM]
        xd_all = xs_all * dt_x
        dye_all = dyv * e_x
        xdd_all = xd_all * ds_x
        triu = lax.broadcasted_iota(jnp.int32, (ell, ell), 0) <= lax.broadcasted_iota(jnp.int32, (ell, ell), 1)
        lane_g = lax.broadcasted_iota(jnp.int32, (ell, gw), 1)
        lane = lax.broadcasted_iota(jnp.int32, shape, 1)
        sub = lax.broadcasted_iota(jnp.int32, shape, 0)
        dcs_acc = jnp.zeros(shape, F32)
        dcs_rows = jnp.zeros(shape, F32)
        dxs, dbs, dcs_parts, dprevs, prod_a, prod_b, prod_c, prod_e = [], [], [], [], [], [], [], []
        for g in range(SSM_GROUPS):
            gs = slice(gw * g, gw * (g + 1))
            bg = xv[:, D_SSM + n * g:D_SSM + n * (g + 1)]
            cg = xv[:, D_SSM + n * (SSM_GROUPS + g):D_SSM + n * (SSM_GROUPS + g + 1)]
            cb = _dot_nt(cg, bg)
            cbt = _dot_nt(bg, cg)
            xs_g, dy_g, xd_g, dye_g, xdd_g = xs_all[:, gs], dyv[:, gs], xd_all[:, gs], dye_all[:, gs], xdd_all[:, gs]
            prev_g, dsn_g = psv[:, gs], dst[:, gs]
            cprev_g = _dot(cg, prev_g)
            dprevs.append(dsn_g * cd_x[:, gs] + _dot(cg.T, dye_g))
            dcg = _dot_nt(dye_g, prev_g)
            dxdd_g = _dot(bg, dsn_g)
            dbg = _dot_nt(xdd_g, dsn_g)
            dxd_g = dxdd_g * ds_x[:, gs]
            prod_a.append(dy_g * cprev_g)
            prod_b.append(dxdd_g * xd_g)
            prod_e.append(jnp.sum(dsn_g * prev_g, axis=0, keepdims=True))
            dcb = jnp.zeros((ell, ell), F32)
            for r in range(rpg):
                h = g * rpg + r
                mine = (lane_g >= p * r) & (lane_g < p * (r + 1))
                lmat = jnp.exp(jnp.where(tril, cs[:, h:h + 1] - cst[h:h + 1, :], -jnp.inf))
                lmat_t = jnp.exp(jnp.where(triu, cst[h:h + 1, :] - cs[:, h:h + 1], -jnp.inf))
                dgm = _dot_nt(jnp.where(mine, dy_g, 0.0), xd_g)
                dxd_g = dxd_g + jnp.where(mine, _dot(cbt * lmat_t, dy_g), 0.0)
                mm = dgm * (cb * lmat)
                dcs_acc = dcs_acc + jnp.where(lane == h, jnp.sum(mm, axis=1, keepdims=True), 0.0)
                dcs_rows = dcs_rows + jnp.where(sub == h, jnp.sum(mm, axis=0, keepdims=True), 0.0)
                dcb = dcb + dgm * lmat
            dxs.append(dxd_g * dt_x[:, gs] + dy_g * dskip_x[:, gs])
            prod_c.append(dxd_g * xs_g)
            dbs.append(dbg + _dot_tn(dcb, cg))
            dcs_parts.append(dcg + _dot(dcb, bg))
        dx_ref[...] = jnp.concatenate(dxs + dbs + dcs_parts, axis=1)
        dstate[...] = jnp.concatenate(dprevs, axis=1)
        sum_a = gather(jnp.concatenate(prod_a, axis=1))
        sum_b = gather(jnp.concatenate(prod_b, axis=1))
        sum_c = gather(jnp.concatenate(prod_c, axis=1))
        sum_d = gather(dyv * xs_all)
        dcd = gather(jnp.broadcast_to(jnp.concatenate(prod_e, axis=1), (8, D_SSM)))[0:1, :]
        tmp = sum_b * ds
        dlast = dcd * cd + jnp.sum(tmp, axis=0, keepdims=True)
        dcs = dcs_acc + sum_a * e - tmp - dcs_rows.T + jnp.where(sub == ell - 1, dlast, 0.0)
        dad = jnp.dot(triu.astype(F32), dcs, precision=HIGHEST, preferred_element_type=F32)
        ddt = sum_c + dad * a
        dalog_ref[...] += jnp.sum(dad * dt, axis=0, keepdims=True) * a
        ddskip_ref[...] += jnp.sum(sum_d, axis=0, keepdims=True)
        ddraw = jnp.where(lane < SSM_HEADS, ddt * _sigmoid(pre), 0.0)
        ddt_ref[...] = ddraw.astype(ddt_ref.dtype)
        dbias_ref[...] += jnp.sum(ddraw, axis=0, keepdims=True)

    vec = pl.BlockSpec((1, LANES), lambda c: (0, 0))
    rev = lambda c: nc - 1 - c
    ex = _head_expand()
    outs = pl.pallas_call(
        body, grid=(nc,),
        in_specs=[pl.BlockSpec((ell, CONV_CH), lambda c: (rev(c), 0)),
                  pl.BlockSpec((ell, LANES), lambda c: (rev(c), P_DT // LANES)), vec, vec, vec,
                  pl.BlockSpec((1, n, D_SSM), lambda c: (rev(c), 0, 0)),
                  pl.BlockSpec((ell, D_SSM), lambda c: (rev(c), 0)),
                  pl.BlockSpec((LANES, D_SSM), lambda c: (0, 0)), pl.BlockSpec((D_SSM, LANES), lambda c: (0, 0))],
        out_specs=(pl.BlockSpec((ell, CONV_CH), lambda c: (rev(c), 0)), pl.BlockSpec((ell, LANES), lambda c: (rev(c), 0)),
                   vec, vec, vec),
        out_shape=(SDS((s, CONV_CH), F32), SDS((s, LANES), BF16), SDS((1, LANES), F32), SDS((1, LANES), F32),
                   SDS((1, LANES), F32)),
        scratch_shapes=[pltpu.VMEM((n, D_SSM), F32)], name=name,
        compiler_params=_params(("arbitrary",), 40 << 20))(xbc, proj, dt_bias, a_log, d_skip, prev_states, dy, ex, ex.T)
    return outs


def _rope_swap(t):
    lane = lax.broadcasted_iota(jnp.int32, t.shape, 1)
    half = QK_ROPE // 2
    lo = (lane >= QK_NOPE) & (lane < QK_NOPE + half)
    hi = (lane >= QK_NOPE + half) & (lane < QK_NOPE + QK_ROPE)
    return jnp.where(lo, pltpu.roll(t, HEAD_PAD - half, axis=1), jnp.where(hi, pltpu.roll(t, half, axis=1), 0.0))


def _mla_prep(q, kv, proj, cos, sins, *, name):
    s = q.shape[0]
    tm = min(s, 256)
    scale = (QK_NOPE + QK_ROPE) ** -0.5

    def body(q_ref, kv_ref, kr_ref, cos_ref, sin_ref, qo_ref, ko_ref, vo_ref):
        cosv, sinv = cos_ref[...], sin_ref[...]
        kr = pltpu.roll(kr_ref[...], QK_NOPE, axis=1)
        lane = lax.broadcasted_iota(jnp.int32, kr.shape, 1)
        nope = lane < QK_NOPE
        kr = jnp.where(nope, 0.0, kr)
        kpe = kr * cosv + _rope_swap(kr) * sinv
        for hp in range(MLA_HEADS // 2):
            vs = []
            for h in (2 * hp, 2 * hp + 1):
                hs = slice(HEAD_PAD * h, HEAD_PAD * (h + 1))
                qh = q_ref[:, hs]
                kvh = kv_ref[:, hs]
                qo_ref[:, hs] = ((qh * cosv + _rope_swap(qh) * sinv) * scale).astype(qo_ref.dtype)
                ko_ref[:, hs] = (jnp.where(nope, kvh, 0.0) + kpe).astype(ko_ref.dtype)
                vs.append(kvh[:, QK_NOPE:])
            vo_ref[:, 2 * V_DIM * hp:2 * V_DIM * (hp + 1)] = jnp.concatenate(vs, axis=1).astype(vo_ref.dtype)

    wide = pl.BlockSpec((tm, MLA_HEADS * HEAD_PAD), lambda i: (i, 0))
    half = pl.BlockSpec((tm, MLA_HEADS * V_DIM), lambda i: (i, 0))
    tab = pl.BlockSpec((tm, LANES), lambda i: (i, 0))
    return pl.pallas_call(
        body, grid=(s // tm,),
        in_specs=[wide, wide, pl.BlockSpec((tm, LANES), lambda i: (i, P_KR // LANES)), tab, tab],
        out_specs=(wide, wide, half),
        out_shape=(SDS((s, MLA_HEADS * HEAD_PAD), BF16), SDS((s, MLA_HEADS * HEAD_PAD), BF16),
                   SDS((s, MLA_HEADS * V_DIM), BF16)), name=name,
        compiler_params=_params(("parallel",), 32 << 20))(q, kv, proj, cos, sins)


def _mla_prep_bwd(dqr, dkr, dv, cos, sins, *, name):
    s = dqr.shape[0]
    tm = min(s, 256)
    scale = (QK_NOPE + QK_ROPE) ** -0.5

    def body(dq_ref, dk_ref, dv_ref, cos_ref, sin_ref, dqo_ref, dkv_ref, dkr_ref):
        cosv, sinv = cos_ref[...], sin_ref[...]
        lane = lax.broadcasted_iota(jnp.int32, cosv.shape, 1)
        ksum = jnp.zeros(cosv.shape, F32)
        for h in range(MLA_HEADS):
            hs = slice(HEAD_PAD * h, HEAD_PAD * (h + 1))
            d = dq_ref[:, hs]
            dk = dk_ref[:, hs]
            dqo_ref[:, hs] = ((d * cosv + _rope_swap(d * sinv)) * scale).astype(dqo_ref.dtype)
            dkv_ref[:, hs] = jnp.concatenate([dk[:, :QK_NOPE], dv_ref[:, V_DIM * h:V_DIM * (h + 1)]], axis=1).astype(dkv_ref.dtype)
            ksum = ksum + dk
        ksum = jnp.where((lane >= QK_NOPE) & (lane < QK_NOPE + QK_ROPE), ksum, 0.0)
        un = ksum * cosv + _rope_swap(ksum * sinv)
        dkr_ref[...] = pltpu.roll(un, HEAD_PAD - QK_NOPE, axis=1).astype(dkr_ref.dtype)

    wide = pl.BlockSpec((tm, MLA_HEADS * HEAD_PAD), lambda i: (i, 0))
    half = pl.BlockSpec((tm, MLA_HEADS * V_DIM), lambda i: (i, 0))
    tab = pl.BlockSpec((tm, LANES), lambda i: (i, 0))
    return pl.pallas_call(
        body, grid=(s // tm,), in_specs=[wide, wide, half, tab, tab], out_specs=(wide, wide, tab),
        out_shape=(SDS((s, MLA_HEADS * HEAD_PAD), BF16), SDS((s, MLA_HEADS * HEAD_PAD), BF16), SDS((s, LANES), BF16)),
        name=name, compiler_params=_params(("parallel",), 40 << 20))(dqr, dkr, dv, cos, sins)


FLASH_TILE = 512
FLASH_ROWS = 32


def _flash_fwd(q, k, v, *, name):
    s = q.shape[0]
    t = min(s, FLASH_TILE)
    nq = s // t
    npair = MLA_HEADS // 2

    def body(q_ref, k_ref, v_ref, o_ref, lse_ref):
        i = pl.program_id(1)
        qs = [q_ref[:, HEAD_PAD * e:HEAD_PAD * (e + 1)] for e in range(2)]
        diag = lax.broadcasted_iota(jnp.int32, (t, t), 0) >= lax.broadcasted_iota(jnp.int32, (t, t), 1)

        def step(j, carry, masked):
            rows = pl.ds(pl.multiple_of(j * t, t), t)
            new = []
            for e in range(2):
                m, l, acc = carry[e]
                sc = _dot_nt(qs[e], k_ref[rows, HEAD_PAD * e:HEAD_PAD * (e + 1)])
                if masked:
                    sc = jnp.where(diag, sc, NEG)
                m_new = jnp.maximum(m, jnp.max(sc, axis=1, keepdims=True))
                pr = jnp.exp(sc - m_new)
                alpha = jnp.exp(m - m_new)
                l = alpha * l + jnp.sum(pr, axis=1, keepdims=True)
                acc = alpha * acc + _dot(pr, v_ref[rows, V_DIM * e:V_DIM * (e + 1)])
                new.append((m_new, l, acc))
            return tuple(new)

        init = tuple((jnp.full((t, 1), NEG, F32), jnp.zeros((t, 1), F32), jnp.zeros((t, V_DIM), F32)) for _ in range(2))
        carry = lax.fori_loop(0, i, functools.partial(step, masked=False), init)
        carry = step(i, carry, True)
        o_ref[...] = jnp.concatenate([acc / l for _, l, acc in carry], axis=1)
        lse_ref[0] = jnp.concatenate([jnp.broadcast_to(m + jnp.log(l), (t, V_DIM)) for m, l, _ in carry], axis=1)

    return pl.pallas_call(
        body, grid=(npair, nq),
        in_specs=[pl.BlockSpec((t, 2 * HEAD_PAD), lambda hp, i: (i, hp)), pl.BlockSpec((s, 2 * HEAD_PAD), lambda hp, i: (0, hp)),
                  pl.BlockSpec((s, 2 * V_DIM), lambda hp, i: (0, hp))],
        out_specs=(pl.BlockSpec((t, 2 * V_DIM), lambda hp, i: (i, hp)), pl.BlockSpec((1, t, LANES), lambda hp, i: (hp, i, 0))),
        out_shape=(SDS((s, MLA_HEADS * V_DIM), F32), SDS((npair, s, LANES), F32)), name=name,
        compiler_params=_params(("parallel", "parallel"), 40 << 20))(q, k, v)


def _flash_bwd(q, k, v, o, lse, do, *, name):
    s = q.shape[0]
    t = min(s, FLASH_TILE)
    nq = s // t
    npair = MLA_HEADS // 2
    nchunk = t // FLASH_ROWS

    def valid_cols(r):
        return min(t, -(-((r + 1) * FLASH_ROWS) // LANES) * LANES)

    def body(q_ref, k_ref, v_ref, o_ref, lse_ref, do_ref, dq_ref, dk_ref, dv_ref, s_scr, dp_scr, p_scr, ds_scr, dk_acc, dv_acc):
        j = pl.program_id(1)

        @pl.when(j == 0)
        def _():
            dq_ref[...] = jnp.zeros_like(dq_ref)

        dk_acc[...] = jnp.zeros(dk_acc.shape, F32)
        dv_acc[...] = jnp.zeros(dv_acc.shape, F32)
        qsl = [slice(HEAD_PAD * e, HEAD_PAD * (e + 1)) for e in range(2)]
        vsl = [slice(V_DIM * e, V_DIM * (e + 1)) for e in range(2)]

        def step(i, carry, masked):
            rows = pl.ds(pl.multiple_of(i * t, t), t)
            for e in range(2):
                ke = k_ref[:, qsl[e]]
                qi = q_ref[rows, qsl[e]]
                doi = do_ref[rows, vsl[e]]
                delta = jnp.sum(doi * o_ref[rows, vsl[e]], axis=1, keepdims=True)
                lse_i = lse_ref[0, rows, vsl[e]][:, 0:1]
                dob = doi.astype(BF16)
                s_scr[e] = _dot_nt(qi, ke)
                dp_scr[e] = _dot_nt(dob, v_ref[:, vsl[e]])
                for r in range(nchunk):
                    rs = slice(r * FLASH_ROWS, (r + 1) * FLASH_ROWS)
                    width = valid_cols(r) if masked else t
                    sc = s_scr[e, rs, 0:width]
                    if masked:
                        row = r * FLASH_ROWS + lax.broadcasted_iota(jnp.int32, (FLASH_ROWS, width), 0)
                        sc = jnp.where(row >= lax.broadcasted_iota(jnp.int32, (FLASH_ROWS, width), 1), sc, NEG)
                    pr = jnp.exp(sc - lse_i[rs, :])
                    dsc = pr * (dp_scr[e, rs, 0:width] - delta[rs, :])
                    p_scr[e, rs, 0:width] = pr.astype(BF16)
                    ds_scr[e, rs, 0:width] = dsc.astype(BF16)
                    if width < t:
                        p_scr[e, rs, width:t] = jnp.zeros((FLASH_ROWS, t - width), BF16)
                        ds_scr[e, rs, width:t] = jnp.zeros((FLASH_ROWS, t - width), BF16)
                dv_acc[e] += _dot_tn(p_scr[e], dob)
                dk_acc[e] += _dot_tn(ds_scr[e], qi)
                dq_ref[rows, qsl[e]] += _dot(ds_scr[e], ke)
            return carry

        step(j, 0, True)
        lax.fori_loop(j + 1, nq, functools.partial(step, masked=False), 0)
        dk_ref[...] = jnp.concatenate([dk_acc[e] for e in range(2)], axis=1)
        dv_ref[...] = jnp.concatenate([dv_acc[e] for e in range(2)], axis=1)

    full_q = pl.BlockSpec((s, 2 * HEAD_PAD), lambda hp, j: (0, hp))
    full_v = pl.BlockSpec((s, 2 * V_DIM), lambda hp, j: (0, hp))
    blk_k = pl.BlockSpec((t, 2 * HEAD_PAD), lambda hp, j: (j, hp))
    blk_v = pl.BlockSpec((t, 2 * V_DIM), lambda hp, j: (j, hp))
    return pl.pallas_call(
        body, grid=(npair, nq),
        in_specs=[full_q, blk_k, blk_v, full_v, pl.BlockSpec((1, s, LANES), lambda hp, j: (hp, 0, 0)), full_v],
        out_specs=(full_q, blk_k, blk_v),
        out_shape=(SDS((s, MLA_HEADS * HEAD_PAD), F32), SDS((s, MLA_HEADS * HEAD_PAD), F32), SDS((s, MLA_HEADS * V_DIM), F32)),
        scratch_shapes=[pltpu.VMEM((2, t, t), F32), pltpu.VMEM((2, t, t), F32), pltpu.VMEM((2, t, t), BF16),
                        pltpu.VMEM((2, t, t), BF16), pltpu.VMEM((2, t, HEAD_PAD), F32), pltpu.VMEM((2, t, V_DIM), F32)],
        name=name, compiler_params=_params(("parallel", "arbitrary"), 48 << 20))(q, k, v, o, lse, do)


def _mem_attn_fwd(q, k, v, *, name):
    s = q.shape[0]
    tm = min(s, 512)
    ml = k.shape[0]
    scale = MEM_HEAD_DIM ** -0.5

    def body(q_ref, k_ref, v_ref, o_ref):
        for h in range(MEM_HEADS):
            hs = slice(MEM_HEAD_DIM * h, MEM_HEAD_DIM * (h + 1))
            sc = _dot_nt(q_ref[:, hs], k_ref[:, hs]) * scale
            pr = jnp.exp(sc - jnp.max(sc, axis=1, keepdims=True))
            pr = pr / jnp.sum(pr, axis=1, keepdims=True)
            o_ref[:, hs] = _dot(pr, v_ref[:, hs]).astype(o_ref.dtype)

    blk = pl.BlockSpec((tm, D_MODEL), lambda i: (i, 0))
    kv = pl.BlockSpec((ml, D_MODEL), lambda i: (0, 0))
    return pl.pallas_call(body, grid=(s // tm,), in_specs=[blk, kv, kv], out_specs=blk,
                          out_shape=SDS((s, D_MODEL), BF16), name=name,
                          compiler_params=_params(("parallel",), 24 << 20))(q, k, v)


def _mem_attn_bwd(q, k, v, do, *, name):
    s = q.shape[0]
    tm = min(s, 512)
    ml = k.shape[0]
    scale = MEM_HEAD_DIM ** -0.5

    def body(q_ref, k_ref, v_ref, do_ref, dq_ref, dk_ref, dv_ref):
        @pl.when(pl.program_id(0) == 0)
        def _():
            dk_ref[...] = jnp.zeros_like(dk_ref)
            dv_ref[...] = jnp.zeros_like(dv_ref)

        for h in range(MEM_HEADS):
            hs = slice(MEM_HEAD_DIM * h, MEM_HEAD_DIM * (h + 1))
            qh, kh, vh, doh = q_ref[:, hs], k_ref[:, hs], v_ref[:, hs], do_ref[:, hs]
            sc = _dot_nt(qh, kh) * scale
            pr = jnp.exp(sc - jnp.max(sc, axis=1, keepdims=True))
            pr = pr / jnp.sum(pr, axis=1, keepdims=True)
            dp = _dot_nt(doh, vh)
            dsc = pr * (dp - jnp.sum(pr * dp, axis=1, keepdims=True)) * scale
            dq_ref[:, hs] = _dot(dsc, kh).astype(dq_ref.dtype)
            dk_ref[:, hs] += _dot_tn(dsc, qh)
            dv_ref[:, hs] += _dot_tn(pr, doh)

    blk = pl.BlockSpec((tm, D_MODEL), lambda i: (i, 0))
    kv = pl.BlockSpec((ml, D_MODEL), lambda i: (0, 0))
    return pl.pallas_call(body, grid=(s // tm,), in_specs=[blk, kv, kv, blk], out_specs=(blk, kv, kv),
                          out_shape=(SDS((s, D_MODEL), BF16), SDS((ml, D_MODEL), F32), SDS((ml, D_MODEL), F32)), name=name,
                          compiler_params=_params(("arbitrary",), 32 << 20))(q, k, v, do)


MATS = (("w_in", (1024, 940), 1), ("w_uq", (384, 384), 1), ("w_ukv", (256, 512), 1), ("w_out", (512, 1024), 0),
        ("ssm_conv_w", (4, 512), 1),
        ("w_mq", (256, 1024), 0), ("w_mk", (256, 1024), 0), ("w_mv", (256, 1024), 0), ("w_mo", (256, 1024), 0),
        ("w_up", (1024, 1408), 1), ("w_down", (704, 1024), 0), ("ffn_conv_w", (3, 1408), 1))
GROUPS = {"proj": (0,), "mixer": (1, 2, 3, 4), "mem": (5, 6, 7, 8), "ffn": (9, 10, 11)}
UP_SHARD_COLS = 1408
F32_ON_WIRE = ("ssm_conv_w", "ffn_conv_w")
SMALL = (("norm_mix", 1024), ("ssm_conv_b", 2048), ("dt_bias", 16), ("a_log", 16), ("d_skip", 16), ("ssm_norm", 1024),
         ("q_norm", 384), ("kv_norm", 256), ("attn_out_norm", 1024), ("norm_mem_q", 1024), ("norm_mem_kv", 1024),
         ("norm_ffn", 1024), ("ffn_conv_b", 5632))
PACK_COLS = 1024


def _pad_cols(t, n):
    return jnp.pad(t, ((0, 0),) * (t.ndim - 1) + ((0, n - t.shape[-1]),))


def _w_in_to_padded(t):
    z, xbc, dt, cq, ckv, kr = jnp.split(t, (1024, 3072, 3088, 3472, 3728), axis=-1)
    return jnp.concatenate([xbc, z, cq, _pad_cols(dt, LANES), ckv, _pad_cols(kr, P_IN - P_KR)], axis=-1)


def _w_in_from_padded(t):
    return jnp.concatenate([t[..., P_Z:P_Z + 1024], t[..., P_XBC:P_XBC + 2048], t[..., P_DT:P_DT + SSM_HEADS],
                            t[..., P_CQ:P_CQ + Q_LORA], t[..., P_CKV:P_CKV + KV_LORA], t[..., P_KR:P_KR + QK_ROPE]], axis=-1)


def _cols_joined(g):
    return jnp.concatenate([g[j] for j in range(N_CHIPS)], axis=-1)


def _cols_by_chip(t, dtype):
    k = t.shape[0]
    return t.reshape(k, N_CHIPS, -1).transpose(1, 0, 2).astype(dtype)


def _rows_by_chip(t):
    return t.reshape(N_CHIPS, -1, t.shape[-1])


def _mixer_weights(gw):
    wl = {}
    uq = _cols_joined(gw["w_uq"]).reshape(Q_LORA, MLA_HEADS, QK_NOPE + QK_ROPE)
    wl["w_uq"] = _pad_cols(uq, HEAD_PAD).reshape(Q_LORA, MLA_HEADS * HEAD_PAD)
    wl["w_ukv"] = _cols_joined(gw["w_ukv"])
    wl["ssm_conv_w"] = _cols_joined(gw["ssm_conv_w"])
    return wl


def _layer_fwd(x0, mem, cos, sins, weights, sp, li):
    n = lambda t: f"l{li}_{t}"
    lead = ()
    sv = {"x0": x0}
    gw = dict(weights("proj", x0))
    w_in = _w_in_to_padded(_cols_joined(gw["w_in"]))
    h = _rms_fwd(x0, sp["norm_mix"], name=n("mix_norm"))
    in_hbm = lambda t: pltpu.with_memory_space_constraint(t, pltpu.HBM)
    proj = in_hbm(_mm(h, w_in, name=n("mix_proj")))
    gw.update(weights("mixer", proj))
    wl = dict(_mixer_weights(gw), w_in=w_in)
    xbc = in_hbm(_ssm_conv_fwd(proj, wl["ssm_conv_w"], sp["ssm_conv_b"], name=n("ssm_conv")))
    y, pstates = _ssd_fwd(xbc, proj, sp["dt_bias"], sp["a_log"], sp["d_skip"], name=n("ssd"))
    y_ssm = _gated_rms_fwd(y, proj, sp["ssm_norm"], name=n("ssm_gate"))
    cqn = _rms_fwd(proj, sp["q_norm"], col=(Q_LORA, P_CQ // Q_LORA), name=n("q_norm"))
    ckvn = _rms_fwd(proj, sp["kv_norm"], col=(KV_LORA, P_CKV // KV_LORA), name=n("kv_norm"))
    q = in_hbm(_mm(cqn, wl["w_uq"], name=n("uq")))
    kv = in_hbm(_mm(ckvn, wl["w_ukv"], name=n("ukv")))
    qr, kr, v = _mla_prep(q, kv, proj, cos, sins, name=n("rope"))
    att, lse = _flash_fwd(qr, kr, v, name=n("flash"))
    y_att = _rms_fwd(att, sp["attn_out_norm"], name=n("att_norm"))
    x1 = _mm(y_ssm, gw["w_out"], b_lead=lead, b_rows=(0, D_SSM), res=x0, name=n("out_a"))
    x1 = _mm(y_att, gw["w_out"], b_lead=lead, b_rows=(D_SSM, D_SSM), res=x1, name=n("out_b"))
    sv.update(h=h, proj=proj, xbc=xbc, y=y, pstates=pstates, y_ssm=y_ssm, cqn=cqn, ckvn=ckvn, qr=qr, kr=kr, v=v,
              att=att, lse=lse, y_att=y_att, x1=x1)
    gw.update(weights("mem", x1))
    hq = _rms_fwd(x1, sp["norm_mem_q"], name=n("memq_norm"))
    hm = _rms_fwd(mem, sp["norm_mem_kv"], name=n("memkv_norm"))
    mq = _mm(hq, gw["w_mq"], b_lead=lead, out_dtype=BF16, name=n("mq"))
    mk = _mm(hm, gw["w_mk"], b_lead=lead, out_dtype=BF16, name=n("mk"))
    mv = _mm(hm, gw["w_mv"], b_lead=lead, out_dtype=BF16, name=n("mv"))
    mo = _mem_attn_fwd(mq, mk, mv, name=n("mem_attn"))
    x2 = _mm(mo, gw["w_mo"], b_lead=lead, res=x1, name=n("mo"))
    sv.update(hq=hq, hm=hm, mq=mq, mk=mk, mv=mv, mo=mo, x2=x2)
    gw.update(weights("ffn", x2))
    wl["ffn_conv_w"] = _cols_joined(gw["ffn_conv_w"])
    hf = _rms_fwd(x2, sp["norm_ffn"], name=n("ffn_norm"))
    up_g = _mm(hf, gw["w_up"], b_lead=lead, b_chips=(0, 2), name=n("up_g"))
    up_v = _mm(hf, gw["w_up"], b_lead=lead, b_chips=(2, 2), name=n("up_v"))
    act = _ffn_conv_fwd(up_g, up_v, wl["ffn_conv_w"], sp["ffn_conv_b"], name=n("ffn_conv"))
    x3 = _mm(act, gw["w_down"], b_lead=lead, res=x2, name=n("down"))
    sv.update(hf=hf, up_g=up_g, up_v=up_v, act=act)
    return x3, sv, gw, wl


def _layer_bwd(dx3, dx3b, mem, cos, sins, gw, wl, sp, sv, li, emit):
    n = lambda t: f"l{li}_b_{t}"
    lead = ()
    g = {}

    def after(token, v):
        return v if token is None else v + token[0, 0]

    in_hbm = lambda t: pltpu.with_memory_space_constraint(t, pltpu.HBM)
    dact = _mm(dx3b, gw["w_down"], tb=True, b_lead=lead, out_dtype=BF16, name=n("down_dx"))
    g["w_down"] = _rows_by_chip(_mm(sv["act"], dx3b, ta=True, out_dtype=BF16, name=n("down_dw")))
    dup_g, dup_v, dcw, g["ffn_conv_b"] = _ffn_conv_bwd(
        sv["up_g"], sv["up_v"], wl["ffn_conv_w"], sp["ffn_conv_b"], dact, name=n("ffn_conv"))
    g["ffn_conv_w"] = _cols_by_chip(dcw, F32)
    nsh = UP_SHARD_COLS
    dhf = None
    for c4 in range(N_CHIPS):
        dhf = _mm(dup_g if c4 < 2 else dup_v, gw["w_up"], tb=True, a_col=(nsh, c4 % 2), b_lead=(c4,), res=dhf,
                  name=n(f"up{c4}_dx"))
    g["w_up"] = jnp.concatenate([_mm(sv["hf"], dup_g, ta=True, o_chips=nsh, out_dtype=BF16, name=n("upg_dw")),
                                 _mm(sv["hf"], dup_v, ta=True, o_chips=nsh, out_dtype=BF16, name=n("upv_dw"))], axis=0)
    dx2, dx2b, g["norm_ffn"] = _rms_bwd(sv["x2"], after(emit("ffn", g), sp["norm_ffn"]), dhf, dx3, name=n("ffn_norm"))
    dmo = _mm(dx2b, gw["w_mo"], tb=True, b_lead=lead, out_dtype=BF16, name=n("mo_dx"))
    g["w_mo"] = _rows_by_chip(_mm(sv["mo"], dx2b, ta=True, out_dtype=BF16, name=n("mo_dw")))
    dmq, dmk, dmv = _mem_attn_bwd(sv["mq"], sv["mk"], sv["mv"], dmo, name=n("mem_attn"))
    dhq = in_hbm(_mm(dmq, gw["w_mq"], tb=True, b_lead=lead, name=n("mq_dx")))
    g["w_mq"] = _rows_by_chip(_mm(sv["hq"], dmq, ta=True, out_dtype=BF16, name=n("mq_dw")))
    dhm = _mm(dmk, gw["w_mk"], tb=True, b_lead=lead, name=n("mk_dx"))
    dhm = _mm(dmv, gw["w_mv"], tb=True, b_lead=lead, res=dhm, name=n("mv_dx"))
    g["w_mk"] = _rows_by_chip(_mm(sv["hm"], dmk, ta=True, out_dtype=BF16, name=n("mk_dw")))
    g["w_mv"] = _rows_by_chip(_mm(sv["hm"], dmv, ta=True, out_dtype=BF16, name=n("mv_dw")))
    dx1, dx1b, g["norm_mem_q"] = _rms_bwd(sv["x1"], after(emit("mem", g), sp["norm_mem_q"]), dhq, dx2, name=n("memq_norm"))
    _, _, g["norm_mem_kv"] = _rms_bwd(mem, sp["norm_mem_kv"], dhm, name=n("memkv_norm"))
    dy_ssm = in_hbm(_mm(dx1b, gw["w_out"], tb=True, b_lead=lead, b_rows=(0, D_SSM), name=n("outa_dx")))
    dy_att = in_hbm(_mm(dx1b, gw["w_out"], tb=True, b_lead=lead, b_rows=(D_SSM, D_SSM), name=n("outb_dx")))
    g["w_out"] = _rows_by_chip(jnp.concatenate([_mm(sv["y_ssm"], dx1b, ta=True, out_dtype=BF16, name=n("outa_dw")),
                                                _mm(sv["y_att"], dx1b, ta=True, out_dtype=BF16, name=n("outb_dw"))], axis=0))
    datt, _, g["attn_out_norm"] = _rms_bwd(sv["att"], sp["attn_out_norm"], dy_att, name=n("att_norm"))
    dqr, dkr, dv = [in_hbm(t) for t in _flash_bwd(sv["qr"], sv["kr"], sv["v"], sv["att"], sv["lse"], in_hbm(datt), name=n("flash"))]
    dq, dkv, dkrope = _mla_prep_bwd(dqr, dkr, dv, cos, sins, name=n("rope"))
    duq = _mm(sv["cqn"], dq, ta=True, name=n("uq_dw")).reshape(Q_LORA, MLA_HEADS, HEAD_PAD)[..., :QK_NOPE + QK_ROPE]
    g["w_uq"] = _cols_by_chip(duq.reshape(Q_LORA, -1), BF16)
    dcqn = _mm(dq, wl["w_uq"], tb=True, name=n("uq_dx"))
    g["w_ukv"] = _cols_by_chip(_mm(sv["ckvn"], dkv, ta=True, name=n("ukv_dw")), BF16)
    dckvn = _mm(dkv, wl["w_ukv"], tb=True, name=n("ukv_dx"))
    proj = sv["proj"]
    _, dcq, g["q_norm"] = _rms_bwd(proj, sp["q_norm"], dcqn, col=(Q_LORA, P_CQ // Q_LORA), name=n("q_norm"))
    _, dckv, g["kv_norm"] = _rms_bwd(proj, sp["kv_norm"], dckvn, col=(KV_LORA, P_CKV // KV_LORA), name=n("kv_norm"))
    dy, dz, g["ssm_norm"] = _gated_rms_bwd(sv["y"], proj, sp["ssm_norm"], dy_ssm, name=n("ssm_gate"))
    dxbc, ddt, da_log, dd_skip, ddt_bias = _ssd_bwd(
        sv["xbc"], proj, sp["dt_bias"], sp["a_log"], sp["d_skip"], sv["pstates"], in_hbm(dy), name=n("ssd"))
    dxbc = in_hbm(dxbc)
    g["a_log"], g["d_skip"], g["dt_bias"] = da_log[0, :SSM_HEADS], dd_skip[0, :SSM_HEADS], ddt_bias[0, :SSM_HEADS]
    dxbc_pre, dsw, g["ssm_conv_b"] = _ssm_conv_bwd(proj, wl["ssm_conv_w"], sp["ssm_conv_b"], dxbc, name=n("ssm_conv"))
    g["ssm_conv_w"] = _cols_by_chip(dsw, F32)
    started = emit("mixer", g)
    s = proj.shape[0]
    dproj = jnp.concatenate([dxbc_pre, dz, dcq, ddt, dckv, dkrope,
                             jnp.zeros((s, P_IN - P_KR - LANES), BF16)], axis=1)
    dh = in_hbm(_mm(dproj, wl["w_in"], tb=True, name=n("proj_dx")))
    g["w_in"] = _cols_by_chip(_w_in_from_padded(_mm(sv["h"], dproj, ta=True, name=n("proj_dw"))), BF16)
    dx0, dx0b, g["norm_mix"] = _rms_bwd(sv["x0"], after(started, sp["norm_mix"]), dh, dx1, name=n("mix_norm"))
    return dx0, dx0b, g, emit("proj", g)


def _chip_peers(x, y):
    return [(1 - x, y), (x, 1 - y), (1 - x, 1 - y)]


HBM_SPEC = pl.BlockSpec(memory_space=pltpu.HBM)
SEM_SPEC = pl.BlockSpec(memory_space=pltpu.SEMAPHORE)
ANY_SPEC = pl.BlockSpec(memory_space=pl.ANY)
VMEM_SPEC = pl.BlockSpec(memory_space=pltpu.VMEM)
DATAFLOW = pltpu.SideEffectType.DATAFLOW_SIDE_EFFECTING
TOKEN_SHAPE = (8, LANES)


def _exchange_start(srcs, land_shapes, src_view, dst_view, token, *, name):
    n = len(srcs)

    def body(*refs):
        s, l, tok_in = refs[:n], refs[n:2 * n], refs[2 * n]
        send_sems, recv_sems = refs[2 * n + 1], refs[2 * n + 2]
        tok_out = refs[-1]
        x, y, c = lax.axis_index("x"), lax.axis_index("y"), lax.axis_index("c")
        me = 2 * x + y
        for t in range(n):
            for k, (px, py) in enumerate(_chip_peers(x, y)):
                pltpu.make_async_remote_copy(
                    src_ref=src_view(t, s[t], 2 * px + py), dst_ref=dst_view(t, l[t], me), send_sem=send_sems.at[3 * t + k],
                    recv_sem=recv_sems.at[3 * t + k], device_id=(px, py, c), device_id_type=MESH).start()
            pltpu.make_async_copy(src_view(t, s[t], me), dst_view(t, l[t], me), send_sems.at[3 * n + t]).start()
        tok_out[...] = tok_in[...]

    hbm = lambda t: pltpu.with_memory_space_constraint(t, pltpu.HBM)
    lands = [lax.empty(l.shape, l.dtype) for l in land_shapes]
    outs = pl.pallas_call(
        body, name=name,
        out_shape=(pltpu.SemaphoreType.DMA((4 * n,)), pltpu.SemaphoreType.DMA((3 * n,)),
                   *[pltpu.HBM(l.shape, l.dtype) for l in land_shapes], SDS(TOKEN_SHAPE, F32)),
        in_specs=[HBM_SPEC] * (2 * n) + [VMEM_SPEC], out_specs=(SEM_SPEC, SEM_SPEC, *[HBM_SPEC] * n, VMEM_SPEC),
        input_output_aliases={n + t: 2 + t for t in range(n)},
        compiler_params=pltpu.CompilerParams(has_side_effects=DATAFLOW))(*[hbm(t) for t in srcs], *[hbm(t) for t in lands], token)
    return outs[0], outs[1], list(outs[2:2 + n]), outs[-1]


def _exchange_wait(srcs, lands, send_sems, recv_sems, after, src_view, dst_view, which, *, name):
    n = len(srcs)
    m = len(which)

    def body(*refs):
        s, l = refs[:m], refs[m:2 * m]
        send_ref, recv_ref = refs[2 * m], refs[2 * m + 1]
        x, y, c = lax.axis_index("x"), lax.axis_index("y"), lax.axis_index("c")
        me = 2 * x + y
        for i, t in enumerate(which):
            for k, (px, py) in enumerate(_chip_peers(x, y)):
                chip = 2 * px + py
                cp = pltpu.make_async_remote_copy(
                    src_ref=src_view(t, s[i], chip), dst_ref=dst_view(t, l[i], chip), send_sem=send_ref.at[3 * t + k],
                    recv_sem=recv_ref.at[3 * t + k], device_id=(px, py, c), device_id_type=MESH)
                cp.wait_send()
                cp.wait_recv()
            pltpu.make_async_copy(src_view(t, s[i], me), dst_view(t, l[i], me), send_ref.at[3 * n + t]).wait()

    outs = pl.pallas_call(
        body, name=name, out_shape=[pltpu.HBM(lands[t].shape, lands[t].dtype) for t in which],
        in_specs=[HBM_SPEC] * (2 * m) + [SEM_SPEC, SEM_SPEC, ANY_SPEC], out_specs=[HBM_SPEC] * m,
        input_output_aliases={m + i: i for i in range(m)},
        compiler_params=pltpu.CompilerParams(has_side_effects=DATAFLOW))(
            *[srcs[t] for t in which], *[lands[t] for t in which], send_sems, recv_sems, after)
    return list(outs)


def _gather_layer_start(shards, li, token, tag=""):
    src_view = lambda t, ref, chip: ref.at[li]
    dst_view = lambda t, ref, chip: ref.at[chip]
    send_sems, recv_sems, lands, token = _exchange_start(
        shards, [SDS((N_CHIPS,) + s.shape[1:], s.dtype) for s in shards], src_view, dst_view, token,
        name=f"gather{li}{tag}_start")
    return (shards, lands, send_sems, recv_sems, src_view, dst_view, f"gather{li}{tag}"), token


def _scatter_start(grads, tag, token):
    view = lambda t, ref, chip: ref.at[chip]
    send_sems, recv_sems, lands, token = _exchange_start(
        grads, [SDS(g.shape, g.dtype) for g in grads], view, view, token, name=f"scatter{tag}_start")
    return (grads, lands, send_sems, recv_sems, view, view, f"scatter{tag}"), token


def _exchange_finish(handle, after, which=None, tag=""):
    srcs, lands, send_sems, recv_sems, src_view, dst_view, name = handle
    which = tuple(range(len(srcs))) if which is None else which
    return _exchange_wait(srcs, lands, send_sems, recv_sems, after, src_view, dst_view, which, name=f"{name}{tag}_wait")


def _swap_start(bufs, token, *, name):
    n = len(bufs)

    def body(*refs):
        s, l, tok_in = refs[:n], refs[n:2 * n], refs[2 * n]
        send_sems, recv_sems = refs[2 * n + 1], refs[2 * n + 2]
        x, y, c = lax.axis_index("x"), lax.axis_index("y"), lax.axis_index("c")
        for t in range(n):
            pltpu.make_async_remote_copy(src_ref=s[t], dst_ref=l[t], send_sem=send_sems.at[t], recv_sem=recv_sems.at[t],
                                         device_id=(x, y, 1 - c), device_id_type=MESH).start()
        refs[-1][...] = tok_in[...]

    hbm = lambda t: pltpu.with_memory_space_constraint(t, pltpu.HBM)
    lands = [lax.empty(b.shape, b.dtype) for b in bufs]
    outs = pl.pallas_call(
        body, name=f"{name}_start",
        out_shape=(pltpu.SemaphoreType.DMA((n,)), pltpu.SemaphoreType.DMA((n,)),
                   *[pltpu.HBM(b.shape, b.dtype) for b in bufs], SDS(TOKEN_SHAPE, F32)),
        in_specs=[HBM_SPEC] * (2 * n) + [VMEM_SPEC], out_specs=(SEM_SPEC, SEM_SPEC, *[HBM_SPEC] * n, VMEM_SPEC),
        input_output_aliases={n + t: 2 + t for t in range(n)},
        compiler_params=pltpu.CompilerParams(has_side_effects=DATAFLOW))(*[hbm(t) for t in bufs], *[hbm(t) for t in lands], token)
    return (bufs, list(outs[2:2 + n]), outs[0], outs[1], name), outs[-1]


def _swap_wait(handle, after):
    bufs, lands, send_sems, recv_sems, name = handle
    n = len(bufs)

    def body(*refs):
        s, l = refs[:n], refs[n:2 * n]
        send_ref, recv_ref = refs[2 * n], refs[2 * n + 1]
        x, y, c = lax.axis_index("x"), lax.axis_index("y"), lax.axis_index("c")
        for t in range(n):
            cp = pltpu.make_async_remote_copy(src_ref=s[t], dst_ref=l[t], send_sem=send_ref.at[t], recv_sem=recv_ref.at[t],
                                              device_id=(x, y, 1 - c), device_id_type=MESH)
            cp.wait_send()
            cp.wait_recv()

    outs = pl.pallas_call(
        body, name=f"{name}_wait", out_shape=[pltpu.HBM(b.shape, b.dtype) for b in bufs],
        in_specs=[HBM_SPEC] * (2 * n) + [SEM_SPEC, SEM_SPEC, ANY_SPEC], out_specs=[HBM_SPEC] * n,
        input_output_aliases={n + t: t for t in range(n)},
        compiler_params=pltpu.CompilerParams(has_side_effects=DATAFLOW))(*bufs, *lands, send_sems, recv_sems, after)
    return list(outs)


def _all_gather8(src, *, name):
    def body(src_ref, out_ref, send_sems, recv_sems, local_sem):
        x, y, c = lax.axis_index("x"), lax.axis_index("y"), lax.axis_index("c")
        me = 4 * x + 2 * y + c
        mine = pltpu.make_async_copy(src_ref, out_ref.at[me], local_sem)
        mine.start()

        def peer(k):
            return (x ^ (k >> 2 & 1), y ^ (k >> 1 & 1), c ^ (k & 1))

        sends = []
        for k in range(1, N_DEV):
            cp = pltpu.make_async_remote_copy(src_ref=src_ref, dst_ref=out_ref.at[me], send_sem=send_sems.at[k - 1],
                                              recv_sem=recv_sems.at[k - 1], device_id=peer(k), device_id_type=MESH)
            cp.start()
            sends.append(cp)
        for k in range(1, N_DEV):
            px, py, pc = peer(k)
            pltpu.make_async_remote_copy(src_ref=src_ref, dst_ref=out_ref.at[4 * px + 2 * py + pc],
                                         send_sem=send_sems.at[k - 1], recv_sem=recv_sems.at[k - 1],
                                         device_id=peer(k), device_id_type=MESH).wait_recv()
        for cp in sends:
            cp.wait_send()
        mine.wait()

    any_spec = pl.BlockSpec(memory_space=pl.ANY)
    return pl.pallas_call(
        body, in_specs=[any_spec], out_specs=any_spec, out_shape=SDS((N_DEV,) + src.shape, src.dtype),
        scratch_shapes=[pltpu.SemaphoreType.DMA((N_DEV - 1,)), pltpu.SemaphoreType.DMA((N_DEV - 1,)), pltpu.SemaphoreType.DMA],
        name=name)(src)


def _adam_terms(w, g, m, v):
    m = ADAM_B1 * m + (1.0 - ADAM_B1) * g
    v = ADAM_B2 * v + (1.0 - ADAM_B2) * (g * g)
    m_hat = m / (1.0 - ADAM_B1 ** ADAM_STEP)
    v_hat = v / (1.0 - ADAM_B2 ** ADAM_STEP)
    delta = -ADAM_LR * (m_hat / (jnp.sqrt(v_hat) + ADAM_EPS) + ADAM_WD * w)
    return delta, m, v


def _adamw_shard(mine, other, w, m, v, *, name):
    d, a, b = w.shape
    tr = next((t for t in (128, 64, 32, 16) if a % t == 0), a)

    def body(*refs):
        ga, gb = refs[:d], refs[d:2 * d]
        w_ref, m_ref, v_ref, g_ref, d_ref, nm_ref, nv_ref = refs[2 * d:]

        def plane(ref):
            return ((ref[0].astype(F32) + ref[1].astype(F32)) + ref[2].astype(F32)) + ref[3].astype(F32)

        for lp in range(d):
            @pl.when(pl.program_id(0) == lp)
            def _(lp=lp):
                g = plane(ga[lp]) + plane(gb[lp])
                delta, mn, vn = _adam_terms(w_ref[...], g, m_ref[...], v_ref[...])
                g_ref[...] = g
                d_ref[...] = delta
                nm_ref[...] = mn
                nv_ref[...] = vn

    gspecs = [pl.BlockSpec((N_CHIPS, tr, b), lambda l, i, lp=lp: (0, jnp.where(l == lp, i, 0), 0)) for lp in range(d)]
    blk = pl.BlockSpec((None, tr, b), lambda l, i: (l, i, 0))
    shp = SDS((d, a, b), F32)
    return pl.pallas_call(
        body, grid=(d, a // tr), in_specs=gspecs + gspecs + [blk, blk, blk], out_specs=(blk,) * 4, out_shape=(shp,) * 4,
        name=name, compiler_params=_params(("arbitrary", "arbitrary"), 48 << 20))(*mine, *other, w, m, v)


def _adamw_small(g8, w, m, v, *, name):
    n = w.shape[1]

    def body(g8_ref, w_ref, m_ref, v_ref, g_ref, d_ref, nm_ref, nv_ref):
        g = g8_ref[0]
        for k in range(1, N_DEV):
            g = g + g8_ref[k]
        delta, mn, vn = _adam_terms(w_ref[...], g, m_ref[...], v_ref[...])
        g_ref[...] = g
        d_ref[...] = delta
        nm_ref[...] = mn
        nv_ref[...] = vn

    shp = SDS((1, n), F32)
    return pl.pallas_call(body, out_shape=(shp,) * 4, name=name, compiler_params=_params(None, 24 << 20))(g8, w, m, v)


def _rope_tables(positions):
    inv_freq = 1.0 / (ROPE_THETA ** (jnp.arange(0, QK_ROPE, 2, dtype=F32) / QK_ROPE))
    ang = positions.astype(F32)[:, None] * inv_freq
    c, s = jnp.cos(ang), jnp.sin(ang)
    n = positions.shape[0]
    pad = jnp.zeros((n, HEAD_PAD - QK_NOPE - QK_ROPE), F32)
    cos = jnp.concatenate([jnp.ones((n, QK_NOPE), F32), c, c, pad], axis=1)
    sins = jnp.concatenate([jnp.zeros((n, QK_NOPE), F32), -s, s, pad], axis=1)
    return cos, sins


def _pad_lanes(v):
    return _pad_cols(v.reshape(1, -1), LANES)


def _local_step(x, mem, positions, weights, small, final_norm, loss_target, emit, token):
    cos, sins = _rope_tables(positions)
    saved, gws, wls, sps = [], [], [], []
    h = x
    for li in range(DEPTH):
        sp = {k: small[k][li] for k, _ in SMALL}
        if li == 0:
            sp["norm_mix"] = sp["norm_mix"] + token[0, 0]
        for k in ("dt_bias", "a_log", "d_skip"):
            sp[k] = _pad_lanes(sp[k])
        h, sv, gw, wl = _layer_fwd(h, mem, cos, sins, functools.partial(weights, li), sp, li)
        saved.append(sv)
        gws.append(gw)
        wls.append(wl)
        sps.append(sp)
    loss, dh, dhb, g_final = _final_loss(h, final_norm, loss_target, name="final_loss")
    grads = [None] * DEPTH
    started = None
    for li in reversed(range(DEPTH)):
        sp = sps[li]
        if started is not None:
            sp = dict(sp, ffn_conv_b=sp["ffn_conv_b"] + started[0, 0])
        dh, dhb, grads[li], started = _layer_bwd(dh, dhb, mem, cos, sins, gws[li], wls[li], sp, saved[li], li,
                                                 functools.partial(emit, li))
    return loss, dh, grads, g_final


def _gathered_views(which, lands):
    return {MATS[t][0]: (b.reshape(-1, b.shape[-1]) if MATS[t][2] == 0 else b) for t, b in zip(which, lands)}


def kernel(x, mem, positions, norm_mix, w_in, ssm_conv_w, ssm_conv_b, dt_bias, a_log, d_skip, ssm_norm, q_norm, w_uq, kv_norm, w_ukv, attn_out_norm, w_out, norm_mem_q, norm_mem_kv, w_mq, w_mk, w_mv, w_mo, norm_ffn, w_up, ffn_conv_w, ffn_conv_b, w_down, final_norm, loss_target, m_norm_mix, m_w_in, m_ssm_conv_w, m_ssm_conv_b, m_dt_bias, m_a_log, m_d_skip, m_ssm_norm, m_q_norm, m_w_uq, m_kv_norm, m_w_ukv, m_attn_out_norm, m_w_out, m_norm_mem_q, m_norm_mem_kv, m_w_mq, m_w_mk, m_w_mv, m_w_mo, m_norm_ffn, m_w_up, m_ffn_conv_w, m_ffn_conv_b, m_w_down, m_final_norm, v_norm_mix, v_w_in, v_ssm_conv_w, v_ssm_conv_b, v_dt_bias, v_a_log, v_d_skip, v_ssm_norm, v_q_norm, v_w_uq, v_kv_norm, v_w_ukv, v_attn_out_norm, v_w_out, v_norm_mem_q, v_norm_mem_kv, v_w_mq, v_w_mk, v_w_mv, v_w_mo, v_norm_ffn, v_w_up, v_ffn_conv_w, v_ffn_conv_b, v_w_down, v_final_norm):
    args = dict(locals())
    names = ["norm_mix", "w_in", "ssm_conv_w", "ssm_conv_b", "dt_bias", "a_log", "d_skip", "ssm_norm", "q_norm", "w_uq",
             "kv_norm", "w_ukv", "attn_out_norm", "w_out", "norm_mem_q", "norm_mem_kv", "w_mq", "w_mk", "w_mv", "w_mo",
             "norm_ffn", "w_up", "ffn_conv_w", "ffn_conv_b", "w_down", "final_norm"]
    wts = {k: args[k] for k in names}
    mom = {k: args["m_" + k] for k in names}
    var = {k: args["v_" + k] for k in names}
    mat_names = [k for k, _, _ in MATS]

    shards = [wts[k] if k in F32_ON_WIRE else wts[k].astype(BF16) for k in mat_names]
    token = jnp.zeros(TOKEN_SHAPE, F32)
    first, token = _gather_layer_start(shards[:1], 0, token, tag="_first")
    gathers = []
    for li in range(DEPTH):
        handle, token = _gather_layer_start(shards[1:] if li == 0 else shards, li, token)
        gathers.append(handle)
    small = {k: wts[k] for k, _ in SMALL}

    def weights(li, group, after):
        which = GROUPS[group]
        if li > 0:
            return _gathered_views(which, _exchange_finish(gathers[li], after, which, tag=f"_{group}"))
        if group == "proj":
            return _gathered_views(which, _exchange_finish(first, after))
        return _gathered_views(which, _exchange_finish(gathers[0], after, tuple(t - 1 for t in which), tag=f"_{group}"))

    scatters = [[] for _ in range(DEPTH)]
    nm = len(mat_names)
    mine = [[None] * nm for _ in range(DEPTH)]
    swaps = [None] * DEPTH
    last_started = [None]

    def swap_layer(li, after):
        for which, handle in scatters[li]:
            for t, b in zip(which, _exchange_finish(handle, after)):
                mine[li][t] = b
        swaps[li], started = _swap_start(mine[li], jnp.zeros(TOKEN_SHAPE, F32), name=f"swap{li}")
        return started

    def emit(li, group, g):
        last = group == "proj"
        if li == 0:
            which = GROUPS[group]
        elif last:
            which = tuple(range(nm))
        else:
            return None
        handle, started = _scatter_start([g[MATS[t][0]] for t in which], f"{li}_{group}", jnp.zeros(TOKEN_SHAPE, F32))
        scatters[li].append((which, handle))
        last_started[0] = started
        if li + 1 < DEPTH and group == ("mixer" if li == 0 else "proj"):
            started = started + swap_layer(li + 1, g["ssm_conv_w"])
        return started

    loss, grad_x, grads, g_final = _local_step(x[0], mem[0], positions[0], weights, small, wts["final_norm"],
                                               loss_target[0], emit, token)
    loss = lax.psum(loss, ("x", "y", "c"))

    w_in_t = GROUPS["proj"][0]
    others = [t for t in range(nm) if t != w_in_t]
    for which, handle in scatters[0]:
        if which != GROUPS["proj"]:
            for t, b in zip(which, _exchange_finish(handle, grad_x)):
                mine[0][t] = b
    swaps[0], _ = _swap_start([mine[0][t] for t in others], last_started[0], name="swap0")
    other = [dict(zip(others, _swap_wait(swaps[0], grad_x)))] + [
        dict(enumerate(_swap_wait(swaps[li], grad_x))) for li in range(1, DEPTH)]

    def adamw(t):
        k = mat_names[t]
        return _adamw_shard([mine[li][t] for li in range(DEPTH)], [other[li][t] for li in range(DEPTH)],
                            wts[k], mom[k], var[k], name=f"adamw_{k}")

    mat_out = {mat_names[t]: adamw(t) for t in others}
    done = sum(mat_out[mat_names[t]][0][0, 0, :1] for t in others)
    (last_handle,) = [handle for which, handle in scatters[0] if which == GROUPS["proj"]]
    (mine[0][w_in_t],) = _exchange_finish(last_handle, done)
    last_swap, _ = _swap_start([mine[0][w_in_t]], jnp.zeros(TOKEN_SHAPE, F32), name="swap0_last")
    (other[0][w_in_t],) = _swap_wait(last_swap, done)
    mat_out[mat_names[w_in_t]] = adamw(w_in_t)

    def pack_small(get, fin):
        flat = [get(k).reshape(-1) for k, _ in SMALL] + [fin.reshape(-1)]
        n = sum(f.shape[0] for f in flat)
        return jnp.concatenate(flat + [jnp.zeros((-n % PACK_COLS,), F32)]).reshape(1, -1)

    gs = pack_small(lambda k: jnp.stack([grads[li][k] for li in range(DEPTH)]), g_final)
    g8 = _all_gather8(gs, name="gather_small_grads")
    small_out = _adamw_small(g8, pack_small(wts.get, wts["final_norm"]), pack_small(mom.get, mom["final_norm"]),
                             pack_small(var.get, var["final_norm"]), name="adamw_small")

    def unpack_small(buf):
        out, off = {}, 0
        for k, nel in SMALL:
            out[k] = buf[0, off:off + DEPTH * nel].reshape(DEPTH, nel)
            off += DEPTH * nel
        out["final_norm"] = buf[0, off:off + D_MODEL]
        return out

    small_res = [unpack_small(b) for b in small_out]
    res = []
    for kind in range(4):
        for k in names:
            res.append(small_res[kind][k] if k in small_res[kind] else mat_out[k][kind])
    return (loss, grad_x[None], *res)
```

```python
import functools
import math

import jax
import jax.numpy as jnp
from jax import lax
from jax.experimental import pallas as pl
from jax.experimental.pallas import tpu as pltpu

F32 = jnp.float32
BF16 = jnp.bfloat16
HIGHEST = lax.Precision.HIGHEST
SDS = jax.ShapeDtypeStruct
MESH = pl.DeviceIdType.MESH

D_MODEL = 1024
DEPTH = 4
EPS = 1e-6
SSM_HEADS = 16
SSM_HEAD_DIM = 64
D_SSM = 1024
SSM_GROUPS = 4
SSM_STATE = 128
SSM_CONV = 4
SSM_CHUNK = 128
CONV_CH = 2048
MLA_HEADS = 16
QK_NOPE = 64
QK_ROPE = 32
V_DIM = 64
Q_LORA = 384
KV_LORA = 256
ROPE_THETA = 10000.0
MEM_HEADS = 4
MEM_HEAD_DIM = 256
D_FF = 2816
FFN_CONV = 3
D_IN = 3760
ADAM_LR = 0.001
ADAM_B1 = 0.9
ADAM_B2 = 0.999
ADAM_EPS = 1e-08
ADAM_WD = 0.01
ADAM_STEP = 10

LANES = 128
HEAD_PAD = 128
N_CHIPS = 4
N_DEV = 8
VMEM_CAP_MB = 56

P_XBC, P_Z, P_CQ, P_DT, P_CKV, P_KR, P_IN = 0, 2048, 3072, 3456, 3584, 3840, 4096
NEG = -1e30


def _tile(n, pref):
    t = (min(n, pref) // LANES) * LANES
    while t >= LANES:
        if n % t == 0:
            return t
        t -= LANES
    return n


def _params(sem=None, vmem_bytes=None):
    kw = {}
    if sem is not None:
        kw["dimension_semantics"] = sem
    if vmem_bytes is not None:
        kw["vmem_limit_bytes"] = int(min(max(vmem_bytes, 16 << 20), VMEM_CAP_MB << 20))
    return pltpu.CompilerParams(**kw)


def _nbytes(shape, dtype):
    return math.prod(shape) * jnp.dtype(dtype).itemsize


def _mm(a, b, *, ta=False, tb=False, res=None, out_dtype=F32, name, a_col=None, b_lead=(), b_rows=None,
        b_chips=None, o_chips=None):
    if ta:
        k, m = a.shape
    else:
        m, k = (a.shape[0], a.shape[1] if a_col is None else a_col[0])
    rows_b, cols_b = b.shape[-2:]
    row0 = 0
    if b_rows is not None:
        row0, rows_b = b_rows
    nlead = len(b_lead)
    if b_chips is not None:
        assert not tb
        kb, tn, n = rows_b, cols_b, b_chips[1] * cols_b
        b_blk = (None,) * (1 + nlead) + (kb, tn)
        b_map = lambda i, j: (b_chips[0] + j,) + tuple(b_lead) + (0, 0)
    elif tb:
        n, kb = rows_b, cols_b
        tn = _tile(n, 512)
        assert row0 % tn == 0
        b_blk = (None,) * nlead + (tn, kb)
        b_map = lambda i, j: tuple(b_lead) + (j + row0 // tn, 0)
    else:
        kb, n = rows_b, cols_b
        tn = o_chips if o_chips else _tile(n, 512)
        assert row0 % kb == 0
        b_blk = (None,) * nlead + (kb, tn)
        b_map = lambda i, j: tuple(b_lead) + (row0 // kb, j)
    assert k == kb, (a.shape, b.shape, ta, tb, k, kb)
    tm = _tile(m, 1024)
    if ta:
        a_blk, a_map = (k, tm), (lambda i, j: (0, i))
    else:
        a_blk, a_map = (tm, k), ((lambda i, j: (i, 0)) if a_col is None else (lambda i, j: (i, a_col[1])))
    if o_chips:
        o_spec = pl.BlockSpec((None, tm, tn), lambda i, j: (j, i, 0))
        o_shape = SDS((n // tn, m, tn), out_dtype)
    else:
        o_spec = pl.BlockSpec((tm, tn), lambda i, j: (i, j))
        o_shape = SDS((m, n), out_dtype)
    dims = (((0 if ta else 1,), (1 if tb else 0,)), ((), ()))
    has_res = res is not None

    def body(*refs):
        a_ref, b_ref = refs[0], refs[1]
        o_ref = refs[-1]
        acc = lax.dot_general(a_ref[...].astype(BF16), b_ref[...].astype(BF16), dims, preferred_element_type=F32)
        if has_res:
            acc = acc + refs[2][...]
        o_ref[...] = acc.astype(o_ref.dtype)

    bb = tuple(d for d in b_blk if d is not None)
    vmem = 2 * (_nbytes(a_blk, a.dtype) + _nbytes(bb, b.dtype) + (2 if has_res else 1) * _nbytes((tm, tn), F32))
    vmem += _nbytes(a_blk, BF16) + _nbytes(bb, BF16) + 2 * _nbytes((tm, tn), F32) + (4 << 20)
    args = (a, b) + ((res,) if has_res else ())
    specs = [pl.BlockSpec(a_blk, a_map), pl.BlockSpec(b_blk, b_map)] + ([o_spec] if has_res else [])
    return pl.pallas_call(body, grid=(m // tm, n // tn), in_specs=specs, out_specs=o_spec, out_shape=o_shape, name=name,
                          compiler_params=_params(("parallel", "parallel"), vmem))(*args)


def _sigmoid(x):
    return 1.0 / (1.0 + jnp.exp(-x))


def _rms_fwd(x, g, *, col=None, name):
    s = x.shape[0]
    w, ci = (x.shape[1], 0) if col is None else col
    tm = min(s, 512)

    def body(x_ref, g_ref, o_ref):
        xv = x_ref[...].astype(F32)
        r = lax.rsqrt(jnp.mean(xv * xv, axis=-1, keepdims=True) + EPS)
        o_ref[...] = (xv * r * g_ref[...]).astype(o_ref.dtype)

    return pl.pallas_call(
        body, grid=(s // tm,),
        in_specs=[pl.BlockSpec((tm, w), lambda i: (i, ci)), pl.BlockSpec((1, w), lambda i: (0, 0))],
        out_specs=pl.BlockSpec((tm, w), lambda i: (i, 0)), out_shape=SDS((s, w), BF16), name=name,
        compiler_params=_params(("parallel",), 10 * tm * w * 4))(x, g.reshape(1, w))


def _rms_bwd(x, g, dy, dres=None, *, col=None, name):
    s = x.shape[0]
    w, ci = (x.shape[1], 0) if col is None else col
    tm = min(s, 512)
    has_res = dres is not None

    def body(*refs):
        x_ref, g_ref, dy_ref = refs[:3]
        dx_ref, dxb_ref, dg_ref = refs[-3:]
        xv = x_ref[...].astype(F32)
        dyv = dy_ref[...].astype(F32)
        r = lax.rsqrt(jnp.mean(xv * xv, axis=-1, keepdims=True) + EPS)
        u = dyv * g_ref[...]
        dx = r * u - xv * (r * r * r) * jnp.mean(xv * u, axis=-1, keepdims=True)
        if has_res:
            dx = dx + refs[3][...]
        dx_ref[...] = dx
        dxb_ref[...] = dx.astype(BF16)

        @pl.when(pl.program_id(0) == 0)
        def _():
            dg_ref[...] = jnp.zeros_like(dg_ref)

        dg_ref[...] += jnp.sum(dyv * xv * r, axis=0, keepdims=True)

    blk = pl.BlockSpec((tm, w), lambda i: (i, 0))
    specs = [pl.BlockSpec((tm, w), lambda i: (i, ci)), pl.BlockSpec((1, w), lambda i: (0, 0)), blk]
    args = [x, g.reshape(1, w), dy]
    if has_res:
        specs.append(blk)
        args.append(dres)
    dx, dxb, dg = pl.pallas_call(
        body, grid=(s // tm,), in_specs=specs,
        out_specs=(blk, blk, pl.BlockSpec((1, w), lambda i: (0, 0))),
        out_shape=(SDS((s, w), F32), SDS((s, w), BF16), SDS((1, w), F32)), name=name,
        compiler_params=_params(("arbitrary",), 18 * tm * w * 4))(*args)
    return dx, dxb, dg.reshape(w)


def _gated_rms_fwd(y, proj, g, *, name):
    s, w = y.shape
    tm = min(s, 512)

    def body(y_ref, z_ref, g_ref, o_ref):
        z = z_ref[...]
        t = y_ref[...] * (z * _sigmoid(z))
        r = lax.rsqrt(jnp.mean(t * t, axis=-1, keepdims=True) + EPS)
        o_ref[...] = (t * r * g_ref[...]).astype(o_ref.dtype)

    blk = pl.BlockSpec((tm, w), lambda i: (i, 0))
    return pl.pallas_call(
        body, grid=(s // tm,),
        in_specs=[blk, pl.BlockSpec((tm, w), lambda i: (i, P_Z // w)), pl.BlockSpec((1, w), lambda i: (0, 0))],
        out_specs=blk, out_shape=SDS((s, w), BF16), name=name,
        compiler_params=_params(("parallel",), 14 * tm * w * 4))(y, proj, g.reshape(1, w))


def _gated_rms_bwd(y, proj, g, dout, *, name):
    s, w = y.shape
    tm = min(s, 512)

    def body(y_ref, z_ref, g_ref, do_ref, dy_ref, dz_ref, dg_ref):
        z = z_ref[...]
        yv = y_ref[...]
        dov = do_ref[...]
        sg = _sigmoid(z)
        sz = z * sg
        t = yv * sz
        r = lax.rsqrt(jnp.mean(t * t, axis=-1, keepdims=True) + EPS)
        u = dov * g_ref[...]
        dt = r * u - t * (r * r * r) * jnp.mean(t * u, axis=-1, keepdims=True)
        dy_ref[...] = dt * sz
        dz_ref[...] = (dt * yv * (sg * (1.0 + z * (1.0 - sg)))).astype(dz_ref.dtype)

        @pl.when(pl.program_id(0) == 0)
        def _():
            dg_ref[...] = jnp.zeros_like(dg_ref)

        dg_ref[...] += jnp.sum(dov * t * r, axis=0, keepdims=True)

    blk = pl.BlockSpec((tm, w), lambda i: (i, 0))
    vec = pl.BlockSpec((1, w), lambda i: (0, 0))
    dy, dz, dg = pl.pallas_call(
        body, grid=(s // tm,),
        in_specs=[blk, pl.BlockSpec((tm, w), lambda i: (i, P_Z // w)), vec, blk],
        out_specs=(blk, blk, vec), out_shape=(SDS((s, w), F32), SDS((s, w), BF16), SDS((1, w), F32)), name=name,
        compiler_params=_params(("arbitrary",), 24 * tm * w * 4))(y, proj, g.reshape(1, w), dout)
    return dy, dz, dg.reshape(w)


def _final_loss(x, g, target, *, name):
    s, w = x.shape
    tm = min(s, 512)

    def body(x_ref, g_ref, t_ref, loss_ref, dx_ref, dxb_ref, dg_ref):
        xv = x_ref[...]
        gv = g_ref[...]
        r = lax.rsqrt(jnp.mean(xv * xv, axis=-1, keepdims=True) + EPS)
        xn = xv * r
        diff = xn * gv - t_ref[...]
        dy = diff * (1.0 / w)
        u = dy * gv
        dx = r * u - xv * (r * r * r) * jnp.mean(xv * u, axis=-1, keepdims=True)
        dx_ref[...] = dx
        dxb_ref[...] = dx.astype(BF16)

        @pl.when(pl.program_id(0) == 0)
        def _():
            dg_ref[...] = jnp.zeros_like(dg_ref)
            loss_ref[...] = jnp.zeros_like(loss_ref)

        dg_ref[...] += jnp.sum(dy * xn, axis=0, keepdims=True)
        part = jnp.sum(jnp.sum(diff * diff, axis=1, keepdims=True), axis=0, keepdims=True) * (0.5 / w)
        loss_ref[...] += jnp.broadcast_to(part, loss_ref.shape)

    blk = pl.BlockSpec((tm, w), lambda i: (i, 0))
    vec = pl.BlockSpec((1, w), lambda i: (0, 0))
    loss, dx, dxb, dg = pl.pallas_call(
        body, grid=(s // tm,), in_specs=[blk, vec, blk],
        out_specs=(pl.BlockSpec((1, LANES), lambda i: (0, 0)), blk, blk, vec),
        out_shape=(SDS((1, LANES), F32), SDS((s, w), F32), SDS((s, w), BF16), SDS((1, w), F32)), name=name,
        compiler_params=_params(("arbitrary",), 18 * tm * w * 4))(x, g.reshape(1, w), target)
    return loss[0, 0], dx, dxb, dg.reshape(w)


def _shift_down(x, k):
    if k == 0:
        return x
    row = lax.broadcasted_iota(jnp.int32, x.shape, 0)
    return jnp.where(row < k, 0.0, pltpu.roll(x, k, axis=0))


def _shift_up(x, k):
    if k == 0:
        return x
    s = x.shape[0]
    row = lax.broadcasted_iota(jnp.int32, x.shape, 0)
    return jnp.where(row >= s - k, 0.0, pltpu.roll(x, s - k, axis=0))


def _conv_pre(x, w, b, kw):
    pre = b
    for j in range(kw):
        pre = pre + w[j:j + 1, :] * _shift_down(x, kw - 1 - j)
    return pre


def _conv_bwd_terms(x, w, dpre, kw):
    dx = jnp.zeros_like(x)
    dws = []
    for j in range(kw):
        dx = dx + w[j:j + 1, :] * _shift_up(dpre, kw - 1 - j)
        dws.append(jnp.sum(dpre * _shift_down(x, kw - 1 - j), axis=0, keepdims=True))
    return dx, jnp.concatenate(dws, axis=0), jnp.sum(dpre, axis=0, keepdims=True)


def _ssm_conv_fwd(proj, w, b, *, name):
    s = proj.shape[0]
    cw = 256

    def body(x_ref, w_ref, b_ref, o_ref):
        pre = _conv_pre(x_ref[...], w_ref[...], b_ref[...], SSM_CONV)
        o_ref[...] = pre * _sigmoid(pre)

    return pl.pallas_call(
        body, grid=(CONV_CH // cw,),
        in_specs=[pl.BlockSpec((s, cw), lambda j: (0, j)), pl.BlockSpec((SSM_CONV, cw), lambda j: (0, j)),
                  pl.BlockSpec((1, cw), lambda j: (0, j))],
        out_specs=pl.BlockSpec((s, cw), lambda j: (0, j)), out_shape=SDS((s, CONV_CH), F32), name=name,
        compiler_params=_params(("parallel",), 12 * s * cw * 4))(proj, w, b.reshape(1, CONV_CH))


def _ssm_conv_bwd(proj, w, b, dxbc, *, name):
    s = proj.shape[0]
    cw = 256

    def body(x_ref, w_ref, b_ref, dy_ref, dx_ref, dw_ref, db_ref):
        x = x_ref[...]
        wv = w_ref[...]
        pre = _conv_pre(x, wv, b_ref[...], SSM_CONV)
        sg = _sigmoid(pre)
        dpre = dy_ref[...] * (sg * (1.0 + pre * (1.0 - sg)))
        dx, dw, db = _conv_bwd_terms(x, wv, dpre, SSM_CONV)
        dx_ref[...] = dx.astype(dx_ref.dtype)
        dw_ref[...] = dw
        db_ref[...] = db

    col = pl.BlockSpec((s, cw), lambda j: (0, j))
    wsp = pl.BlockSpec((SSM_CONV, cw), lambda j: (0, j))
    bsp = pl.BlockSpec((1, cw), lambda j: (0, j))
    dx, dw, db = pl.pallas_call(
        body, grid=(CONV_CH // cw,), in_specs=[col, wsp, bsp, col], out_specs=(col, wsp, bsp),
        out_shape=(SDS((s, CONV_CH), BF16), SDS((SSM_CONV, CONV_CH), F32), SDS((1, CONV_CH), F32)), name=name,
        compiler_params=_params(("parallel",), 20 * s * cw * 4))(proj, w, b.reshape(1, CONV_CH), dxbc)
    return dx, dw, db.reshape(CONV_CH)


def _ffn_conv_fwd(up_g, up_v, w, b, *, name):
    s = up_g.shape[0]
    cw = 256
    nb = D_FF // cw

    def body(g_ref, v_ref, wg_ref, wv_ref, bg_ref, bv_ref, o_ref):
        gate = _conv_pre(g_ref[...], wg_ref[...], bg_ref[...], FFN_CONV)
        val = _conv_pre(v_ref[...], wv_ref[...], bv_ref[...], FFN_CONV)
        o_ref[...] = (gate * _sigmoid(gate) * val).astype(o_ref.dtype)

    col = pl.BlockSpec((s, cw), lambda j: (0, j))
    b2 = b.reshape(1, 2 * D_FF)
    return pl.pallas_call(
        body, grid=(nb,),
        in_specs=[col, col, pl.BlockSpec((FFN_CONV, cw), lambda j: (0, j)), pl.BlockSpec((FFN_CONV, cw), lambda j: (0, j + nb)),
                  pl.BlockSpec((1, cw), lambda j: (0, j)), pl.BlockSpec((1, cw), lambda j: (0, j + nb))],
        out_specs=col, out_shape=SDS((s, D_FF), BF16), name=name,
        compiler_params=_params(("parallel",), 16 * s * cw * 4))(up_g, up_v, w, w, b2, b2)


def _ffn_conv_bwd(up_g, up_v, w, b, dact, *, name):
    s = up_g.shape[0]
    cw = 256
    nb = D_FF // cw

    def body(g_ref, v_ref, wg_ref, wv_ref, bg_ref, bv_ref, da_ref, dg_ref, dv_ref, dwg_ref, dwv_ref, dbg_ref, dbv_ref):
        xg, xv = g_ref[...], v_ref[...]
        wg, wv = wg_ref[...], wv_ref[...]
        gate = _conv_pre(xg, wg, bg_ref[...], FFN_CONV)
        val = _conv_pre(xv, wv, bv_ref[...], FFN_CONV)
        da = da_ref[...].astype(F32)
        sg = _sigmoid(gate)
        dgate = da * val * (sg * (1.0 + gate * (1.0 - sg)))
        dval = da * gate * sg
        dxg, dwg, dbg = _conv_bwd_terms(xg, wg, dgate, FFN_CONV)
        dxv, dwv, dbv = _conv_bwd_terms(xv, wv, dval, FFN_CONV)
        dg_ref[...] = dxg.astype(dg_ref.dtype)
        dv_ref[...] = dxv.astype(dv_ref.dtype)
        dwg_ref[...] = dwg
        dwv_ref[...] = dwv
        dbg_ref[...] = dbg
        dbv_ref[...] = dbv

    col = pl.BlockSpec((s, cw), lambda j: (0, j))
    wsp = pl.BlockSpec((FFN_CONV, cw), lambda j: (0, j))
    bsp = pl.BlockSpec((1, cw), lambda j: (0, j))
    b2 = b.reshape(1, 2 * D_FF)
    dg, dv, dwg, dwv, dbg, dbv = pl.pallas_call(
        body, grid=(nb,),
        in_specs=[col, col, wsp, pl.BlockSpec((FFN_CONV, cw), lambda j: (0, j + nb)), bsp,
                  pl.BlockSpec((1, cw), lambda j: (0, j + nb)), col],
        out_specs=(col, col, wsp, wsp, bsp, bsp),
        out_shape=(SDS((s, D_FF), BF16), SDS((s, D_FF), BF16), SDS((FFN_CONV, D_FF), F32), SDS((FFN_CONV, D_FF), F32),
                   SDS((1, D_FF), F32), SDS((1, D_FF), F32)), name=name,
        compiler_params=_params(("parallel",), 32 * s * cw * 4))(up_g, up_v, w, w, b2, b2, dact)
    return dg, dv, jnp.concatenate([dwg, dwv], axis=1), jnp.concatenate([dbg, dbv], axis=1).reshape(2 * D_FF)


def _dot(a, b):
    return jnp.dot(a.astype(BF16), b.astype(BF16), preferred_element_type=F32)


def _dot_nt(a, b):
    return lax.dot_general(a.astype(BF16), b.astype(BF16), (((1,), (1,)), ((), ())), preferred_element_type=F32)


def _dot_tn(a, b):
    return lax.dot_general(a.astype(BF16), b.astype(BF16), (((0,), (0,)), ((), ())), preferred_element_type=F32)


def _ssd_chunk_terms(dtraw, bias, a_log):
    ell = dtraw.shape[0]
    lane = lax.broadcasted_iota(jnp.int32, dtraw.shape, 1)
    valid = lane < SSM_HEADS
    pre = dtraw + bias
    dt = jnp.where(valid, jnp.where(pre > 20.0, pre, jnp.log(1.0 + jnp.exp(jnp.minimum(pre, 20.0)))), 0.0)
    a = -jnp.exp(a_log)
    ad = dt * a
    row = lax.broadcasted_iota(jnp.int32, (ell, ell), 0)
    colm = lax.broadcasted_iota(jnp.int32, (ell, ell), 1)
    tril = row >= colm
    cs = jnp.dot(tril.astype(F32), ad, precision=HIGHEST, preferred_element_type=F32)
    cs_last = cs[ell - 1:ell, :]
    return pre, dt, a, cs, cs_last, tril


def _head_expand():
    h = lax.broadcasted_iota(jnp.int32, (LANES, D_SSM), 0)
    c = lax.broadcasted_iota(jnp.int32, (LANES, D_SSM), 1)
    return (c // SSM_HEAD_DIM == h).astype(F32)


def _ssd_fwd(xbc, proj, dt_bias, a_log, d_skip, *, name):
    s = xbc.shape[0]
    nc = s // SSM_CHUNK
    ell, n, p = SSM_CHUNK, SSM_STATE, SSM_HEAD_DIM
    rpg = SSM_HEADS // SSM_GROUPS
    gw = rpg * p

    def body(x_ref, dt_ref, bias_ref, alog_ref, dskip_ref, ex_ref, y_ref, ps_ref, state):
        @pl.when(pl.program_id(0) == 0)
        def _():
            state[...] = jnp.zeros_like(state)

        _, dt, _, cs, cs_last, tril = _ssd_chunk_terms(dt_ref[...], bias_ref[...], alog_ref[...])
        cst = cs.T
        ex = ex_ref[...]
        spread = lambda v: jnp.dot(v, ex, precision=HIGHEST, preferred_element_type=F32)
        dt_x, e_x, ds_x = spread(dt), spread(jnp.exp(cs)), spread(jnp.exp(cs_last - cs))
        cd_x = spread(jnp.broadcast_to(jnp.exp(cs_last), (8, LANES)))[0:1, :]
        dskip_x = spread(jnp.broadcast_to(dskip_ref[...], (8, LANES)))[0:1, :]
        st = state[...]
        ps_ref[0] = st
        xv = x_ref[...]
        xs_all = xv[:, 0:D_SSM]
        xd_all = xs_all * dt_x
        xdd_all = xd_all * ds_x
        lane_g = lax.broadcasted_iota(jnp.int32, (ell, gw), 1)
        ys, new = [], []
        for g in range(SSM_GROUPS):
            gs = slice(gw * g, gw * (g + 1))
            bg = xv[:, D_SSM + n * g:D_SSM + n * (g + 1)]
            cg = xv[:, D_SSM + n * (SSM_GROUPS + g):D_SSM + n * (SSM_GROUPS + g + 1)]
            cb = _dot_nt(cg, bg)
            xd_g, prev_g = xd_all[:, gs], st[:, gs]
            y_g = _dot(cg, prev_g) * e_x[:, gs] + xs_all[:, gs] * dskip_x[:, gs]
            for r in range(rpg):
                h = g * rpg + r
                lmat = jnp.exp(jnp.where(tril, cs[:, h:h + 1] - cst[h:h + 1, :], -jnp.inf))
                y_g = y_g + jnp.where((lane_g >= p * r) & (lane_g < p * (r + 1)), _dot(cb * lmat, xd_g), 0.0)
            ys.append(y_g)
            new.append(prev_g * cd_x[:, gs] + _dot(bg.T, xdd_all[:, gs]))
        y_ref[...] = jnp.concatenate(ys, axis=1)
        state[...] = jnp.concatenate(new, axis=1)

    vec = pl.BlockSpec((1, LANES), lambda c: (0, 0))
    return pl.pallas_call(
        body, grid=(nc,),
        in_specs=[pl.BlockSpec((ell, CONV_CH), lambda c: (c, 0)), pl.BlockSpec((ell, LANES), lambda c: (c, P_DT // LANES)),
                  vec, vec, vec, pl.BlockSpec((LANES, D_SSM), lambda c: (0, 0))],
        out_specs=(pl.BlockSpec((ell, D_SSM), lambda c: (c, 0)), pl.BlockSpec((1, n, D_SSM), lambda c: (c, 0, 0))),
        out_shape=(SDS((s, D_SSM), F32), SDS((nc, n, D_SSM), F32)),
        scratch_shapes=[pltpu.VMEM((n, D_SSM), F32)], name=name,
        compiler_params=_params(("arbitrary",), 32 << 20))(xbc, proj, dt_bias, a_log, d_skip, _head_expand())


def _ssd_bwd(xbc, proj, dt_bias, a_log, d_skip, prev_states, dy, *, name):
    s = xbc.shape[0]
    nc = s // SSM_CHUNK
    ell, n, p = SSM_CHUNK, SSM_STATE, SSM_HEAD_DIM
    rpg = SSM_HEADS // SSM_GROUPS
    gw = rpg * p

    def body(x_ref, dt_ref, bias_ref, alog_ref, dskip_ref, ps_ref, dy_ref, ex_ref, ext_ref,
             dx_ref, ddt_ref, dalog_ref, ddskip_ref, dbias_ref, dstate):
        @pl.when(pl.program_id(0) == 0)
        def _():
            dstate[...] = jnp.zeros_like(dstate)
            dalog_ref[...] = jnp.zeros_like(dalog_ref)
            ddskip_ref[...] = jnp.zeros_like(ddskip_ref)
            dbias_ref[...] = jnp.zeros_like(dbias_ref)

        pre, dt, a, cs, cs_last, tril = _ssd_chunk_terms(dt_ref[...], bias_ref[...], alog_ref[...])
        e = jnp.exp(cs)
        ds = jnp.exp(cs_last - cs)
        cd = jnp.exp(cs_last)
        cst = cs.T
        shape = (ell, LANES)
        ex, ext = ex_ref[...], ext_ref[...]
        spread = lambda v: jnp.dot(v, ex, precision=HIGHEST, preferred_element_type=F32)
        gather = lambda v: jnp.dot(v, ext, precision=HIGHEST, preferred_element_type=F32)
        dt_x, e_x, ds_x = spread(dt), spread(e), spread(ds)
        cd_x = spread(jnp.broadcast_to(cd, (8, LANES)))[0:1, :]
        dskip_x = spread(jnp.broadcast_to(dskip_ref[...], (8, LANES)))[0:1, :]
        xv, dyv, psv, dst = x_ref[...], dy_ref[...], ps_ref[0], dstate[...]
        xs_all = xv[:, 0:D_SSM]
        xd_all = xs_all * dt_x
        dye_all = dyv * e_x
        xdd_all = xd_all * ds_x
        triu = lax.broadcasted_iota(jnp.int32, (ell, ell), 0) <= lax.broadcasted_iota(jnp.int32, (ell, ell), 1)
        lane_g = lax.broadcasted_iota(jnp.int32, (ell, gw), 1)
        lane = lax.broadcasted_iota(jnp.int32, shape, 1)
        sub = lax.broadcasted_iota(jnp.int32, shape, 0)
        dcs_acc = jnp.zeros(shape, F32)
        dcs_rows = jnp.zeros(shape, F32)
        dxs, dbs, dcs_parts, dprevs, prod_a, prod_b, prod_c, prod_e = [], [], [], [], [], [], [], []
        for g in range(SSM_GROUPS):
            gs = slice(gw * g, gw * (g + 1))
            bg = xv[:, D_SSM + n * g:D_SSM + n * (g + 1)]
            cg = xv[:, D_SSM + n * (SSM_GROUPS + g):D_SSM + n * (SSM_GROUPS + g + 1)]
            cb = _dot_nt(cg, bg)
            cbt = _dot_nt(bg, cg)
            xs_g, dy_g, xd_g, dye_g, xdd_g = xs_all[:, gs], dyv[:, gs], xd_all[:, gs], dye_all[:, gs], xdd_all[:, gs]
            prev_g, dsn_g = psv[:, gs], dst[:, gs]
            cprev_g = _dot(cg, prev_g)
            dprevs.append(dsn_g * cd_x[:, gs] + _dot(cg.T, dye_g))
            dcg = _dot_nt(dye_g, prev_g)
            dxdd_g = _dot(bg, dsn_g)
            dbg = _dot_nt(xdd_g, dsn_g)
            dxd_g = dxdd_g * ds_x[:, gs]
            prod_a.append(dy_g * cprev_g)
            prod_b.append(dxdd_g * xd_g)
            prod_e.append(jnp.sum(dsn_g * prev_g, axis=0, keepdims=True))
            dcb = jnp.zeros((ell, ell), F32)
            for r in range(rpg):
                h = g * rpg + r
                mine = (lane_g >= p * r) & (lane_g < p * (r + 1))
                lmat = jnp.exp(jnp.where(tril, cs[:, h:h + 1] - cst[h:h + 1, :], -jnp.inf))
                lmat_t = jnp.exp(jnp.where(triu, cst[h:h + 1, :] - cs[:, h:h + 1], -jnp.inf))
                dgm = _dot_nt(jnp.where(mine, dy_g, 0.0), xd_g)
                dxd_g = dxd_g + jnp.where(mine, _dot(cbt * lmat_t, dy_g), 0.0)
                mm = dgm * (cb * lmat)
                dcs_acc = dcs_acc + jnp.where(lane == h, jnp.sum(mm, axis=1, keepdims=True), 0.0)
                dcs_rows = dcs_rows + jnp.where(sub == h, jnp.sum(mm, axis=0, keepdims=True), 0.0)
                dcb = dcb + dgm * lmat
            dxs.append(dxd_g * dt_x[:, gs] + dy_g * dskip_x[:, gs])
            prod_c.append(dxd_g * xs_g)
            dbs.append(dbg + _dot_tn(dcb, cg))
            dcs_parts.append(dcg + _dot(dcb, bg))
        dx_ref[...] = jnp.concatenate(dxs + dbs + dcs_parts, axis=1)
        dstate[...] = jnp.concatenate(dprevs, axis=1)
        sum_a = gather(jnp.concatenate(prod_a, axis=1))
        sum_b = gather(jnp.concatenate(prod_b, axis=1))
        sum_c = gather(jnp.concatenate(prod_c, axis=1))
        sum_d = gather(dyv * xs_all)
        dcd = gather(jnp.broadcast_to(jnp.concatenate(prod_e, axis=1), (8, D_SSM)))[0:1, :]
        tmp = sum_b * ds
        dlast = dcd * cd + jnp.sum(tmp, axis=0, keepdims=True)
        dcs = dcs_acc + sum_a * e - tmp - dcs_rows.T + jnp.where(sub == ell - 1, dlast, 0.0)
        dad = jnp.dot(triu.astype(F32), dcs, precision=HIGHEST, preferred_element_type=F32)
        ddt = sum_c + dad * a
        dalog_ref[...] += jnp.sum(dad * dt, axis=0, keepdims=True) * a
        ddskip_ref[...] += jnp.sum(sum_d, axis=0, keepdims=True)
        ddraw = jnp.where(lane < SSM_HEADS, ddt * _sigmoid(pre), 0.0)
        ddt_ref[...] = ddraw.astype(ddt_ref.dtype)
        dbias_ref[...] += jnp.sum(ddraw, axis=0, keepdims=True)

    vec = pl.BlockSpec((1, LANES), lambda c: (0, 0))
    rev = lambda c: nc - 1 - c
    ex = _head_expand()
    outs = pl.pallas_call(
        body, grid=(nc,),
        in_specs=[pl.BlockSpec((ell, CONV_CH), lambda c: (rev(c), 0)),
                  pl.BlockSpec((ell, LANES), lambda c: (rev(c), P_DT // LANES)), vec, vec, vec,
                  pl.BlockSpec((1, n, D_SSM), lambda c: (rev(c), 0, 0)),
                  pl.BlockSpec((ell, D_SSM), lambda c: (rev(c), 0)),
                  pl.BlockSpec((LANES, D_SSM), lambda c: (0, 0)), pl.BlockSpec((D_SSM, LANES), lambda c: (0, 0))],
        out_specs=(pl.BlockSpec((ell, CONV_CH), lambda c: (rev(c), 0)), pl.BlockSpec((ell, LANES), lambda c: (rev(c), 0)),
                   vec, vec, vec),
        out_shape=(SDS((s, CONV_CH), F32), SDS((s, LANES), BF16), SDS((1, LANES), F32), SDS((1, LANES), F32),
                   SDS((1, LANES), F32)),
        scratch_shapes=[pltpu.VMEM((n, D_SSM), F32)], name=name,
        compiler_params=_params(("arbitrary",), 40 << 20))(xbc, proj, dt_bias, a_log, d_skip, prev_states, dy, ex, ex.T)
    return outs


def _rope_swap(t):
    lane = lax.broadcasted_iota(jnp.int32, t.shape, 1)
    half = QK_ROPE // 2
    lo = (lane >= QK_NOPE) & (lane < QK_NOPE + half)
    hi = (lane >= QK_NOPE + half) & (lane < QK_NOPE + QK_ROPE)
    return jnp.where(lo, pltpu.roll(t, HEAD_PAD - half, axis=1), jnp.where(hi, pltpu.roll(t, half, axis=1), 0.0))


def _mla_prep(q, kv, proj, cos, sins, *, name):
    s = q.shape[0]
    tm = min(s, 256)
    scale = (QK_NOPE + QK_ROPE) ** -0.5

    def body(q_ref, kv_ref, kr_ref, cos_ref, sin_ref, qo_ref, ko_ref, vo_ref):
        cosv, sinv = cos_ref[...], sin_ref[...]
        kr = pltpu.roll(kr_ref[...], QK_NOPE, axis=1)
        lane = lax.broadcasted_iota(jnp.int32, kr.shape, 1)
        nope = lane < QK_NOPE
        kr = jnp.where(nope, 0.0, kr)
        kpe = kr * cosv + _rope_swap(kr) * sinv
        for hp in range(MLA_HEADS // 2):
            vs = []
            for h in (2 * hp, 2 * hp + 1):
                hs = slice(HEAD_PAD * h, HEAD_PAD * (h + 1))
                qh = q_ref[:, hs]
                kvh = kv_ref[:, hs]
                qo_ref[:, hs] = ((qh * cosv + _rope_swap(qh) * sinv) * scale).astype(qo_ref.dtype)
                ko_ref[:, hs] = (jnp.where(nope, kvh, 0.0) + kpe).astype(ko_ref.dtype)
                vs.append(kvh[:, QK_NOPE:])
            vo_ref[:, 2 * V_DIM * hp:2 * V_DIM * (hp + 1)] = jnp.concatenate(vs, axis=1).astype(vo_ref.dtype)

    wide = pl.BlockSpec((tm, MLA_HEADS * HEAD_PAD), lambda i: (i, 0))
    half = pl.BlockSpec((tm, MLA_HEADS * V_DIM), lambda i: (i, 0))
    tab = pl.BlockSpec((tm, LANES), lambda i: (i, 0))
    return pl.pallas_call(
        body, grid=(s // tm,),
        in_specs=[wide, wide, pl.BlockSpec((tm, LANES), lambda i: (i, P_KR // LANES)), tab, tab],
        out_specs=(wide, wide, half),
        out_shape=(SDS((s, MLA_HEADS * HEAD_PAD), BF16), SDS((s, MLA_HEADS * HEAD_PAD), BF16),
                   SDS((s, MLA_HEADS * V_DIM), BF16)), name=name,
        compiler_params=_params(("parallel",), 32 << 20))(q, kv, proj, cos, sins)


def _mla_prep_bwd(dqr, dkr, dv, cos, sins, *, name):
    s = dqr.shape[0]
    tm = min(s, 256)
    scale = (QK_NOPE + QK_ROPE) ** -0.5

    def body(dq_ref, dk_ref, dv_ref, cos_ref, sin_ref, dqo_ref, dkv_ref, dkr_ref):
        cosv, sinv = cos_ref[...], sin_ref[...]
        lane = lax.broadcasted_iota(jnp.int32, cosv.shape, 1)
        ksum = jnp.zeros(cosv.shape, F32)
        for h in range(MLA_HEADS):
            hs = slice(HEAD_PAD * h, HEAD_PAD * (h + 1))
            d = dq_ref[:, hs]
            dk = dk_ref[:, hs]
            dqo_ref[:, hs] = ((d * cosv + _rope_swap(d * sinv)) * scale).astype(dqo_ref.dtype)
            dkv_ref[:, hs] = jnp.concatenate([dk[:, :QK_NOPE], dv_ref[:, V_DIM * h:V_DIM * (h + 1)]], axis=1).astype(dkv_ref.dtype)
            ksum = ksum + dk
        ksum = jnp.where((lane >= QK_NOPE) & (lane < QK_NOPE + QK_ROPE), ksum, 0.0)
        un = ksum * cosv + _rope_swap(ksum * sinv)
        dkr_ref[...] = pltpu.roll(un, HEAD_PAD - QK_NOPE, axis=1).astype(dkr_ref.dtype)

    wide = pl.BlockSpec((tm, MLA_HEADS * HEAD_PAD), lambda i: (i, 0))
    half = pl.BlockSpec((tm, MLA_HEADS * V_DIM), lambda i: (i, 0))
    tab = pl.BlockSpec((tm, LANES), lambda i: (i, 0))
    return pl.pallas_call(
        body, grid=(s // tm,), in_specs=[wide, wide, half, tab, tab], out_specs=(wide, wide, tab),
        out_shape=(SDS((s, MLA_HEADS * HEAD_PAD), BF16), SDS((s, MLA_HEADS * HEAD_PAD), BF16), SDS((s, LANES), BF16)),
        name=name, compiler_params=_params(("parallel",), 40 << 20))(dqr, dkr, dv, cos, sins)


FLASH_TILE = 512
FLASH_ROWS = 32


def _flash_fwd(q, k, v, *, name):
    s = q.shape[0]
    t = min(s, FLASH_TILE)
    nq = s // t
    npair = MLA_HEADS // 2

    def body(q_ref, k_ref, v_ref, o_ref, lse_ref):
        i = pl.program_id(1)
        qs = [q_ref[:, HEAD_PAD * e:HEAD_PAD * (e + 1)] for e in range(2)]
        diag = lax.broadcasted_iota(jnp.int32, (t, t), 0) >= lax.broadcasted_iota(jnp.int32, (t, t), 1)

        def step(j, carry, masked):
            rows = pl.ds(pl.multiple_of(j * t, t), t)
            new = []
            for e in range(2):
                m, l, acc = carry[e]
                sc = _dot_nt(qs[e], k_ref[rows, HEAD_PAD * e:HEAD_PAD * (e + 1)])
                if masked:
                    sc = jnp.where(diag, sc, NEG)
                m_new = jnp.maximum(m, jnp.max(sc, axis=1, keepdims=True))
                pr = jnp.exp(sc - m_new)
                alpha = jnp.exp(m - m_new)
                l = alpha * l + jnp.sum(pr, axis=1, keepdims=True)
                acc = alpha * acc + _dot(pr, v_ref[rows, V_DIM * e:V_DIM * (e + 1)])
                new.append((m_new, l, acc))
            return tuple(new)

        init = tuple((jnp.full((t, 1), NEG, F32), jnp.zeros((t, 1), F32), jnp.zeros((t, V_DIM), F32)) for _ in range(2))
        carry = lax.fori_loop(0, i, functools.partial(step, masked=False), init)
        carry = step(i, carry, True)
        o_ref[...] = jnp.concatenate([acc / l for _, l, acc in carry], axis=1)
        lse_ref[0] = jnp.concatenate([jnp.broadcast_to(m + jnp.log(l), (t, V_DIM)) for m, l, _ in carry], axis=1)

    return pl.pallas_call(
        body, grid=(npair, nq),
        in_specs=[pl.BlockSpec((t, 2 * HEAD_PAD), lambda hp, i: (i, hp)), pl.BlockSpec((s, 2 * HEAD_PAD), lambda hp, i: (0, hp)),
                  pl.BlockSpec((s, 2 * V_DIM), lambda hp, i: (0, hp))],
        out_specs=(pl.BlockSpec((t, 2 * V_DIM), lambda hp, i: (i, hp)), pl.BlockSpec((1, t, LANES), lambda hp, i: (hp, i, 0))),
        out_shape=(SDS((s, MLA_HEADS * V_DIM), F32), SDS((npair, s, LANES), F32)), name=name,
        compiler_params=_params(("parallel", "parallel"), 40 << 20))(q, k, v)


def _flash_bwd(q, k, v, o, lse, do, *, name):
    s = q.shape[0]
    t = min(s, FLASH_TILE)
    nq = s // t
    npair = MLA_HEADS // 2
    nchunk = t // FLASH_ROWS

    def valid_cols(r):
        return min(t, -(-((r + 1) * FLASH_ROWS) // LANES) * LANES)

    def body(q_ref, k_ref, v_ref, o_ref, lse_ref, do_ref, dq_ref, dk_ref, dv_ref, s_scr, dp_scr, p_scr, ds_scr, dk_acc, dv_acc):
        j = pl.program_id(1)

        @pl.when(j == 0)
        def _():
            dq_ref[...] = jnp.zeros_like(dq_ref)

        dk_acc[...] = jnp.zeros(dk_acc.shape, F32)
        dv_acc[...] = jnp.zeros(dv_acc.shape, F32)
        qsl = [slice(HEAD_PAD * e, HEAD_PAD * (e + 1)) for e in range(2)]
        vsl = [slice(V_DIM * e, V_DIM * (e + 1)) for e in range(2)]

        def step(i, carry, masked):
            rows = pl.ds(pl.multiple_of(i * t, t), t)
            for e in range(2):
                ke = k_ref[:, qsl[e]]
                qi = q_ref[rows, qsl[e]]
                doi = do_ref[rows, vsl[e]]
                delta = jnp.sum(doi * o_ref[rows, vsl[e]], axis=1, keepdims=True)
                lse_i = lse_ref[0, rows, vsl[e]][:, 0:1]
                dob = doi.astype(BF16)
                s_scr[e] = _dot_nt(qi, ke)
                dp_scr[e] = _dot_nt(dob, v_ref[:, vsl[e]])
                for r in range(nchunk):
                    rs = slice(r * FLASH_ROWS, (r + 1) * FLASH_ROWS)
                    width = valid_cols(r) if masked else t
                    sc = s_scr[e, rs, 0:width]
                    if masked:
                        row = r * FLASH_ROWS + lax.broadcasted_iota(jnp.int32, (FLASH_ROWS, width), 0)
                        sc = jnp.where(row >= lax.broadcasted_iota(jnp.int32, (FLASH_ROWS, width), 1), sc, NEG)
                    pr = jnp.exp(sc - lse_i[rs, :])
                    dsc = pr * (dp_scr[e, rs, 0:width] - delta[rs, :])
                    p_scr[e, rs, 0:width] = pr.astype(BF16)
                    ds_scr[e, rs, 0:width] = dsc.astype(BF16)
                    if width < t:
                        p_scr[e, rs, width:t] = jnp.zeros((FLASH_ROWS, t - width), BF16)
                        ds_scr[e, rs, width:t] = jnp.zeros((FLASH_ROWS, t - width), BF16)
                dv_acc[e] += _dot_tn(p_scr[e], dob)
                dk_acc[e] += _dot_tn(ds_scr[e], qi)
                dq_ref[rows, qsl[e]] += _dot(ds_scr[e], ke)
            return carry

        step(j, 0, True)
        lax.fori_loop(j + 1, nq, functools.partial(step, masked=False), 0)
        dk_ref[...] = jnp.concatenate([dk_acc[e] for e in range(2)], axis=1)
        dv_ref[...] = jnp.concatenate([dv_acc[e] for e in range(2)], axis=1)

    full_q = pl.BlockSpec((s, 2 * HEAD_PAD), lambda hp, j: (0, hp))
    full_v = pl.BlockSpec((s, 2 * V_DIM), lambda hp, j: (0, hp))
    blk_k = pl.BlockSpec((t, 2 * HEAD_PAD), lambda hp, j: (j, hp))
    blk_v = pl.BlockSpec((t, 2 * V_DIM), lambda hp, j: (j, hp))
    return pl.pallas_call(
        body, grid=(npair, nq),
        in_specs=[full_q, blk_k, blk_v, full_v, pl.BlockSpec((1, s, LANES), lambda hp, j: (hp, 0, 0)), full_v],
        out_specs=(full_q, blk_k, blk_v),
        out_shape=(SDS((s, MLA_HEADS * HEAD_PAD), F32), SDS((s, MLA_HEADS * HEAD_PAD), F32), SDS((s, MLA_HEADS * V_DIM), F32)),
        scratch_shapes=[pltpu.VMEM((2, t, t), F32), pltpu.VMEM((2, t, t), F32), pltpu.VMEM((2, t, t), BF16),
                        pltpu.VMEM((2, t, t), BF16), pltpu.VMEM((2, t, HEAD_PAD), F32), pltpu.VMEM((2, t, V_DIM), F32)],
        name=name, compiler_params=_params(("parallel", "arbitrary"), 48 << 20))(q, k, v, o, lse, do)


def _mem_attn_fwd(q, k, v, *, name):
    s = q.shape[0]
    tm = min(s, 512)
    ml = k.shape[0]
    scale = MEM_HEAD_DIM ** -0.5

    def body(q_ref, k_ref, v_ref, o_ref):
        for h in range(MEM_HEADS):
            hs = slice(MEM_HEAD_DIM * h, MEM_HEAD_DIM * (h + 1))
            sc = _dot_nt(q_ref[:, hs], k_ref[:, hs]) * scale
            pr = jnp.exp(sc - jnp.max(sc, axis=1, keepdims=True))
            pr = pr / jnp.sum(pr, axis=1, keepdims=True)
            o_ref[:, hs] = _dot(pr, v_ref[:, hs]).astype(o_ref.dtype)

    blk = pl.BlockSpec((tm, D_MODEL), lambda i: (i, 0))
    kv = pl.BlockSpec((ml, D_MODEL), lambda i: (0, 0))
    return pl.pallas_call(body, grid=(s // tm,), in_specs=[blk, kv, kv], out_specs=blk,
                          out_shape=SDS((s, D_MODEL), BF16), name=name,
                          compiler_params=_params(("parallel",), 24 << 20))(q, k, v)


def _mem_attn_bwd(q, k, v, do, *, name):
    s = q.shape[0]
    tm = min(s, 512)
    ml = k.shape[0]
    scale = MEM_HEAD_DIM ** -0.5

    def body(q_ref, k_ref, v_ref, do_ref, dq_ref, dk_ref, dv_ref):
        @pl.when(pl.program_id(0) == 0)
        def _():
            dk_ref[...] = jnp.zeros_like(dk_ref)
            dv_ref[...] = jnp.zeros_like(dv_ref)

        for h in range(MEM_HEADS):
            hs = slice(MEM_HEAD_DIM * h, MEM_HEAD_DIM * (h + 1))
            qh, kh, vh, doh = q_ref[:, hs], k_ref[:, hs], v_ref[:, hs], do_ref[:, hs]
            sc = _dot_nt(qh, kh) * scale
            pr = jnp.exp(sc - jnp.max(sc, axis=1, keepdims=True))
            pr = pr / jnp.sum(pr, axis=1, keepdims=True)
            dp = _dot_nt(doh, vh)
            dsc = pr * (dp - jnp.sum(pr * dp, axis=1, keepdims=True)) * scale
            dq_ref[:, hs] = _dot(dsc, kh).astype(dq_ref.dtype)
            dk_ref[:, hs] += _dot_tn(dsc, qh)
            dv_ref[:, hs] += _dot_tn(pr, doh)

    blk = pl.BlockSpec((tm, D_MODEL), lambda i: (i, 0))
    kv = pl.BlockSpec((ml, D_MODEL), lambda i: (0, 0))
    return pl.pallas_call(body, grid=(s // tm,), in_specs=[blk, kv, kv, blk], out_specs=(blk, kv, kv),
                          out_shape=(SDS((s, D_MODEL), BF16), SDS((ml, D_MODEL), F32), SDS((ml, D_MODEL), F32)), name=name,
                          compiler_params=_params(("arbitrary",), 32 << 20))(q, k, v, do)


MATS = (("w_in", (1024, 940), 1), ("w_uq", (384, 384), 1), ("w_ukv", (256, 512), 1), ("w_out", (512, 1024), 0),
        ("ssm_conv_w", (4, 512), 1),
        ("w_mq", (256, 1024), 0), ("w_mk", (256, 1024), 0), ("w_mv", (256, 1024), 0), ("w_mo", (256, 1024), 0),
        ("w_up", (1024, 1408), 1), ("w_down", (704, 1024), 0), ("ffn_conv_w", (3, 1408), 1))
GROUPS = {"proj": (0,), "mixer": (1, 2, 3, 4), "mem": (5, 6, 7, 8), "ffn": (9, 10, 11)}
UP_SHARD_COLS = 1408
F32_ON_WIRE = ("ssm_conv_w", "ffn_conv_w")
SMALL = (("norm_mix", 1024), ("ssm_conv_b", 2048), ("dt_bias", 16), ("a_log", 16), ("d_skip", 16), ("ssm_norm", 1024),
         ("q_norm", 384), ("kv_norm", 256), ("attn_out_norm", 1024), ("norm_mem_q", 1024), ("norm_mem_kv", 1024),
         ("norm_ffn", 1024), ("ffn_conv_b", 5632))
PACK_COLS = 1024


def _pad_cols(t, n):
    return jnp.pad(t, ((0, 0),) * (t.ndim - 1) + ((0, n - t.shape[-1]),))


def _w_in_to_padded(t):
    z, xbc, dt, cq, ckv, kr = jnp.split(t, (1024, 3072, 3088, 3472, 3728), axis=-1)
    return jnp.concatenate([xbc, z, cq, _pad_cols(dt, LANES), ckv, _pad_cols(kr, P_IN - P_KR)], axis=-1)


def _w_in_from_padded(t):
    return jnp.concatenate([t[..., P_Z:P_Z + 1024], t[..., P_XBC:P_XBC + 2048], t[..., P_DT:P_DT + SSM_HEADS],
                            t[..., P_CQ:P_CQ + Q_LORA], t[..., P_CKV:P_CKV + KV_LORA], t[..., P_KR:P_KR + QK_ROPE]], axis=-1)


def _cols_joined(g):
    return jnp.concatenate([g[j] for j in range(N_CHIPS)], axis=-1)


def _cols_by_chip(t, dtype):
    k = t.shape[0]
    return t.reshape(k, N_CHIPS, -1).transpose(1, 0, 2).astype(dtype)


def _rows_by_chip(t):
    return t.reshape(N_CHIPS, -1, t.shape[-1])


def _mixer_weights(gw):
    wl = {}
    uq = _cols_joined(gw["w_uq"]).reshape(Q_LORA, MLA_HEADS, QK_NOPE + QK_ROPE)
    wl["w_uq"] = _pad_cols(uq, HEAD_PAD).reshape(Q_LORA, MLA_HEADS * HEAD_PAD)
    wl["w_ukv"] = _cols_joined(gw["w_ukv"])
    wl["ssm_conv_w"] = _cols_joined(gw["ssm_conv_w"])
    return wl


def _layer_fwd(x0, mem, cos, sins, weights, sp, li):
    n = lambda t: f"l{li}_{t}"
    lead = ()
    sv = {"x0": x0}
    gw = dict(weights("proj", x0))
    w_in = _w_in_to_padded(_cols_joined(gw["w_in"]))
    h = _rms_fwd(x0, sp["norm_mix"], name=n("mix_norm"))
    in_hbm = lambda t: pltpu.with_memory_space_constraint(t, pltpu.HBM)
    proj = in_hbm(_mm(h, w_in, name=n("mix_proj")))
    gw.update(weights("mixer", proj))
    wl = dict(_mixer_weights(gw), w_in=w_in)
    xbc = in_hbm(_ssm_conv_fwd(proj, wl["ssm_conv_w"], sp["ssm_conv_b"], name=n("ssm_conv")))
    y, pstates = _ssd_fwd(xbc, proj, sp["dt_bias"], sp["a_log"], sp["d_skip"], name=n("ssd"))
    y_ssm = _gated_rms_fwd(y, proj, sp["ssm_norm"], name=n("ssm_gate"))
    cqn = _rms_fwd(proj, sp["q_norm"], col=(Q_LORA, P_CQ // Q_LORA), name=n("q_norm"))
    ckvn = _rms_fwd(proj, sp["kv_norm"], col=(KV_LORA, P_CKV // KV_LORA), name=n("kv_norm"))
    q = in_hbm(_mm(cqn, wl["w_uq"], name=n("uq")))
    kv = in_hbm(_mm(ckvn, wl["w_ukv"], name=n("ukv")))
    qr, kr, v = _mla_prep(q, kv, proj, cos, sins, name=n("rope"))
    att, lse = _flash_fwd(qr, kr, v, name=n("flash"))
    y_att = _rms_fwd(att, sp["attn_out_norm"], name=n("att_norm"))
    x1 = _mm(y_ssm, gw["w_out"], b_lead=lead, b_rows=(0, D_SSM), res=x0, name=n("out_a"))
    x1 = _mm(y_att, gw["w_out"], b_lead=lead, b_rows=(D_SSM, D_SSM), res=x1, name=n("out_b"))
    sv.update(h=h, proj=proj, xbc=xbc, y=y, pstates=pstates, y_ssm=y_ssm, cqn=cqn, ckvn=ckvn, qr=qr, kr=kr, v=v,
              att=att, lse=lse, y_att=y_att, x1=x1)
    gw.update(weights("mem", x1))
    hq = _rms_fwd(x1, sp["norm_mem_q"], name=n("memq_norm"))
    hm = _rms_fwd(mem, sp["norm_mem_kv"], name=n("memkv_norm"))
    mq = _mm(hq, gw["w_mq"], b_lead=lead, out_dtype=BF16, name=n("mq"))
    mk = _mm(hm, gw["w_mk"], b_lead=lead, out_dtype=BF16, name=n("mk"))
    mv = _mm(hm, gw["w_mv"], b_lead=lead, out_dtype=BF16, name=n("mv"))
    mo = _mem_attn_fwd(mq, mk, mv, name=n("mem_attn"))
    x2 = _mm(mo, gw["w_mo"], b_lead=lead, res=x1, name=n("mo"))
    sv.update(hq=hq, hm=hm, mq=mq, mk=mk, mv=mv, mo=mo, x2=x2)
    gw.update(weights("ffn", x2))
    wl["ffn_conv_w"] = _cols_joined(gw["ffn_conv_w"])
    hf = _rms_fwd(x2, sp["norm_ffn"], name=n("ffn_norm"))
    up_g = _mm(hf, gw["w_up"], b_lead=lead, b_chips=(0, 2), name=n("up_g"))
    up_v = _mm(hf, gw["w_up"], b_lead=lead, b_chips=(2, 2), name=n("up_v"))
    act = _ffn_conv_fwd(up_g, up_v, wl["ffn_conv_w"], sp["ffn_conv_b"], name=n("ffn_conv"))
    x3 = _mm(act, gw["w_down"], b_lead=lead, res=x2, name=n("down"))
    sv.update(hf=hf, up_g=up_g, up_v=up_v, act=act)
    return x3, sv, gw, wl


def _layer_bwd(dx3, dx3b, mem, cos, sins, gw, wl, sp, sv, li, emit):
    n = lambda t: f"l{li}_b_{t}"
    lead = ()
    g = {}

    def after(token, v):
        return v if token is None else v + token[0, 0]

    dact = _mm(dx3b, gw["w_down"], tb=True, b_lead=lead, out_dtype=BF16, name=n("down_dx"))
    g["w_down"] = _rows_by_chip(_mm(sv["act"], dx3b, ta=True, out_dtype=BF16, name=n("down_dw")))
    dup_g, dup_v, dcw, g["ffn_conv_b"] = _ffn_conv_bwd(
        sv["up_g"], sv["up_v"], wl["ffn_conv_w"], sp["ffn_conv_b"], dact, name=n("ffn_conv"))
    g["ffn_conv_w"] = _cols_by_chip(dcw, F32)
    nsh = UP_SHARD_COLS
    dhf = None
    for c4 in range(N_CHIPS):
        dhf = _mm(dup_g if c4 < 2 else dup_v, gw["w_up"], tb=True, a_col=(nsh, c4 % 2), b_lead=(c4,), res=dhf,
                  name=n(f"up{c4}_dx"))
    g["w_up"] = jnp.concatenate([_mm(sv["hf"], dup_g, ta=True, o_chips=nsh, out_dtype=BF16, name=n("upg_dw")),
                                 _mm(sv["hf"], dup_v, ta=True, o_chips=nsh, out_dtype=BF16, name=n("upv_dw"))], axis=0)
    dx2, dx2b, g["norm_ffn"] = _rms_bwd(sv["x2"], after(emit("ffn", g), sp["norm_ffn"]), dhf, dx3, name=n("ffn_norm"))
    dmo = _mm(dx2b, gw["w_mo"], tb=True, b_lead=lead, out_dtype=BF16, name=n("mo_dx"))
    g["w_mo"] = _rows_by_chip(_mm(sv["mo"], dx2b, ta=True, out_dtype=BF16, name=n("mo_dw")))
    dmq, dmk, dmv = _mem_attn_bwd(sv["mq"], sv["mk"], sv["mv"], dmo, name=n("mem_attn"))
    dhq = _mm(dmq, gw["w_mq"], tb=True, b_lead=lead, name=n("mq_dx"))
    g["w_mq"] = _rows_by_chip(_mm(sv["hq"], dmq, ta=True, out_dtype=BF16, name=n("mq_dw")))
    dhm = _mm(dmk, gw["w_mk"], tb=True, b_lead=lead, name=n("mk_dx"))
    dhm = _mm(dmv, gw["w_mv"], tb=True, b_lead=lead, res=dhm, name=n("mv_dx"))
    g["w_mk"] = _rows_by_chip(_mm(sv["hm"], dmk, ta=True, out_dtype=BF16, name=n("mk_dw")))
    g["w_mv"] = _rows_by_chip(_mm(sv["hm"], dmv, ta=True, out_dtype=BF16, name=n("mv_dw")))
    dx1, dx1b, g["norm_mem_q"] = _rms_bwd(sv["x1"], after(emit("mem", g), sp["norm_mem_q"]), dhq, dx2, name=n("memq_norm"))
    _, _, g["norm_mem_kv"] = _rms_bwd(mem, sp["norm_mem_kv"], dhm, name=n("memkv_norm"))
    dy_ssm = _mm(dx1b, gw["w_out"], tb=True, b_lead=lead, b_rows=(0, D_SSM), name=n("outa_dx"))
    dy_att = _mm(dx1b, gw["w_out"], tb=True, b_lead=lead, b_rows=(D_SSM, D_SSM), name=n("outb_dx"))
    g["w_out"] = _rows_by_chip(jnp.concatenate([_mm(sv["y_ssm"], dx1b, ta=True, out_dtype=BF16, name=n("outa_dw")),
                                                _mm(sv["y_att"], dx1b, ta=True, out_dtype=BF16, name=n("outb_dw"))], axis=0))
    datt, _, g["attn_out_norm"] = _rms_bwd(sv["att"], sp["attn_out_norm"], dy_att, name=n("att_norm"))
    dqr, dkr, dv = _flash_bwd(sv["qr"], sv["kr"], sv["v"], sv["att"], sv["lse"], datt, name=n("flash"))
    dq, dkv, dkrope = _mla_prep_bwd(dqr, dkr, dv, cos, sins, name=n("rope"))
    duq = _mm(sv["cqn"], dq, ta=True, name=n("uq_dw")).reshape(Q_LORA, MLA_HEADS, HEAD_PAD)[..., :QK_NOPE + QK_ROPE]
    g["w_uq"] = _cols_by_chip(duq.reshape(Q_LORA, -1), BF16)
    dcqn = _mm(dq, wl["w_uq"], tb=True, name=n("uq_dx"))
    g["w_ukv"] = _cols_by_chip(_mm(sv["ckvn"], dkv, ta=True, name=n("ukv_dw")), BF16)
    dckvn = _mm(dkv, wl["w_ukv"], tb=True, name=n("ukv_dx"))
    proj = sv["proj"]
    _, dcq, g["q_norm"] = _rms_bwd(proj, sp["q_norm"], dcqn, col=(Q_LORA, P_CQ // Q_LORA), name=n("q_norm"))
    _, dckv, g["kv_norm"] = _rms_bwd(proj, sp["kv_norm"], dckvn, col=(KV_LORA, P_CKV // KV_LORA), name=n("kv_norm"))
    dy, dz, g["ssm_norm"] = _gated_rms_bwd(sv["y"], proj, sp["ssm_norm"], dy_ssm, name=n("ssm_gate"))
    dxbc, ddt, da_log, dd_skip, ddt_bias = _ssd_bwd(
        sv["xbc"], proj, sp["dt_bias"], sp["a_log"], sp["d_skip"], sv["pstates"], dy, name=n("ssd"))
    g["a_log"], g["d_skip"], g["dt_bias"] = da_log[0, :SSM_HEADS], dd_skip[0, :SSM_HEADS], ddt_bias[0, :SSM_HEADS]
    dxbc_pre, dsw, g["ssm_conv_b"] = _ssm_conv_bwd(proj, wl["ssm_conv_w"], sp["ssm_conv_b"], dxbc, name=n("ssm_conv"))
    g["ssm_conv_w"] = _cols_by_chip(dsw, F32)
    started = emit("mixer", g)
    s = proj.shape[0]
    dproj = jnp.concatenate([dxbc_pre, dz, dcq, ddt, dckv, dkrope,
                             jnp.zeros((s, P_IN - P_KR - LANES), BF16)], axis=1)
    dh = _mm(dproj, wl["w_in"], tb=True, name=n("proj_dx"))
    g["w_in"] = _cols_by_chip(_w_in_from_padded(_mm(sv["h"], dproj, ta=True, name=n("proj_dw"))), BF16)
    dx0, dx0b, g["norm_mix"] = _rms_bwd(sv["x0"], after(started, sp["norm_mix"]), dh, dx1, name=n("mix_norm"))
    return dx0, dx0b, g, emit("proj", g)


def _chip_peers(x, y):
    return [(1 - x, y), (x, 1 - y), (1 - x, 1 - y)]


HBM_SPEC = pl.BlockSpec(memory_space=pltpu.HBM)
SEM_SPEC = pl.BlockSpec(memory_space=pltpu.SEMAPHORE)
ANY_SPEC = pl.BlockSpec(memory_space=pl.ANY)
VMEM_SPEC = pl.BlockSpec(memory_space=pltpu.VMEM)
DATAFLOW = pltpu.SideEffectType.DATAFLOW_SIDE_EFFECTING
TOKEN_SHAPE = (8, LANES)


def _exchange_start(srcs, land_shapes, src_view, dst_view, token, *, name):
    n = len(srcs)

    def body(*refs):
        s, l, tok_in = refs[:n], refs[n:2 * n], refs[2 * n]
        send_sems, recv_sems = refs[2 * n + 1], refs[2 * n + 2]
        tok_out = refs[-1]
        x, y, c = lax.axis_index("x"), lax.axis_index("y"), lax.axis_index("c")
        me = 2 * x + y
        for t in range(n):
            for k, (px, py) in enumerate(_chip_peers(x, y)):
                pltpu.make_async_remote_copy(
                    src_ref=src_view(t, s[t], 2 * px + py), dst_ref=dst_view(t, l[t], me), send_sem=send_sems.at[3 * t + k],
                    recv_sem=recv_sems.at[3 * t + k], device_id=(px, py, c), device_id_type=MESH).start()
            pltpu.make_async_copy(src_view(t, s[t], me), dst_view(t, l[t], me), send_sems.at[3 * n + t]).start()
        tok_out[...] = tok_in[...]

    hbm = lambda t: pltpu.with_memory_space_constraint(t, pltpu.HBM)
    lands = [lax.empty(l.shape, l.dtype) for l in land_shapes]
    outs = pl.pallas_call(
        body, name=name,
        out_shape=(pltpu.SemaphoreType.DMA((4 * n,)), pltpu.SemaphoreType.DMA((3 * n,)),
                   *[pltpu.HBM(l.shape, l.dtype) for l in land_shapes], SDS(TOKEN_SHAPE, F32)),
        in_specs=[HBM_SPEC] * (2 * n) + [VMEM_SPEC], out_specs=(SEM_SPEC, SEM_SPEC, *[HBM_SPEC] * n, VMEM_SPEC),
        input_output_aliases={n + t: 2 + t for t in range(n)},
        compiler_params=pltpu.CompilerParams(has_side_effects=DATAFLOW))(*[hbm(t) for t in srcs], *[hbm(t) for t in lands], token)
    return outs[0], outs[1], list(outs[2:2 + n]), outs[-1]


def _exchange_wait(srcs, lands, send_sems, recv_sems, after, src_view, dst_view, which, *, name):
    n = len(srcs)
    m = len(which)

    def body(*refs):
        s, l = refs[:m], refs[m:2 * m]
        send_ref, recv_ref = refs[2 * m], refs[2 * m + 1]
        x, y, c = lax.axis_index("x"), lax.axis_index("y"), lax.axis_index("c")
        me = 2 * x + y
        for i, t in enumerate(which):
            for k, (px, py) in enumerate(_chip_peers(x, y)):
                chip = 2 * px + py
                cp = pltpu.make_async_remote_copy(
                    src_ref=src_view(t, s[i], chip), dst_ref=dst_view(t, l[i], chip), send_sem=send_ref.at[3 * t + k],
                    recv_sem=recv_ref.at[3 * t + k], device_id=(px, py, c), device_id_type=MESH)
                cp.wait_send()
                cp.wait_recv()
            pltpu.make_async_copy(src_view(t, s[i], me), dst_view(t, l[i], me), send_ref.at[3 * n + t]).wait()

    outs = pl.pallas_call(
        body, name=name, out_shape=[pltpu.HBM(lands[t].shape, lands[t].dtype) for t in which],
        in_specs=[HBM_SPEC] * (2 * m) + [SEM_SPEC, SEM_SPEC, ANY_SPEC], out_specs=[HBM_SPEC] * m,
        input_output_aliases={m + i: i for i in range(m)},
        compiler_params=pltpu.CompilerParams(has_side_effects=DATAFLOW))(
            *[srcs[t] for t in which], *[lands[t] for t in which], send_sems, recv_sems, after)
    return list(outs)


def _gather_layer_start(shards, li, token, tag=""):
    src_view = lambda t, ref, chip: ref.at[li]
    dst_view = lambda t, ref, chip: ref.at[chip]
    send_sems, recv_sems, lands, token = _exchange_start(
        shards, [SDS((N_CHIPS,) + s.shape[1:], s.dtype) for s in shards], src_view, dst_view, token,
        name=f"gather{li}{tag}_start")
    return (shards, lands, send_sems, recv_sems, src_view, dst_view, f"gather{li}{tag}"), token


def _scatter_start(grads, tag, token):
    view = lambda t, ref, chip: ref.at[chip]
    send_sems, recv_sems, lands, token = _exchange_start(
        grads, [SDS(g.shape, g.dtype) for g in grads], view, view, token, name=f"scatter{tag}_start")
    return (grads, lands, send_sems, recv_sems, view, view, f"scatter{tag}"), token


def _exchange_finish(handle, after, which=None, tag=""):
    srcs, lands, send_sems, recv_sems, src_view, dst_view, name = handle
    which = tuple(range(len(srcs))) if which is None else which
    return _exchange_wait(srcs, lands, send_sems, recv_sems, after, src_view, dst_view, which, name=f"{name}{tag}_wait")


def _swap_start(bufs, token, *, name):
    n = len(bufs)

    def body(*refs):
        s, l, tok_in = refs[:n], refs[n:2 * n], refs[2 * n]
        send_sems, recv_sems = refs[2 * n + 1], refs[2 * n + 2]
        x, y, c = lax.axis_index("x"), lax.axis_index("y"), lax.axis_index("c")
        for t in range(n):
            pltpu.make_async_remote_copy(src_ref=s[t], dst_ref=l[t], send_sem=send_sems.at[t], recv_sem=recv_sems.at[t],
                                         device_id=(x, y, 1 - c), device_id_type=MESH).start()
        refs[-1][...] = tok_in[...]

    hbm = lambda t: pltpu.with_memory_space_constraint(t, pltpu.HBM)
    lands = [lax.empty(b.shape, b.dtype) for b in bufs]
    outs = pl.pallas_call(
        body, name=f"{name}_start",
        out_shape=(pltpu.SemaphoreType.DMA((n,)), pltpu.SemaphoreType.DMA((n,)),
                   *[pltpu.HBM(b.shape, b.dtype) for b in bufs], SDS(TOKEN_SHAPE, F32)),
        in_specs=[HBM_SPEC] * (2 * n) + [VMEM_SPEC], out_specs=(SEM_SPEC, SEM_SPEC, *[HBM_SPEC] * n, VMEM_SPEC),
        input_output_aliases={n + t: 2 + t for t in range(n)},
        compiler_params=pltpu.CompilerParams(has_side_effects=DATAFLOW))(*[hbm(t) for t in bufs], *[hbm(t) for t in lands], token)
    return (bufs, list(outs[2:2 + n]), outs[0], outs[1], name), outs[-1]


def _swap_wait(handle, after):
    bufs, lands, send_sems, recv_sems, name = handle
    n = len(bufs)

    def body(*refs):
        s, l = refs[:n], refs[n:2 * n]
        send_ref, recv_ref = refs[2 * n], refs[2 * n + 1]
        x, y, c = lax.axis_index("x"), lax.axis_index("y"), lax.axis_index("c")
        for t in range(n):
            cp = pltpu.make_async_remote_copy(src_ref=s[t], dst_ref=l[t], send_sem=send_ref.at[t], recv_sem=recv_ref.at[t],
                                              device_id=(x, y, 1 - c), device_id_type=MESH)
            cp.wait_send()
            cp.wait_recv()

    outs = pl.pallas_call(
        body, name=f"{name}_wait", out_shape=[pltpu.HBM(b.shape, b.dtype) for b in bufs],
        in_specs=[HBM_SPEC] * (2 * n) + [SEM_SPEC, SEM_SPEC, ANY_SPEC], out_specs=[HBM_SPEC] * n,
        input_output_aliases={n + t: t for t in range(n)},
        compiler_params=pltpu.CompilerParams(has_side_effects=DATAFLOW))(*bufs, *lands, send_sems, recv_sems, after)
    return list(outs)


def _all_gather8(src, *, name):
    def body(src_ref, out_ref, send_sems, recv_sems, local_sem):
        x, y, c = lax.axis_index("x"), lax.axis_index("y"), lax.axis_index("c")
        me = 4 * x + 2 * y + c
        mine = pltpu.make_async_copy(src_ref, out_ref.at[me], local_sem)
        mine.start()

        def peer(k):
            return (x ^ (k >> 2 & 1), y ^ (k >> 1 & 1), c ^ (k & 1))

        sends = []
        for k in range(1, N_DEV):
            cp = pltpu.make_async_remote_copy(src_ref=src_ref, dst_ref=out_ref.at[me], send_sem=send_sems.at[k - 1],
                                              recv_sem=recv_sems.at[k - 1], device_id=peer(k), device_id_type=MESH)
            cp.start()
            sends.append(cp)
        for k in range(1, N_DEV):
            px, py, pc = peer(k)
            pltpu.make_async_remote_copy(src_ref=src_ref, dst_ref=out_ref.at[4 * px + 2 * py + pc],
                                         send_sem=send_sems.at[k - 1], recv_sem=recv_sems.at[k - 1],
                                         device_id=peer(k), device_id_type=MESH).wait_recv()
        for cp in sends:
            cp.wait_send()
        mine.wait()

    any_spec = pl.BlockSpec(memory_space=pl.ANY)
    return pl.pallas_call(
        body, in_specs=[any_spec], out_specs=any_spec, out_shape=SDS((N_DEV,) + src.shape, src.dtype),
        scratch_shapes=[pltpu.SemaphoreType.DMA((N_DEV - 1,)), pltpu.SemaphoreType.DMA((N_DEV - 1,)), pltpu.SemaphoreType.DMA],
        name=name)(src)


def _adam_terms(w, g, m, v):
    m = ADAM_B1 * m + (1.0 - ADAM_B1) * g
    v = ADAM_B2 * v + (1.0 - ADAM_B2) * (g * g)
    m_hat = m / (1.0 - ADAM_B1 ** ADAM_STEP)
    v_hat = v / (1.0 - ADAM_B2 ** ADAM_STEP)
    delta = -ADAM_LR * (m_hat / (jnp.sqrt(v_hat) + ADAM_EPS) + ADAM_WD * w)
    return delta, m, v


def _adamw_shard(mine, other, w, m, v, *, name):
    d, a, b = w.shape
    tr = next((t for t in (128, 64, 32, 16) if a % t == 0), a)

    def body(*refs):
        ga, gb = refs[:d], refs[d:2 * d]
        w_ref, m_ref, v_ref, g_ref, d_ref, nm_ref, nv_ref = refs[2 * d:]

        def plane(ref):
            return ((ref[0].astype(F32) + ref[1].astype(F32)) + ref[2].astype(F32)) + ref[3].astype(F32)

        for lp in range(d):
            @pl.when(pl.program_id(0) == lp)
            def _(lp=lp):
                g = plane(ga[lp]) + plane(gb[lp])
                delta, mn, vn = _adam_terms(w_ref[...], g, m_ref[...], v_ref[...])
                g_ref[...] = g
                d_ref[...] = delta
                nm_ref[...] = mn
                nv_ref[...] = vn

    gspecs = [pl.BlockSpec((N_CHIPS, tr, b), lambda l, i, lp=lp: (0, jnp.where(l == lp, i, 0), 0)) for lp in range(d)]
    blk = pl.BlockSpec((None, tr, b), lambda l, i: (l, i, 0))
    shp = SDS((d, a, b), F32)
    return pl.pallas_call(
        body, grid=(d, a // tr), in_specs=gspecs + gspecs + [blk, blk, blk], out_specs=(blk,) * 4, out_shape=(shp,) * 4,
        name=name, compiler_params=_params(("arbitrary", "arbitrary"), 48 << 20))(*mine, *other, w, m, v)


def _adamw_small(g8, w, m, v, *, name):
    n = w.shape[1]

    def body(g8_ref, w_ref, m_ref, v_ref, g_ref, d_ref, nm_ref, nv_ref):
        g = g8_ref[0]
        for k in range(1, N_DEV):
            g = g + g8_ref[k]
        delta, mn, vn = _adam_terms(w_ref[...], g, m_ref[...], v_ref[...])
        g_ref[...] = g
        d_ref[...] = delta
        nm_ref[...] = mn
        nv_ref[...] = vn

    shp = SDS((1, n), F32)
    return pl.pallas_call(body, out_shape=(shp,) * 4, name=name, compiler_params=_params(None, 24 << 20))(g8, w, m, v)


def _rope_tables(positions):
    inv_freq = 1.0 / (ROPE_THETA ** (jnp.arange(0, QK_ROPE, 2, dtype=F32) / QK_ROPE))
    ang = positions.astype(F32)[:, None] * inv_freq
    c, s = jnp.cos(ang), jnp.sin(ang)
    n = positions.shape[0]
    pad = jnp.zeros((n, HEAD_PAD - QK_NOPE - QK_ROPE), F32)
    cos = jnp.concatenate([jnp.ones((n, QK_NOPE), F32), c, c, pad], axis=1)
    sins = jnp.concatenate([jnp.zeros((n, QK_NOPE), F32), -s, s, pad], axis=1)
    return cos, sins


def _pad_lanes(v):
    return _pad_cols(v.reshape(1, -1), LANES)


def _local_step(x, mem, positions, weights, small, final_norm, loss_target, emit, token):
    cos, sins = _rope_tables(positions)
    saved, gws, wls, sps = [], [], [], []
    h = x
    for li in range(DEPTH):
        sp = {k: small[k][li] for k, _ in SMALL}
        if li == 0:
            sp["norm_mix"] = sp["norm_mix"] + token[0, 0]
        for k in ("dt_bias", "a_log", "d_skip"):
            sp[k] = _pad_lanes(sp[k])
        h, sv, gw, wl = _layer_fwd(h, mem, cos, sins, functools.partial(weights, li), sp, li)
        saved.append(sv)
        gws.append(gw)
        wls.append(wl)
        sps.append(sp)
    loss, dh, dhb, g_final = _final_loss(h, final_norm, loss_target, name="final_loss")
    grads = [None] * DEPTH
    started = None
    for li in reversed(range(DEPTH)):
        sp = sps[li]
        if started is not None:
            sp = dict(sp, ffn_conv_b=sp["ffn_conv_b"] + started[0, 0])
        dh, dhb, grads[li], started = _layer_bwd(dh, dhb, mem, cos, sins, gws[li], wls[li], sp, saved[li], li,
                                                 functools.partial(emit, li))
    return loss, dh, grads, g_final


def _gathered_views(which, lands):
    return {MATS[t][0]: (b.reshape(-1, b.shape[-1]) if MATS[t][2] == 0 else b) for t, b in zip(which, lands)}


def kernel(x, mem, positions, norm_mix, w_in, ssm_conv_w, ssm_conv_b, dt_bias, a_log, d_skip, ssm_norm, q_norm, w_uq, kv_norm, w_ukv, attn_out_norm, w_out, norm_mem_q, norm_mem_kv, w_mq, w_mk, w_mv, w_mo, norm_ffn, w_up, ffn_conv_w, ffn_conv_b, w_down, final_norm, loss_target, m_norm_mix, m_w_in, m_ssm_conv_w, m_ssm_conv_b, m_dt_bias, m_a_log, m_d_skip, m_ssm_norm, m_q_norm, m_w_uq, m_kv_norm, m_w_ukv, m_attn_out_norm, m_w_out, m_norm_mem_q, m_norm_mem_kv, m_w_mq, m_w_mk, m_w_mv, m_w_mo, m_norm_ffn, m_w_up, m_ffn_conv_w, m_ffn_conv_b, m_w_down, m_final_norm, v_norm_mix, v_w_in, v_ssm_conv_w, v_ssm_conv_b, v_dt_bias, v_a_log, v_d_skip, v_ssm_norm, v_q_norm, v_w_uq, v_kv_norm, v_w_ukv, v_attn_out_norm, v_w_out, v_norm_mem_q, v_norm_mem_kv, v_w_mq, v_w_mk, v_w_mv, v_w_mo, v_norm_ffn, v_w_up, v_ffn_conv_w, v_ffn_conv_b, v_w_down, v_final_norm):
    args = dict(locals())
    names = ["norm_mix", "w_in", "ssm_conv_w", "ssm_conv_b", "dt_bias", "a_log", "d_skip", "ssm_norm", "q_norm", "w_uq",
             "kv_norm", "w_ukv", "attn_out_norm", "w_out", "norm_mem_q", "norm_mem_kv", "w_mq", "w_mk", "w_mv", "w_mo",
             "norm_ffn", "w_up", "ffn_conv_w", "ffn_conv_b", "w_down", "final_norm"]
    wts = {k: args[k] for k in names}
    mom = {k: args["m_" + k] for k in names}
    var = {k: args["v_" + k] for k in names}
    mat_names = [k for k, _, _ in MATS]

    shards = [wts[k] if k in F32_ON_WIRE else wts[k].astype(BF16) for k in mat_names]
    token = jnp.zeros(TOKEN_SHAPE, F32)
    first, token = _gather_layer_start(shards[:1], 0, token, tag="_first")
    gathers = []
    for li in range(DEPTH):
        handle, token = _gather_layer_start(shards[1:] if li == 0 else shards, li, token)
        gathers.append(handle)
    small = {k: wts[k] for k, _ in SMALL}

    def weights(li, group, after):
        which = GROUPS[group]
        if li > 0:
            return _gathered_views(which, _exchange_finish(gathers[li], after, which, tag=f"_{group}"))
        if group == "proj":
            return _gathered_views(which, _exchange_finish(first, after))
        return _gathered_views(which, _exchange_finish(gathers[0], after, tuple(t - 1 for t in which), tag=f"_{group}"))

    scatters = [[] for _ in range(DEPTH)]
    nm = len(mat_names)
    mine = [[None] * nm for _ in range(DEPTH)]
    swaps = [None] * DEPTH
    last_started = [None]

    def swap_layer(li, after):
        for which, handle in scatters[li]:
            for t, b in zip(which, _exchange_finish(handle, after)):
                mine[li][t] = b
        swaps[li], started = _swap_start(mine[li], jnp.zeros(TOKEN_SHAPE, F32), name=f"swap{li}")
        return started

    def emit(li, group, g):
        last = group == "proj"
        if li == 0:
            which = GROUPS[group]
        elif last:
            which = tuple(range(nm))
        else:
            return None
        handle, started = _scatter_start([g[MATS[t][0]] for t in which], f"{li}_{group}", jnp.zeros(TOKEN_SHAPE, F32))
        scatters[li].append((which, handle))
        last_started[0] = started
        if li + 1 < DEPTH and group == ("mixer" if li == 0 else "proj"):
            started = started + swap_layer(li + 1, g["ssm_conv_w"])
        return started

    loss, grad_x, grads, g_final = _local_step(x[0], mem[0], positions[0], weights, small, wts["final_norm"],
                                               loss_target[0], emit, token)
    loss = lax.psum(loss, ("x", "y", "c"))

    w_in_t = GROUPS["proj"][0]
    others = [t for t in range(nm) if t != w_in_t]
    for which, handle in scatters[0]:
        if which != GROUPS["proj"]:
            for t, b in zip(which, _exchange_finish(handle, grad_x)):
                mine[0][t] = b
    swaps[0], _ = _swap_start([mine[0][t] for t in others], last_started[0], name="swap0")
    other = [dict(zip(others, _swap_wait(swaps[0], grad_x)))] + [
        dict(enumerate(_swap_wait(swaps[li], grad_x))) for li in range(1, DEPTH)]

    def adamw(t):
        k = mat_names[t]
        return _adamw_shard([mine[li][t] for li in range(DEPTH)], [other[li][t] for li in range(DEPTH)],
                            wts[k], mom[k], var[k], name=f"adamw_{k}")

    mat_out = {mat_names[t]: adamw(t) for t in others}
    done = sum(mat_out[mat_names[t]][0][0, 0, :1] for t in others)
    (last_handle,) = [handle for which, handle in scatters[0] if which == GROUPS["proj"]]
    (mine[0][w_in_t],) = _exchange_finish(last_handle, done)
    last_swap, _ = _swap_start([mine[0][w_in_t]], jnp.zeros(TOKEN_SHAPE, F32), name="swap0_last")
    (other[0][w_in_t],) = _swap_wait(last_swap, done)
    mat_out[mat_names[w_in_t]] = adamw(w_in_t)

    def pack_small(get, fin):
        flat = [get(k).reshape(-1) for k, _ in SMALL] + [fin.reshape(-1)]
        n = sum(f.shape[0] for f in flat)
        return jnp.concatenate(flat + [jnp.zeros((-n % PACK_COLS,), F32)]).reshape(1, -1)

    gs = pack_small(lambda k: jnp.stack([grads[li][k] for li in range(DEPTH)]), g_final)
    g8 = _all_gather8(gs, name="gather_small_grads")
    small_out = _adamw_small(g8, pack_small(wts.get, wts["final_norm"]), pack_small(mom.get, mom["final_norm"]),
                             pack_small(var.get, var["final_norm"]), name="adamw_small")

    def unpack_small(buf):
        out, off = {}, 0
        for k, nel in SMALL:
            out[k] = buf[0, off:off + DEPTH * nel].reshape(DEPTH, nel)
            off += DEPTH * nel
        out["final_norm"] = buf[0, off:off + D_MODEL]
        return out

    small_res = [unpack_small(b) for b in small_out]
    res = []
    for kind in range(4):
        for k in names:
            res.append(small_res[kind][k] if k in small_res[kind] else mat_out[k][kind])
    return (loss, grad_x[None], *res)
```

```python
import functools
import math

import jax
import jax.numpy as jnp
from jax import lax
from jax.experimental import pallas as pl
from jax.experimental.pallas import tpu as pltpu

F32 = jnp.float32
BF16 = jnp.bfloat16
HIGHEST = lax.Precision.HIGHEST
SDS = jax.ShapeDtypeStruct
MESH = pl.DeviceIdType.MESH

D_MODEL = 1024
DEPTH = 4
EPS = 1e-6
SSM_HEADS = 16
SSM_HEAD_DIM = 64
D_SSM = 1024
SSM_GROUPS = 4
SSM_STATE = 128
SSM_CONV = 4
SSM_CHUNK = 128
CONV_CH = 2048
MLA_HEADS = 16
QK_NOPE = 64
QK_ROPE = 32
V_DIM = 64
Q_LORA = 384
KV_LORA = 256
ROPE_THETA = 10000.0
MEM_HEADS = 4
MEM_HEAD_DIM = 256
D_FF = 2816
FFN_CONV = 3
D_IN = 3760
ADAM_LR = 0.001
ADAM_B1 = 0.9
ADAM_B2 = 0.999
ADAM_EPS = 1e-08
ADAM_WD = 0.01
ADAM_STEP = 10

LANES = 128
HEAD_PAD = 128
N_CHIPS = 4
N_DEV = 8
VMEM_CAP_MB = 56
MM_ROW_BLOCK_BYTES = 8 << 20

P_XBC, P_Z, P_CQ, P_DT, P_CKV, P_KR, P_IN = 0, 2048, 3072, 3456, 3584, 3840, 4096
NEG = -1e30


def _tile(n, pref):
    t = (min(n, pref) // LANES) * LANES
    while t >= LANES:
        if n % t == 0:
            return t
        t -= LANES
    return n


def _params(sem=None, vmem_bytes=None):
    kw = {}
    if sem is not None:
        kw["dimension_semantics"] = sem
    if vmem_bytes is not None:
        kw["vmem_limit_bytes"] = int(min(max(vmem_bytes, 16 << 20), VMEM_CAP_MB << 20))
    return pltpu.CompilerParams(**kw)


def _nbytes(shape, dtype):
    return math.prod(shape) * jnp.dtype(dtype).itemsize


def _mm(a, b, *, ta=False, tb=False, res=None, out_dtype=F32, name, a_col=None, b_lead=(), b_rows=None,
        b_chips=None, o_chips=None):
    if ta:
        k, m = a.shape
    else:
        m, k = (a.shape[0], a.shape[1] if a_col is None else a_col[0])
    rows_b, cols_b = b.shape[-2:]
    row0 = 0
    if b_rows is not None:
        row0, rows_b = b_rows
    nlead = len(b_lead)
    if b_chips is not None:
        assert not tb
        kb, tn, n = rows_b, cols_b, b_chips[1] * cols_b
        b_blk = (None,) * (1 + nlead) + (kb, tn)
        b_map = lambda i, j: (b_chips[0] + j,) + tuple(b_lead) + (0, 0)
    elif tb:
        n, kb = rows_b, cols_b
        tn = _tile(n, 512)
        assert row0 % tn == 0
        b_blk = (None,) * nlead + (tn, kb)
        b_map = lambda i, j: tuple(b_lead) + (j + row0 // tn, 0)
    else:
        kb, n = rows_b, cols_b
        tn = o_chips if o_chips else _tile(n, 512)
        assert row0 % kb == 0
        b_blk = (None,) * nlead + (kb, tn)
        b_map = lambda i, j: tuple(b_lead) + (row0 // kb, j)
    assert k == kb, (a.shape, b.shape, ta, tb, k, kb)
    tm = _tile(m, 2048)
    if _nbytes((tm, k), BF16) > MM_ROW_BLOCK_BYTES:
        tm = _tile(m, 1024)
    if ta:
        a_blk, a_map = (k, tm), (lambda i, j: (0, i))
    else:
        a_blk, a_map = (tm, k), ((lambda i, j: (i, 0)) if a_col is None else (lambda i, j: (i, a_col[1])))
    if o_chips:
        o_spec = pl.BlockSpec((None, tm, tn), lambda i, j: (j, i, 0))
        o_shape = SDS((n // tn, m, tn), out_dtype)
    else:
        o_spec = pl.BlockSpec((tm, tn), lambda i, j: (i, j))
        o_shape = SDS((m, n), out_dtype)
    dims = (((0 if ta else 1,), (1 if tb else 0,)), ((), ()))
    has_res = res is not None

    def body(*refs):
        a_ref, b_ref = refs[0], refs[1]
        o_ref = refs[-1]
        acc = lax.dot_general(a_ref[...].astype(BF16), b_ref[...].astype(BF16), dims, preferred_element_type=F32)
        if has_res:
            acc = acc + refs[2][...]
        o_ref[...] = acc.astype(o_ref.dtype)

    bb = tuple(d for d in b_blk if d is not None)
    vmem = 2 * (_nbytes(a_blk, a.dtype) + _nbytes(bb, b.dtype) + (2 if has_res else 1) * _nbytes((tm, tn), F32))
    vmem += _nbytes(a_blk, BF16) + _nbytes(bb, BF16) + 2 * _nbytes((tm, tn), F32) + (4 << 20)
    args = (a, b) + ((res,) if has_res else ())
    specs = [pl.BlockSpec(a_blk, a_map), pl.BlockSpec(b_blk, b_map)] + ([o_spec] if has_res else [])
    return pl.pallas_call(body, grid=(m // tm, n // tn), in_specs=specs, out_specs=o_spec, out_shape=o_shape, name=name,
                          compiler_params=_params(("parallel", "parallel"), vmem))(*args)


def _sigmoid(x):
    return 1.0 / (1.0 + jnp.exp(-x))


def _rms_fwd(x, g, *, col=None, name):
    s = x.shape[0]
    w, ci = (x.shape[1], 0) if col is None else col
    tm = min(s, 512)

    def body(x_ref, g_ref, o_ref):
        xv = x_ref[...].astype(F32)
        r = lax.rsqrt(jnp.mean(xv * xv, axis=-1, keepdims=True) + EPS)
        o_ref[...] = (xv * r * g_ref[...]).astype(o_ref.dtype)

    return pl.pallas_call(
        body, grid=(s // tm,),
        in_specs=[pl.BlockSpec((tm, w), lambda i: (i, ci)), pl.BlockSpec((1, w), lambda i: (0, 0))],
        out_specs=pl.BlockSpec((tm, w), lambda i: (i, 0)), out_shape=SDS((s, w), BF16), name=name,
        compiler_params=_params(("parallel",), 10 * tm * w * 4))(x, g.reshape(1, w))


def _rms_bwd(x, g, dy, dres=None, *, col=None, name):
    s = x.shape[0]
    w, ci = (x.shape[1], 0) if col is None else col
    tm = min(s, 512)
    has_res = dres is not None

    def body(*refs):
        x_ref, g_ref, dy_ref = refs[:3]
        dx_ref, dxb_ref, dg_ref = refs[-3:]
        xv = x_ref[...].astype(F32)
        dyv = dy_ref[...].astype(F32)
        r = lax.rsqrt(jnp.mean(xv * xv, axis=-1, keepdims=True) + EPS)
        u = dyv * g_ref[...]
        dx = r * u - xv * (r * r * r) * jnp.mean(xv * u, axis=-1, keepdims=True)
        if has_res:
            dx = dx + refs[3][...]
        dx_ref[...] = dx
        dxb_ref[...] = dx.astype(BF16)

        @pl.when(pl.program_id(0) == 0)
        def _():
            dg_ref[...] = jnp.zeros_like(dg_ref)

        dg_ref[...] += jnp.sum(dyv * xv * r, axis=0, keepdims=True)

    blk = pl.BlockSpec((tm, w), lambda i: (i, 0))
    specs = [pl.BlockSpec((tm, w), lambda i: (i, ci)), pl.BlockSpec((1, w), lambda i: (0, 0)), blk]
    args = [x, g.reshape(1, w), dy]
    if has_res:
        specs.append(blk)
        args.append(dres)
    dx, dxb, dg = pl.pallas_call(
        body, grid=(s // tm,), in_specs=specs,
        out_specs=(blk, blk, pl.BlockSpec((1, w), lambda i: (0, 0))),
        out_shape=(SDS((s, w), F32), SDS((s, w), BF16), SDS((1, w), F32)), name=name,
        compiler_params=_params(("arbitrary",), 18 * tm * w * 4))(*args)
    return dx, dxb, dg.reshape(w)


def _gated_rms_fwd(y, proj, g, *, name):
    s, w = y.shape
    tm = min(s, 512)

    def body(y_ref, z_ref, g_ref, o_ref):
        z = z_ref[...]
        t = y_ref[...] * (z * _sigmoid(z))
        r = lax.rsqrt(jnp.mean(t * t, axis=-1, keepdims=True) + EPS)
        o_ref[...] = (t * r * g_ref[...]).astype(o_ref.dtype)

    blk = pl.BlockSpec((tm, w), lambda i: (i, 0))
    return pl.pallas_call(
        body, grid=(s // tm,),
        in_specs=[blk, pl.BlockSpec((tm, w), lambda i: (i, P_Z // w)), pl.BlockSpec((1, w), lambda i: (0, 0))],
        out_specs=blk, out_shape=SDS((s, w), BF16), name=name,
        compiler_params=_params(("parallel",), 14 * tm * w * 4))(y, proj, g.reshape(1, w))


def _gated_rms_bwd(y, proj, g, dout, *, name):
    s, w = y.shape
    tm = min(s, 512)

    def body(y_ref, z_ref, g_ref, do_ref, dy_ref, dz_ref, dg_ref):
        z = z_ref[...]
        yv = y_ref[...]
        dov = do_ref[...]
        sg = _sigmoid(z)
        sz = z * sg
        t = yv * sz
        r = lax.rsqrt(jnp.mean(t * t, axis=-1, keepdims=True) + EPS)
        u = dov * g_ref[...]
        dt = r * u - t * (r * r * r) * jnp.mean(t * u, axis=-1, keepdims=True)
        dy_ref[...] = dt * sz
        dz_ref[...] = (dt * yv * (sg * (1.0 + z * (1.0 - sg)))).astype(dz_ref.dtype)

        @pl.when(pl.program_id(0) == 0)
        def _():
            dg_ref[...] = jnp.zeros_like(dg_ref)

        dg_ref[...] += jnp.sum(dov * t * r, axis=0, keepdims=True)

    blk = pl.BlockSpec((tm, w), lambda i: (i, 0))
    vec = pl.BlockSpec((1, w), lambda i: (0, 0))
    dy, dz, dg = pl.pallas_call(
        body, grid=(s // tm,),
        in_specs=[blk, pl.BlockSpec((tm, w), lambda i: (i, P_Z // w)), vec, blk],
        out_specs=(blk, blk, vec), out_shape=(SDS((s, w), F32), SDS((s, w), BF16), SDS((1, w), F32)), name=name,
        compiler_params=_params(("arbitrary",), 24 * tm * w * 4))(y, proj, g.reshape(1, w), dout)
    return dy, dz, dg.reshape(w)


def _final_loss(x, g, target, *, name):
    s, w = x.shape
    tm = min(s, 512)

    def body(x_ref, g_ref, t_ref, loss_ref, dx_ref, dxb_ref, dg_ref):
        xv = x_ref[...]
        gv = g_ref[...]
        r = lax.rsqrt(jnp.mean(xv * xv, axis=-1, keepdims=True) + EPS)
        xn = xv * r
        diff = xn * gv - t_ref[...]
        dy = diff * (1.0 / w)
        u = dy * gv
        dx = r * u - xv * (r * r * r) * jnp.mean(xv * u, axis=-1, keepdims=True)
        dx_ref[...] = dx
        dxb_ref[...] = dx.astype(BF16)

        @pl.when(pl.program_id(0) == 0)
        def _():
            dg_ref[...] = jnp.zeros_like(dg_ref)
            loss_ref[...] = jnp.zeros_like(loss_ref)

        dg_ref[...] += jnp.sum(dy * xn, axis=0, keepdims=True)
        part = jnp.sum(jnp.sum(diff * diff, axis=1, keepdims=True), axis=0, keepdims=True) * (0.5 / w)
        loss_ref[...] += jnp.broadcast_to(part, loss_ref.shape)

    blk = pl.BlockSpec((tm, w), lambda i: (i, 0))
    vec = pl.BlockSpec((1, w), lambda i: (0, 0))
    loss, dx, dxb, dg = pl.pallas_call(
        body, grid=(s // tm,), in_specs=[blk, vec, blk],
        out_specs=(pl.BlockSpec((1, LANES), lambda i: (0, 0)), blk, blk, vec),
        out_shape=(SDS((1, LANES), F32), SDS((s, w), F32), SDS((s, w), BF16), SDS((1, w), F32)), name=name,
        compiler_params=_params(("arbitrary",), 18 * tm * w * 4))(x, g.reshape(1, w), target)
    return loss[0, 0], dx, dxb, dg.reshape(w)


def _shift_down(x, k):
    if k == 0:
        return x
    row = lax.broadcasted_iota(jnp.int32, x.shape, 0)
    return jnp.where(row < k, 0.0, pltpu.roll(x, k, axis=0))


def _shift_up(x, k):
    if k == 0:
        return x
    s = x.shape[0]
    row = lax.broadcasted_iota(jnp.int32, x.shape, 0)
    return jnp.where(row >= s - k, 0.0, pltpu.roll(x, s - k, axis=0))


def _conv_pre(x, w, b, kw):
    pre = b
    for j in range(kw):
        pre = pre + w[j:j + 1, :] * _shift_down(x, kw - 1 - j)
    return pre


def _conv_bwd_terms(x, w, dpre, kw):
    dx = jnp.zeros_like(x)
    dws = []
    for j in range(kw):
        dx = dx + w[j:j + 1, :] * _shift_up(dpre, kw - 1 - j)
        dws.append(jnp.sum(dpre * _shift_down(x, kw - 1 - j), axis=0, keepdims=True))
    return dx, jnp.concatenate(dws, axis=0), jnp.sum(dpre, axis=0, keepdims=True)


def _ssm_conv_fwd(proj, w, b, *, name):
    s = proj.shape[0]
    cw = 256

    def body(x_ref, w_ref, b_ref, o_ref):
        pre = _conv_pre(x_ref[...], w_ref[...], b_ref[...], SSM_CONV)
        o_ref[...] = pre * _sigmoid(pre)

    return pl.pallas_call(
        body, grid=(CONV_CH // cw,),
        in_specs=[pl.BlockSpec((s, cw), lambda j: (0, j)), pl.BlockSpec((SSM_CONV, cw), lambda j: (0, j)),
                  pl.BlockSpec((1, cw), lambda j: (0, j))],
        out_specs=pl.BlockSpec((s, cw), lambda j: (0, j)), out_shape=SDS((s, CONV_CH), F32), name=name,
        compiler_params=_params(("parallel",), 12 * s * cw * 4))(proj, w, b.reshape(1, CONV_CH))


def _ssm_conv_bwd(proj, w, b, dxbc, *, name):
    s = proj.shape[0]
    cw = 256

    def body(x_ref, w_ref, b_ref, dy_ref, dx_ref, dw_ref, db_ref):
        x = x_ref[...]
        wv = w_ref[...]
        pre = _conv_pre(x, wv, b_ref[...], SSM_CONV)
        sg = _sigmoid(pre)
        dpre = dy_ref[...] * (sg * (1.0 + pre * (1.0 - sg)))
        dx, dw, db = _conv_bwd_terms(x, wv, dpre, SSM_CONV)
        dx_ref[...] = dx.astype(dx_ref.dtype)
        dw_ref[...] = dw
        db_ref[...] = db

    col = pl.BlockSpec((s, cw), lambda j: (0, j))
    wsp = pl.BlockSpec((SSM_CONV, cw), lambda j: (0, j))
    bsp = pl.BlockSpec((1, cw), lambda j: (0, j))
    dx, dw, db = pl.pallas_call(
        body, grid=(CONV_CH // cw,), in_specs=[col, wsp, bsp, col], out_specs=(col, wsp, bsp),
        out_shape=(SDS((s, CONV_CH), BF16), SDS((SSM_CONV, CONV_CH), F32), SDS((1, CONV_CH), F32)), name=name,
        compiler_params=_params(("parallel",), 20 * s * cw * 4))(proj, w, b.reshape(1, CONV_CH), dxbc)
    return dx, dw, db.reshape(CONV_CH)


def _ffn_conv_fwd(up_g, up_v, w, b, *, name):
    s = up_g.shape[0]
    cw = 256
    nb = D_FF // cw

    def body(g_ref, v_ref, wg_ref, wv_ref, bg_ref, bv_ref, o_ref):
        gate = _conv_pre(g_ref[...], wg_ref[...], bg_ref[...], FFN_CONV)
        val = _conv_pre(v_ref[...], wv_ref[...], bv_ref[...], FFN_CONV)
        o_ref[...] = (gate * _sigmoid(gate) * val).astype(o_ref.dtype)

    col = pl.BlockSpec((s, cw), lambda j: (0, j))
    b2 = b.reshape(1, 2 * D_FF)
    return pl.pallas_call(
        body, grid=(nb,),
        in_specs=[col, col, pl.BlockSpec((FFN_CONV, cw), lambda j: (0, j)), pl.BlockSpec((FFN_CONV, cw), lambda j: (0, j + nb)),
                  pl.BlockSpec((1, cw), lambda j: (0, j)), pl.BlockSpec((1, cw), lambda j: (0, j + nb))],
        out_specs=col, out_shape=SDS((s, D_FF), BF16), name=name,
        compiler_params=_params(("parallel",), 16 * s * cw * 4))(up_g, up_v, w, w, b2, b2)


def _ffn_conv_bwd(up_g, up_v, w, b, dact, *, name):
    s = up_g.shape[0]
    cw = 256
    nb = D_FF // cw

    def body(g_ref, v_ref, wg_ref, wv_ref, bg_ref, bv_ref, da_ref, dg_ref, dv_ref, dwg_ref, dwv_ref, dbg_ref, dbv_ref):
        xg, xv = g_ref[...], v_ref[...]
        wg, wv = wg_ref[...], wv_ref[...]
        gate = _conv_pre(xg, wg, bg_ref[...], FFN_CONV)
        val = _conv_pre(xv, wv, bv_ref[...], FFN_CONV)
        da = da_ref[...].astype(F32)
        sg = _sigmoid(gate)
        dgate = da * val * (sg * (1.0 + gate * (1.0 - sg)))
        dval = da * gate * sg
        dxg, dwg, dbg = _conv_bwd_terms(xg, wg, dgate, FFN_CONV)
        dxv, dwv, dbv = _conv_bwd_terms(xv, wv, dval, FFN_CONV)
        dg_ref[...] = dxg.astype(dg_ref.dtype)
        dv_ref[...] = dxv.astype(dv_ref.dtype)
        dwg_ref[...] = dwg
        dwv_ref[...] = dwv
        dbg_ref[...] = dbg
        dbv_ref[...] = dbv

    col = pl.BlockSpec((s, cw), lambda j: (0, j))
    wsp = pl.BlockSpec((FFN_CONV, cw), lambda j: (0, j))
    bsp = pl.BlockSpec((1, cw), lambda j: (0, j))
    b2 = b.reshape(1, 2 * D_FF)
    dg, dv, dwg, dwv, dbg, dbv = pl.pallas_call(
        body, grid=(nb,),
        in_specs=[col, col, wsp, pl.BlockSpec((FFN_CONV, cw), lambda j: (0, j + nb)), bsp,
                  pl.BlockSpec((1, cw), lambda j: (0, j + nb)), col],
        out_specs=(col, col, wsp, wsp, bsp, bsp),
        out_shape=(SDS((s, D_FF), BF16), SDS((s, D_FF), BF16), SDS((FFN_CONV, D_FF), F32), SDS((FFN_CONV, D_FF), F32),
                   SDS((1, D_FF), F32), SDS((1, D_FF), F32)), name=name,
        compiler_params=_params(("parallel",), 32 * s * cw * 4))(up_g, up_v, w, w, b2, b2, dact)
    return dg, dv, jnp.concatenate([dwg, dwv], axis=1), jnp.concatenate([dbg, dbv], axis=1).reshape(2 * D_FF)


def _dot(a, b):
    return jnp.dot(a.astype(BF16), b.astype(BF16), preferred_element_type=F32)


def _dot_nt(a, b):
    return lax.dot_general(a.astype(BF16), b.astype(BF16), (((1,), (1,)), ((), ())), preferred_element_type=F32)


def _dot_tn(a, b):
    return lax.dot_general(a.astype(BF16), b.astype(BF16), (((0,), (0,)), ((), ())), preferred_element_type=F32)


def _ssd_chunk_terms(dtraw, bias, a_log):
    ell = dtraw.shape[0]
    lane = lax.broadcasted_iota(jnp.int32, dtraw.shape, 1)
    valid = lane < SSM_HEADS
    pre = dtraw + bias
    dt = jnp.where(valid, jnp.where(pre > 20.0, pre, jnp.log(1.0 + jnp.exp(jnp.minimum(pre, 20.0)))), 0.0)
    a = -jnp.exp(a_log)
    ad = dt * a
    row = lax.broadcasted_iota(jnp.int32, (ell, ell), 0)
    colm = lax.broadcasted_iota(jnp.int32, (ell, ell), 1)
    tril = row >= colm
    cs = jnp.dot(tril.astype(F32), ad, precision=HIGHEST, preferred_element_type=F32)
    cs_last = cs[ell - 1:ell, :]
    return pre, dt, a, cs, cs_last, tril


def _head_expand():
    h = lax.broadcasted_iota(jnp.int32, (LANES, D_SSM), 0)
    c = lax.broadcasted_iota(jnp.int32, (LANES, D_SSM), 1)
    return (c // SSM_HEAD_DIM == h).astype(F32)


def _ssd_fwd(xbc, proj, dt_bias, a_log, d_skip, *, name):
    s = xbc.shape[0]
    nc = s // SSM_CHUNK
    ell, n, p = SSM_CHUNK, SSM_STATE, SSM_HEAD_DIM
    rpg = SSM_HEADS // SSM_GROUPS
    gw = rpg * p

    def body(x_ref, dt_ref, bias_ref, alog_ref, dskip_ref, ex_ref, y_ref, ps_ref, state):
        @pl.when(pl.program_id(0) == 0)
        def _():
            state[...] = jnp.zeros_like(state)

        _, dt, _, cs, cs_last, tril = _ssd_chunk_terms(dt_ref[...], bias_ref[...], alog_ref[...])
        cst = cs.T
        ex = ex_ref[...]
        spread = lambda v: jnp.dot(v, ex, precision=HIGHEST, preferred_element_type=F32)
        dt_x, e_x, ds_x = spread(dt), spread(jnp.exp(cs)), spread(jnp.exp(cs_last - cs))
        cd_x = spread(jnp.broadcast_to(jnp.exp(cs_last), (8, LANES)))[0:1, :]
        dskip_x = spread(jnp.broadcast_to(dskip_ref[...], (8, LANES)))[0:1, :]
        st = state[...]
        ps_ref[0] = st
        xv = x_ref[...]
        xs_all = xv[:, 0:D_SSM]
        xd_all = xs_all * dt_x
        xdd_all = xd_all * ds_x
        lane_g = lax.broadcasted_iota(jnp.int32, (ell, gw), 1)
        ys, new = [], []
        for g in range(SSM_GROUPS):
            gs = slice(gw * g, gw * (g + 1))
            bg = xv[:, D_SSM + n * g:D_SSM + n * (g + 1)]
            cg = xv[:, D_SSM + n * (SSM_GROUPS + g):D_SSM + n * (SSM_GROUPS + g + 1)]
            cb = _dot_nt(cg, bg)
            xd_g, prev_g = xd_all[:, gs], st[:, gs]
            y_g = _dot(cg, prev_g) * e_x[:, gs] + xs_all[:, gs] * dskip_x[:, gs]
            for r in range(rpg):
                h = g * rpg + r
                lmat = jnp.exp(jnp.where(tril, cs[:, h:h + 1] - cst[h:h + 1, :], -jnp.inf))
                y_g = y_g + jnp.where((lane_g >= p * r) & (lane_g < p * (r + 1)), _dot(cb * lmat, xd_g), 0.0)
            ys.append(y_g)
            new.append(prev_g * cd_x[:, gs] + _dot(bg.T, xdd_all[:, gs]))
        y_ref[...] = jnp.concatenate(ys, axis=1)
        state[...] = jnp.concatenate(new, axis=1)

    vec = pl.BlockSpec((1, LANES), lambda c: (0, 0))
    return pl.pallas_call(
        body, grid=(nc,),
        in_specs=[pl.BlockSpec((ell, CONV_CH), lambda c: (c, 0)), pl.BlockSpec((ell, LANES), lambda c: (c, P_DT // LANES)),
                  vec, vec, vec, pl.BlockSpec((LANES, D_SSM), lambda c: (0, 0))],
        out_specs=(pl.BlockSpec((ell, D_SSM), lambda c: (c, 0)), pl.BlockSpec((1, n, D_SSM), lambda c: (c, 0, 0))),
        out_shape=(SDS((s, D_SSM), F32), SDS((nc, n, D_SSM), F32)),
        scratch_shapes=[pltpu.VMEM((n, D_SSM), F32)], name=name,
        compiler_params=_params(("arbitrary",), 32 << 20))(xbc, proj, dt_bias, a_log, d_skip, _head_expand())


def _ssd_bwd(xbc, proj, dt_bias, a_log, d_skip, prev_states, dy, *, name):
    s = xbc.shape[0]
    nc = s // SSM_CHUNK
    ell, n, p = SSM_CHUNK, SSM_STATE, SSM_HEAD_DIM
    rpg = SSM_HEADS // SSM_GROUPS
    gw = rpg * p

    def body(x_ref, dt_ref, bias_ref, alog_ref, dskip_ref, ps_ref, dy_ref, ex_ref, ext_ref,
             dx_ref, ddt_ref, dalog_ref, ddskip_ref, dbias_ref, dstate):
        @pl.when(pl.program_id(0) == 0)
        def _():
            dstate[...] = jnp.zeros_like(dstate)
            dalog_ref[...] = jnp.zeros_like(dalog_ref)
            ddskip_ref[...] = jnp.zeros_like(ddskip_ref)
            dbias_ref[...] = jnp.zeros_like(dbias_ref)

        pre, dt, a, cs, cs_last, tril = _ssd_chunk_terms(dt_ref[...], bias_ref[...], alog_ref[...])
        e = jnp.exp(cs)
        ds = jnp.exp(cs_last - cs)
        cd = jnp.exp(cs_last)
        cst = cs.T
        shape = (ell, LANES)
        ex, ext = ex_ref[...], ext_ref[...]
        spread = lambda v: jnp.dot(v, ex, precision=HIGHEST, preferred_element_type=F32)
        gather = lambda v: jnp.dot(v, ext, precision=HIGHEST, preferred_element_type=F32)
        dt_x, e_x, ds_x = spread(dt), spread(e), spread(ds)
        cd_x = spread(jnp.broadcast_to(cd, (8, LANES)))[0:1, :]
        dskip_x = spread(jnp.broadcast_to(dskip_ref[...], (8, LANES)))[0:1, :]
        xv, dyv, psv, dst = x_ref[...], dy_ref[...], ps_ref[0], dstate[...]
        xs_all = xv[:, 0:D_SSM]
        xd_all = xs_all * dt_x
        dye_all = dyv * e_x
        xdd_all = xd_all * ds_x
        triu = lax.broadcasted_iota(jnp.int32, (ell, ell), 0) <= lax.broadcasted_iota(jnp.int32, (ell, ell), 1)
        lane_g = lax.broadcasted_iota(jnp.int32, (ell, gw), 1)
        lane = lax.broadcasted_iota(jnp.int32, shape, 1)
        sub = lax.broadcasted_iota(jnp.int32, shape, 0)
        dcs_acc = jnp.zeros(shape, F32)
        dcs_rows = jnp.zeros(shape, F32)
        dxs, dbs, dcs_parts, dprevs, prod_a, prod_b, prod_c, prod_e = [], [], [], [], [], [], [], []
        for g in range(SSM_GROUPS):
            gs = slice(gw * g, gw * (g + 1))
            bg = xv[:, D_SSM + n * g:D_SSM + n * (g + 1)]
            cg = xv[:, D_SSM + n * (SSM_GROUPS + g):D_SSM + n * (SSM_GROUPS + g + 1)]
            cb = _dot_nt(cg, bg)
            cbt = _dot_nt(bg, cg)
            xs_g, dy_g, xd_g, dye_g, xdd_g = xs_all[:, gs], dyv[:, gs], xd_all[:, gs], dye_all[:, gs], xdd_all[:, gs]
            prev_g, dsn_g = psv[:, gs], dst[:, gs]
            cprev_g = _dot(cg, prev_g)
            dprevs.append(dsn_g * cd_x[:, gs] + _dot(cg.T, dye_g))
            dcg = _dot_nt(dye_g, prev_g)
            dxdd_g = _dot(bg, dsn_g)
            dbg = _dot_nt(xdd_g, dsn_g)
            dxd_g = dxdd_g * ds_x[:, gs]
            prod_a.append(dy_g * cprev_g)
            prod_b.append(dxdd_g * xd_g)
            prod_e.append(jnp.sum(dsn_g * prev_g, axis=0, keepdims=True))
            dcb = jnp.zeros((ell, ell), F32)
            for r in range(rpg):
                h = g * rpg + r
                mine = (lane_g >= p * r) & (lane_g < p * (r + 1))
                lmat = jnp.exp(jnp.where(tril, cs[:, h:h + 1] - cst[h:h + 1, :], -jnp.inf))
                lmat_t = jnp.exp(jnp.where(triu, cst[h:h + 1, :] - cs[:, h:h + 1], -jnp.inf))
                dgm = _dot_nt(jnp.where(mine, dy_g, 0.0), xd_g)
                dxd_g = dxd_g + jnp.where(mine, _dot(cbt * lmat_t, dy_g), 0.0)
                mm = dgm * (cb * lmat)
                dcs_acc = dcs_acc + jnp.where(lane == h, jnp.sum(mm, axis=1, keepdims=True), 0.0)
                dcs_rows = dcs_rows + jnp.where(sub == h, jnp.sum(mm, axis=0, keepdims=True), 0.0)
                dcb = dcb + dgm * lmat
            dxs.append(dxd_g * dt_x[:, gs] + dy_g * dskip_x[:, gs])
            prod_c.append(dxd_g * xs_g)
            dbs.append(dbg + _dot_tn(dcb, cg))
            dcs_parts.append(dcg + _dot(dcb, bg))
        dx_ref[...] = jnp.concatenate(dxs + dbs + dcs_parts, axis=1)
        dstate[...] = jnp.concatenate(dprevs, axis=1)
        sum_a = gather(jnp.concatenate(prod_a, axis=1))
        sum_b = gather(jnp.concatenate(prod_b, axis=1))
        sum_c = gather(jnp.concatenate(prod_c, axis=1))
        sum_d = gather(dyv * xs_all)
        dcd = gather(jnp.broadcast_to(jnp.concatenate(prod_e, axis=1), (8, D_SSM)))[0:1, :]
        tmp = sum_b * ds
        dlast = dcd * cd + jnp.sum(tmp, axis=0, keepdims=True)
        dcs = dcs_acc + sum_a * e - tmp - dcs_rows.T + jnp.where(sub == ell - 1, dlast, 0.0)
        dad = jnp.dot(triu.astype(F32), dcs, precision=HIGHEST, preferred_element_type=F32)
        ddt = sum_c + dad * a
        dalog_ref[...] += jnp.sum(dad * dt, axis=0, keepdims=True) * a
        ddskip_ref[...] += jnp.sum(sum_d, axis=0, keepdims=True)
        ddraw = jnp.where(lane < SSM_HEADS, ddt * _sigmoid(pre), 0.0)
        ddt_ref[...] = ddraw.astype(ddt_ref.dtype)
        dbias_ref[...] += jnp.sum(ddraw, axis=0, keepdims=True)

    vec = pl.BlockSpec((1, LANES), lambda c: (0, 0))
    rev = lambda c: nc - 1 - c
    ex = _head_expand()
    outs = pl.pallas_call(
        body, grid=(nc,),
        in_specs=[pl.BlockSpec((ell, CONV_CH), lambda c: (rev(c), 0)),
                  pl.BlockSpec((ell, LANES), lambda c: (rev(c), P_DT // LANES)), vec, vec, vec,
                  pl.BlockSpec((1, n, D_SSM), lambda c: (rev(c), 0, 0)),
                  pl.BlockSpec((ell, D_SSM), lambda c: (rev(c), 0)),
                  pl.BlockSpec((LANES, D_SSM), lambda c: (0, 0)), pl.BlockSpec((D_SSM, LANES), lambda c: (0, 0))],
        out_specs=(pl.BlockSpec((ell, CONV_CH), lambda c: (rev(c), 0)), pl.BlockSpec((ell, LANES), lambda c: (rev(c), 0)),
                   vec, vec, vec),
        out_shape=(SDS((s, CONV_CH), F32), SDS((s, LANES), BF16), SDS((1, LANES), F32), SDS((1, LANES), F32),
                   SDS((1, LANES), F32)),
        scratch_shapes=[pltpu.VMEM((n, D_SSM), F32)], name=name,
        compiler_params=_params(("arbitrary",), 40 << 20))(xbc, proj, dt_bias, a_log, d_skip, prev_states, dy, ex, ex.T)
    return outs


def _rope_swap(t):
    lane = lax.broadcasted_iota(jnp.int32, t.shape, 1)
    half = QK_ROPE // 2
    lo = (lane >= QK_NOPE) & (lane < QK_NOPE + half)
    hi = (lane >= QK_NOPE + half) & (lane < QK_NOPE + QK_ROPE)
    return jnp.where(lo, pltpu.roll(t, HEAD_PAD - half, axis=1), jnp.where(hi, pltpu.roll(t, half, axis=1), 0.0))


def _mla_prep(q, kv, proj, cos, sins, *, name):
    s = q.shape[0]
    tm = min(s, 256)
    scale = (QK_NOPE + QK_ROPE) ** -0.5

    def body(q_ref, kv_ref, kr_ref, cos_ref, sin_ref, qo_ref, ko_ref, vo_ref):
        cosv, sinv = cos_ref[...], sin_ref[...]
        kr = pltpu.roll(kr_ref[...], QK_NOPE, axis=1)
        lane = lax.broadcasted_iota(jnp.int32, kr.shape, 1)
        nope = lane < QK_NOPE
        kr = jnp.where(nope, 0.0, kr)
        kpe = kr * cosv + _rope_swap(kr) * sinv
        for hp in range(MLA_HEADS // 2):
            vs = []
            for h in (2 * hp, 2 * hp + 1):
                hs = slice(HEAD_PAD * h, HEAD_PAD * (h + 1))
                qh = q_ref[:, hs]
                kvh = kv_ref[:, hs]
                qo_ref[:, hs] = ((qh * cosv + _rope_swap(qh) * sinv) * scale).astype(qo_ref.dtype)
                ko_ref[:, hs] = (jnp.where(nope, kvh, 0.0) + kpe).astype(ko_ref.dtype)
                vs.append(kvh[:, QK_NOPE:])
            vo_ref[:, 2 * V_DIM * hp:2 * V_DIM * (hp + 1)] = jnp.concatenate(vs, axis=1).astype(vo_ref.dtype)

    wide = pl.BlockSpec((tm, MLA_HEADS * HEAD_PAD), lambda i: (i, 0))
    half = pl.BlockSpec((tm, MLA_HEADS * V_DIM), lambda i: (i, 0))
    tab = pl.BlockSpec((tm, LANES), lambda i: (i, 0))
    return pl.pallas_call(
        body, grid=(s // tm,),
        in_specs=[wide, wide, pl.BlockSpec((tm, LANES), lambda i: (i, P_KR // LANES)), tab, tab],
        out_specs=(wide, wide, half),
        out_shape=(SDS((s, MLA_HEADS * HEAD_PAD), BF16), SDS((s, MLA_HEADS * HEAD_PAD), BF16),
                   SDS((s, MLA_HEADS * V_DIM), BF16)), name=name,
        compiler_params=_params(("parallel",), 32 << 20))(q, kv, proj, cos, sins)


def _mla_prep_bwd(dqr, dkr, dv, cos, sins, *, name):
    s = dqr.shape[0]
    tm = min(s, 256)
    scale = (QK_NOPE + QK_ROPE) ** -0.5

    def body(dq_ref, dk_ref, dv_ref, cos_ref, sin_ref, dqo_ref, dkv_ref, dkr_ref):
        cosv, sinv = cos_ref[...], sin_ref[...]
        lane = lax.broadcasted_iota(jnp.int32, cosv.shape, 1)
        ksum = jnp.zeros(cosv.shape, F32)
        for h in range(MLA_HEADS):
            hs = slice(HEAD_PAD * h, HEAD_PAD * (h + 1))
            d = dq_ref[:, hs]
            dk = dk_ref[:, hs]
            dqo_ref[:, hs] = ((d * cosv + _rope_swap(d * sinv)) * scale).astype(dqo_ref.dtype)
            dkv_ref[:, hs] = jnp.concatenate([dk[:, :QK_NOPE], dv_ref[:, V_DIM * h:V_DIM * (h + 1)]], axis=1).astype(dkv_ref.dtype)
            ksum = ksum + dk
        ksum = jnp.where((lane >= QK_NOPE) & (lane < QK_NOPE + QK_ROPE), ksum, 0.0)
        un = ksum * cosv + _rope_swap(ksum * sinv)
        dkr_ref[...] = pltpu.roll(un, HEAD_PAD - QK_NOPE, axis=1).astype(dkr_ref.dtype)

    wide = pl.BlockSpec((tm, MLA_HEADS * HEAD_PAD), lambda i: (i, 0))
    half = pl.BlockSpec((tm, MLA_HEADS * V_DIM), lambda i: (i, 0))
    tab = pl.BlockSpec((tm, LANES), lambda i: (i, 0))
    return pl.pallas_call(
        body, grid=(s // tm,), in_specs=[wide, wide, half, tab, tab], out_specs=(wide, wide, tab),
        out_shape=(SDS((s, MLA_HEADS * HEAD_PAD), BF16), SDS((s, MLA_HEADS * HEAD_PAD), BF16), SDS((s, LANES), BF16)),
        name=name, compiler_params=_params(("parallel",), 40 << 20))(dqr, dkr, dv, cos, sins)


FLASH_TILE = 512
FLASH_ROWS = 32


def _flash_fwd(q, k, v, *, name):
    s = q.shape[0]
    t = min(s, FLASH_TILE)
    nq = s // t
    npair = MLA_HEADS // 2

    def body(q_ref, k_ref, v_ref, o_ref, lse_ref):
        i = pl.program_id(1)
        qs = [q_ref[:, HEAD_PAD * e:HEAD_PAD * (e + 1)] for e in range(2)]
        diag = lax.broadcasted_iota(jnp.int32, (t, t), 0) >= lax.broadcasted_iota(jnp.int32, (t, t), 1)

        def step(j, carry, masked):
            rows = pl.ds(pl.multiple_of(j * t, t), t)
            new = []
            for e in range(2):
                m, l, acc = carry[e]
                sc = _dot_nt(qs[e], k_ref[rows, HEAD_PAD * e:HEAD_PAD * (e + 1)])
                if masked:
                    sc = jnp.where(diag, sc, NEG)
                m_new = jnp.maximum(m, jnp.max(sc, axis=1, keepdims=True))
                pr = jnp.exp(sc - m_new)
                alpha = jnp.exp(m - m_new)
                l = alpha * l + jnp.sum(pr, axis=1, keepdims=True)
                acc = alpha * acc + _dot(pr, v_ref[rows, V_DIM * e:V_DIM * (e + 1)])
                new.append((m_new, l, acc))
            return tuple(new)

        init = tuple((jnp.full((t, 1), NEG, F32), jnp.zeros((t, 1), F32), jnp.zeros((t, V_DIM), F32)) for _ in range(2))
        carry = lax.fori_loop(0, i, functools.partial(step, masked=False), init)
        carry = step(i, carry, True)
        o_ref[...] = jnp.concatenate([acc / l for _, l, acc in carry], axis=1)
        lse_ref[0] = jnp.concatenate([jnp.broadcast_to(m + jnp.log(l), (t, V_DIM)) for m, l, _ in carry], axis=1)

    return pl.pallas_call(
        body, grid=(npair, nq),
        in_specs=[pl.BlockSpec((t, 2 * HEAD_PAD), lambda hp, i: (i, hp)), pl.BlockSpec((s, 2 * HEAD_PAD), lambda hp, i: (0, hp)),
                  pl.BlockSpec((s, 2 * V_DIM), lambda hp, i: (0, hp))],
        out_specs=(pl.BlockSpec((t, 2 * V_DIM), lambda hp, i: (i, hp)), pl.BlockSpec((1, t, LANES), lambda hp, i: (hp, i, 0))),
        out_shape=(SDS((s, MLA_HEADS * V_DIM), F32), SDS((npair, s, LANES), F32)), name=name,
        compiler_params=_params(("parallel", "parallel"), 40 << 20))(q, k, v)


def _flash_bwd(q, k, v, o, lse, do, *, name):
    s = q.shape[0]
    t = min(s, FLASH_TILE)
    nq = s // t
    npair = MLA_HEADS // 2
    nchunk = t // FLASH_ROWS

    def valid_cols(r):
        return min(t, -(-((r + 1) * FLASH_ROWS) // LANES) * LANES)

    def body(q_ref, k_ref, v_ref, o_ref, lse_ref, do_ref, dq_ref, dk_ref, dv_ref, s_scr, dp_scr, p_scr, ds_scr, dk_acc, dv_acc):
        j = pl.program_id(1)

        @pl.when(j == 0)
        def _():
            dq_ref[...] = jnp.zeros_like(dq_ref)

        dk_acc[...] = jnp.zeros(dk_acc.shape, F32)
        dv_acc[...] = jnp.zeros(dv_acc.shape, F32)
        qsl = [slice(HEAD_PAD * e, HEAD_PAD * (e + 1)) for e in range(2)]
        vsl = [slice(V_DIM * e, V_DIM * (e + 1)) for e in range(2)]

        def step(i, carry, masked):
            rows = pl.ds(pl.multiple_of(i * t, t), t)
            for e in range(2):
                ke = k_ref[:, qsl[e]]
                qi = q_ref[rows, qsl[e]]
                doi = do_ref[rows, vsl[e]]
                delta = jnp.sum(doi * o_ref[rows, vsl[e]], axis=1, keepdims=True)
                lse_i = lse_ref[0, rows, vsl[e]][:, 0:1]
                dob = doi.astype(BF16)
                s_scr[e] = _dot_nt(qi, ke)
                dp_scr[e] = _dot_nt(dob, v_ref[:, vsl[e]])
                for r in range(nchunk):
                    rs = slice(r * FLASH_ROWS, (r + 1) * FLASH_ROWS)
                    width = valid_cols(r) if masked else t
                    sc = s_scr[e, rs, 0:width]
                    if masked:
                        row = r * FLASH_ROWS + lax.broadcasted_iota(jnp.int32, (FLASH_ROWS, width), 0)
                        sc = jnp.where(row >= lax.broadcasted_iota(jnp.int32, (FLASH_ROWS, width), 1), sc, NEG)
                    pr = jnp.exp(sc - lse_i[rs, :])
                    dsc = pr * (dp_scr[e, rs, 0:width] - delta[rs, :])
                    p_scr[e, rs, 0:width] = pr.astype(BF16)
                    ds_scr[e, rs, 0:width] = dsc.astype(BF16)
                    if width < t:
                        p_scr[e, rs, width:t] = jnp.zeros((FLASH_ROWS, t - width), BF16)
                        ds_scr[e, rs, width:t] = jnp.zeros((FLASH_ROWS, t - width), BF16)
                dv_acc[e] += _dot_tn(p_scr[e], dob)
                dk_acc[e] += _dot_tn(ds_scr[e], qi)
                dq_ref[rows, qsl[e]] += _dot(ds_scr[e], ke)
            return carry

        step(j, 0, True)
        lax.fori_loop(j + 1, nq, functools.partial(step, masked=False), 0)
        dk_ref[...] = jnp.concatenate([dk_acc[e] for e in range(2)], axis=1)
        dv_ref[...] = jnp.concatenate([dv_acc[e] for e in range(2)], axis=1)

    full_q = pl.BlockSpec((s, 2 * HEAD_PAD), lambda hp, j: (0, hp))
    full_v = pl.BlockSpec((s, 2 * V_DIM), lambda hp, j: (0, hp))
    blk_k = pl.BlockSpec((t, 2 * HEAD_PAD), lambda hp, j: (j, hp))
    blk_v = pl.BlockSpec((t, 2 * V_DIM), lambda hp, j: (j, hp))
    return pl.pallas_call(
        body, grid=(npair, nq),
        in_specs=[full_q, blk_k, blk_v, full_v, pl.BlockSpec((1, s, LANES), lambda hp, j: (hp, 0, 0)), full_v],
        out_specs=(full_q, blk_k, blk_v),
        out_shape=(SDS((s, MLA_HEADS * HEAD_PAD), F32), SDS((s, MLA_HEADS * HEAD_PAD), F32), SDS((s, MLA_HEADS * V_DIM), F32)),
        scratch_shapes=[pltpu.VMEM((2, t, t), F32), pltpu.VMEM((2, t, t), F32), pltpu.VMEM((2, t, t), BF16),
                        pltpu.VMEM((2, t, t), BF16), pltpu.VMEM((2, t, HEAD_PAD), F32), pltpu.VMEM((2, t, V_DIM), F32)],
        name=name, compiler_params=_params(("parallel", "arbitrary"), 48 << 20))(q, k, v, o, lse, do)


def _mem_attn_fwd(q, k, v, *, name):
    s = q.shape[0]
    tm = min(s, 512)
    ml = k.shape[0]
    scale = MEM_HEAD_DIM ** -0.5

    def body(q_ref, k_ref, v_ref, o_ref):
        for h in range(MEM_HEADS):
            hs = slice(MEM_HEAD_DIM * h, MEM_HEAD_DIM * (h + 1))
            sc = _dot_nt(q_ref[:, hs], k_ref[:, hs]) * scale
            pr = jnp.exp(sc - jnp.max(sc, axis=1, keepdims=True))
            pr = pr / jnp.sum(pr, axis=1, keepdims=True)
            o_ref[:, hs] = _dot(pr, v_ref[:, hs]).astype(o_ref.dtype)

    blk = pl.BlockSpec((tm, D_MODEL), lambda i: (i, 0))
    kv = pl.BlockSpec((ml, D_MODEL), lambda i: (0, 0))
    return pl.pallas_call(body, grid=(s // tm,), in_specs=[blk, kv, kv], out_specs=blk,
                          out_shape=SDS((s, D_MODEL), BF16), name=name,
                          compiler_params=_params(("parallel",), 24 << 20))(q, k, v)


def _mem_attn_bwd(q, k, v, do, *, name):
    s = q.shape[0]
    tm = min(s, 512)
    ml = k.shape[0]
    scale = MEM_HEAD_DIM ** -0.5

    def body(q_ref, k_ref, v_ref, do_ref, dq_ref, dk_ref, dv_ref):
        @pl.when(pl.program_id(0) == 0)
        def _():
            dk_ref[...] = jnp.zeros_like(dk_ref)
            dv_ref[...] = jnp.zeros_like(dv_ref)

        for h in range(MEM_HEADS):
            hs = slice(MEM_HEAD_DIM * h, MEM_HEAD_DIM * (h + 1))
            qh, kh, vh, doh = q_ref[:, hs], k_ref[:, hs], v_ref[:, hs], do_ref[:, hs]
            sc = _dot_nt(qh, kh) * scale
            pr = jnp.exp(sc - jnp.max(sc, axis=1, keepdims=True))
            pr = pr / jnp.sum(pr, axis=1, keepdims=True)
            dp = _dot_nt(doh, vh)
            dsc = pr * (dp - jnp.sum(pr * dp, axis=1, keepdims=True)) * scale
            dq_ref[:, hs] = _dot(dsc, kh).astype(dq_ref.dtype)
            dk_ref[:, hs] += _dot_tn(dsc, qh)
            dv_ref[:, hs] += _dot_tn(pr, doh)

    blk = pl.BlockSpec((tm, D_MODEL), lambda i: (i, 0))
    kv = pl.BlockSpec((ml, D_MODEL), lambda i: (0, 0))
    return pl.pallas_call(body, grid=(s // tm,), in_specs=[blk, kv, kv, blk], out_specs=(blk, kv, kv),
                          out_shape=(SDS((s, D_MODEL), BF16), SDS((ml, D_MODEL), F32), SDS((ml, D_MODEL), F32)), name=name,
                          compiler_params=_params(("arbitrary",), 32 << 20))(q, k, v, do)


MATS = (("w_in", (1024, 940), 1), ("w_uq", (384, 384), 1), ("w_ukv", (256, 512), 1), ("w_out", (512, 1024), 0),
        ("ssm_conv_w", (4, 512), 1),
        ("w_mq", (256, 1024), 0), ("w_mk", (256, 1024), 0), ("w_mv", (256, 1024), 0), ("w_mo", (256, 1024), 0),
        ("w_up", (1024, 1408), 1), ("w_down", (704, 1024), 0), ("ffn_conv_w", (3, 1408), 1))
GROUPS = {"proj": (0,), "mixer": (1, 2, 3, 4), "mem": (5, 6, 7, 8), "ffn": (9, 10, 11)}
UP_SHARD_COLS = 1408
F32_ON_WIRE = ("ssm_conv_w", "ffn_conv_w")
SMALL = (("norm_mix", 1024), ("ssm_conv_b", 2048), ("dt_bias", 16), ("a_log", 16), ("d_skip", 16), ("ssm_norm", 1024),
         ("q_norm", 384), ("kv_norm", 256), ("attn_out_norm", 1024), ("norm_mem_q", 1024), ("norm_mem_kv", 1024),
         ("norm_ffn", 1024), ("ffn_conv_b", 5632))
PACK_COLS = 1024


def _pad_cols(t, n):
    return jnp.pad(t, ((0, 0),) * (t.ndim - 1) + ((0, n - t.shape[-1]),))


def _w_in_to_padded(t):
    z, xbc, dt, cq, ckv, kr = jnp.split(t, (1024, 3072, 3088, 3472, 3728), axis=-1)
    return jnp.concatenate([xbc, z, cq, _pad_cols(dt, LANES), ckv, _pad_cols(kr, P_IN - P_KR)], axis=-1)


def _w_in_from_padded(t):
    return jnp.concatenate([t[..., P_Z:P_Z + 1024], t[..., P_XBC:P_XBC + 2048], t[..., P_DT:P_DT + SSM_HEADS],
                            t[..., P_CQ:P_CQ + Q_LORA], t[..., P_CKV:P_CKV + KV_LORA], t[..., P_KR:P_KR + QK_ROPE]], axis=-1)


def _cols_joined(g):
    return jnp.concatenate([g[j] for j in range(N_CHIPS)], axis=-1)


def _cols_by_chip(t, dtype):
    k = t.shape[0]
    return t.reshape(k, N_CHIPS, -1).transpose(1, 0, 2).astype(dtype)


def _rows_by_chip(t):
    return t.reshape(N_CHIPS, -1, t.shape[-1])


def _mixer_weights(gw):
    wl = {}
    uq = _cols_joined(gw["w_uq"]).reshape(Q_LORA, MLA_HEADS, QK_NOPE + QK_ROPE)
    wl["w_uq"] = _pad_cols(uq, HEAD_PAD).reshape(Q_LORA, MLA_HEADS * HEAD_PAD)
    wl["w_ukv"] = _cols_joined(gw["w_ukv"])
    wl["ssm_conv_w"] = _cols_joined(gw["ssm_conv_w"])
    return wl


def _layer_fwd(x0, mem, cos, sins, weights, sp, li):
    n = lambda t: f"l{li}_{t}"
    lead = ()
    sv = {"x0": x0}
    gw = dict(weights("proj", x0))
    w_in = _w_in_to_padded(_cols_joined(gw["w_in"]))
    h = _rms_fwd(x0, sp["norm_mix"], name=n("mix_norm"))
    in_hbm = lambda t: pltpu.with_memory_space_constraint(t, pltpu.HBM)
    proj = in_hbm(_mm(h, w_in, name=n("mix_proj")))
    gw.update(weights("mixer", proj))
    wl = dict(_mixer_weights(gw), w_in=w_in)
    xbc = in_hbm(_ssm_conv_fwd(proj, wl["ssm_conv_w"], sp["ssm_conv_b"], name=n("ssm_conv")))
    y, pstates = _ssd_fwd(xbc, proj, sp["dt_bias"], sp["a_log"], sp["d_skip"], name=n("ssd"))
    y_ssm = _gated_rms_fwd(y, proj, sp["ssm_norm"], name=n("ssm_gate"))
    cqn = _rms_fwd(proj, sp["q_norm"], col=(Q_LORA, P_CQ // Q_LORA), name=n("q_norm"))
    ckvn = _rms_fwd(proj, sp["kv_norm"], col=(KV_LORA, P_CKV // KV_LORA), name=n("kv_norm"))
    q = in_hbm(_mm(cqn, wl["w_uq"], name=n("uq")))
    kv = in_hbm(_mm(ckvn, wl["w_ukv"], name=n("ukv")))
    qr, kr, v = _mla_prep(q, kv, proj, cos, sins, name=n("rope"))
    att, lse = _flash_fwd(qr, kr, v, name=n("flash"))
    y_att = _rms_fwd(att, sp["attn_out_norm"], name=n("att_norm"))
    x1 = _mm(y_ssm, gw["w_out"], b_lead=lead, b_rows=(0, D_SSM), res=x0, name=n("out_a"))
    x1 = _mm(y_att, gw["w_out"], b_lead=lead, b_rows=(D_SSM, D_SSM), res=x1, name=n("out_b"))
    sv.update(h=h, proj=proj, xbc=xbc, y=y, pstates=pstates, y_ssm=y_ssm, cqn=cqn, ckvn=ckvn, qr=qr, kr=kr, v=v,
              att=att, lse=lse, y_att=y_att, x1=x1)
    gw.update(weights("mem", x1))
    hq = _rms_fwd(x1, sp["norm_mem_q"], name=n("memq_norm"))
    hm = _rms_fwd(mem, sp["norm_mem_kv"], name=n("memkv_norm"))
    mq = _mm(hq, gw["w_mq"], b_lead=lead, out_dtype=BF16, name=n("mq"))
    mk = _mm(hm, gw["w_mk"], b_lead=lead, out_dtype=BF16, name=n("mk"))
    mv = _mm(hm, gw["w_mv"], b_lead=lead, out_dtype=BF16, name=n("mv"))
    mo = _mem_attn_fwd(mq, mk, mv, name=n("mem_attn"))
    x2 = _mm(mo, gw["w_mo"], b_lead=lead, res=x1, name=n("mo"))
    sv.update(hq=hq, hm=hm, mq=mq, mk=mk, mv=mv, mo=mo, x2=x2)
    gw.update(weights("ffn", x2))
    wl["ffn_conv_w"] = _cols_joined(gw["ffn_conv_w"])
    hf = _rms_fwd(x2, sp["norm_ffn"], name=n("ffn_norm"))
    up_g = _mm(hf, gw["w_up"], b_lead=lead, b_chips=(0, 2), name=n("up_g"))
    up_v = _mm(hf, gw["w_up"], b_lead=lead, b_chips=(2, 2), name=n("up_v"))
    act = _ffn_conv_fwd(up_g, up_v, wl["ffn_conv_w"], sp["ffn_conv_b"], name=n("ffn_conv"))
    x3 = _mm(act, gw["w_down"], b_lead=lead, res=x2, name=n("down"))
    sv.update(hf=hf, up_g=up_g, up_v=up_v, act=act)
    return x3, sv, gw, wl


def _layer_bwd(dx3, dx3b, mem, cos, sins, gw, wl, sp, sv, li, emit):
    n = lambda t: f"l{li}_b_{t}"
    lead = ()
    g = {}

    def after(token, v):
        return v if token is None else v + token[0, 0]

    dact = _mm(dx3b, gw["w_down"], tb=True, b_lead=lead, out_dtype=BF16, name=n("down_dx"))
    g["w_down"] = _rows_by_chip(_mm(sv["act"], dx3b, ta=True, out_dtype=BF16, name=n("down_dw")))
    dup_g, dup_v, dcw, g["ffn_conv_b"] = _ffn_conv_bwd(
        sv["up_g"], sv["up_v"], wl["ffn_conv_w"], sp["ffn_conv_b"], dact, name=n("ffn_conv"))
    g["ffn_conv_w"] = _cols_by_chip(dcw, F32)
    nsh = UP_SHARD_COLS
    dhf = None
    for c4 in range(N_CHIPS):
        dhf = _mm(dup_g if c4 < 2 else dup_v, gw["w_up"], tb=True, a_col=(nsh, c4 % 2), b_lead=(c4,), res=dhf,
                  name=n(f"up{c4}_dx"))
    g["w_up"] = jnp.concatenate([_mm(sv["hf"], dup_g, ta=True, o_chips=nsh, out_dtype=BF16, name=n("upg_dw")),
                                 _mm(sv["hf"], dup_v, ta=True, o_chips=nsh, out_dtype=BF16, name=n("upv_dw"))], axis=0)
    dx2, dx2b, g["norm_ffn"] = _rms_bwd(sv["x2"], after(emit("ffn", g), sp["norm_ffn"]), dhf, dx3, name=n("ffn_norm"))
    dmo = _mm(dx2b, gw["w_mo"], tb=True, b_lead=lead, out_dtype=BF16, name=n("mo_dx"))
    g["w_mo"] = _rows_by_chip(_mm(sv["mo"], dx2b, ta=True, out_dtype=BF16, name=n("mo_dw")))
    dmq, dmk, dmv = _mem_attn_bwd(sv["mq"], sv["mk"], sv["mv"], dmo, name=n("mem_attn"))
    dhq = _mm(dmq, gw["w_mq"], tb=True, b_lead=lead, name=n("mq_dx"))
    g["w_mq"] = _rows_by_chip(_mm(sv["hq"], dmq, ta=True, out_dtype=BF16, name=n("mq_dw")))
    dhm = _mm(dmk, gw["w_mk"], tb=True, b_lead=lead, name=n("mk_dx"))
    dhm = _mm(dmv, gw["w_mv"], tb=True, b_lead=lead, res=dhm, name=n("mv_dx"))
    g["w_mk"] = _rows_by_chip(_mm(sv["hm"], dmk, ta=True, out_dtype=BF16, name=n("mk_dw")))
    g["w_mv"] = _rows_by_chip(_mm(sv["hm"], dmv, ta=True, out_dtype=BF16, name=n("mv_dw")))
    dx1, dx1b, g["norm_mem_q"] = _rms_bwd(sv["x1"], after(emit("mem", g), sp["norm_mem_q"]), dhq, dx2, name=n("memq_norm"))
    _, _, g["norm_mem_kv"] = _rms_bwd(mem, sp["norm_mem_kv"], dhm, name=n("memkv_norm"))
    dy_ssm = _mm(dx1b, gw["w_out"], tb=True, b_lead=lead, b_rows=(0, D_SSM), name=n("outa_dx"))
    dy_att = _mm(dx1b, gw["w_out"], tb=True, b_lead=lead, b_rows=(D_SSM, D_SSM), name=n("outb_dx"))
    g["w_out"] = _rows_by_chip(jnp.concatenate([_mm(sv["y_ssm"], dx1b, ta=True, out_dtype=BF16, name=n("outa_dw")),
                                                _mm(sv["y_att"], dx1b, ta=True, out_dtype=BF16, name=n("outb_dw"))], axis=0))
    datt, _, g["attn_out_norm"] = _rms_bwd(sv["att"], sp["attn_out_norm"], dy_att, name=n("att_norm"))
    dqr, dkr, dv = _flash_bwd(sv["qr"], sv["kr"], sv["v"], sv["att"], sv["lse"], datt, name=n("flash"))
    dq, dkv, dkrope = _mla_prep_bwd(dqr, dkr, dv, cos, sins, name=n("rope"))
    duq = _mm(sv["cqn"], dq, ta=True, name=n("uq_dw")).reshape(Q_LORA, MLA_HEADS, HEAD_PAD)[..., :QK_NOPE + QK_ROPE]
    g["w_uq"] = _cols_by_chip(duq.reshape(Q_LORA, -1), BF16)
    dcqn = _mm(dq, wl["w_uq"], tb=True, name=n("uq_dx"))
    g["w_ukv"] = _cols_by_chip(_mm(sv["ckvn"], dkv, ta=True, name=n("ukv_dw")), BF16)
    dckvn = _mm(dkv, wl["w_ukv"], tb=True, name=n("ukv_dx"))
    proj = sv["proj"]
    _, dcq, g["q_norm"] = _rms_bwd(proj, sp["q_norm"], dcqn, col=(Q_LORA, P_CQ // Q_LORA), name=n("q_norm"))
    _, dckv, g["kv_norm"] = _rms_bwd(proj, sp["kv_norm"], dckvn, col=(KV_LORA, P_CKV // KV_LORA), name=n("kv_norm"))
    dy, dz, g["ssm_norm"] = _gated_rms_bwd(sv["y"], proj, sp["ssm_norm"], dy_ssm, name=n("ssm_gate"))
    dxbc, ddt, da_log, dd_skip, ddt_bias = _ssd_bwd(
        sv["xbc"], proj, sp["dt_bias"], sp["a_log"], sp["d_skip"], sv["pstates"], dy, name=n("ssd"))
    g["a_log"], g["d_skip"], g["dt_bias"] = da_log[0, :SSM_HEADS], dd_skip[0, :SSM_HEADS], ddt_bias[0, :SSM_HEADS]
    dxbc_pre, dsw, g["ssm_conv_b"] = _ssm_conv_bwd(proj, wl["ssm_conv_w"], sp["ssm_conv_b"], dxbc, name=n("ssm_conv"))
    g["ssm_conv_w"] = _cols_by_chip(dsw, F32)
    started = emit("mixer", g)
    s = proj.shape[0]
    dproj = jnp.concatenate([dxbc_pre, dz, dcq, ddt, dckv, dkrope,
                             jnp.zeros((s, P_IN - P_KR - LANES), BF16)], axis=1)
    dh = _mm(dproj, wl["w_in"], tb=True, name=n("proj_dx"))
    g["w_in"] = _cols_by_chip(_w_in_from_padded(_mm(sv["h"], dproj, ta=True, name=n("proj_dw"))), BF16)
    dx0, dx0b, g["norm_mix"] = _rms_bwd(sv["x0"], after(started, sp["norm_mix"]), dh, dx1, name=n("mix_norm"))
    return dx0, dx0b, g, emit("proj", g)


def _chip_peers(x, y):
    return [(1 - x, y), (x, 1 - y), (1 - x, 1 - y)]


HBM_SPEC = pl.BlockSpec(memory_space=pltpu.HBM)
SEM_SPEC = pl.BlockSpec(memory_space=pltpu.SEMAPHORE)
ANY_SPEC = pl.BlockSpec(memory_space=pl.ANY)
VMEM_SPEC = pl.BlockSpec(memory_space=pltpu.VMEM)
DATAFLOW = pltpu.SideEffectType.DATAFLOW_SIDE_EFFECTING
TOKEN_SHAPE = (8, LANES)


def _exchange_start(srcs, land_shapes, src_view, dst_view, token, *, name):
    n = len(srcs)

    def body(*refs):
        s, l, tok_in = refs[:n], refs[n:2 * n], refs[2 * n]
        send_sems, recv_sems = refs[2 * n + 1], refs[2 * n + 2]
        tok_out = refs[-1]
        x, y, c = lax.axis_index("x"), lax.axis_index("y"), lax.axis_index("c")
        me = 2 * x + y
        for t in range(n):
            for k, (px, py) in enumerate(_chip_peers(x, y)):
                pltpu.make_async_remote_copy(
                    src_ref=src_view(t, s[t], 2 * px + py), dst_ref=dst_view(t, l[t], me), send_sem=send_sems.at[3 * t + k],
                    recv_sem=recv_sems.at[3 * t + k], device_id=(px, py, c), device_id_type=MESH).start()
            pltpu.make_async_copy(src_view(t, s[t], me), dst_view(t, l[t], me), send_sems.at[3 * n + t]).start()
        tok_out[...] = tok_in[...]

    hbm = lambda t: pltpu.with_memory_space_constraint(t, pltpu.HBM)
    lands = [lax.empty(l.shape, l.dtype) for l in land_shapes]
    outs = pl.pallas_call(
        body, name=name,
        out_shape=(pltpu.SemaphoreType.DMA((4 * n,)), pltpu.SemaphoreType.DMA((3 * n,)),
                   *[pltpu.HBM(l.shape, l.dtype) for l in land_shapes], SDS(TOKEN_SHAPE, F32)),
        in_specs=[HBM_SPEC] * (2 * n) + [VMEM_SPEC], out_specs=(SEM_SPEC, SEM_SPEC, *[HBM_SPEC] * n, VMEM_SPEC),
        input_output_aliases={n + t: 2 + t for t in range(n)},
        compiler_params=pltpu.CompilerParams(has_side_effects=DATAFLOW))(*[hbm(t) for t in srcs], *[hbm(t) for t in lands], token)
    return outs[0], outs[1], list(outs[2:2 + n]), outs[-1]


def _exchange_wait(srcs, lands, send_sems, recv_sems, after, src_view, dst_view, which, *, name):
    n = len(srcs)
    m = len(which)

    def body(*refs):
        s, l = refs[:m], refs[m:2 * m]
        send_ref, recv_ref = refs[2 * m], refs[2 * m + 1]
        x, y, c = lax.axis_index("x"), lax.axis_index("y"), lax.axis_index("c")
        me = 2 * x + y
        for i, t in enumerate(which):
            for k, (px, py) in enumerate(_chip_peers(x, y)):
                chip = 2 * px + py
                cp = pltpu.make_async_remote_copy(
                    src_ref=src_view(t, s[i], chip), dst_ref=dst_view(t, l[i], chip), send_sem=send_ref.at[3 * t + k],
                    recv_sem=recv_ref.at[3 * t + k], device_id=(px, py, c), device_id_type=MESH)
                cp.wait_send()
                cp.wait_recv()
            pltpu.make_async_copy(src_view(t, s[i], me), dst_view(t, l[i], me), send_ref.at[3 * n + t]).wait()

    outs = pl.pallas_call(
        body, name=name, out_shape=[pltpu.HBM(lands[t].shape, lands[t].dtype) for t in which],
        in_specs=[HBM_SPEC] * (2 * m) + [SEM_SPEC, SEM_SPEC, ANY_SPEC], out_specs=[HBM_SPEC] * m,
        input_output_aliases={m + i: i for i in range(m)},
        compiler_params=pltpu.CompilerParams(has_side_effects=DATAFLOW))(
            *[srcs[t] for t in which], *[lands[t] for t in which], send_sems, recv_sems, after)
    return list(outs)


def _gather_layer_start(shards, li, token, tag=""):
    src_view = lambda t, ref, chip: ref.at[li]
    dst_view = lambda t, ref, chip: ref.at[chip]
    send_sems, recv_sems, lands, token = _exchange_start(
        shards, [SDS((N_CHIPS,) + s.shape[1:], s.dtype) for s in shards], src_view, dst_view, token,
        name=f"gather{li}{tag}_start")
    return (shards, lands, send_sems, recv_sems, src_view, dst_view, f"gather{li}{tag}"), token


def _scatter_start(grads, tag, token):
    view = lambda t, ref, chip: ref.at[chip]
    send_sems, recv_sems, lands, token = _exchange_start(
        grads, [SDS(g.shape, g.dtype) for g in grads], view, view, token, name=f"scatter{tag}_start")
    return (grads, lands, send_sems, recv_sems, view, view, f"scatter{tag}"), token


def _exchange_finish(handle, after, which=None, tag=""):
    srcs, lands, send_sems, recv_sems, src_view, dst_view, name = handle
    which = tuple(range(len(srcs))) if which is None else which
    return _exchange_wait(srcs, lands, send_sems, recv_sems, after, src_view, dst_view, which, name=f"{name}{tag}_wait")


def _swap_start(bufs, token, *, name):
    n = len(bufs)

    def body(*refs):
        s, l, tok_in = refs[:n], refs[n:2 * n], refs[2 * n]
        send_sems, recv_sems = refs[2 * n + 1], refs[2 * n + 2]
        x, y, c = lax.axis_index("x"), lax.axis_index("y"), lax.axis_index("c")
        for t in range(n):
            pltpu.make_async_remote_copy(src_ref=s[t], dst_ref=l[t], send_sem=send_sems.at[t], recv_sem=recv_sems.at[t],
                                         device_id=(x, y, 1 - c), device_id_type=MESH).start()
        refs[-1][...] = tok_in[...]

    hbm = lambda t: pltpu.with_memory_space_constraint(t, pltpu.HBM)
    lands = [lax.empty(b.shape, b.dtype) for b in bufs]
    outs = pl.pallas_call(
        body, name=f"{name}_start",
        out_shape=(pltpu.SemaphoreType.DMA((n,)), pltpu.SemaphoreType.DMA((n,)),
                   *[pltpu.HBM(b.shape, b.dtype) for b in bufs], SDS(TOKEN_SHAPE, F32)),
        in_specs=[HBM_SPEC] * (2 * n) + [VMEM_SPEC], out_specs=(SEM_SPEC, SEM_SPEC, *[HBM_SPEC] * n, VMEM_SPEC),
        input_output_aliases={n + t: 2 + t for t in range(n)},
        compiler_params=pltpu.CompilerParams(has_side_effects=DATAFLOW))(*[hbm(t) for t in bufs], *[hbm(t) for t in lands], token)
    return (bufs, list(outs[2:2 + n]), outs[0], outs[1], name), outs[-1]


def _swap_wait(handle, after):
    bufs, lands, send_sems, recv_sems, name = handle
    n = len(bufs)

    def body(*refs):
        s, l = refs[:n], refs[n:2 * n]
        send_ref, recv_ref = refs[2 * n], refs[2 * n + 1]
        x, y, c = lax.axis_index("x"), lax.axis_index("y"), lax.axis_index("c")
        for t in range(n):
            cp = pltpu.make_async_remote_copy(src_ref=s[t], dst_ref=l[t], send_sem=send_ref.at[t], recv_sem=recv_ref.at[t],
                                              device_id=(x, y, 1 - c), device_id_type=MESH)
            cp.wait_send()
            cp.wait_recv()

    outs = pl.pallas_call(
        body, name=f"{name}_wait", out_shape=[pltpu.HBM(b.shape, b.dtype) for b in bufs],
        in_specs=[HBM_SPEC] * (2 * n) + [SEM_SPEC, SEM_SPEC, ANY_SPEC], out_specs=[HBM_SPEC] * n,
        input_output_aliases={n + t: t for t in range(n)},
        compiler_params=pltpu.CompilerParams(has_side_effects=DATAFLOW))(*bufs, *lands, send_sems, recv_sems, after)
    return list(outs)


def _all_gather8(src, *, name):
    def body(src_ref, out_ref, send_sems, recv_sems, local_sem):
        x, y, c = lax.axis_index("x"), lax.axis_index("y"), lax.axis_index("c")
        me = 4 * x + 2 * y + c
        mine = pltpu.make_async_copy(src_ref, out_ref.at[me], local_sem)
        mine.start()

        def peer(k):
            return (x ^ (k >> 2 & 1), y ^ (k >> 1 & 1), c ^ (k & 1))

        sends = []
        for k in range(1, N_DEV):
            cp = pltpu.make_async_remote_copy(src_ref=src_ref, dst_ref=out_ref.at[me], send_sem=send_sems.at[k - 1],
                                              recv_sem=recv_sems.at[k - 1], device_id=peer(k), device_id_type=MESH)
            cp.start()
            sends.append(cp)
        for k in range(1, N_DEV):
            px, py, pc = peer(k)
            pltpu.make_async_remote_copy(src_ref=src_ref, dst_ref=out_ref.at[4 * px + 2 * py + pc],
                                         send_sem=send_sems.at[k - 1], recv_sem=recv_sems.at[k - 1],
                                         device_id=peer(k), device_id_type=MESH).wait_recv()
        for cp in sends:
            cp.wait_send()
        mine.wait()

    any_spec = pl.BlockSpec(memory_space=pl.ANY)
    return pl.pallas_call(
        body, in_specs=[any_spec], out_specs=any_spec, out_shape=SDS((N_DEV,) + src.shape, src.dtype),
        scratch_shapes=[pltpu.SemaphoreType.DMA((N_DEV - 1,)), pltpu.SemaphoreType.DMA((N_DEV - 1,)), pltpu.SemaphoreType.DMA],
        name=name)(src)


def _adam_terms(w, g, m, v):
    m = ADAM_B1 * m + (1.0 - ADAM_B1) * g
    v = ADAM_B2 * v + (1.0 - ADAM_B2) * (g * g)
    m_hat = m / (1.0 - ADAM_B1 ** ADAM_STEP)
    v_hat = v / (1.0 - ADAM_B2 ** ADAM_STEP)
    delta = -ADAM_LR * (m_hat / (jnp.sqrt(v_hat) + ADAM_EPS) + ADAM_WD * w)
    return delta, m, v


def _adamw_shard(mine, other, w, m, v, *, name):
    d, a, b = w.shape
    tr = next((t for t in (128, 64, 32, 16) if a % t == 0), a)

    def body(*refs):
        ga, gb = refs[:d], refs[d:2 * d]
        w_ref, m_ref, v_ref, g_ref, d_ref, nm_ref, nv_ref = refs[2 * d:]

        def plane(ref):
            return ((ref[0].astype(F32) + ref[1].astype(F32)) + ref[2].astype(F32)) + ref[3].astype(F32)

        for lp in range(d):
            @pl.when(pl.program_id(0) == lp)
            def _(lp=lp):
                g = plane(ga[lp]) + plane(gb[lp])
                delta, mn, vn = _adam_terms(w_ref[...], g, m_ref[...], v_ref[...])
                g_ref[...] = g
                d_ref[...] = delta
                nm_ref[...] = mn
                nv_ref[...] = vn

    gspecs = [pl.BlockSpec((N_CHIPS, tr, b), lambda l, i, lp=lp: (0, jnp.where(l == lp, i, 0), 0)) for lp in range(d)]
    blk = pl.BlockSpec((None, tr, b), lambda l, i: (l, i, 0))
    shp = SDS((d, a, b), F32)
    return pl.pallas_call(
        body, grid=(d, a // tr), in_specs=gspecs + gspecs + [blk, blk, blk], out_specs=(blk,) * 4, out_shape=(shp,) * 4,
        name=name, compiler_params=_params(("arbitrary", "arbitrary"), 48 << 20))(*mine, *other, w, m, v)


def _adamw_small(g8, w, m, v, *, name):
    n = w.shape[1]

    def body(g8_ref, w_ref, m_ref, v_ref, g_ref, d_ref, nm_ref, nv_ref):
        g = g8_ref[0]
        for k in range(1, N_DEV):
            g = g + g8_ref[k]
        delta, mn, vn = _adam_terms(w_ref[...], g, m_ref[...], v_ref[...])
        g_ref[...] = g
        d_ref[...] = delta
        nm_ref[...] = mn
        nv_ref[...] = vn

    shp = SDS((1, n), F32)
    return pl.pallas_call(body, out_shape=(shp,) * 4, name=name, compiler_params=_params(None, 24 << 20))(g8, w, m, v)


def _rope_tables(positions):
    inv_freq = 1.0 / (ROPE_THETA ** (jnp.arange(0, QK_ROPE, 2, dtype=F32) / QK_ROPE))
    ang = positions.astype(F32)[:, None] * inv_freq
    c, s = jnp.cos(ang), jnp.sin(ang)
    n = positions.shape[0]
    pad = jnp.zeros((n, HEAD_PAD - QK_NOPE - QK_ROPE), F32)
    cos = jnp.concatenate([jnp.ones((n, QK_NOPE), F32), c, c, pad], axis=1)
    sins = jnp.concatenate([jnp.zeros((n, QK_NOPE), F32), -s, s, pad], axis=1)
    return cos, sins


def _pad_lanes(v):
    return _pad_cols(v.reshape(1, -1), LANES)


def _local_step(x, mem, positions, weights, small, final_norm, loss_target, emit, token):
    cos, sins = _rope_tables(positions)
    saved, gws, wls, sps = [], [], [], []
    h = x
    for li in range(DEPTH):
        sp = {k: small[k][li] for k, _ in SMALL}
        if li == 0:
            sp["norm_mix"] = sp["norm_mix"] + token[0, 0]
        for k in ("dt_bias", "a_log", "d_skip"):
            sp[k] = _pad_lanes(sp[k])
        h, sv, gw, wl = _layer_fwd(h, mem, cos, sins, functools.partial(weights, li), sp, li)
        saved.append(sv)
        gws.append(gw)
        wls.append(wl)
        sps.append(sp)
    loss, dh, dhb, g_final = _final_loss(h, final_norm, loss_target, name="final_loss")
    grads = [None] * DEPTH
    started = None
    for li in reversed(range(DEPTH)):
        sp = sps[li]
        if started is not None:
            sp = dict(sp, ffn_conv_b=sp["ffn_conv_b"] + started[0, 0])
        dh, dhb, grads[li], started = _layer_bwd(dh, dhb, mem, cos, sins, gws[li], wls[li], sp, saved[li], li,
                                                 functools.partial(emit, li))
    return loss, dh, grads, g_final


def _gathered_views(which, lands):
    return {MATS[t][0]: (b.reshape(-1, b.shape[-1]) if MATS[t][2] == 0 else b) for t, b in zip(which, lands)}


def kernel(x, mem, positions, norm_mix, w_in, ssm_conv_w, ssm_conv_b, dt_bias, a_log, d_skip, ssm_norm, q_norm, w_uq, kv_norm, w_ukv, attn_out_norm, w_out, norm_mem_q, norm_mem_kv, w_mq, w_mk, w_mv, w_mo, norm_ffn, w_up, ffn_conv_w, ffn_conv_b, w_down, final_norm, loss_target, m_norm_mix, m_w_in, m_ssm_conv_w, m_ssm_conv_b, m_dt_bias, m_a_log, m_d_skip, m_ssm_norm, m_q_norm, m_w_uq, m_kv_norm, m_w_ukv, m_attn_out_norm, m_w_out, m_norm_mem_q, m_norm_mem_kv, m_w_mq, m_w_mk, m_w_mv, m_w_mo, m_norm_ffn, m_w_up, m_ffn_conv_w, m_ffn_conv_b, m_w_down, m_final_norm, v_norm_mix, v_w_in, v_ssm_conv_w, v_ssm_conv_b, v_dt_bias, v_a_log, v_d_skip, v_ssm_norm, v_q_norm, v_w_uq, v_kv_norm, v_w_ukv, v_attn_out_norm, v_w_out, v_norm_mem_q, v_norm_mem_kv, v_w_mq, v_w_mk, v_w_mv, v_w_mo, v_norm_ffn, v_w_up, v_ffn_conv_w, v_ffn_conv_b, v_w_down, v_final_norm):
    args = dict(locals())
    names = ["norm_mix", "w_in", "ssm_conv_w", "ssm_conv_b", "dt_bias", "a_log", "d_skip", "ssm_norm", "q_norm", "w_uq",
             "kv_norm", "w_ukv", "attn_out_norm", "w_out", "norm_mem_q", "norm_mem_kv", "w_mq", "w_mk", "w_mv", "w_mo",
             "norm_ffn", "w_up", "ffn_conv_w", "ffn_conv_b", "w_down", "final_norm"]
    wts = {k: args[k] for k in names}
    mom = {k: args["m_" + k] for k in names}
    var = {k: args["v_" + k] for k in names}
    mat_names = [k for k, _, _ in MATS]

    shards = [wts[k] if k in F32_ON_WIRE else wts[k].astype(BF16) for k in mat_names]
    token = jnp.zeros(TOKEN_SHAPE, F32)
    first, token = _gather_layer_start(shards[:1], 0, token, tag="_first")
    gathers = []
    for li in range(DEPTH):
        handle, token = _gather_layer_start(shards[1:] if li == 0 else shards, li, token)
        gathers.append(handle)
    small = {k: wts[k] for k, _ in SMALL}

    def weights(li, group, after):
        which = GROUPS[group]
        if li > 0:
            return _gathered_views(which, _exchange_finish(gathers[li], after, which, tag=f"_{group}"))
        if group == "proj":
            return _gathered_views(which, _exchange_finish(first, after))
        return _gathered_views(which, _exchange_finish(gathers[0], after, tuple(t - 1 for t in which), tag=f"_{group}"))

    scatters = [[] for _ in range(DEPTH)]
    nm = len(mat_names)
    mine = [[None] * nm for _ in range(DEPTH)]
    swaps = [None] * DEPTH
    last_started = [None]

    def swap_layer(li, after):
        for which, handle in scatters[li]:
            for t, b in zip(which, _exchange_finish(handle, after)):
                mine[li][t] = b
        swaps[li], started = _swap_start(mine[li], jnp.zeros(TOKEN_SHAPE, F32), name=f"swap{li}")
        return started

    def emit(li, group, g):
        last = group == "proj"
        if li == 0:
            which = GROUPS[group]
        elif last:
            which = tuple(range(nm))
        else:
            return None
        handle, started = _scatter_start([g[MATS[t][0]] for t in which], f"{li}_{group}", jnp.zeros(TOKEN_SHAPE, F32))
        scatters[li].append((which, handle))
        last_started[0] = started
        if li + 1 < DEPTH and group == ("mixer" if li == 0 else "proj"):
            started = started + swap_layer(li + 1, g["ssm_conv_w"])
        return started

    loss, grad_x, grads, g_final = _local_step(x[0], mem[0], positions[0], weights, small, wts["final_norm"],
                                               loss_target[0], emit, token)
    loss = lax.psum(loss, ("x", "y", "c"))

    w_in_t = GROUPS["proj"][0]
    others = [t for t in range(nm) if t != w_in_t]
    for which, handle in scatters[0]:
        if which != GROUPS["proj"]:
            for t, b in zip(which, _exchange_finish(handle, grad_x)):
                mine[0][t] = b
    swaps[0], _ = _swap_start([mine[0][t] for t in others], last_started[0], name="swap0")
    other = [dict(zip(others, _swap_wait(swaps[0], grad_x)))] + [
        dict(enumerate(_swap_wait(swaps[li], grad_x))) for li in range(1, DEPTH)]

    def adamw(t):
        k = mat_names[t]
        return _adamw_shard([mine[li][t] for li in range(DEPTH)], [other[li][t] for li in range(DEPTH)],
                            wts[k], mom[k], var[k], name=f"adamw_{k}")

    mat_out = {mat_names[t]: adamw(t) for t in others}
    done = sum(mat_out[mat_names[t]][0][0, 0, :1] for t in others)
    (last_handle,) = [handle for which, handle in scatters[0] if which == GROUPS["proj"]]
    (mine[0][w_in_t],) = _exchange_finish(last_handle, done)
    last_swap, _ = _swap_start([mine[0][w_in_t]], jnp.zeros(TOKEN_SHAPE, F32), name="swap0_last")
    (other[0][w_in_t],) = _swap_wait(last_swap, done)
    mat_out[mat_names[w_in_t]] = adamw(w_in_t)

    def pack_small(get, fin):
        flat = [get(k).reshape(-1) for k, _ in SMALL] + [fin.reshape(-1)]
        n = sum(f.shape[0] for f in flat)
        return jnp.concatenate(flat + [jnp.zeros((-n % PACK_COLS,), F32)]).reshape(1, -1)

    gs = pack_small(lambda k: jnp.stack([grads[li][k] for li in range(DEPTH)]), g_final)
    g8 = _all_gather8(gs, name="gather_small_grads")
    small_out = _adamw_small(g8, pack_small(wts.get, wts["final_norm"]), pack_small(mom.get, mom["final_norm"]),
                             pack_small(var.get, var["final_norm"]), name="adamw_small")

    def unpack_small(buf):
        out, off = {}, 0
        for k, nel in SMALL:
            out[k] = buf[0, off:off + DEPTH * nel].reshape(DEPTH, nel)
            off += DEPTH * nel
        out["final_norm"] = buf[0, off:off + D_MODEL]
        return out

    small_res = [unpack_small(b) for b in small_out]
    res = []
    for kind in range(4):
        for k in names:
            res.append(small_res[kind][k] if k in small_res[kind] else mat_out[k][kind])
    return (loss, grad_x[None], *res)
```

```python
import functools
import math

import jax
import jax.numpy as jnp
from jax import lax
from jax.experimental import pallas as pl
from jax.experimental.pallas import tpu as pltpu

F32 = jnp.float32
BF16 = jnp.bfloat16
HIGHEST = lax.Precision.HIGHEST
SDS = jax.ShapeDtypeStruct
MESH = pl.DeviceIdType.MESH

D_MODEL = 1024
DEPTH = 4
EPS = 1e-6
SSM_HEADS = 16
SSM_HEAD_DIM = 64
D_SSM = 1024
SSM_GROUPS = 4
SSM_STATE = 128
SSM_CONV = 4
SSM_CHUNK = 128
CONV_CH = 2048
MLA_HEADS = 16
QK_NOPE = 64
QK_ROPE = 32
V_DIM = 64
Q_LORA = 384
KV_LORA = 256
ROPE_THETA = 10000.0
MEM_HEADS = 4
MEM_HEAD_DIM = 256
D_FF = 2816
FFN_CONV = 3
D_IN = 3760
ADAM_LR = 0.001
ADAM_B1 = 0.9
ADAM_B2 = 0.999
ADAM_EPS = 1e-08
ADAM_WD = 0.01
ADAM_STEP = 10

LANES = 128
HEAD_PAD = 128
N_CHIPS = 4
N_DEV = 8
VMEM_CAP_MB = 56
MM_COL_BLOCK_BYTES = 2 << 20

P_XBC, P_Z, P_CQ, P_DT, P_CKV, P_KR, P_IN = 0, 2048, 3072, 3456, 3584, 3840, 4096
NEG = -1e30


def _tile(n, pref):
    t = (min(n, pref) // LANES) * LANES
    while t >= LANES:
        if n % t == 0:
            return t
        t -= LANES
    return n


def _params(sem=None, vmem_bytes=None):
    kw = {}
    if sem is not None:
        kw["dimension_semantics"] = sem
    if vmem_bytes is not None:
        kw["vmem_limit_bytes"] = int(min(max(vmem_bytes, 16 << 20), VMEM_CAP_MB << 20))
    return pltpu.CompilerParams(**kw)


def _nbytes(shape, dtype):
    return math.prod(shape) * jnp.dtype(dtype).itemsize


def _mm(a, b, *, ta=False, tb=False, res=None, out_dtype=F32, name, a_col=None, b_lead=(), b_rows=None,
        b_chips=None, o_chips=None):
    if ta:
        k, m = a.shape
    else:
        m, k = (a.shape[0], a.shape[1] if a_col is None else a_col[0])
    rows_b, cols_b = b.shape[-2:]
    row0 = 0
    if b_rows is not None:
        row0, rows_b = b_rows
    nlead = len(b_lead)
    if b_chips is not None:
        assert not tb
        kb, tn, n = rows_b, cols_b, b_chips[1] * cols_b
        b_blk = (None,) * (1 + nlead) + (kb, tn)
        b_map = lambda i, j: (b_chips[0] + j,) + tuple(b_lead) + (0, 0)
    elif tb:
        n, kb = rows_b, cols_b
        tn = _tile(n, 1024 if _nbytes((1024, kb), BF16) <= MM_COL_BLOCK_BYTES else 512)
        assert row0 % tn == 0
        b_blk = (None,) * nlead + (tn, kb)
        b_map = lambda i, j: tuple(b_lead) + (j + row0 // tn, 0)
    else:
        kb, n = rows_b, cols_b
        tn = o_chips if o_chips else _tile(n, 1024 if _nbytes((kb, 1024), BF16) <= MM_COL_BLOCK_BYTES else 512)
        assert row0 % kb == 0
        b_blk = (None,) * nlead + (kb, tn)
        b_map = lambda i, j: tuple(b_lead) + (row0 // kb, j)
    assert k == kb, (a.shape, b.shape, ta, tb, k, kb)
    tm = _tile(m, 1024)
    if ta:
        a_blk, a_map = (k, tm), (lambda i, j: (0, i))
    else:
        a_blk, a_map = (tm, k), ((lambda i, j: (i, 0)) if a_col is None else (lambda i, j: (i, a_col[1])))
    if o_chips:
        o_spec = pl.BlockSpec((None, tm, tn), lambda i, j: (j, i, 0))
        o_shape = SDS((n // tn, m, tn), out_dtype)
    else:
        o_spec = pl.BlockSpec((tm, tn), lambda i, j: (i, j))
        o_shape = SDS((m, n), out_dtype)
    dims = (((0 if ta else 1,), (1 if tb else 0,)), ((), ()))
    has_res = res is not None

    def body(*refs):
        a_ref, b_ref = refs[0], refs[1]
        o_ref = refs[-1]
        acc = lax.dot_general(a_ref[...].astype(BF16), b_ref[...].astype(BF16), dims, preferred_element_type=F32)
        if has_res:
            acc = acc + refs[2][...]
        o_ref[...] = acc.astype(o_ref.dtype)

    bb = tuple(d for d in b_blk if d is not None)
    vmem = 2 * (_nbytes(a_blk, a.dtype) + _nbytes(bb, b.dtype) + (2 if has_res else 1) * _nbytes((tm, tn), F32))
    vmem += _nbytes(a_blk, BF16) + _nbytes(bb, BF16) + 2 * _nbytes((tm, tn), F32) + (4 << 20)
    args = (a, b) + ((res,) if has_res else ())
    specs = [pl.BlockSpec(a_blk, a_map), pl.BlockSpec(b_blk, b_map)] + ([o_spec] if has_res else [])
    return pl.pallas_call(body, grid=(m // tm, n // tn), in_specs=specs, out_specs=o_spec, out_shape=o_shape, name=name,
                          compiler_params=_params(("parallel", "parallel"), vmem))(*args)


def _sigmoid(x):
    return 1.0 / (1.0 + jnp.exp(-x))


def _rms_fwd(x, g, *, col=None, name):
    s = x.shape[0]
    w, ci = (x.shape[1], 0) if col is None else col
    tm = min(s, 512)

    def body(x_ref, g_ref, o_ref):
        xv = x_ref[...].astype(F32)
        r = lax.rsqrt(jnp.mean(xv * xv, axis=-1, keepdims=True) + EPS)
        o_ref[...] = (xv * r * g_ref[...]).astype(o_ref.dtype)

    return pl.pallas_call(
        body, grid=(s // tm,),
        in_specs=[pl.BlockSpec((tm, w), lambda i: (i, ci)), pl.BlockSpec((1, w), lambda i: (0, 0))],
        out_specs=pl.BlockSpec((tm, w), lambda i: (i, 0)), out_shape=SDS((s, w), BF16), name=name,
        compiler_params=_params(("parallel",), 10 * tm * w * 4))(x, g.reshape(1, w))


def _rms_bwd(x, g, dy, dres=None, *, col=None, name):
    s = x.shape[0]
    w, ci = (x.shape[1], 0) if col is None else col
    tm = min(s, 512)
    has_res = dres is not None

    def body(*refs):
        x_ref, g_ref, dy_ref = refs[:3]
        dx_ref, dxb_ref, dg_ref = refs[-3:]
        xv = x_ref[...].astype(F32)
        dyv = dy_ref[...].astype(F32)
        r = lax.rsqrt(jnp.mean(xv * xv, axis=-1, keepdims=True) + EPS)
        u = dyv * g_ref[...]
        dx = r * u - xv * (r * r * r) * jnp.mean(xv * u, axis=-1, keepdims=True)
        if has_res:
            dx = dx + refs[3][...]
        dx_ref[...] = dx
        dxb_ref[...] = dx.astype(BF16)

        @pl.when(pl.program_id(0) == 0)
        def _():
            dg_ref[...] = jnp.zeros_like(dg_ref)

        dg_ref[...] += jnp.sum(dyv * xv * r, axis=0, keepdims=True)

    blk = pl.BlockSpec((tm, w), lambda i: (i, 0))
    specs = [pl.BlockSpec((tm, w), lambda i: (i, ci)), pl.BlockSpec((1, w), lambda i: (0, 0)), blk]
    args = [x, g.reshape(1, w), dy]
    if has_res:
        specs.append(blk)
        args.append(dres)
    dx, dxb, dg = pl.pallas_call(
        body, grid=(s // tm,), in_specs=specs,
        out_specs=(blk, blk, pl.BlockSpec((1, w), lambda i: (0, 0))),
        out_shape=(SDS((s, w), F32), SDS((s, w), BF16), SDS((1, w), F32)), name=name,
        compiler_params=_params(("arbitrary",), 18 * tm * w * 4))(*args)
    return dx, dxb, dg.reshape(w)


def _gated_rms_fwd(y, proj, g, *, name):
    s, w = y.shape
    tm = min(s, 512)

    def body(y_ref, z_ref, g_ref, o_ref):
        z = z_ref[...]
        t = y_ref[...] * (z * _sigmoid(z))
        r = lax.rsqrt(jnp.mean(t * t, axis=-1, keepdims=True) + EPS)
        o_ref[...] = (t * r * g_ref[...]).astype(o_ref.dtype)

    blk = pl.BlockSpec((tm, w), lambda i: (i, 0))
    return pl.pallas_call(
        body, grid=(s // tm,),
        in_specs=[blk, pl.BlockSpec((tm, w), lambda i: (i, P_Z // w)), pl.BlockSpec((1, w), lambda i: (0, 0))],
        out_specs=blk, out_shape=SDS((s, w), BF16), name=name,
        compiler_params=_params(("parallel",), 14 * tm * w * 4))(y, proj, g.reshape(1, w))


def _gated_rms_bwd(y, proj, g, dout, *, name):
    s, w = y.shape
    tm = min(s, 512)

    def body(y_ref, z_ref, g_ref, do_ref, dy_ref, dz_ref, dg_ref):
        z = z_ref[...]
        yv = y_ref[...]
        dov = do_ref[...]
        sg = _sigmoid(z)
        sz = z * sg
        t = yv * sz
        r = lax.rsqrt(jnp.mean(t * t, axis=-1, keepdims=True) + EPS)
        u = dov * g_ref[...]
        dt = r * u - t * (r * r * r) * jnp.mean(t * u, axis=-1, keepdims=True)
        dy_ref[...] = dt * sz
        dz_ref[...] = (dt * yv * (sg * (1.0 + z * (1.0 - sg)))).astype(dz_ref.dtype)

        @pl.when(pl.program_id(0) == 0)
        def _():
            dg_ref[...] = jnp.zeros_like(dg_ref)

        dg_ref[...] += jnp.sum(dov * t * r, axis=0, keepdims=True)

    blk = pl.BlockSpec((tm, w), lambda i: (i, 0))
    vec = pl.BlockSpec((1, w), lambda i: (0, 0))
    dy, dz, dg = pl.pallas_call(
        body, grid=(s // tm,),
        in_specs=[blk, pl.BlockSpec((tm, w), lambda i: (i, P_Z // w)), vec, blk],
        out_specs=(blk, blk, vec), out_shape=(SDS((s, w), F32), SDS((s, w), BF16), SDS((1, w), F32)), name=name,
        compiler_params=_params(("arbitrary",), 24 * tm * w * 4))(y, proj, g.reshape(1, w), dout)
    return dy, dz, dg.reshape(w)


def _final_loss(x, g, target, *, name):
    s, w = x.shape
    tm = min(s, 512)

    def body(x_ref, g_ref, t_ref, loss_ref, dx_ref, dxb_ref, dg_ref):
        xv = x_ref[...]
        gv = g_ref[...]
        r = lax.rsqrt(jnp.mean(xv * xv, axis=-1, keepdims=True) + EPS)
        xn = xv * r
        diff = xn * gv - t_ref[...]
        dy = diff * (1.0 / w)
        u = dy * gv
        dx = r * u - xv * (r * r * r) * jnp.mean(xv * u, axis=-1, keepdims=True)
        dx_ref[...] = dx
        dxb_ref[...] = dx.astype(BF16)

        @pl.when(pl.program_id(0) == 0)
        def _():
            dg_ref[...] = jnp.zeros_like(dg_ref)
            loss_ref[...] = jnp.zeros_like(loss_ref)

        dg_ref[...] += jnp.sum(dy * xn, axis=0, keepdims=True)
        part = jnp.sum(jnp.sum(diff * diff, axis=1, keepdims=True), axis=0, keepdims=True) * (0.5 / w)
        loss_ref[...] += jnp.broadcast_to(part, loss_ref.shape)

    blk = pl.BlockSpec((tm, w), lambda i: (i, 0))
    vec = pl.BlockSpec((1, w), lambda i: (0, 0))
    loss, dx, dxb, dg = pl.pallas_call(
        body, grid=(s // tm,), in_specs=[blk, vec, blk],
        out_specs=(pl.BlockSpec((1, LANES), lambda i: (0, 0)), blk, blk, vec),
        out_shape=(SDS((1, LANES), F32), SDS((s, w), F32), SDS((s, w), BF16), SDS((1, w), F32)), name=name,
        compiler_params=_params(("arbitrary",), 18 * tm * w * 4))(x, g.reshape(1, w), target)
    return loss[0, 0], dx, dxb, dg.reshape(w)


def _shift_down(x, k):
    if k == 0:
        return x
    row = lax.broadcasted_iota(jnp.int32, x.shape, 0)
    return jnp.where(row < k, 0.0, pltpu.roll(x, k, axis=0))


def _shift_up(x, k):
    if k == 0:
        return x
    s = x.shape[0]
    row = lax.broadcasted_iota(jnp.int32, x.shape, 0)
    return jnp.where(row >= s - k, 0.0, pltpu.roll(x, s - k, axis=0))


def _conv_pre(x, w, b, kw):
    pre = b
    for j in range(kw):
        pre = pre + w[j:j + 1, :] * _shift_down(x, kw - 1 - j)
    return pre


def _conv_bwd_terms(x, w, dpre, kw):
    dx = jnp.zeros_like(x)
    dws = []
    for j in range(kw):
        dx = dx + w[j:j + 1, :] * _shift_up(dpre, kw - 1 - j)
        dws.append(jnp.sum(dpre * _shift_down(x, kw - 1 - j), axis=0, keepdims=True))
    return dx, jnp.concatenate(dws, axis=0), jnp.sum(dpre, axis=0, keepdims=True)


def _ssm_conv_fwd(proj, w, b, *, name):
    s = proj.shape[0]
    cw = 256

    def body(x_ref, w_ref, b_ref, o_ref):
        pre = _conv_pre(x_ref[...], w_ref[...], b_ref[...], SSM_CONV)
        o_ref[...] = pre * _sigmoid(pre)

    return pl.pallas_call(
        body, grid=(CONV_CH // cw,),
        in_specs=[pl.BlockSpec((s, cw), lambda j: (0, j)), pl.BlockSpec((SSM_CONV, cw), lambda j: (0, j)),
                  pl.BlockSpec((1, cw), lambda j: (0, j))],
        out_specs=pl.BlockSpec((s, cw), lambda j: (0, j)), out_shape=SDS((s, CONV_CH), F32), name=name,
        compiler_params=_params(("parallel",), 12 * s * cw * 4))(proj, w, b.reshape(1, CONV_CH))


def _ssm_conv_bwd(proj, w, b, dxbc, *, name):
    s = proj.shape[0]
    cw = 256

    def body(x_ref, w_ref, b_ref, dy_ref, dx_ref, dw_ref, db_ref):
        x = x_ref[...]
        wv = w_ref[...]
        pre = _conv_pre(x, wv, b_ref[...], SSM_CONV)
        sg = _sigmoid(pre)
        dpre = dy_ref[...] * (sg * (1.0 + pre * (1.0 - sg)))
        dx, dw, db = _conv_bwd_terms(x, wv, dpre, SSM_CONV)
        dx_ref[...] = dx.astype(dx_ref.dtype)
        dw_ref[...] = dw
        db_ref[...] = db

    col = pl.BlockSpec((s, cw), lambda j: (0, j))
    wsp = pl.BlockSpec((SSM_CONV, cw), lambda j: (0, j))
    bsp = pl.BlockSpec((1, cw), lambda j: (0, j))
    dx, dw, db = pl.pallas_call(
        body, grid=(CONV_CH // cw,), in_specs=[col, wsp, bsp, col], out_specs=(col, wsp, bsp),
        out_shape=(SDS((s, CONV_CH), BF16), SDS((SSM_CONV, CONV_CH), F32), SDS((1, CONV_CH), F32)), name=name,
        compiler_params=_params(("parallel",), 20 * s * cw * 4))(proj, w, b.reshape(1, CONV_CH), dxbc)
    return dx, dw, db.reshape(CONV_CH)


def _ffn_conv_fwd(up_g, up_v, w, b, *, name):
    s = up_g.shape[0]
    cw = 256
    nb = D_FF // cw

    def body(g_ref, v_ref, wg_ref, wv_ref, bg_ref, bv_ref, o_ref):
        gate = _conv_pre(g_ref[...], wg_ref[...], bg_ref[...], FFN_CONV)
        val = _conv_pre(v_ref[...], wv_ref[...], bv_ref[...], FFN_CONV)
        o_ref[...] = (gate * _sigmoid(gate) * val).astype(o_ref.dtype)

    col = pl.BlockSpec((s, cw), lambda j: (0, j))
    b2 = b.reshape(1, 2 * D_FF)
    return pl.pallas_call(
        body, grid=(nb,),
        in_specs=[col, col, pl.BlockSpec((FFN_CONV, cw), lambda j: (0, j)), pl.BlockSpec((FFN_CONV, cw), lambda j: (0, j + nb)),
                  pl.BlockSpec((1, cw), lambda j: (0, j)), pl.BlockSpec((1, cw), lambda j: (0, j + nb))],
        out_specs=col, out_shape=SDS((s, D_FF), BF16), name=name,
        compiler_params=_params(("parallel",), 16 * s * cw * 4))(up_g, up_v, w, w, b2, b2)


def _ffn_conv_bwd(up_g, up_v, w, b, dact, *, name):
    s = up_g.shape[0]
    cw = 256
    nb = D_FF // cw

    def body(g_ref, v_ref, wg_ref, wv_ref, bg_ref, bv_ref, da_ref, dg_ref, dv_ref, dwg_ref, dwv_ref, dbg_ref, dbv_ref):
        xg, xv = g_ref[...], v_ref[...]
        wg, wv = wg_ref[...], wv_ref[...]
        gate = _conv_pre(xg, wg, bg_ref[...], FFN_CONV)
        val = _conv_pre(xv, wv, bv_ref[...], FFN_CONV)
        da = da_ref[...].astype(F32)
        sg = _sigmoid(gate)
        dgate = da * val * (sg * (1.0 + gate * (1.0 - sg)))
        dval = da * gate * sg
        dxg, dwg, dbg = _conv_bwd_terms(xg, wg, dgate, FFN_CONV)
        dxv, dwv, dbv = _conv_bwd_terms(xv, wv, dval, FFN_CONV)
        dg_ref[...] = dxg.astype(dg_ref.dtype)
        dv_ref[...] = dxv.astype(dv_ref.dtype)
        dwg_ref[...] = dwg
        dwv_ref[...] = dwv
        dbg_ref[...] = dbg
        dbv_ref[...] = dbv

    col = pl.BlockSpec((s, cw), lambda j: (0, j))
    wsp = pl.BlockSpec((FFN_CONV, cw), lambda j: (0, j))
    bsp = pl.BlockSpec((1, cw), lambda j: (0, j))
    b2 = b.reshape(1, 2 * D_FF)
    dg, dv, dwg, dwv, dbg, dbv = pl.pallas_call(
        body, grid=(nb,),
        in_specs=[col, col, wsp, pl.BlockSpec((FFN_CONV, cw), lambda j: (0, j + nb)), bsp,
                  pl.BlockSpec((1, cw), lambda j: (0, j + nb)), col],
        out_specs=(col, col, wsp, wsp, bsp, bsp),
        out_shape=(SDS((s, D_FF), BF16), SDS((s, D_FF), BF16), SDS((FFN_CONV, D_FF), F32), SDS((FFN_CONV, D_FF), F32),
                   SDS((1, D_FF), F32), SDS((1, D_FF), F32)), name=name,
        compiler_params=_params(("parallel",), 32 * s * cw * 4))(up_g, up_v, w, w, b2, b2, dact)
    return dg, dv, jnp.concatenate([dwg, dwv], axis=1), jnp.concatenate([dbg, dbv], axis=1).reshape(2 * D_FF)


def _dot(a, b):
    return jnp.dot(a.astype(BF16), b.astype(BF16), preferred_element_type=F32)


def _dot_nt(a, b):
    return lax.dot_general(a.astype(BF16), b.astype(BF16), (((1,), (1,)), ((), ())), preferred_element_type=F32)


def _dot_tn(a, b):
    return lax.dot_general(a.astype(BF16), b.astype(BF16), (((0,), (0,)), ((), ())), preferred_element_type=F32)


def _ssd_chunk_terms(dtraw, bias, a_log):
    ell = dtraw.shape[0]
    lane = lax.broadcasted_iota(jnp.int32, dtraw.shape, 1)
    valid = lane < SSM_HEADS
    pre = dtraw + bias
    dt = jnp.where(valid, jnp.where(pre > 20.0, pre, jnp.log(1.0 + jnp.exp(jnp.minimum(pre, 20.0)))), 0.0)
    a = -jnp.exp(a_log)
    ad = dt * a
    row = lax.broadcasted_iota(jnp.int32, (ell, ell), 0)
    colm = lax.broadcasted_iota(jnp.int32, (ell, ell), 1)
    tril = row >= colm
    cs = jnp.dot(tril.astype(F32), ad, precision=HIGHEST, preferred_element_type=F32)
    cs_last = cs[ell - 1:ell, :]
    return pre, dt, a, cs, cs_last, tril


def _head_expand():
    h = lax.broadcasted_iota(jnp.int32, (LANES, D_SSM), 0)
    c = lax.broadcasted_iota(jnp.int32, (LANES, D_SSM), 1)
    return (c // SSM_HEAD_DIM == h).astype(F32)


def _ssd_fwd(xbc, proj, dt_bias, a_log, d_skip, *, name):
    s = xbc.shape[0]
    nc = s // SSM_CHUNK
    ell, n, p = SSM_CHUNK, SSM_STATE, SSM_HEAD_DIM
    rpg = SSM_HEADS // SSM_GROUPS
    gw = rpg * p

    def body(x_ref, dt_ref, bias_ref, alog_ref, dskip_ref, ex_ref, y_ref, ps_ref, state):
        @pl.when(pl.program_id(0) == 0)
        def _():
            state[...] = jnp.zeros_like(state)

        _, dt, _, cs, cs_last, tril = _ssd_chunk_terms(dt_ref[...], bias_ref[...], alog_ref[...])
        cst = cs.T
        ex = ex_ref[...]
        spread = lambda v: jnp.dot(v, ex, precision=HIGHEST, preferred_element_type=F32)
        dt_x, e_x, ds_x = spread(dt), spread(jnp.exp(cs)), spread(jnp.exp(cs_last - cs))
        cd_x = spread(jnp.broadcast_to(jnp.exp(cs_last), (8, LANES)))[0:1, :]
        dskip_x = spread(jnp.broadcast_to(dskip_ref[...], (8, LANES)))[0:1, :]
        st = state[...]
        ps_ref[0] = st
        xv = x_ref[...]
        xs_all = xv[:, 0:D_SSM]
        xd_all = xs_all * dt_x
        xdd_all = xd_all * ds_x
        lane_g = lax.broadcasted_iota(jnp.int32, (ell, gw), 1)
        ys, new = [], []
        for g in range(SSM_GROUPS):
            gs = slice(gw * g, gw * (g + 1))
            bg = xv[:, D_SSM + n * g:D_SSM + n * (g + 1)]
            cg = xv[:, D_SSM + n * (SSM_GROUPS + g):D_SSM + n * (SSM_GROUPS + g + 1)]
            cb = _dot_nt(cg, bg)
            xd_g, prev_g = xd_all[:, gs], st[:, gs]
            y_g = _dot(cg, prev_g) * e_x[:, gs] + xs_all[:, gs] * dskip_x[:, gs]
            for r in range(rpg):
                h = g * rpg + r
                lmat = jnp.exp(jnp.where(tril, cs[:, h:h + 1] - cst[h:h + 1, :], -jnp.inf))
                y_g = y_g + jnp.where((lane_g >= p * r) & (lane_g < p * (r + 1)), _dot(cb * lmat, xd_g), 0.0)
            ys.append(y_g)
            new.append(prev_g * cd_x[:, gs] + _dot(bg.T, xdd_all[:, gs]))
        y_ref[...] = jnp.concatenate(ys, axis=1)
        state[...] = jnp.concatenate(new, axis=1)

    vec = pl.BlockSpec((1, LANES), lambda c: (0, 0))
    return pl.pallas_call(
        body, grid=(nc,),
        in_specs=[pl.BlockSpec((ell, CONV_CH), lambda c: (c, 0)), pl.BlockSpec((ell, LANES), lambda c: (c, P_DT // LANES)),
                  vec, vec, vec, pl.BlockSpec((LANES, D_SSM), lambda c: (0, 0))],
        out_specs=(pl.BlockSpec((ell, D_SSM), lambda c: (c, 0)), pl.BlockSpec((1, n, D_SSM), lambda c: (c, 0, 0))),
        out_shape=(SDS((s, D_SSM), F32), SDS((nc, n, D_SSM), F32)),
        scratch_shapes=[pltpu.VMEM((n, D_SSM), F32)], name=name,
        compiler_params=_params(("arbitrary",), 32 << 20))(xbc, proj, dt_bias, a_log, d_skip, _head_expand())


def _ssd_bwd(xbc, proj, dt_bias, a_log, d_skip, prev_states, dy, *, name):
    s = xbc.shape[0]
    nc = s // SSM_CHUNK
    ell, n, p = SSM_CHUNK, SSM_STATE, SSM_HEAD_DIM
    rpg = SSM_HEADS // SSM_GROUPS
    gw = rpg * p

    def body(x_ref, dt_ref, bias_ref, alog_ref, dskip_ref, ps_ref, dy_ref, ex_ref, ext_ref,
             dx_ref, ddt_ref, dalog_ref, ddskip_ref, dbias_ref, dstate):
        @pl.when(pl.program_id(0) == 0)
        def _():
            dstate[...] = jnp.zeros_like(dstate)
            dalog_ref[...] = jnp.zeros_like(dalog_ref)
            ddskip_ref[...] = jnp.zeros_like(ddskip_ref)
            dbias_ref[...] = jnp.zeros_like(dbias_ref)

        pre, dt, a, cs, cs_last, tril = _ssd_chunk_terms(dt_ref[...], bias_ref[...], alog_ref[...])
        e = jnp.exp(cs)
        ds = jnp.exp(cs_last - cs)
        cd = jnp.exp(cs_last)
        cst = cs.T
        shape = (ell, LANES)
        ex, ext = ex_ref[...], ext_ref[...]
        spread = lambda v: jnp.dot(v, ex, precision=HIGHEST, preferred_element_type=F32)
        gather = lambda v: jnp.dot(v, ext, precision=HIGHEST, preferred_element_type=F32)
        dt_x, e_x, ds_x = spread(dt), spread(e), spread(ds)
        cd_x = spread(jnp.broadcast_to(cd, (8, LANES)))[0:1, :]
        dskip_x = spread(jnp.broadcast_to(dskip_ref[...], (8, LANES)))[0:1, :]
        xv, dyv, psv, dst = x_ref[...], dy_ref[...], ps_ref[0], dstate[...]
        xs_all = xv[:, 0:D_SSM]
        xd_all = xs_all * dt_x
        dye_all = dyv * e_x
        xdd_all = xd_all * ds_x
        triu = lax.broadcasted_iota(jnp.int32, (ell, ell), 0) <= lax.broadcasted_iota(jnp.int32, (ell, ell), 1)
        lane_g = lax.broadcasted_iota(jnp.int32, (ell, gw), 1)
        lane = lax.broadcasted_iota(jnp.int32, shape, 1)
        sub = lax.broadcasted_iota(jnp.int32, shape, 0)
        dcs_acc = jnp.zeros(shape, F32)
        dcs_rows = jnp.zeros(shape, F32)
        dxs, dbs, dcs_parts, dprevs, prod_a, prod_b, prod_c, prod_e = [], [], [], [], [], [], [], []
        for g in range(SSM_GROUPS):
            gs = slice(gw * g, gw * (g + 1))
            bg = xv[:, D_SSM + n * g:D_SSM + n * (g + 1)]
            cg = xv[:, D_SSM + n * (SSM_GROUPS + g):D_SSM + n * (SSM_GROUPS + g + 1)]
            cb = _dot_nt(cg, bg)
            cbt = _dot_nt(bg, cg)
            xs_g, dy_g, xd_g, dye_g, xdd_g = xs_all[:, gs], dyv[:, gs], xd_all[:, gs], dye_all[:, gs], xdd_all[:, gs]
            prev_g, dsn_g = psv[:, gs], dst[:, gs]
            cprev_g = _dot(cg, prev_g)
            dprevs.append(dsn_g * cd_x[:, gs] + _dot(cg.T, dye_g))
            dcg = _dot_nt(dye_g, prev_g)
            dxdd_g = _dot(bg, dsn_g)
            dbg = _dot_nt(xdd_g, dsn_g)
            dxd_g = dxdd_g * ds_x[:, gs]
            prod_a.append(dy_g * cprev_g)
            prod_b.append(dxdd_g * xd_g)
            prod_e.append(jnp.sum(dsn_g * prev_g, axis=0, keepdims=True))
            dcb = jnp.zeros((ell, ell), F32)
            for r in range(rpg):
                h = g * rpg + r
                mine = (lane_g >= p * r) & (lane_g < p * (r + 1))
                lmat = jnp.exp(jnp.where(tril, cs[:, h:h + 1] - cst[h:h + 1, :], -jnp.inf))
                lmat_t = jnp.exp(jnp.where(triu, cst[h:h + 1, :] - cs[:, h:h + 1], -jnp.inf))
                dgm = _dot_nt(jnp.where(mine, dy_g, 0.0), xd_g)
                dxd_g = dxd_g + jnp.where(mine, _dot(cbt * lmat_t, dy_g), 0.0)
                mm = dgm * (cb * lmat)
                dcs_acc = dcs_acc + jnp.where(lane == h, jnp.sum(mm, axis=1, keepdims=True), 0.0)
                dcs_rows = dcs_rows + jnp.where(sub == h, jnp.sum(mm, axis=0, keepdims=True), 0.0)
                dcb = dcb + dgm * lmat
            dxs.append(dxd_g * dt_x[:, gs] + dy_g * dskip_x[:, gs])
            prod_c.append(dxd_g * xs_g)
            dbs.append(dbg + _dot_tn(dcb, cg))
            dcs_parts.append(dcg + _dot(dcb, bg))
        dx_ref[...] = jnp.concatenate(dxs + dbs + dcs_parts, axis=1)
        dstate[...] = jnp.concatenate(dprevs, axis=1)
        sum_a = gather(jnp.concatenate(prod_a, axis=1))
        sum_b = gather(jnp.concatenate(prod_b, axis=1))
        sum_c = gather(jnp.concatenate(prod_c, axis=1))
        sum_d = gather(dyv * xs_all)
        dcd = gather(jnp.broadcast_to(jnp.concatenate(prod_e, axis=1), (8, D_SSM)))[0:1, :]
        tmp = sum_b * ds
        dlast = dcd * cd + jnp.sum(tmp, axis=0, keepdims=True)
        dcs = dcs_acc + sum_a * e - tmp - dcs_rows.T + jnp.where(sub == ell - 1, dlast, 0.0)
        dad = jnp.dot(triu.astype(F32), dcs, precision=HIGHEST, preferred_element_type=F32)
        ddt = sum_c + dad * a
        dalog_ref[...] += jnp.sum(dad * dt, axis=0, keepdims=True) * a
        ddskip_ref[...] += jnp.sum(sum_d, axis=0, keepdims=True)
        ddraw = jnp.where(lane < SSM_HEADS, ddt * _sigmoid(pre), 0.0)
        ddt_ref[...] = ddraw.astype(ddt_ref.dtype)
        dbias_ref[...] += jnp.sum(ddraw, axis=0, keepdims=True)

    vec = pl.BlockSpec((1, LANES), lambda c: (0, 0))
    rev = lambda c: nc - 1 - c
    ex = _head_expand()
    outs = pl.pallas_call(
        body, grid=(nc,),
        in_specs=[pl.BlockSpec((ell, CONV_CH), lambda c: (rev(c), 0)),
                  pl.BlockSpec((ell, LANES), lambda c: (rev(c), P_DT // LANES)), vec, vec, vec,
                  pl.BlockSpec((1, n, D_SSM), lambda c: (rev(c), 0, 0)),
                  pl.BlockSpec((ell, D_SSM), lambda c: (rev(c), 0)),
                  pl.BlockSpec((LANES, D_SSM), lambda c: (0, 0)), pl.BlockSpec((D_SSM, LANES), lambda c: (0, 0))],
        out_specs=(pl.BlockSpec((ell, CONV_CH), lambda c: (rev(c), 0)), pl.BlockSpec((ell, LANES), lambda c: (rev(c), 0)),
                   vec, vec, vec),
        out_shape=(SDS((s, CONV_CH), F32), SDS((s, LANES), BF16), SDS((1, LANES), F32), SDS((1, LANES), F32),
                   SDS((1, LANES), F32)),
        scratch_shapes=[pltpu.VMEM((n, D_SSM), F32)], name=name,
        compiler_params=_params(("arbitrary",), 40 << 20))(xbc, proj, dt_bias, a_log, d_skip, prev_states, dy, ex, ex.T)
    return outs


def _rope_swap(t):
    lane = lax.broadcasted_iota(jnp.int32, t.shape, 1)
    half = QK_ROPE // 2
    lo = (lane >= QK_NOPE) & (lane < QK_NOPE + half)
    hi = (lane >= QK_NOPE + half) & (lane < QK_NOPE + QK_ROPE)
    return jnp.where(lo, pltpu.roll(t, HEAD_PAD - half, axis=1), jnp.where(hi, pltpu.roll(t, half, axis=1), 0.0))


def _mla_prep(q, kv, proj, cos, sins, *, name):
    s = q.shape[0]
    tm = min(s, 256)
    scale = (QK_NOPE + QK_ROPE) ** -0.5

    def body(q_ref, kv_ref, kr_ref, cos_ref, sin_ref, qo_ref, ko_ref, vo_ref):
        cosv, sinv = cos_ref[...], sin_ref[...]
        kr = pltpu.roll(kr_ref[...], QK_NOPE, axis=1)
        lane = lax.broadcasted_iota(jnp.int32, kr.shape, 1)
        nope = lane < QK_NOPE
        kr = jnp.where(nope, 0.0, kr)
        kpe = kr * cosv + _rope_swap(kr) * sinv
        for hp in range(MLA_HEADS // 2):
            vs = []
            for h in (2 * hp, 2 * hp + 1):
                hs = slice(HEAD_PAD * h, HEAD_PAD * (h + 1))
                qh = q_ref[:, hs]
                kvh = kv_ref[:, hs]
                qo_ref[:, hs] = ((qh * cosv + _rope_swap(qh) * sinv) * scale).astype(qo_ref.dtype)
                ko_ref[:, hs] = (jnp.where(nope, kvh, 0.0) + kpe).astype(ko_ref.dtype)
                vs.append(kvh[:, QK_NOPE:])
            vo_ref[:, 2 * V_DIM * hp:2 * V_DIM * (hp + 1)] = jnp.concatenate(vs, axis=1).astype(vo_ref.dtype)

    wide = pl.BlockSpec((tm, MLA_HEADS * HEAD_PAD), lambda i: (i, 0))
    half = pl.BlockSpec((tm, MLA_HEADS * V_DIM), lambda i: (i, 0))
    tab = pl.BlockSpec((tm, LANES), lambda i: (i, 0))
    return pl.pallas_call(
        body, grid=(s // tm,),
        in_specs=[wide, wide, pl.BlockSpec((tm, LANES), lambda i: (i, P_KR // LANES)), tab, tab],
        out_specs=(wide, wide, half),
        out_shape=(SDS((s, MLA_HEADS * HEAD_PAD), BF16), SDS((s, MLA_HEADS * HEAD_PAD), BF16),
                   SDS((s, MLA_HEADS * V_DIM), BF16)), name=name,
        compiler_params=_params(("parallel",), 32 << 20))(q, kv, proj, cos, sins)


def _mla_prep_bwd(dqr, dkr, dv, cos, sins, *, name):
    s = dqr.shape[0]
    tm = min(s, 256)
    scale = (QK_NOPE + QK_ROPE) ** -0.5

    def body(dq_ref, dk_ref, dv_ref, cos_ref, sin_ref, dqo_ref, dkv_ref, dkr_ref):
        cosv, sinv = cos_ref[...], sin_ref[...]
        lane = lax.broadcasted_iota(jnp.int32, cosv.shape, 1)
        ksum = jnp.zeros(cosv.shape, F32)
        for h in range(MLA_HEADS):
            hs = slice(HEAD_PAD * h, HEAD_PAD * (h + 1))
            d = dq_ref[:, hs]
            dk = dk_ref[:, hs]
            dqo_ref[:, hs] = ((d * cosv + _rope_swap(d * sinv)) * scale).astype(dqo_ref.dtype)
            dkv_ref[:, hs] = jnp.concatenate([dk[:, :QK_NOPE], dv_ref[:, V_DIM * h:V_DIM * (h + 1)]], axis=1).astype(dkv_ref.dtype)
            ksum = ksum + dk
        ksum = jnp.where((lane >= QK_NOPE) & (lane < QK_NOPE + QK_ROPE), ksum, 0.0)
        un = ksum * cosv + _rope_swap(ksum * sinv)
        dkr_ref[...] = pltpu.roll(un, HEAD_PAD - QK_NOPE, axis=1).astype(dkr_ref.dtype)

    wide = pl.BlockSpec((tm, MLA_HEADS * HEAD_PAD), lambda i: (i, 0))
    half = pl.BlockSpec((tm, MLA_HEADS * V_DIM), lambda i: (i, 0))
    tab = pl.BlockSpec((tm, LANES), lambda i: (i, 0))
    return pl.pallas_call(
        body, grid=(s // tm,), in_specs=[wide, wide, half, tab, tab], out_specs=(wide, wide, tab),
        out_shape=(SDS((s, MLA_HEADS * HEAD_PAD), BF16), SDS((s, MLA_HEADS * HEAD_PAD), BF16), SDS((s, LANES), BF16)),
        name=name, compiler_params=_params(("parallel",), 40 << 20))(dqr, dkr, dv, cos, sins)


FLASH_TILE = 512
FLASH_ROWS = 32


def _flash_fwd(q, k, v, *, name):
    s = q.shape[0]
    t = min(s, FLASH_TILE)
    nq = s // t
    npair = MLA_HEADS // 2

    def body(q_ref, k_ref, v_ref, o_ref, lse_ref):
        i = pl.program_id(1)
        qs = [q_ref[:, HEAD_PAD * e:HEAD_PAD * (e + 1)] for e in range(2)]
        diag = lax.broadcasted_iota(jnp.int32, (t, t), 0) >= lax.broadcasted_iota(jnp.int32, (t, t), 1)

        def step(j, carry, masked):
            rows = pl.ds(pl.multiple_of(j * t, t), t)
            new = []
            for e in range(2):
                m, l, acc = carry[e]
                sc = _dot_nt(qs[e], k_ref[rows, HEAD_PAD * e:HEAD_PAD * (e + 1)])
                if masked:
                    sc = jnp.where(diag, sc, NEG)
                m_new = jnp.maximum(m, jnp.max(sc, axis=1, keepdims=True))
                pr = jnp.exp(sc - m_new)
                alpha = jnp.exp(m - m_new)
                l = alpha * l + jnp.sum(pr, axis=1, keepdims=True)
                acc = alpha * acc + _dot(pr, v_ref[rows, V_DIM * e:V_DIM * (e + 1)])
                new.append((m_new, l, acc))
            return tuple(new)

        init = tuple((jnp.full((t, 1), NEG, F32), jnp.zeros((t, 1), F32), jnp.zeros((t, V_DIM), F32)) for _ in range(2))
        carry = lax.fori_loop(0, i, functools.partial(step, masked=False), init)
        carry = step(i, carry, True)
        o_ref[...] = jnp.concatenate([acc / l for _, l, acc in carry], axis=1)
        lse_ref[0] = jnp.concatenate([jnp.broadcast_to(m + jnp.log(l), (t, V_DIM)) for m, l, _ in carry], axis=1)

    return pl.pallas_call(
        body, grid=(npair, nq),
        in_specs=[pl.BlockSpec((t, 2 * HEAD_PAD), lambda hp, i: (i, hp)), pl.BlockSpec((s, 2 * HEAD_PAD), lambda hp, i: (0, hp)),
                  pl.BlockSpec((s, 2 * V_DIM), lambda hp, i: (0, hp))],
        out_specs=(pl.BlockSpec((t, 2 * V_DIM), lambda hp, i: (i, hp)), pl.BlockSpec((1, t, LANES), lambda hp, i: (hp, i, 0))),
        out_shape=(SDS((s, MLA_HEADS * V_DIM), F32), SDS((npair, s, LANES), F32)), name=name,
        compiler_params=_params(("parallel", "parallel"), 40 << 20))(q, k, v)


def _flash_bwd(q, k, v, o, lse, do, *, name):
    s = q.shape[0]
    t = min(s, FLASH_TILE)
    nq = s // t
    npair = MLA_HEADS // 2
    nchunk = t // FLASH_ROWS

    def valid_cols(r):
        return min(t, -(-((r + 1) * FLASH_ROWS) // LANES) * LANES)

    def body(q_ref, k_ref, v_ref, o_ref, lse_ref, do_ref, dq_ref, dk_ref, dv_ref, s_scr, dp_scr, p_scr, ds_scr, dk_acc, dv_acc):
        j = pl.program_id(1)

        @pl.when(j == 0)
        def _():
            dq_ref[...] = jnp.zeros_like(dq_ref)

        dk_acc[...] = jnp.zeros(dk_acc.shape, F32)
        dv_acc[...] = jnp.zeros(dv_acc.shape, F32)
        qsl = [slice(HEAD_PAD * e, HEAD_PAD * (e + 1)) for e in range(2)]
        vsl = [slice(V_DIM * e, V_DIM * (e + 1)) for e in range(2)]

        def step(i, carry, masked):
            rows = pl.ds(pl.multiple_of(i * t, t), t)
            for e in range(2):
                ke = k_ref[:, qsl[e]]
                qi = q_ref[rows, qsl[e]]
                doi = do_ref[rows, vsl[e]]
                delta = jnp.sum(doi * o_ref[rows, vsl[e]], axis=1, keepdims=True)
                lse_i = lse_ref[0, rows, vsl[e]][:, 0:1]
                dob = doi.astype(BF16)
                s_scr[e] = _dot_nt(qi, ke)
                dp_scr[e] = _dot_nt(dob, v_ref[:, vsl[e]])
                for r in range(nchunk):
                    rs = slice(r * FLASH_ROWS, (r + 1) * FLASH_ROWS)
                    width = valid_cols(r) if masked else t
                    sc = s_scr[e, rs, 0:width]
                    if masked:
                        row = r * FLASH_ROWS + lax.broadcasted_iota(jnp.int32, (FLASH_ROWS, width), 0)
                        sc = jnp.where(row >= lax.broadcasted_iota(jnp.int32, (FLASH_ROWS, width), 1), sc, NEG)
                    pr = jnp.exp(sc - lse_i[rs, :])
                    dsc = pr * (dp_scr[e, rs, 0:width] - delta[rs, :])
                    p_scr[e, rs, 0:width] = pr.astype(BF16)
                    ds_scr[e, rs, 0:width] = dsc.astype(BF16)
                    if width < t:
                        p_scr[e, rs, width:t] = jnp.zeros((FLASH_ROWS, t - width), BF16)
                        ds_scr[e, rs, width:t] = jnp.zeros((FLASH_ROWS, t - width), BF16)
                dv_acc[e] += _dot_tn(p_scr[e], dob)
                dk_acc[e] += _dot_tn(ds_scr[e], qi)
                dq_ref[rows, qsl[e]] += _dot(ds_scr[e], ke)
            return carry

        step(j, 0, True)
        lax.fori_loop(j + 1, nq, functools.partial(step, masked=False), 0)
        dk_ref[...] = jnp.concatenate([dk_acc[e] for e in range(2)], axis=1)
        dv_ref[...] = jnp.concatenate([dv_acc[e] for e in range(2)], axis=1)

    full_q = pl.BlockSpec((s, 2 * HEAD_PAD), lambda hp, j: (0, hp))
    full_v = pl.BlockSpec((s, 2 * V_DIM), lambda hp, j: (0, hp))
    blk_k = pl.BlockSpec((t, 2 * HEAD_PAD), lambda hp, j: (j, hp))
    blk_v = pl.BlockSpec((t, 2 * V_DIM), lambda hp, j: (j, hp))
    return pl.pallas_call(
        body, grid=(npair, nq),
        in_specs=[full_q, blk_k, blk_v, full_v, pl.BlockSpec((1, s, LANES), lambda hp, j: (hp, 0, 0)), full_v],
        out_specs=(full_q, blk_k, blk_v),
        out_shape=(SDS((s, MLA_HEADS * HEAD_PAD), F32), SDS((s, MLA_HEADS * HEAD_PAD), F32), SDS((s, MLA_HEADS * V_DIM), F32)),
        scratch_shapes=[pltpu.VMEM((2, t, t), F32), pltpu.VMEM((2, t, t), F32), pltpu.VMEM((2, t, t), BF16),
                        pltpu.VMEM((2, t, t), BF16), pltpu.VMEM((2, t, HEAD_PAD), F32), pltpu.VMEM((2, t, V_DIM), F32)],
        name=name, compiler_params=_params(("parallel", "arbitrary"), 48 << 20))(q, k, v, o, lse, do)


def _mem_attn_fwd(q, k, v, *, name):
    s = q.shape[0]
    tm = min(s, 512)
    ml = k.shape[0]
    scale = MEM_HEAD_DIM ** -0.5

    def body(q_ref, k_ref, v_ref, o_ref):
        for h in range(MEM_HEADS):
            hs = slice(MEM_HEAD_DIM * h, MEM_HEAD_DIM * (h + 1))
            sc = _dot_nt(q_ref[:, hs], k_ref[:, hs]) * scale
            pr = jnp.exp(sc - jnp.max(sc, axis=1, keepdims=True))
            pr = pr / jnp.sum(pr, axis=1, keepdims=True)
            o_ref[:, hs] = _dot(pr, v_ref[:, hs]).astype(o_ref.dtype)

    blk = pl.BlockSpec((tm, D_MODEL), lambda i: (i, 0))
    kv = pl.BlockSpec((ml, D_MODEL), lambda i: (0, 0))
    return pl.pallas_call(body, grid=(s // tm,), in_specs=[blk, kv, kv], out_specs=blk,
                          out_shape=SDS((s, D_MODEL), BF16), name=name,
                          compiler_params=_params(("parallel",), 24 << 20))(q, k, v)


def _mem_attn_bwd(q, k, v, do, *, name):
    s = q.shape[0]
    tm = min(s, 512)
    ml = k.shape[0]
    scale = MEM_HEAD_DIM ** -0.5

    def body(q_ref, k_ref, v_ref, do_ref, dq_ref, dk_ref, dv_ref):
        @pl.when(pl.program_id(0) == 0)
        def _():
            dk_ref[...] = jnp.zeros_like(dk_ref)
            dv_ref[...] = jnp.zeros_like(dv_ref)

        for h in range(MEM_HEADS):
            hs = slice(MEM_HEAD_DIM * h, MEM_HEAD_DIM * (h + 1))
            qh, kh, vh, doh = q_ref[:, hs], k_ref[:, hs], v_ref[:, hs], do_ref[:, hs]
            sc = _dot_nt(qh, kh) * scale
            pr = jnp.exp(sc - jnp.max(sc, axis=1, keepdims=True))
            pr = pr / jnp.sum(pr, axis=1, keepdims=True)
            dp = _dot_nt(doh, vh)
            dsc = pr * (dp - jnp.sum(pr * dp, axis=1, keepdims=True)) * scale
            dq_ref[:, hs] = _dot(dsc, kh).astype(dq_ref.dtype)
            dk_ref[:, hs] += _dot_tn(dsc, qh)
            dv_ref[:, hs] += _dot_tn(pr, doh)

    blk = pl.BlockSpec((tm, D_MODEL), lambda i: (i, 0))
    kv = pl.BlockSpec((ml, D_MODEL), lambda i: (0, 0))
    return pl.pallas_call(body, grid=(s // tm,), in_specs=[blk, kv, kv, blk], out_specs=(blk, kv, kv),
                          out_shape=(SDS((s, D_MODEL), BF16), SDS((ml, D_MODEL), F32), SDS((ml, D_MODEL), F32)), name=name,
                          compiler_params=_params(("arbitrary",), 32 << 20))(q, k, v, do)


MATS = (("w_in", (1024, 940), 1), ("w_uq", (384, 384), 1), ("w_ukv", (256, 512), 1), ("w_out", (512, 1024), 0),
        ("ssm_conv_w", (4, 512), 1),
        ("w_mq", (256, 1024), 0), ("w_mk", (256, 1024), 0), ("w_mv", (256, 1024), 0), ("w_mo", (256, 1024), 0),
        ("w_up", (1024, 1408), 1), ("w_down", (704, 1024), 0), ("ffn_conv_w", (3, 1408), 1))
GROUPS = {"proj": (0,), "mixer": (1, 2, 3, 4), "mem": (5, 6, 7, 8), "ffn": (9, 10, 11)}
UP_SHARD_COLS = 1408
F32_ON_WIRE = ("ssm_conv_w", "ffn_conv_w")
SMALL = (("norm_mix", 1024), ("ssm_conv_b", 2048), ("dt_bias", 16), ("a_log", 16), ("d_skip", 16), ("ssm_norm", 1024),
         ("q_norm", 384), ("kv_norm", 256), ("attn_out_norm", 1024), ("norm_mem_q", 1024), ("norm_mem_kv", 1024),
         ("norm_ffn", 1024), ("ffn_conv_b", 5632))
PACK_COLS = 1024


def _pad_cols(t, n):
    return jnp.pad(t, ((0, 0),) * (t.ndim - 1) + ((0, n - t.shape[-1]),))


def _w_in_to_padded(t):
    z, xbc, dt, cq, ckv, kr = jnp.split(t, (1024, 3072, 3088, 3472, 3728), axis=-1)
    return jnp.concatenate([xbc, z, cq, _pad_cols(dt, LANES), ckv, _pad_cols(kr, P_IN - P_KR)], axis=-1)


def _w_in_from_padded(t):
    return jnp.concatenate([t[..., P_Z:P_Z + 1024], t[..., P_XBC:P_XBC + 2048], t[..., P_DT:P_DT + SSM_HEADS],
                            t[..., P_CQ:P_CQ + Q_LORA], t[..., P_CKV:P_CKV + KV_LORA], t[..., P_KR:P_KR + QK_ROPE]], axis=-1)


def _cols_joined(g):
    return jnp.concatenate([g[j] for j in range(N_CHIPS)], axis=-1)


def _cols_by_chip(t, dtype):
    k = t.shape[0]
    return t.reshape(k, N_CHIPS, -1).transpose(1, 0, 2).astype(dtype)


def _rows_by_chip(t):
    return t.reshape(N_CHIPS, -1, t.shape[-1])


def _mixer_weights(gw):
    wl = {}
    uq = _cols_joined(gw["w_uq"]).reshape(Q_LORA, MLA_HEADS, QK_NOPE + QK_ROPE)
    wl["w_uq"] = _pad_cols(uq, HEAD_PAD).reshape(Q_LORA, MLA_HEADS * HEAD_PAD)
    wl["w_ukv"] = _cols_joined(gw["w_ukv"])
    wl["ssm_conv_w"] = _cols_joined(gw["ssm_conv_w"])
    return wl


def _layer_fwd(x0, mem, cos, sins, weights, sp, li):
    n = lambda t: f"l{li}_{t}"
    lead = ()
    sv = {"x0": x0}
    gw = dict(weights("proj", x0))
    w_in = _w_in_to_padded(_cols_joined(gw["w_in"]))
    h = _rms_fwd(x0, sp["norm_mix"], name=n("mix_norm"))
    in_hbm = lambda t: pltpu.with_memory_space_constraint(t, pltpu.HBM)
    proj = in_hbm(_mm(h, w_in, name=n("mix_proj")))
    gw.update(weights("mixer", proj))
    wl = dict(_mixer_weights(gw), w_in=w_in)
    xbc = in_hbm(_ssm_conv_fwd(proj, wl["ssm_conv_w"], sp["ssm_conv_b"], name=n("ssm_conv")))
    y, pstates = _ssd_fwd(xbc, proj, sp["dt_bias"], sp["a_log"], sp["d_skip"], name=n("ssd"))
    y_ssm = _gated_rms_fwd(y, proj, sp["ssm_norm"], name=n("ssm_gate"))
    cqn = _rms_fwd(proj, sp["q_norm"], col=(Q_LORA, P_CQ // Q_LORA), name=n("q_norm"))
    ckvn = _rms_fwd(proj, sp["kv_norm"], col=(KV_LORA, P_CKV // KV_LORA), name=n("kv_norm"))
    q = in_hbm(_mm(cqn, wl["w_uq"], name=n("uq")))
    kv = in_hbm(_mm(ckvn, wl["w_ukv"], name=n("ukv")))
    qr, kr, v = _mla_prep(q, kv, proj, cos, sins, name=n("rope"))
    att, lse = _flash_fwd(qr, kr, v, name=n("flash"))
    y_att = _rms_fwd(att, sp["attn_out_norm"], name=n("att_norm"))
    x1 = _mm(y_ssm, gw["w_out"], b_lead=lead, b_rows=(0, D_SSM), res=x0, name=n("out_a"))
    x1 = _mm(y_att, gw["w_out"], b_lead=lead, b_rows=(D_SSM, D_SSM), res=x1, name=n("out_b"))
    sv.update(h=h, proj=proj, xbc=xbc, y=y, pstates=pstates, y_ssm=y_ssm, cqn=cqn, ckvn=ckvn, qr=qr, kr=kr, v=v,
              att=att, lse=lse, y_att=y_att, x1=x1)
    gw.update(weights("mem", x1))
    hq = _rms_fwd(x1, sp["norm_mem_q"], name=n("memq_norm"))
    hm = _rms_fwd(mem, sp["norm_mem_kv"], name=n("memkv_norm"))
    mq = _mm(hq, gw["w_mq"], b_lead=lead, out_dtype=BF16, name=n("mq"))
    mk = _mm(hm, gw["w_mk"], b_lead=lead, out_dtype=BF16, name=n("mk"))
    mv = _mm(hm, gw["w_mv"], b_lead=lead, out_dtype=BF16, name=n("mv"))
    mo = _mem_attn_fwd(mq, mk, mv, name=n("mem_attn"))
    x2 = _mm(mo, gw["w_mo"], b_lead=lead, res=x1, name=n("mo"))
    sv.update(hq=hq, hm=hm, mq=mq, mk=mk, mv=mv, mo=mo, x2=x2)
    gw.update(weights("ffn", x2))
    wl["ffn_conv_w"] = _cols_joined(gw["ffn_conv_w"])
    hf = _rms_fwd(x2, sp["norm_ffn"], name=n("ffn_norm"))
    up_g = _mm(hf, gw["w_up"], b_lead=lead, b_chips=(0, 2), name=n("up_g"))
    up_v = _mm(hf, gw["w_up"], b_lead=lead, b_chips=(2, 2), name=n("up_v"))
    act = _ffn_conv_fwd(up_g, up_v, wl["ffn_conv_w"], sp["ffn_conv_b"], name=n("ffn_conv"))
    x3 = _mm(act, gw["w_down"], b_lead=lead, res=x2, name=n("down"))
    sv.update(hf=hf, up_g=up_g, up_v=up_v, act=act)
    return x3, sv, gw, wl


def _layer_bwd(dx3, dx3b, mem, cos, sins, gw, wl, sp, sv, li, emit):
    n = lambda t: f"l{li}_b_{t}"
    lead = ()
    g = {}

    def after(token, v):
        return v if token is None else v + token[0, 0]

    dact = _mm(dx3b, gw["w_down"], tb=True, b_lead=lead, out_dtype=BF16, name=n("down_dx"))
    g["w_down"] = _rows_by_chip(_mm(sv["act"], dx3b, ta=True, out_dtype=BF16, name=n("down_dw")))
    dup_g, dup_v, dcw, g["ffn_conv_b"] = _ffn_conv_bwd(
        sv["up_g"], sv["up_v"], wl["ffn_conv_w"], sp["ffn_conv_b"], dact, name=n("ffn_conv"))
    g["ffn_conv_w"] = _cols_by_chip(dcw, F32)
    nsh = UP_SHARD_COLS
    dhf = None
    for c4 in range(N_CHIPS):
        dhf = _mm(dup_g if c4 < 2 else dup_v, gw["w_up"], tb=True, a_col=(nsh, c4 % 2), b_lead=(c4,), res=dhf,
                  name=n(f"up{c4}_dx"))
    g["w_up"] = jnp.concatenate([_mm(sv["hf"], dup_g, ta=True, o_chips=nsh, out_dtype=BF16, name=n("upg_dw")),
                                 _mm(sv["hf"], dup_v, ta=True, o_chips=nsh, out_dtype=BF16, name=n("upv_dw"))], axis=0)
    dx2, dx2b, g["norm_ffn"] = _rms_bwd(sv["x2"], after(emit("ffn", g), sp["norm_ffn"]), dhf, dx3, name=n("ffn_norm"))
    dmo = _mm(dx2b, gw["w_mo"], tb=True, b_lead=lead, out_dtype=BF16, name=n("mo_dx"))
    g["w_mo"] = _rows_by_chip(_mm(sv["mo"], dx2b, ta=True, out_dtype=BF16, name=n("mo_dw")))
    dmq, dmk, dmv = _mem_attn_bwd(sv["mq"], sv["mk"], sv["mv"], dmo, name=n("mem_attn"))
    dhq = _mm(dmq, gw["w_mq"], tb=True, b_lead=lead, name=n("mq_dx"))
    g["w_mq"] = _rows_by_chip(_mm(sv["hq"], dmq, ta=True, out_dtype=BF16, name=n("mq_dw")))
    dhm = _mm(dmk, gw["w_mk"], tb=True, b_lead=lead, name=n("mk_dx"))
    dhm = _mm(dmv, gw["w_mv"], tb=True, b_lead=lead, res=dhm, name=n("mv_dx"))
    g["w_mk"] = _rows_by_chip(_mm(sv["hm"], dmk, ta=True, out_dtype=BF16, name=n("mk_dw")))
    g["w_mv"] = _rows_by_chip(_mm(sv["hm"], dmv, ta=True, out_dtype=BF16, name=n("mv_dw")))
    dx1, dx1b, g["norm_mem_q"] = _rms_bwd(sv["x1"], after(emit("mem", g), sp["norm_mem_q"]), dhq, dx2, name=n("memq_norm"))
    _, _, g["norm_mem_kv"] = _rms_bwd(mem, sp["norm_mem_kv"], dhm, name=n("memkv_norm"))
    dy_ssm = _mm(dx1b, gw["w_out"], tb=True, b_lead=lead, b_rows=(0, D_SSM), name=n("outa_dx"))
    dy_att = _mm(dx1b, gw["w_out"], tb=True, b_lead=lead, b_rows=(D_SSM, D_SSM), name=n("outb_dx"))
    g["w_out"] = _rows_by_chip(jnp.concatenate([_mm(sv["y_ssm"], dx1b, ta=True, out_dtype=BF16, name=n("outa_dw")),
                                                _mm(sv["y_att"], dx1b, ta=True, out_dtype=BF16, name=n("outb_dw"))], axis=0))
    datt, _, g["attn_out_norm"] = _rms_bwd(sv["att"], sp["attn_out_norm"], dy_att, name=n("att_norm"))
    dqr, dkr, dv = _flash_bwd(sv["qr"], sv["kr"], sv["v"], sv["att"], sv["lse"], datt, name=n("flash"))
    dq, dkv, dkrope = _mla_prep_bwd(dqr, dkr, dv, cos, sins, name=n("rope"))
    duq = _mm(sv["cqn"], dq, ta=True, name=n("uq_dw")).reshape(Q_LORA, MLA_HEADS, HEAD_PAD)[..., :QK_NOPE + QK_ROPE]
    g["w_uq"] = _cols_by_chip(duq.reshape(Q_LORA, -1), BF16)
    dcqn = _mm(dq, wl["w_uq"], tb=True, name=n("uq_dx"))
    g["w_ukv"] = _cols_by_chip(_mm(sv["ckvn"], dkv, ta=True, name=n("ukv_dw")), BF16)
    dckvn = _mm(dkv, wl["w_ukv"], tb=True, name=n("ukv_dx"))
    proj = sv["proj"]
    _, dcq, g["q_norm"] = _rms_bwd(proj, sp["q_norm"], dcqn, col=(Q_LORA, P_CQ // Q_LORA), name=n("q_norm"))
    _, dckv, g["kv_norm"] = _rms_bwd(proj, sp["kv_norm"], dckvn, col=(KV_LORA, P_CKV // KV_LORA), name=n("kv_norm"))
    dy, dz, g["ssm_norm"] = _gated_rms_bwd(sv["y"], proj, sp["ssm_norm"], dy_ssm, name=n("ssm_gate"))
    dxbc, ddt, da_log, dd_skip, ddt_bias = _ssd_bwd(
        sv["xbc"], proj, sp["dt_bias"], sp["a_log"], sp["d_skip"], sv["pstates"], dy, name=n("ssd"))
    g["a_log"], g["d_skip"], g["dt_bias"] = da_log[0, :SSM_HEADS], dd_skip[0, :SSM_HEADS], ddt_bias[0, :SSM_HEADS]
    dxbc_pre, dsw, g["ssm_conv_b"] = _ssm_conv_bwd(proj, wl["ssm_conv_w"], sp["ssm_conv_b"], dxbc, name=n("ssm_conv"))
    g["ssm_conv_w"] = _cols_by_chip(dsw, F32)
    started = emit("mixer", g)
    s = proj.shape[0]
    dproj = jnp.concatenate([dxbc_pre, dz, dcq, ddt, dckv, dkrope,
                             jnp.zeros((s, P_IN - P_KR - LANES), BF16)], axis=1)
    dh = _mm(dproj, wl["w_in"], tb=True, name=n("proj_dx"))
    g["w_in"] = _cols_by_chip(_w_in_from_padded(_mm(sv["h"], dproj, ta=True, name=n("proj_dw"))), BF16)
    dx0, dx0b, g["norm_mix"] = _rms_bwd(sv["x0"], after(started, sp["norm_mix"]), dh, dx1, name=n("mix_norm"))
    return dx0, dx0b, g, emit("proj", g)


def _chip_peers(x, y):
    return [(1 - x, y), (x, 1 - y), (1 - x, 1 - y)]


HBM_SPEC = pl.BlockSpec(memory_space=pltpu.HBM)
SEM_SPEC = pl.BlockSpec(memory_space=pltpu.SEMAPHORE)
ANY_SPEC = pl.BlockSpec(memory_space=pl.ANY)
VMEM_SPEC = pl.BlockSpec(memory_space=pltpu.VMEM)
DATAFLOW = pltpu.SideEffectType.DATAFLOW_SIDE_EFFECTING
TOKEN_SHAPE = (8, LANES)


def _exchange_start(srcs, land_shapes, src_view, dst_view, token, *, name):
    n = len(srcs)

    def body(*refs):
        s, l, tok_in = refs[:n], refs[n:2 * n], refs[2 * n]
        send_sems, recv_sems = refs[2 * n + 1], refs[2 * n + 2]
        tok_out = refs[-1]
        x, y, c = lax.axis_index("x"), lax.axis_index("y"), lax.axis_index("c")
        me = 2 * x + y
        for t in range(n):
            for k, (px, py) in enumerate(_chip_peers(x, y)):
                pltpu.make_async_remote_copy(
                    src_ref=src_view(t, s[t], 2 * px + py), dst_ref=dst_view(t, l[t], me), send_sem=send_sems.at[3 * t + k],
                    recv_sem=recv_sems.at[3 * t + k], device_id=(px, py, c), device_id_type=MESH).start()
            pltpu.make_async_copy(src_view(t, s[t], me), dst_view(t, l[t], me), send_sems.at[3 * n + t]).start()
        tok_out[...] = tok_in[...]

    hbm = lambda t: pltpu.with_memory_space_constraint(t, pltpu.HBM)
    lands = [lax.empty(l.shape, l.dtype) for l in land_shapes]
    outs = pl.pallas_call(
        body, name=name,
        out_shape=(pltpu.SemaphoreType.DMA((4 * n,)), pltpu.SemaphoreType.DMA((3 * n,)),
                   *[pltpu.HBM(l.shape, l.dtype) for l in land_shapes], SDS(TOKEN_SHAPE, F32)),
        in_specs=[HBM_SPEC] * (2 * n) + [VMEM_SPEC], out_specs=(SEM_SPEC, SEM_SPEC, *[HBM_SPEC] * n, VMEM_SPEC),
        input_output_aliases={n + t: 2 + t for t in range(n)},
        compiler_params=pltpu.CompilerParams(has_side_effects=DATAFLOW))(*[hbm(t) for t in srcs], *[hbm(t) for t in lands], token)
    return outs[0], outs[1], list(outs[2:2 + n]), outs[-1]


def _exchange_wait(srcs, lands, send_sems, recv_sems, after, src_view, dst_view, which, *, name):
    n = len(srcs)
    m = len(which)

    def body(*refs):
        s, l = refs[:m], refs[m:2 * m]
        send_ref, recv_ref = refs[2 * m], refs[2 * m + 1]
        x, y, c = lax.axis_index("x"), lax.axis_index("y"), lax.axis_index("c")
        me = 2 * x + y
        for i, t in enumerate(which):
            for k, (px, py) in enumerate(_chip_peers(x, y)):
                chip = 2 * px + py
                cp = pltpu.make_async_remote_copy(
                    src_ref=src_view(t, s[i], chip), dst_ref=dst_view(t, l[i], chip), send_sem=send_ref.at[3 * t + k],
                    recv_sem=recv_ref.at[3 * t + k], device_id=(px, py, c), device_id_type=MESH)
                cp.wait_send()
                cp.wait_recv()
            pltpu.make_async_copy(src_view(t, s[i], me), dst_view(t, l[i], me), send_ref.at[3 * n + t]).wait()

    outs = pl.pallas_call(
        body, name=name, out_shape=[pltpu.HBM(lands[t].shape, lands[t].dtype) for t in which],
        in_specs=[HBM_SPEC] * (2 * m) + [SEM_SPEC, SEM_SPEC, ANY_SPEC], out_specs=[HBM_SPEC] * m,
        input_output_aliases={m + i: i for i in range(m)},
        compiler_params=pltpu.CompilerParams(has_side_effects=DATAFLOW))(
            *[srcs[t] for t in which], *[lands[t] for t in which], send_sems, recv_sems, after)
    return list(outs)


def _gather_layer_start(shards, li, token, tag=""):
    src_view = lambda t, ref, chip: ref.at[li]
    dst_view = lambda t, ref, chip: ref.at[chip]
    send_sems, recv_sems, lands, token = _exchange_start(
        shards, [SDS((N_CHIPS,) + s.shape[1:], s.dtype) for s in shards], src_view, dst_view, token,
        name=f"gather{li}{tag}_start")
    return (shards, lands, send_sems, recv_sems, src_view, dst_view, f"gather{li}{tag}"), token


def _scatter_start(grads, tag, token):
    view = lambda t, ref, chip: ref.at[chip]
    send_sems, recv_sems, lands, token = _exchange_start(
        grads, [SDS(g.shape, g.dtype) for g in grads], view, view, token, name=f"scatter{tag}_start")
    return (grads, lands, send_sems, recv_sems, view, view, f"scatter{tag}"), token


def _exchange_finish(handle, after, which=None, tag=""):
    srcs, lands, send_sems, recv_sems, src_view, dst_view, name = handle
    which = tuple(range(len(srcs))) if which is None else which
    return _exchange_wait(srcs, lands, send_sems, recv_sems, after, src_view, dst_view, which, name=f"{name}{tag}_wait")


def _swap_start(bufs, token, *, name):
    n = len(bufs)

    def body(*refs):
        s, l, tok_in = refs[:n], refs[n:2 * n], refs[2 * n]
        send_sems, recv_sems = refs[2 * n + 1], refs[2 * n + 2]
        x, y, c = lax.axis_index("x"), lax.axis_index("y"), lax.axis_index("c")
        for t in range(n):
            pltpu.make_async_remote_copy(src_ref=s[t], dst_ref=l[t], send_sem=send_sems.at[t], recv_sem=recv_sems.at[t],
                                         device_id=(x, y, 1 - c), device_id_type=MESH).start()
        refs[-1][...] = tok_in[...]

    hbm = lambda t: pltpu.with_memory_space_constraint(t, pltpu.HBM)
    lands = [lax.empty(b.shape, b.dtype) for b in bufs]
    outs = pl.pallas_call(
        body, name=f"{name}_start",
        out_shape=(pltpu.SemaphoreType.DMA((n,)), pltpu.SemaphoreType.DMA((n,)),
                   *[pltpu.HBM(b.shape, b.dtype) for b in bufs], SDS(TOKEN_SHAPE, F32)),
        in_specs=[HBM_SPEC] * (2 * n) + [VMEM_SPEC], out_specs=(SEM_SPEC, SEM_SPEC, *[HBM_SPEC] * n, VMEM_SPEC),
        input_output_aliases={n + t: 2 + t for t in range(n)},
        compiler_params=pltpu.CompilerParams(has_side_effects=DATAFLOW))(*[hbm(t) for t in bufs], *[hbm(t) for t in lands], token)
    return (bufs, list(outs[2:2 + n]), outs[0], outs[1], name), outs[-1]


def _swap_wait(handle, after):
    bufs, lands, send_sems, recv_sems, name = handle
    n = len(bufs)

    def body(*refs):
        s, l = refs[:n], refs[n:2 * n]
        send_ref, recv_ref = refs[2 * n], refs[2 * n + 1]
        x, y, c = lax.axis_index("x"), lax.axis_index("y"), lax.axis_index("c")
        for t in range(n):
            cp = pltpu.make_async_remote_copy(src_ref=s[t], dst_ref=l[t], send_sem=send_ref.at[t], recv_sem=recv_ref.at[t],
                                              device_id=(x, y, 1 - c), device_id_type=MESH)
            cp.wait_send()
            cp.wait_recv()

    outs = pl.pallas_call(
        body, name=f"{name}_wait", out_shape=[pltpu.HBM(b.shape, b.dtype) for b in bufs],
        in_specs=[HBM_SPEC] * (2 * n) + [SEM_SPEC, SEM_SPEC, ANY_SPEC], out_specs=[HBM_SPEC] * n,
        input_output_aliases={n + t: t for t in range(n)},
        compiler_params=pltpu.CompilerParams(has_side_effects=DATAFLOW))(*bufs, *lands, send_sems, recv_sems, after)
    return list(outs)


def _all_gather8(src, *, name):
    def body(src_ref, out_ref, send_sems, recv_sems, local_sem):
        x, y, c = lax.axis_index("x"), lax.axis_index("y"), lax.axis_index("c")
        me = 4 * x + 2 * y + c
        mine = pltpu.make_async_copy(src_ref, out_ref.at[me], local_sem)
        mine.start()

        def peer(k):
            return (x ^ (k >> 2 & 1), y ^ (k >> 1 & 1), c ^ (k & 1))

        sends = []
        for k in range(1, N_DEV):
            cp = pltpu.make_async_remote_copy(src_ref=src_ref, dst_ref=out_ref.at[me], send_sem=send_sems.at[k - 1],
                                              recv_sem=recv_sems.at[k - 1], device_id=peer(k), device_id_type=MESH)
            cp.start()
            sends.append(cp)
        for k in range(1, N_DEV):
            px, py, pc = peer(k)
            pltpu.make_async_remote_copy(src_ref=src_ref, dst_ref=out_ref.at[4 * px + 2 * py + pc],
                                         send_sem=send_sems.at[k - 1], recv_sem=recv_sems.at[k - 1],
                                         device_id=peer(k), device_id_type=MESH).wait_recv()
        for cp in sends:
            cp.wait_send()
        mine.wait()

    any_spec = pl.BlockSpec(memory_space=pl.ANY)
    return pl.pallas_call(
        body, in_specs=[any_spec], out_specs=any_spec, out_shape=SDS((N_DEV,) + src.shape, src.dtype),
        scratch_shapes=[pltpu.SemaphoreType.DMA((N_DEV - 1,)), pltpu.SemaphoreType.DMA((N_DEV - 1,)), pltpu.SemaphoreType.DMA],
        name=name)(src)


def _adam_terms(w, g, m, v):
    m = ADAM_B1 * m + (1.0 - ADAM_B1) * g
    v = ADAM_B2 * v + (1.0 - ADAM_B2) * (g * g)
    m_hat = m / (1.0 - ADAM_B1 ** ADAM_STEP)
    v_hat = v / (1.0 - ADAM_B2 ** ADAM_STEP)
    delta = -ADAM_LR * (m_hat / (jnp.sqrt(v_hat) + ADAM_EPS) + ADAM_WD * w)
    return delta, m, v


def _adamw_shard(mine, other, w, m, v, *, name):
    d, a, b = w.shape
    tr = next((t for t in (128, 64, 32, 16) if a % t == 0), a)

    def body(*refs):
        ga, gb = refs[:d], refs[d:2 * d]
        w_ref, m_ref, v_ref, g_ref, d_ref, nm_ref, nv_ref = refs[2 * d:]

        def plane(ref):
            return ((ref[0].astype(F32) + ref[1].astype(F32)) + ref[2].astype(F32)) + ref[3].astype(F32)

        for lp in range(d):
            @pl.when(pl.program_id(0) == lp)
            def _(lp=lp):
                g = plane(ga[lp]) + plane(gb[lp])
                delta, mn, vn = _adam_terms(w_ref[...], g, m_ref[...], v_ref[...])
                g_ref[...] = g
                d_ref[...] = delta
                nm_ref[...] = mn
                nv_ref[...] = vn

    gspecs = [pl.BlockSpec((N_CHIPS, tr, b), lambda l, i, lp=lp: (0, jnp.where(l == lp, i, 0), 0)) for lp in range(d)]
    blk = pl.BlockSpec((None, tr, b), lambda l, i: (l, i, 0))
    shp = SDS((d, a, b), F32)
    return pl.pallas_call(
        body, grid=(d, a // tr), in_specs=gspecs + gspecs + [blk, blk, blk], out_specs=(blk,) * 4, out_shape=(shp,) * 4,
        name=name, compiler_params=_params(("arbitrary", "arbitrary"), 48 << 20))(*mine, *other, w, m, v)


def _adamw_small(g8, w, m, v, *, name):
    n = w.shape[1]

    def body(g8_ref, w_ref, m_ref, v_ref, g_ref, d_ref, nm_ref, nv_ref):
        g = g8_ref[0]
        for k in range(1, N_DEV):
            g = g + g8_ref[k]
        delta, mn, vn = _adam_terms(w_ref[...], g, m_ref[...], v_ref[...])
        g_ref[...] = g
        d_ref[...] = delta
        nm_ref[...] = mn
        nv_ref[...] = vn

    shp = SDS((1, n), F32)
    return pl.pallas_call(body, out_shape=(shp,) * 4, name=name, compiler_params=_params(None, 24 << 20))(g8, w, m, v)


def _rope_tables(positions):
    inv_freq = 1.0 / (ROPE_THETA ** (jnp.arange(0, QK_ROPE, 2, dtype=F32) / QK_ROPE))
    ang = positions.astype(F32)[:, None] * inv_freq
    c, s = jnp.cos(ang), jnp.sin(ang)
    n = positions.shape[0]
    pad = jnp.zeros((n, HEAD_PAD - QK_NOPE - QK_ROPE), F32)
    cos = jnp.concatenate([jnp.ones((n, QK_NOPE), F32), c, c, pad], axis=1)
    sins = jnp.concatenate([jnp.zeros((n, QK_NOPE), F32), -s, s, pad], axis=1)
    return cos, sins


def _pad_lanes(v):
    return _pad_cols(v.reshape(1, -1), LANES)


def _local_step(x, mem, positions, weights, small, final_norm, loss_target, emit, token):
    cos, sins = _rope_tables(positions)
    saved, gws, wls, sps = [], [], [], []
    h = x
    for li in range(DEPTH):
        sp = {k: small[k][li] for k, _ in SMALL}
        if li == 0:
            sp["norm_mix"] = sp["norm_mix"] + token[0, 0]
        for k in ("dt_bias", "a_log", "d_skip"):
            sp[k] = _pad_lanes(sp[k])
        h, sv, gw, wl = _layer_fwd(h, mem, cos, sins, functools.partial(weights, li), sp, li)
        saved.append(sv)
        gws.append(gw)
        wls.append(wl)
        sps.append(sp)
    loss, dh, dhb, g_final = _final_loss(h, final_norm, loss_target, name="final_loss")
    grads = [None] * DEPTH
    started = None
    for li in reversed(range(DEPTH)):
        sp = sps[li]
        if started is not None:
            sp = dict(sp, ffn_conv_b=sp["ffn_conv_b"] + started[0, 0])
        dh, dhb, grads[li], started = _layer_bwd(dh, dhb, mem, cos, sins, gws[li], wls[li], sp, saved[li], li,
                                                 functools.partial(emit, li))
    return loss, dh, grads, g_final


def _gathered_views(which, lands):
    return {MATS[t][0]: (b.reshape(-1, b.shape[-1]) if MATS[t][2] == 0 else b) for t, b in zip(which, lands)}


def kernel(x, mem, positions, norm_mix, w_in, ssm_conv_w, ssm_conv_b, dt_bias, a_log, d_skip, ssm_norm, q_norm, w_uq, kv_norm, w_ukv, attn_out_norm, w_out, norm_mem_q, norm_mem_kv, w_mq, w_mk, w_mv, w_mo, norm_ffn, w_up, ffn_conv_w, ffn_conv_b, w_down, final_norm, loss_target, m_norm_mix, m_w_in, m_ssm_conv_w, m_ssm_conv_b, m_dt_bias, m_a_log, m_d_skip, m_ssm_norm, m_q_norm, m_w_uq, m_kv_norm, m_w_ukv, m_attn_out_norm, m_w_out, m_norm_mem_q, m_norm_mem_kv, m_w_mq, m_w_mk, m_w_mv, m_w_mo, m_norm_ffn, m_w_up, m_ffn_conv_w, m_ffn_conv_b, m_w_down, m_final_norm, v_norm_mix, v_w_in, v_ssm_conv_w, v_ssm_conv_b, v_dt_bias, v_a_log, v_d_skip, v_ssm_norm, v_q_norm, v_w_uq, v_kv_norm, v_w_ukv, v_attn_out_norm, v_w_out, v_norm_mem_q, v_norm_mem_kv, v_w_mq, v_w_mk, v_w_mv, v_w_mo, v_norm_ffn, v_w_up, v_ffn_conv_w, v_ffn_conv_b, v_w_down, v_final_norm):
    args = dict(locals())
    names = ["norm_mix", "w_in", "ssm_conv_w", "ssm_conv_b", "dt_bias", "a_log", "d_skip", "ssm_norm", "q_norm", "w_uq",
             "kv_norm", "w_ukv", "attn_out_norm", "w_out", "norm_mem_q", "norm_mem_kv", "w_mq", "w_mk", "w_mv", "w_mo",
             "norm_ffn", "w_up", "ffn_conv_w", "ffn_conv_b", "w_down", "final_norm"]
    wts = {k: args[k] for k in names}
    mom = {k: args["m_" + k] for k in names}
    var = {k: args["v_" + k] for k in names}
    mat_names = [k for k, _, _ in MATS]

    shards = [wts[k] if k in F32_ON_WIRE else wts[k].astype(BF16) for k in mat_names]
    token = jnp.zeros(TOKEN_SHAPE, F32)
    first, token = _gather_layer_start(shards[:1], 0, token, tag="_first")
    gathers = []
    for li in range(DEPTH):
        handle, token = _gather_layer_start(shards[1:] if li == 0 else shards, li, token)
        gathers.append(handle)
    small = {k: wts[k] for k, _ in SMALL}

    def weights(li, group, after):
        which = GROUPS[group]
        if li > 0:
            return _gathered_views(which, _exchange_finish(gathers[li], after, which, tag=f"_{group}"))
        if group == "proj":
            return _gathered_views(which, _exchange_finish(first, after))
        return _gathered_views(which, _exchange_finish(gathers[0], after, tuple(t - 1 for t in which), tag=f"_{group}"))

    scatters = [[] for _ in range(DEPTH)]
    nm = len(mat_names)
    mine = [[None] * nm for _ in range(DEPTH)]
    swaps = [None] * DEPTH
    last_started = [None]

    def swap_layer(li, after):
        for which, handle in scatters[li]:
            for t, b in zip(which, _exchange_finish(handle, after)):
                mine[li][t] = b
        swaps[li], started = _swap_start(mine[li], jnp.zeros(TOKEN_SHAPE, F32), name=f"swap{li}")
        return started

    def emit(li, group, g):
        last = group == "proj"
        if li == 0:
            which = GROUPS[group]
        elif last:
            which = tuple(range(nm))
        else:
            return None
        handle, started = _scatter_start([g[MATS[t][0]] for t in which], f"{li}_{group}", jnp.zeros(TOKEN_SHAPE, F32))
        scatters[li].append((which, handle))
        last_started[0] = started
        if li + 1 < DEPTH and group == ("mixer" if li == 0 else "proj"):
            started = started + swap_layer(li + 1, g["ssm_conv_w"])
        return started

    loss, grad_x, grads, g_final = _local_step(x[0], mem[0], positions[0], weights, small, wts["final_norm"],
                                               loss_target[0], emit, token)
    loss = lax.psum(loss, ("x", "y", "c"))

    w_in_t = GROUPS["proj"][0]
    others = [t for t in range(nm) if t != w_in_t]
    for which, handle in scatters[0]:
        if which != GROUPS["proj"]:
            for t, b in zip(which, _exchange_finish(handle, grad_x)):
                mine[0][t] = b
    swaps[0], _ = _swap_start([mine[0][t] for t in others], last_started[0], name="swap0")
    other = [dict(zip(others, _swap_wait(swaps[0], grad_x)))] + [
        dict(enumerate(_swap_wait(swaps[li], grad_x))) for li in range(1, DEPTH)]

    def adamw(t):
        k = mat_names[t]
        return _adamw_shard([mine[li][t] for li in range(DEPTH)], [other[li][t] for li in range(DEPTH)],
                            wts[k], mom[k], var[k], name=f"adamw_{k}")

    mat_out = {mat_names[t]: adamw(t) for t in others}
    done = sum(mat_out[mat_names[t]][0][0, 0, :1] for t in others)
    (last_handle,) = [handle for which, handle in scatters[0] if which == GROUPS["proj"]]
    (mine[0][w_in_t],) = _exchange_finish(last_handle, done)
    last_swap, _ = _swap_start([mine[0][w_in_t]], jnp.zeros(TOKEN_SHAPE, F32), name="swap0_last")
    (other[0][w_in_t],) = _swap_wait(last_swap, done)
    mat_out[mat_names[w_in_t]] = adamw(w_in_t)

    def pack_small(get, fin):
        flat = [get(k).reshape(-1) for k, _ in SMALL] + [fin.reshape(-1)]
        n = sum(f.shape[0] for f in flat)
        return jnp.concatenate(flat + [jnp.zeros((-n % PACK_COLS,), F32)]).reshape(1, -1)

    gs = pack_small(lambda k: jnp.stack([grads[li][k] for li in range(DEPTH)]), g_final)
    g8 = _all_gather8(gs, name="gather_small_grads")
    small_out = _adamw_small(g8, pack_small(wts.get, wts["final_norm"]), pack_small(mom.get, mom["final_norm"]),
                             pack_small(var.get, var["final_norm"]), name="adamw_small")

    def unpack_small(buf):
        out, off = {}, 0
        for k, nel in SMALL:
            out[k] = buf[0, off:off + DEPTH * nel].reshape(DEPTH, nel)
            off += DEPTH * nel
        out["final_norm"] = buf[0, off:off + D_MODEL]
        return out

    small_res = [unpack_small(b) for b in small_out]
    res = []
    for kind in range(4):
        for k in names:
            res.append(small_res[kind][k] if k in small_res[kind] else mat_out[k][kind])
    return (loss, grad_x[None], *res)
```
